```python
import math
import jax, jax.numpy as jnp
from jax import lax
import numpy as np

D_MODEL = 2048
BATCH = 8
SEQ = 4096
DEPTH = 2

PLE_DIM = 256
SG_HEADS = 4
SG_HEAD_DIM = 128
SG_WIDTH = SG_HEADS * SG_HEAD_DIM
SG_CHUNK = 128
SC_GROUPS = 4
SC_GROUP_DIM = 128
SC_WIDTH = SC_GROUPS * SC_GROUP_DIM
SC_KERNEL = 3
GDN_HEADS = 8
GDN_HEAD_DIM = 128
GDN_WIDTH = GDN_HEADS * GDN_HEAD_DIM
GDN_CONV = 4
GDN_CHUNK = 64

MIX_WIDTH = SG_WIDTH + SC_WIDTH + GDN_WIDTH
D_FF = 4 * D_MODEL
EPS = 1e-6

IN_SIZES = (2 * SG_WIDTH, 3 * SC_WIDTH, 3 * GDN_WIDTH, GDN_WIDTH, GDN_HEADS, GDN_HEADS)
IN_COLS = int(sum(IN_SIZES))
IN_SPLITS = [int(s) for s in np.cumsum(IN_SIZES)[:-1]]

kernel_name = "hybrid_sgu_shortconv_gdn_block"


def rmsnorm(x, g):
    x32 = x.astype(jnp.float32)
    y = x32 * lax.rsqrt(jnp.mean(x32 * x32, axis=-1, keepdims=True) + EPS)
    return (y * g).astype(x.dtype)


def group_rmsnorm(x, g, n_groups):
    shp = x.shape
    xg = x.reshape(shp[:-1] + (n_groups, shp[-1] // n_groups))
    y = rmsnorm(xg, g.reshape(n_groups, shp[-1] // n_groups))
    return y.reshape(shp)


def layernorm(x, g, b):
    x32 = x.astype(jnp.float32)
    mu = jnp.mean(x32, axis=-1, keepdims=True)
    var = jnp.mean(jnp.square(x32 - mu), axis=-1, keepdims=True)
    return ((x32 - mu) * lax.rsqrt(var + EPS) * g + b).astype(x.dtype)


def causal_dwconv(x, w):
    K = w.shape[0]
    S = x.shape[1]
    xp = jnp.pad(x, ((0, 0), (K - 1, 0), (0, 0)))
    y = xp[:, 0:S] * w[0]
    for j in range(1, K):
        y = y + xp[:, j:j + S] * w[j]
    return y


def spatial_gating(u, v, ln_g, ln_b, w_s, b_s):
    Bsz, S, _ = v.shape
    nc = S // SG_CHUNK
    vh = v.reshape(Bsz, nc, SG_CHUNK, SG_HEADS, SG_HEAD_DIM)
    vh = layernorm(vh, ln_g.reshape(SG_HEADS, SG_HEAD_DIM), ln_b.reshape(SG_HEADS, SG_HEAD_DIM))
    w_causal = jnp.tril(w_s)
    f = jnp.einsum('hts,bnshd->bnthd', w_causal, vh) + jnp.transpose(b_s)[None, None, :, :, None]
    return u * f.reshape(Bsz, S, SG_WIDTH)


def gated_delta_chunked(q, k, v, g, beta):
    Bsz, S, H, Dk = q.shape
    Dv = v.shape[-1]
    L = GDN_CHUNK
    nc = S // L

    def chunk4(t):
        return t.reshape(Bsz, nc, L, H, t.shape[-1]).transpose(0, 3, 1, 2, 4)

    def chunk3(t):
        return t.reshape(Bsz, nc, L, H).transpose(0, 3, 1, 2)

    q = chunk4(q.astype(jnp.float32)) * (Dk ** -0.5)
    k = chunk4(k.astype(jnp.float32))
    v = chunk4(v.astype(jnp.float32))
    gc = jnp.cumsum(chunk3(g.astype(jnp.float32)), axis=-1)
    beta = chunk3(beta.astype(jnp.float32))

    incl = jnp.tril(jnp.ones((L, L), dtype=bool))
    strict = jnp.tril(jnp.ones((L, L), dtype=bool), -1)
    diff = gc[..., :, None] - gc[..., None, :]
    decay_incl = jnp.where(incl, jnp.exp(jnp.where(incl, diff, 0.0)), 0.0)
    decay_strict = jnp.where(strict, decay_incl, 0.0)

    kb = k * beta[..., None]
    vb = v * beta[..., None]
    A = jnp.einsum('bhnid,bhnjd->bhnij', kb, k) * decay_strict
    eye = jnp.eye(L, dtype=jnp.float32)
    rhs = jnp.concatenate([vb, kb * jnp.exp(gc)[..., None]], axis=-1)
    sol = lax.linalg.triangular_solve(A + eye, rhs, left_side=True, lower=True,
                                      transpose_a=False, conjugate_a=False, unit_diagonal=True)
    value, kcd = sol[..., :Dv], sol[..., Dv:]

    intra = jnp.einsum('bhnid,bhnjd->bhnij', q, k) * decay_incl
    q_exp = q * jnp.exp(gc)[..., None]
    k_tail = k * jnp.exp(gc[..., -1:] - gc)[..., None]
    g_last = jnp.exp(gc[..., -1])

    def step(state, inp):
        qe, kc, val, intra_c, kt, gl = inp
        v_new = val - jnp.einsum('bhld,bhdv->bhlv', kc, state)
        o = jnp.einsum('bhld,bhdv->bhlv', qe, state) + jnp.einsum('bhij,bhjv->bhiv', intra_c, v_new)
        state = state * gl[..., None, None] + jnp.einsum('bhld,bhlv->bhdv', kt, v_new)
        return state, o

    xs = (jnp.moveaxis(q_exp, 2, 0), jnp.moveaxis(kcd, 2, 0), jnp.moveaxis(value, 2, 0),
          jnp.moveaxis(intra, 2, 0), jnp.moveaxis(k_tail, 2, 0), jnp.moveaxis(g_last, 2, 0))
    s0 = jnp.zeros((Bsz, H, Dk, Dv), jnp.float32)
    _, o = lax.scan(step, s0, xs)
    return o.transpose(1, 0, 3, 2, 4).reshape(Bsz, S, H, Dv)


def l2norm(x):
    x32 = x.astype(jnp.float32)
    return x32 * lax.rsqrt(jnp.sum(x32 * x32, axis=-1, keepdims=True) + EPS)


def hybrid_layer(h, p_i, norm_mix, w_in, sg_ln_g, sg_ln_b, sg_w, sg_b, sc_conv, gdn_conv,
                 gdn_a_log, gdn_dt_bias, gdn_norm, out_norm_a, out_norm_b, w_o, norm_ffn,
                 w_ff1, w_ff2, norm_ple, w_ple_gate, w_ple_proj):
    Bsz, S, _ = h.shape
    xn = rmsnorm(h, norm_mix)
    proj = xn @ w_in
    sg_uv, sc_bcx, gdn_qkv, gdn_z, gdn_a, gdn_b = jnp.split(proj, IN_SPLITS, axis=-1)

    u, v = jnp.split(jax.nn.gelu(sg_uv), 2, axis=-1)
    ya = spatial_gating(u, v, sg_ln_g, sg_ln_b, sg_w, sg_b)
    ya = group_rmsnorm(ya, out_norm_a, SG_HEADS)

    gb, gc_, xin = jnp.split(sc_bcx, 3, axis=-1)
    yb = gb * causal_dwconv(gc_ * xin, sc_conv)
    yb = group_rmsnorm(yb, out_norm_b, SC_GROUPS)

    qkv = jax.nn.silu(causal_dwconv(gdn_qkv, gdn_conv))
    q, k, vv = jnp.split(qkv, 3, axis=-1)
    q = l2norm(q.reshape(Bsz, S, GDN_HEADS, GDN_HEAD_DIM))
    k = l2norm(k.reshape(Bsz, S, GDN_HEADS, GDN_HEAD_DIM))
    vv = vv.reshape(Bsz, S, GDN_HEADS, GDN_HEAD_DIM)
    g = -jnp.exp(gdn_a_log.astype(jnp.float32)) * jax.nn.softplus(
        gdn_a.astype(jnp.float32) + gdn_dt_bias.astype(jnp.float32))
    beta = jax.nn.sigmoid(gdn_b.astype(jnp.float32))
    o = gated_delta_chunked(q, k, vv, g, beta)
    z = gdn_z.reshape(Bsz, S, GDN_HEADS, GDN_HEAD_DIM).astype(jnp.float32)
    yc = (rmsnorm(o, gdn_norm) * jax.nn.silu(z)).reshape(Bsz, S, GDN_WIDTH).astype(h.dtype)

    h = h + jnp.concatenate([ya, yb, yc], axis=-1) @ w_o

    hn = rmsnorm(h, norm_ffn)
    h = h + jnp.square(jax.nn.relu(hn @ w_ff1)) @ w_ff2

    hn = rmsnorm(h, norm_ple)
    h = h + (p_i @ w_ple_proj) * jax.nn.sigmoid(hn @ w_ple_gate)
    return h


def _fwd_setup_inputs(seed: int = 0) -> dict:
    key = jax.random.key(seed)
    ks = jax.random.split(key, 24)

    def nrm(k, shape, scale):
        return jax.random.normal(k, shape, jnp.float32) * scale

    def gain(k, w):
        return 1.0 + nrm(k, (DEPTH, w), 0.02)

    dt = jnp.exp(jax.random.uniform(ks[11], (DEPTH, GDN_HEADS), jnp.float32,
                                    math.log(1e-3), math.log(1e-1)))
    return {
        "x": nrm(ks[0], (BATCH, SEQ, D_MODEL), 1.0),
        "p": nrm(ks[1], (DEPTH, BATCH, SEQ, PLE_DIM), 1.0),
        "norm_mix": gain(ks[2], D_MODEL),
        "w_in": nrm(ks[3], (DEPTH, D_MODEL, IN_COLS), D_MODEL ** -0.5),
        "sg_ln_g": gain(ks[4], SG_WIDTH),
        "sg_ln_b": nrm(ks[5], (DEPTH, SG_WIDTH), 0.02),
        "sg_w": nrm(ks[6], (DEPTH, SG_HEADS, SG_CHUNK, SG_CHUNK), 0.5 * SG_CHUNK ** -0.5),
        "sg_b": 1.0 + nrm(ks[7], (DEPTH, SG_HEADS, SG_CHUNK), 0.1),
        "sc_conv": nrm(ks[8], (DEPTH, SC_KERNEL, SC_WIDTH), SC_KERNEL ** -0.5),
        "gdn_conv": nrm(ks[9], (DEPTH, GDN_CONV, 3 * GDN_WIDTH), GDN_CONV ** -0.5),
        "gdn_a_log": jnp.log(jax.random.uniform(ks[10], (DEPTH, GDN_HEADS), jnp.float32, 1.0, 16.0)),
        "gdn_dt_bias": dt + jnp.log(-jnp.expm1(-dt)),
        "gdn_norm": gain(ks[12], GDN_HEAD_DIM),
        "out_norm_a": gain(ks[13], SG_WIDTH),
        "out_norm_b": gain(ks[14], SC_WIDTH),
        "w_o": nrm(ks[15], (DEPTH, MIX_WIDTH, D_MODEL), MIX_WIDTH ** -0.5),
        "norm_ffn": gain(ks[16], D_MODEL),
        "w_ff1": nrm(ks[17], (DEPTH, D_MODEL, D_FF), D_MODEL ** -0.5),
        "w_ff2": nrm(ks[18], (DEPTH, D_FF, D_MODEL), D_FF ** -0.5),
        "norm_ple": gain(ks[19], D_MODEL),
        "w_ple_gate": nrm(ks[20], (DEPTH, D_MODEL, D_MODEL), D_MODEL ** -0.5),
        "w_ple_proj": nrm(ks[21], (DEPTH, PLE_DIM, D_MODEL), PLE_DIM ** -0.5),
        "norm_final": 1.0 + nrm(ks[22], (D_MODEL,), 0.02),
    }


def _fwd_reference(x, p, norm_mix, w_in, sg_ln_g, sg_ln_b, sg_w, sg_b, sc_conv, gdn_conv,
              gdn_a_log, gdn_dt_bias, gdn_norm, out_norm_a, out_norm_b, w_o, norm_ffn,
              w_ff1, w_ff2, norm_ple, w_ple_gate, w_ple_proj, norm_final):
    h = x
    for i in range(DEPTH):
        h = hybrid_layer(h, p[i], norm_mix[i], w_in[i], sg_ln_g[i], sg_ln_b[i], sg_w[i], sg_b[i],
                         sc_conv[i], gdn_conv[i], gdn_a_log[i], gdn_dt_bias[i], gdn_norm[i],
                         out_norm_a[i], out_norm_b[i], w_o[i], norm_ffn[i], w_ff1[i], w_ff2[i],
                         norm_ple[i], w_ple_gate[i], w_ple_proj[i])
    return rmsnorm(h, norm_final)


import jax as _jax
import jax.numpy as _jnp

TWIN_FORMAT = 'train_step'
FWD_PARAMS = ['x', 'p', 'norm_mix', 'w_in', 'sg_ln_g', 'sg_ln_b', 'sg_w', 'sg_b', 'sc_conv', 'gdn_conv', 'gdn_a_log', 'gdn_dt_bias', 'gdn_norm', 'out_norm_a', 'out_norm_b', 'w_o', 'norm_ffn', 'w_ff1', 'w_ff2', 'norm_ple', 'w_ple_gate', 'w_ple_proj', 'norm_final']
TWIN_WEIGHTS = ['norm_mix', 'w_in', 'sg_ln_g', 'sg_ln_b', 'sg_w', 'sg_b', 'sc_conv', 'gdn_conv', 'gdn_a_log', 'gdn_dt_bias', 'gdn_norm', 'out_norm_a', 'out_norm_b', 'w_o', 'norm_ffn', 'w_ff1', 'w_ff2', 'norm_ple', 'w_ple_gate', 'w_ple_proj', 'norm_final']
TWIN_DIFF_INPUT = 'x'
TWIN_INPUTS = ['x', 'p', 'norm_mix', 'w_in', 'sg_ln_g', 'sg_ln_b', 'sg_w', 'sg_b', 'sc_conv', 'gdn_conv', 'gdn_a_log', 'gdn_dt_bias', 'gdn_norm', 'out_norm_a', 'out_norm_b', 'w_o', 'norm_ffn', 'w_ff1', 'w_ff2', 'norm_ple', 'w_ple_gate', 'w_ple_proj', 'norm_final', 'loss_target', 'm_norm_mix', 'm_w_in', 'm_sg_ln_g', 'm_sg_ln_b', 'm_sg_w', 'm_sg_b', 'm_sc_conv', 'm_gdn_conv', 'm_gdn_a_log', 'm_gdn_dt_bias', 'm_gdn_norm', 'm_out_norm_a', 'm_out_norm_b', 'm_w_o', 'm_norm_ffn', 'm_w_ff1', 'm_w_ff2', 'm_norm_ple', 'm_w_ple_gate', 'm_w_ple_proj', 'm_norm_final', 'v_norm_mix', 'v_w_in', 'v_sg_ln_g', 'v_sg_ln_b', 'v_sg_w', 'v_sg_b', 'v_sc_conv', 'v_gdn_conv', 'v_gdn_a_log', 'v_gdn_dt_bias', 'v_gdn_norm', 'v_out_norm_a', 'v_out_norm_b', 'v_w_o', 'v_norm_ffn', 'v_w_ff1', 'v_w_ff2', 'v_norm_ple', 'v_w_ple_gate', 'v_w_ple_proj', 'v_norm_final']
TWIN_OUTPUTS = ['loss', 'grad_x', 'grad_norm_mix', 'grad_w_in', 'grad_sg_ln_g', 'grad_sg_ln_b', 'grad_sg_w', 'grad_sg_b', 'grad_sc_conv', 'grad_gdn_conv', 'grad_gdn_a_log', 'grad_gdn_dt_bias', 'grad_gdn_norm', 'grad_out_norm_a', 'grad_out_norm_b', 'grad_w_o', 'grad_norm_ffn', 'grad_w_ff1', 'grad_w_ff2', 'grad_norm_ple', 'grad_w_ple_gate', 'grad_w_ple_proj', 'grad_norm_final', 'delta_norm_mix', 'delta_w_in', 'delta_sg_ln_g', 'delta_sg_ln_b', 'delta_sg_w', 'delta_sg_b', 'delta_sc_conv', 'delta_gdn_conv', 'delta_gdn_a_log', 'delta_gdn_dt_bias', 'delta_gdn_norm', 'delta_out_norm_a', 'delta_out_norm_b', 'delta_w_o', 'delta_norm_ffn', 'delta_w_ff1', 'delta_w_ff2', 'delta_norm_ple', 'delta_w_ple_gate', 'delta_w_ple_proj', 'delta_norm_final', 'new_m_norm_mix', 'new_m_w_in', 'new_m_sg_ln_g', 'new_m_sg_ln_b', 'new_m_sg_w', 'new_m_sg_b', 'new_m_sc_conv', 'new_m_gdn_conv', 'new_m_gdn_a_log', 'new_m_gdn_dt_bias', 'new_m_gdn_norm', 'new_m_out_norm_a', 'new_m_out_norm_b', 'new_m_w_o', 'new_m_norm_ffn', 'new_m_w_ff1', 'new_m_w_ff2', 'new_m_norm_ple', 'new_m_w_ple_gate', 'new_m_w_ple_proj', 'new_m_norm_final', 'new_v_norm_mix', 'new_v_w_in', 'new_v_sg_ln_g', 'new_v_sg_ln_b', 'new_v_sg_w', 'new_v_sg_b', 'new_v_sc_conv', 'new_v_gdn_conv', 'new_v_gdn_a_log', 'new_v_gdn_dt_bias', 'new_v_gdn_norm', 'new_v_out_norm_a', 'new_v_out_norm_b', 'new_v_w_o', 'new_v_norm_ffn', 'new_v_w_ff1', 'new_v_w_ff2', 'new_v_norm_ple', 'new_v_w_ple_gate', 'new_v_w_ple_proj', 'new_v_norm_final']
TWIN_LEAF_KINDS = {'loss': 'loss', 'grad_x': 'grad_x', 'grad_norm_mix': 'grad_w', 'grad_w_in': 'grad_w', 'grad_sg_ln_g': 'grad_w', 'grad_sg_ln_b': 'grad_w', 'grad_sg_w': 'grad_w', 'grad_sg_b': 'grad_w', 'grad_sc_conv': 'grad_w', 'grad_gdn_conv': 'grad_w', 'grad_gdn_a_log': 'grad_w', 'grad_gdn_dt_bias': 'grad_w', 'grad_gdn_norm': 'grad_w', 'grad_out_norm_a': 'grad_w', 'grad_out_norm_b': 'grad_w', 'grad_w_o': 'grad_w', 'grad_norm_ffn': 'grad_w', 'grad_w_ff1': 'grad_w', 'grad_w_ff2': 'grad_w', 'grad_norm_ple': 'grad_w', 'grad_w_ple_gate': 'grad_w', 'grad_w_ple_proj': 'grad_w', 'grad_norm_final': 'grad_w', 'delta_norm_mix': 'delta_w', 'delta_w_in': 'delta_w', 'delta_sg_ln_g': 'delta_w', 'delta_sg_ln_b': 'delta_w', 'delta_sg_w': 'delta_w', 'delta_sg_b': 'delta_w', 'delta_sc_conv': 'delta_w', 'delta_gdn_conv': 'delta_w', 'delta_gdn_a_log': 'delta_w', 'delta_gdn_dt_bias': 'delta_w', 'delta_gdn_norm': 'delta_w', 'delta_out_norm_a': 'delta_w', 'delta_out_norm_b': 'delta_w', 'delta_w_o': 'delta_w', 'delta_norm_ffn': 'delta_w', 'delta_w_ff1': 'delta_w', 'delta_w_ff2': 'delta_w', 'delta_norm_ple': 'delta_w', 'delta_w_ple_gate': 'delta_w', 'delta_w_ple_proj': 'delta_w', 'delta_norm_final': 'delta_w', 'new_m_norm_mix': 'new_m', 'new_m_w_in': 'new_m', 'new_m_sg_ln_g': 'new_m', 'new_m_sg_ln_b': 'new_m', 'new_m_sg_w': 'new_m', 'new_m_sg_b': 'new_m', 'new_m_sc_conv': 'new_m', 'new_m_gdn_conv': 'new_m', 'new_m_gdn_a_log': 'new_m', 'new_m_gdn_dt_bias': 'new_m', 'new_m_gdn_norm': 'new_m', 'new_m_out_norm_a': 'new_m', 'new_m_out_norm_b': 'new_m', 'new_m_w_o': 'new_m', 'new_m_norm_ffn': 'new_m', 'new_m_w_ff1': 'new_m', 'new_m_w_ff2': 'new_m', 'new_m_norm_ple': 'new_m', 'new_m_w_ple_gate': 'new_m', 'new_m_w_ple_proj': 'new_m', 'new_m_norm_final': 'new_m', 'new_v_norm_mix': 'new_v', 'new_v_w_in': 'new_v', 'new_v_sg_ln_g': 'new_v', 'new_v_sg_ln_b': 'new_v', 'new_v_sg_w': 'new_v', 'new_v_sg_b': 'new_v', 'new_v_sc_conv': 'new_v', 'new_v_gdn_conv': 'new_v', 'new_v_gdn_a_log': 'new_v', 'new_v_gdn_dt_bias': 'new_v', 'new_v_gdn_norm': 'new_v', 'new_v_out_norm_a': 'new_v', 'new_v_out_norm_b': 'new_v', 'new_v_w_o': 'new_v', 'new_v_norm_ffn': 'new_v', 'new_v_w_ff1': 'new_v', 'new_v_w_ff2': 'new_v', 'new_v_norm_ple': 'new_v', 'new_v_w_ple_gate': 'new_v', 'new_v_w_ple_proj': 'new_v', 'new_v_norm_final': 'new_v'}


def _forward(args):
    return _fwd_reference(*[args[k] for k in FWD_PARAMS])


def _output_shape():
    def fwd():
        inp = _fwd_setup_inputs(0)
        return _fwd_reference(*[inp[k] for k in FWD_PARAMS])
    out = _jax.eval_shape(fwd)
    return out.shape, out.dtype

N_MICROBATCH = 1
ADAM_LR = 0.001
ADAM_B1 = 0.9
ADAM_B2 = 0.999
ADAM_EPS = 1e-08
ADAM_WD = 0.01
ADAM_STEP = 10
PER_EXAMPLE_BATCH_AXIS = {'x': 0, 'p': 1, 'loss_target': 0}
SHARED_INPUTS = []
_WEIGHT_DTYPES = {'norm_mix': _jnp.float32, 'w_in': _jnp.float32, 'sg_ln_g': _jnp.float32, 'sg_ln_b': _jnp.float32, 'sg_w': _jnp.float32, 'sg_b': _jnp.float32, 'sc_conv': _jnp.float32, 'gdn_conv': _jnp.float32, 'gdn_a_log': _jnp.float32, 'gdn_dt_bias': _jnp.float32, 'gdn_norm': _jnp.float32, 'out_norm_a': _jnp.float32, 'out_norm_b': _jnp.float32, 'w_o': _jnp.float32, 'norm_ffn': _jnp.float32, 'w_ff1': _jnp.float32, 'w_ff2': _jnp.float32, 'norm_ple': _jnp.float32, 'w_ple_gate': _jnp.float32, 'w_ple_proj': _jnp.float32, 'norm_final': _jnp.float32}
MOMENT_SCALE = {'norm_mix': 8.216985e-02, 'w_in': 4.564202e-02, 'sg_ln_g': 2.124653e-02, 'sg_ln_b': 2.226398e-02, 'sg_w': 4.281928e-02, 'sg_b': 2.120979e-02, 'sc_conv': 7.009737e-02, 'gdn_conv': 2.949953e-02, 'gdn_a_log': 1.122581e-01, 'gdn_dt_bias': 1.085139e-01, 'gdn_norm': 1.124682e-01, 'out_norm_a': 7.917700e-02, 'out_norm_b': 6.546200e-02, 'w_o': 5.650406e-02, 'norm_ffn': 6.827727e-02, 'w_ff1': 3.351439e-02, 'w_ff2': 6.764078e-02, 'norm_ple': 9.982038e-03, 'w_ple_gate': 1.009604e-02, 'w_ple_proj': 2.496501e-02, 'norm_final': 1.617904e+01}


def _to_microbatches(a, axis):
    t = _jnp.moveaxis(a, axis, 0)
    t = t.reshape((N_MICROBATCH, t.shape[0] // N_MICROBATCH) + t.shape[1:])
    return _jnp.moveaxis(t, 1, axis + 1)


def setup_inputs(seed: int = 0) -> dict:
    inp = _fwd_setup_inputs(seed)
    key = _jax.random.fold_in(_jax.random.key(seed), 7919)
    shape, _ = _output_shape()
    out = dict(inp)
    out["loss_target"] = _jax.random.normal(_jax.random.fold_in(key, 0), shape, _jnp.float32)
    for i, name in enumerate(TWIN_WEIGHTS):
        w = inp[name].astype(_jnp.float32)
        if MOMENT_SCALE is None:
            s = _jnp.sqrt(_jnp.mean(_jnp.square(w)) + 1e-30)
        else:
            s = MOMENT_SCALE[name]
        km, kv = _jax.random.split(_jax.random.fold_in(key, i + 1))
        out[name] = w
        out["m_" + name] = s * _jax.random.normal(km, w.shape, _jnp.float32)
        out["v_" + name] = (s * s) * _jax.random.uniform(kv, w.shape, _jnp.float32, 0.5, 1.5)
    if N_MICROBATCH > 1:
        for name, axis in PER_EXAMPLE_BATCH_AXIS.items():
            out[name] = _to_microbatches(out[name], axis)
    return {'x': out['x'], 'p': out['p'], 'norm_mix': out['norm_mix'], 'w_in': out['w_in'], 'sg_ln_g': out['sg_ln_g'], 'sg_ln_b': out['sg_ln_b'], 'sg_w': out['sg_w'], 'sg_b': out['sg_b'], 'sc_conv': out['sc_conv'], 'gdn_conv': out['gdn_conv'], 'gdn_a_log': out['gdn_a_log'], 'gdn_dt_bias': out['gdn_dt_bias'], 'gdn_norm': out['gdn_norm'], 'out_norm_a': out['out_norm_a'], 'out_norm_b': out['out_norm_b'], 'w_o': out['w_o'], 'norm_ffn': out['norm_ffn'], 'w_ff1': out['w_ff1'], 'w_ff2': out['w_ff2'], 'norm_ple': out['norm_ple'], 'w_ple_gate': out['w_ple_gate'], 'w_ple_proj': out['w_ple_proj'], 'norm_final': out['norm_final'], 'loss_target': out['loss_target'], 'm_norm_mix': out['m_norm_mix'], 'm_w_in': out['m_w_in'], 'm_sg_ln_g': out['m_sg_ln_g'], 'm_sg_ln_b': out['m_sg_ln_b'], 'm_sg_w': out['m_sg_w'], 'm_sg_b': out['m_sg_b'], 'm_sc_conv': out['m_sc_conv'], 'm_gdn_conv': out['m_gdn_conv'], 'm_gdn_a_log': out['m_gdn_a_log'], 'm_gdn_dt_bias': out['m_gdn_dt_bias'], 'm_gdn_norm': out['m_gdn_norm'], 'm_out_norm_a': out['m_out_norm_a'], 'm_out_norm_b': out['m_out_norm_b'], 'm_w_o': out['m_w_o'], 'm_norm_ffn': out['m_norm_ffn'], 'm_w_ff1': out['m_w_ff1'], 'm_w_ff2': out['m_w_ff2'], 'm_norm_ple': out['m_norm_ple'], 'm_w_ple_gate': out['m_w_ple_gate'], 'm_w_ple_proj': out['m_w_ple_proj'], 'm_norm_final': out['m_norm_final'], 'v_norm_mix': out['v_norm_mix'], 'v_w_in': out['v_w_in'], 'v_sg_ln_g': out['v_sg_ln_g'], 'v_sg_ln_b': out['v_sg_ln_b'], 'v_sg_w': out['v_sg_w'], 'v_sg_b': out['v_sg_b'], 'v_sc_conv': out['v_sc_conv'], 'v_gdn_conv': out['v_gdn_conv'], 'v_gdn_a_log': out['v_gdn_a_log'], 'v_gdn_dt_bias': out['v_gdn_dt_bias'], 'v_gdn_norm': out['v_gdn_norm'], 'v_out_norm_a': out['v_out_norm_a'], 'v_out_norm_b': out['v_out_norm_b'], 'v_w_o': out['v_w_o'], 'v_norm_ffn': out['v_norm_ffn'], 'v_w_ff1': out['v_w_ff1'], 'v_w_ff2': out['v_w_ff2'], 'v_norm_ple': out['v_norm_ple'], 'v_w_ple_gate': out['v_w_ple_gate'], 'v_w_ple_proj': out['v_w_ple_proj'], 'v_norm_final': out['v_norm_final']}


def _loss(weights, diff, rest, loss_target):
    with _jax.named_scope("forward"):
        args = {**rest, TWIN_DIFF_INPUT: diff, **{k: w.astype(_WEIGHT_DTYPES[k]) for k, w in weights.items()}}
        y = _forward(args)
    with _jax.named_scope("loss_head"):
        err = _jnp.square(y.astype(_jnp.float32) - loss_target)
        return 0.5 * _jnp.sum(_jnp.mean(err, axis=-1)) if err.ndim else 0.5 * err


def _adamw(w, g, m, v):
    m = ADAM_B1 * m + (1.0 - ADAM_B1) * g
    v = ADAM_B2 * v + (1.0 - ADAM_B2) * _jnp.square(g)
    m_hat = m / (1.0 - ADAM_B1 ** ADAM_STEP)
    v_hat = v / (1.0 - ADAM_B2 ** ADAM_STEP)
    delta = -ADAM_LR * (m_hat / (_jnp.sqrt(v_hat) + ADAM_EPS) + ADAM_WD * w)
    return delta, m, v


def reference(x, p, norm_mix, w_in, sg_ln_g, sg_ln_b, sg_w, sg_b, sc_conv, gdn_conv, gdn_a_log, gdn_dt_bias, gdn_norm, out_norm_a, out_norm_b, w_o, norm_ffn, w_ff1, w_ff2, norm_ple, w_ple_gate, w_ple_proj, norm_final, loss_target, m_norm_mix, m_w_in, m_sg_ln_g, m_sg_ln_b, m_sg_w, m_sg_b, m_sc_conv, m_gdn_conv, m_gdn_a_log, m_gdn_dt_bias, m_gdn_norm, m_out_norm_a, m_out_norm_b, m_w_o, m_norm_ffn, m_w_ff1, m_w_ff2, m_norm_ple, m_w_ple_gate, m_w_ple_proj, m_norm_final, v_norm_mix, v_w_in, v_sg_ln_g, v_sg_ln_b, v_sg_w, v_sg_b, v_sc_conv, v_gdn_conv, v_gdn_a_log, v_gdn_dt_bias, v_gdn_norm, v_out_norm_a, v_out_norm_b, v_w_o, v_norm_ffn, v_w_ff1, v_w_ff2, v_norm_ple, v_w_ple_gate, v_w_ple_proj, v_norm_final):
    given = dict(x=x, p=p, norm_mix=norm_mix, w_in=w_in, sg_ln_g=sg_ln_g, sg_ln_b=sg_ln_b, sg_w=sg_w, sg_b=sg_b, sc_conv=sc_conv, gdn_conv=gdn_conv, gdn_a_log=gdn_a_log, gdn_dt_bias=gdn_dt_bias, gdn_norm=gdn_norm, out_norm_a=out_norm_a, out_norm_b=out_norm_b, w_o=w_o, norm_ffn=norm_ffn, w_ff1=w_ff1, w_ff2=w_ff2, norm_ple=norm_ple, w_ple_gate=w_ple_gate, w_ple_proj=w_ple_proj, norm_final=norm_final, loss_target=loss_target, m_norm_mix=m_norm_mix, m_w_in=m_w_in, m_sg_ln_g=m_sg_ln_g, m_sg_ln_b=m_sg_ln_b, m_sg_w=m_sg_w, m_sg_b=m_sg_b, m_sc_conv=m_sc_conv, m_gdn_conv=m_gdn_conv, m_gdn_a_log=m_gdn_a_log, m_gdn_dt_bias=m_gdn_dt_bias, m_gdn_norm=m_gdn_norm, m_out_norm_a=m_out_norm_a, m_out_norm_b=m_out_norm_b, m_w_o=m_w_o, m_norm_ffn=m_norm_ffn, m_w_ff1=m_w_ff1, m_w_ff2=m_w_ff2, m_norm_ple=m_norm_ple, m_w_ple_gate=m_w_ple_gate, m_w_ple_proj=m_w_ple_proj, m_norm_final=m_norm_final, v_norm_mix=v_norm_mix, v_w_in=v_w_in, v_sg_ln_g=v_sg_ln_g, v_sg_ln_b=v_sg_ln_b, v_sg_w=v_sg_w, v_sg_b=v_sg_b, v_sc_conv=v_sc_conv, v_gdn_conv=v_gdn_conv, v_gdn_a_log=v_gdn_a_log, v_gdn_dt_bias=v_gdn_dt_bias, v_gdn_norm=v_gdn_norm, v_out_norm_a=v_out_norm_a, v_out_norm_b=v_out_norm_b, v_w_o=v_w_o, v_norm_ffn=v_norm_ffn, v_w_ff1=v_w_ff1, v_w_ff2=v_w_ff2, v_norm_ple=v_norm_ple, v_w_ple_gate=v_w_ple_gate, v_w_ple_proj=v_w_ple_proj, v_norm_final=v_norm_final)
    weights = {n: given[n] for n in TWIN_WEIGHTS}
    shared = {n: given[n] for n in SHARED_INPUTS}
    per_example = {n: given[n] for n in ['x', 'p']}
    grad_fn = _jax.value_and_grad(_loss, argnums=(0, 1))

    def one_microbatch(ex, loss_target):
        ex = dict(ex)
        diff = ex.pop(TWIN_DIFF_INPUT)
        return grad_fn(weights, diff, {**shared, **ex}, loss_target)

    if N_MICROBATCH == 1:
        loss, (grad_w, grad_x) = one_microbatch(per_example, given["loss_target"])
    else:
        def body(carry, xs):
            loss_sum, grad_sum = carry
            l_k, (gw_k, gx_k) = one_microbatch(xs[0], xs[1])
            with _jax.named_scope("update"):
                return (loss_sum + l_k, _jax.tree.map(_jnp.add, grad_sum, gw_k)), gx_k

        init = (_jnp.zeros((), _jnp.float32), _jax.tree.map(_jnp.zeros_like, weights))
        (loss, grad_w), grad_x = _jax.lax.scan(body, init, (per_example, given["loss_target"]))
    with _jax.named_scope("update"):
        delta_w, new_m, new_v = {}, {}, {}
        for n in TWIN_WEIGHTS:
            delta_w[n], new_m[n], new_v[n] = _adamw(weights[n], grad_w[n], given["m_" + n], given["v_" + n])
    return (loss, grad_x, *[grad_w[n] for n in TWIN_WEIGHTS], *[delta_w[n] for n in TWIN_WEIGHTS],
            *[new_m[n] for n in TWIN_WEIGHTS], *[new_v[n] for n in TWIN_WEIGHTS])
```

```python
import functools
import math

import jax
import jax.numpy as jnp
from jax import lax
from jax.experimental import pallas as pl
from jax.experimental.pallas import tpu as pltpu

F32 = jnp.float32
BF16 = jnp.bfloat16
HI = lax.Precision.HIGHEST
MESH = pl.DeviceIdType.MESH

LANES = 128
EPS = 1e-6
SG_HEADS, SG_CHUNK = 4, 128
SC_GROUPS, SC_KERNEL = 4, 3
GDN_HEADS, GDN_CONV, GDN_CHUNK = 8, 4, 64
SG_W = SG_HEADS * LANES
SC_W = SC_GROUPS * LANES
GDN_W = GDN_HEADS * LANES
IN_COLS = 2 * SG_W + 3 * SC_W + 4 * GDN_W + 2 * GDN_HEADS
PROJ_W = 7168
CB_U, CB_V = 0, 4
CB_GB, CB_GC, CB_X = 8, 12, 16
CB_Q, CB_K, CB_VV, CB_Z, CB_AB = 20, 28, 36, 44, 52
N_SHARD = 4
N_DEV = 8
DEPTH = 2

ADAM_LR, ADAM_B1, ADAM_B2, ADAM_EPS, ADAM_WD, ADAM_STEP = 0.001, 0.9, 0.999, 1e-08, 0.01, 10

VMEM_LIMIT = 56 << 20

WEIGHTS = ['norm_mix', 'w_in', 'sg_ln_g', 'sg_ln_b', 'sg_w', 'sg_b', 'sc_conv', 'gdn_conv', 'gdn_a_log',
           'gdn_dt_bias', 'gdn_norm', 'out_norm_a', 'out_norm_b', 'w_o', 'norm_ffn', 'w_ff1', 'w_ff2', 'norm_ple',
           'w_ple_gate', 'w_ple_proj', 'norm_final']
BIG = ['w_in', 'w_o', 'w_ff1', 'w_ff2', 'w_ple_gate', 'w_ple_proj']
CONVS = ['sc_conv', 'gdn_conv']
SMALL = [n for n in WEIGHTS if n not in BIG]


def _cparams(sem=None):
    return pltpu.CompilerParams(dimension_semantics=sem, vmem_limit_bytes=VMEM_LIMIT)


def _dot(a, b, dims, prec=None):
    return lax.dot_general(a, b, (dims, ((), ())), precision=prec, preferred_element_type=F32)


def _mm(a, b):
    return _dot(a, b, ((1,), (0,)), HI)


def _mm_nt(a, b):
    return _dot(a, b, ((1,), (1,)), HI)


def _mm_tn(a, b):
    return _dot(a, b, ((0,), (0,)), HI)


@jax.custom_vjp
def _bmm(a, b):
    return _dot(a.astype(BF16), b.astype(BF16), ((1,), (0,)))


def _bmm_fwd(a, b):
    return _bmm(a, b), (a, b)


def _bmm_bwd(res, g):
    a, b = res
    gb = g.astype(BF16)
    return _dot(gb, b.astype(BF16), ((1,), (1,))), _dot(a.astype(BF16), gb, ((0,), (0,)))


_bmm.defvjp(_bmm_fwd, _bmm_bwd)


def _sigmoid(x):
    return 1.0 / (1.0 + jnp.exp(-x))


def _silu(x):
    return x * _sigmoid(x)


def _gelu(x):
    c = math.sqrt(2.0 / math.pi)
    return 0.5 * x * (1.0 + jnp.tanh(c * (x + 0.044715 * (x * x * x))))


def _softplus(x):
    return jnp.maximum(x, 0.0) + jnp.log(1.0 + jnp.exp(-jnp.abs(x)))


def _rms(x, g):
    return x * lax.rsqrt(jnp.mean(x * x, axis=-1, keepdims=True) + EPS) * g


def _roll_rows(x, shift):
    return pltpu.roll(x, shift % x.shape[0], 0)


@functools.partial(jax.custom_vjp, nondiff_argnums=(1,))
def _shift_down(x, j):
    row = lax.broadcasted_iota(jnp.int32, x.shape, 0)
    return jnp.where(row >= j, _roll_rows(x, j), 0.0)


def _shift_down_fwd(x, j):
    return _shift_down(x, j), None


def _shift_down_bwd(j, _, dy):
    row = lax.broadcasted_iota(jnp.int32, dy.shape, 0)
    return (jnp.where(row < dy.shape[0] - j, _roll_rows(dy, -j), 0.0),)


_shift_down.defvjp(_shift_down_fwd, _shift_down_bwd)


def _causal_conv(x, taps):
    k = len(taps)
    y = taps[k - 1] * x
    for j in range(k - 1):
        y = y + taps[j] * _shift_down(x, k - 1 - j)
    return y


class _In:
    def __init__(self, arr, block, imap, shared=False, gshape=None, gmap=None):
        self.arr, self.block, self.imap, self.shared = arr, block, imap, shared
        self.gshape = arr.shape if gshape is None else gshape
        self.gmap = imap if gmap is None else gmap


def _fused(name, fn, n, ins, outs, douts=None, need=None, gdt=None, add=None):
    n_in, n_out = len(ins), len(outs)
    in_specs = [pl.BlockSpec(s.block, s.imap) for s in ins]
    out_specs = [pl.BlockSpec(bs, im) for _, _, bs, im in outs]
    if douts is None:
        def body(*refs):
            res = fn(*[r[...] for r in refs[:n_in]])
            for r, v in zip(refs[n_in:], res):
                r[...] = v.astype(r.dtype)

        return pl.pallas_call(
            body, name=name, grid=(n,), in_specs=in_specs, out_specs=out_specs,
            out_shape=[jax.ShapeDtypeStruct(s, d) for s, d, _, _ in outs],
            compiler_params=_cparams(("arbitrary",)),
        )(*[s.arr for s in ins])

    gdt = list(gdt) if gdt is not None else [F32] * n_in
    add = dict(add or {})
    gidx = [i for i in range(n_in) if need[i]]
    aidx = [i for i in gidx if i in add]

    def body(*refs):
        in_refs, d_refs = refs[:n_in], refs[n_in:n_in + n_out]
        a_refs = dict(zip(aidx, refs[n_in + n_out:n_in + n_out + len(aidx)]))
        g_refs = refs[n_in + n_out + len(aidx):]
        vals = [r[...] for r in in_refs]

        def f(*dv):
            full = list(vals)
            for i, v in zip(gidx, dv):
                full[i] = v
            return tuple(o.astype(F32) for o in fn(*full))

        _, vjp = jax.vjp(f, *[vals[i].astype(F32) for i in gidx])
        grads = vjp(tuple(r[...].astype(F32) for r in d_refs))
        for i, g_ref, g in zip(gidx, g_refs, grads):
            if i in a_refs:
                g = g + a_refs[i][...].astype(F32)
            if ins[i].shared:
                @pl.when(pl.program_id(0) == 0)
                def _():
                    g_ref[...] = jnp.zeros_like(g_ref)
                g_ref[...] += g.astype(g_ref.dtype)
            else:
                g_ref[...] = g.astype(g_ref.dtype)

    g_specs = [pl.BlockSpec(ins[i].block, ins[i].gmap) for i in gidx]
    g_shape = [jax.ShapeDtypeStruct(ins[i].gshape, gdt[i]) for i in gidx]
    res = pl.pallas_call(
        body, name=name, grid=(n,), in_specs=in_specs + out_specs + [g_specs[gidx.index(i)] for i in aidx],
        out_specs=g_specs, out_shape=g_shape,
        compiler_params=_cparams(("arbitrary",)),
    )(*[s.arr for s in ins], *douts, *[add[i] for i in aidx])
    full = [None] * n_in
    for i, g in zip(gidx, res):
        full[i] = g
    return full


def _row_in(a, tm, cb=None):
    if cb is None:
        return _In(a, (tm, a.shape[1]), lambda i: (i, 0))
    return _In(a, (tm, LANES), lambda i: (i, cb), gshape=(a.shape[0], LANES), gmap=lambda i: (i, 0))


def _row_shared(a):
    return _In(a, a.shape, lambda i: (0, 0), shared=True)


def _row_out(t, w, dt, tm):
    return ((t, w), dt, (tm, w), lambda i: (i, 0))


def _col_in(a, base, nblk):
    t = a.shape[0]
    return _In(a, (t, LANES), lambda i: (0, base + i), gshape=(t, nblk * LANES), gmap=lambda i: (0, i))


def _col_par(a):
    return _In(a, (None,) + a.shape[1:], lambda i: (i, 0, 0))


def _col_out(t, w, dt, base=0):
    return ((t, w), dt, (t, LANES), lambda i: (0, base + i))


def _matmul(name, a, b, mode, m, n, k, tm, tn, tk, b_spec=None, epilogue=None, extras=(), outs=None):
    tm, tn, tk = min(tm, m), min(tn, n), min(tk, k)
    assert m % tm == 0 and n % tn == 0 and k % tk == 0, (name, m, n, k, tm, tn, tk)
    nk = k // tk
    a_spec = (pl.BlockSpec((tk, tm), lambda i, j, kk: (kk, i)) if mode == "tn"
              else pl.BlockSpec((tm, tk), lambda i, j, kk: (i, kk)))
    if b_spec is None:
        b_spec = (pl.BlockSpec((tn, tk), lambda i, j, kk: (j, kk)) if mode == "nt"
                  else pl.BlockSpec((tk, tn), lambda i, j, kk: (kk, j)))
    dims = {"nn": ((1,), (0,)), "nt": ((1,), (1,)), "tn": ((0,), (0,))}[mode]
    if outs is None:
        outs = [((m, n), F32, (tm, tn), lambda i, j, kk: (i, j))]
    if epilogue is None:
        epilogue = lambda acc: (acc,)
    n_ex = len(extras)

    def body(*refs):
        a_ref, b_ref = refs[0], refs[1]
        ex_refs = refs[2:2 + n_ex]
        o_refs = refs[2 + n_ex:2 + n_ex + len(outs)]
        part = _dot(a_ref[...].astype(BF16), b_ref[...].astype(BF16), dims)

        def finish(acc):
            for r, v in zip(o_refs, epilogue(acc, *[e[...] for e in ex_refs])):
                r[...] = v.astype(r.dtype)

        if nk == 1:
            finish(part)
        else:
            acc_ref = refs[-1]
            kk = pl.program_id(2)

            @pl.when(kk == 0)
            def _():
                acc_ref[...] = part

            @pl.when(kk > 0)
            def _():
                acc_ref[...] += part

            @pl.when(kk == nk - 1)
            def _():
                finish(acc_ref[...])

    ex_specs = [pl.BlockSpec((tm, tn), lambda i, j, kk: (i, j)) for _ in extras]
    res = pl.pallas_call(
        body, name=name, grid=(m // tm, n // tn, nk),
        in_specs=[a_spec, b_spec] + ex_specs,
        out_specs=[pl.BlockSpec(bs, im) for _, _, bs, im in outs],
        out_shape=[jax.ShapeDtypeStruct(s, d) for s, d, _, _ in outs],
        scratch_shapes=[pltpu.VMEM((tm, tn), F32)] if nk > 1 else [],
        compiler_params=_cparams(("parallel", "parallel", "arbitrary")),
    )(a, b, *extras)
    return res if len(res) > 1 else res[0]


def _fn_sgu(u_pre, v_pre, ln_g, ln_b, w, bb, na):
    t = u_pre.shape[0]
    u = _gelu(u_pre)
    v = _gelu(v_pre)
    mu = jnp.mean(v, axis=-1, keepdims=True)
    vc = v - mu
    vh = vc * lax.rsqrt(jnp.mean(vc * vc, axis=-1, keepdims=True) + EPS) * ln_g + ln_b
    ri = lax.broadcasted_iota(jnp.int32, w.shape, 0)
    ci = lax.broadcasted_iota(jnp.int32, w.shape, 1)
    wc = jnp.where(ri >= ci, w, 0.0)
    f = jnp.concatenate([_bmm(wc, vh[c * SG_CHUNK:(c + 1) * SG_CHUNK]) + bb for c in range(t // SG_CHUNK)], axis=0)
    return (_rms(u * f, na),)


def _fn_sconv(gb, gc, xin, w0, w1, w2, nb):
    return (_rms(gb * _causal_conv(gc * xin, (w0, w1, w2)), nb),)


def _fn_gdn_qk(pre, w0, w1, w2, w3):
    a = _silu(_causal_conv(pre, (w0, w1, w2, w3)))
    return (a * lax.rsqrt(jnp.sum(a * a, axis=-1, keepdims=True) + EPS),)


def _fn_gdn_v(pre, w0, w1, w2, w3):
    return (_silu(_causal_conv(pre, (w0, w1, w2, w3))),)


def _fn_gdn_gates(ab, a_log, dt_bias):
    lane = lax.broadcasted_iota(jnp.int32, ab.shape, 1)
    g = -jnp.exp(a_log) * _softplus(ab + dt_bias)
    return (jnp.where(lane < GDN_HEADS, g, jnp.where(lane < 2 * GDN_HEADS, _sigmoid(ab), 0.0)),)


def _solve_unit_lower(a):
    n = a.shape[0]
    ri = lax.broadcasted_iota(jnp.int32, (n, n), 0)
    ci = lax.broadcasted_iota(jnp.int32, (n, n), 1)
    t = (ri == ci).astype(F32)
    sh = 0
    while (1 << sh) < n:
        rb = jnp.right_shift(ri, sh)
        cb = jnp.right_shift(ci, sh)
        off = ((rb & 1) == 1) & (cb == rb - 1)
        t = t - _mm(_mm(t, jnp.where(off, a, 0.0)), t)
        sh += 1
    return t


def _gdn_chunk(s, q, k, v, g, beta):
    n, dk = q.shape
    ri = lax.broadcasted_iota(jnp.int32, (n, n), 0)
    ci = lax.broadcasted_iota(jnp.int32, (n, n), 1)
    incl = ri >= ci
    eye = (ri == ci).astype(F32)
    g_row = jnp.sum(eye * g, axis=0, keepdims=True)
    gc = jnp.sum(jnp.where(incl, g_row, 0.0), axis=1, keepdims=True)
    gc_row = jnp.sum(eye * gc, axis=0, keepdims=True)
    decay_incl = jnp.where(incl, jnp.exp(jnp.where(incl, gc - gc_row, 0.0)), 0.0)
    decay_strict = jnp.where(ri > ci, decay_incl, 0.0)
    q = q * (dk ** -0.5)
    kb = k * beta
    t = _solve_unit_lower(_mm_nt(kb, k) * decay_strict)
    eg = jnp.exp(gc)
    value = _mm(t, v * beta)
    kcd = _mm(t, kb * eg)
    intra = _mm_nt(q, k) * decay_incl
    last = lax.broadcasted_iota(jnp.int32, (n, 1), 0) == (n - 1)
    g_last = jnp.sum(jnp.where(last, gc, 0.0), axis=0, keepdims=True)
    v_new = value - _mm(kcd, s)
    o = _mm(q * eg, s) + _mm(intra, v_new)
    s_new = s * jnp.exp(g_last) + _mm_tn(k * jnp.exp(g_last - gc), v_new)
    return s_new, o


def _gdn_post(o, z, nrm):
    return _rms(o, nrm) * _silu(z)


def _head_gates(gates, h):
    lane = lax.broadcasted_iota(jnp.int32, gates.shape, 1)
    g = jnp.sum(jnp.where(lane == h, gates, 0.0), axis=1, keepdims=True)
    b = jnp.sum(jnp.where(lane == h + GDN_HEADS, gates, 0.0), axis=1, keepdims=True)
    return g, b


def _gdn_specs(t):
    col = lambda base: pl.BlockSpec((t, LANES), lambda h: (0, base + h))
    shared = pl.BlockSpec((t, LANES), lambda h: (0, 0))
    par = pl.BlockSpec((None, 1, LANES), lambda h: (h, 0, 0))
    return col, shared, par


def _gdn_core_fwd(q, k, v, gates, proj, nrm):
    t = q.shape[0]
    nc = t // GDN_CHUNK

    def body(q_ref, k_ref, v_ref, gt_ref, z_ref, n_ref, y_ref, o_scr):
        h = pl.program_id(0)

        def step(c, s):
            rows = pl.ds(pl.multiple_of(c * GDN_CHUNK, GDN_CHUNK), GDN_CHUNK)
            g, b = _head_gates(gt_ref[rows, :], h)
            s, o = _gdn_chunk(s, q_ref[rows, :], k_ref[rows, :], v_ref[rows, :], g, b)
            o_scr[rows, :] = o
            return s

        lax.fori_loop(0, nc, step, jnp.zeros((LANES, LANES), F32))
        y_ref[...] = _gdn_post(o_scr[...], z_ref[...], n_ref[...]).astype(y_ref.dtype)

    col, shared, par = _gdn_specs(t)
    return pl.pallas_call(
        body, name="gdn_core_fwd", grid=(GDN_HEADS,),
        in_specs=[col(0), col(0), col(0), shared, col(CB_Z), par],
        out_specs=col(0), out_shape=jax.ShapeDtypeStruct((t, GDN_W), BF16),
        scratch_shapes=[pltpu.VMEM((t, LANES), F32)],
        compiler_params=_cparams(("arbitrary",)),
    )(q, k, v, gates, proj, nrm)


def _gdn_core_bwd(q, k, v, gates, proj, nrm, dy, dy_base):
    t = q.shape[0]
    nc = t // GDN_CHUNK

    def body(q_ref, k_ref, v_ref, gt_ref, z_ref, n_ref, dy_ref, dq_ref, dk_ref, dv_ref, dgt_ref, dz_ref, dn_ref,
             o_scr, s_scr):
        h = pl.program_id(0)

        def step(c, s):
            rows = pl.ds(pl.multiple_of(c * GDN_CHUNK, GDN_CHUNK), GDN_CHUNK)
            s_scr[c] = s
            g, b = _head_gates(gt_ref[rows, :], h)
            s, o = _gdn_chunk(s, q_ref[rows, :], k_ref[rows, :], v_ref[rows, :], g, b)
            o_scr[rows, :] = o
            return s

        lax.fori_loop(0, nc, step, jnp.zeros((LANES, LANES), F32))
        _, vjp_post = jax.vjp(_gdn_post, o_scr[...], z_ref[...], n_ref[...])
        do, dz, dn = vjp_post(dy_ref[...].astype(F32))
        dz_ref[...] = dz.astype(dz_ref.dtype)
        dn_ref[...] = dn
        o_scr[...] = do

        @pl.when(h == 0)
        def _():
            dgt_ref[...] = jnp.zeros_like(dgt_ref)

        def rstep(i, ds):
            c = nc - 1 - i
            rows = pl.ds(pl.multiple_of(c * GDN_CHUNK, GDN_CHUNK), GDN_CHUNK)
            g, b = _head_gates(gt_ref[rows, :], h)
            _, vjp_c = jax.vjp(_gdn_chunk, s_scr[c], q_ref[rows, :], k_ref[rows, :], v_ref[rows, :], g, b)
            ds, dq, dk, dv, dg, db = vjp_c((ds, o_scr[rows, :]))
            dq_ref[rows, :] = dq
            dk_ref[rows, :] = dk
            dv_ref[rows, :] = dv
            lane = lax.broadcasted_iota(jnp.int32, (GDN_CHUNK, LANES), 1)
            dgt_ref[rows, :] += jnp.where(lane == h, dg, 0.0) + jnp.where(lane == h + GDN_HEADS, db, 0.0)
            return ds

        lax.fori_loop(0, nc, rstep, jnp.zeros((LANES, LANES), F32))

    col, shared, par = _gdn_specs(t)
    wide = jax.ShapeDtypeStruct((t, GDN_W), F32)
    return pl.pallas_call(
        body, name="gdn_core_bwd", grid=(GDN_HEADS,),
        in_specs=[col(0), col(0), col(0), shared, col(CB_Z), par, col(dy_base)],
        out_specs=[col(0), col(0), col(0), shared, col(0), par],
        out_shape=[wide, wide, wide, jax.ShapeDtypeStruct((t, LANES), F32),
                   jax.ShapeDtypeStruct((t, GDN_W), BF16), jax.ShapeDtypeStruct((GDN_HEADS, 1, LANES), F32)],
        scratch_shapes=[pltpu.VMEM((t, LANES), F32), pltpu.VMEM((nc, LANES, LANES), F32)],
        compiler_params=_cparams(("arbitrary",)),
    )(q, k, v, gates, proj, nrm, dy)


TM = 512
TR = 256


def _rms_fwd(name, h, g):
    t, d = h.shape
    tm = min(TR, t)
    return _fused(name, lambda hb, gb: (_rms(hb, gb),), t // tm, [_row_in(h, tm), _row_shared(g)],
                  [_row_out(t, d, BF16, tm)])[0]


def _rms_bwd(name, h, g, dxn, dh_next):
    t, d = h.shape
    tm = min(TR, t)
    dh, dg = _fused(name, lambda hb, gb: (_rms(hb, gb),), t // tm, [_row_in(h, tm), _row_shared(g)],
                    [_row_out(t, d, F32, tm)], douts=[dxn], need=[True, True], add={0: dh_next})
    return dh, dg


def _mixer_ins(proj, lw):
    sgu = [_col_in(proj, CB_U, SG_HEADS), _col_in(proj, CB_V, SG_HEADS), _col_par(lw['sg_ln_g']),
           _col_par(lw['sg_ln_b']), _col_par(lw['sg_w']), _col_par(lw['sg_bb']), _col_par(lw['out_norm_a'])]
    sconv = [_col_in(proj, CB_GB, SC_GROUPS), _col_in(proj, CB_GC, SC_GROUPS), _col_in(proj, CB_X, SC_GROUPS)] + \
            [_col_par(w) for w in lw['sc_taps']] + [_col_par(lw['out_norm_b'])]
    gq = [_col_in(proj, CB_Q, GDN_HEADS)] + [_col_par(w) for w in lw['q_taps']]
    gk = [_col_in(proj, CB_K, GDN_HEADS)] + [_col_par(w) for w in lw['k_taps']]
    gv = [_col_in(proj, CB_VV, GDN_HEADS)] + [_col_par(w) for w in lw['v_taps']]
    return sgu, sconv, gq, gk, gv


def _gates_ins(proj, lw, tm):
    return [_row_in(proj, tm, CB_AB), _row_shared(lw['a_log_row']), _row_shared(lw['dt_bias_row'])]


def _layer_fwd(h, p_l, lw):
    t, d = h.shape
    dff = lw['w_ff2'].shape[0]
    xn = _rms_fwd("rms_fwd", h, lw['norm_mix'])
    proj = _matmul("proj_fwd", xn, lw['w_in'], "nn", t, PROJ_W, d, TM, 1024, d)
    sgu, sconv, gq, gk, gv = _mixer_ins(proj, lw)
    ya = _fused("sgu_fwd", _fn_sgu, SG_HEADS, sgu, [_col_out(t, SG_W, BF16)])[0]
    yb = _fused("sconv_fwd", _fn_sconv, SC_GROUPS, sconv, [_col_out(t, SC_W, BF16)])[0]
    q = _fused("gdn_q_fwd", _fn_gdn_qk, GDN_HEADS, gq, [_col_out(t, GDN_W, F32)])[0]
    k = _fused("gdn_k_fwd", _fn_gdn_qk, GDN_HEADS, gk, [_col_out(t, GDN_W, F32)])[0]
    v = _fused("gdn_v_fwd", _fn_gdn_v, GDN_HEADS, gv, [_col_out(t, GDN_W, F32)])[0]
    tm = min(TR, t)
    gates = _fused("gdn_gates_fwd", _fn_gdn_gates, t // tm, _gates_ins(proj, lw, tm),
                   [_row_out(t, LANES, F32, tm)])[0]
    yc = _gdn_core_fwd(q, k, v, gates, proj, lw['gdn_norm'])
    ycat = jnp.concatenate([ya, yb, yc], axis=1)
    h2 = _matmul("wo_fwd", ycat, lw['w_o'], "nn", t, d, d, TM, 1024, d,
                 epilogue=lambda acc, hb: (hb + acc,), extras=(h,))
    hn = _rms_fwd("rms_fwd", h2, lw['norm_ffn'])
    per = d // 1024
    s, r = _matmul("ff1_fwd", hn, lw['w_ff1'], "nn", t, dff, d, TM, 1024, d,
                   b_spec=pl.BlockSpec((None, d, 1024), lambda i, j, kk: (j // per, 0, j % per)),
                   epilogue=lambda acc: (jnp.maximum(acc, 0.0), jnp.square(jnp.maximum(acc, 0.0))),
                   outs=[((t, dff), BF16, (min(TM, t), 1024), lambda i, j, kk: (i, j))] * 2)
    h3 = _matmul("ff2_fwd", r, lw['w_ff2'], "nn", t, d, dff, TM, 1024, 2048,
                 epilogue=lambda acc, hb: (hb + acc,), extras=(h2,))
    hn2 = _rms_fwd("rms_fwd", h3, lw['norm_ple'])
    pp = _matmul("ple_proj_fwd", p_l, lw['w_ple_proj'], "nn", t, d, p_l.shape[1], TM, 1024, p_l.shape[1])

    def gate_epilogue(acc, hb, ppb):
        sg = _sigmoid(acc)
        return hb + ppb * sg, sg

    h4, gate = _matmul("ple_gate_fwd", hn2, lw['w_ple_gate'], "nn", t, d, d, TM, 1024, d, epilogue=gate_epilogue,
                       extras=(h3, pp), outs=[((t, d), F32, (min(TM, t), 1024), lambda i, j, kk: (i, j))] * 2)
    saved = dict(h=h, xn=xn, proj=proj, q=q, k=k, v=v, gates=gates, ycat=ycat, h2=h2, hn=hn, s=s, r=r, h3=h3,
                 hn2=hn2, pp=pp, gate=gate, p=p_l)
    return h4, saved


def _layer_bwd(dh4, sv, lw):
    t, d = dh4.shape
    dff = lw['w_ff2'].shape[0]
    tm = min(TR, t)
    g = {}
    dacc, dpp = _fused("ple_bwd_gate", lambda dh, pp, gt: (dh * pp * gt * (1.0 - gt), dh * gt), t // tm,
                       [_row_in(dh4, tm), _row_in(sv['pp'], tm), _row_in(sv['gate'], tm)],
                       [_row_out(t, d, BF16, tm)] * 2)
    g['w_ple_gate'] = _matmul("dw_ple_gate", sv['hn2'], dacc, "tn", d, d, t, TM, 1024, 1024)
    g['w_ple_proj'] = _matmul("dw_ple_proj", sv['p'], dpp, "tn", sv['p'].shape[1], d, t, TM, 1024, 1024)
    dhn2 = _matmul("dx_ple_gate", dacc, lw['w_ple_gate'], "nt", t, d, d, TM, 1024, d)
    dh3, g['norm_ple'] = _rms_bwd("rms_bwd", sv['h3'], lw['norm_ple'], dhn2, dh4)
    da = _matmul("dx_ff2", dh3, lw['w_ff2'], "nt", t, dff, d, TM, 1024, d,
                 epilogue=lambda acc, sb: (acc * (2.0 * sb.astype(F32)),), extras=(sv['s'],),
                 outs=[((t, dff), BF16, (min(TM, t), 1024), lambda i, j, kk: (i, j))])
    g['w_ff2'] = _matmul("dw_ff2", sv['r'], dh3, "tn", dff, d, t, TM, 1024, 1024)
    per = d // 1024
    dhn = _matmul("dx_ff1", da, lw['w_ff1'], "nt", t, d, dff, TM, 1024, 1024,
                  b_spec=pl.BlockSpec((None, 1024, 1024), lambda i, j, kk: (kk // per, j, kk % per)))
    g['w_ff1'] = _matmul("dw_ff1", sv['hn'], da, "tn", d, dff, t, TM, 1024, 1024,
                         outs=[((N_SHARD, d, d), F32, (None, TM, 1024), lambda i, j, kk: (j // per, i, j % per))])
    dh2, g['norm_ffn'] = _rms_bwd("rms_bwd", sv['h2'], lw['norm_ffn'], dhn, dh3)
    dycat = _matmul("dx_o", dh2, lw['w_o'], "nt", t, d, d, TM, 1024, d)
    g['w_o'] = _matmul("dw_o", sv['ycat'], dh2, "tn", d, d, t, TM, 1024, 1024)
    proj = sv['proj']
    sgu, sconv, gq, gk, gv = _mixer_ins(proj, lw)
    bf2 = [BF16, BF16]
    r_ = _fused("sgu_bwd", _fn_sgu, SG_HEADS, sgu, [_col_out(t, d, F32, 0)], douts=[dycat], need=[True] * 7,
                gdt=bf2 + [F32] * 5)
    du, dv_, g['sg_ln_g'], g['sg_ln_b'], g['sg_w'], g['sg_bb'], g['out_norm_a'] = r_
    r_ = _fused("sconv_bwd", _fn_sconv, SC_GROUPS, sconv, [_col_out(t, d, F32, SG_HEADS)], douts=[dycat],
                need=[True] * 7, gdt=[BF16] * 3 + [F32] * 4)
    dgb, dgc, dxin = r_[:3]
    g['sc_taps'], g['out_norm_b'] = r_[3:6], r_[6]
    dq, dk, dvv, dgates, dz, g['gdn_norm'] = _gdn_core_bwd(sv['q'], sv['k'], sv['v'], sv['gates'], proj,
                                                           lw['gdn_norm'], dycat, SG_HEADS + SC_GROUPS)
    one = [_col_out(t, GDN_W, F32)]
    r_ = _fused("gdn_q_bwd", _fn_gdn_qk, GDN_HEADS, gq, one, douts=[dq], need=[True] * 5, gdt=[BF16] + [F32] * 4)
    dpq, g['q_taps'] = r_[0], r_[1:]
    r_ = _fused("gdn_k_bwd", _fn_gdn_qk, GDN_HEADS, gk, one, douts=[dk], need=[True] * 5, gdt=[BF16] + [F32] * 4)
    dpk, g['k_taps'] = r_[0], r_[1:]
    r_ = _fused("gdn_v_bwd", _fn_gdn_v, GDN_HEADS, gv, one, douts=[dvv], need=[True] * 5, gdt=[BF16] + [F32] * 4)
    dpv, g['v_taps'] = r_[0], r_[1:]
    dab, g['a_log_row'], g['dt_bias_row'] = _fused(
        "gdn_gates_bwd", _fn_gdn_gates, t // tm, _gates_ins(proj, lw, tm), [_row_out(t, LANES, F32, tm)],
        douts=[dgates], need=[True] * 3, gdt=[BF16, F32, F32])
    pad = jnp.zeros((t, PROJ_W - (CB_AB + 1) * LANES), BF16)
    dproj = jnp.concatenate([du, dv_, dgb, dgc, dxin, dpq, dpk, dpv, dz, dab, pad], axis=1)
    dxn = _matmul("dx_in", dproj, lw['w_in'], "nt", t, d, PROJ_W, TM, 1024, 1792)
    g['w_in'] = _matmul("dw_in", sv['xn'], dproj, "tn", d, PROJ_W, t, TM, 1024, 1024)
    dh, g['norm_mix'] = _rms_bwd("rms_bwd", sv['h'], lw['norm_mix'], dxn, dh2)
    return dh, g


def _loss_grad(h, g, tgt):
    t, d = h.shape
    tm = min(TR, t)

    def body(h_ref, g_ref, t_ref, loss_ref, dh_ref, dg_ref):
        y, vjp = jax.vjp(_rms, h_ref[...], g_ref[...])
        e = y - t_ref[...]
        dh, dg = vjp(e * (1.0 / d))

        @pl.when(pl.program_id(0) == 0)
        def _():
            loss_ref[...] = jnp.zeros_like(loss_ref)
            dg_ref[...] = jnp.zeros_like(dg_ref)

        loss_ref[...] += jnp.sum(jnp.sum(e * e, axis=1, keepdims=True), axis=0, keepdims=True) * (0.5 / d)
        dh_ref[...] = dh
        dg_ref[...] += dg

    row = pl.BlockSpec((tm, d), lambda i: (i, 0))
    return pl.pallas_call(
        body, name="loss_grad", grid=(t // tm,),
        in_specs=[row, pl.BlockSpec((1, d), lambda i: (0, 0)), row],
        out_specs=[pl.BlockSpec((1, LANES), lambda i: (0, 0)), row, pl.BlockSpec((1, d), lambda i: (0, 0))],
        out_shape=[jax.ShapeDtypeStruct((1, LANES), F32), jax.ShapeDtypeStruct((t, d), F32),
                   jax.ShapeDtypeStruct((1, d), F32)],
        compiler_params=_cparams(("arbitrary",)),
    )(h, g, tgt)


def _layer_weights(l, full, small):
    d = full['w_o'].shape[-1]
    lw = {}
    w_in = jnp.transpose(full['w_in'][l], (1, 0, 2)).reshape(d, IN_COLS)
    lw['w_in'] = jnp.pad(w_in, ((0, 0), (0, PROJ_W - IN_COLS)))
    lw['w_o'] = full['w_o'][l].reshape(d, d)
    lw['w_ff1'] = full['w_ff1'][l]
    lw['w_ff2'] = full['w_ff2'][l].reshape(-1, d)
    lw['w_ple_gate'] = full['w_ple_gate'][l].reshape(d, d)
    wpp = full['w_ple_proj'][l]
    lw['w_ple_proj'] = jnp.transpose(wpp, (1, 0, 2)).reshape(wpp.shape[1], d)
    for n in ('norm_mix', 'norm_ffn', 'norm_ple'):
        lw[n] = small[n][l].reshape(1, d)
    lw['sg_ln_g'] = small['sg_ln_g'][l].reshape(SG_HEADS, 1, LANES)
    lw['sg_ln_b'] = small['sg_ln_b'][l].reshape(SG_HEADS, 1, LANES)
    lw['sg_w'] = small['sg_w'][l]
    lw['sg_bb'] = jnp.broadcast_to(small['sg_b'][l][:, :, None], (SG_HEADS, SG_CHUNK, LANES))
    lw['out_norm_a'] = small['out_norm_a'][l].reshape(SG_HEADS, 1, LANES)
    lw['out_norm_b'] = small['out_norm_b'][l].reshape(SC_GROUPS, 1, LANES)
    lw['gdn_norm'] = jnp.broadcast_to(small['gdn_norm'][l].reshape(1, 1, LANES), (GDN_HEADS, 1, LANES))
    lw['a_log_row'] = jnp.pad(small['gdn_a_log'][l].reshape(1, GDN_HEADS), ((0, 0), (0, LANES - GDN_HEADS)))
    lw['dt_bias_row'] = jnp.pad(small['gdn_dt_bias'][l].reshape(1, GDN_HEADS), ((0, 0), (0, LANES - GDN_HEADS)))
    sc = full['sc_conv'][l]
    lw['sc_taps'] = [sc[:, j:j + 1, :] for j in range(SC_KERNEL)]
    gc = jnp.transpose(full['gdn_conv'][l], (1, 0, 2)).reshape(GDN_CONV, 3 * GDN_W)
    for i, nm in enumerate(('q_taps', 'k_taps', 'v_taps')):
        part = gc[:, i * GDN_W:(i + 1) * GDN_W].reshape(GDN_CONV, GDN_HEADS, 1, LANES)
        lw[nm] = [part[j] for j in range(GDN_CONV)]
    return lw


def _weight_grads(g):
    d = g['w_o'].shape[0]
    out = {}
    w_in = g['w_in'][:, :IN_COLS].reshape(d, N_SHARD, IN_COLS // N_SHARD)
    out['w_in'] = jnp.transpose(w_in, (1, 0, 2))
    out['w_o'] = g['w_o'].reshape(N_SHARD, d // N_SHARD, d)
    out['w_ff1'] = g['w_ff1']
    out['w_ff2'] = g['w_ff2'].reshape(N_SHARD, -1, d)
    out['w_ple_gate'] = g['w_ple_gate'].reshape(N_SHARD, d // N_SHARD, d)
    wpp = g['w_ple_proj']
    out['w_ple_proj'] = jnp.transpose(wpp.reshape(wpp.shape[0], N_SHARD, d // N_SHARD), (1, 0, 2))
    for n in ('norm_mix', 'norm_ffn', 'norm_ple'):
        out[n] = g[n].reshape(d)
    out['sg_ln_g'] = g['sg_ln_g'].reshape(SG_W)
    out['sg_ln_b'] = g['sg_ln_b'].reshape(SG_W)
    out['sg_w'] = g['sg_w']
    out['sg_b'] = jnp.sum(g['sg_bb'], axis=2)
    out['out_norm_a'] = g['out_norm_a'].reshape(SG_W)
    out['out_norm_b'] = g['out_norm_b'].reshape(SC_W)
    out['gdn_norm'] = jnp.sum(g['gdn_norm'], axis=(0, 1))
    out['gdn_a_log'] = g['a_log_row'][0, :GDN_HEADS]
    out['gdn_dt_bias'] = g['dt_bias_row'][0, :GDN_HEADS]
    out['sc_conv'] = jnp.concatenate([w.reshape(1, SC_W) for w in g['sc_taps']], axis=0)
    taps = [jnp.concatenate([g[nm][j].reshape(1, GDN_W) for nm in ('q_taps', 'k_taps', 'v_taps')], axis=1)
            for j in range(GDN_CONV)]
    out['gdn_conv'] = jnp.concatenate(taps, axis=0)
    return out


def _local_step(x, p, tgt, full, small):
    h = x
    saved, lws = [], []
    for l in range(DEPTH):
        lw = _layer_weights(l, full, small)
        h, sv = _layer_fwd(h, p[l], lw)
        saved.append(sv)
        lws.append(lw)
    loss, dh, dnf = _loss_grad(h, small['norm_final'].reshape(1, -1), tgt)
    per_layer = [None] * DEPTH
    for l in reversed(range(DEPTH)):
        dh, g = _layer_bwd(dh, saved[l], lws[l])
        per_layer[l] = _weight_grads(g)
    grads = {n: jnp.stack([per_layer[l][n] for l in range(DEPTH)], axis=0) for n in per_layer[0]}
    grads['norm_final'] = dnf.reshape(-1)
    return loss, dh, grads


ANY = pl.BlockSpec(memory_space=pl.ANY)


def _place():
    x, y, c = lax.axis_index("x"), lax.axis_index("y"), lax.axis_index("c")
    chips = [(1 - x, y), (x, 1 - y), (1 - x, 1 - y)]
    return x, y, c, chips


def _all_gather(shards):
    n = len(shards)

    def body(*refs):
        srcs, outs = refs[:n], refs[n:2 * n]
        send_sems, recv_sems, local_sems = refs[2 * n:]
        x, y, c, chips = _place()
        me = 2 * x + y
        sibling = (x, y, 1 - c)

        def cp(w, k, shard, layer, to, own=False):
            return pltpu.make_async_remote_copy(
                src_ref=srcs[w].at[layer] if own else outs[w].at[layer, shard], dst_ref=outs[w].at[layer, shard],
                send_sem=send_sems.at[w, k], recv_sem=recv_sems.at[w, k], device_id=to, device_id_type=MESH)

        started = []
        for w in range(n):
            for l in range(DEPTH):
                lc = pltpu.make_async_copy(srcs[w].at[l], outs[w].at[l, me], local_sems.at[w, l])
                lc.start()
                started.append(lc)
        sends = []
        for w in range(n):
            for j, chip in enumerate(chips):
                s = cp(w, j, me, c, (*chip, c), own=True)
                s.start()
                sends.append(s)
        for w in range(n):
            for j, (px, py) in enumerate(chips):
                cp(w, j, 2 * px + py, c, sibling).wait_recv()
                s = cp(w, 3 + j, 2 * px + py, c, sibling)
                s.start()
                sends.append(s)
        for w in range(n):
            for j, (px, py) in enumerate(chips):
                cp(w, 3 + j, 2 * px + py, 1 - c, sibling).wait_recv()
        for s in sends:
            s.wait_send()
        for lc in started:
            lc.wait()

    return pl.pallas_call(
        body, name="all_gather",
        in_specs=[ANY] * n, out_specs=[ANY] * n,
        out_shape=[jax.ShapeDtypeStruct((DEPTH, N_SHARD) + s.shape[1:], s.dtype) for s in shards],
        scratch_shapes=[pltpu.SemaphoreType.DMA((n, 6)), pltpu.SemaphoreType.DMA((n, 6)),
                        pltpu.SemaphoreType.DMA((n, DEPTH))],
    )(*shards)


def _swap_layers(grads):
    n = len(grads)

    def body(*refs):
        srcs, outs = refs[:n], refs[n:2 * n]
        send_sems, recv_sems = refs[2 * n:]
        x, y, c, _ = _place()
        cps = [pltpu.make_async_remote_copy(src_ref=srcs[w].at[1 - c], dst_ref=outs[w], send_sem=send_sems.at[w],
                                            recv_sem=recv_sems.at[w], device_id=(x, y, 1 - c), device_id_type=MESH)
               for w in range(n)]
        for cpy in cps:
            cpy.start()
        for cpy in cps:
            cpy.wait()

    return pl.pallas_call(
        body, name="rs_swap_layers", in_specs=[ANY] * n, out_specs=[ANY] * n,
        out_shape=[jax.ShapeDtypeStruct(g.shape[1:], g.dtype) for g in grads],
        scratch_shapes=[pltpu.SemaphoreType.DMA((n,)), pltpu.SemaphoreType.DMA((n,))],
    )(*grads)


def _scatter_shards(parts):
    n = len(parts)

    def body(*refs):
        srcs, outs = refs[:n], refs[n:2 * n]
        send_sems, recv_sems, local_sems = refs[2 * n:]
        x, y, c, chips = _place()
        me = 2 * x + y
        local = [pltpu.make_async_copy(srcs[w].at[me], outs[w].at[me], local_sems.at[w]) for w in range(n)]
        for lc in local:
            lc.start()

        def cp(w, j, src_shard, dst_slot, to):
            return pltpu.make_async_remote_copy(
                src_ref=srcs[w].at[src_shard], dst_ref=outs[w].at[dst_slot], send_sem=send_sems.at[w, j],
                recv_sem=recv_sems.at[w, j], device_id=to, device_id_type=MESH)

        sends = [cp(w, j, 2 * px + py, me, (px, py, c)) for w in range(n) for j, (px, py) in enumerate(chips)]
        for s in sends:
            s.start()
        for w in range(n):
            for j, (px, py) in enumerate(chips):
                cp(w, j, me, 2 * px + py, (px, py, c)).wait_recv()
        for s in sends:
            s.wait_send()
        for lc in local:
            lc.wait()

    return pl.pallas_call(
        body, name="rs_scatter", in_specs=[ANY] * n, out_specs=[ANY] * n,
        out_shape=[jax.ShapeDtypeStruct(g.shape, g.dtype) for g in parts],
        scratch_shapes=[pltpu.SemaphoreType.DMA((n, 3)), pltpu.SemaphoreType.DMA((n, 3)),
                        pltpu.SemaphoreType.DMA((n,))],
    )(*parts)


def _join_layers(reduced):
    n = len(reduced)

    def body(*refs):
        srcs, outs = refs[:n], refs[n:2 * n]
        send_sems, recv_sems, local_sems = refs[2 * n:]
        x, y, c, _ = _place()
        local = [pltpu.make_async_copy(srcs[w], outs[w].at[c], local_sems.at[w]) for w in range(n)]
        for lc in local:
            lc.start()
        sends = [pltpu.make_async_remote_copy(src_ref=srcs[w], dst_ref=outs[w].at[c], send_sem=send_sems.at[w],
                                              recv_sem=recv_sems.at[w], device_id=(x, y, 1 - c), device_id_type=MESH)
                 for w in range(n)]
        for s in sends:
            s.start()
        for w in range(n):
            pltpu.make_async_remote_copy(src_ref=srcs[w], dst_ref=outs[w].at[1 - c], send_sem=send_sems.at[w],
                                         recv_sem=recv_sems.at[w], device_id=(x, y, 1 - c),
                                         device_id_type=MESH).wait_recv()
        for s in sends:
            s.wait_send()
        for lc in local:
            lc.wait()

    return pl.pallas_call(
        body, name="rs_join_layers", in_specs=[ANY] * n, out_specs=[ANY] * n,
        out_shape=[jax.ShapeDtypeStruct((DEPTH,) + g.shape, g.dtype) for g in reduced],
        scratch_shapes=[pltpu.SemaphoreType.DMA((n,)), pltpu.SemaphoreType.DMA((n,)),
                        pltpu.SemaphoreType.DMA((n,))],
    )(*reduced)


def _exchange_small(vec):
    def body(src, out, send_sems, recv_sems, local_sem):
        x, y, c, _ = _place()
        me = 4 * x + 2 * y + c
        lc = pltpu.make_async_copy(src, out.at[me], local_sem)
        lc.start()
        sends = []
        for k in range(1, N_DEV):
            fx, fy, fc = (k >> 2) & 1, (k >> 1) & 1, k & 1
            to = (x ^ fx, y ^ fy, c ^ fc)
            s = pltpu.make_async_remote_copy(src_ref=src, dst_ref=out.at[me], send_sem=send_sems.at[k - 1],
                                             recv_sem=recv_sems.at[k - 1], device_id=to, device_id_type=MESH)
            s.start()
            sends.append(s)
        for k in range(1, N_DEV):
            fx, fy, fc = (k >> 2) & 1, (k >> 1) & 1, k & 1
            frm = 4 * (x ^ fx) + 2 * (y ^ fy) + (c ^ fc)
            pltpu.make_async_remote_copy(src_ref=src, dst_ref=out.at[frm], send_sem=send_sems.at[k - 1],
                                         recv_sem=recv_sems.at[k - 1], device_id=(x ^ fx, y ^ fy, c ^ fc),
                                         device_id_type=MESH).wait_recv()
        for s in sends:
            s.wait_send()
        lc.wait()

    return pl.pallas_call(
        body, name="small_exchange", in_specs=[ANY], out_specs=ANY,
        out_shape=jax.ShapeDtypeStruct((N_DEV,) + vec.shape, vec.dtype),
        scratch_shapes=[pltpu.SemaphoreType.DMA((N_DEV - 1,)), pltpu.SemaphoreType.DMA((N_DEV - 1,)),
                        pltpu.SemaphoreType.DMA],
    )(vec)


def _add_own_layer(g, other, c_arr):
    _, ns, r, cc = g.shape
    tr = min(TR, r)
    nb = r // tr

    def body(c_ref, g_ref, o_ref, out_ref):
        out_ref[...] = g_ref[...] + o_ref[...]

    slab = pl.BlockSpec((None, tr, cc), lambda i, cr: (i // nb, i % nb, 0))
    return pl.pallas_call(
        body, name="rs_add_own_layer",
        grid_spec=pltpu.PrefetchScalarGridSpec(
            num_scalar_prefetch=1, grid=(ns * nb,),
            in_specs=[pl.BlockSpec((None, None, tr, cc), lambda i, cr: (cr[0], i // nb, i % nb, 0)), slab],
            out_specs=slab),
        out_shape=jax.ShapeDtypeStruct(other.shape, F32),
        compiler_params=_cparams(("arbitrary",)),
    )(c_arr, g, other)


def _sum_slots(name, a):
    ns, r, cc = a.shape
    tr = min(TR, r)
    ins = [_In(a, (None, tr, cc), functools.partial(lambda s, i: (s, i, 0), s)) for s in range(ns)]

    def fn(*blocks):
        acc = blocks[0]
        for b in blocks[1:]:
            acc = acc + b
        return (acc,)

    return _fused(name, fn, r // tr, ins, [((r, cc), F32, (tr, cc), lambda i: (i, 0))])[0]


def _adamw_fn(w, g, m, v):
    m = ADAM_B1 * m + (1.0 - ADAM_B1) * g
    v = ADAM_B2 * v + (1.0 - ADAM_B2) * jnp.square(g)
    m_hat = m / (1.0 - ADAM_B1 ** ADAM_STEP)
    v_hat = v / (1.0 - ADAM_B2 ** ADAM_STEP)
    delta = -ADAM_LR * (m_hat / (jnp.sqrt(v_hat) + ADAM_EPS) + ADAM_WD * w)
    return delta, m, v


def _adamw(w, g, m, v):
    r, cc = w.shape
    tr = min(TR, r)
    ins = [_In(a, (tr, cc), lambda i: (i, 0)) for a in (w, g, m, v)]
    return _fused("adamw", _adamw_fn, r // tr, ins, [((r, cc), F32, (tr, cc), lambda i: (i, 0))] * 3)


def _pack(arrs):
    flat = jnp.concatenate([a.reshape(-1) for a in arrs])
    tile = TR * LANES
    n = -(-flat.shape[0] // tile) * tile
    return jnp.pad(flat, (0, n - flat.shape[0])).reshape(-1, LANES)


def _unpack(vec, shapes):
    flat = vec.reshape(-1)
    out, o = [], 0
    for s in shapes:
        n = math.prod(s)
        out.append(flat[o:o + n].reshape(s))
        o += n
    return out


def kernel(x, p, norm_mix, w_in, sg_ln_g, sg_ln_b, sg_w, sg_b, sc_conv, gdn_conv, gdn_a_log, gdn_dt_bias, gdn_norm, out_norm_a, out_norm_b, w_o, norm_ffn, w_ff1, w_ff2, norm_ple, w_ple_gate, w_ple_proj, norm_final, loss_target, m_norm_mix, m_w_in, m_sg_ln_g, m_sg_ln_b, m_sg_w, m_sg_b, m_sc_conv, m_gdn_conv, m_gdn_a_log, m_gdn_dt_bias, m_gdn_norm, m_out_norm_a, m_out_norm_b, m_w_o, m_norm_ffn, m_w_ff1, m_w_ff2, m_norm_ple, m_w_ple_gate, m_w_ple_proj, m_norm_final, v_norm_mix, v_w_in, v_sg_ln_g, v_sg_ln_b, v_sg_w, v_sg_b, v_sc_conv, v_gdn_conv, v_gdn_a_log, v_gdn_dt_bias, v_gdn_norm, v_out_norm_a, v_out_norm_b, v_w_o, v_norm_ffn, v_w_ff1, v_w_ff2, v_norm_ple, v_w_ple_gate, v_w_ple_proj, v_norm_final):
    given = dict(locals())
    w = {n: given[n] for n in WEIGHTS}
    m = {n: given['m_' + n] for n in WEIGHTS}
    v = {n: given['v_' + n] for n in WEIGHTS}
    shard = 2 * lax.axis_index("x") + lax.axis_index("y")
    core = lax.axis_index("c")

    gathered = _all_gather([w[n].astype(BF16) for n in BIG] + [w[n] for n in CONVS])
    full = dict(zip(BIG + CONVS, gathered))
    small = {n: w[n] for n in SMALL if n not in CONVS}

    loss, grad_x, grads = _local_step(x[0], p[:, 0], loss_target[0], full, small)

    big = [grads[n] for n in BIG]
    theirs = _swap_layers(big)
    c_arr = core.reshape(1).astype(jnp.int32)
    parts = [_add_own_layer(g, o, c_arr) for g, o in zip(big, theirs)]
    slots = _scatter_shards(parts)
    reduced = [_sum_slots("rs_sum_shards", s) for s in slots]
    g_big = dict(zip(BIG, _join_layers(reduced)))

    rep = [n for n in SMALL if n not in CONVS]
    names = rep + CONVS
    vec = _pack([grads[n] for n in names] + [loss[0, :1]])
    total = _sum_slots("small_sum", _exchange_small(vec))
    parts_small = _unpack(total, [grads[n].shape for n in names] + [(1,)])
    g_small = dict(zip(names, parts_small[:-1]))
    loss_out = parts_small[-1].reshape(())
    for n in CONVS:
        width = w[n].shape[-1]
        g_small[n] = lax.dynamic_slice_in_dim(g_small[n], shard * width, width, axis=2)

    delta, new_m, new_v, grad_w = {}, {}, {}, {}
    for n in BIG:
        shp = w[n].shape
        two_d = lambda a: a.reshape(-1, shp[-1])
        d_, m_, v_ = _adamw(two_d(w[n]), two_d(g_big[n]), two_d(m[n]), two_d(v[n]))
        delta[n], new_m[n], new_v[n], grad_w[n] = d_.reshape(shp), m_.reshape(shp), v_.reshape(shp), g_big[n]
    shapes = [w[n].shape for n in SMALL]
    d_, m_, v_ = _adamw(_pack([w[n] for n in SMALL]), _pack([g_small[n] for n in SMALL]),
                        _pack([m[n] for n in SMALL]), _pack([v[n] for n in SMALL]))
    for n, dd, mm, vv in zip(SMALL, _unpack(d_, shapes), _unpack(m_, shapes), _unpack(v_, shapes)):
        delta[n], new_m[n], new_v[n], grad_w[n] = dd, mm, vv, g_small[n]

    return (loss_out, grad_x[None], *[grad_w[n] for n in WEIGHTS], *[delta[n] for n in WEIGHTS],
            *[new_m[n] for n in WEIGHTS], *[new_v[n] for n in WEIGHTS])
```

```python
import functools
import math

import jax
import jax.numpy as jnp
from jax import lax
from jax.experimental import pallas as pl
from jax.experimental.pallas import tpu as pltpu

F32 = jnp.float32
BF16 = jnp.bfloat16
HI = lax.Precision.HIGHEST
MESH = pl.DeviceIdType.MESH

LANES = 128
EPS = 1e-6
SG_HEADS, SG_CHUNK = 4, 128
SC_GROUPS, SC_KERNEL = 4, 3
GDN_HEADS, GDN_CONV = 8, 4
GDN_L = 128
GDN_CPS = 2
SG_W = SG_HEADS * LANES
SC_W = SC_GROUPS * LANES
GDN_W = GDN_HEADS * LANES
IN_COLS = 2 * SG_W + 3 * SC_W + 4 * GDN_W + 2 * GDN_HEADS
PROJ_W = 7168
CB_U, CB_V = 0, 4
CB_GB, CB_GC, CB_X = 8, 12, 16
CB_Q, CB_K, CB_VV, CB_Z, CB_AB = 20, 28, 36, 44, 52
N_SHARD = 4
N_DEV = 8
DEPTH = 2

ADAM_LR, ADAM_B1, ADAM_B2, ADAM_EPS, ADAM_WD, ADAM_STEP = 0.001, 0.9, 0.999, 1e-08, 0.01, 10

VMEM_LIMIT = 56 << 20

WEIGHTS = ['norm_mix', 'w_in', 'sg_ln_g', 'sg_ln_b', 'sg_w', 'sg_b', 'sc_conv', 'gdn_conv', 'gdn_a_log',
           'gdn_dt_bias', 'gdn_norm', 'out_norm_a', 'out_norm_b', 'w_o', 'norm_ffn', 'w_ff1', 'w_ff2', 'norm_ple',
           'w_ple_gate', 'w_ple_proj', 'norm_final']
BIG = ['w_in', 'w_o', 'w_ff1', 'w_ff2', 'w_ple_gate', 'w_ple_proj']
CONVS = ['sc_conv', 'gdn_conv']
SMALL = [n for n in WEIGHTS if n not in BIG]


def _cparams(sem=None):
    return pltpu.CompilerParams(dimension_semantics=sem, vmem_limit_bytes=VMEM_LIMIT)


def _dot(a, b, dims, prec=None):
    return lax.dot_general(a, b, (dims, ((), ())), precision=prec, preferred_element_type=F32)


def _mm(a, b):
    return _dot(a, b, ((1,), (0,)), HI)


def _mm_nt(a, b):
    return _dot(a, b, ((1,), (1,)), HI)


def _mm_tn(a, b):
    return _dot(a, b, ((0,), (0,)), HI)


@jax.custom_vjp
def _bmm(a, b):
    return _dot(a.astype(BF16), b.astype(BF16), ((1,), (0,)))


def _bmm_fwd(a, b):
    return _bmm(a, b), (a, b)


def _bmm_bwd(res, g):
    a, b = res
    gb = g.astype(BF16)
    return _dot(gb, b.astype(BF16), ((1,), (1,))), _dot(a.astype(BF16), gb, ((0,), (0,)))


_bmm.defvjp(_bmm_fwd, _bmm_bwd)


@jax.custom_vjp
def _bmm_tn(a, b):
    return _dot(a.astype(BF16), b.astype(BF16), ((0,), (0,)))


def _bmm_tn_fwd(a, b):
    return _bmm_tn(a, b), (a, b)


def _bmm_tn_bwd(res, g):
    a, b = res
    gb = g.astype(BF16)
    return _dot(b.astype(BF16), gb, ((1,), (1,))), _dot(a.astype(BF16), gb, ((1,), (0,)))


_bmm_tn.defvjp(_bmm_tn_fwd, _bmm_tn_bwd)


def _sigmoid(x):
    return 1.0 / (1.0 + jnp.exp(-x))


def _silu(x):
    return x * _sigmoid(x)


def _gelu(x):
    c = math.sqrt(2.0 / math.pi)
    return 0.5 * x * (1.0 + jnp.tanh(c * (x + 0.044715 * (x * x * x))))


def _softplus(x):
    return jnp.maximum(x, 0.0) + jnp.log(1.0 + jnp.exp(-jnp.abs(x)))


def _rms(x, g):
    return x * lax.rsqrt(jnp.mean(x * x, axis=-1, keepdims=True) + EPS) * g


def _roll_rows(x, shift):
    return pltpu.roll(x, shift % x.shape[0], 0)


@functools.partial(jax.custom_vjp, nondiff_argnums=(1,))
def _shift_down(x, j):
    row = lax.broadcasted_iota(jnp.int32, x.shape, 0)
    return jnp.where(row >= j, _roll_rows(x, j), 0.0)


def _shift_down_fwd(x, j):
    return _shift_down(x, j), None


def _shift_down_bwd(j, _, dy):
    row = lax.broadcasted_iota(jnp.int32, dy.shape, 0)
    return (jnp.where(row < dy.shape[0] - j, _roll_rows(dy, -j), 0.0),)


_shift_down.defvjp(_shift_down_fwd, _shift_down_bwd)


def _causal_conv(x, taps):
    k = len(taps)
    y = taps[k - 1] * x
    for j in range(k - 1):
        y = y + taps[j] * _shift_down(x, k - 1 - j)
    return y


class _In:
    def __init__(self, arr, block, imap, shared=False, gshape=None, gmap=None):
        self.arr, self.block, self.imap, self.shared = arr, block, imap, shared
        self.gshape = arr.shape if gshape is None else gshape
        self.gmap = imap if gmap is None else gmap


def _fused(name, fn, n, ins, outs, douts=None, need=None, gdt=None, add=None):
    n_in, n_out = len(ins), len(outs)
    in_specs = [pl.BlockSpec(s.block, s.imap) for s in ins]
    out_specs = [pl.BlockSpec(bs, im) for _, _, bs, im in outs]
    if douts is None:
        def body(*refs):
            res = fn(*[r[...] for r in refs[:n_in]])
            for r, v in zip(refs[n_in:], res):
                r[...] = v.astype(r.dtype)

        return pl.pallas_call(
            body, name=name, grid=(n,), in_specs=in_specs, out_specs=out_specs,
            out_shape=[jax.ShapeDtypeStruct(s, d) for s, d, _, _ in outs],
            compiler_params=_cparams(("arbitrary",)),
        )(*[s.arr for s in ins])

    gdt = list(gdt) if gdt is not None else [F32] * n_in
    add = dict(add or {})
    gidx = [i for i in range(n_in) if need[i]]
    aidx = [i for i in gidx if i in add]

    def body(*refs):
        in_refs, d_refs = refs[:n_in], refs[n_in:n_in + n_out]
        a_refs = dict(zip(aidx, refs[n_in + n_out:n_in + n_out + len(aidx)]))
        g_refs = refs[n_in + n_out + len(aidx):]
        vals = [r[...] for r in in_refs]

        def f(*dv):
            full = list(vals)
            for i, v in zip(gidx, dv):
                full[i] = v
            return tuple(o.astype(F32) for o in fn(*full))

        _, vjp = jax.vjp(f, *[vals[i].astype(F32) for i in gidx])
        grads = vjp(tuple(r[...].astype(F32) for r in d_refs))
        for i, g_ref, g in zip(gidx, g_refs, grads):
            if i in a_refs:
                g = g + a_refs[i][...].astype(F32)
            if ins[i].shared:
                period = n if ins[i].shared is True else ins[i].shared

                @pl.when(pl.program_id(0) % period == 0)
                def _():
                    g_ref[...] = jnp.zeros_like(g_ref)
                g_ref[...] += g.astype(g_ref.dtype)
            else:
                g_ref[...] = g.astype(g_ref.dtype)

    g_specs = [pl.BlockSpec(ins[i].block, ins[i].gmap) for i in gidx]
    g_shape = [jax.ShapeDtypeStruct(ins[i].gshape, gdt[i]) for i in gidx]
    res = pl.pallas_call(
        body, name=name, grid=(n,), in_specs=in_specs + out_specs + [g_specs[gidx.index(i)] for i in aidx],
        out_specs=g_specs, out_shape=g_shape,
        compiler_params=_cparams(("arbitrary",)),
    )(*[s.arr for s in ins], *douts, *[add[i] for i in aidx])
    full = [None] * n_in
    for i, g in zip(gidx, res):
        full[i] = g
    return full


def _row_in(a, tm, cb=None):
    if cb is None:
        return _In(a, (tm, a.shape[1]), lambda i: (i, 0))
    return _In(a, (tm, LANES), lambda i: (i, cb), gshape=(a.shape[0], LANES), gmap=lambda i: (i, 0))


def _row_shared(a):
    return _In(a, a.shape, lambda i: (0, 0), shared=True)


def _row_out(t, w, dt, tm):
    return ((t, w), dt, (tm, w), lambda i: (i, 0))


def _col_in(a, base, nblk):
    t = a.shape[0]
    return _In(a, (t, LANES), lambda i: (0, base + i), gshape=(t, nblk * LANES), gmap=lambda i: (0, i))


def _col_par(a):
    return _In(a, (None,) + a.shape[1:], lambda i: (i, 0, 0))


def _col_out(t, w, dt, base=0):
    return ((t, w), dt, (t, LANES), lambda i: (0, base + i))


def _matmul(name, a, b, mode, m, n, k, tm, tn, tk, b_spec=None, epilogue=None, extras=(), outs=None, out_dtype=F32):
    tm, tn, tk = min(tm, m), min(tn, n), min(tk, k)
    assert m % tm == 0 and n % tn == 0 and k % tk == 0, (name, m, n, k, tm, tn, tk)
    nk = k // tk
    a_spec = (pl.BlockSpec((tk, tm), lambda i, j, kk: (kk, i)) if mode == "tn"
              else pl.BlockSpec((tm, tk), lambda i, j, kk: (i, kk)))
    if b_spec is None:
        b_spec = (pl.BlockSpec((tn, tk), lambda i, j, kk: (j, kk)) if mode == "nt"
                  else pl.BlockSpec((tk, tn), lambda i, j, kk: (kk, j)))
    dims = {"nn": ((1,), (0,)), "nt": ((1,), (1,)), "tn": ((0,), (0,))}[mode]
    if outs is None:
        outs = [((m, n), out_dtype, (tm, tn), lambda i, j, kk: (i, j))]
    if epilogue is None:
        epilogue = lambda acc: (acc,)
    n_ex = len(extras)

    def body(*refs):
        a_ref, b_ref = refs[0], refs[1]
        ex_refs = refs[2:2 + n_ex]
        o_refs = refs[2 + n_ex:2 + n_ex + len(outs)]
        part = _dot(a_ref[...].astype(BF16), b_ref[...].astype(BF16), dims)

        def finish(acc):
            for r, v in zip(o_refs, epilogue(acc, *[e[...] for e in ex_refs])):
                r[...] = v.astype(r.dtype)

        if nk == 1:
            finish(part)
        else:
            acc_ref = refs[-1]
            kk = pl.program_id(2)

            @pl.when(kk == 0)
            def _():
                acc_ref[...] = part

            @pl.when(kk > 0)
            def _():
                acc_ref[...] += part

            @pl.when(kk == nk - 1)
            def _():
                finish(acc_ref[...])

    ex_specs = [pl.BlockSpec((tm, tn), lambda i, j, kk: (i, j)) for _ in extras]
    res = pl.pallas_call(
        body, name=name, grid=(m // tm, n // tn, nk),
        in_specs=[a_spec, b_spec] + ex_specs,
        out_specs=[pl.BlockSpec(bs, im) for _, _, bs, im in outs],
        out_shape=[jax.ShapeDtypeStruct(s, d) for s, d, _, _ in outs],
        scratch_shapes=[pltpu.VMEM((tm, tn), F32)] if nk > 1 else [],
        compiler_params=_cparams(("parallel", "parallel", "arbitrary")),
    )(a, b, *extras)
    return res if len(res) > 1 else res[0]


def _fn_sgu(u_pre, v_pre, ln_g, ln_b, w, bb, na):
    t = u_pre.shape[0]
    u = _gelu(u_pre)
    v = _gelu(v_pre)
    mu = jnp.mean(v, axis=-1, keepdims=True)
    vc = v - mu
    vh = vc * lax.rsqrt(jnp.mean(vc * vc, axis=-1, keepdims=True) + EPS) * ln_g + ln_b
    ri = lax.broadcasted_iota(jnp.int32, w.shape, 0)
    ci = lax.broadcasted_iota(jnp.int32, w.shape, 1)
    wc = jnp.where(ri >= ci, w, 0.0)
    f = jnp.concatenate([_bmm(wc, vh[c * SG_CHUNK:(c + 1) * SG_CHUNK]) + bb for c in range(t // SG_CHUNK)], axis=0)
    return (_rms(u * f, na),)


def _fn_sconv(gb, gc, xin, w0, w1, w2, nb):
    return (_rms(gb * _causal_conv(gc * xin, (w0, w1, w2)), nb),)


def _fn_gdn_qk(pre, w0, w1, w2, w3):
    a = _silu(_causal_conv(pre, (w0, w1, w2, w3)))
    return (a * lax.rsqrt(jnp.sum(a * a, axis=-1, keepdims=True) + EPS),)


def _fn_gdn_v(pre, w0, w1, w2, w3):
    return (_silu(_causal_conv(pre, (w0, w1, w2, w3))),)


def _fn_gdn_gates(ab, a_log, dt_bias):
    lane = lax.broadcasted_iota(jnp.int32, ab.shape, 1)
    g = -jnp.exp(a_log) * _softplus(ab + dt_bias)
    return (jnp.where(lane < GDN_HEADS, g, jnp.where(lane < 2 * GDN_HEADS, _sigmoid(ab), 0.0)),)


def _solve_unit_lower(a):
    n = a.shape[0]
    ri = lax.broadcasted_iota(jnp.int32, (n, n), 0)
    ci = lax.broadcasted_iota(jnp.int32, (n, n), 1)
    t = (ri == ci).astype(F32)
    sh = 0
    while (1 << sh) < n:
        rb = jnp.right_shift(ri, sh)
        cb = jnp.right_shift(ci, sh)
        off = ((rb & 1) == 1) & (cb == rb - 1)
        t = t - _mm(_mm(t, jnp.where(off, a, 0.0)), t)
        sh += 1
    return t


def _gdn_wy_chunk(q, k, v, g, beta):
    n, dk = q.shape
    ri = lax.broadcasted_iota(jnp.int32, (n, n), 0)
    ci = lax.broadcasted_iota(jnp.int32, (n, n), 1)
    incl = ri >= ci
    eye = (ri == ci).astype(F32)
    g_row = jnp.sum(eye * g, axis=0, keepdims=True)
    gc = jnp.sum(jnp.where(incl, g_row, 0.0), axis=1, keepdims=True)
    gc_row = jnp.sum(eye * gc, axis=0, keepdims=True)
    decay_incl = jnp.where(incl, jnp.exp(jnp.where(incl, gc - gc_row, 0.0)), 0.0)
    decay_strict = jnp.where(ri > ci, decay_incl, 0.0)
    q = q * (dk ** -0.5)
    kb = k * beta
    t = _solve_unit_lower(_mm_nt(kb, k) * decay_strict)
    eg = jnp.exp(gc)
    value = _mm(t, v * beta)
    kcd = _mm(t, kb * eg)
    intra = _mm_nt(q, k) * decay_incl
    last = lax.broadcasted_iota(jnp.int32, (n, 1), 0) == (n - 1)
    g_last = jnp.sum(jnp.where(last, gc, 0.0), axis=0, keepdims=True)
    carry = jnp.broadcast_to(jnp.exp(g_last), (8, LANES))
    return value, kcd, q * eg, k * jnp.exp(g_last - gc), intra, carry


def _fn_gdn_wy(q, k, v, gates, pick_g, pick_b):
    outs = []
    for c in range(q.shape[0] // GDN_L):
        rows = slice(c * GDN_L, (c + 1) * GDN_L)
        g = jnp.sum(gates[rows] * pick_g, axis=1, keepdims=True)
        beta = jnp.sum(gates[rows] * pick_b, axis=1, keepdims=True)
        outs.append(_gdn_wy_chunk(q[rows], k[rows], v[rows], g, beta))
    return tuple(jnp.concatenate(parts, axis=0) for parts in zip(*outs))


def _gdn_step(s, value, kcd, qe, kt, intra, carry):
    v_new = value - _bmm(kcd, s)
    o = _bmm(qe, s) + _bmm(intra, v_new)
    return s * carry + _bmm_tn(kt, v_new), o


def _gdn_post(o, z, nrm):
    return _rms(o, nrm) * _silu(z)


def _gdn_wy_ins(q, k, v, gates):
    t = q.shape[0]
    rb = min(GDN_CPS * GDN_L, t)
    hd = GDN_HEADS
    lane = jnp.arange(LANES)[None, None, :]
    pick_g = (lane == jnp.arange(hd)[:, None, None]).astype(F32)
    pick_b = (lane == jnp.arange(hd)[:, None, None] + hd).astype(F32)
    blk = lambda a: _In(a, (rb, LANES), lambda i: (i // hd, i % hd))
    par = lambda a: _In(a, (None, 1, LANES), lambda i: (i % hd, 0, 0))
    ins = [blk(q), blk(k), blk(v), _In(gates, (rb, LANES), lambda i: (i // hd, 0), shared=hd), par(pick_g),
           par(pick_b)]
    wide = lambda dt: ((t, GDN_W), dt, (rb, LANES), lambda i: (i // hd, i % hd))
    carry = ((hd, 8 * (t // GDN_L), LANES), F32, (None, 8 * (rb // GDN_L), LANES), lambda i: (i % hd, i // hd, 0))
    outs = [wide(F32), wide(BF16), wide(BF16), wide(BF16), wide(BF16), carry]
    return (t // rb) * hd, ins, outs


def _gdn_scan_specs(t):
    col = lambda base: pl.BlockSpec((t, LANES), lambda h: (0, base + h))
    carry = pl.BlockSpec((None, 8 * (t // GDN_L), LANES), lambda h: (h, 0, 0))
    par = pl.BlockSpec((None, 1, LANES), lambda h: (h, 0, 0))
    return col, carry, par


def _gdn_scan_fwd(wy, proj, nrm):
    t = wy[0].shape[0]
    nc = t // GDN_L

    def body(val_ref, kcd_ref, qe_ref, kt_ref, in_ref, cy_ref, z_ref, n_ref, y_ref, o_scr):
        def step(c, s):
            rows = pl.ds(pl.multiple_of(c * GDN_L, GDN_L), GDN_L)
            s, o = _gdn_step(s, val_ref[rows, :], kcd_ref[rows, :], qe_ref[rows, :], kt_ref[rows, :],
                             in_ref[rows, :], cy_ref[pl.ds(pl.multiple_of(c * 8, 8), 1), :])
            o_scr[rows, :] = o
            return s

        lax.fori_loop(0, nc, step, jnp.zeros((LANES, LANES), F32))
        y_ref[...] = _gdn_post(o_scr[...], z_ref[...], n_ref[...]).astype(y_ref.dtype)

    col, carry, par = _gdn_scan_specs(t)
    return pl.pallas_call(
        body, name="gdn_scan_fwd", grid=(GDN_HEADS,),
        in_specs=[col(0)] * 5 + [carry, col(CB_Z), par],
        out_specs=col(0), out_shape=jax.ShapeDtypeStruct((t, GDN_W), BF16),
        scratch_shapes=[pltpu.VMEM((t, LANES), F32)],
        compiler_params=_cparams(("arbitrary",)),
    )(*wy, proj, nrm)


def _gdn_scan_bwd(wy, proj, nrm, dy, dy_base):
    t = wy[0].shape[0]
    nc = t // GDN_L

    def body(val_ref, kcd_ref, qe_ref, kt_ref, in_ref, cy_ref, z_ref, n_ref, dy_ref,
             dval_ref, dkcd_ref, dqe_ref, dkt_ref, din_ref, dcy_ref, dz_ref, dn_ref, o_scr, s_scr):
        def operands(c):
            rows = pl.ds(pl.multiple_of(c * GDN_L, GDN_L), GDN_L)
            return rows, (val_ref[rows, :], kcd_ref[rows, :], qe_ref[rows, :], kt_ref[rows, :], in_ref[rows, :],
                          cy_ref[pl.ds(pl.multiple_of(c * 8, 8), 1), :])

        def step(c, s):
            rows, ops = operands(c)
            s_scr[c] = s
            s, o = _gdn_step(s, *ops)
            o_scr[rows, :] = o
            return s

        lax.fori_loop(0, nc, step, jnp.zeros((LANES, LANES), F32))
        _, vjp_post = jax.vjp(_gdn_post, o_scr[...], z_ref[...], n_ref[...])
        do, dz, dn = vjp_post(dy_ref[...].astype(F32))
        dz_ref[...] = dz.astype(dz_ref.dtype)
        dn_ref[...] = dn
        o_scr[...] = do
        dcy_ref[...] = jnp.zeros_like(dcy_ref)

        def rstep(i, ds):
            c = nc - 1 - i
            rows, ops = operands(c)
            _, vjp_c = jax.vjp(_gdn_step, s_scr[c], *[o.astype(F32) for o in ops])
            ds, dval, dkcd, dqe, dkt, din, dcy = vjp_c((ds, o_scr[rows, :]))
            for r, g in ((dval_ref, dval), (dkcd_ref, dkcd), (dqe_ref, dqe), (dkt_ref, dkt), (din_ref, din)):
                r[rows, :] = g.astype(r.dtype)
            dcy_ref[pl.ds(pl.multiple_of(c * 8, 8), 1), :] = dcy
            return ds

        lax.fori_loop(0, nc, rstep, jnp.zeros((LANES, LANES), F32))

    col, carry, par = _gdn_scan_specs(t)
    wide = jax.ShapeDtypeStruct((t, GDN_W), BF16)
    return pl.pallas_call(
        body, name="gdn_scan_bwd", grid=(GDN_HEADS,),
        in_specs=[col(0)] * 5 + [carry, col(CB_Z), par, col(dy_base)],
        out_specs=[col(0)] * 5 + [carry, col(0), par],
        out_shape=[wide] * 5 + [jax.ShapeDtypeStruct(wy[5].shape, F32), wide,
                                jax.ShapeDtypeStruct((GDN_HEADS, 1, LANES), F32)],
        scratch_shapes=[pltpu.VMEM((t, LANES), F32), pltpu.VMEM((nc, LANES, LANES), F32)],
        compiler_params=_cparams(("arbitrary",)),
    )(*wy, proj, nrm, dy)


TM = 512
TR = 256


def _rms_fwd(name, h, g):
    t, d = h.shape
    tm = min(TR, t)
    return _fused(name, lambda hb, gb: (_rms(hb, gb),), t // tm, [_row_in(h, tm), _row_shared(g)],
                  [_row_out(t, d, BF16, tm)])[0]


def _rms_bwd(name, h, g, dxn, dh_next):
    t, d = h.shape
    tm = min(TR, t)
    dh, dg = _fused(name, lambda hb, gb: (_rms(hb, gb),), t // tm, [_row_in(h, tm), _row_shared(g)],
                    [_row_out(t, d, F32, tm)], douts=[dxn], need=[True, True], add={0: dh_next})
    return dh, dg


def _mixer_ins(proj, lw):
    sgu = [_col_in(proj, CB_U, SG_HEADS), _col_in(proj, CB_V, SG_HEADS), _col_par(lw['sg_ln_g']),
           _col_par(lw['sg_ln_b']), _col_par(lw['sg_w']), _col_par(lw['sg_bb']), _col_par(lw['out_norm_a'])]
    sconv = [_col_in(proj, CB_GB, SC_GROUPS), _col_in(proj, CB_GC, SC_GROUPS), _col_in(proj, CB_X, SC_GROUPS)] + \
            [_col_par(w) for w in lw['sc_taps']] + [_col_par(lw['out_norm_b'])]
    gq = [_col_in(proj, CB_Q, GDN_HEADS)] + [_col_par(w) for w in lw['q_taps']]
    gk = [_col_in(proj, CB_K, GDN_HEADS)] + [_col_par(w) for w in lw['k_taps']]
    gv = [_col_in(proj, CB_VV, GDN_HEADS)] + [_col_par(w) for w in lw['v_taps']]
    return sgu, sconv, gq, gk, gv


def _gates_ins(proj, lw, tm):
    return [_row_in(proj, tm, CB_AB), _row_shared(lw['a_log_row']), _row_shared(lw['dt_bias_row'])]


def _layer_fwd(h, p_l, lw):
    t, d = h.shape
    dff = lw['w_ff2'].shape[0]
    xn = _rms_fwd("rms_fwd", h, lw['norm_mix'])
    proj = _matmul("proj_fwd", xn, lw['w_in'], "nn", t, PROJ_W, d, TM, 1024, d)
    sgu, sconv, gq, gk, gv = _mixer_ins(proj, lw)
    ya = _fused("sgu_fwd", _fn_sgu, SG_HEADS, sgu, [_col_out(t, SG_W, BF16)])[0]
    yb = _fused("sconv_fwd", _fn_sconv, SC_GROUPS, sconv, [_col_out(t, SC_W, BF16)])[0]
    q = _fused("gdn_q_fwd", _fn_gdn_qk, GDN_HEADS, gq, [_col_out(t, GDN_W, F32)])[0]
    k = _fused("gdn_k_fwd", _fn_gdn_qk, GDN_HEADS, gk, [_col_out(t, GDN_W, F32)])[0]
    v = _fused("gdn_v_fwd", _fn_gdn_v, GDN_HEADS, gv, [_col_out(t, GDN_W, F32)])[0]
    tm = min(TR, t)
    gates = _fused("gdn_gates_fwd", _fn_gdn_gates, t // tm, _gates_ins(proj, lw, tm),
                   [_row_out(t, LANES, F32, tm)])[0]
    n_wy, wy_ins, wy_outs = _gdn_wy_ins(q, k, v, gates)
    wy = _fused("gdn_wy_fwd", _fn_gdn_wy, n_wy, wy_ins, wy_outs)
    yc = _gdn_scan_fwd(wy, proj, lw['gdn_norm'])
    ycat = jnp.concatenate([ya, yb, yc], axis=1)
    h2 = _matmul("wo_fwd", ycat, lw['w_o'], "nn", t, d, d, TM, 1024, d,
                 epilogue=lambda acc, hb: (hb + acc,), extras=(h,))
    hn = _rms_fwd("rms_fwd", h2, lw['norm_ffn'])
    per = d // 1024
    s, r = _matmul("ff1_fwd", hn, lw['w_ff1'], "nn", t, dff, d, TM, 1024, d,
                   b_spec=pl.BlockSpec((None, d, 1024), lambda i, j, kk: (j // per, 0, j % per)),
                   epilogue=lambda acc: (jnp.maximum(acc, 0.0), jnp.square(jnp.maximum(acc, 0.0))),
                   outs=[((t, dff), BF16, (min(TM, t), 1024), lambda i, j, kk: (i, j))] * 2)
    h3 = _matmul("ff2_fwd", r, lw['w_ff2'], "nn", t, d, dff, TM, 1024, 2048,
                 epilogue=lambda acc, hb: (hb + acc,), extras=(h2,))
    hn2 = _rms_fwd("rms_fwd", h3, lw['norm_ple'])
    pp = _matmul("ple_proj_fwd", p_l, lw['w_ple_proj'], "nn", t, d, p_l.shape[1], TM, 1024, p_l.shape[1])

    def gate_epilogue(acc, hb, ppb):
        sg = _sigmoid(acc)
        return hb + ppb * sg, sg

    h4, gate = _matmul("ple_gate_fwd", hn2, lw['w_ple_gate'], "nn", t, d, d, TM, 1024, d, epilogue=gate_epilogue,
                       extras=(h3, pp), outs=[((t, d), F32, (min(TM, t), 1024), lambda i, j, kk: (i, j))] * 2)
    saved = dict(h=h, xn=xn, proj=proj, q=q, k=k, v=v, gates=gates, ycat=ycat, h2=h2, hn=hn, s=s, r=r, h3=h3,
                 hn2=hn2, pp=pp, gate=gate, p=p_l, wy=wy)
    return h4, saved


def _layer_bwd(dh4, sv, lw):
    t, d = dh4.shape
    dff = lw['w_ff2'].shape[0]
    tm = min(TR, t)
    g = {}
    dacc, dpp = _fused("ple_bwd_gate", lambda dh, pp, gt: (dh * pp * gt * (1.0 - gt), dh * gt), t // tm,
                       [_row_in(dh4, tm), _row_in(sv['pp'], tm), _row_in(sv['gate'], tm)],
                       [_row_out(t, d, BF16, tm)] * 2)
    g['w_ple_gate'] = _matmul("dw_ple_gate", sv['hn2'], dacc, "tn", d, d, t, TM, 1024, 1024, out_dtype=BF16)
    g['w_ple_proj'] = _matmul("dw_ple_proj", sv['p'], dpp, "tn", sv['p'].shape[1], d, t, TM, 1024, 1024,
                              out_dtype=BF16)
    dhn2 = _matmul("dx_ple_gate", dacc, lw['w_ple_gate'], "nt", t, d, d, TM, 1024, d)
    dh3, g['norm_ple'] = _rms_bwd("rms_bwd", sv['h3'], lw['norm_ple'], dhn2, dh4)
    da = _matmul("dx_ff2", dh3, lw['w_ff2'], "nt", t, dff, d, TM, 1024, d,
                 epilogue=lambda acc, sb: (acc * (2.0 * sb.astype(F32)),), extras=(sv['s'],),
                 outs=[((t, dff), BF16, (min(TM, t), 1024), lambda i, j, kk: (i, j))])
    g['w_ff2'] = _matmul("dw_ff2", sv['r'], dh3, "tn", dff, d, t, TM, 1024, 1024, out_dtype=BF16)
    per = d // 1024
    dhn = _matmul("dx_ff1", da, lw['w_ff1'], "nt", t, d, dff, TM, 1024, 1024,
                  b_spec=pl.BlockSpec((None, 1024, 1024), lambda i, j, kk: (kk // per, j, kk % per)))
    g['w_ff1'] = _matmul("dw_ff1", sv['hn'], da, "tn", d, dff, t, TM, 1024, 1024,
                         outs=[((N_SHARD, d, d), BF16, (None, TM, 1024), lambda i, j, kk: (j // per, i, j % per))])
    dh2, g['norm_ffn'] = _rms_bwd("rms_bwd", sv['h2'], lw['norm_ffn'], dhn, dh3)
    dycat = _matmul("dx_o", dh2, lw['w_o'], "nt", t, d, d, TM, 1024, d)
    g['w_o'] = _matmul("dw_o", sv['ycat'], dh2, "tn", d, d, t, TM, 1024, 1024, out_dtype=BF16)
    proj = sv['proj']
    sgu, sconv, gq, gk, gv = _mixer_ins(proj, lw)
    bf2 = [BF16, BF16]
    r_ = _fused("sgu_bwd", _fn_sgu, SG_HEADS, sgu, [_col_out(t, d, F32, 0)], douts=[dycat], need=[True] * 7,
                gdt=bf2 + [F32] * 5)
    du, dv_, g['sg_ln_g'], g['sg_ln_b'], g['sg_w'], g['sg_bb'], g['out_norm_a'] = r_
    r_ = _fused("sconv_bwd", _fn_sconv, SC_GROUPS, sconv, [_col_out(t, d, F32, SG_HEADS)], douts=[dycat],
                need=[True] * 7, gdt=[BF16] * 3 + [F32] * 4)
    dgb, dgc, dxin = r_[:3]
    g['sc_taps'], g['out_norm_b'] = r_[3:6], r_[6]
    r_ = _gdn_scan_bwd(sv['wy'], proj, lw['gdn_norm'], dycat, SG_HEADS + SC_GROUPS)
    dwy, dz, g['gdn_norm'] = r_[:6], r_[6], r_[7]
    n_wy, wy_ins, wy_outs = _gdn_wy_ins(sv['q'], sv['k'], sv['v'], sv['gates'])
    dq, dk, dvv, dgates = _fused("gdn_wy_bwd", _fn_gdn_wy, n_wy, wy_ins, wy_outs, douts=dwy,
                                 need=[True] * 4 + [False] * 2)[:4]
    one = [_col_out(t, GDN_W, F32)]
    r_ = _fused("gdn_q_bwd", _fn_gdn_qk, GDN_HEADS, gq, one, douts=[dq], need=[True] * 5, gdt=[BF16] + [F32] * 4)
    dpq, g['q_taps'] = r_[0], r_[1:]
    r_ = _fused("gdn_k_bwd", _fn_gdn_qk, GDN_HEADS, gk, one, douts=[dk], need=[True] * 5, gdt=[BF16] + [F32] * 4)
    dpk, g['k_taps'] = r_[0], r_[1:]
    r_ = _fused("gdn_v_bwd", _fn_gdn_v, GDN_HEADS, gv, one, douts=[dvv], need=[True] * 5, gdt=[BF16] + [F32] * 4)
    dpv, g['v_taps'] = r_[0], r_[1:]
    dab, g['a_log_row'], g['dt_bias_row'] = _fused(
        "gdn_gates_bwd", _fn_gdn_gates, t // tm, _gates_ins(proj, lw, tm), [_row_out(t, LANES, F32, tm)],
        douts=[dgates], need=[True] * 3, gdt=[BF16, F32, F32])
    pad = jnp.zeros((t, PROJ_W - (CB_AB + 1) * LANES), BF16)
    dproj = jnp.concatenate([du, dv_, dgb, dgc, dxin, dpq, dpk, dpv, dz, dab, pad], axis=1)
    dxn = _matmul("dx_in", dproj, lw['w_in'], "nt", t, d, PROJ_W, TM, 1024, 1792)
    g['w_in'] = _matmul("dw_in", sv['xn'], dproj, "tn", d, PROJ_W, t, TM, 1024, 1024, out_dtype=BF16)
    dh, g['norm_mix'] = _rms_bwd("rms_bwd", sv['h'], lw['norm_mix'], dxn, dh2)
    return dh, g


def _loss_grad(h, g, tgt):
    t, d = h.shape
    tm = min(TR, t)

    def body(h_ref, g_ref, t_ref, loss_ref, dh_ref, dg_ref):
        y, vjp = jax.vjp(_rms, h_ref[...], g_ref[...])
        e = y - t_ref[...]
        dh, dg = vjp(e * (1.0 / d))

        @pl.when(pl.program_id(0) == 0)
        def _():
            loss_ref[...] = jnp.zeros_like(loss_ref)
            dg_ref[...] = jnp.zeros_like(dg_ref)

        loss_ref[...] += jnp.sum(jnp.sum(e * e, axis=1, keepdims=True), axis=0, keepdims=True) * (0.5 / d)
        dh_ref[...] = dh
        dg_ref[...] += dg

    row = pl.BlockSpec((tm, d), lambda i: (i, 0))
    return pl.pallas_call(
        body, name="loss_grad", grid=(t // tm,),
        in_specs=[row, pl.BlockSpec((1, d), lambda i: (0, 0)), row],
        out_specs=[pl.BlockSpec((1, LANES), lambda i: (0, 0)), row, pl.BlockSpec((1, d), lambda i: (0, 0))],
        out_shape=[jax.ShapeDtypeStruct((1, LANES), F32), jax.ShapeDtypeStruct((t, d), F32),
                   jax.ShapeDtypeStruct((1, d), F32)],
        compiler_params=_cparams(("arbitrary",)),
    )(h, g, tgt)


def _layer_weights(l, full, small):
    d = full['w_o'].shape[-1]
    lw = {}
    w_in = jnp.transpose(full['w_in'][l], (1, 0, 2)).reshape(d, IN_COLS)
    lw['w_in'] = jnp.pad(w_in, ((0, 0), (0, PROJ_W - IN_COLS)))
    lw['w_o'] = full['w_o'][l].reshape(d, d)
    lw['w_ff1'] = full['w_ff1'][l]
    lw['w_ff2'] = full['w_ff2'][l].reshape(-1, d)
    lw['w_ple_gate'] = full['w_ple_gate'][l].reshape(d, d)
    wpp = full['w_ple_proj'][l]
    lw['w_ple_proj'] = jnp.transpose(wpp, (1, 0, 2)).reshape(wpp.shape[1], d)
    for n in ('norm_mix', 'norm_ffn', 'norm_ple'):
        lw[n] = small[n][l].reshape(1, d)
    lw['sg_ln_g'] = small['sg_ln_g'][l].reshape(SG_HEADS, 1, LANES)
    lw['sg_ln_b'] = small['sg_ln_b'][l].reshape(SG_HEADS, 1, LANES)
    lw['sg_w'] = small['sg_w'][l]
    lw['sg_bb'] = jnp.broadcast_to(small['sg_b'][l][:, :, None], (SG_HEADS, SG_CHUNK, LANES))
    lw['out_norm_a'] = small['out_norm_a'][l].reshape(SG_HEADS, 1, LANES)
    lw['out_norm_b'] = small['out_norm_b'][l].reshape(SC_GROUPS, 1, LANES)
    lw['gdn_norm'] = jnp.broadcast_to(small['gdn_norm'][l].reshape(1, 1, LANES), (GDN_HEADS, 1, LANES))
    lw['a_log_row'] = jnp.pad(small['gdn_a_log'][l].reshape(1, GDN_HEADS), ((0, 0), (0, LANES - GDN_HEADS)))
    lw['dt_bias_row'] = jnp.pad(small['gdn_dt_bias'][l].reshape(1, GDN_HEADS), ((0, 0), (0, LANES - GDN_HEADS)))
    sc = full['sc_conv'][l]
    lw['sc_taps'] = [sc[:, j:j + 1, :] for j in range(SC_KERNEL)]
    gc = jnp.transpose(full['gdn_conv'][l], (1, 0, 2)).reshape(GDN_CONV, 3 * GDN_W)
    for i, nm in enumerate(('q_taps', 'k_taps', 'v_taps')):
        part = gc[:, i * GDN_W:(i + 1) * GDN_W].reshape(GDN_CONV, GDN_HEADS, 1, LANES)
        lw[nm] = [part[j] for j in range(GDN_CONV)]
    return lw


def _weight_grads(g):
    d = g['w_o'].shape[0]
    out = {}
    w_in = g['w_in'][:, :IN_COLS].reshape(d, N_SHARD, IN_COLS // N_SHARD)
    out['w_in'] = jnp.transpose(w_in, (1, 0, 2))
    out['w_o'] = g['w_o'].reshape(N_SHARD, d // N_SHARD, d)
    out['w_ff1'] = g['w_ff1']
    out['w_ff2'] = g['w_ff2'].reshape(N_SHARD, -1, d)
    out['w_ple_gate'] = g['w_ple_gate'].reshape(N_SHARD, d // N_SHARD, d)
    wpp = g['w_ple_proj']
    out['w_ple_proj'] = jnp.transpose(wpp.reshape(wpp.shape[0], N_SHARD, d // N_SHARD), (1, 0, 2))
    for n in ('norm_mix', 'norm_ffn', 'norm_ple'):
        out[n] = g[n].reshape(d)
    out['sg_ln_g'] = g['sg_ln_g'].reshape(SG_W)
    out['sg_ln_b'] = g['sg_ln_b'].reshape(SG_W)
    out['sg_w'] = g['sg_w']
    out['sg_b'] = jnp.sum(g['sg_bb'], axis=2)
    out['out_norm_a'] = g['out_norm_a'].reshape(SG_W)
    out['out_norm_b'] = g['out_norm_b'].reshape(SC_W)
    out['gdn_norm'] = jnp.sum(g['gdn_norm'], axis=(0, 1))
    out['gdn_a_log'] = g['a_log_row'][0, :GDN_HEADS]
    out['gdn_dt_bias'] = g['dt_bias_row'][0, :GDN_HEADS]
    out['sc_conv'] = jnp.concatenate([w.reshape(1, SC_W) for w in g['sc_taps']], axis=0)
    taps = [jnp.concatenate([g[nm][j].reshape(1, GDN_W) for nm in ('q_taps', 'k_taps', 'v_taps')], axis=1)
            for j in range(GDN_CONV)]
    out['gdn_conv'] = jnp.concatenate(taps, axis=0)
    return out


def _local_step(x, p, tgt, full, small):
    h = x
    saved, lws = [], []
    for l in range(DEPTH):
        lw = _layer_weights(l, full, small)
        h, sv = _layer_fwd(h, p[l], lw)
        saved.append(sv)
        lws.append(lw)
    loss, dh, dnf = _loss_grad(h, small['norm_final'].reshape(1, -1), tgt)
    per_layer = [None] * DEPTH
    for l in reversed(range(DEPTH)):
        dh, g = _layer_bwd(dh, saved[l], lws[l])
        per_layer[l] = _weight_grads(g)
    grads = {n: jnp.stack([per_layer[l][n] for l in range(DEPTH)], axis=0) for n in per_layer[0]}
    grads['norm_final'] = dnf.reshape(-1)
    return loss, dh, grads


ANY = pl.BlockSpec(memory_space=pl.ANY)


def _place():
    x, y, c = lax.axis_index("x"), lax.axis_index("y"), lax.axis_index("c")
    chips = [(1 - x, y), (x, 1 - y), (1 - x, 1 - y)]
    return x, y, c, chips


def _all_gather(shards):
    n = len(shards)

    def body(*refs):
        srcs, outs = refs[:n], refs[n:2 * n]
        send_sems, recv_sems, local_sems = refs[2 * n:]
        x, y, c, chips = _place()
        me = 2 * x + y
        sibling = (x, y, 1 - c)

        def cp(w, k, shard, layer, to, own=False):
            return pltpu.make_async_remote_copy(
                src_ref=srcs[w].at[layer] if own else outs[w].at[layer, shard], dst_ref=outs[w].at[layer, shard],
                send_sem=send_sems.at[w, k], recv_sem=recv_sems.at[w, k], device_id=to, device_id_type=MESH)

        started = []
        for w in range(n):
            for l in range(DEPTH):
                lc = pltpu.make_async_copy(srcs[w].at[l], outs[w].at[l, me], local_sems.at[w, l])
                lc.start()
                started.append(lc)
        sends = []
        for w in range(n):
            for j, chip in enumerate(chips):
                s = cp(w, j, me, c, (*chip, c), own=True)
                s.start()
                sends.append(s)
        for w in range(n):
            for j, (px, py) in enumerate(chips):
                cp(w, j, 2 * px + py, c, sibling).wait_recv()
                s = cp(w, 3 + j, 2 * px + py, c, sibling)
                s.start()
                sends.append(s)
        for w in range(n):
            for j, (px, py) in enumerate(chips):
                cp(w, 3 + j, 2 * px + py, 1 - c, sibling).wait_recv()
        for s in sends:
            s.wait_send()
        for lc in started:
            lc.wait()

    return pl.pallas_call(
        body, name="all_gather",
        in_specs=[ANY] * n, out_specs=[ANY] * n,
        out_shape=[jax.ShapeDtypeStruct((DEPTH, N_SHARD) + s.shape[1:], s.dtype) for s in shards],
        scratch_shapes=[pltpu.SemaphoreType.DMA((n, 6)), pltpu.SemaphoreType.DMA((n, 6)),
                        pltpu.SemaphoreType.DMA((n, DEPTH))],
    )(*shards)


def _swap_layers(grads):
    n = len(grads)

    def body(*refs):
        srcs, outs = refs[:n], refs[n:2 * n]
        send_sems, recv_sems = refs[2 * n:]
        x, y, c, _ = _place()
        cps = [pltpu.make_async_remote_copy(src_ref=srcs[w].at[1 - c], dst_ref=outs[w], send_sem=send_sems.at[w],
                                            recv_sem=recv_sems.at[w], device_id=(x, y, 1 - c), device_id_type=MESH)
               for w in range(n)]
        for cpy in cps:
            cpy.start()
        for cpy in cps:
            cpy.wait()

    return pl.pallas_call(
        body, name="rs_swap_layers", in_specs=[ANY] * n, out_specs=[ANY] * n,
        out_shape=[jax.ShapeDtypeStruct(g.shape[1:], g.dtype) for g in grads],
        scratch_shapes=[pltpu.SemaphoreType.DMA((n,)), pltpu.SemaphoreType.DMA((n,))],
    )(*grads)


def _scatter_shards(parts):
    n = len(parts)

    def body(*refs):
        srcs, outs = refs[:n], refs[n:2 * n]
        send_sems, recv_sems, local_sems = refs[2 * n:]
        x, y, c, chips = _place()
        me = 2 * x + y
        local = [pltpu.make_async_copy(srcs[w].at[me], outs[w].at[me], local_sems.at[w]) for w in range(n)]
        for lc in local:
            lc.start()

        def cp(w, j, src_shard, dst_slot, to):
            return pltpu.make_async_remote_copy(
                src_ref=srcs[w].at[src_shard], dst_ref=outs[w].at[dst_slot], send_sem=send_sems.at[w, j],
                recv_sem=recv_sems.at[w, j], device_id=to, device_id_type=MESH)

        sends = [cp(w, j, 2 * px + py, me, (px, py, c)) for w in range(n) for j, (px, py) in enumerate(chips)]
        for s in sends:
            s.start()
        for w in range(n):
            for j, (px, py) in enumerate(chips):
                cp(w, j, me, 2 * px + py, (px, py, c)).wait_recv()
        for s in sends:
            s.wait_send()
        for lc in local:
            lc.wait()

    return pl.pallas_call(
        body, name="rs_scatter", in_specs=[ANY] * n, out_specs=[ANY] * n,
        out_shape=[jax.ShapeDtypeStruct(g.shape, g.dtype) for g in parts],
        scratch_shapes=[pltpu.SemaphoreType.DMA((n, 3)), pltpu.SemaphoreType.DMA((n, 3)),
                        pltpu.SemaphoreType.DMA((n,))],
    )(*parts)


def _join_layers(reduced):
    n = len(reduced)

    def body(*refs):
        srcs, outs = refs[:n], refs[n:2 * n]
        send_sems, recv_sems, local_sems = refs[2 * n:]
        x, y, c, _ = _place()
        local = [pltpu.make_async_copy(srcs[w], outs[w].at[c], local_sems.at[w]) for w in range(n)]
        for lc in local:
            lc.start()
        sends = [pltpu.make_async_remote_copy(src_ref=srcs[w], dst_ref=outs[w].at[c], send_sem=send_sems.at[w],
                                              recv_sem=recv_sems.at[w], device_id=(x, y, 1 - c), device_id_type=MESH)
                 for w in range(n)]
        for s in sends:
            s.start()
        for w in range(n):
            pltpu.make_async_remote_copy(src_ref=srcs[w], dst_ref=outs[w].at[1 - c], send_sem=send_sems.at[w],
                                         recv_sem=recv_sems.at[w], device_id=(x, y, 1 - c),
                                         device_id_type=MESH).wait_recv()
        for s in sends:
            s.wait_send()
        for lc in local:
            lc.wait()

    return pl.pallas_call(
        body, name="rs_join_layers", in_specs=[ANY] * n, out_specs=[ANY] * n,
        out_shape=[jax.ShapeDtypeStruct((DEPTH,) + g.shape, g.dtype) for g in reduced],
        scratch_shapes=[pltpu.SemaphoreType.DMA((n,)), pltpu.SemaphoreType.DMA((n,)),
                        pltpu.SemaphoreType.DMA((n,))],
    )(*reduced)


def _exchange_small(vec):
    def body(src, out, send_sems, recv_sems, local_sem):
        x, y, c, _ = _place()
        me = 4 * x + 2 * y + c
        lc = pltpu.make_async_copy(src, out.at[me], local_sem)
        lc.start()
        sends = []
        for k in range(1, N_DEV):
            fx, fy, fc = (k >> 2) & 1, (k >> 1) & 1, k & 1
            to = (x ^ fx, y ^ fy, c ^ fc)
            s = pltpu.make_async_remote_copy(src_ref=src, dst_ref=out.at[me], send_sem=send_sems.at[k - 1],
                                             recv_sem=recv_sems.at[k - 1], device_id=to, device_id_type=MESH)
            s.start()
            sends.append(s)
        for k in range(1, N_DEV):
            fx, fy, fc = (k >> 2) & 1, (k >> 1) & 1, k & 1
            frm = 4 * (x ^ fx) + 2 * (y ^ fy) + (c ^ fc)
            pltpu.make_async_remote_copy(src_ref=src, dst_ref=out.at[frm], send_sem=send_sems.at[k - 1],
                                         recv_sem=recv_sems.at[k - 1], device_id=(x ^ fx, y ^ fy, c ^ fc),
                                         device_id_type=MESH).wait_recv()
        for s in sends:
            s.wait_send()
        lc.wait()

    return pl.pallas_call(
        body, name="small_exchange", in_specs=[ANY], out_specs=ANY,
        out_shape=jax.ShapeDtypeStruct((N_DEV,) + vec.shape, vec.dtype),
        scratch_shapes=[pltpu.SemaphoreType.DMA((N_DEV - 1,)), pltpu.SemaphoreType.DMA((N_DEV - 1,)),
                        pltpu.SemaphoreType.DMA],
    )(vec)


def _add_own_layer(g, other, c_arr):
    _, ns, r, cc = g.shape
    tr = min(TR, r)
    nb = r // tr

    def body(c_ref, g_ref, o_ref, out_ref):
        out_ref[...] = (g_ref[...].astype(F32) + o_ref[...].astype(F32)).astype(out_ref.dtype)

    slab = pl.BlockSpec((None, tr, cc), lambda i, cr: (i // nb, i % nb, 0))
    return pl.pallas_call(
        body, name="rs_add_own_layer",
        grid_spec=pltpu.PrefetchScalarGridSpec(
            num_scalar_prefetch=1, grid=(ns * nb,),
            in_specs=[pl.BlockSpec((None, None, tr, cc), lambda i, cr: (cr[0], i // nb, i % nb, 0)), slab],
            out_specs=slab),
        out_shape=jax.ShapeDtypeStruct(other.shape, other.dtype),
        compiler_params=_cparams(("arbitrary",)),
    )(c_arr, g, other)


def _sum_slots(name, a):
    ns, r, cc = a.shape
    tr = min(TR, r)
    ins = [_In(a, (None, tr, cc), functools.partial(lambda s, i: (s, i, 0), s)) for s in range(ns)]

    def fn(*blocks):
        acc = blocks[0].astype(F32)
        for b in blocks[1:]:
            acc = acc + b.astype(F32)
        return (acc,)

    return _fused(name, fn, r // tr, ins, [((r, cc), F32, (tr, cc), lambda i: (i, 0))])[0]


def _adamw_fn(w, g, m, v):
    m = ADAM_B1 * m + (1.0 - ADAM_B1) * g
    v = ADAM_B2 * v + (1.0 - ADAM_B2) * jnp.square(g)
    m_hat = m / (1.0 - ADAM_B1 ** ADAM_STEP)
    v_hat = v / (1.0 - ADAM_B2 ** ADAM_STEP)
    delta = -ADAM_LR * (m_hat / (jnp.sqrt(v_hat) + ADAM_EPS) + ADAM_WD * w)
    return delta, m, v


def _adamw(w, g, m, v):
    r, cc = w.shape
    tr = min(TR, r)
    ins = [_In(a, (tr, cc), lambda i: (i, 0)) for a in (w, g, m, v)]
    return _fused("adamw", _adamw_fn, r // tr, ins, [((r, cc), F32, (tr, cc), lambda i: (i, 0))] * 3)


def _pack(arrs):
    flat = jnp.concatenate([a.reshape(-1) for a in arrs])
    tile = TR * LANES
    n = -(-flat.shape[0] // tile) * tile
    return jnp.pad(flat, (0, n - flat.shape[0])).reshape(-1, LANES)


def _unpack(vec, shapes):
    flat = vec.reshape(-1)
    out, o = [], 0
    for s in shapes:
        n = math.prod(s)
        out.append(flat[o:o + n].reshape(s))
        o += n
    return out


def kernel(x, p, norm_mix, w_in, sg_ln_g, sg_ln_b, sg_w, sg_b, sc_conv, gdn_conv, gdn_a_log, gdn_dt_bias, gdn_norm, out_norm_a, out_norm_b, w_o, norm_ffn, w_ff1, w_ff2, norm_ple, w_ple_gate, w_ple_proj, norm_final, loss_target, m_norm_mix, m_w_in, m_sg_ln_g, m_sg_ln_b, m_sg_w, m_sg_b, m_sc_conv, m_gdn_conv, m_gdn_a_log, m_gdn_dt_bias, m_gdn_norm, m_out_norm_a, m_out_norm_b, m_w_o, m_norm_ffn, m_w_ff1, m_w_ff2, m_norm_ple, m_w_ple_gate, m_w_ple_proj, m_norm_final, v_norm_mix, v_w_in, v_sg_ln_g, v_sg_ln_b, v_sg_w, v_sg_b, v_sc_conv, v_gdn_conv, v_gdn_a_log, v_gdn_dt_bias, v_gdn_norm, v_out_norm_a, v_out_norm_b, v_w_o, v_norm_ffn, v_w_ff1, v_w_ff2, v_norm_ple, v_w_ple_gate, v_w_ple_proj, v_norm_final):
    given = dict(locals())
    w = {n: given[n] for n in WEIGHTS}
    m = {n: given['m_' + n] for n in WEIGHTS}
    v = {n: given['v_' + n] for n in WEIGHTS}
    shard = 2 * lax.axis_index("x") + lax.axis_index("y")
    core = lax.axis_index("c")

    gathered = _all_gather([w[n].astype(BF16) for n in BIG] + [w[n] for n in CONVS])
    full = dict(zip(BIG + CONVS, gathered))
    small = {n: w[n] for n in SMALL if n not in CONVS}

    loss, grad_x, grads = _local_step(x[0], p[:, 0], loss_target[0], full, small)

    big = [grads[n] for n in BIG]
    theirs = _swap_layers(big)
    c_arr = core.reshape(1).astype(jnp.int32)
    parts = [_add_own_layer(g, o, c_arr) for g, o in zip(big, theirs)]
    slots = _scatter_shards(parts)
    reduced = [_sum_slots("rs_sum_shards", s) for s in slots]
    g_big = dict(zip(BIG, _join_layers(reduced)))

    rep = [n for n in SMALL if n not in CONVS]
    names = rep + CONVS
    vec = _pack([grads[n] for n in names] + [loss[0, :1]])
    total = _sum_slots("small_sum", _exchange_small(vec))
    parts_small = _unpack(total, [grads[n].shape for n in names] + [(1,)])
    g_small = dict(zip(names, parts_small[:-1]))
    loss_out = parts_small[-1].reshape(())
    for n in CONVS:
        width = w[n].shape[-1]
        g_small[n] = lax.dynamic_slice_in_dim(g_small[n], shard * width, width, axis=2)

    delta, new_m, new_v, grad_w = {}, {}, {}, {}
    for n in BIG:
        shp = w[n].shape
        two_d = lambda a: a.reshape(-1, shp[-1])
        d_, m_, v_ = _adamw(two_d(w[n]), two_d(g_big[n]), two_d(m[n]), two_d(v[n]))
        delta[n], new_m[n], new_v[n], grad_w[n] = d_.reshape(shp), m_.reshape(shp), v_.reshape(shp), g_big[n]
    shapes = [w[n].shape for n in SMALL]
    d_, m_, v_ = _adamw(_pack([w[n] for n in SMALL]), _pack([g_small[n] for n in SMALL]),
                        _pack([m[n] for n in SMALL]), _pack([v[n] for n in SMALL]))
    for n, dd, mm, vv in zip(SMALL, _unpack(d_, shapes), _unpack(m_, shapes), _unpack(v_, shapes)):
        delta[n], new_m[n], new_v[n], grad_w[n] = dd, mm, vv, g_small[n]

    return (loss_out, grad_x[None], *[grad_w[n] for n in WEIGHTS], *[delta[n] for n in WEIGHTS],
            *[new_m[n] for n in WEIGHTS], *[new_v[n] for n in WEIGHTS])
```

```python
import functools
import math

import jax
import jax.numpy as jnp
from jax import lax
from jax.experimental import pallas as pl
from jax.experimental.pallas import tpu as pltpu

F32 = jnp.float32
BF16 = jnp.bfloat16
HI = lax.Precision.HIGHEST
MESH = pl.DeviceIdType.MESH

LANES = 128
EPS = 1e-6
SG_HEADS, SG_CHUNK = 4, 128
SC_GROUPS, SC_KERNEL = 4, 3
GDN_HEADS, GDN_CONV = 8, 4
GDN_L = 128
GDN_CPS = 2
SG_W = SG_HEADS * LANES
SC_W = SC_GROUPS * LANES
GDN_W = GDN_HEADS * LANES
IN_COLS = 2 * SG_W + 3 * SC_W + 4 * GDN_W + 2 * GDN_HEADS
PROJ_W = 7168
CB_U, CB_V = 0, 4
CB_GB, CB_GC, CB_X = 8, 12, 16
CB_Q, CB_K, CB_VV, CB_Z, CB_AB = 20, 28, 36, 44, 52
N_SHARD = 4
N_DEV = 8
DEPTH = 2

ADAM_LR, ADAM_B1, ADAM_B2, ADAM_EPS, ADAM_WD, ADAM_STEP = 0.001, 0.9, 0.999, 1e-08, 0.01, 10

VMEM_LIMIT = 56 << 20

WEIGHTS = ['norm_mix', 'w_in', 'sg_ln_g', 'sg_ln_b', 'sg_w', 'sg_b', 'sc_conv', 'gdn_conv', 'gdn_a_log',
           'gdn_dt_bias', 'gdn_norm', 'out_norm_a', 'out_norm_b', 'w_o', 'norm_ffn', 'w_ff1', 'w_ff2', 'norm_ple',
           'w_ple_gate', 'w_ple_proj', 'norm_final']
BIG = ['w_in', 'w_o', 'w_ff1', 'w_ff2', 'w_ple_gate', 'w_ple_proj']
CONVS = ['sc_conv', 'gdn_conv']
SMALL = [n for n in WEIGHTS if n not in BIG]


def _cparams(sem=None):
    return pltpu.CompilerParams(dimension_semantics=sem, vmem_limit_bytes=VMEM_LIMIT)


def _dot(a, b, dims, prec=None):
    return lax.dot_general(a, b, (dims, ((), ())), precision=prec, preferred_element_type=F32)


def _mm(a, b):
    return _dot(a, b, ((1,), (0,)), HI)


def _mm_nt(a, b):
    return _dot(a, b, ((1,), (1,)), HI)


def _mm_tn(a, b):
    return _dot(a, b, ((0,), (0,)), HI)


@jax.custom_vjp
def _bmm(a, b):
    return _dot(a.astype(BF16), b.astype(BF16), ((1,), (0,)))


def _bmm_fwd(a, b):
    return _bmm(a, b), (a, b)


def _bmm_bwd(res, g):
    a, b = res
    gb = g.astype(BF16)
    return _dot(gb, b.astype(BF16), ((1,), (1,))), _dot(a.astype(BF16), gb, ((0,), (0,)))


_bmm.defvjp(_bmm_fwd, _bmm_bwd)


@jax.custom_vjp
def _bmm_tn(a, b):
    return _dot(a.astype(BF16), b.astype(BF16), ((0,), (0,)))


def _bmm_tn_fwd(a, b):
    return _bmm_tn(a, b), (a, b)


def _bmm_tn_bwd(res, g):
    a, b = res
    gb = g.astype(BF16)
    return _dot(b.astype(BF16), gb, ((1,), (1,))), _dot(a.astype(BF16), gb, ((1,), (0,)))


_bmm_tn.defvjp(_bmm_tn_fwd, _bmm_tn_bwd)


def _sigmoid(x):
    return 1.0 / (1.0 + jnp.exp(-x))


def _silu(x):
    return x * _sigmoid(x)


def _gelu(x):
    c = math.sqrt(2.0 / math.pi)
    return 0.5 * x * (1.0 + jnp.tanh(c * (x + 0.044715 * (x * x * x))))


def _softplus(x):
    return jnp.maximum(x, 0.0) + jnp.log(1.0 + jnp.exp(-jnp.abs(x)))


def _rms(x, g):
    return x * lax.rsqrt(jnp.mean(x * x, axis=-1, keepdims=True) + EPS) * g


def _roll_rows(x, shift):
    return pltpu.roll(x, shift % x.shape[0], 0)


@functools.partial(jax.custom_vjp, nondiff_argnums=(1,))
def _shift_down(x, j):
    row = lax.broadcasted_iota(jnp.int32, x.shape, 0)
    return jnp.where(row >= j, _roll_rows(x, j), 0.0)


def _shift_down_fwd(x, j):
    return _shift_down(x, j), None


def _shift_down_bwd(j, _, dy):
    row = lax.broadcasted_iota(jnp.int32, dy.shape, 0)
    return (jnp.where(row < dy.shape[0] - j, _roll_rows(dy, -j), 0.0),)


_shift_down.defvjp(_shift_down_fwd, _shift_down_bwd)


def _causal_conv(x, taps):
    k = len(taps)
    y = taps[k - 1] * x
    for j in range(k - 1):
        y = y + taps[j] * _shift_down(x, k - 1 - j)
    return y


class _In:
    def __init__(self, arr, block, imap, shared=False, gshape=None, gmap=None):
        self.arr, self.block, self.imap, self.shared = arr, block, imap, shared
        self.gshape = arr.shape if gshape is None else gshape
        self.gmap = imap if gmap is None else gmap


def _fused(name, fn, n, ins, outs, douts=None, need=None, gdt=None, add=None):
    n_in, n_out = len(ins), len(outs)
    in_specs = [pl.BlockSpec(s.block, s.imap) for s in ins]
    out_specs = [pl.BlockSpec(bs, im) for _, _, bs, im in outs]
    if douts is None:
        def body(*refs):
            res = fn(*[r[...] for r in refs[:n_in]])
            for r, v in zip(refs[n_in:], res):
                r[...] = v.astype(r.dtype)

        return pl.pallas_call(
            body, name=name, grid=(n,), in_specs=in_specs, out_specs=out_specs,
            out_shape=[jax.ShapeDtypeStruct(s, d) for s, d, _, _ in outs],
            compiler_params=_cparams(("arbitrary",)),
        )(*[s.arr for s in ins])

    gdt = list(gdt) if gdt is not None else [F32] * n_in
    add = dict(add or {})
    gidx = [i for i in range(n_in) if need[i]]
    aidx = [i for i in gidx if i in add]

    def body(*refs):
        in_refs, d_refs = refs[:n_in], refs[n_in:n_in + n_out]
        a_refs = dict(zip(aidx, refs[n_in + n_out:n_in + n_out + len(aidx)]))
        g_refs = refs[n_in + n_out + len(aidx):]
        vals = [r[...] for r in in_refs]

        def f(*dv):
            full = list(vals)
            for i, v in zip(gidx, dv):
                full[i] = v
            return tuple(o.astype(F32) for o in fn(*full))

        _, vjp = jax.vjp(f, *[vals[i].astype(F32) for i in gidx])
        grads = vjp(tuple(r[...].astype(F32) for r in d_refs))
        for i, g_ref, g in zip(gidx, g_refs, grads):
            if i in a_refs:
                g = g + a_refs[i][...].astype(F32)
            if ins[i].shared:
                period = n if ins[i].shared is True else ins[i].shared

                @pl.when(pl.program_id(0) % period == 0)
                def _():
                    g_ref[...] = jnp.zeros_like(g_ref)
                g_ref[...] += g.astype(g_ref.dtype)
            else:
                g_ref[...] = g.astype(g_ref.dtype)

    g_specs = [pl.BlockSpec(ins[i].block, ins[i].gmap) for i in gidx]
    g_shape = [jax.ShapeDtypeStruct(ins[i].gshape, gdt[i]) for i in gidx]
    res = pl.pallas_call(
        body, name=name, grid=(n,), in_specs=in_specs + out_specs + [g_specs[gidx.index(i)] for i in aidx],
        out_specs=g_specs, out_shape=g_shape,
        compiler_params=_cparams(("arbitrary",)),
    )(*[s.arr for s in ins], *douts, *[add[i] for i in aidx])
    full = [None] * n_in
    for i, g in zip(gidx, res):
        full[i] = g
    return full


def _row_in(a, tm, cb=None):
    if cb is None:
        return _In(a, (tm, a.shape[1]), lambda i: (i, 0))
    return _In(a, (tm, LANES), lambda i: (i, cb), gshape=(a.shape[0], LANES), gmap=lambda i: (i, 0))


def _row_shared(a):
    return _In(a, a.shape, lambda i: (0, 0), shared=True)


def _row_out(t, w, dt, tm):
    return ((t, w), dt, (tm, w), lambda i: (i, 0))


def _col_in(a, base, nblk):
    t = a.shape[0]
    return _In(a, (t, LANES), lambda i: (0, base + i), gshape=(t, nblk * LANES), gmap=lambda i: (0, i))


def _col_par(a):
    return _In(a, (None,) + a.shape[1:], lambda i: (i, 0, 0))


def _col_out(t, w, dt, base=0):
    return ((t, w), dt, (t, LANES), lambda i: (0, base + i))


def _matmul(name, a, b, mode, m, n, k, tm, tn, tk, b_spec=None, epilogue=None, extras=(), outs=None, out_dtype=F32):
    tm, tn, tk = min(tm, m), min(tn, n), min(tk, k)
    assert m % tm == 0 and n % tn == 0 and k % tk == 0, (name, m, n, k, tm, tn, tk)
    nk = k // tk
    a_spec = (pl.BlockSpec((tk, tm), lambda i, j, kk: (kk, i)) if mode == "tn"
              else pl.BlockSpec((tm, tk), lambda i, j, kk: (i, kk)))
    if b_spec is None:
        b_spec = (pl.BlockSpec((tn, tk), lambda i, j, kk: (j, kk)) if mode == "nt"
                  else pl.BlockSpec((tk, tn), lambda i, j, kk: (kk, j)))
    dims = {"nn": ((1,), (0,)), "nt": ((1,), (1,)), "tn": ((0,), (0,))}[mode]
    if outs is None:
        outs = [((m, n), out_dtype, (tm, tn), lambda i, j, kk: (i, j))]
    if epilogue is None:
        epilogue = lambda acc: (acc,)
    n_ex = len(extras)

    def body(*refs):
        a_ref, b_ref = refs[0], refs[1]
        ex_refs = refs[2:2 + n_ex]
        o_refs = refs[2 + n_ex:2 + n_ex + len(outs)]
        part = _dot(a_ref[...].astype(BF16), b_ref[...].astype(BF16), dims)

        def finish(acc):
            for r, v in zip(o_refs, epilogue(acc, *[e[...] for e in ex_refs])):
                r[...] = v.astype(r.dtype)

        if nk == 1:
            finish(part)
        else:
            acc_ref = refs[-1]
            kk = pl.program_id(2)

            @pl.when(kk == 0)
            def _():
                acc_ref[...] = part

            @pl.when(kk > 0)
            def _():
                acc_ref[...] += part

            @pl.when(kk == nk - 1)
            def _():
                finish(acc_ref[...])

    ex_specs = [pl.BlockSpec((tm, tn), lambda i, j, kk: (i, j)) for _ in extras]
    res = pl.pallas_call(
        body, name=name, grid=(m // tm, n // tn, nk),
        in_specs=[a_spec, b_spec] + ex_specs,
        out_specs=[pl.BlockSpec(bs, im) for _, _, bs, im in outs],
        out_shape=[jax.ShapeDtypeStruct(s, d) for s, d, _, _ in outs],
        scratch_shapes=[pltpu.VMEM((tm, tn), F32)] if nk > 1 else [],
        compiler_params=_cparams(("parallel", "parallel", "arbitrary")),
    )(a, b, *extras)
    return res if len(res) > 1 else res[0]


def _fn_sgu(u_pre, v_pre, ln_g, ln_b, w, bb, na):
    t = u_pre.shape[0]
    u = _gelu(u_pre)
    v = _gelu(v_pre)
    mu = jnp.mean(v, axis=-1, keepdims=True)
    vc = v - mu
    vh = vc * lax.rsqrt(jnp.mean(vc * vc, axis=-1, keepdims=True) + EPS) * ln_g + ln_b
    ri = lax.broadcasted_iota(jnp.int32, w.shape, 0)
    ci = lax.broadcasted_iota(jnp.int32, w.shape, 1)
    wc = jnp.where(ri >= ci, w, 0.0)
    f = jnp.concatenate([_bmm(wc, vh[c * SG_CHUNK:(c + 1) * SG_CHUNK]) + bb for c in range(t // SG_CHUNK)], axis=0)
    return (_rms(u * f, na),)


def _fn_sconv(gb, gc, xin, w0, w1, w2, nb):
    return (_rms(gb * _causal_conv(gc * xin, (w0, w1, w2)), nb),)


def _fn_gdn_qk(pre, w0, w1, w2, w3):
    a = _silu(_causal_conv(pre, (w0, w1, w2, w3)))
    return (a * lax.rsqrt(jnp.sum(a * a, axis=-1, keepdims=True) + EPS),)


def _fn_gdn_v(pre, w0, w1, w2, w3):
    return (_silu(_causal_conv(pre, (w0, w1, w2, w3))),)


def _fn_gdn_gates(ab, a_log, dt_bias):
    lane = lax.broadcasted_iota(jnp.int32, ab.shape, 1)
    g = -jnp.exp(a_log) * _softplus(ab + dt_bias)
    return (jnp.where(lane < GDN_HEADS, g, jnp.where(lane < 2 * GDN_HEADS, _sigmoid(ab), 0.0)),)


@jax.custom_vjp
def _solve_unit_lower(a):
    return _solve_unit_lower_impl(a)


def _solve_unit_lower_fwd(a):
    t = _solve_unit_lower_impl(a)
    return t, t


def _solve_unit_lower_bwd(t, dt):
    return (-_mm_tn(t, _mm_nt(dt, t)),)


_solve_unit_lower.defvjp(_solve_unit_lower_fwd, _solve_unit_lower_bwd)


@jax.custom_vjp
def _solved_unit_lower(a, t):
    return t


def _solved_unit_lower_fwd(a, t):
    return t, t


def _solved_unit_lower_bwd(t, dt):
    return _solve_unit_lower_bwd(t, dt)[0], jnp.zeros_like(t)


_solved_unit_lower.defvjp(_solved_unit_lower_fwd, _solved_unit_lower_bwd)


def _solve_unit_lower_impl(a):
    n = a.shape[0]
    ri = lax.broadcasted_iota(jnp.int32, (n, n), 0)
    ci = lax.broadcasted_iota(jnp.int32, (n, n), 1)
    t = (ri == ci).astype(F32)
    sh = 0
    while (1 << sh) < n:
        rb = jnp.right_shift(ri, sh)
        cb = jnp.right_shift(ci, sh)
        off = ((rb & 1) == 1) & (cb == rb - 1)
        t = t - _mm(_mm(t, jnp.where(off, a, 0.0)), t)
        sh += 1
    return t


def _gdn_wy_chunk(q, k, v, g, beta, t_saved=None):
    n, dk = q.shape
    ri = lax.broadcasted_iota(jnp.int32, (n, n), 0)
    ci = lax.broadcasted_iota(jnp.int32, (n, n), 1)
    incl = ri >= ci
    eye = (ri == ci).astype(F32)
    g_row = jnp.sum(eye * g, axis=0, keepdims=True)
    gc = jnp.sum(jnp.where(incl, g_row, 0.0), axis=1, keepdims=True)
    gc_row = jnp.sum(eye * gc, axis=0, keepdims=True)
    decay_incl = jnp.where(incl, jnp.exp(jnp.where(incl, gc - gc_row, 0.0)), 0.0)
    decay_strict = jnp.where(ri > ci, decay_incl, 0.0)
    q = q * (dk ** -0.5)
    kb = k * beta
    a = _mm_nt(kb, k) * decay_strict
    t = _solve_unit_lower(a) if t_saved is None else _solved_unit_lower(a, t_saved)
    eg = jnp.exp(gc)
    value = _mm(t, v * beta)
    kcd = _mm(t, kb * eg)
    intra = _mm_nt(q, k) * decay_incl
    last = lax.broadcasted_iota(jnp.int32, (n, 1), 0) == (n - 1)
    g_last = jnp.sum(jnp.where(last, gc, 0.0), axis=0, keepdims=True)
    carry = jnp.broadcast_to(jnp.exp(g_last), (8, LANES))
    return value, kcd, q * eg, k * jnp.exp(g_last - gc), intra, carry, t


def _fn_gdn_wy(q, k, v, gates, pick_g, pick_b, t_saved=None):
    outs = []
    for c in range(q.shape[0] // GDN_L):
        rows = slice(c * GDN_L, (c + 1) * GDN_L)
        g = jnp.sum(gates[rows] * pick_g, axis=1, keepdims=True)
        beta = jnp.sum(gates[rows] * pick_b, axis=1, keepdims=True)
        ts = None if t_saved is None else t_saved[rows]
        outs.append(_gdn_wy_chunk(q[rows], k[rows], v[rows], g, beta, ts))
    res = tuple(jnp.concatenate(parts, axis=0) for parts in zip(*outs))
    return res if t_saved is None else res[:-1]


def _gdn_step(s, value, kcd, qe, kt, intra, carry):
    v_new = value - _bmm(kcd, s)
    o = _bmm(qe, s) + _bmm(intra, v_new)
    return s * carry + _bmm_tn(kt, v_new), o


def _gdn_post(o, z, nrm):
    return _rms(o, nrm) * _silu(z)


def _gdn_wy_ins(q, k, v, gates, t_saved=None):
    t = q.shape[0]
    rb = min(GDN_CPS * GDN_L, t)
    hd = GDN_HEADS
    lane = jnp.arange(LANES)[None, None, :]
    pick_g = (lane == jnp.arange(hd)[:, None, None]).astype(F32)
    pick_b = (lane == jnp.arange(hd)[:, None, None] + hd).astype(F32)
    blk = lambda a: _In(a, (rb, LANES), lambda i: (i // hd, i % hd))
    par = lambda a: _In(a, (None, 1, LANES), lambda i: (i % hd, 0, 0))
    ins = [blk(q), blk(k), blk(v), _In(gates, (rb, LANES), lambda i: (i // hd, 0), shared=hd), par(pick_g),
           par(pick_b)]
    wide = lambda dt: ((t, GDN_W), dt, (rb, LANES), lambda i: (i // hd, i % hd))
    carry = ((hd, 8 * (t // GDN_L), LANES), F32, (None, 8 * (rb // GDN_L), LANES), lambda i: (i % hd, i // hd, 0))
    outs = [wide(F32), wide(BF16), wide(BF16), wide(BF16), wide(BF16), carry]
    if t_saved is None:
        outs.append(wide(F32))
    else:
        ins.append(blk(t_saved))
    return (t // rb) * hd, ins, outs


def _gdn_scan_specs(t):
    col = lambda base: pl.BlockSpec((t, LANES), lambda h: (0, base + h))
    carry = pl.BlockSpec((None, 8 * (t // GDN_L), LANES), lambda h: (h, 0, 0))
    par = pl.BlockSpec((None, 1, LANES), lambda h: (h, 0, 0))
    return col, carry, par


def _gdn_scan_fwd(wy, proj, nrm):
    t = wy[0].shape[0]
    nc = t // GDN_L

    def body(val_ref, kcd_ref, qe_ref, kt_ref, in_ref, cy_ref, z_ref, n_ref, y_ref, o_scr):
        def step(c, s):
            rows = pl.ds(pl.multiple_of(c * GDN_L, GDN_L), GDN_L)
            s, o = _gdn_step(s, val_ref[rows, :], kcd_ref[rows, :], qe_ref[rows, :], kt_ref[rows, :],
                             in_ref[rows, :], cy_ref[pl.ds(pl.multiple_of(c * 8, 8), 1), :])
            o_scr[rows, :] = o
            return s

        lax.fori_loop(0, nc, step, jnp.zeros((LANES, LANES), F32))
        y_ref[...] = _gdn_post(o_scr[...], z_ref[...], n_ref[...]).astype(y_ref.dtype)

    col, carry, par = _gdn_scan_specs(t)
    return pl.pallas_call(
        body, name="gdn_scan_fwd", grid=(GDN_HEADS,),
        in_specs=[col(0)] * 5 + [carry, col(CB_Z), par],
        out_specs=col(0), out_shape=jax.ShapeDtypeStruct((t, GDN_W), BF16),
        scratch_shapes=[pltpu.VMEM((t, LANES), F32)],
        compiler_params=_cparams(("arbitrary",)),
    )(*wy, proj, nrm)


def _gdn_scan_bwd(wy, proj, nrm, dy, dy_base):
    t = wy[0].shape[0]
    nc = t // GDN_L

    def body(val_ref, kcd_ref, qe_ref, kt_ref, in_ref, cy_ref, z_ref, n_ref, dy_ref,
             dval_ref, dkcd_ref, dqe_ref, dkt_ref, din_ref, dcy_ref, dz_ref, dn_ref, o_scr, s_scr):
        def operands(c):
            rows = pl.ds(pl.multiple_of(c * GDN_L, GDN_L), GDN_L)
            return rows, (val_ref[rows, :], kcd_ref[rows, :], qe_ref[rows, :], kt_ref[rows, :], in_ref[rows, :],
                          cy_ref[pl.ds(pl.multiple_of(c * 8, 8), 1), :])

        def step(c, s):
            rows, ops = operands(c)
            s_scr[c] = s
            s, o = _gdn_step(s, *ops)
            o_scr[rows, :] = o
            return s

        lax.fori_loop(0, nc, step, jnp.zeros((LANES, LANES), F32))
        _, vjp_post = jax.vjp(_gdn_post, o_scr[...], z_ref[...], n_ref[...])
        do, dz, dn = vjp_post(dy_ref[...].astype(F32))
        dz_ref[...] = dz.astype(dz_ref.dtype)
        dn_ref[...] = dn
        o_scr[...] = do
        dcy_ref[...] = jnp.zeros_like(dcy_ref)

        def rstep(i, ds):
            c = nc - 1 - i
            rows, ops = operands(c)
            _, vjp_c = jax.vjp(_gdn_step, s_scr[c], *[o.astype(F32) for o in ops])
            ds, dval, dkcd, dqe, dkt, din, dcy = vjp_c((ds, o_scr[rows, :]))
            for r, g in ((dval_ref, dval), (dkcd_ref, dkcd), (dqe_ref, dqe), (dkt_ref, dkt), (din_ref, din)):
                r[rows, :] = g.astype(r.dtype)
            dcy_ref[pl.ds(pl.multiple_of(c * 8, 8), 1), :] = dcy
            return ds

        lax.fori_loop(0, nc, rstep, jnp.zeros((LANES, LANES), F32))

    col, carry, par = _gdn_scan_specs(t)
    wide = jax.ShapeDtypeStruct((t, GDN_W), BF16)
    return pl.pallas_call(
        body, name="gdn_scan_bwd", grid=(GDN_HEADS,),
        in_specs=[col(0)] * 5 + [carry, col(CB_Z), par, col(dy_base)],
        out_specs=[col(0)] * 5 + [carry, col(0), par],
        out_shape=[wide] * 5 + [jax.ShapeDtypeStruct(wy[5].shape, F32), wide,
                                jax.ShapeDtypeStruct((GDN_HEADS, 1, LANES), F32)],
        scratch_shapes=[pltpu.VMEM((t, LANES), F32), pltpu.VMEM((nc, LANES, LANES), F32)],
        compiler_params=_cparams(("arbitrary",)),
    )(*wy, proj, nrm, dy)


TM = 512
TR = 256


def _rms_fwd(name, h, g):
    t, d = h.shape
    tm = min(TR, t)
    return _fused(name, lambda hb, gb: (_rms(hb, gb),), t // tm, [_row_in(h, tm), _row_shared(g)],
                  [_row_out(t, d, BF16, tm)])[0]


def _rms_bwd(name, h, g, dxn, dh_next):
    t, d = h.shape
    tm = min(TR, t)
    dh, dg = _fused(name, lambda hb, gb: (_rms(hb, gb),), t // tm, [_row_in(h, tm), _row_shared(g)],
                    [_row_out(t, d, F32, tm)], douts=[dxn], need=[True, True], add={0: dh_next})
    return dh, dg


def _mixer_ins(proj, lw):
    sgu = [_col_in(proj, CB_U, SG_HEADS), _col_in(proj, CB_V, SG_HEADS), _col_par(lw['sg_ln_g']),
           _col_par(lw['sg_ln_b']), _col_par(lw['sg_w']), _col_par(lw['sg_bb']), _col_par(lw['out_norm_a'])]
    sconv = [_col_in(proj, CB_GB, SC_GROUPS), _col_in(proj, CB_GC, SC_GROUPS), _col_in(proj, CB_X, SC_GROUPS)] + \
            [_col_par(w) for w in lw['sc_taps']] + [_col_par(lw['out_norm_b'])]
    gq = [_col_in(proj, CB_Q, GDN_HEADS)] + [_col_par(w) for w in lw['q_taps']]
    gk = [_col_in(proj, CB_K, GDN_HEADS)] + [_col_par(w) for w in lw['k_taps']]
    gv = [_col_in(proj, CB_VV, GDN_HEADS)] + [_col_par(w) for w in lw['v_taps']]
    return sgu, sconv, gq, gk, gv


def _gates_ins(proj, lw, tm):
    return [_row_in(proj, tm, CB_AB), _row_shared(lw['a_log_row']), _row_shared(lw['dt_bias_row'])]


def _layer_fwd(h, p_l, lw):
    t, d = h.shape
    dff = lw['w_ff2'].shape[0]
    xn = _rms_fwd("rms_fwd", h, lw['norm_mix'])
    proj = _matmul("proj_fwd", xn, lw['w_in'], "nn", t, PROJ_W, d, TM, 1024, d)
    sgu, sconv, gq, gk, gv = _mixer_ins(proj, lw)
    ya = _fused("sgu_fwd", _fn_sgu, SG_HEADS, sgu, [_col_out(t, SG_W, BF16)])[0]
    yb = _fused("sconv_fwd", _fn_sconv, SC_GROUPS, sconv, [_col_out(t, SC_W, BF16)])[0]
    q = _fused("gdn_q_fwd", _fn_gdn_qk, GDN_HEADS, gq, [_col_out(t, GDN_W, F32)])[0]
    k = _fused("gdn_k_fwd", _fn_gdn_qk, GDN_HEADS, gk, [_col_out(t, GDN_W, F32)])[0]
    v = _fused("gdn_v_fwd", _fn_gdn_v, GDN_HEADS, gv, [_col_out(t, GDN_W, F32)])[0]
    tm = min(TR, t)
    gates = _fused("gdn_gates_fwd", _fn_gdn_gates, t // tm, _gates_ins(proj, lw, tm),
                   [_row_out(t, LANES, F32, tm)])[0]
    n_wy, wy_ins, wy_outs = _gdn_wy_ins(q, k, v, gates)
    wy = _fused("gdn_wy_fwd", _fn_gdn_wy, n_wy, wy_ins, wy_outs)
    wy, wy_t = wy[:6], wy[6]
    yc = _gdn_scan_fwd(wy, proj, lw['gdn_norm'])
    ycat = jnp.concatenate([ya, yb, yc], axis=1)
    h2 = _matmul("wo_fwd", ycat, lw['w_o'], "nn", t, d, d, TM, 1024, d,
                 epilogue=lambda acc, hb: (hb + acc,), extras=(h,))
    hn = _rms_fwd("rms_fwd", h2, lw['norm_ffn'])
    per = d // 1024
    s, r = _matmul("ff1_fwd", hn, lw['w_ff1'], "nn", t, dff, d, TM, 1024, d,
                   b_spec=pl.BlockSpec((None, d, 1024), lambda i, j, kk: (j // per, 0, j % per)),
                   epilogue=lambda acc: (jnp.maximum(acc, 0.0), jnp.square(jnp.maximum(acc, 0.0))),
                   outs=[((t, dff), BF16, (min(TM, t), 1024), lambda i, j, kk: (i, j))] * 2)
    h3 = _matmul("ff2_fwd", r, lw['w_ff2'], "nn", t, d, dff, TM, 1024, 2048,
                 epilogue=lambda acc, hb: (hb + acc,), extras=(h2,))
    hn2 = _rms_fwd("rms_fwd", h3, lw['norm_ple'])
    pp = _matmul("ple_proj_fwd", p_l, lw['w_ple_proj'], "nn", t, d, p_l.shape[1], TM, 1024, p_l.shape[1])

    def gate_epilogue(acc, hb, ppb):
        sg = _sigmoid(acc)
        return hb + ppb * sg, sg

    h4, gate = _matmul("ple_gate_fwd", hn2, lw['w_ple_gate'], "nn", t, d, d, TM, 1024, d, epilogue=gate_epilogue,
                       extras=(h3, pp), outs=[((t, d), F32, (min(TM, t), 1024), lambda i, j, kk: (i, j))] * 2)
    saved = dict(h=h, xn=xn, proj=proj, q=q, k=k, v=v, gates=gates, ycat=ycat, h2=h2, hn=hn, s=s, r=r, h3=h3,
                 hn2=hn2, pp=pp, gate=gate, p=p_l, wy=wy, wy_t=wy_t)
    return h4, saved


def _layer_bwd(dh4, sv, lw):
    t, d = dh4.shape
    dff = lw['w_ff2'].shape[0]
    tm = min(TR, t)
    g = {}
    dacc, dpp = _fused("ple_bwd_gate", lambda dh, pp, gt: (dh * pp * gt * (1.0 - gt), dh * gt), t // tm,
                       [_row_in(dh4, tm), _row_in(sv['pp'], tm), _row_in(sv['gate'], tm)],
                       [_row_out(t, d, BF16, tm)] * 2)
    g['w_ple_gate'] = _matmul("dw_ple_gate", sv['hn2'], dacc, "tn", d, d, t, TM, 1024, 1024, out_dtype=BF16)
    g['w_ple_proj'] = _matmul("dw_ple_proj", sv['p'], dpp, "tn", sv['p'].shape[1], d, t, TM, 1024, 1024,
                              out_dtype=BF16)
    dhn2 = _matmul("dx_ple_gate", dacc, lw['w_ple_gate'], "nt", t, d, d, TM, 1024, d)
    dh3, g['norm_ple'] = _rms_bwd("rms_bwd", sv['h3'], lw['norm_ple'], dhn2, dh4)
    da = _matmul("dx_ff2", dh3, lw['w_ff2'], "nt", t, dff, d, TM, 1024, d,
                 epilogue=lambda acc, sb: (acc * (2.0 * sb.astype(F32)),), extras=(sv['s'],),
                 outs=[((t, dff), BF16, (min(TM, t), 1024), lambda i, j, kk: (i, j))])
    g['w_ff2'] = _matmul("dw_ff2", sv['r'], dh3, "tn", dff, d, t, TM, 1024, 1024, out_dtype=BF16)
    per = d // 1024
    dhn = _matmul("dx_ff1", da, lw['w_ff1'], "nt", t, d, dff, TM, 1024, 1024,
                  b_spec=pl.BlockSpec((None, 1024, 1024), lambda i, j, kk: (kk // per, j, kk % per)))
    g['w_ff1'] = _matmul("dw_ff1", sv['hn'], da, "tn", d, dff, t, TM, 1024, 1024,
                         outs=[((N_SHARD, d, d), BF16, (None, TM, 1024), lambda i, j, kk: (j // per, i, j % per))])
    dh2, g['norm_ffn'] = _rms_bwd("rms_bwd", sv['h2'], lw['norm_ffn'], dhn, dh3)
    dycat = _matmul("dx_o", dh2, lw['w_o'], "nt", t, d, d, TM, 1024, d)
    g['w_o'] = _matmul("dw_o", sv['ycat'], dh2, "tn", d, d, t, TM, 1024, 1024, out_dtype=BF16)
    proj = sv['proj']
    sgu, sconv, gq, gk, gv = _mixer_ins(proj, lw)
    bf2 = [BF16, BF16]
    r_ = _fused("sgu_bwd", _fn_sgu, SG_HEADS, sgu, [_col_out(t, d, F32, 0)], douts=[dycat], need=[True] * 7,
                gdt=bf2 + [F32] * 5)
    du, dv_, g['sg_ln_g'], g['sg_ln_b'], g['sg_w'], g['sg_bb'], g['out_norm_a'] = r_
    r_ = _fused("sconv_bwd", _fn_sconv, SC_GROUPS, sconv, [_col_out(t, d, F32, SG_HEADS)], douts=[dycat],
                need=[True] * 7, gdt=[BF16] * 3 + [F32] * 4)
    dgb, dgc, dxin = r_[:3]
    g['sc_taps'], g['out_norm_b'] = r_[3:6], r_[6]
    r_ = _gdn_scan_bwd(sv['wy'], proj, lw['gdn_norm'], dycat, SG_HEADS + SC_GROUPS)
    dwy, dz, g['gdn_norm'] = r_[:6], r_[6], r_[7]
    n_wy, wy_ins, wy_outs = _gdn_wy_ins(sv['q'], sv['k'], sv['v'], sv['gates'], sv['wy_t'])
    dq, dk, dvv, dgates = _fused("gdn_wy_bwd", _fn_gdn_wy, n_wy, wy_ins, wy_outs, douts=dwy,
                                 need=[True] * 4 + [False] * 3)[:4]
    one = [_col_out(t, GDN_W, F32)]
    r_ = _fused("gdn_q_bwd", _fn_gdn_qk, GDN_HEADS, gq, one, douts=[dq], need=[True] * 5, gdt=[BF16] + [F32] * 4)
    dpq, g['q_taps'] = r_[0], r_[1:]
    r_ = _fused("gdn_k_bwd", _fn_gdn_qk, GDN_HEADS, gk, one, douts=[dk], need=[True] * 5, gdt=[BF16] + [F32] * 4)
    dpk, g['k_taps'] = r_[0], r_[1:]
    r_ = _fused("gdn_v_bwd", _fn_gdn_v, GDN_HEADS, gv, one, douts=[dvv], need=[True] * 5, gdt=[BF16] + [F32] * 4)
    dpv, g['v_taps'] = r_[0], r_[1:]
    dab, g['a_log_row'], g['dt_bias_row'] = _fused(
        "gdn_gates_bwd", _fn_gdn_gates, t // tm, _gates_ins(proj, lw, tm), [_row_out(t, LANES, F32, tm)],
        douts=[dgates], need=[True] * 3, gdt=[BF16, F32, F32])
    pad = jnp.zeros((t, PROJ_W - (CB_AB + 1) * LANES), BF16)
    dproj = jnp.concatenate([du, dv_, dgb, dgc, dxin, dpq, dpk, dpv, dz, dab, pad], axis=1)
    dxn = _matmul("dx_in", dproj, lw['w_in'], "nt", t, d, PROJ_W, TM, 1024, 1792)
    g['w_in'] = _matmul("dw_in", sv['xn'], dproj, "tn", d, PROJ_W, t, TM, 1024, 1024, out_dtype=BF16)
    dh, g['norm_mix'] = _rms_bwd("rms_bwd", sv['h'], lw['norm_mix'], dxn, dh2)
    return dh, g


def _loss_grad(h, g, tgt):
    t, d = h.shape
    tm = min(TR, t)

    def body(h_ref, g_ref, t_ref, loss_ref, dh_ref, dg_ref):
        y, vjp = jax.vjp(_rms, h_ref[...], g_ref[...])
        e = y - t_ref[...]
        dh, dg = vjp(e * (1.0 / d))

        @pl.when(pl.program_id(0) == 0)
        def _():
            loss_ref[...] = jnp.zeros_like(loss_ref)
            dg_ref[...] = jnp.zeros_like(dg_ref)

        loss_ref[...] += jnp.sum(jnp.sum(e * e, axis=1, keepdims=True), axis=0, keepdims=True) * (0.5 / d)
        dh_ref[...] = dh
        dg_ref[...] += dg

    row = pl.BlockSpec((tm, d), lambda i: (i, 0))
    return pl.pallas_call(
        body, name="loss_grad", grid=(t // tm,),
        in_specs=[row, pl.BlockSpec((1, d), lambda i: (0, 0)), row],
        out_specs=[pl.BlockSpec((1, LANES), lambda i: (0, 0)), row, pl.BlockSpec((1, d), lambda i: (0, 0))],
        out_shape=[jax.ShapeDtypeStruct((1, LANES), F32), jax.ShapeDtypeStruct((t, d), F32),
                   jax.ShapeDtypeStruct((1, d), F32)],
        compiler_params=_cparams(("arbitrary",)),
    )(h, g, tgt)


def _layer_weights(l, full, small):
    d = full['w_o'].shape[-1]
    lw = {}
    w_in = jnp.transpose(full['w_in'][l], (1, 0, 2)).reshape(d, IN_COLS)
    lw['w_in'] = jnp.pad(w_in, ((0, 0), (0, PROJ_W - IN_COLS)))
    lw['w_o'] = full['w_o'][l].reshape(d, d)
    lw['w_ff1'] = full['w_ff1'][l]
    lw['w_ff2'] = full['w_ff2'][l].reshape(-1, d)
    lw['w_ple_gate'] = full['w_ple_gate'][l].reshape(d, d)
    wpp = full['w_ple_proj'][l]
    lw['w_ple_proj'] = jnp.transpose(wpp, (1, 0, 2)).reshape(wpp.shape[1], d)
    for n in ('norm_mix', 'norm_ffn', 'norm_ple'):
        lw[n] = small[n][l].reshape(1, d)
    lw['sg_ln_g'] = small['sg_ln_g'][l].reshape(SG_HEADS, 1, LANES)
    lw['sg_ln_b'] = small['sg_ln_b'][l].reshape(SG_HEADS, 1, LANES)
    lw['sg_w'] = small['sg_w'][l]
    lw['sg_bb'] = jnp.broadcast_to(small['sg_b'][l][:, :, None], (SG_HEADS, SG_CHUNK, LANES))
    lw['out_norm_a'] = small['out_norm_a'][l].reshape(SG_HEADS, 1, LANES)
    lw['out_norm_b'] = small['out_norm_b'][l].reshape(SC_GROUPS, 1, LANES)
    lw['gdn_norm'] = jnp.broadcast_to(small['gdn_norm'][l].reshape(1, 1, LANES), (GDN_HEADS, 1, LANES))
    lw['a_log_row'] = jnp.pad(small['gdn_a_log'][l].reshape(1, GDN_HEADS), ((0, 0), (0, LANES - GDN_HEADS)))
    lw['dt_bias_row'] = jnp.pad(small['gdn_dt_bias'][l].reshape(1, GDN_HEADS), ((0, 0), (0, LANES - GDN_HEADS)))
    sc = full['sc_conv'][l]
    lw['sc_taps'] = [sc[:, j:j + 1, :] for j in range(SC_KERNEL)]
    gc = jnp.transpose(full['gdn_conv'][l], (1, 0, 2)).reshape(GDN_CONV, 3 * GDN_W)
    for i, nm in enumerate(('q_taps', 'k_taps', 'v_taps')):
        part = gc[:, i * GDN_W:(i + 1) * GDN_W].reshape(GDN_CONV, GDN_HEADS, 1, LANES)
        lw[nm] = [part[j] for j in range(GDN_CONV)]
    return lw


def _weight_grads(g):
    d = g['w_o'].shape[0]
    out = {}
    w_in = g['w_in'][:, :IN_COLS].reshape(d, N_SHARD, IN_COLS // N_SHARD)
    out['w_in'] = jnp.transpose(w_in, (1, 0, 2))
    out['w_o'] = g['w_o'].reshape(N_SHARD, d // N_SHARD, d)
    out['w_ff1'] = g['w_ff1']
    out['w_ff2'] = g['w_ff2'].reshape(N_SHARD, -1, d)
    out['w_ple_gate'] = g['w_ple_gate'].reshape(N_SHARD, d // N_SHARD, d)
    wpp = g['w_ple_proj']
    out['w_ple_proj'] = jnp.transpose(wpp.reshape(wpp.shape[0], N_SHARD, d // N_SHARD), (1, 0, 2))
    for n in ('norm_mix', 'norm_ffn', 'norm_ple'):
        out[n] = g[n].reshape(d)
    out['sg_ln_g'] = g['sg_ln_g'].reshape(SG_W)
    out['sg_ln_b'] = g['sg_ln_b'].reshape(SG_W)
    out['sg_w'] = g['sg_w']
    out['sg_b'] = jnp.sum(g['sg_bb'], axis=2)
    out['out_norm_a'] = g['out_norm_a'].reshape(SG_W)
    out['out_norm_b'] = g['out_norm_b'].reshape(SC_W)
    out['gdn_norm'] = jnp.sum(g['gdn_norm'], axis=(0, 1))
    out['gdn_a_log'] = g['a_log_row'][0, :GDN_HEADS]
    out['gdn_dt_bias'] = g['dt_bias_row'][0, :GDN_HEADS]
    out['sc_conv'] = jnp.concatenate([w.reshape(1, SC_W) for w in g['sc_taps']], axis=0)
    taps = [jnp.concatenate([g[nm][j].reshape(1, GDN_W) for nm in ('q_taps', 'k_taps', 'v_taps')], axis=1)
            for j in range(GDN_CONV)]
    out['gdn_conv'] = jnp.concatenate(taps, axis=0)
    return out


def _local_step(x, p, tgt, full, small):
    h = x
    saved, lws = [], []
    for l in range(DEPTH):
        lw = _layer_weights(l, full, small)
        h, sv = _layer_fwd(h, p[l], lw)
        saved.append(sv)
        lws.append(lw)
    loss, dh, dnf = _loss_grad(h, small['norm_final'].reshape(1, -1), tgt)
    per_layer = [None] * DEPTH
    for l in reversed(range(DEPTH)):
        dh, g = _layer_bwd(dh, saved[l], lws[l])
        per_layer[l] = _weight_grads(g)
    grads = {n: jnp.stack([per_layer[l][n] for l in range(DEPTH)], axis=0) for n in per_layer[0]}
    grads['norm_final'] = dnf.reshape(-1)
    return loss, dh, grads


ANY = pl.BlockSpec(memory_space=pl.ANY)


def _place():
    x, y, c = lax.axis_index("x"), lax.axis_index("y"), lax.axis_index("c")
    chips = [(1 - x, y), (x, 1 - y), (1 - x, 1 - y)]
    return x, y, c, chips


def _place_shard(wsh, shard, dtype):
    dp, r, cc = wsh.shape
    tr = min(TR, r)
    nb = r // tr

    def body(idx_ref, w_ref, o_ref):
        o_ref[...] = w_ref[...].astype(o_ref.dtype)

    return pl.pallas_call(
        body, name="ag_place_shard",
        grid_spec=pltpu.PrefetchScalarGridSpec(
            num_scalar_prefetch=1, grid=(dp * nb,),
            in_specs=[pl.BlockSpec((None, tr, cc), lambda i, ix: (i // nb, i % nb, 0))],
            out_specs=pl.BlockSpec((None, None, tr, cc), lambda i, ix: (i // nb, ix[0], i % nb, 0))),
        out_shape=jax.ShapeDtypeStruct((dp, N_SHARD, r, cc), dtype),
        compiler_params=_cparams(("arbitrary",)),
    )(shard, wsh)


def _all_gather(bufs):
    n = len(bufs)

    def body(*refs):
        outs = refs[n:2 * n]
        send_sems, recv_sems = refs[2 * n:]
        x, y, c, chips = _place()
        me = 2 * x + y
        sibling = (x, y, 1 - c)

        def cp(w, k, shard, layer, to):
            blk = outs[w].at[layer, shard]
            return pltpu.make_async_remote_copy(src_ref=blk, dst_ref=blk, send_sem=send_sems.at[w, k],
                                                recv_sem=recv_sems.at[w, k], device_id=to, device_id_type=MESH)

        sends = []
        for w in range(n):
            for j, chip in enumerate(chips):
                s = cp(w, j, me, c, (*chip, c))
                s.start()
                sends.append(s)
        for w in range(n):
            for j, (px, py) in enumerate(chips):
                cp(w, j, 2 * px + py, c, sibling).wait_recv()
                s = cp(w, 3 + j, 2 * px + py, c, sibling)
                s.start()
                sends.append(s)
        for w in range(n):
            for j, (px, py) in enumerate(chips):
                cp(w, 3 + j, 2 * px + py, 1 - c, sibling).wait_recv()
        for s in sends:
            s.wait_send()

    return pl.pallas_call(
        body, name="all_gather",
        in_specs=[ANY] * n, out_specs=[ANY] * n,
        out_shape=[jax.ShapeDtypeStruct(b.shape, b.dtype) for b in bufs],
        input_output_aliases={w: w for w in range(n)},
        scratch_shapes=[pltpu.SemaphoreType.DMA((n, 6)), pltpu.SemaphoreType.DMA((n, 6))],
    )(*bufs)


def _swap_layers(grads):
    n = len(grads)

    def body(*refs):
        srcs, outs = refs[:n], refs[n:2 * n]
        send_sems, recv_sems = refs[2 * n:]
        x, y, c, _ = _place()
        cps = [pltpu.make_async_remote_copy(src_ref=srcs[w].at[1 - c], dst_ref=outs[w], send_sem=send_sems.at[w],
                                            recv_sem=recv_sems.at[w], device_id=(x, y, 1 - c), device_id_type=MESH)
               for w in range(n)]
        for cpy in cps:
            cpy.start()
        for cpy in cps:
            cpy.wait()

    return pl.pallas_call(
        body, name="rs_swap_layers", in_specs=[ANY] * n, out_specs=[ANY] * n,
        out_shape=[jax.ShapeDtypeStruct(g.shape[1:], g.dtype) for g in grads],
        scratch_shapes=[pltpu.SemaphoreType.DMA((n,)), pltpu.SemaphoreType.DMA((n,))],
    )(*grads)


def _scatter_shards(parts):
    n = len(parts)

    def body(*refs):
        srcs, outs = refs[:n], refs[n:2 * n]
        send_sems, recv_sems = refs[2 * n:]
        x, y, c, chips = _place()

        def cp(w, j, src_shard, to):
            return pltpu.make_async_remote_copy(
                src_ref=srcs[w].at[src_shard], dst_ref=outs[w].at[j], send_sem=send_sems.at[w, j],
                recv_sem=recv_sems.at[w, j], device_id=to, device_id_type=MESH)

        sends = [cp(w, j, 2 * px + py, (px, py, c)) for w in range(n) for j, (px, py) in enumerate(chips)]
        for s in sends:
            s.start()
        for s in sends:
            s.wait()

    return pl.pallas_call(
        body, name="rs_scatter", in_specs=[ANY] * n, out_specs=[ANY] * n,
        out_shape=[jax.ShapeDtypeStruct((3,) + g.shape[1:], g.dtype) for g in parts],
        scratch_shapes=[pltpu.SemaphoreType.DMA((n, 3)), pltpu.SemaphoreType.DMA((n, 3))],
    )(*parts)


def _sum_reduced(part, slots, shard, core):
    _, r, cc = part.shape
    tr = min(TR, r)

    def body(shard_ref, core_ref, p_ref, s0_ref, s1_ref, s2_ref, o_ref):
        acc = p_ref[...].astype(F32)
        for s_ref in (s0_ref, s1_ref, s2_ref):
            acc = acc + s_ref[...].astype(F32)
        o_ref[...] = acc

    slot = lambda j: pl.BlockSpec((None, tr, cc), lambda i, sh, co: (j, i, 0))
    return pl.pallas_call(
        body, name="rs_sum_reduced",
        grid_spec=pltpu.PrefetchScalarGridSpec(
            num_scalar_prefetch=2, grid=(r // tr,),
            in_specs=[pl.BlockSpec((None, tr, cc), lambda i, sh, co: (sh[0], i, 0)), slot(0), slot(1), slot(2)],
            out_specs=pl.BlockSpec((None, tr, cc), lambda i, sh, co: (co[0], i, 0))),
        out_shape=jax.ShapeDtypeStruct((DEPTH, r, cc), F32),
        compiler_params=_cparams(("arbitrary",)),
    )(shard, core, part, slots, slots, slots)


def _join_layers(bufs):
    n = len(bufs)

    def body(*refs):
        outs = refs[n:2 * n]
        send_sems, recv_sems = refs[2 * n:]
        x, y, c, _ = _place()
        sends = [pltpu.make_async_remote_copy(src_ref=outs[w].at[c], dst_ref=outs[w].at[c], send_sem=send_sems.at[w],
                                              recv_sem=recv_sems.at[w], device_id=(x, y, 1 - c), device_id_type=MESH)
                 for w in range(n)]
        for s in sends:
            s.start()
        for w in range(n):
            blk = outs[w].at[1 - c]
            pltpu.make_async_remote_copy(src_ref=blk, dst_ref=blk, send_sem=send_sems.at[w], recv_sem=recv_sems.at[w],
                                         device_id=(x, y, 1 - c), device_id_type=MESH).wait_recv()
        for s in sends:
            s.wait_send()

    return pl.pallas_call(
        body, name="rs_join_layers", in_specs=[ANY] * n, out_specs=[ANY] * n,
        out_shape=[jax.ShapeDtypeStruct(b.shape, b.dtype) for b in bufs],
        input_output_aliases={w: w for w in range(n)},
        scratch_shapes=[pltpu.SemaphoreType.DMA((n,)), pltpu.SemaphoreType.DMA((n,))],
    )(*bufs)


def _exchange_small(vec):
    def body(src, out, send_sems, recv_sems, local_sem):
        x, y, c, _ = _place()
        me = 4 * x + 2 * y + c
        lc = pltpu.make_async_copy(src, out.at[me], local_sem)
        lc.start()
        sends = []
        for k in range(1, N_DEV):
            fx, fy, fc = (k >> 2) & 1, (k >> 1) & 1, k & 1
            to = (x ^ fx, y ^ fy, c ^ fc)
            s = pltpu.make_async_remote_copy(src_ref=src, dst_ref=out.at[me], send_sem=send_sems.at[k - 1],
                                             recv_sem=recv_sems.at[k - 1], device_id=to, device_id_type=MESH)
            s.start()
            sends.append(s)
        for k in range(1, N_DEV):
            fx, fy, fc = (k >> 2) & 1, (k >> 1) & 1, k & 1
            frm = 4 * (x ^ fx) + 2 * (y ^ fy) + (c ^ fc)
            pltpu.make_async_remote_copy(src_ref=src, dst_ref=out.at[frm], send_sem=send_sems.at[k - 1],
                                         recv_sem=recv_sems.at[k - 1], device_id=(x ^ fx, y ^ fy, c ^ fc),
                                         device_id_type=MESH).wait_recv()
        for s in sends:
            s.wait_send()
        lc.wait()

    return pl.pallas_call(
        body, name="small_exchange", in_specs=[ANY], out_specs=ANY,
        out_shape=jax.ShapeDtypeStruct((N_DEV,) + vec.shape, vec.dtype),
        scratch_shapes=[pltpu.SemaphoreType.DMA((N_DEV - 1,)), pltpu.SemaphoreType.DMA((N_DEV - 1,)),
                        pltpu.SemaphoreType.DMA],
    )(vec)


def _add_own_layer(g, other, c_arr):
    _, ns, r, cc = g.shape
    tr = min(TR, r)
    nb = r // tr

    def body(c_ref, g_ref, o_ref, out_ref):
        out_ref[...] = (g_ref[...].astype(F32) + o_ref[...].astype(F32)).astype(out_ref.dtype)

    slab = pl.BlockSpec((None, tr, cc), lambda i, cr: (i // nb, i % nb, 0))
    return pl.pallas_call(
        body, name="rs_add_own_layer",
        grid_spec=pltpu.PrefetchScalarGridSpec(
            num_scalar_prefetch=1, grid=(ns * nb,),
            in_specs=[pl.BlockSpec((None, None, tr, cc), lambda i, cr: (cr[0], i // nb, i % nb, 0)), slab],
            out_specs=slab),
        out_shape=jax.ShapeDtypeStruct(other.shape, other.dtype),
        compiler_params=_cparams(("arbitrary",)),
    )(c_arr, g, other)


def _sum_slots(name, a):
    ns, r, cc = a.shape
    tr = min(TR, r)
    ins = [_In(a, (None, tr, cc), functools.partial(lambda s, i: (s, i, 0), s)) for s in range(ns)]

    def fn(*blocks):
        acc = blocks[0].astype(F32)
        for b in blocks[1:]:
            acc = acc + b.astype(F32)
        return (acc,)

    return _fused(name, fn, r // tr, ins, [((r, cc), F32, (tr, cc), lambda i: (i, 0))])[0]


def _adamw_fn(w, g, m, v):
    m = ADAM_B1 * m + (1.0 - ADAM_B1) * g
    v = ADAM_B2 * v + (1.0 - ADAM_B2) * jnp.square(g)
    m_hat = m / (1.0 - ADAM_B1 ** ADAM_STEP)
    v_hat = v / (1.0 - ADAM_B2 ** ADAM_STEP)
    delta = -ADAM_LR * (m_hat / (jnp.sqrt(v_hat) + ADAM_EPS) + ADAM_WD * w)
    return delta, m, v


def _adamw(w, g, m, v):
    r, cc = w.shape
    tr = min(TR, r)
    ins = [_In(a, (tr, cc), lambda i: (i, 0)) for a in (w, g, m, v)]
    return _fused("adamw", _adamw_fn, r // tr, ins, [((r, cc), F32, (tr, cc), lambda i: (i, 0))] * 3)


def _pack(arrs):
    flat = jnp.concatenate([a.reshape(-1) for a in arrs])
    tile = TR * LANES
    n = -(-flat.shape[0] // tile) * tile
    return jnp.pad(flat, (0, n - flat.shape[0])).reshape(-1, LANES)


def _unpack(vec, shapes):
    flat = vec.reshape(-1)
    out, o = [], 0
    for s in shapes:
        n = math.prod(s)
        out.append(flat[o:o + n].reshape(s))
        o += n
    return out


def kernel(x, p, norm_mix, w_in, sg_ln_g, sg_ln_b, sg_w, sg_b, sc_conv, gdn_conv, gdn_a_log, gdn_dt_bias, gdn_norm, out_norm_a, out_norm_b, w_o, norm_ffn, w_ff1, w_ff2, norm_ple, w_ple_gate, w_ple_proj, norm_final, loss_target, m_norm_mix, m_w_in, m_sg_ln_g, m_sg_ln_b, m_sg_w, m_sg_b, m_sc_conv, m_gdn_conv, m_gdn_a_log, m_gdn_dt_bias, m_gdn_norm, m_out_norm_a, m_out_norm_b, m_w_o, m_norm_ffn, m_w_ff1, m_w_ff2, m_norm_ple, m_w_ple_gate, m_w_ple_proj, m_norm_final, v_norm_mix, v_w_in, v_sg_ln_g, v_sg_ln_b, v_sg_w, v_sg_b, v_sc_conv, v_gdn_conv, v_gdn_a_log, v_gdn_dt_bias, v_gdn_norm, v_out_norm_a, v_out_norm_b, v_w_o, v_norm_ffn, v_w_ff1, v_w_ff2, v_norm_ple, v_w_ple_gate, v_w_ple_proj, v_norm_final):
    given = dict(locals())
    w = {n: given[n] for n in WEIGHTS}
    m = {n: given['m_' + n] for n in WEIGHTS}
    v = {n: given['v_' + n] for n in WEIGHTS}
    shard = 2 * lax.axis_index("x") + lax.axis_index("y")
    core = lax.axis_index("c")

    shard_arr = shard.reshape(1).astype(jnp.int32)
    c_arr = core.reshape(1).astype(jnp.int32)
    gathered = _all_gather([_place_shard(w[n].reshape((DEPTH, -1, w[n].shape[-1])), shard_arr,
                                         BF16 if n in BIG else F32) for n in BIG + CONVS])
    full = dict(zip(BIG + CONVS, gathered))
    small = {n: w[n] for n in SMALL if n not in CONVS}

    loss, grad_x, grads = _local_step(x[0], p[:, 0], loss_target[0], full, small)

    big = [grads[n] for n in BIG]
    theirs = _swap_layers(big)
    parts =[_add_own_layer(g, o, c_arr) for g, o in zip(big, theirs)]
    slots = _scatter_shards(parts)
    g_big = dict(zip(BIG, _join_layers([_sum_reduced(pt, sl, shard_arr, c_arr) for pt, sl in zip(parts, slots)])))

    rep = [n for n in SMALL if n not in CONVS]
    names = rep + CONVS
    vec = _pack([grads[n] for n in names] + [loss[0, :1]])
    total = _sum_slots("small_sum", _exchange_small(vec))
    parts_small = _unpack(total, [grads[n].shape for n in names] + [(1,)])
    g_small = dict(zip(names, parts_small[:-1]))
    loss_out = parts_small[-1].reshape(())
    for n in CONVS:
        width = w[n].shape[-1]
        g_small[n] = lax.dynamic_slice_in_dim(g_small[n], shard * width, width, axis=2)

    delta, new_m, new_v, grad_w = {}, {}, {}, {}
    for n in BIG:
        shp = w[n].shape
        two_d = lambda a: a.reshape(-1, shp[-1])
        d_, m_, v_ = _adamw(two_d(w[n]), two_d(g_big[n]), two_d(m[n]), two_d(v[n]))
        delta[n], new_m[n], new_v[n], grad_w[n] = d_.reshape(shp), m_.reshape(shp), v_.reshape(shp), g_big[n]
    shapes = [w[n].shape for n in SMALL]
    d_, m_, v_ = _adamw(_pack([w[n] for n in SMALL]), _pack([g_small[n] for n in SMALL]),
                        _pack([m[n] for n in SMALL]), _pack([v[n] for n in SMALL]))
    for n, dd, mm, vv in zip(SMALL, _unpack(d_, shapes), _unpack(m_, shapes), _unpack(v_, shapes)):
        delta[n], new_m[n], new_v[n], grad_w[n] = dd, mm, vv, g_small[n]

    return (loss_out, grad_x[None], *[grad_w[n] for n in WEIGHTS], *[delta[n] for n in WEIGHTS],
            *[new_m[n] for n in WEIGHTS], *[new_v[n] for n in WEIGHTS])
```

```python
import functools
import math

import jax
import jax.numpy as jnp
from jax import lax
from jax.experimental import pallas as pl
from jax.experimental.pallas import tpu as pltpu

F32 = jnp.float32
BF16 = jnp.bfloat16
HI = lax.Precision.HIGH
MESH = pl.DeviceIdType.MESH

LANES = 128
EPS = 1e-6
SG_HEADS, SG_CHUNK = 4, 128
SC_GROUPS, SC_KERNEL = 4, 3
GDN_HEADS, GDN_CONV = 8, 4
GDN_L = 128
GDN_CPS = 8
SG_W = SG_HEADS * LANES
SC_W = SC_GROUPS * LANES
GDN_W = GDN_HEADS * LANES
IN_COLS = 2 * SG_W + 3 * SC_W + 4 * GDN_W + 2 * GDN_HEADS
PROJ_W = 7168
CB_U, CB_V = 0, 4
CB_GB, CB_GC, CB_X = 8, 12, 16
CB_Q, CB_K, CB_VV, CB_Z, CB_AB = 20, 28, 36, 44, 52
N_SHARD = 4
N_DEV = 8
DEPTH = 2

ADAM_LR, ADAM_B1, ADAM_B2, ADAM_EPS, ADAM_WD, ADAM_STEP = 0.001, 0.9, 0.999, 1e-08, 0.01, 10

VMEM_LIMIT = 56 << 20

WEIGHTS = ['norm_mix', 'w_in', 'sg_ln_g', 'sg_ln_b', 'sg_w', 'sg_b', 'sc_conv', 'gdn_conv', 'gdn_a_log',
           'gdn_dt_bias', 'gdn_norm', 'out_norm_a', 'out_norm_b', 'w_o', 'norm_ffn', 'w_ff1', 'w_ff2', 'norm_ple',
           'w_ple_gate', 'w_ple_proj', 'norm_final']
BIG = ['w_in', 'w_o', 'w_ff1', 'w_ff2', 'w_ple_gate', 'w_ple_proj']
CONVS = ['sc_conv', 'gdn_conv']
SMALL = [n for n in WEIGHTS if n not in BIG]


def _cparams(sem=None):
    return pltpu.CompilerParams(dimension_semantics=sem, vmem_limit_bytes=VMEM_LIMIT)


def _dot(a, b, dims, prec=None):
    return lax.dot_general(a, b, (dims, ((), ())), precision=prec, preferred_element_type=F32)


def _mm(a, b):
    return _dot(a, b, ((1,), (0,)), HI)


def _mm_nt(a, b):
    return _dot(a, b, ((1,), (1,)), HI)


def _mm_tn(a, b):
    return _dot(a, b, ((0,), (0,)), HI)


@jax.custom_vjp
def _bmm(a, b):
    return _dot(a.astype(BF16), b.astype(BF16), ((1,), (0,)))


def _bmm_fwd(a, b):
    return _bmm(a, b), (a, b)


def _bmm_bwd(res, g):
    a, b = res
    gb = g.astype(BF16)
    return _dot(gb, b.astype(BF16), ((1,), (1,))), _dot(a.astype(BF16), gb, ((0,), (0,)))


_bmm.defvjp(_bmm_fwd, _bmm_bwd)


@jax.custom_vjp
def _bmm_tn(a, b):
    return _dot(a.astype(BF16), b.astype(BF16), ((0,), (0,)))


def _bmm_tn_fwd(a, b):
    return _bmm_tn(a, b), (a, b)


def _bmm_tn_bwd(res, g):
    a, b = res
    gb = g.astype(BF16)
    return _dot(b.astype(BF16), gb, ((1,), (1,))), _dot(a.astype(BF16), gb, ((1,), (0,)))


_bmm_tn.defvjp(_bmm_tn_fwd, _bmm_tn_bwd)


def _sigmoid(x):
    return 1.0 / (1.0 + jnp.exp(-x))


def _silu(x):
    return x * _sigmoid(x)


def _gelu(x):
    c = math.sqrt(2.0 / math.pi)
    return 0.5 * x * (1.0 + jnp.tanh(c * (x + 0.044715 * (x * x * x))))


def _softplus(x):
    return jnp.maximum(x, 0.0) + jnp.log(1.0 + jnp.exp(-jnp.abs(x)))


def _rms(x, g):
    return x * lax.rsqrt(jnp.mean(x * x, axis=-1, keepdims=True) + EPS) * g


def _roll_rows(x, shift):
    return pltpu.roll(x, shift % x.shape[0], 0)


@functools.partial(jax.custom_vjp, nondiff_argnums=(1,))
def _shift_down(x, j):
    row = lax.broadcasted_iota(jnp.int32, x.shape, 0)
    return jnp.where(row >= j, _roll_rows(x, j), 0.0)


def _shift_down_fwd(x, j):
    return _shift_down(x, j), None


def _shift_down_bwd(j, _, dy):
    row = lax.broadcasted_iota(jnp.int32, dy.shape, 0)
    return (jnp.where(row < dy.shape[0] - j, _roll_rows(dy, -j), 0.0),)


_shift_down.defvjp(_shift_down_fwd, _shift_down_bwd)


def _causal_conv(x, taps):
    k = len(taps)
    y = taps[k - 1] * x
    for j in range(k - 1):
        y = y + taps[j] * _shift_down(x, k - 1 - j)
    return y


class _In:
    def __init__(self, arr, block, imap, shared=False, gshape=None, gmap=None):
        self.arr, self.block, self.imap, self.shared = arr, block, imap, shared
        self.gshape = arr.shape if gshape is None else gshape
        self.gmap = imap if gmap is None else gmap


def _fused(name, fn, n, ins, outs, douts=None, need=None, gdt=None, add=None):
    n_in, n_out = len(ins), len(outs)
    in_specs = [pl.BlockSpec(s.block, s.imap) for s in ins]
    out_specs = [pl.BlockSpec(bs, im) for _, _, bs, im in outs]
    if douts is None:
        def body(*refs):
            res = fn(*[r[...] for r in refs[:n_in]])
            for r, v in zip(refs[n_in:], res):
                r[...] = v.astype(r.dtype)

        return pl.pallas_call(
            body, name=name, grid=(n,), in_specs=in_specs, out_specs=out_specs,
            out_shape=[jax.ShapeDtypeStruct(s, d) for s, d, _, _ in outs],
            compiler_params=_cparams(("arbitrary",)),
        )(*[s.arr for s in ins])

    gdt = list(gdt) if gdt is not None else [F32] * n_in
    add = dict(add or {})
    gidx = [i for i in range(n_in) if need[i]]
    aidx = [i for i in gidx if i in add]

    def body(*refs):
        in_refs, d_refs = refs[:n_in], refs[n_in:n_in + n_out]
        a_refs = dict(zip(aidx, refs[n_in + n_out:n_in + n_out + len(aidx)]))
        g_refs = refs[n_in + n_out + len(aidx):]
        vals = [r[...] for r in in_refs]

        def f(*dv):
            full = list(vals)
            for i, v in zip(gidx, dv):
                full[i] = v
            return tuple(o.astype(F32) for o in fn(*full))

        _, vjp = jax.vjp(f, *[vals[i].astype(F32) for i in gidx])
        grads = vjp(tuple(r[...].astype(F32) for r in d_refs))
        for i, g_ref, g in zip(gidx, g_refs, grads):
            if i in a_refs:
                g = g + a_refs[i][...].astype(F32)
            if ins[i].shared:
                period = n if ins[i].shared is True else ins[i].shared

                @pl.when(pl.program_id(0) % period == 0)
                def _():
                    g_ref[...] = jnp.zeros_like(g_ref)
                g_ref[...] += g.astype(g_ref.dtype)
            else:
                g_ref[...] = g.astype(g_ref.dtype)

    g_specs = [pl.BlockSpec(ins[i].block, ins[i].gmap) for i in gidx]
    g_shape = [jax.ShapeDtypeStruct(ins[i].gshape, gdt[i]) for i in gidx]
    res = pl.pallas_call(
        body, name=name, grid=(n,), in_specs=in_specs + out_specs + [g_specs[gidx.index(i)] for i in aidx],
        out_specs=g_specs, out_shape=g_shape,
        compiler_params=_cparams(("arbitrary",)),
    )(*[s.arr for s in ins], *douts, *[add[i] for i in aidx])
    full = [None] * n_in
    for i, g in zip(gidx, res):
        full[i] = g
    return full


def _row_in(a, tm, cb=None):
    if cb is None:
        return _In(a, (tm, a.shape[1]), lambda i: (i, 0))
    return _In(a, (tm, LANES), lambda i: (i, cb), gshape=(a.shape[0], LANES), gmap=lambda i: (i, 0))


def _row_shared(a):
    return _In(a, a.shape, lambda i: (0, 0), shared=True)


def _row_out(t, w, dt, tm):
    return ((t, w), dt, (tm, w), lambda i: (i, 0))


def _col_in(a, base, nblk):
    t = a.shape[0]
    return _In(a, (t, LANES), lambda i: (0, base + i), gshape=(t, nblk * LANES), gmap=lambda i: (0, i))


def _col_par(a):
    return _In(a, (None,) + a.shape[1:], lambda i: (i, 0, 0))


def _col_out(t, w, dt, base=0):
    return ((t, w), dt, (t, LANES), lambda i: (0, base + i))


def _matmul(name, a, b, mode, m, n, k, tm, tn, tk, b_spec=None, epilogue=None, extras=(), outs=None, out_dtype=F32):
    tm, tn, tk = min(tm, m), min(tn, n), min(tk, k)
    assert m % tm == 0 and n % tn == 0 and k % tk == 0, (name, m, n, k, tm, tn, tk)
    nk = k // tk
    a_spec = (pl.BlockSpec((tk, tm), lambda i, j, kk: (kk, i)) if mode == "tn"
              else pl.BlockSpec((tm, tk), lambda i, j, kk: (i, kk)))
    if b_spec is None:
        b_spec = (pl.BlockSpec((tn, tk), lambda i, j, kk: (j, kk)) if mode == "nt"
                  else pl.BlockSpec((tk, tn), lambda i, j, kk: (kk, j)))
    dims = {"nn": ((1,), (0,)), "nt": ((1,), (1,)), "tn": ((0,), (0,))}[mode]
    if outs is None:
        outs = [((m, n), out_dtype, (tm, tn), lambda i, j, kk: (i, j))]
    if epilogue is None:
        epilogue = lambda acc: (acc,)
    n_ex = len(extras)

    def body(*refs):
        a_ref, b_ref = refs[0], refs[1]
        ex_refs = refs[2:2 + n_ex]
        o_refs = refs[2 + n_ex:2 + n_ex + len(outs)]
        part = _dot(a_ref[...].astype(BF16), b_ref[...].astype(BF16), dims)

        def finish(acc):
            for r, v in zip(o_refs, epilogue(acc, *[e[...] for e in ex_refs])):
                r[...] = v.astype(r.dtype)

        if nk == 1:
            finish(part)
        else:
            acc_ref = refs[-1]
            kk = pl.program_id(2)

            @pl.when(kk == 0)
            def _():
                acc_ref[...] = part

            @pl.when(kk > 0)
            def _():
                acc_ref[...] += part

            @pl.when(kk == nk - 1)
            def _():
                finish(acc_ref[...])

    ex_specs = [pl.BlockSpec((tm, tn), lambda i, j, kk: (i, j)) for _ in extras]
    res = pl.pallas_call(
        body, name=name, grid=(m // tm, n // tn, nk),
        in_specs=[a_spec, b_spec] + ex_specs,
        out_specs=[pl.BlockSpec(bs, im) for _, _, bs, im in outs],
        out_shape=[jax.ShapeDtypeStruct(s, d) for s, d, _, _ in outs],
        scratch_shapes=[pltpu.VMEM((tm, tn), F32)] if nk > 1 else [],
        compiler_params=_cparams(("parallel", "parallel", "arbitrary")),
    )(a, b, *extras)
    return res if len(res) > 1 else res[0]


def _fn_sgu(u_pre, v_pre, ln_g, ln_b, w, bb, na):
    t = u_pre.shape[0]
    u = _gelu(u_pre)
    v = _gelu(v_pre)
    mu = jnp.mean(v, axis=-1, keepdims=True)
    vc = v - mu
    vh = vc * lax.rsqrt(jnp.mean(vc * vc, axis=-1, keepdims=True) + EPS) * ln_g + ln_b
    ri = lax.broadcasted_iota(jnp.int32, w.shape, 0)
    ci = lax.broadcasted_iota(jnp.int32, w.shape, 1)
    wc = jnp.where(ri >= ci, w, 0.0)
    f = jnp.concatenate([_bmm(wc, vh[c * SG_CHUNK:(c + 1) * SG_CHUNK]) + bb for c in range(t // SG_CHUNK)], axis=0)
    return (_rms(u * f, na),)


def _fn_sconv(gb, gc, xin, w0, w1, w2, nb):
    return (_rms(gb * _causal_conv(gc * xin, (w0, w1, w2)), nb),)


def _fn_gdn_qk(pre, w0, w1, w2, w3):
    a = _silu(_causal_conv(pre, (w0, w1, w2, w3)))
    return (a * lax.rsqrt(jnp.sum(a * a, axis=-1, keepdims=True) + EPS),)


def _fn_gdn_v(pre, w0, w1, w2, w3):
    return (_silu(_causal_conv(pre, (w0, w1, w2, w3))),)


def _fn_gdn_gates(ab, a_log, dt_bias):
    lane = lax.broadcasted_iota(jnp.int32, ab.shape, 1)
    g = -jnp.exp(a_log) * _softplus(ab + dt_bias)
    return (jnp.where(lane < GDN_HEADS, g, jnp.where(lane < 2 * GDN_HEADS, _sigmoid(ab), 0.0)),)


def _solve_unit_lower_impl(mats):
    n = mats[0].shape[0]
    ri = lax.broadcasted_iota(jnp.int32, (n, n), 0)
    ci = lax.broadcasted_iota(jnp.int32, (n, n), 1)
    ts = [(ri == ci).astype(F32)] * len(mats)
    sh = 0
    while (1 << sh) < n:
        rb = jnp.right_shift(ri, sh)
        cb = jnp.right_shift(ci, sh)
        off = ((rb & 1) == 1) & (cb == rb - 1)
        us = [_mm(t, jnp.where(off, a, 0.0)) for t, a in zip(ts, mats)]
        ts = [t - _mm(u, t) for t, u in zip(ts, us)]
        sh += 1
    return tuple(ts)


def _solve_cotangents(ts, dts):
    us = [_mm_nt(dt, t) for t, dt in zip(ts, dts)]
    return tuple(-_mm_tn(t, u) for t, u in zip(ts, us))


@jax.custom_vjp
def _solve_unit_lower(mats):
    return _solve_unit_lower_impl(mats)


def _solve_unit_lower_fwd(mats):
    ts = _solve_unit_lower_impl(mats)
    return ts, ts


def _solve_unit_lower_bwd(ts, dts):
    return (_solve_cotangents(ts, dts),)


_solve_unit_lower.defvjp(_solve_unit_lower_fwd, _solve_unit_lower_bwd)


@jax.custom_vjp
def _solved_unit_lower(mats, ts):
    return ts


def _solved_unit_lower_fwd(mats, ts):
    return ts, ts


def _solved_unit_lower_bwd(ts, dts):
    return _solve_cotangents(ts, dts), tuple(jnp.zeros_like(t) for t in ts)


_solved_unit_lower.defvjp(_solved_unit_lower_fwd, _solved_unit_lower_bwd)


def _fn_gdn_wy(q, k, v, gates, pick_g, pick_b, t_saved=None):
    n, dk = GDN_L, q.shape[1]
    rows = [slice(c * n, (c + 1) * n) for c in range(q.shape[0] // n)]
    ri = lax.broadcasted_iota(jnp.int32, (n, n), 0)
    ci = lax.broadcasted_iota(jnp.int32, (n, n), 1)
    incl = ri >= ci
    eye = (ri == ci).astype(F32)
    last = lax.broadcasted_iota(jnp.int32, (n, 1), 0) == (n - 1)
    qs = [q[r] * (dk ** -0.5) for r in rows]
    ks = [k[r] for r in rows]
    gcs, betas, d_incl = [], [], []
    for r in rows:
        g = jnp.sum(gates[r] * pick_g, axis=1, keepdims=True)
        betas.append(jnp.sum(gates[r] * pick_b, axis=1, keepdims=True))
        g_row = jnp.sum(eye * g, axis=0, keepdims=True)
        gc = jnp.sum(jnp.where(incl, g_row, 0.0), axis=1, keepdims=True)
        gc_row = jnp.sum(eye * gc, axis=0, keepdims=True)
        gcs.append(gc)
        d_incl.append(jnp.where(incl, jnp.exp(jnp.where(incl, gc - gc_row, 0.0)), 0.0))
    kbs = [kk * b for kk, b in zip(ks, betas)]
    mats = tuple(_mm_nt(kb, kk) * jnp.where(ri > ci, d, 0.0) for kb, kk, d in zip(kbs, ks, d_incl))
    ts = _solve_unit_lower(mats) if t_saved is None else _solved_unit_lower(mats, tuple(t_saved[r] for r in rows))
    egs = [jnp.exp(gc) for gc in gcs]
    values = [_mm(t, v[r] * b) for t, r, b in zip(ts, rows, betas)]
    kcds = [_mm(t, kb * eg) for t, kb, eg in zip(ts, kbs, egs)]
    intras = [_mm_nt(qq, kk) * d for qq, kk, d in zip(qs, ks, d_incl)]
    g_lasts = [jnp.sum(jnp.where(last, gc, 0.0), axis=0, keepdims=True) for gc in gcs]
    qes = [qq * eg for qq, eg in zip(qs, egs)]
    kts = [kk * jnp.exp(gl - gc) for kk, gl, gc in zip(ks, g_lasts, gcs)]
    carries = [jnp.broadcast_to(jnp.exp(gl), (8, LANES)) for gl in g_lasts]
    cat = lambda parts: jnp.concatenate(parts, axis=0)
    res = (cat(values), cat(kcds), cat(qes), cat(kts), cat(intras), cat(carries))
    return res + (cat(ts),) if t_saved is None else res


def _gdn_step(s, value, kcd, qe, kt, intra, carry):
    v_new = value - _bmm(kcd, s)
    o = _bmm(qe, s) + _bmm(intra, v_new)
    return s * carry + _bmm_tn(kt, v_new), o


def _gdn_post(o, z, nrm):
    return _rms(o, nrm) * _silu(z)


def _gdn_wy_ins(q, k, v, gates, t_saved=None):
    t = q.shape[0]
    rb = min(GDN_CPS * GDN_L, t)
    hd = GDN_HEADS
    lane = jnp.arange(LANES)[None, None, :]
    pick_g = (lane == jnp.arange(hd)[:, None, None]).astype(F32)
    pick_b = (lane == jnp.arange(hd)[:, None, None] + hd).astype(F32)
    blk = lambda a: _In(a, (rb, LANES), lambda i: (i // hd, i % hd))
    par = lambda a: _In(a, (None, 1, LANES), lambda i: (i % hd, 0, 0))
    ins = [blk(q), blk(k), blk(v), _In(gates, (rb, LANES), lambda i: (i // hd, 0), shared=hd), par(pick_g),
           par(pick_b)]
    wide = lambda dt: ((t, GDN_W), dt, (rb, LANES), lambda i: (i // hd, i % hd))
    carry = ((hd, 8 * (t // GDN_L), LANES), F32, (None, 8 * (rb // GDN_L), LANES), lambda i: (i % hd, i // hd, 0))
    outs = [wide(F32), wide(BF16), wide(BF16), wide(BF16), wide(BF16), carry]
    if t_saved is None:
        outs.append(wide(F32))
    else:
        ins.append(blk(t_saved))
    return (t // rb) * hd, ins, outs


def _gdn_scan_specs(t):
    col = lambda base: pl.BlockSpec((t, LANES), lambda h: (0, base + h))
    carry = pl.BlockSpec((None, 8 * (t // GDN_L), LANES), lambda h: (h, 0, 0))
    par = pl.BlockSpec((None, 1, LANES), lambda h: (h, 0, 0))
    return col, carry, par


def _gdn_scan_fwd(wy, proj, nrm):
    t = wy[0].shape[0]
    nc = t // GDN_L

    def body(val_ref, kcd_ref, qe_ref, kt_ref, in_ref, cy_ref, z_ref, n_ref, y_ref, o_scr):
        def step(c, s):
            rows = pl.ds(pl.multiple_of(c * GDN_L, GDN_L), GDN_L)
            s, o = _gdn_step(s, val_ref[rows, :], kcd_ref[rows, :], qe_ref[rows, :], kt_ref[rows, :],
                             in_ref[rows, :], cy_ref[pl.ds(pl.multiple_of(c * 8, 8), 1), :])
            o_scr[rows, :] = o
            return s

        lax.fori_loop(0, nc, step, jnp.zeros((LANES, LANES), F32))
        y_ref[...] = _gdn_post(o_scr[...], z_ref[...], n_ref[...]).astype(y_ref.dtype)

    col, carry, par = _gdn_scan_specs(t)
    return pl.pallas_call(
        body, name="gdn_scan_fwd", grid=(GDN_HEADS,),
        in_specs=[col(0)] * 5 + [carry, col(CB_Z), par],
        out_specs=col(0), out_shape=jax.ShapeDtypeStruct((t, GDN_W), BF16),
        scratch_shapes=[pltpu.VMEM((t, LANES), F32)],
        compiler_params=_cparams(("arbitrary",)),
    )(*wy, proj, nrm)


def _gdn_scan_bwd(wy, proj, nrm, dy, dy_base):
    t = wy[0].shape[0]
    nc = t // GDN_L

    def body(val_ref, kcd_ref, qe_ref, kt_ref, in_ref, cy_ref, z_ref, n_ref, dy_ref,
             dval_ref, dkcd_ref, dqe_ref, dkt_ref, din_ref, dcy_ref, dz_ref, dn_ref, o_scr, s_scr):
        def operands(c):
            rows = pl.ds(pl.multiple_of(c * GDN_L, GDN_L), GDN_L)
            return rows, (val_ref[rows, :], kcd_ref[rows, :], qe_ref[rows, :], kt_ref[rows, :], in_ref[rows, :],
                          cy_ref[pl.ds(pl.multiple_of(c * 8, 8), 1), :])

        def step(c, s):
            rows, ops = operands(c)
            s_scr[c] = s
            s, o = _gdn_step(s, *ops)
            o_scr[rows, :] = o
            return s

        lax.fori_loop(0, nc, step, jnp.zeros((LANES, LANES), F32))
        _, vjp_post = jax.vjp(_gdn_post, o_scr[...], z_ref[...], n_ref[...])
        do, dz, dn = vjp_post(dy_ref[...].astype(F32))
        dz_ref[...] = dz.astype(dz_ref.dtype)
        dn_ref[...] = dn
        o_scr[...] = do
        dcy_ref[...] = jnp.zeros_like(dcy_ref)

        def rstep(i, ds):
            c = nc - 1 - i
            rows, ops = operands(c)
            _, vjp_c = jax.vjp(_gdn_step, s_scr[c], *[o.astype(F32) for o in ops])
            ds, dval, dkcd, dqe, dkt, din, dcy = vjp_c((ds, o_scr[rows, :]))
            for r, g in ((dval_ref, dval), (dkcd_ref, dkcd), (dqe_ref, dqe), (dkt_ref, dkt), (din_ref, din)):
                r[rows, :] = g.astype(r.dtype)
            dcy_ref[pl.ds(pl.multiple_of(c * 8, 8), 1), :] = dcy
            return ds

        lax.fori_loop(0, nc, rstep, jnp.zeros((LANES, LANES), F32))

    col, carry, par = _gdn_scan_specs(t)
    wide = jax.ShapeDtypeStruct((t, GDN_W), BF16)
    return pl.pallas_call(
        body, name="gdn_scan_bwd", grid=(GDN_HEADS,),
        in_specs=[col(0)] * 5 + [carry, col(CB_Z), par, col(dy_base)],
        out_specs=[col(0)] * 5 + [carry, col(0), par],
        out_shape=[wide] * 5 + [jax.ShapeDtypeStruct(wy[5].shape, F32), wide,
                                jax.ShapeDtypeStruct((GDN_HEADS, 1, LANES), F32)],
        scratch_shapes=[pltpu.VMEM((t, LANES), F32), pltpu.VMEM((nc, LANES, LANES), F32)],
        compiler_params=_cparams(("arbitrary",)),
    )(*wy, proj, nrm, dy)


TM = 512
TR = 256


def _rms_fwd(name, h, g):
    t, d = h.shape
    tm = min(TR, t)
    return _fused(name, lambda hb, gb: (_rms(hb, gb),), t // tm, [_row_in(h, tm), _row_shared(g)],
                  [_row_out(t, d, BF16, tm)])[0]


def _rms_bwd(name, h, g, dxn, dh_next):
    t, d = h.shape
    tm = min(TR, t)
    dh, dg = _fused(name, lambda hb, gb: (_rms(hb, gb),), t // tm, [_row_in(h, tm), _row_shared(g)],
                    [_row_out(t, d, F32, tm)], douts=[dxn], need=[True, True], add={0: dh_next})
    return dh, dg


def _mixer_ins(proj, lw):
    sgu = [_col_in(proj, CB_U, SG_HEADS), _col_in(proj, CB_V, SG_HEADS), _col_par(lw['sg_ln_g']),
           _col_par(lw['sg_ln_b']), _col_par(lw['sg_w']), _col_par(lw['sg_bb']), _col_par(lw['out_norm_a'])]
    sconv = [_col_in(proj, CB_GB, SC_GROUPS), _col_in(proj, CB_GC, SC_GROUPS), _col_in(proj, CB_X, SC_GROUPS)] + \
            [_col_par(w) for w in lw['sc_taps']] + [_col_par(lw['out_norm_b'])]
    gq = [_col_in(proj, CB_Q, GDN_HEADS)] + [_col_par(w) for w in lw['q_taps']]
    gk = [_col_in(proj, CB_K, GDN_HEADS)] + [_col_par(w) for w in lw['k_taps']]
    gv = [_col_in(proj, CB_VV, GDN_HEADS)] + [_col_par(w) for w in lw['v_taps']]
    return sgu, sconv, gq, gk, gv


def _gates_ins(proj, lw, tm):
    return [_row_in(proj, tm, CB_AB), _row_shared(lw['a_log_row']), _row_shared(lw['dt_bias_row'])]


def _layer_fwd(h, p_l, lw):
    t, d = h.shape
    dff = lw['w_ff2'].shape[0]
    xn = _rms_fwd("rms_fwd", h, lw['norm_mix'])
    proj = _matmul("proj_fwd", xn, lw['w_in'], "nn", t, PROJ_W, d, TM, 1024, d)
    sgu, sconv, gq, gk, gv = _mixer_ins(proj, lw)
    ya = _fused("sgu_fwd", _fn_sgu, SG_HEADS, sgu, [_col_out(t, SG_W, BF16)])[0]
    yb = _fused("sconv_fwd", _fn_sconv, SC_GROUPS, sconv, [_col_out(t, SC_W, BF16)])[0]
    q = _fused("gdn_q_fwd", _fn_gdn_qk, GDN_HEADS, gq, [_col_out(t, GDN_W, F32)])[0]
    k = _fused("gdn_k_fwd", _fn_gdn_qk, GDN_HEADS, gk, [_col_out(t, GDN_W, F32)])[0]
    v = _fused("gdn_v_fwd", _fn_gdn_v, GDN_HEADS, gv, [_col_out(t, GDN_W, F32)])[0]
    tm = min(TR, t)
    gates = _fused("gdn_gates_fwd", _fn_gdn_gates, t // tm, _gates_ins(proj, lw, tm),
                   [_row_out(t, LANES, F32, tm)])[0]
    n_wy, wy_ins, wy_outs = _gdn_wy_ins(q, k, v, gates)
    wy = _fused("gdn_wy_fwd", _fn_gdn_wy, n_wy, wy_ins, wy_outs)
    wy, wy_t = wy[:6], wy[6]
    yc = _gdn_scan_fwd(wy, proj, lw['gdn_norm'])
    ycat = jnp.concatenate([ya, yb, yc], axis=1)
    h2 = _matmul("wo_fwd", ycat, lw['w_o'], "nn", t, d, d, TM, 1024, d,
                 epilogue=lambda acc, hb: (hb + acc,), extras=(h,))
    hn = _rms_fwd("rms_fwd", h2, lw['norm_ffn'])
    per = d // 1024
    s, r = _matmul("ff1_fwd", hn, lw['w_ff1'], "nn", t, dff, d, TM, 1024, d,
                   b_spec=pl.BlockSpec((None, d, 1024), lambda i, j, kk: (j // per, 0, j % per)),
                   epilogue=lambda acc: (jnp.maximum(acc, 0.0), jnp.square(jnp.maximum(acc, 0.0))),
                   outs=[((t, dff), BF16, (min(TM, t), 1024), lambda i, j, kk: (i, j))] * 2)
    h3 = _matmul("ff2_fwd", r, lw['w_ff2'], "nn", t, d, dff, TM, 1024, 2048,
                 epilogue=lambda acc, hb: (hb + acc,), extras=(h2,))
    hn2 = _rms_fwd("rms_fwd", h3, lw['norm_ple'])
    pp = _matmul("ple_proj_fwd", p_l, lw['w_ple_proj'], "nn", t, d, p_l.shape[1], TM, 1024, p_l.shape[1])

    def gate_epilogue(acc, hb, ppb):
        sg = _sigmoid(acc)
        return hb + ppb * sg, sg

    h4, gate = _matmul("ple_gate_fwd", hn2, lw['w_ple_gate'], "nn", t, d, d, TM, 1024, d, epilogue=gate_epilogue,
                       extras=(h3, pp), outs=[((t, d), F32, (min(TM, t), 1024), lambda i, j, kk: (i, j))] * 2)
    saved = dict(h=h, xn=xn, proj=proj, q=q, k=k, v=v, gates=gates, ycat=ycat, h2=h2, hn=hn, s=s, r=r, h3=h3,
                 hn2=hn2, pp=pp, gate=gate, p=p_l, wy=wy, wy_t=wy_t)
    return h4, saved


def _layer_bwd(dh4, sv, lw):
    t, d = dh4.shape
    dff = lw['w_ff2'].shape[0]
    tm = min(TR, t)
    g = {}
    dacc, dpp = _fused("ple_bwd_gate", lambda dh, pp, gt: (dh * pp * gt * (1.0 - gt), dh * gt), t // tm,
                       [_row_in(dh4, tm), _row_in(sv['pp'], tm), _row_in(sv['gate'], tm)],
                       [_row_out(t, d, BF16, tm)] * 2)
    g['w_ple_gate'] = _matmul("dw_ple_gate", sv['hn2'], dacc, "tn", d, d, t, TM, 1024, 1024, out_dtype=BF16)
    g['w_ple_proj'] = _matmul("dw_ple_proj", sv['p'], dpp, "tn", sv['p'].shape[1], d, t, TM, 1024, 1024,
                              out_dtype=BF16)
    dhn2 = _matmul("dx_ple_gate", dacc, lw['w_ple_gate'], "nt", t, d, d, TM, 1024, d)
    dh3, g['norm_ple'] = _rms_bwd("rms_bwd", sv['h3'], lw['norm_ple'], dhn2, dh4)
    da = _matmul("dx_ff2", dh3, lw['w_ff2'], "nt", t, dff, d, TM, 1024, d,
                 epilogue=lambda acc, sb: (acc * (2.0 * sb.astype(F32)),), extras=(sv['s'],),
                 outs=[((t, dff), BF16, (min(TM, t), 1024), lambda i, j, kk: (i, j))])
    g['w_ff2'] = _matmul("dw_ff2", sv['r'], dh3, "tn", dff, d, t, TM, 1024, 1024, out_dtype=BF16)
    per = d // 1024
    dhn = _matmul("dx_ff1", da, lw['w_ff1'], "nt", t, d, dff, TM, 1024, 1024,
                  b_spec=pl.BlockSpec((None, 1024, 1024), lambda i, j, kk: (kk // per, j, kk % per)))
    g['w_ff1'] = _matmul("dw_ff1", sv['hn'], da, "tn", d, dff, t, TM, 1024, 1024,
                         outs=[((N_SHARD, d, d), BF16, (None, TM, 1024), lambda i, j, kk: (j // per, i, j % per))])
    dh2, g['norm_ffn'] = _rms_bwd("rms_bwd", sv['h2'], lw['norm_ffn'], dhn, dh3)
    dycat = _matmul("dx_o", dh2, lw['w_o'], "nt", t, d, d, TM, 1024, d)
    g['w_o'] = _matmul("dw_o", sv['ycat'], dh2, "tn", d, d, t, TM, 1024, 1024, out_dtype=BF16)
    proj = sv['proj']
    sgu, sconv, gq, gk, gv = _mixer_ins(proj, lw)
    bf2 = [BF16, BF16]
    r_ = _fused("sgu_bwd", _fn_sgu, SG_HEADS, sgu, [_col_out(t, d, F32, 0)], douts=[dycat], need=[True] * 7,
                gdt=bf2 + [F32] * 5)
    du, dv_, g['sg_ln_g'], g['sg_ln_b'], g['sg_w'], g['sg_bb'], g['out_norm_a'] = r_
    r_ = _fused("sconv_bwd", _fn_sconv, SC_GROUPS, sconv, [_col_out(t, d, F32, SG_HEADS)], douts=[dycat],
                need=[True] * 7, gdt=[BF16] * 3 + [F32] * 4)
    dgb, dgc, dxin = r_[:3]
    g['sc_taps'], g['out_norm_b'] = r_[3:6], r_[6]
    r_ = _gdn_scan_bwd(sv['wy'], proj, lw['gdn_norm'], dycat, SG_HEADS + SC_GROUPS)
    dwy, dz, g['gdn_norm'] = r_[:6], r_[6], r_[7]
    n_wy, wy_ins, wy_outs = _gdn_wy_ins(sv['q'], sv['k'], sv['v'], sv['gates'], sv['wy_t'])
    dq, dk, dvv, dgates = _fused("gdn_wy_bwd", _fn_gdn_wy, n_wy, wy_ins, wy_outs, douts=dwy,
                                 need=[True] * 4 + [False] * 3)[:4]
    one = [_col_out(t, GDN_W, F32)]
    r_ = _fused("gdn_q_bwd", _fn_gdn_qk, GDN_HEADS, gq, one, douts=[dq], need=[True] * 5, gdt=[BF16] + [F32] * 4)
    dpq, g['q_taps'] = r_[0], r_[1:]
    r_ = _fused("gdn_k_bwd", _fn_gdn_qk, GDN_HEADS, gk, one, douts=[dk], need=[True] * 5, gdt=[BF16] + [F32] * 4)
    dpk, g['k_taps'] = r_[0], r_[1:]
    r_ = _fused("gdn_v_bwd", _fn_gdn_v, GDN_HEADS, gv, one, douts=[dvv], need=[True] * 5, gdt=[BF16] + [F32] * 4)
    dpv, g['v_taps'] = r_[0], r_[1:]
    dab, g['a_log_row'], g['dt_bias_row'] = _fused(
        "gdn_gates_bwd", _fn_gdn_gates, t // tm, _gates_ins(proj, lw, tm), [_row_out(t, LANES, F32, tm)],
        douts=[dgates], need=[True] * 3, gdt=[BF16, F32, F32])
    pad = jnp.zeros((t, PROJ_W - (CB_AB + 1) * LANES), BF16)
    dproj = jnp.concatenate([du, dv_, dgb, dgc, dxin, dpq, dpk, dpv, dz, dab, pad], axis=1)
    dxn = _matmul("dx_in", dproj, lw['w_in'], "nt", t, d, PROJ_W, TM, 1024, 1792)
    g['w_in'] = _matmul("dw_in", sv['xn'], dproj, "tn", d, PROJ_W, t, TM, 1024, 1024, out_dtype=BF16)
    dh, g['norm_mix'] = _rms_bwd("rms_bwd", sv['h'], lw['norm_mix'], dxn, dh2)
    return dh, g


def _loss_grad(h, g, tgt):
    t, d = h.shape
    tm = min(TR, t)

    def body(h_ref, g_ref, t_ref, loss_ref, dh_ref, dg_ref):
        y, vjp = jax.vjp(_rms, h_ref[...], g_ref[...])
        e = y - t_ref[...]
        dh, dg = vjp(e * (1.0 / d))

        @pl.when(pl.program_id(0) == 0)
        def _():
            loss_ref[...] = jnp.zeros_like(loss_ref)
            dg_ref[...] = jnp.zeros_like(dg_ref)

        loss_ref[...] += jnp.sum(jnp.sum(e * e, axis=1, keepdims=True), axis=0, keepdims=True) * (0.5 / d)
        dh_ref[...] = dh
        dg_ref[...] += dg

    row = pl.BlockSpec((tm, d), lambda i: (i, 0))
    return pl.pallas_call(
        body, name="loss_grad", grid=(t // tm,),
        in_specs=[row, pl.BlockSpec((1, d), lambda i: (0, 0)), row],
        out_specs=[pl.BlockSpec((1, LANES), lambda i: (0, 0)), row, pl.BlockSpec((1, d), lambda i: (0, 0))],
        out_shape=[jax.ShapeDtypeStruct((1, LANES), F32), jax.ShapeDtypeStruct((t, d), F32),
                   jax.ShapeDtypeStruct((1, d), F32)],
        compiler_params=_cparams(("arbitrary",)),
    )(h, g, tgt)


def _layer_weights(l, full, small):
    d = full['w_o'].shape[-1]
    lw = {}
    w_in = jnp.transpose(full['w_in'][l], (1, 0, 2)).reshape(d, IN_COLS)
    lw['w_in'] = jnp.pad(w_in, ((0, 0), (0, PROJ_W - IN_COLS)))
    lw['w_o'] = full['w_o'][l].reshape(d, d)
    lw['w_ff1'] = full['w_ff1'][l]
    lw['w_ff2'] = full['w_ff2'][l].reshape(-1, d)
    lw['w_ple_gate'] = full['w_ple_gate'][l].reshape(d, d)
    wpp = full['w_ple_proj'][l]
    lw['w_ple_proj'] = jnp.transpose(wpp, (1, 0, 2)).reshape(wpp.shape[1], d)
    for n in ('norm_mix', 'norm_ffn', 'norm_ple'):
        lw[n] = small[n][l].reshape(1, d)
    lw['sg_ln_g'] = small['sg_ln_g'][l].reshape(SG_HEADS, 1, LANES)
    lw['sg_ln_b'] = small['sg_ln_b'][l].reshape(SG_HEADS, 1, LANES)
    lw['sg_w'] = small['sg_w'][l]
    lw['sg_bb'] = jnp.broadcast_to(small['sg_b'][l][:, :, None], (SG_HEADS, SG_CHUNK, LANES))
    lw['out_norm_a'] = small['out_norm_a'][l].reshape(SG_HEADS, 1, LANES)
    lw['out_norm_b'] = small['out_norm_b'][l].reshape(SC_GROUPS, 1, LANES)
    lw['gdn_norm'] = jnp.broadcast_to(small['gdn_norm'][l].reshape(1, 1, LANES), (GDN_HEADS, 1, LANES))
    lw['a_log_row'] = jnp.pad(small['gdn_a_log'][l].reshape(1, GDN_HEADS), ((0, 0), (0, LANES - GDN_HEADS)))
    lw['dt_bias_row'] = jnp.pad(small['gdn_dt_bias'][l].reshape(1, GDN_HEADS), ((0, 0), (0, LANES - GDN_HEADS)))
    sc = full['sc_conv'][l]
    lw['sc_taps'] = [sc[:, j:j + 1, :] for j in range(SC_KERNEL)]
    gc = jnp.transpose(full['gdn_conv'][l], (1, 0, 2)).reshape(GDN_CONV, 3 * GDN_W)
    for i, nm in enumerate(('q_taps', 'k_taps', 'v_taps')):
        part = gc[:, i * GDN_W:(i + 1) * GDN_W].reshape(GDN_CONV, GDN_HEADS, 1, LANES)
        lw[nm] = [part[j] for j in range(GDN_CONV)]
    return lw


def _weight_grads(g):
    d = g['w_o'].shape[0]
    out = {}
    w_in = g['w_in'][:, :IN_COLS].reshape(d, N_SHARD, IN_COLS // N_SHARD)
    out['w_in'] = jnp.transpose(w_in, (1, 0, 2))
    out['w_o'] = g['w_o'].reshape(N_SHARD, d // N_SHARD, d)
    out['w_ff1'] = g['w_ff1']
    out['w_ff2'] = g['w_ff2'].reshape(N_SHARD, -1, d)
    out['w_ple_gate'] = g['w_ple_gate'].reshape(N_SHARD, d // N_SHARD, d)
    wpp = g['w_ple_proj']
    out['w_ple_proj'] = jnp.transpose(wpp.reshape(wpp.shape[0], N_SHARD, d // N_SHARD), (1, 0, 2))
    for n in ('norm_mix', 'norm_ffn', 'norm_ple'):
        out[n] = g[n].reshape(d)
    out['sg_ln_g'] = g['sg_ln_g'].reshape(SG_W)
    out['sg_ln_b'] = g['sg_ln_b'].reshape(SG_W)
    out['sg_w'] = g['sg_w']
    out['sg_b'] = jnp.sum(g['sg_bb'], axis=2)
    out['out_norm_a'] = g['out_norm_a'].reshape(SG_W)
    out['out_norm_b'] = g['out_norm_b'].reshape(SC_W)
    out['gdn_norm'] = jnp.sum(g['gdn_norm'], axis=(0, 1))
    out['gdn_a_log'] = g['a_log_row'][0, :GDN_HEADS]
    out['gdn_dt_bias'] = g['dt_bias_row'][0, :GDN_HEADS]
    out['sc_conv'] = jnp.concatenate([w.reshape(1, SC_W) for w in g['sc_taps']], axis=0)
    taps = [jnp.concatenate([g[nm][j].reshape(1, GDN_W) for nm in ('q_taps', 'k_taps', 'v_taps')], axis=1)
            for j in range(GDN_CONV)]
    out['gdn_conv'] = jnp.concatenate(taps, axis=0)
    return out


def _local_step(x, p, tgt, full, small):
    h = x
    saved, lws = [], []
    for l in range(DEPTH):
        lw = _layer_weights(l, full, small)
        h, sv = _layer_fwd(h, p[l], lw)
        saved.append(sv)
        lws.append(lw)
    loss, dh, dnf = _loss_grad(h, small['norm_final'].reshape(1, -1), tgt)
    per_layer = [None] * DEPTH
    for l in reversed(range(DEPTH)):
        dh, g = _layer_bwd(dh, saved[l], lws[l])
        per_layer[l] = _weight_grads(g)
    grads = {n: jnp.stack([per_layer[l][n] for l in range(DEPTH)], axis=0) for n in per_layer[0]}
    grads['norm_final'] = dnf.reshape(-1)
    return loss, dh, grads


ANY = pl.BlockSpec(memory_space=pl.ANY)


def _place():
    x, y, c = lax.axis_index("x"), lax.axis_index("y"), lax.axis_index("c")
    chips = [(1 - x, y), (x, 1 - y), (1 - x, 1 - y)]
    return x, y, c, chips


def _place_shard(wsh, shard, dtype):
    dp, r, cc = wsh.shape
    tr = min(TR, r)
    nb = r // tr

    def body(idx_ref, w_ref, o_ref):
        o_ref[...] = w_ref[...].astype(o_ref.dtype)

    return pl.pallas_call(
        body, name="ag_place_shard",
        grid_spec=pltpu.PrefetchScalarGridSpec(
            num_scalar_prefetch=1, grid=(dp * nb,),
            in_specs=[pl.BlockSpec((None, tr, cc), lambda i, ix: (i // nb, i % nb, 0))],
            out_specs=pl.BlockSpec((None, None, tr, cc), lambda i, ix: (i // nb, ix[0], i % nb, 0))),
        out_shape=jax.ShapeDtypeStruct((dp, N_SHARD, r, cc), dtype),
        compiler_params=_cparams(("arbitrary",)),
    )(shard, wsh)


def _all_gather(bufs):
    n = len(bufs)

    def body(*refs):
        outs = refs[n:2 * n]
        send_sems, recv_sems = refs[2 * n:]
        x, y, c, chips = _place()
        me = 2 * x + y
        sibling = (x, y, 1 - c)

        def cp(w, k, shard, layer, to):
            blk = outs[w].at[layer, shard]
            return pltpu.make_async_remote_copy(src_ref=blk, dst_ref=blk, send_sem=send_sems.at[w, k],
                                                recv_sem=recv_sems.at[w, k], device_id=to, device_id_type=MESH)

        sends = []
        for w in range(n):
            for j, chip in enumerate(chips):
                s = cp(w, j, me, c, (*chip, c))
                s.start()
                sends.append(s)
        for w in range(n):
            for j, (px, py) in enumerate(chips):
                cp(w, j, 2 * px + py, c, sibling).wait_recv()
                s = cp(w, 3 + j, 2 * px + py, c, sibling)
                s.start()
                sends.append(s)
        for w in range(n):
            for j, (px, py) in enumerate(chips):
                cp(w, 3 + j, 2 * px + py, 1 - c, sibling).wait_recv()
        for s in sends:
            s.wait_send()

    return pl.pallas_call(
        body, name="all_gather",
        in_specs=[ANY] * n, out_specs=[ANY] * n,
        out_shape=[jax.ShapeDtypeStruct(b.shape, b.dtype) for b in bufs],
        input_output_aliases={w: w for w in range(n)},
        scratch_shapes=[pltpu.SemaphoreType.DMA((n, 6)), pltpu.SemaphoreType.DMA((n, 6))],
    )(*bufs)


def _swap_layers(grads):
    n = len(grads)

    def body(*refs):
        srcs, outs = refs[:n], refs[n:2 * n]
        send_sems, recv_sems = refs[2 * n:]
        x, y, c, _ = _place()
        cps = [pltpu.make_async_remote_copy(src_ref=srcs[w].at[1 - c], dst_ref=outs[w], send_sem=send_sems.at[w],
                                            recv_sem=recv_sems.at[w], device_id=(x, y, 1 - c), device_id_type=MESH)
               for w in range(n)]
        for cpy in cps:
            cpy.start()
        for cpy in cps:
            cpy.wait()

    return pl.pallas_call(
        body, name="rs_swap_layers", in_specs=[ANY] * n, out_specs=[ANY] * n,
        out_shape=[jax.ShapeDtypeStruct(g.shape[1:], g.dtype) for g in grads],
        scratch_shapes=[pltpu.SemaphoreType.DMA((n,)), pltpu.SemaphoreType.DMA((n,))],
    )(*grads)


def _scatter_shards(parts):
    n = len(parts)

    def body(*refs):
        srcs, outs = refs[:n], refs[n:2 * n]
        send_sems, recv_sems = refs[2 * n:]
        x, y, c, chips = _place()

        def cp(w, j, src_shard, to):
            return pltpu.make_async_remote_copy(
                src_ref=srcs[w].at[src_shard], dst_ref=outs[w].at[j], send_sem=send_sems.at[w, j],
                recv_sem=recv_sems.at[w, j], device_id=to, device_id_type=MESH)

        sends = [cp(w, j, 2 * px + py, (px, py, c)) for w in range(n) for j, (px, py) in enumerate(chips)]
        for s in sends:
            s.start()
        for s in sends:
            s.wait()

    return pl.pallas_call(
        body, name="rs_scatter", in_specs=[ANY] * n, out_specs=[ANY] * n,
        out_shape=[jax.ShapeDtypeStruct((3,) + g.shape[1:], g.dtype) for g in parts],
        scratch_shapes=[pltpu.SemaphoreType.DMA((n, 3)), pltpu.SemaphoreType.DMA((n, 3))],
    )(*parts)


def _sum_reduced(part, slots, shard, core):
    _, r, cc = part.shape
    tr = min(TR, r)

    def body(shard_ref, core_ref, p_ref, s0_ref, s1_ref, s2_ref, o_ref):
        acc = p_ref[...].astype(F32)
        for s_ref in (s0_ref, s1_ref, s2_ref):
            acc = acc + s_ref[...].astype(F32)
        o_ref[...] = acc

    slot = lambda j: pl.BlockSpec((None, tr, cc), lambda i, sh, co: (j, i, 0))
    return pl.pallas_call(
        body, name="rs_sum_reduced",
        grid_spec=pltpu.PrefetchScalarGridSpec(
            num_scalar_prefetch=2, grid=(r // tr,),
            in_specs=[pl.BlockSpec((None, tr, cc), lambda i, sh, co: (sh[0], i, 0)), slot(0), slot(1), slot(2)],
            out_specs=pl.BlockSpec((None, tr, cc), lambda i, sh, co: (co[0], i, 0))),
        out_shape=jax.ShapeDtypeStruct((DEPTH, r, cc), F32),
        compiler_params=_cparams(("arbitrary",)),
    )(shard, core, part, slots, slots, slots)


def _join_layers(bufs):
    n = len(bufs)

    def body(*refs):
        outs = refs[n:2 * n]
        send_sems, recv_sems = refs[2 * n:]
        x, y, c, _ = _place()
        sends = [pltpu.make_async_remote_copy(src_ref=outs[w].at[c], dst_ref=outs[w].at[c], send_sem=send_sems.at[w],
                                              recv_sem=recv_sems.at[w], device_id=(x, y, 1 - c), device_id_type=MESH)
                 for w in range(n)]
        for s in sends:
            s.start()
        for w in range(n):
            blk = outs[w].at[1 - c]
            pltpu.make_async_remote_copy(src_ref=blk, dst_ref=blk, send_sem=send_sems.at[w], recv_sem=recv_sems.at[w],
                                         device_id=(x, y, 1 - c), device_id_type=MESH).wait_recv()
        for s in sends:
            s.wait_send()

    return pl.pallas_call(
        body, name="rs_join_layers", in_specs=[ANY] * n, out_specs=[ANY] * n,
        out_shape=[jax.ShapeDtypeStruct(b.shape, b.dtype) for b in bufs],
        input_output_aliases={w: w for w in range(n)},
        scratch_shapes=[pltpu.SemaphoreType.DMA((n,)), pltpu.SemaphoreType.DMA((n,))],
    )(*bufs)


def _exchange_small(vec):
    def body(src, out, send_sems, recv_sems, local_sem):
        x, y, c, _ = _place()
        me = 4 * x + 2 * y + c
        lc = pltpu.make_async_copy(src, out.at[me], local_sem)
        lc.start()
        sends = []
        for k in range(1, N_DEV):
            fx, fy, fc = (k >> 2) & 1, (k >> 1) & 1, k & 1
            to = (x ^ fx, y ^ fy, c ^ fc)
            s = pltpu.make_async_remote_copy(src_ref=src, dst_ref=out.at[me], send_sem=send_sems.at[k - 1],
                                             recv_sem=recv_sems.at[k - 1], device_id=to, device_id_type=MESH)
            s.start()
            sends.append(s)
        for k in range(1, N_DEV):
            fx, fy, fc = (k >> 2) & 1, (k >> 1) & 1, k & 1
            frm = 4 * (x ^ fx) + 2 * (y ^ fy) + (c ^ fc)
            pltpu.make_async_remote_copy(src_ref=src, dst_ref=out.at[frm], send_sem=send_sems.at[k - 1],
                                         recv_sem=recv_sems.at[k - 1], device_id=(x ^ fx, y ^ fy, c ^ fc),
                                         device_id_type=MESH).wait_recv()
        for s in sends:
            s.wait_send()
        lc.wait()

    return pl.pallas_call(
        body, name="small_exchange", in_specs=[ANY], out_specs=ANY,
        out_shape=jax.ShapeDtypeStruct((N_DEV,) + vec.shape, vec.dtype),
        scratch_shapes=[pltpu.SemaphoreType.DMA((N_DEV - 1,)), pltpu.SemaphoreType.DMA((N_DEV - 1,)),
                        pltpu.SemaphoreType.DMA],
    )(vec)


def _add_own_layer(g, other, c_arr):
    _, ns, r, cc = g.shape
    tr = min(TR, r)
    nb = r // tr

    def body(c_ref, g_ref, o_ref, out_ref):
        out_ref[...] = (g_ref[...].astype(F32) + o_ref[...].astype(F32)).astype(out_ref.dtype)

    slab = pl.BlockSpec((None, tr, cc), lambda i, cr: (i // nb, i % nb, 0))
    return pl.pallas_call(
        body, name="rs_add_own_layer",
        grid_spec=pltpu.PrefetchScalarGridSpec(
            num_scalar_prefetch=1, grid=(ns * nb,),
            in_specs=[pl.BlockSpec((None, None, tr, cc), lambda i, cr: (cr[0], i // nb, i % nb, 0)), slab],
            out_specs=slab),
        out_shape=jax.ShapeDtypeStruct(other.shape, other.dtype),
        compiler_params=_cparams(("arbitrary",)),
    )(c_arr, g, other)


def _sum_slots(name, a):
    ns, r, cc = a.shape
    tr = min(TR, r)
    ins = [_In(a, (None, tr, cc), functools.partial(lambda s, i: (s, i, 0), s)) for s in range(ns)]

    def fn(*blocks):
        acc = blocks[0].astype(F32)
        for b in blocks[1:]:
            acc = acc + b.astype(F32)
        return (acc,)

    return _fused(name, fn, r // tr, ins, [((r, cc), F32, (tr, cc), lambda i: (i, 0))])[0]


def _adamw_fn(w, g, m, v):
    m = ADAM_B1 * m + (1.0 - ADAM_B1) * g
    v = ADAM_B2 * v + (1.0 - ADAM_B2) * jnp.square(g)
    m_hat = m / (1.0 - ADAM_B1 ** ADAM_STEP)
    v_hat = v / (1.0 - ADAM_B2 ** ADAM_STEP)
    delta = -ADAM_LR * (m_hat / (jnp.sqrt(v_hat) + ADAM_EPS) + ADAM_WD * w)
    return delta, m, v


def _adamw(w, g, m, v):
    r, cc = w.shape
    tr = min(TR, r)
    ins = [_In(a, (tr, cc), lambda i: (i, 0)) for a in (w, g, m, v)]
    return _fused("adamw", _adamw_fn, r // tr, ins, [((r, cc), F32, (tr, cc), lambda i: (i, 0))] * 3)


def _pack(arrs):
    flat = jnp.concatenate([a.reshape(-1) for a in arrs])
    tile = TR * LANES
    n = -(-flat.shape[0] // tile) * tile
    return jnp.pad(flat, (0, n - flat.shape[0])).reshape(-1, LANES)


def _unpack(vec, shapes):
    flat = vec.reshape(-1)
    out, o = [], 0
    for s in shapes:
        n = math.prod(s)
        out.append(flat[o:o + n].reshape(s))
        o += n
    return out


def kernel(x, p, norm_mix, w_in, sg_ln_g, sg_ln_b, sg_w, sg_b, sc_conv, gdn_conv, gdn_a_log, gdn_dt_bias, gdn_norm, out_norm_a, out_norm_b, w_o, norm_ffn, w_ff1, w_ff2, norm_ple, w_ple_gate, w_ple_proj, norm_final, loss_target, m_norm_mix, m_w_in, m_sg_ln_g, m_sg_ln_b, m_sg_w, m_sg_b, m_sc_conv, m_gdn_conv, m_gdn_a_log, m_gdn_dt_bias, m_gdn_norm, m_out_norm_a, m_out_norm_b, m_w_o, m_norm_ffn, m_w_ff1, m_w_ff2, m_norm_ple, m_w_ple_gate, m_w_ple_proj, m_norm_final, v_norm_mix, v_w_in, v_sg_ln_g, v_sg_ln_b, v_sg_w, v_sg_b, v_sc_conv, v_gdn_conv, v_gdn_a_log, v_gdn_dt_bias, v_gdn_norm, v_out_norm_a, v_out_norm_b, v_w_o, v_norm_ffn, v_w_ff1, v_w_ff2, v_norm_ple, v_w_ple_gate, v_w_ple_proj, v_norm_final):
    given = dict(locals())
    w = {n: given[n] for n in WEIGHTS}
    m = {n: given['m_' + n] for n in WEIGHTS}
    v = {n: given['v_' + n] for n in WEIGHTS}
    shard = 2 * lax.axis_index("x") + lax.axis_index("y")
    core = lax.axis_index("c")

    shard_arr = shard.reshape(1).astype(jnp.int32)
    c_arr = core.reshape(1).astype(jnp.int32)
    gathered = _all_gather([_place_shard(w[n].reshape((DEPTH, -1, w[n].shape[-1])), shard_arr,
                                         BF16 if n in BIG else F32) for n in BIG + CONVS])
    full = dict(zip(BIG + CONVS, gathered))
    small = {n: w[n] for n in SMALL if n not in CONVS}

    loss, grad_x, grads = _local_step(x[0], p[:, 0], loss_target[0], full, small)

    big = [grads[n] for n in BIG]
    theirs = _swap_layers(big)
    parts = [_add_own_layer(g, o, c_arr) for g, o in zip(big, theirs)]
    slots = _scatter_shards(parts)
    g_big = dict(zip(BIG, _join_layers([_sum_reduced(pt, sl, shard_arr, c_arr) for pt, sl in zip(parts, slots)])))

    rep = [n for n in SMALL if n not in CONVS]
    names = rep + CONVS
    vec = _pack([grads[n] for n in names] + [loss[0, :1]])
    total = _sum_slots("small_sum", _exchange_small(vec))
    parts_small = _unpack(total, [grads[n].shape for n in names] + [(1,)])
    g_small = dict(zip(names, parts_small[:-1]))
    loss_out = parts_small[-1].reshape(())
    for n in CONVS:
        width = w[n].shape[-1]
        g_small[n] = lax.dynamic_slice_in_dim(g_small[n], shard * width, width, axis=2)

    delta, new_m, new_v, grad_w = {}, {}, {}, {}
    for n in BIG:
        shp = w[n].shape
        two_d = lambda a: a.reshape(-1, shp[-1])
        d_, m_, v_ = _adamw(two_d(w[n]), two_d(g_big[n]), two_d(m[n]), two_d(v[n]))
        delta[n], new_m[n], new_v[n], grad_w[n] = d_.reshape(shp), m_.reshape(shp), v_.reshape(shp), g_big[n]
    shapes = [w[n].shape for n in SMALL]
    d_, m_, v_ = _adamw(_pack([w[n] for n in SMALL]), _pack([g_small[n] for n in SMALL]),
                        _pack([m[n] for n in SMALL]), _pack([v[n] for n in SMALL]))
    for n, dd, mm, vv in zip(SMALL, _unpack(d_, shapes), _unpack(m_, shapes), _unpack(v_, shapes)):
        delta[n], new_m[n], new_v[n], grad_w[n] = dd, mm, vv, g_small[n]

    return (loss_out, grad_x[None], *[grad_w[n] for n in WEIGHTS], *[delta[n] for n in WEIGHTS],
            *[new_m[n] for n in WEIGHTS], *[new_v[n] for n in WEIGHTS])
```

```python
import functools
import math

import jax
import jax.numpy as jnp
from jax import lax
from jax.experimental import pallas as pl
from jax.experimental.pallas import tpu as pltpu

F32 = jnp.float32
BF16 = jnp.bfloat16
HI = lax.Precision.HIGH
MESH = pl.DeviceIdType.MESH

LANES = 128
EPS = 1e-6
SG_HEADS, SG_CHUNK = 4, 128
SC_GROUPS, SC_KERNEL = 4, 3
GDN_HEADS, GDN_CONV = 8, 4
GDN_L = 128
GDN_CPS = 8
SG_W = SG_HEADS * LANES
SC_W = SC_GROUPS * LANES
GDN_W = GDN_HEADS * LANES
IN_COLS = 2 * SG_W + 3 * SC_W + 4 * GDN_W + 2 * GDN_HEADS
PROJ_W = 7168
CB_U, CB_V = 0, 4
CB_GB, CB_GC, CB_X = 8, 12, 16
CB_Q, CB_K, CB_VV, CB_Z, CB_AB = 20, 28, 36, 44, 52
N_SHARD = 4
N_DEV = 8
DEPTH = 2

ADAM_LR, ADAM_B1, ADAM_B2, ADAM_EPS, ADAM_WD, ADAM_STEP = 0.001, 0.9, 0.999, 1e-08, 0.01, 10

VMEM_LIMIT = 56 << 20

WEIGHTS = ['norm_mix', 'w_in', 'sg_ln_g', 'sg_ln_b', 'sg_w', 'sg_b', 'sc_conv', 'gdn_conv', 'gdn_a_log',
           'gdn_dt_bias', 'gdn_norm', 'out_norm_a', 'out_norm_b', 'w_o', 'norm_ffn', 'w_ff1', 'w_ff2', 'norm_ple',
           'w_ple_gate', 'w_ple_proj', 'norm_final']
BIG = ['w_in', 'w_o', 'w_ff1', 'w_ff2', 'w_ple_gate', 'w_ple_proj']
CONVS = ['sc_conv', 'gdn_conv']
SMALL = [n for n in WEIGHTS if n not in BIG]


def _cparams(sem=None):
    return pltpu.CompilerParams(dimension_semantics=sem, vmem_limit_bytes=VMEM_LIMIT)


def _dot(a, b, dims, prec=None):
    return lax.dot_general(a, b, (dims, ((), ())), precision=prec, preferred_element_type=F32)


def _mm(a, b):
    return _dot(a, b, ((1,), (0,)), HI)


def _mm_nt(a, b):
    return _dot(a, b, ((1,), (1,)), HI)


def _mm_tn(a, b):
    return _dot(a, b, ((0,), (0,)), HI)


@jax.custom_vjp
def _bmm(a, b):
    return _dot(a.astype(BF16), b.astype(BF16), ((1,), (0,)))


def _bmm_fwd(a, b):
    return _bmm(a, b), (a, b)


def _bmm_bwd(res, g):
    a, b = res
    gb = g.astype(BF16)
    return _dot(gb, b.astype(BF16), ((1,), (1,))), _dot(a.astype(BF16), gb, ((0,), (0,)))


_bmm.defvjp(_bmm_fwd, _bmm_bwd)


@jax.custom_vjp
def _bmm_tn(a, b):
    return _dot(a.astype(BF16), b.astype(BF16), ((0,), (0,)))


def _bmm_tn_fwd(a, b):
    return _bmm_tn(a, b), (a, b)


def _bmm_tn_bwd(res, g):
    a, b = res
    gb = g.astype(BF16)
    return _dot(b.astype(BF16), gb, ((1,), (1,))), _dot(a.astype(BF16), gb, ((1,), (0,)))


_bmm_tn.defvjp(_bmm_tn_fwd, _bmm_tn_bwd)


def _sigmoid(x):
    return 1.0 / (1.0 + jnp.exp(-x))


def _silu(x):
    return x * _sigmoid(x)


def _gelu(x):
    c = math.sqrt(2.0 / math.pi)
    return 0.5 * x * (1.0 + jnp.tanh(c * (x + 0.044715 * (x * x * x))))


def _softplus(x):
    return jnp.maximum(x, 0.0) + jnp.log(1.0 + jnp.exp(-jnp.abs(x)))


def _rms(x, g):
    return x * lax.rsqrt(jnp.mean(x * x, axis=-1, keepdims=True) + EPS) * g


def _roll_rows(x, shift):
    return pltpu.roll(x, shift % x.shape[0], 0)


@functools.partial(jax.custom_vjp, nondiff_argnums=(1,))
def _shift_down(x, j):
    row = lax.broadcasted_iota(jnp.int32, x.shape, 0)
    return jnp.where(row >= j, _roll_rows(x, j), 0.0)


def _shift_down_fwd(x, j):
    return _shift_down(x, j), None


def _shift_down_bwd(j, _, dy):
    row = lax.broadcasted_iota(jnp.int32, dy.shape, 0)
    return (jnp.where(row < dy.shape[0] - j, _roll_rows(dy, -j), 0.0),)


_shift_down.defvjp(_shift_down_fwd, _shift_down_bwd)


def _causal_conv(x, taps):
    k = len(taps)
    y = taps[k - 1] * x
    for j in range(k - 1):
        y = y + taps[j] * _shift_down(x, k - 1 - j)
    return y


class _In:
    def __init__(self, arr, block, imap, shared=False, gshape=None, gmap=None):
        self.arr, self.block, self.imap, self.shared = arr, block, imap, shared
        self.gshape = arr.shape if gshape is None else gshape
        self.gmap = imap if gmap is None else gmap


def _fused(name, fn, n, ins, outs, douts=None, need=None, gdt=None, add=None):
    n_in, n_out = len(ins), len(outs)
    in_specs = [pl.BlockSpec(s.block, s.imap) for s in ins]
    out_specs = [pl.BlockSpec(bs, im) for _, _, bs, im in outs]
    if douts is None:
        def body(*refs):
            res = fn(*[r[...] for r in refs[:n_in]])
            for r, v in zip(refs[n_in:], res):
                r[...] = v.astype(r.dtype)

        return pl.pallas_call(
            body, name=name, grid=(n,), in_specs=in_specs, out_specs=out_specs,
            out_shape=[jax.ShapeDtypeStruct(s, d) for s, d, _, _ in outs],
            compiler_params=_cparams(("arbitrary",)),
        )(*[s.arr for s in ins])

    gdt = list(gdt) if gdt is not None else [F32] * n_in
    add = dict(add or {})
    gidx = [i for i in range(n_in) if need[i]]
    aidx = [i for i in gidx if i in add]

    def body(*refs):
        in_refs, d_refs = refs[:n_in], refs[n_in:n_in + n_out]
        a_refs = dict(zip(aidx, refs[n_in + n_out:n_in + n_out + len(aidx)]))
        g_refs = refs[n_in + n_out + len(aidx):]
        vals = [r[...] for r in in_refs]

        def f(*dv):
            full = list(vals)
            for i, v in zip(gidx, dv):
                full[i] = v
            return tuple(o.astype(F32) for o in fn(*full))

        _, vjp = jax.vjp(f, *[vals[i].astype(F32) for i in gidx])
        grads = vjp(tuple(r[...].astype(F32) for r in d_refs))
        for i, g_ref, g in zip(gidx, g_refs, grads):
            if i in a_refs:
                g = g + a_refs[i][...].astype(F32)
            if ins[i].shared:
                period = n if ins[i].shared is True else ins[i].shared

                @pl.when(pl.program_id(0) % period == 0)
                def _():
                    g_ref[...] = jnp.zeros_like(g_ref)
                g_ref[...] += g.astype(g_ref.dtype)
            else:
                g_ref[...] = g.astype(g_ref.dtype)

    g_specs = [pl.BlockSpec(ins[i].block, ins[i].gmap) for i in gidx]
    g_shape = [jax.ShapeDtypeStruct(ins[i].gshape, gdt[i]) for i in gidx]
    res = pl.pallas_call(
        body, name=name, grid=(n,), in_specs=in_specs + out_specs + [g_specs[gidx.index(i)] for i in aidx],
        out_specs=g_specs, out_shape=g_shape,
        compiler_params=_cparams(("arbitrary",)),
    )(*[s.arr for s in ins], *douts, *[add[i] for i in aidx])
    full = [None] * n_in
    for i, g in zip(gidx, res):
        full[i] = g
    return full


def _row_in(a, tm, cb=None):
    if cb is None:
        return _In(a, (tm, a.shape[1]), lambda i: (i, 0))
    return _In(a, (tm, LANES), lambda i: (i, cb), gshape=(a.shape[0], LANES), gmap=lambda i: (i, 0))


def _row_shared(a):
    return _In(a, a.shape, lambda i: (0, 0), shared=True)


def _row_out(t, w, dt, tm):
    return ((t, w), dt, (tm, w), lambda i: (i, 0))


def _col_in(a, base, nblk):
    t = a.shape[0]
    return _In(a, (t, LANES), lambda i: (0, base + i), gshape=(t, nblk * LANES), gmap=lambda i: (0, i))


def _col_par(a):
    return _In(a, (None,) + a.shape[1:], lambda i: (i, 0, 0))


def _col_out(t, w, dt, base=0):
    return ((t, w), dt, (t, LANES), lambda i: (0, base + i))


def _matmul(name, a, b, mode, m, n, k, tm, tn, tk, b_spec=None, epilogue=None, extras=(), outs=None, out_dtype=F32,
            n_outer=False):
    tm, tn, tk = min(tm, m), min(tn, n), min(tk, k)
    assert m % tm == 0 and n % tn == 0 and k % tk == 0, (name, m, n, k, tm, tn, tk)
    nk = k // tk
    a_spec = (pl.BlockSpec((tk, tm), lambda i, j, kk: (kk, i)) if mode == "tn"
              else pl.BlockSpec((tm, tk), lambda i, j, kk: (i, kk)))
    if b_spec is None:
        b_spec = (pl.BlockSpec((tn, tk), lambda i, j, kk: (j, kk)) if mode == "nt"
                  else pl.BlockSpec((tk, tn), lambda i, j, kk: (kk, j)))
    dims = {"nn": ((1,), (0,)), "nt": ((1,), (1,)), "tn": ((0,), (0,))}[mode]
    if outs is None:
        outs = [((m, n), out_dtype, (tm, tn), lambda i, j, kk: (i, j))]
    if epilogue is None:
        epilogue = lambda acc: (acc,)
    n_ex = len(extras)

    def body(*refs):
        a_ref, b_ref = refs[0], refs[1]
        ex_refs = refs[2:2 + n_ex]
        o_refs = refs[2 + n_ex:2 + n_ex + len(outs)]
        part = _dot(a_ref[...].astype(BF16), b_ref[...].astype(BF16), dims)

        def finish(acc):
            for r, v in zip(o_refs, epilogue(acc, *[e[...] for e in ex_refs])):
                r[...] = v.astype(r.dtype)

        if nk == 1:
            finish(part)
        else:
            acc_ref = refs[-1]
            kk = pl.program_id(2)

            @pl.when(kk == 0)
            def _():
                acc_ref[...] = part

            @pl.when(kk > 0)
            def _():
                acc_ref[...] += part

            @pl.when(kk == nk - 1)
            def _():
                finish(acc_ref[...])

    ex_specs = [pl.BlockSpec((tm, tn), lambda i, j, kk: (i, j)) for _ in extras]
    out_specs = [pl.BlockSpec(bs, im) for _, _, bs, im in outs]
    in_specs = [a_spec, b_spec] + ex_specs
    grid = (m // tm, n // tn, nk)
    if n_outer:
        swap = lambda sp: pl.BlockSpec(sp.block_shape, functools.partial(lambda f, j, i, kk: f(i, j, kk), sp.index_map))
        in_specs, out_specs, grid = [swap(sp) for sp in in_specs], [swap(sp) for sp in out_specs], (n // tn, m // tm, nk)
    res = pl.pallas_call(
        body, name=name, grid=grid,
        in_specs=in_specs,
        out_specs=out_specs,
        out_shape=[jax.ShapeDtypeStruct(s, d) for s, d, _, _ in outs],
        scratch_shapes=[pltpu.VMEM((tm, tn), F32)] if nk > 1 else [],
        compiler_params=_cparams(("parallel", "parallel", "arbitrary")),
    )(a, b, *extras)
    return res if len(res) > 1 else res[0]


def _fn_sgu(u_pre, v_pre, ln_g, ln_b, w, bb, na):
    t = u_pre.shape[0]
    u = _gelu(u_pre)
    v = _gelu(v_pre)
    mu = jnp.mean(v, axis=-1, keepdims=True)
    vc = v - mu
    vh = vc * lax.rsqrt(jnp.mean(vc * vc, axis=-1, keepdims=True) + EPS) * ln_g + ln_b
    ri = lax.broadcasted_iota(jnp.int32, w.shape, 0)
    ci = lax.broadcasted_iota(jnp.int32, w.shape, 1)
    wc = jnp.where(ri >= ci, w, 0.0)
    f = jnp.concatenate([_bmm(wc, vh[c * SG_CHUNK:(c + 1) * SG_CHUNK]) + bb for c in range(t // SG_CHUNK)], axis=0)
    return (_rms(u * f, na),)


def _fn_sconv(gb, gc, xin, w0, w1, w2, nb):
    return (_rms(gb * _causal_conv(gc * xin, (w0, w1, w2)), nb),)


def _fn_gdn_qk(pre, w0, w1, w2, w3):
    a = _silu(_causal_conv(pre, (w0, w1, w2, w3)))
    return (a * lax.rsqrt(jnp.sum(a * a, axis=-1, keepdims=True) + EPS),)


def _fn_gdn_v(pre, w0, w1, w2, w3):
    return (_silu(_causal_conv(pre, (w0, w1, w2, w3))),)


def _fn_gdn_gates(ab, a_log, dt_bias):
    lane = lax.broadcasted_iota(jnp.int32, ab.shape, 1)
    g = -jnp.exp(a_log) * _softplus(ab + dt_bias)
    return (jnp.where(lane < GDN_HEADS, g, jnp.where(lane < 2 * GDN_HEADS, _sigmoid(ab), 0.0)),)


def _solve_unit_lower_impl(mats):
    n = mats[0].shape[0]
    ri = lax.broadcasted_iota(jnp.int32, (n, n), 0)
    ci = lax.broadcasted_iota(jnp.int32, (n, n), 1)
    ts = [(ri == ci).astype(F32)] * len(mats)
    sh = 0
    while (1 << sh) < n:
        rb = jnp.right_shift(ri, sh)
        cb = jnp.right_shift(ci, sh)
        off = ((rb & 1) == 1) & (cb == rb - 1)
        us = [_mm(t, jnp.where(off, a, 0.0)) for t, a in zip(ts, mats)]
        ts = [t - _mm(u, t) for t, u in zip(ts, us)]
        sh += 1
    return tuple(ts)


def _solve_cotangents(ts, dts):
    us = [_mm_nt(dt, t) for t, dt in zip(ts, dts)]
    return tuple(-_mm_tn(t, u) for t, u in zip(ts, us))


@jax.custom_vjp
def _solve_unit_lower(mats):
    return _solve_unit_lower_impl(mats)


def _solve_unit_lower_fwd(mats):
    ts = _solve_unit_lower_impl(mats)
    return ts, ts


def _solve_unit_lower_bwd(ts, dts):
    return (_solve_cotangents(ts, dts),)


_solve_unit_lower.defvjp(_solve_unit_lower_fwd, _solve_unit_lower_bwd)


@jax.custom_vjp
def _solved_unit_lower(mats, ts):
    return ts


def _solved_unit_lower_fwd(mats, ts):
    return ts, ts


def _solved_unit_lower_bwd(ts, dts):
    return _solve_cotangents(ts, dts), tuple(jnp.zeros_like(t) for t in ts)


_solved_unit_lower.defvjp(_solved_unit_lower_fwd, _solved_unit_lower_bwd)


def _fn_gdn_wy(q, k, v, gates, pick_g, pick_b, t_saved=None):
    n, dk = GDN_L, q.shape[1]
    rows = [slice(c * n, (c + 1) * n) for c in range(q.shape[0] // n)]
    ri = lax.broadcasted_iota(jnp.int32, (n, n), 0)
    ci = lax.broadcasted_iota(jnp.int32, (n, n), 1)
    incl = ri >= ci
    eye = (ri == ci).astype(F32)
    last = lax.broadcasted_iota(jnp.int32, (n, 1), 0) == (n - 1)
    qs = [q[r] * (dk ** -0.5) for r in rows]
    ks = [k[r] for r in rows]
    gcs, betas, d_incl = [], [], []
    for r in rows:
        g = jnp.sum(gates[r] * pick_g, axis=1, keepdims=True)
        betas.append(jnp.sum(gates[r] * pick_b, axis=1, keepdims=True))
        g_row = jnp.sum(eye * g, axis=0, keepdims=True)
        gc = jnp.sum(jnp.where(incl, g_row, 0.0), axis=1, keepdims=True)
        gc_row = jnp.sum(eye * gc, axis=0, keepdims=True)
        gcs.append(gc)
        d_incl.append(jnp.where(incl, jnp.exp(jnp.where(incl, gc - gc_row, 0.0)), 0.0))
    kbs = [kk * b for kk, b in zip(ks, betas)]
    mats = tuple(_mm_nt(kb, kk) * jnp.where(ri > ci, d, 0.0) for kb, kk, d in zip(kbs, ks, d_incl))
    ts = _solve_unit_lower(mats) if t_saved is None else _solved_unit_lower(mats, tuple(t_saved[r] for r in rows))
    egs = [jnp.exp(gc) for gc in gcs]
    values = [_mm(t, v[r] * b) for t, r, b in zip(ts, rows, betas)]
    kcds = [_mm(t, kb * eg) for t, kb, eg in zip(ts, kbs, egs)]
    intras = [_mm_nt(qq, kk) * d for qq, kk, d in zip(qs, ks, d_incl)]
    g_lasts = [jnp.sum(jnp.where(last, gc, 0.0), axis=0, keepdims=True) for gc in gcs]
    qes = [qq * eg for qq, eg in zip(qs, egs)]
    kts = [kk * jnp.exp(gl - gc) for kk, gl, gc in zip(ks, g_lasts, gcs)]
    carries = [jnp.broadcast_to(jnp.exp(gl), (8, LANES)) for gl in g_lasts]
    cat = lambda parts: jnp.concatenate(parts, axis=0)
    res = (cat(values), cat(kcds), cat(qes), cat(kts), cat(intras), cat(carries))
    return res + (cat(ts),) if t_saved is None else res


def _gdn_step(s, value, kcd, qe, kt, intra, carry):
    v_new = value - _bmm(kcd, s)
    o = _bmm(qe, s) + _bmm(intra, v_new)
    return s * carry + _bmm_tn(kt, v_new), o


def _gdn_post(o, z, nrm):
    return _rms(o, nrm) * _silu(z)


def _gdn_wy_ins(q, k, v, gates, t_saved=None):
    t = q.shape[0]
    rb = min(GDN_CPS * GDN_L, t)
    hd = GDN_HEADS
    lane = jnp.arange(LANES)[None, None, :]
    pick_g = (lane == jnp.arange(hd)[:, None, None]).astype(F32)
    pick_b = (lane == jnp.arange(hd)[:, None, None] + hd).astype(F32)
    blk = lambda a: _In(a, (rb, LANES), lambda i: (i // hd, i % hd))
    par = lambda a: _In(a, (None, 1, LANES), lambda i: (i % hd, 0, 0))
    ins = [blk(q), blk(k), blk(v), _In(gates, (rb, LANES), lambda i: (i // hd, 0), shared=hd), par(pick_g),
           par(pick_b)]
    wide = lambda dt: ((t, GDN_W), dt, (rb, LANES), lambda i: (i // hd, i % hd))
    carry = ((hd, 8 * (t // GDN_L), LANES), F32, (None, 8 * (rb // GDN_L), LANES), lambda i: (i % hd, i // hd, 0))
    outs = [wide(F32), wide(BF16), wide(BF16), wide(BF16), wide(BF16), carry]
    if t_saved is None:
        outs.append(wide(F32))
    else:
        ins.append(blk(t_saved))
    return (t // rb) * hd, ins, outs


def _gdn_scan_specs(t):
    col = lambda base: pl.BlockSpec((t, LANES), lambda h: (0, base + h))
    carry = pl.BlockSpec((None, 8 * (t // GDN_L), LANES), lambda h: (h, 0, 0))
    par = pl.BlockSpec((None, 1, LANES), lambda h: (h, 0, 0))
    return col, carry, par


def _gdn_scan_fwd(wy, proj, nrm):
    t = wy[0].shape[0]
    nc = t // GDN_L

    def body(val_ref, kcd_ref, qe_ref, kt_ref, in_ref, cy_ref, z_ref, n_ref, y_ref, o_scr):
        def step(c, s):
            rows = pl.ds(pl.multiple_of(c * GDN_L, GDN_L), GDN_L)
            s, o = _gdn_step(s, val_ref[rows, :], kcd_ref[rows, :], qe_ref[rows, :], kt_ref[rows, :],
                             in_ref[rows, :], cy_ref[pl.ds(pl.multiple_of(c * 8, 8), 1), :])
            o_scr[rows, :] = o
            return s

        lax.fori_loop(0, nc, step, jnp.zeros((LANES, LANES), F32))
        y_ref[...] = _gdn_post(o_scr[...], z_ref[...], n_ref[...]).astype(y_ref.dtype)

    col, carry, par = _gdn_scan_specs(t)
    return pl.pallas_call(
        body, name="gdn_scan_fwd", grid=(GDN_HEADS,),
        in_specs=[col(0)] * 5 + [carry, col(CB_Z), par],
        out_specs=col(0), out_shape=jax.ShapeDtypeStruct((t, GDN_W), BF16),
        scratch_shapes=[pltpu.VMEM((t, LANES), F32)],
        compiler_params=_cparams(("arbitrary",)),
    )(*wy, proj, nrm)


def _gdn_scan_bwd(wy, proj, nrm, dy, dy_base):
    t = wy[0].shape[0]
    nc = t // GDN_L

    def body(val_ref, kcd_ref, qe_ref, kt_ref, in_ref, cy_ref, z_ref, n_ref, dy_ref,
             dval_ref, dkcd_ref, dqe_ref, dkt_ref, din_ref, dcy_ref, dz_ref, dn_ref, o_scr, s_scr):
        def operands(c):
            rows = pl.ds(pl.multiple_of(c * GDN_L, GDN_L), GDN_L)
            return rows, (val_ref[rows, :], kcd_ref[rows, :], qe_ref[rows, :], kt_ref[rows, :], in_ref[rows, :],
                          cy_ref[pl.ds(pl.multiple_of(c * 8, 8), 1), :])

        def step(c, s):
            rows, ops = operands(c)
            s_scr[c] = s
            s, o = _gdn_step(s, *ops)
            o_scr[rows, :] = o
            return s

        lax.fori_loop(0, nc, step, jnp.zeros((LANES, LANES), F32))
        _, vjp_post = jax.vjp(_gdn_post, o_scr[...], z_ref[...], n_ref[...])
        do, dz, dn = vjp_post(dy_ref[...].astype(F32))
        dz_ref[...] = dz.astype(dz_ref.dtype)
        dn_ref[...] = dn
        o_scr[...] = do
        dcy_ref[...] = jnp.zeros_like(dcy_ref)

        def rstep(i, ds):
            c = nc - 1 - i
            rows, ops = operands(c)
            _, vjp_c = jax.vjp(_gdn_step, s_scr[c], *[o.astype(F32) for o in ops])
            ds, dval, dkcd, dqe, dkt, din, dcy = vjp_c((ds, o_scr[rows, :]))
            for r, g in ((dval_ref, dval), (dkcd_ref, dkcd), (dqe_ref, dqe), (dkt_ref, dkt), (din_ref, din)):
                r[rows, :] = g.astype(r.dtype)
            dcy_ref[pl.ds(pl.multiple_of(c * 8, 8), 1), :] = dcy
            return ds

        lax.fori_loop(0, nc, rstep, jnp.zeros((LANES, LANES), F32))

    col, carry, par = _gdn_scan_specs(t)
    wide = jax.ShapeDtypeStruct((t, GDN_W), BF16)
    return pl.pallas_call(
        body, name="gdn_scan_bwd", grid=(GDN_HEADS,),
        in_specs=[col(0)] * 5 + [carry, col(CB_Z), par, col(dy_base)],
        out_specs=[col(0)] * 5 + [carry, col(0), par],
        out_shape=[wide] * 5 + [jax.ShapeDtypeStruct(wy[5].shape, F32), wide,
                                jax.ShapeDtypeStruct((GDN_HEADS, 1, LANES), F32)],
        scratch_shapes=[pltpu.VMEM((t, LANES), F32), pltpu.VMEM((nc, LANES, LANES), F32)],
        compiler_params=_cparams(("arbitrary",)),
    )(*wy, proj, nrm, dy)


TM = 512
TR = 256


def _rms_fwd(name, h, g):
    t, d = h.shape
    tm = min(TR, t)
    return _fused(name, lambda hb, gb: (_rms(hb, gb),), t // tm, [_row_in(h, tm), _row_shared(g)],
                  [_row_out(t, d, BF16, tm)])[0]


def _rms_bwd(name, h, g, dxn, dh_next):
    t, d = h.shape
    tm = min(TR, t)
    dh, dg = _fused(name, lambda hb, gb: (_rms(hb, gb),), t // tm, [_row_in(h, tm), _row_shared(g)],
                    [_row_out(t, d, F32, tm)], douts=[dxn], need=[True, True], add={0: dh_next})
    return dh, dg


def _mixer_ins(proj, lw):
    sgu = [_col_in(proj, CB_U, SG_HEADS), _col_in(proj, CB_V, SG_HEADS), _col_par(lw['sg_ln_g']),
           _col_par(lw['sg_ln_b']), _col_par(lw['sg_w']), _col_par(lw['sg_bb']), _col_par(lw['out_norm_a'])]
    sconv = [_col_in(proj, CB_GB, SC_GROUPS), _col_in(proj, CB_GC, SC_GROUPS), _col_in(proj, CB_X, SC_GROUPS)] + \
            [_col_par(w) for w in lw['sc_taps']] + [_col_par(lw['out_norm_b'])]
    gq = [_col_in(proj, CB_Q, GDN_HEADS)] + [_col_par(w) for w in lw['q_taps']]
    gk = [_col_in(proj, CB_K, GDN_HEADS)] + [_col_par(w) for w in lw['k_taps']]
    gv = [_col_in(proj, CB_VV, GDN_HEADS)] + [_col_par(w) for w in lw['v_taps']]
    return sgu, sconv, gq, gk, gv


def _gates_ins(proj, lw, tm):
    return [_row_in(proj, tm, CB_AB), _row_shared(lw['a_log_row']), _row_shared(lw['dt_bias_row'])]


def _layer_fwd(h, p_l, lw):
    t, d = h.shape
    dff = lw['w_ff2'].shape[0]
    xn = _rms_fwd("rms_fwd", h, lw['norm_mix'])
    proj = _matmul("proj_fwd", xn, lw['w_in'], "nn", t, PROJ_W, d, TM, 1024, d)
    sgu, sconv, gq, gk, gv = _mixer_ins(proj, lw)
    ya = _fused("sgu_fwd", _fn_sgu, SG_HEADS, sgu, [_col_out(t, SG_W, BF16)])[0]
    yb = _fused("sconv_fwd", _fn_sconv, SC_GROUPS, sconv, [_col_out(t, SC_W, BF16)])[0]
    q = _fused("gdn_q_fwd", _fn_gdn_qk, GDN_HEADS, gq, [_col_out(t, GDN_W, F32)])[0]
    k = _fused("gdn_k_fwd", _fn_gdn_qk, GDN_HEADS, gk, [_col_out(t, GDN_W, F32)])[0]
    v = _fused("gdn_v_fwd", _fn_gdn_v, GDN_HEADS, gv, [_col_out(t, GDN_W, F32)])[0]
    tm = min(TR, t)
    gates = _fused("gdn_gates_fwd", _fn_gdn_gates, t // tm, _gates_ins(proj, lw, tm),
                   [_row_out(t, LANES, F32, tm)])[0]
    n_wy, wy_ins, wy_outs = _gdn_wy_ins(q, k, v, gates)
    wy = _fused("gdn_wy_fwd", _fn_gdn_wy, n_wy, wy_ins, wy_outs)
    wy, wy_t = wy[:6], wy[6]
    yc = _gdn_scan_fwd(wy, proj, lw['gdn_norm'])
    ycat = jnp.concatenate([ya, yb, yc], axis=1)
    h2 = _matmul("wo_fwd", ycat, lw['w_o'], "nn", t, d, d, TM, 1024, d,
                 epilogue=lambda acc, hb: (hb + acc,), extras=(h,))
    hn = _rms_fwd("rms_fwd", h2, lw['norm_ffn'])
    per = d // 1024
    s, r = _matmul("ff1_fwd", hn, lw['w_ff1'], "nn", t, dff, d, TM, 1024, d,
                   b_spec=pl.BlockSpec((None, d, 1024), lambda i, j, kk: (j // per, 0, j % per)),
                   epilogue=lambda acc: (jnp.maximum(acc, 0.0), jnp.square(jnp.maximum(acc, 0.0))),
                   outs=[((t, dff), BF16, (min(TM, t), 1024), lambda i, j, kk: (i, j))] * 2)
    h3 = _matmul("ff2_fwd", r, lw['w_ff2'], "nn", t, d, dff, TM, 1024, 4096,
                 epilogue=lambda acc, hb: (hb + acc,), extras=(h2,))
    hn2 = _rms_fwd("rms_fwd", h3, lw['norm_ple'])
    pp = _matmul("ple_proj_fwd", p_l, lw['w_ple_proj'], "nn", t, d, p_l.shape[1], TM, 1024, p_l.shape[1])

    def gate_epilogue(acc, hb, ppb):
        sg = _sigmoid(acc)
        return hb + ppb * sg, sg

    h4, gate = _matmul("ple_gate_fwd", hn2, lw['w_ple_gate'], "nn", t, d, d, TM, 1024, d, epilogue=gate_epilogue,
                       extras=(h3, pp), outs=[((t, d), F32, (min(TM, t), 1024), lambda i, j, kk: (i, j))] * 2)
    saved = dict(h=h, xn=xn, proj=proj, q=q, k=k, v=v, gates=gates, ycat=ycat, h2=h2, hn=hn, s=s, r=r, h3=h3,
                 hn2=hn2, pp=pp, gate=gate, p=p_l, wy=wy, wy_t=wy_t)
    return h4, saved


def _layer_bwd(dh4, sv, lw):
    t, d = dh4.shape
    dff = lw['w_ff2'].shape[0]
    tm = min(TR, t)
    g = {}
    dacc, dpp = _fused("ple_bwd_gate", lambda dh, pp, gt: (dh * pp * gt * (1.0 - gt), dh * gt), t // tm,
                       [_row_in(dh4, tm), _row_in(sv['pp'], tm), _row_in(sv['gate'], tm)],
                       [_row_out(t, d, BF16, tm)] * 2)
    g['w_ple_gate'] = _matmul("dw_ple_gate", sv['hn2'], dacc, "tn", d, d, t, TM, 1024, t, out_dtype=BF16,
                              n_outer=True)
    g['w_ple_proj'] = _matmul("dw_ple_proj", sv['p'], dpp, "tn", sv['p'].shape[1], d, t, TM, 1024, t,
                              out_dtype=BF16, n_outer=True)
    dhn2 = _matmul("dx_ple_gate", dacc, lw['w_ple_gate'], "nt", t, d, d, TM, 1024, d)
    dh3, g['norm_ple'] = _rms_bwd("rms_bwd", sv['h3'], lw['norm_ple'], dhn2, dh4)
    da = _matmul("dx_ff2", dh3, lw['w_ff2'], "nt", t, dff, d, TM, 1024, d,
                 epilogue=lambda acc, sb: (acc * (2.0 * sb.astype(F32)),), extras=(sv['s'],),
                 outs=[((t, dff), BF16, (min(TM, t), 1024), lambda i, j, kk: (i, j))])
    g['w_ff2'] = _matmul("dw_ff2", sv['r'], dh3, "tn", dff, d, t, TM, 512, t, out_dtype=BF16, n_outer=True)
    per = d // 1024
    dhn = _matmul("dx_ff1", da, lw['w_ff1'], "nt", t, d, dff, TM, 1024, d,
                  b_spec=pl.BlockSpec((None, 1024, d), lambda i, j, kk: (kk, j, 0)))
    g['w_ff1'] = _matmul("dw_ff1", sv['hn'], da, "tn", d, dff, t, TM, 1024, t, n_outer=True,
                         outs=[((N_SHARD, d, d), BF16, (None, TM, 1024), lambda i, j, kk: (j // per, i, j % per))])
    dh2, g['norm_ffn'] = _rms_bwd("rms_bwd", sv['h2'], lw['norm_ffn'], dhn, dh3)
    dycat = _matmul("dx_o", dh2, lw['w_o'], "nt", t, d, d, TM, 1024, d)
    g['w_o'] = _matmul("dw_o", sv['ycat'], dh2, "tn", d, d, t, TM, 512, t, out_dtype=BF16, n_outer=True)
    proj = sv['proj']
    sgu, sconv, gq, gk, gv = _mixer_ins(proj, lw)
    bf2 = [BF16, BF16]
    r_ = _fused("sgu_bwd", _fn_sgu, SG_HEADS, sgu, [_col_out(t, d, F32, 0)], douts=[dycat], need=[True] * 7,
                gdt=bf2 + [F32] * 5)
    du, dv_, g['sg_ln_g'], g['sg_ln_b'], g['sg_w'], g['sg_bb'], g['out_norm_a'] = r_
    r_ = _fused("sconv_bwd", _fn_sconv, SC_GROUPS, sconv, [_col_out(t, d, F32, SG_HEADS)], douts=[dycat],
                need=[True] * 7, gdt=[BF16] * 3 + [F32] * 4)
    dgb, dgc, dxin = r_[:3]
    g['sc_taps'], g['out_norm_b'] = r_[3:6], r_[6]
    r_ = _gdn_scan_bwd(sv['wy'], proj, lw['gdn_norm'], dycat, SG_HEADS + SC_GROUPS)
    dwy, dz, g['gdn_norm'] = r_[:6], r_[6], r_[7]
    n_wy, wy_ins, wy_outs = _gdn_wy_ins(sv['q'], sv['k'], sv['v'], sv['gates'], sv['wy_t'])
    dq, dk, dvv, dgates = _fused("gdn_wy_bwd", _fn_gdn_wy, n_wy, wy_ins, wy_outs, douts=dwy,
                                 need=[True] * 4 + [False] * 3)[:4]
    one = [_col_out(t, GDN_W, F32)]
    r_ = _fused("gdn_q_bwd", _fn_gdn_qk, GDN_HEADS, gq, one, douts=[dq], need=[True] * 5, gdt=[BF16] + [F32] * 4)
    dpq, g['q_taps'] = r_[0], r_[1:]
    r_ = _fused("gdn_k_bwd", _fn_gdn_qk, GDN_HEADS, gk, one, douts=[dk], need=[True] * 5, gdt=[BF16] + [F32] * 4)
    dpk, g['k_taps'] = r_[0], r_[1:]
    r_ = _fused("gdn_v_bwd", _fn_gdn_v, GDN_HEADS, gv, one, douts=[dvv], need=[True] * 5, gdt=[BF16] + [F32] * 4)
    dpv, g['v_taps'] = r_[0], r_[1:]
    dab, g['a_log_row'], g['dt_bias_row'] = _fused(
        "gdn_gates_bwd", _fn_gdn_gates, t // tm, _gates_ins(proj, lw, tm), [_row_out(t, LANES, F32, tm)],
        douts=[dgates], need=[True] * 3, gdt=[BF16, F32, F32])
    pad = jnp.zeros((t, PROJ_W - (CB_AB + 1) * LANES), BF16)
    dproj = jnp.concatenate([du, dv_, dgb, dgc, dxin, dpq, dpk, dpv, dz, dab, pad], axis=1)
    dxn = _matmul("dx_in", dproj, lw['w_in'], "nt", t, d, PROJ_W, TM, 1024, PROJ_W // 2)
    g['w_in'] = _matmul("dw_in", sv['xn'], dproj, "tn", d, PROJ_W, t, TM, 1024, t, out_dtype=BF16, n_outer=True)
    dh, g['norm_mix'] = _rms_bwd("rms_bwd", sv['h'], lw['norm_mix'], dxn, dh2)
    return dh, g


def _loss_grad(h, g, tgt):
    t, d = h.shape
    tm = min(TR, t)

    def body(h_ref, g_ref, t_ref, loss_ref, dh_ref, dg_ref):
        y, vjp = jax.vjp(_rms, h_ref[...], g_ref[...])
        e = y - t_ref[...]
        dh, dg = vjp(e * (1.0 / d))

        @pl.when(pl.program_id(0) == 0)
        def _():
            loss_ref[...] = jnp.zeros_like(loss_ref)
            dg_ref[...] = jnp.zeros_like(dg_ref)

        loss_ref[...] += jnp.sum(jnp.sum(e * e, axis=1, keepdims=True), axis=0, keepdims=True) * (0.5 / d)
        dh_ref[...] = dh
        dg_ref[...] += dg

    row = pl.BlockSpec((tm, d), lambda i: (i, 0))
    return pl.pallas_call(
        body, name="loss_grad", grid=(t // tm,),
        in_specs=[row, pl.BlockSpec((1, d), lambda i: (0, 0)), row],
        out_specs=[pl.BlockSpec((1, LANES), lambda i: (0, 0)), row, pl.BlockSpec((1, d), lambda i: (0, 0))],
        out_shape=[jax.ShapeDtypeStruct((1, LANES), F32), jax.ShapeDtypeStruct((t, d), F32),
                   jax.ShapeDtypeStruct((1, d), F32)],
        compiler_params=_cparams(("arbitrary",)),
    )(h, g, tgt)


def _layer_weights(l, full, small):
    d = full['w_o'].shape[-1]
    lw = {}
    w_in = jnp.transpose(full['w_in'][l], (1, 0, 2)).reshape(d, IN_COLS)
    lw['w_in'] = jnp.pad(w_in, ((0, 0), (0, PROJ_W - IN_COLS)))
    lw['w_o'] = full['w_o'][l].reshape(d, d)
    lw['w_ff1'] = full['w_ff1'][l]
    lw['w_ff2'] = full['w_ff2'][l].reshape(-1, d)
    lw['w_ple_gate'] = full['w_ple_gate'][l].reshape(d, d)
    wpp = full['w_ple_proj'][l]
    lw['w_ple_proj'] = jnp.transpose(wpp, (1, 0, 2)).reshape(wpp.shape[1], d)
    for n in ('norm_mix', 'norm_ffn', 'norm_ple'):
        lw[n] = small[n][l].reshape(1, d)
    lw['sg_ln_g'] = small['sg_ln_g'][l].reshape(SG_HEADS, 1, LANES)
    lw['sg_ln_b'] = small['sg_ln_b'][l].reshape(SG_HEADS, 1, LANES)
    lw['sg_w'] = small['sg_w'][l]
    lw['sg_bb'] = jnp.broadcast_to(small['sg_b'][l][:, :, None], (SG_HEADS, SG_CHUNK, LANES))
    lw['out_norm_a'] = small['out_norm_a'][l].reshape(SG_HEADS, 1, LANES)
    lw['out_norm_b'] = small['out_norm_b'][l].reshape(SC_GROUPS, 1, LANES)
    lw['gdn_norm'] = jnp.broadcast_to(small['gdn_norm'][l].reshape(1, 1, LANES), (GDN_HEADS, 1, LANES))
    lw['a_log_row'] = jnp.pad(small['gdn_a_log'][l].reshape(1, GDN_HEADS), ((0, 0), (0, LANES - GDN_HEADS)))
    lw['dt_bias_row'] = jnp.pad(small['gdn_dt_bias'][l].reshape(1, GDN_HEADS), ((0, 0), (0, LANES - GDN_HEADS)))
    sc = full['sc_conv'][l]
    lw['sc_taps'] = [sc[:, j:j + 1, :] for j in range(SC_KERNEL)]
    gc = jnp.transpose(full['gdn_conv'][l], (1, 0, 2)).reshape(GDN_CONV, 3 * GDN_W)
    for i, nm in enumerate(('q_taps', 'k_taps', 'v_taps')):
        part = gc[:, i * GDN_W:(i + 1) * GDN_W].reshape(GDN_CONV, GDN_HEADS, 1, LANES)
        lw[nm] = [part[j] for j in range(GDN_CONV)]
    return lw


def _weight_grads(g):
    d = g['w_o'].shape[0]
    out = {}
    w_in = g['w_in'][:, :IN_COLS].reshape(d, N_SHARD, IN_COLS // N_SHARD)
    out['w_in'] = jnp.transpose(w_in, (1, 0, 2))
    out['w_o'] = g['w_o'].reshape(N_SHARD, d // N_SHARD, d)
    out['w_ff1'] = g['w_ff1']
    out['w_ff2'] = g['w_ff2'].reshape(N_SHARD, -1, d)
    out['w_ple_gate'] = g['w_ple_gate'].reshape(N_SHARD, d // N_SHARD, d)
    wpp = g['w_ple_proj']
    out['w_ple_proj'] = jnp.transpose(wpp.reshape(wpp.shape[0], N_SHARD, d // N_SHARD), (1, 0, 2))
    for n in ('norm_mix', 'norm_ffn', 'norm_ple'):
        out[n] = g[n].reshape(d)
    out['sg_ln_g'] = g['sg_ln_g'].reshape(SG_W)
    out['sg_ln_b'] = g['sg_ln_b'].reshape(SG_W)
    out['sg_w'] = g['sg_w']
    out['sg_b'] = jnp.sum(g['sg_bb'], axis=2)
    out['out_norm_a'] = g['out_norm_a'].reshape(SG_W)
    out['out_norm_b'] = g['out_norm_b'].reshape(SC_W)
    out['gdn_norm'] = jnp.sum(g['gdn_norm'], axis=(0, 1))
    out['gdn_a_log'] = g['a_log_row'][0, :GDN_HEADS]
    out['gdn_dt_bias'] = g['dt_bias_row'][0, :GDN_HEADS]
    out['sc_conv'] = jnp.concatenate([w.reshape(1, SC_W) for w in g['sc_taps']], axis=0)
    taps = [jnp.concatenate([g[nm][j].reshape(1, GDN_W) for nm in ('q_taps', 'k_taps', 'v_taps')], axis=1)
            for j in range(GDN_CONV)]
    out['gdn_conv'] = jnp.concatenate(taps, axis=0)
    return out


def _local_step(x, p, tgt, full, small):
    h = x
    saved, lws = [], []
    for l in range(DEPTH):
        lw = _layer_weights(l, full, small)
        h, sv = _layer_fwd(h, p[l], lw)
        saved.append(sv)
        lws.append(lw)
    loss, dh, dnf = _loss_grad(h, small['norm_final'].reshape(1, -1), tgt)
    per_layer = [None] * DEPTH
    for l in reversed(range(DEPTH)):
        dh, g = _layer_bwd(dh, saved[l], lws[l])
        per_layer[l] = _weight_grads(g)
    grads = {n: jnp.stack([per_layer[l][n] for l in range(DEPTH)], axis=0) for n in per_layer[0]}
    grads['norm_final'] = dnf.reshape(-1)
    return loss, dh, grads


ANY = pl.BlockSpec(memory_space=pl.ANY)


def _place():
    x, y, c = lax.axis_index("x"), lax.axis_index("y"), lax.axis_index("c")
    chips = [(1 - x, y), (x, 1 - y), (1 - x, 1 - y)]
    return x, y, c, chips


def _place_shard(wsh, shard, dtype):
    dp, r, cc = wsh.shape
    tr = min(TR, r)
    nb = r // tr

    def body(idx_ref, w_ref, o_ref):
        o_ref[...] = w_ref[...].astype(o_ref.dtype)

    return pl.pallas_call(
        body, name="ag_place_shard",
        grid_spec=pltpu.PrefetchScalarGridSpec(
            num_scalar_prefetch=1, grid=(dp * nb,),
            in_specs=[pl.BlockSpec((None, tr, cc), lambda i, ix: (i // nb, i % nb, 0))],
            out_specs=pl.BlockSpec((None, None, tr, cc), lambda i, ix: (i // nb, ix[0], i % nb, 0))),
        out_shape=jax.ShapeDtypeStruct((dp, N_SHARD, r, cc), dtype),
        compiler_params=_cparams(("arbitrary",)),
    )(shard, wsh)


def _all_gather(bufs):
    n = len(bufs)

    def body(*refs):
        outs = refs[n:2 * n]
        send_sems, recv_sems = refs[2 * n:]
        x, y, c, chips = _place()
        me = 2 * x + y
        sibling = (x, y, 1 - c)

        def cp(w, k, shard, layer, to):
            blk = outs[w].at[layer, shard]
            return pltpu.make_async_remote_copy(src_ref=blk, dst_ref=blk, send_sem=send_sems.at[w, k],
                                                recv_sem=recv_sems.at[w, k], device_id=to, device_id_type=MESH)

        sends = []
        for w in range(n):
            for j, chip in enumerate(chips):
                s = cp(w, j, me, c, (*chip, c))
                s.start()
                sends.append(s)
        for w in range(n):
            for j, (px, py) in enumerate(chips):
                cp(w, j, 2 * px + py, c, sibling).wait_recv()
                s = cp(w, 3 + j, 2 * px + py, c, sibling)
                s.start()
                sends.append(s)
        for w in range(n):
            for j, (px, py) in enumerate(chips):
                cp(w, 3 + j, 2 * px + py, 1 - c, sibling).wait_recv()
        for s in sends:
            s.wait_send()

    return pl.pallas_call(
        body, name="all_gather",
        in_specs=[ANY] * n, out_specs=[ANY] * n,
        out_shape=[jax.ShapeDtypeStruct(b.shape, b.dtype) for b in bufs],
        input_output_aliases={w: w for w in range(n)},
        scratch_shapes=[pltpu.SemaphoreType.DMA((n, 6)), pltpu.SemaphoreType.DMA((n, 6))],
    )(*bufs)


def _swap_layers(grads):
    n = len(grads)

    def body(*refs):
        srcs, outs = refs[:n], refs[n:2 * n]
        send_sems, recv_sems = refs[2 * n:]
        x, y, c, _ = _place()
        cps = [pltpu.make_async_remote_copy(src_ref=srcs[w].at[1 - c], dst_ref=outs[w], send_sem=send_sems.at[w],
                                            recv_sem=recv_sems.at[w], device_id=(x, y, 1 - c), device_id_type=MESH)
               for w in range(n)]
        for cpy in cps:
            cpy.start()
        for cpy in cps:
            cpy.wait()

    return pl.pallas_call(
        body, name="rs_swap_layers", in_specs=[ANY] * n, out_specs=[ANY] * n,
        out_shape=[jax.ShapeDtypeStruct(g.shape[1:], g.dtype) for g in grads],
        scratch_shapes=[pltpu.SemaphoreType.DMA((n,)), pltpu.SemaphoreType.DMA((n,))],
    )(*grads)


def _scatter_shards(parts):
    n = len(parts)

    def body(*refs):
        srcs, outs = refs[:n], refs[n:2 * n]
        send_sems, recv_sems = refs[2 * n:]
        x, y, c, chips = _place()

        def cp(w, j, src_shard, to):
            return pltpu.make_async_remote_copy(
                src_ref=srcs[w].at[src_shard], dst_ref=outs[w].at[j], send_sem=send_sems.at[w, j],
                recv_sem=recv_sems.at[w, j], device_id=to, device_id_type=MESH)

        sends = [cp(w, j, 2 * px + py, (px, py, c)) for w in range(n) for j, (px, py) in enumerate(chips)]
        for s in sends:
            s.start()
        for s in sends:
            s.wait()

    return pl.pallas_call(
        body, name="rs_scatter", in_specs=[ANY] * n, out_specs=[ANY] * n,
        out_shape=[jax.ShapeDtypeStruct((3,) + g.shape[1:], g.dtype) for g in parts],
        scratch_shapes=[pltpu.SemaphoreType.DMA((n, 3)), pltpu.SemaphoreType.DMA((n, 3))],
    )(*parts)


def _sum_reduced(part, slots, shard, core):
    _, r, cc = part.shape
    tr = min(TR, r)

    def body(shard_ref, core_ref, p_ref, s0_ref, s1_ref, s2_ref, o_ref):
        acc = p_ref[...].astype(F32)
        for s_ref in (s0_ref, s1_ref, s2_ref):
            acc = acc + s_ref[...].astype(F32)
        o_ref[...] = acc

    slot = lambda j: pl.BlockSpec((None, tr, cc), lambda i, sh, co: (j, i, 0))
    return pl.pallas_call(
        body, name="rs_sum_reduced",
        grid_spec=pltpu.PrefetchScalarGridSpec(
            num_scalar_prefetch=2, grid=(r // tr,),
            in_specs=[pl.BlockSpec((None, tr, cc), lambda i, sh, co: (sh[0], i, 0)), slot(0), slot(1), slot(2)],
            out_specs=pl.BlockSpec((None, tr, cc), lambda i, sh, co: (co[0], i, 0))),
        out_shape=jax.ShapeDtypeStruct((DEPTH, r, cc), F32),
        compiler_params=_cparams(("arbitrary",)),
    )(shard, core, part, slots, slots, slots)


def _join_layers(bufs):
    n = len(bufs)

    def body(*refs):
        outs = refs[n:2 * n]
        send_sems, recv_sems = refs[2 * n:]
        x, y, c, _ = _place()
        sends = [pltpu.make_async_remote_copy(src_ref=outs[w].at[c], dst_ref=outs[w].at[c], send_sem=send_sems.at[w],
                                              recv_sem=recv_sems.at[w], device_id=(x, y, 1 - c), device_id_type=MESH)
                 for w in range(n)]
        for s in sends:
            s.start()
        for w in range(n):
            blk = outs[w].at[1 - c]
            pltpu.make_async_remote_copy(src_ref=blk, dst_ref=blk, send_sem=send_sems.at[w], recv_sem=recv_sems.at[w],
                                         device_id=(x, y, 1 - c), device_id_type=MESH).wait_recv()
        for s in sends:
            s.wait_send()

    return pl.pallas_call(
        body, name="rs_join_layers", in_specs=[ANY] * n, out_specs=[ANY] * n,
        out_shape=[jax.ShapeDtypeStruct(b.shape, b.dtype) for b in bufs],
        input_output_aliases={w: w for w in range(n)},
        scratch_shapes=[pltpu.SemaphoreType.DMA((n,)), pltpu.SemaphoreType.DMA((n,))],
    )(*bufs)


def _exchange_small(vec):
    def body(src, out, send_sems, recv_sems, local_sem):
        x, y, c, _ = _place()
        me = 4 * x + 2 * y + c
        lc = pltpu.make_async_copy(src, out.at[me], local_sem)
        lc.start()
        sends = []
        for k in range(1, N_DEV):
            fx, fy, fc = (k >> 2) & 1, (k >> 1) & 1, k & 1
            to = (x ^ fx, y ^ fy, c ^ fc)
            s = pltpu.make_async_remote_copy(src_ref=src, dst_ref=out.at[me], send_sem=send_sems.at[k - 1],
                                             recv_sem=recv_sems.at[k - 1], device_id=to, device_id_type=MESH)
            s.start()
            sends.append(s)
        for k in range(1, N_DEV):
            fx, fy, fc = (k >> 2) & 1, (k >> 1) & 1, k & 1
            frm = 4 * (x ^ fx) + 2 * (y ^ fy) + (c ^ fc)
            pltpu.make_async_remote_copy(src_ref=src, dst_ref=out.at[frm], send_sem=send_sems.at[k - 1],
                                         recv_sem=recv_sems.at[k - 1], device_id=(x ^ fx, y ^ fy, c ^ fc),
                                         device_id_type=MESH).wait_recv()
        for s in sends:
            s.wait_send()
        lc.wait()

    return pl.pallas_call(
        body, name="small_exchange", in_specs=[ANY], out_specs=ANY,
        out_shape=jax.ShapeDtypeStruct((N_DEV,) + vec.shape, vec.dtype),
        scratch_shapes=[pltpu.SemaphoreType.DMA((N_DEV - 1,)), pltpu.SemaphoreType.DMA((N_DEV - 1,)),
                        pltpu.SemaphoreType.DMA],
    )(vec)


def _add_own_layer(g, other, c_arr):
    _, ns, r, cc = g.shape
    tr = min(TR, r)
    nb = r // tr

    def body(c_ref, g_ref, o_ref, out_ref):
        out_ref[...] = (g_ref[...].astype(F32) + o_ref[...].astype(F32)).astype(out_ref.dtype)

    slab = pl.BlockSpec((None, tr, cc), lambda i, cr: (i // nb, i % nb, 0))
    return pl.pallas_call(
        body, name="rs_add_own_layer",
        grid_spec=pltpu.PrefetchScalarGridSpec(
            num_scalar_prefetch=1, grid=(ns * nb,),
            in_specs=[pl.BlockSpec((None, None, tr, cc), lambda i, cr: (cr[0], i // nb, i % nb, 0)), slab],
            out_specs=slab),
        out_shape=jax.ShapeDtypeStruct(other.shape, other.dtype),
        compiler_params=_cparams(("arbitrary",)),
    )(c_arr, g, other)


def _sum_slots(name, a):
    ns, r, cc = a.shape
    tr = min(TR, r)
    ins = [_In(a, (None, tr, cc), functools.partial(lambda s, i: (s, i, 0), s)) for s in range(ns)]

    def fn(*blocks):
        acc = blocks[0].astype(F32)
        for b in blocks[1:]:
            acc = acc + b.astype(F32)
        return (acc,)

    return _fused(name, fn, r // tr, ins, [((r, cc), F32, (tr, cc), lambda i: (i, 0))])[0]


def _adamw_fn(w, g, m, v):
    m = ADAM_B1 * m + (1.0 - ADAM_B1) * g
    v = ADAM_B2 * v + (1.0 - ADAM_B2) * jnp.square(g)
    m_hat = m / (1.0 - ADAM_B1 ** ADAM_STEP)
    v_hat = v / (1.0 - ADAM_B2 ** ADAM_STEP)
    delta = -ADAM_LR * (m_hat / (jnp.sqrt(v_hat) + ADAM_EPS) + ADAM_WD * w)
    return delta, m, v


def _adamw(w, g, m, v):
    r, cc = w.shape
    tr = min(TR, r)
    ins = [_In(a, (tr, cc), lambda i: (i, 0)) for a in (w, g, m, v)]
    return _fused("adamw", _adamw_fn, r // tr, ins, [((r, cc), F32, (tr, cc), lambda i: (i, 0))] * 3)


def _pack(arrs):
    flat = jnp.concatenate([a.reshape(-1) for a in arrs])
    tile = TR * LANES
    n = -(-flat.shape[0] // tile) * tile
    return jnp.pad(flat, (0, n - flat.shape[0])).reshape(-1, LANES)


def _unpack(vec, shapes):
    flat = vec.reshape(-1)
    out, o = [], 0
    for s in shapes:
        n = math.prod(s)
        out.append(flat[o:o + n].reshape(s))
        o += n
    return out


def kernel(x, p, norm_mix, w_in, sg_ln_g, sg_ln_b, sg_w, sg_b, sc_conv, gdn_conv, gdn_a_log, gdn_dt_bias, gdn_norm, out_norm_a, out_norm_b, w_o, norm_ffn, w_ff1, w_ff2, norm_ple, w_ple_gate, w_ple_proj, norm_final, loss_target, m_norm_mix, m_w_in, m_sg_ln_g, m_sg_ln_b, m_sg_w, m_sg_b, m_sc_conv, m_gdn_conv, m_gdn_a_log, m_gdn_dt_bias, m_gdn_norm, m_out_norm_a, m_out_norm_b, m_w_o, m_norm_ffn, m_w_ff1, m_w_ff2, m_norm_ple, m_w_ple_gate, m_w_ple_proj, m_norm_final, v_norm_mix, v_w_in, v_sg_ln_g, v_sg_ln_b, v_sg_w, v_sg_b, v_sc_conv, v_gdn_conv, v_gdn_a_log, v_gdn_dt_bias, v_gdn_norm, v_out_norm_a, v_out_norm_b, v_w_o, v_norm_ffn, v_w_ff1, v_w_ff2, v_norm_ple, v_w_ple_gate, v_w_ple_proj, v_norm_final):
    given = dict(locals())
    w = {n: given[n] for n in WEIGHTS}
    m = {n: given['m_' + n] for n in WEIGHTS}
    v = {n: given['v_' + n] for n in WEIGHTS}
    shard = 2 * lax.axis_index("x") + lax.axis_index("y")
    core = lax.axis_index("c")

    shard_arr = shard.reshape(1).astype(jnp.int32)
    c_arr = core.reshape(1).astype(jnp.int32)
    gathered = _all_gather([_place_shard(w[n].reshape((DEPTH, -1, w[n].shape[-1])), shard_arr,
                                         BF16 if n in BIG else F32) for n in BIG + CONVS])
    full = dict(zip(BIG + CONVS, gathered))
    small = {n: w[n] for n in SMALL if n not in CONVS}

    loss, grad_x, grads = _local_step(x[0], p[:, 0], loss_target[0], full, small)

    big = [grads[n] for n in BIG]
    theirs = _swap_layers(big)
    parts = [_add_own_layer(g, o, c_arr) for g, o in zip(big, theirs)]
    slots = _scatter_shards(parts)
    g_big = dict(zip(BIG, _join_layers([_sum_reduced(pt, sl, shard_arr, c_arr) for pt, sl in zip(parts, slots)])))

    rep = [n for n in SMALL if n not in CONVS]
    names = rep + CONVS
    vec = _pack([grads[n] for n in names] + [loss[0, :1]])
    total = _sum_slots("small_sum", _exchange_small(vec))
    parts_small = _unpack(total, [grads[n].shape for n in names] + [(1,)])
    g_small = dict(zip(names, parts_small[:-1]))
    loss_out = parts_small[-1].reshape(())
    for n in CONVS:
        width = w[n].shape[-1]
        g_small[n] = lax.dynamic_slice_in_dim(g_small[n], shard * width, width, axis=2)

    delta, new_m, new_v, grad_w = {}, {}, {}, {}
    for n in BIG:
        shp = w[n].shape
        two_d = lambda a: a.reshape(-1, shp[-1])
        d_, m_, v_ = _adamw(two_d(w[n]), two_d(g_big[n]), two_d(m[n]), two_d(v[n]))
        delta[n], new_m[n], new_v[n], grad_w[n] = d_.reshape(shp), m_.reshape(shp), v_.reshape(shp), g_big[n]
    shapes = [w[n].shape for n in SMALL]
    d_, m_, v_ = _adamw(_pack([w[n] for n in SMALL]), _pack([g_small[n] for n in SMALL]),
                        _pack([m[n] for n in SMALL]), _pack([v[n] for n in SMALL]))
    for n, dd, mm, vv in zip(SMALL, _unpack(d_, shapes), _unpack(m_, shapes), _unpack(v_, shapes)):
        delta[n], new_m[n], new_v[n], grad_w[n] = dd, mm, vv, g_small[n]

    return (loss_out, grad_x[None], *[grad_w[n] for n in WEIGHTS], *[delta[n] for n in WEIGHTS],
            *[new_m[n] for n in WEIGHTS], *[new_v[n] for n in WEIGHTS])
```

```python
import functools
import math

import jax
import jax.numpy as jnp
from jax import lax
from jax.experimental import pallas as pl
from jax.experimental.pallas import tpu as pltpu

F32 = jnp.float32
BF16 = jnp.bfloat16
HI = lax.Precision.HIGH
MESH = pl.DeviceIdType.MESH

LANES = 128
EPS = 1e-6
SG_HEADS, SG_CHUNK = 4, 128
SC_GROUPS, SC_KERNEL = 4, 3
GDN_HEADS, GDN_CONV = 8, 4
GDN_L = 128
GDN_CPS = 8
SG_W = SG_HEADS * LANES
SC_W = SC_GROUPS * LANES
GDN_W = GDN_HEADS * LANES
IN_COLS = 2 * SG_W + 3 * SC_W + 4 * GDN_W + 2 * GDN_HEADS
PROJ_W = 7168
CB_U, CB_V = 0, 4
CB_GB, CB_GC, CB_X = 8, 12, 16
CB_Q, CB_K, CB_VV, CB_Z, CB_AB = 20, 28, 36, 44, 52
N_SHARD = 4
N_DEV = 8
DEPTH = 2

ADAM_LR, ADAM_B1, ADAM_B2, ADAM_EPS, ADAM_WD, ADAM_STEP = 0.001, 0.9, 0.999, 1e-08, 0.01, 10

VMEM_LIMIT = 56 << 20

WEIGHTS = ['norm_mix', 'w_in', 'sg_ln_g', 'sg_ln_b', 'sg_w', 'sg_b', 'sc_conv', 'gdn_conv', 'gdn_a_log',
           'gdn_dt_bias', 'gdn_norm', 'out_norm_a', 'out_norm_b', 'w_o', 'norm_ffn', 'w_ff1', 'w_ff2', 'norm_ple',
           'w_ple_gate', 'w_ple_proj', 'norm_final']
BIG = ['w_in', 'w_o', 'w_ff1', 'w_ff2', 'w_ple_gate', 'w_ple_proj']
CONVS = ['sc_conv', 'gdn_conv']
SMALL = [n for n in WEIGHTS if n not in BIG]


def _cparams(sem=None):
    return pltpu.CompilerParams(dimension_semantics=sem, vmem_limit_bytes=VMEM_LIMIT)


def _dot(a, b, dims, prec=None):
    return lax.dot_general(a, b, (dims, ((), ())), precision=prec, preferred_element_type=F32)


def _mm(a, b):
    return _dot(a, b, ((1,), (0,)), HI)


def _mm_nt(a, b):
    return _dot(a, b, ((1,), (1,)), HI)


def _mm_tn(a, b):
    return _dot(a, b, ((0,), (0,)), HI)


@jax.custom_vjp
def _bmm(a, b):
    return _dot(a.astype(BF16), b.astype(BF16), ((1,), (0,)))


def _bmm_fwd(a, b):
    return _bmm(a, b), (a, b)


def _bmm_bwd(res, g):
    a, b = res
    gb = g.astype(BF16)
    return _dot(gb, b.astype(BF16), ((1,), (1,))), _dot(a.astype(BF16), gb, ((0,), (0,)))


_bmm.defvjp(_bmm_fwd, _bmm_bwd)


@jax.custom_vjp
def _bmm_tn(a, b):
    return _dot(a.astype(BF16), b.astype(BF16), ((0,), (0,)))


def _bmm_tn_fwd(a, b):
    return _bmm_tn(a, b), (a, b)


def _bmm_tn_bwd(res, g):
    a, b = res
    gb = g.astype(BF16)
    return _dot(b.astype(BF16), gb, ((1,), (1,))), _dot(a.astype(BF16), gb, ((1,), (0,)))


_bmm_tn.defvjp(_bmm_tn_fwd, _bmm_tn_bwd)


def _sigmoid(x):
    return 1.0 / (1.0 + jnp.exp(-x))


def _silu(x):
    return x * _sigmoid(x)


def _gelu(x):
    c = math.sqrt(2.0 / math.pi)
    return 0.5 * x * (1.0 + jnp.tanh(c * (x + 0.044715 * (x * x * x))))


def _softplus(x):
    return jnp.maximum(x, 0.0) + jnp.log(1.0 + jnp.exp(-jnp.abs(x)))


def _rms(x, g):
    return x * lax.rsqrt(jnp.mean(x * x, axis=-1, keepdims=True) + EPS) * g


def _roll_rows(x, shift):
    return pltpu.roll(x, shift % x.shape[0], 0)


@functools.partial(jax.custom_vjp, nondiff_argnums=(1,))
def _shift_down(x, j):
    row = lax.broadcasted_iota(jnp.int32, x.shape, 0)
    return jnp.where(row >= j, _roll_rows(x, j), 0.0)


def _shift_down_fwd(x, j):
    return _shift_down(x, j), None


def _shift_down_bwd(j, _, dy):
    row = lax.broadcasted_iota(jnp.int32, dy.shape, 0)
    return (jnp.where(row < dy.shape[0] - j, _roll_rows(dy, -j), 0.0),)


_shift_down.defvjp(_shift_down_fwd, _shift_down_bwd)


def _causal_conv(x, taps):
    k = len(taps)
    y = taps[k - 1] * x
    for j in range(k - 1):
        y = y + taps[j] * _shift_down(x, k - 1 - j)
    return y


class _In:
    def __init__(self, arr, block, imap, shared=False, gshape=None, gmap=None):
        self.arr, self.block, self.imap, self.shared = arr, block, imap, shared
        self.gshape = arr.shape if gshape is None else gshape
        self.gmap = imap if gmap is None else gmap


def _fused(name, fn, n, ins, outs, douts=None, need=None, gdt=None, add=None):
    n_in, n_out = len(ins), len(outs)
    in_specs = [pl.BlockSpec(s.block, s.imap) for s in ins]
    out_specs = [pl.BlockSpec(bs, im) for _, _, bs, im in outs]
    if douts is None:
        def body(*refs):
            res = fn(*[r[...] for r in refs[:n_in]])
            for r, v in zip(refs[n_in:], res):
                r[...] = v.astype(r.dtype)

        return pl.pallas_call(
            body, name=name, grid=(n,), in_specs=in_specs, out_specs=out_specs,
            out_shape=[jax.ShapeDtypeStruct(s, d) for s, d, _, _ in outs],
            compiler_params=_cparams(("arbitrary",)),
        )(*[s.arr for s in ins])

    gdt = list(gdt) if gdt is not None else [F32] * n_in
    add = dict(add or {})
    gidx = [i for i in range(n_in) if need[i]]
    aidx = [i for i in gidx if i in add]

    def body(*refs):
        in_refs, d_refs = refs[:n_in], refs[n_in:n_in + n_out]
        a_refs = dict(zip(aidx, refs[n_in + n_out:n_in + n_out + len(aidx)]))
        g_refs = refs[n_in + n_out + len(aidx):]
        vals = [r[...] for r in in_refs]

        def f(*dv):
            full = list(vals)
            for i, v in zip(gidx, dv):
                full[i] = v
            return tuple(o.astype(F32) for o in fn(*full))

        _, vjp = jax.vjp(f, *[vals[i].astype(F32) for i in gidx])
        grads = vjp(tuple(r[...].astype(F32) for r in d_refs))
        for i, g_ref, g in zip(gidx, g_refs, grads):
            if i in a_refs:
                g = g + a_refs[i][...].astype(F32)
            if ins[i].shared:
                period = n if ins[i].shared is True else ins[i].shared

                @pl.when(pl.program_id(0) % period == 0)
                def _():
                    g_ref[...] = jnp.zeros_like(g_ref)
                g_ref[...] += g.astype(g_ref.dtype)
            else:
                g_ref[...] = g.astype(g_ref.dtype)

    g_specs = [pl.BlockSpec(ins[i].block, ins[i].gmap) for i in gidx]
    g_shape = [jax.ShapeDtypeStruct(ins[i].gshape, gdt[i]) for i in gidx]
    res = pl.pallas_call(
        body, name=name, grid=(n,), in_specs=in_specs + out_specs + [g_specs[gidx.index(i)] for i in aidx],
        out_specs=g_specs, out_shape=g_shape,
        compiler_params=_cparams(("arbitrary",)),
    )(*[s.arr for s in ins], *douts, *[add[i] for i in aidx])
    full = [None] * n_in
    for i, g in zip(gidx, res):
        full[i] = g
    return full


def _row_in(a, tm, cb=None):
    if cb is None:
        return _In(a, (tm, a.shape[1]), lambda i: (i, 0))
    return _In(a, (tm, LANES), lambda i: (i, cb), gshape=(a.shape[0], LANES), gmap=lambda i: (i, 0))


def _row_shared(a):
    return _In(a, a.shape, lambda i: (0, 0), shared=True)


def _row_out(t, w, dt, tm):
    return ((t, w), dt, (tm, w), lambda i: (i, 0))


def _col_in(a, base, nblk):
    t = a.shape[0]
    return _In(a, (t, LANES), lambda i: (0, base + i), gshape=(t, nblk * LANES), gmap=lambda i: (0, i))


def _col_par(a):
    return _In(a, (None,) + a.shape[1:], lambda i: (i, 0, 0))


def _col_out(t, w, dt, base=0):
    return ((t, w), dt, (t, LANES), lambda i: (0, base + i))


def _matmul(name, a, b, mode, m, n, k, tm, tn, tk, b_spec=None, epilogue=None, extras=(), outs=None, out_dtype=F32,
            n_outer=False):
    tm, tn, tk = min(tm, m), min(tn, n), min(tk, k)
    assert m % tm == 0 and n % tn == 0 and k % tk == 0, (name, m, n, k, tm, tn, tk)
    nk = k // tk
    a_spec = (pl.BlockSpec((tk, tm), lambda i, j, kk: (kk, i)) if mode == "tn"
              else pl.BlockSpec((tm, tk), lambda i, j, kk: (i, kk)))
    if b_spec is None:
        b_spec = (pl.BlockSpec((tn, tk), lambda i, j, kk: (j, kk)) if mode == "nt"
                  else pl.BlockSpec((tk, tn), lambda i, j, kk: (kk, j)))
    dims = {"nn": ((1,), (0,)), "nt": ((1,), (1,)), "tn": ((0,), (0,))}[mode]
    if outs is None:
        outs = [((m, n), out_dtype, (tm, tn), lambda i, j, kk: (i, j))]
    if epilogue is None:
        epilogue = lambda acc: (acc,)
    n_ex = len(extras)

    def body(*refs):
        a_ref, b_ref = refs[0], refs[1]
        ex_refs = refs[2:2 + n_ex]
        o_refs = refs[2 + n_ex:2 + n_ex + len(outs)]
        part = _dot(a_ref[...].astype(BF16), b_ref[...].astype(BF16), dims)

        def finish(acc):
            for r, v in zip(o_refs, epilogue(acc, *[e[...] for e in ex_refs])):
                r[...] = v.astype(r.dtype)

        if nk == 1:
            finish(part)
        else:
            acc_ref = refs[-1]
            kk = pl.program_id(2)

            @pl.when(kk == 0)
            def _():
                acc_ref[...] = part

            @pl.when(kk > 0)
            def _():
                acc_ref[...] += part

            @pl.when(kk == nk - 1)
            def _():
                finish(acc_ref[...])

    ex_specs = [pl.BlockSpec((tm, tn), lambda i, j, kk: (i, j)) for _ in extras]
    out_specs = [pl.BlockSpec(bs, im) for _, _, bs, im in outs]
    in_specs = [a_spec, b_spec] + ex_specs
    grid = (m // tm, n // tn, nk)
    if n_outer:
        swap = lambda sp: pl.BlockSpec(sp.block_shape, functools.partial(lambda f, j, i, kk: f(i, j, kk), sp.index_map))
        in_specs, out_specs, grid = [swap(sp) for sp in in_specs], [swap(sp) for sp in out_specs], (n // tn, m // tm, nk)
    res = pl.pallas_call(
        body, name=name, grid=grid,
        in_specs=in_specs,
        out_specs=out_specs,
        out_shape=[jax.ShapeDtypeStruct(s, d) for s, d, _, _ in outs],
        scratch_shapes=[pltpu.VMEM((tm, tn), F32)] if nk > 1 else [],
        compiler_params=_cparams(("parallel", "parallel", "arbitrary")),
    )(a, b, *extras)
    return res if len(res) > 1 else res[0]


def _fn_sgu(u_pre, v_pre, ln_g, ln_b, w, bb, na):
    t = u_pre.shape[0]
    u = _gelu(u_pre)
    v = _gelu(v_pre)
    mu = jnp.mean(v, axis=-1, keepdims=True)
    vc = v - mu
    vh = vc * lax.rsqrt(jnp.mean(vc * vc, axis=-1, keepdims=True) + EPS) * ln_g + ln_b
    ri = lax.broadcasted_iota(jnp.int32, w.shape, 0)
    ci = lax.broadcasted_iota(jnp.int32, w.shape, 1)
    wc = jnp.where(ri >= ci, w, 0.0)
    f = jnp.concatenate([_bmm(wc, vh[c * SG_CHUNK:(c + 1) * SG_CHUNK]) + bb for c in range(t // SG_CHUNK)], axis=0)
    return (_rms(u * f, na),)


def _fn_sconv(gb, gc, xin, w0, w1, w2, nb):
    return (_rms(gb * _causal_conv(gc * xin, (w0, w1, w2)), nb),)


def _fn_gdn_qk(pre, w0, w1, w2, w3):
    a = _silu(_causal_conv(pre, (w0, w1, w2, w3)))
    return (a * lax.rsqrt(jnp.sum(a * a, axis=-1, keepdims=True) + EPS),)


def _fn_gdn_v(pre, w0, w1, w2, w3):
    return (_silu(_causal_conv(pre, (w0, w1, w2, w3))),)


def _fn_gdn_gates(ab, a_log, dt_bias):
    lane = lax.broadcasted_iota(jnp.int32, ab.shape, 1)
    g = -jnp.exp(a_log) * _softplus(ab + dt_bias)
    return (jnp.where(lane < GDN_HEADS, g, jnp.where(lane < 2 * GDN_HEADS, _sigmoid(ab), 0.0)),)


def _solve_unit_lower_impl(mats):
    n = mats[0].shape[0]
    ri = lax.broadcasted_iota(jnp.int32, (n, n), 0)
    ci = lax.broadcasted_iota(jnp.int32, (n, n), 1)
    ts = [(ri == ci).astype(F32)] * len(mats)
    sh = 0
    while (1 << sh) < n:
        rb = jnp.right_shift(ri, sh)
        cb = jnp.right_shift(ci, sh)
        off = ((rb & 1) == 1) & (cb == rb - 1)
        us = [_mm(t, jnp.where(off, a, 0.0)) for t, a in zip(ts, mats)]
        ts = [t - _mm(u, t) for t, u in zip(ts, us)]
        sh += 1
    return tuple(ts)


def _solve_cotangents(ts, dts):
    us = [_mm_nt(dt, t) for t, dt in zip(ts, dts)]
    return tuple(-_mm_tn(t, u) for t, u in zip(ts, us))


@jax.custom_vjp
def _solve_unit_lower(mats):
    return _solve_unit_lower_impl(mats)


def _solve_unit_lower_fwd(mats):
    ts = _solve_unit_lower_impl(mats)
    return ts, ts


def _solve_unit_lower_bwd(ts, dts):
    return (_solve_cotangents(ts, dts),)


_solve_unit_lower.defvjp(_solve_unit_lower_fwd, _solve_unit_lower_bwd)


@jax.custom_vjp
def _solved_unit_lower(mats, ts):
    return ts


def _solved_unit_lower_fwd(mats, ts):
    return ts, ts


def _solved_unit_lower_bwd(ts, dts):
    return _solve_cotangents(ts, dts), tuple(jnp.zeros_like(t) for t in ts)


_solved_unit_lower.defvjp(_solved_unit_lower_fwd, _solved_unit_lower_bwd)


def _fn_gdn_wy(q, k, v, gates, pick_g, pick_b, t_saved=None):
    n, dk = GDN_L, q.shape[1]
    rows = [slice(c * n, (c + 1) * n) for c in range(q.shape[0] // n)]
    ri = lax.broadcasted_iota(jnp.int32, (n, n), 0)
    ci = lax.broadcasted_iota(jnp.int32, (n, n), 1)
    incl = ri >= ci
    eye = (ri == ci).astype(F32)
    last = lax.broadcasted_iota(jnp.int32, (n, 1), 0) == (n - 1)
    qs = [q[r] * (dk ** -0.5) for r in rows]
    ks = [k[r] for r in rows]
    gcs, betas, d_incl = [], [], []
    for r in rows:
        g = jnp.sum(gates[r] * pick_g, axis=1, keepdims=True)
        betas.append(jnp.sum(gates[r] * pick_b, axis=1, keepdims=True))
        g_row = jnp.sum(eye * g, axis=0, keepdims=True)
        gc = jnp.sum(jnp.where(incl, g_row, 0.0), axis=1, keepdims=True)
        gc_row = jnp.sum(eye * gc, axis=0, keepdims=True)
        gcs.append(gc)
        d_incl.append(jnp.where(incl, jnp.exp(jnp.where(incl, gc - gc_row, 0.0)), 0.0))
    kbs = [kk * b for kk, b in zip(ks, betas)]
    mats = tuple(_mm_nt(kb, kk) * jnp.where(ri > ci, d, 0.0) for kb, kk, d in zip(kbs, ks, d_incl))
    ts = _solve_unit_lower(mats) if t_saved is None else _solved_unit_lower(mats, tuple(t_saved[r] for r in rows))
    egs = [jnp.exp(gc) for gc in gcs]
    values = [_mm(t, v[r] * b) for t, r, b in zip(ts, rows, betas)]
    kcds = [_mm(t, kb * eg) for t, kb, eg in zip(ts, kbs, egs)]
    intras = [_mm_nt(qq, kk) * d for qq, kk, d in zip(qs, ks, d_incl)]
    g_lasts = [jnp.sum(jnp.where(last, gc, 0.0), axis=0, keepdims=True) for gc in gcs]
    qes = [qq * eg for qq, eg in zip(qs, egs)]
    kts = [kk * jnp.exp(gl - gc) for kk, gl, gc in zip(ks, g_lasts, gcs)]
    carries = [jnp.broadcast_to(jnp.exp(gl), (8, LANES)) for gl in g_lasts]
    cat = lambda parts: jnp.concatenate(parts, axis=0)
    res = (cat(values), cat(kcds), cat(qes), cat(kts), cat(intras), cat(carries))
    return res + (cat(ts),) if t_saved is None else res


def _gdn_step(s, value, kcd, qe, kt, intra, carry):
    v_new = value - _bmm(kcd, s)
    o = _bmm(qe, s) + _bmm(intra, v_new)
    return s * carry + _bmm_tn(kt, v_new), o


def _gdn_post(o, z, nrm):
    return _rms(o, nrm) * _silu(z)


def _gdn_wy_ins(q, k, v, gates, t_saved=None):
    t = q.shape[0]
    rb = min(GDN_CPS * GDN_L, t)
    hd = GDN_HEADS
    lane = jnp.arange(LANES)[None, None, :]
    pick_g = (lane == jnp.arange(hd)[:, None, None]).astype(F32)
    pick_b = (lane == jnp.arange(hd)[:, None, None] + hd).astype(F32)
    blk = lambda a: _In(a, (rb, LANES), lambda i: (i // hd, i % hd))
    par = lambda a: _In(a, (None, 1, LANES), lambda i: (i % hd, 0, 0))
    ins = [blk(q), blk(k), blk(v), _In(gates, (rb, LANES), lambda i: (i // hd, 0), shared=hd), par(pick_g),
           par(pick_b)]
    wide = lambda dt: ((t, GDN_W), dt, (rb, LANES), lambda i: (i // hd, i % hd))
    carry = ((hd, 8 * (t // GDN_L), LANES), F32, (None, 8 * (rb // GDN_L), LANES), lambda i: (i % hd, i // hd, 0))
    outs = [wide(F32), wide(BF16), wide(BF16), wide(BF16), wide(BF16), carry]
    if t_saved is None:
        outs.append(wide(F32))
    else:
        ins.append(blk(t_saved))
    return (t // rb) * hd, ins, outs


def _gdn_scan_specs(t):
    col = lambda base: pl.BlockSpec((t, LANES), lambda h: (0, base + h))
    carry = pl.BlockSpec((None, 8 * (t // GDN_L), LANES), lambda h: (h, 0, 0))
    par = pl.BlockSpec((None, 1, LANES), lambda h: (h, 0, 0))
    return col, carry, par


def _gdn_scan_fwd(wy, proj, nrm):
    t = wy[0].shape[0]
    nc = t // GDN_L

    def body(val_ref, kcd_ref, qe_ref, kt_ref, in_ref, cy_ref, z_ref, n_ref, y_ref, o_scr):
        def step(c, s):
            rows = pl.ds(pl.multiple_of(c * GDN_L, GDN_L), GDN_L)
            s, o = _gdn_step(s, val_ref[rows, :], kcd_ref[rows, :], qe_ref[rows, :], kt_ref[rows, :],
                             in_ref[rows, :], cy_ref[pl.ds(pl.multiple_of(c * 8, 8), 1), :])
            o_scr[rows, :] = o
            return s

        lax.fori_loop(0, nc, step, jnp.zeros((LANES, LANES), F32))
        y_ref[...] = _gdn_post(o_scr[...], z_ref[...], n_ref[...]).astype(y_ref.dtype)

    col, carry, par = _gdn_scan_specs(t)
    return pl.pallas_call(
        body, name="gdn_scan_fwd", grid=(GDN_HEADS,),
        in_specs=[col(0)] * 5 + [carry, col(CB_Z), par],
        out_specs=col(0), out_shape=jax.ShapeDtypeStruct((t, GDN_W), BF16),
        scratch_shapes=[pltpu.VMEM((t, LANES), F32)],
        compiler_params=_cparams(("arbitrary",)),
    )(*wy, proj, nrm)


def _gdn_scan_bwd(wy, proj, nrm, dy, dy_base):
    t = wy[0].shape[0]
    nc = t // GDN_L

    def body(val_ref, kcd_ref, qe_ref, kt_ref, in_ref, cy_ref, z_ref, n_ref, dy_ref,
             dval_ref, dkcd_ref, dqe_ref, dkt_ref, din_ref, dcy_ref, dz_ref, dn_ref, o_scr, s_scr):
        def operands(c):
            rows = pl.ds(pl.multiple_of(c * GDN_L, GDN_L), GDN_L)
            return rows, (val_ref[rows, :], kcd_ref[rows, :], qe_ref[rows, :], kt_ref[rows, :], in_ref[rows, :],
                          cy_ref[pl.ds(pl.multiple_of(c * 8, 8), 1), :])

        def step(c, s):
            rows, ops = operands(c)
            s_scr[c] = s
            s, o = _gdn_step(s, *ops)
            o_scr[rows, :] = o
            return s

        lax.fori_loop(0, nc, step, jnp.zeros((LANES, LANES), F32))
        _, vjp_post = jax.vjp(_gdn_post, o_scr[...], z_ref[...], n_ref[...])
        do, dz, dn = vjp_post(dy_ref[...].astype(F32))
        dz_ref[...] = dz.astype(dz_ref.dtype)
        dn_ref[...] = dn
        o_scr[...] = do
        dcy_ref[...] = jnp.zeros_like(dcy_ref)

        def rstep(i, ds):
            c = nc - 1 - i
            rows, ops = operands(c)
            _, vjp_c = jax.vjp(_gdn_step, s_scr[c], *[o.astype(F32) for o in ops])
            ds, dval, dkcd, dqe, dkt, din, dcy = vjp_c((ds, o_scr[rows, :]))
            for r, g in ((dval_ref, dval), (dkcd_ref, dkcd), (dqe_ref, dqe), (dkt_ref, dkt), (din_ref, din)):
                r[rows, :] = g.astype(r.dtype)
            dcy_ref[pl.ds(pl.multiple_of(c * 8, 8), 1), :] = dcy
            return ds

        lax.fori_loop(0, nc, rstep, jnp.zeros((LANES, LANES), F32))

    col, carry, par = _gdn_scan_specs(t)
    wide = jax.ShapeDtypeStruct((t, GDN_W), BF16)
    return pl.pallas_call(
        body, name="gdn_scan_bwd", grid=(GDN_HEADS,),
        in_specs=[col(0)] * 5 + [carry, col(CB_Z), par, col(dy_base)],
        out_specs=[col(0)] * 5 + [carry, col(0), par],
        out_shape=[wide] * 5 + [jax.ShapeDtypeStruct(wy[5].shape, F32), wide,
                                jax.ShapeDtypeStruct((GDN_HEADS, 1, LANES), F32)],
        scratch_shapes=[pltpu.VMEM((t, LANES), F32), pltpu.VMEM((nc, LANES, LANES), F32)],
        compiler_params=_cparams(("arbitrary",)),
    )(*wy, proj, nrm, dy)


TM = 512
TR = 256


def _rms_fwd(name, h, g):
    t, d = h.shape
    tm = min(TR, t)
    return _fused(name, lambda hb, gb: (_rms(hb, gb),), t // tm, [_row_in(h, tm), _row_shared(g)],
                  [_row_out(t, d, BF16, tm)])[0]


def _rms_bwd(name, h, g, dxn, dh_next):
    t, d = h.shape
    tm = min(TR, t)
    dh, dg = _fused(name, lambda hb, gb: (_rms(hb, gb),), t // tm, [_row_in(h, tm), _row_shared(g)],
                    [_row_out(t, d, F32, tm)], douts=[dxn], need=[True, True], add={0: dh_next})
    return dh, dg


def _mixer_ins(proj, lw):
    sgu = [_col_in(proj, CB_U, SG_HEADS), _col_in(proj, CB_V, SG_HEADS), _col_par(lw['sg_ln_g']),
           _col_par(lw['sg_ln_b']), _col_par(lw['sg_w']), _col_par(lw['sg_bb']), _col_par(lw['out_norm_a'])]
    sconv = [_col_in(proj, CB_GB, SC_GROUPS), _col_in(proj, CB_GC, SC_GROUPS), _col_in(proj, CB_X, SC_GROUPS)] + \
            [_col_par(w) for w in lw['sc_taps']] + [_col_par(lw['out_norm_b'])]
    gq = [_col_in(proj, CB_Q, GDN_HEADS)] + [_col_par(w) for w in lw['q_taps']]
    gk = [_col_in(proj, CB_K, GDN_HEADS)] + [_col_par(w) for w in lw['k_taps']]
    gv = [_col_in(proj, CB_VV, GDN_HEADS)] + [_col_par(w) for w in lw['v_taps']]
    return sgu, sconv, gq, gk, gv


def _gates_ins(proj, lw, tm):
    return [_row_in(proj, tm, CB_AB), _row_shared(lw['a_log_row']), _row_shared(lw['dt_bias_row'])]


def _layer_fwd(h, p_l, lw):
    t, d = h.shape
    dff = lw['w_ff2'].shape[0]
    xn = _rms_fwd("rms_fwd", h, lw['norm_mix'])
    proj = _matmul("proj_fwd", xn, lw['w_in'], "nn", t, PROJ_W, d, TM, 1024, d)
    sgu, sconv, gq, gk, gv = _mixer_ins(proj, lw)
    ya = _fused("sgu_fwd", _fn_sgu, SG_HEADS, sgu, [_col_out(t, SG_W, BF16)])[0]
    yb = _fused("sconv_fwd", _fn_sconv, SC_GROUPS, sconv, [_col_out(t, SC_W, BF16)])[0]
    q = _fused("gdn_q_fwd", _fn_gdn_qk, GDN_HEADS, gq, [_col_out(t, GDN_W, F32)])[0]
    k = _fused("gdn_k_fwd", _fn_gdn_qk, GDN_HEADS, gk, [_col_out(t, GDN_W, F32)])[0]
    v = _fused("gdn_v_fwd", _fn_gdn_v, GDN_HEADS, gv, [_col_out(t, GDN_W, F32)])[0]
    tm = min(TR, t)
    gates = _fused("gdn_gates_fwd", _fn_gdn_gates, t // tm, _gates_ins(proj, lw, tm),
                   [_row_out(t, LANES, F32, tm)])[0]
    n_wy, wy_ins, wy_outs = _gdn_wy_ins(q, k, v, gates)
    wy = _fused("gdn_wy_fwd", _fn_gdn_wy, n_wy, wy_ins, wy_outs)
    wy, wy_t = wy[:6], wy[6]
    yc = _gdn_scan_fwd(wy, proj, lw['gdn_norm'])
    ycat = jnp.concatenate([ya, yb, yc], axis=1)
    h2 = _matmul("wo_fwd", ycat, lw['w_o'], "nn", t, d, d, TM, 1024, d,
                 epilogue=lambda acc, hb: (hb + acc,), extras=(h,))
    hn = _rms_fwd("rms_fwd", h2, lw['norm_ffn'])
    per = d // 1024
    s, r = _matmul("ff1_fwd", hn, lw['w_ff1'], "nn", t, dff, d, TM, 1024, d,
                   b_spec=pl.BlockSpec((None, d, 1024), lambda i, j, kk: (j // per, 0, j % per)),
                   epilogue=lambda acc: (jnp.maximum(acc, 0.0), jnp.square(jnp.maximum(acc, 0.0))),
                   outs=[((t, dff), BF16, (min(TM, t), 1024), lambda i, j, kk: (i, j))] * 2)
    h3 = _matmul("ff2_fwd", r, lw['w_ff2'], "nn", t, d, dff, TM, 1024, 4096,
                 epilogue=lambda acc, hb: (hb + acc,), extras=(h2,))
    hn2 = _rms_fwd("rms_fwd", h3, lw['norm_ple'])
    pp = _matmul("ple_proj_fwd", p_l, lw['w_ple_proj'], "nn", t, d, p_l.shape[1], TM, 1024, p_l.shape[1])

    def gate_epilogue(acc, hb, ppb):
        sg = _sigmoid(acc)
        return hb + ppb * sg, sg

    h4, gate = _matmul("ple_gate_fwd", hn2, lw['w_ple_gate'], "nn", t, d, d, TM, 1024, d, epilogue=gate_epilogue,
                       extras=(h3, pp), outs=[((t, d), F32, (min(TM, t), 1024), lambda i, j, kk: (i, j))] * 2)
    saved = dict(h=h, xn=xn, proj=proj, q=q, k=k, v=v, gates=gates, ycat=ycat, h2=h2, hn=hn, s=s, r=r, h3=h3,
                 hn2=hn2, pp=pp, gate=gate, p=p_l, wy=wy, wy_t=wy_t)
    return h4, saved


def _layer_bwd(dh4, sv, lw):
    t, d = dh4.shape
    dff = lw['w_ff2'].shape[0]
    tm = min(TR, t)
    g = {}
    dacc, dpp = _fused("ple_bwd_gate", lambda dh, pp, gt: (dh * pp * gt * (1.0 - gt), dh * gt), t // tm,
                       [_row_in(dh4, tm), _row_in(sv['pp'], tm), _row_in(sv['gate'], tm)],
                       [_row_out(t, d, BF16, tm)] * 2)
    g['w_ple_gate'] = _matmul("dw_ple_gate", sv['hn2'], dacc, "tn", d, d, t, TM, 1024, t, out_dtype=BF16,
                              n_outer=True)
    g['w_ple_proj'] = _matmul("dw_ple_proj", sv['p'], dpp, "tn", sv['p'].shape[1], d, t, TM, 1024, t,
                              out_dtype=BF16, n_outer=True)
    dhn2 = _matmul("dx_ple_gate", dacc, lw['w_ple_gate'], "nt", t, d, d, TM, 1024, d)
    dh3, g['norm_ple'] = _rms_bwd("rms_bwd", sv['h3'], lw['norm_ple'], dhn2, dh4)
    da = _matmul("dx_ff2", dh3, lw['w_ff2'], "nt", t, dff, d, TM, 1024, d,
                 epilogue=lambda acc, sb: (acc * (2.0 * sb.astype(F32)),), extras=(sv['s'],),
                 outs=[((t, dff), BF16, (min(TM, t), 1024), lambda i, j, kk: (i, j))])
    g['w_ff2'] = _matmul("dw_ff2", sv['r'], dh3, "tn", dff, d, t, TM, 512, t, out_dtype=BF16, n_outer=True)
    per = d // 1024
    dhn = _matmul("dx_ff1", da, lw['w_ff1'], "nt", t, d, dff, TM, 1024, d,
                  b_spec=pl.BlockSpec((None, 1024, d), lambda i, j, kk: (kk, j, 0)))
    g['w_ff1'] = _matmul("dw_ff1", sv['hn'], da, "tn", d, dff, t, TM, 1024, t, n_outer=True,
                         outs=[((N_SHARD, d, d), BF16, (None, TM, 1024), lambda i, j, kk: (j // per, i, j % per))])
    dh2, g['norm_ffn'] = _rms_bwd("rms_bwd", sv['h2'], lw['norm_ffn'], dhn, dh3)
    dycat = _matmul("dx_o", dh2, lw['w_o'], "nt", t, d, d, TM, 1024, d)
    g['w_o'] = _matmul("dw_o", sv['ycat'], dh2, "tn", d, d, t, TM, 512, t, out_dtype=BF16, n_outer=True)
    proj = sv['proj']
    sgu, sconv, gq, gk, gv = _mixer_ins(proj, lw)
    bf2 = [BF16, BF16]
    r_ = _fused("sgu_bwd", _fn_sgu, SG_HEADS, sgu, [_col_out(t, d, F32, 0)], douts=[dycat], need=[True] * 7,
                gdt=bf2 + [F32] * 5)
    du, dv_, g['sg_ln_g'], g['sg_ln_b'], g['sg_w'], g['sg_bb'], g['out_norm_a'] = r_
    r_ = _fused("sconv_bwd", _fn_sconv, SC_GROUPS, sconv, [_col_out(t, d, F32, SG_HEADS)], douts=[dycat],
                need=[True] * 7, gdt=[BF16] * 3 + [F32] * 4)
    dgb, dgc, dxin = r_[:3]
    g['sc_taps'], g['out_norm_b'] = r_[3:6], r_[6]
    r_ = _gdn_scan_bwd(sv['wy'], proj, lw['gdn_norm'], dycat, SG_HEADS + SC_GROUPS)
    dwy, dz, g['gdn_norm'] = r_[:6], r_[6], r_[7]
    n_wy, wy_ins, wy_outs = _gdn_wy_ins(sv['q'], sv['k'], sv['v'], sv['gates'], sv['wy_t'])
    dq, dk, dvv, dgates = _fused("gdn_wy_bwd", _fn_gdn_wy, n_wy, wy_ins, wy_outs, douts=dwy,
                                 need=[True] * 4 + [False] * 3)[:4]
    one = [_col_out(t, GDN_W, F32)]
    r_ = _fused("gdn_q_bwd", _fn_gdn_qk, GDN_HEADS, gq, one, douts=[dq], need=[True] * 5, gdt=[BF16] + [F32] * 4)
    dpq, g['q_taps'] = r_[0], r_[1:]
    r_ = _fused("gdn_k_bwd", _fn_gdn_qk, GDN_HEADS, gk, one, douts=[dk], need=[True] * 5, gdt=[BF16] + [F32] * 4)
    dpk, g['k_taps'] = r_[0], r_[1:]
    r_ = _fused("gdn_v_bwd", _fn_gdn_v, GDN_HEADS, gv, one, douts=[dvv], need=[True] * 5, gdt=[BF16] + [F32] * 4)
    dpv, g['v_taps'] = r_[0], r_[1:]
    dab, g['a_log_row'], g['dt_bias_row'] = _fused(
        "gdn_gates_bwd", _fn_gdn_gates, t // tm, _gates_ins(proj, lw, tm), [_row_out(t, LANES, F32, tm)],
        douts=[dgates], need=[True] * 3, gdt=[BF16, F32, F32])
    pad = jnp.zeros((t, PROJ_W - (CB_AB + 1) * LANES), BF16)
    dproj = jnp.concatenate([du, dv_, dgb, dgc, dxin, dpq, dpk, dpv, dz, dab, pad], axis=1)
    dxn = _matmul("dx_in", dproj, lw['w_in'], "nt", t, d, PROJ_W, TM, 1024, PROJ_W // 2)
    g['w_in'] = _matmul("dw_in", sv['xn'], dproj, "tn", d, PROJ_W, t, TM, 1024, t, out_dtype=BF16, n_outer=True)
    dh, g['norm_mix'] = _rms_bwd("rms_bwd", sv['h'], lw['norm_mix'], dxn, dh2)
    return dh, g


def _loss_grad(h, g, tgt):
    t, d = h.shape
    tm = min(TR, t)

    def body(h_ref, g_ref, t_ref, loss_ref, dh_ref, dg_ref):
        y, vjp = jax.vjp(_rms, h_ref[...], g_ref[...])
        e = y - t_ref[...]
        dh, dg = vjp(e * (1.0 / d))

        @pl.when(pl.program_id(0) == 0)
        def _():
            loss_ref[...] = jnp.zeros_like(loss_ref)
            dg_ref[...] = jnp.zeros_like(dg_ref)

        loss_ref[...] += jnp.sum(jnp.sum(e * e, axis=1, keepdims=True), axis=0, keepdims=True) * (0.5 / d)
        dh_ref[...] = dh
        dg_ref[...] += dg

    row = pl.BlockSpec((tm, d), lambda i: (i, 0))
    return pl.pallas_call(
        body, name="loss_grad", grid=(t // tm,),
        in_specs=[row, pl.BlockSpec((1, d), lambda i: (0, 0)), row],
        out_specs=[pl.BlockSpec((1, LANES), lambda i: (0, 0)), row, pl.BlockSpec((1, d), lambda i: (0, 0))],
        out_shape=[jax.ShapeDtypeStruct((1, LANES), F32), jax.ShapeDtypeStruct((t, d), F32),
                   jax.ShapeDtypeStruct((1, d), F32)],
        compiler_params=_cparams(("arbitrary",)),
    )(h, g, tgt)


def _layer_weights(l, full, convs, small):
    d = full['w_o'].shape[-1]
    lw = {}
    w_in = jnp.transpose(full['w_in'], (1, 0, 2)).reshape(d, IN_COLS)
    lw['w_in'] = jnp.pad(w_in, ((0, 0), (0, PROJ_W - IN_COLS)))
    lw['w_o'] = full['w_o'].reshape(d, d)
    lw['w_ff1'] = full['w_ff1']
    lw['w_ff2'] = full['w_ff2'].reshape(-1, d)
    lw['w_ple_gate'] = full['w_ple_gate'].reshape(d, d)
    wpp = full['w_ple_proj']
    lw['w_ple_proj'] = jnp.transpose(wpp, (1, 0, 2)).reshape(wpp.shape[1], d)
    for n in ('norm_mix', 'norm_ffn', 'norm_ple'):
        lw[n] = small[n][l].reshape(1, d)
    lw['sg_ln_g'] = small['sg_ln_g'][l].reshape(SG_HEADS, 1, LANES)
    lw['sg_ln_b'] = small['sg_ln_b'][l].reshape(SG_HEADS, 1, LANES)
    lw['sg_w'] = small['sg_w'][l]
    lw['sg_bb'] = jnp.broadcast_to(small['sg_b'][l][:, :, None], (SG_HEADS, SG_CHUNK, LANES))
    lw['out_norm_a'] = small['out_norm_a'][l].reshape(SG_HEADS, 1, LANES)
    lw['out_norm_b'] = small['out_norm_b'][l].reshape(SC_GROUPS, 1, LANES)
    lw['gdn_norm'] = jnp.broadcast_to(small['gdn_norm'][l].reshape(1, 1, LANES), (GDN_HEADS, 1, LANES))
    lw['a_log_row'] = jnp.pad(small['gdn_a_log'][l].reshape(1, GDN_HEADS), ((0, 0), (0, LANES - GDN_HEADS)))
    lw['dt_bias_row'] = jnp.pad(small['gdn_dt_bias'][l].reshape(1, GDN_HEADS), ((0, 0), (0, LANES - GDN_HEADS)))
    sc = convs['sc_conv'][l]
    lw['sc_taps'] = [sc[:, j:j + 1, :] for j in range(SC_KERNEL)]
    gc = jnp.transpose(convs['gdn_conv'][l], (1, 0, 2)).reshape(GDN_CONV, 3 * GDN_W)
    for i, nm in enumerate(('q_taps', 'k_taps', 'v_taps')):
        part = gc[:, i * GDN_W:(i + 1) * GDN_W].reshape(GDN_CONV, GDN_HEADS, 1, LANES)
        lw[nm] = [part[j] for j in range(GDN_CONV)]
    return lw


def _weight_grads(g):
    d = g['w_o'].shape[0]
    out = {}
    w_in = g['w_in'][:, :IN_COLS].reshape(d, N_SHARD, IN_COLS // N_SHARD)
    out['w_in'] = jnp.transpose(w_in, (1, 0, 2))
    out['w_o'] = g['w_o'].reshape(N_SHARD, d // N_SHARD, d)
    out['w_ff1'] = g['w_ff1']
    out['w_ff2'] = g['w_ff2'].reshape(N_SHARD, -1, d)
    out['w_ple_gate'] = g['w_ple_gate'].reshape(N_SHARD, d // N_SHARD, d)
    wpp = g['w_ple_proj']
    out['w_ple_proj'] = jnp.transpose(wpp.reshape(wpp.shape[0], N_SHARD, d // N_SHARD), (1, 0, 2))
    for n in ('norm_mix', 'norm_ffn', 'norm_ple'):
        out[n] = g[n].reshape(d)
    out['sg_ln_g'] = g['sg_ln_g'].reshape(SG_W)
    out['sg_ln_b'] = g['sg_ln_b'].reshape(SG_W)
    out['sg_w'] = g['sg_w']
    out['sg_b'] = jnp.sum(g['sg_bb'], axis=2)
    out['out_norm_a'] = g['out_norm_a'].reshape(SG_W)
    out['out_norm_b'] = g['out_norm_b'].reshape(SC_W)
    out['gdn_norm'] = jnp.sum(g['gdn_norm'], axis=(0, 1))
    out['gdn_a_log'] = g['a_log_row'][0, :GDN_HEADS]
    out['gdn_dt_bias'] = g['dt_bias_row'][0, :GDN_HEADS]
    out['sc_conv'] = jnp.concatenate([w.reshape(1, SC_W) for w in g['sc_taps']], axis=0)
    taps = [jnp.concatenate([g[nm][j].reshape(1, GDN_W) for nm in ('q_taps', 'k_taps', 'v_taps')], axis=1)
            for j in range(GDN_CONV)]
    out['gdn_conv'] = jnp.concatenate(taps, axis=0)
    return out


ANY = pl.BlockSpec(memory_space=pl.ANY)


def _place():
    x, y, c = lax.axis_index("x"), lax.axis_index("y"), lax.axis_index("c")
    chips = [(1 - x, y), (x, 1 - y), (1 - x, 1 - y)]
    return x, y, c, chips


def _place_shard(wsh, shard, dtype, layer=None):
    dp, r, cc = wsh.shape
    tr = min(TR, r)
    nb = r // tr

    def body(idx_ref, w_ref, o_ref):
        o_ref[...] = w_ref[...].astype(o_ref.dtype)

    if layer is None:
        grid, shape = (dp * nb,), (dp, N_SHARD, r, cc)
        in_spec = pl.BlockSpec((None, tr, cc), lambda i, ix: (i // nb, i % nb, 0))
        out_spec = pl.BlockSpec((None, None, tr, cc), lambda i, ix: (i // nb, ix[0], i % nb, 0))
    else:
        grid, shape = (nb,), (N_SHARD, r, cc)
        in_spec = pl.BlockSpec((None, tr, cc), lambda i, ix: (layer, i, 0))
        out_spec = pl.BlockSpec((None, tr, cc), lambda i, ix: (ix[0], i, 0))
    return pl.pallas_call(
        body, name="ag_place_shard",
        grid_spec=pltpu.PrefetchScalarGridSpec(num_scalar_prefetch=1, grid=grid, in_specs=[in_spec],
                                               out_specs=out_spec),
        out_shape=jax.ShapeDtypeStruct(shape, dtype),
        compiler_params=_cparams(("arbitrary",)),
    )(shard, wsh)


def _rows(ref, slab, half):
    rh = ref.shape[1] // 2
    return ref.at[slab, pl.ds(pl.multiple_of(half * rh, rh), rh), :]


def _gather_direct(bufs):
    n = len(bufs)

    def body(*refs):
        outs = refs[n:2 * n]
        send_sems, recv_sems = refs[2 * n:]
        x, y, c, chips = _place()
        me = 2 * x + y

        def cp(w, l, j, shard, to):
            blk = outs[w].at[l, shard]
            return pltpu.make_async_remote_copy(src_ref=blk, dst_ref=blk, send_sem=send_sems.at[w, l, j],
                                                recv_sem=recv_sems.at[w, l, j], device_id=to, device_id_type=MESH)

        sends = [cp(w, l, j, me, (*chip, c)) for w in range(n) for l in range(DEPTH) for j, chip in enumerate(chips)]
        for s in sends:
            s.start()
        for w in range(n):
            for l in range(DEPTH):
                for j, (px, py) in enumerate(chips):
                    cp(w, l, j, 2 * px + py, (px, py, c)).wait_recv()
        for s in sends:
            s.wait_send()

    return pl.pallas_call(
        body, name="ag_small", in_specs=[ANY] * n, out_specs=[ANY] * n,
        out_shape=[jax.ShapeDtypeStruct(b.shape, b.dtype) for b in bufs],
        input_output_aliases={w: w for w in range(n)},
        scratch_shapes=[pltpu.SemaphoreType.DMA((n, DEPTH, 3)), pltpu.SemaphoreType.DMA((n, DEPTH, 3))],
    )(*bufs)


def _gather_halves(bufs):
    n = len(bufs)

    def body(*refs):
        outs = refs[n:2 * n]
        send_sems, recv_sems = refs[2 * n:]
        x, y, c, chips = _place()
        me = 2 * x + y
        sibling = (x, y, 1 - c)

        def cp(w, k, shard, half, to):
            blk = _rows(outs[w], shard, half)
            return pltpu.make_async_remote_copy(src_ref=blk, dst_ref=blk, send_sem=send_sems.at[w, k],
                                                recv_sem=recv_sems.at[w, k], device_id=to, device_id_type=MESH)

        sends = []
        for w in range(n):
            for j, chip in enumerate(chips):
                s = cp(w, j, me, c, (*chip, c))
                s.start()
                sends.append(s)
        for w in range(n):
            for j, (px, py) in enumerate(chips):
                cp(w, j, 2 * px + py, c, sibling).wait_recv()
                s = cp(w, 3 + j, 2 * px + py, c, sibling)
                s.start()
                sends.append(s)
        for w in range(n):
            for j, (px, py) in enumerate(chips):
                cp(w, 3 + j, 2 * px + py, 1 - c, sibling).wait_recv()
        for s in sends:
            s.wait_send()

    return pl.pallas_call(
        body, name="ag_gather_halves",
        in_specs=[ANY] * n, out_specs=[ANY] * n,
        out_shape=[jax.ShapeDtypeStruct(b.shape, b.dtype) for b in bufs],
        input_output_aliases={w: w for w in range(n)},
        scratch_shapes=[pltpu.SemaphoreType.DMA((n, 6)), pltpu.SemaphoreType.DMA((n, 6))],
    )(*bufs)


HBM = pl.BlockSpec(memory_space=pltpu.HBM)
SEM = pl.BlockSpec(memory_space=pltpu.SEMAPHORE)
_SPLIT = pltpu.CompilerParams(has_side_effects=pltpu.SideEffectType.DATAFLOW_SIDE_EFFECTING)


def _in_hbm(arrs):
    return [pltpu.with_memory_space_constraint(a, pltpu.HBM) for a in arrs]


def _gather_start(bufs, after):
    n = len(bufs)
    k = 3 * n

    def body(*refs):
        b_refs, sems, token = refs[:n], refs[n + 1:n + 1 + 2 * k], refs[-1]
        x, y, c, chips = _place()
        me = 2 * x + y
        for w in range(n):
            for j, chip in enumerate(chips):
                blk = _rows(b_refs[w], me, c)
                pltpu.make_async_remote_copy(src_ref=blk, dst_ref=blk, send_sem=sems[3 * w + j],
                                             recv_sem=sems[k + 3 * w + j], device_id=(*chip, c),
                                             device_id_type=MESH).start()
        token[...] = jnp.zeros_like(token)

    res = pl.pallas_call(
        body, name="ag_start",
        out_shape=(*[pltpu.SemaphoreType.DMA(())] * (2 * k), *[pltpu.HBM(b.shape, b.dtype) for b in bufs],
                   jax.ShapeDtypeStruct((8, LANES), F32)),
        in_specs=[HBM] * n + [ANY], out_specs=(*[SEM] * (2 * k), *[HBM] * n, pl.BlockSpec(memory_space=pltpu.VMEM)),
        input_output_aliases={w: 2 * k + w for w in range(n)}, compiler_params=_SPLIT,
    )(*_in_hbm(bufs), after)
    return list(res[:2 * k]), list(res[2 * k:2 * k + n]), res[2 * k + n]


def _gather_wait(sems, bufs, after):
    n = len(bufs)
    k = 3 * n

    def body(*refs):
        b_refs, sems = refs[:n], refs[n:n + 2 * k]
        x, y, c, chips = _place()
        me = 2 * x + y
        for w in range(n):
            for j, (px, py) in enumerate(chips):
                mine, theirs = _rows(b_refs[w], me, c), _rows(b_refs[w], 2 * px + py, c)
                cp = pltpu.make_async_remote_copy(src_ref=mine, dst_ref=theirs, send_sem=sems[3 * w + j],
                                                  recv_sem=sems[k + 3 * w + j], device_id=(px, py, c),
                                                  device_id_type=MESH)
                cp.wait_send()
                cp.wait_recv()

    res = pl.pallas_call(
        body, name="ag_wait", out_shape=tuple(pltpu.HBM(b.shape, b.dtype) for b in bufs),
        in_specs=[HBM] * n + [SEM] * (2 * k) + [ANY], out_specs=(HBM,) * n,
        input_output_aliases={w: w for w in range(n)}, compiler_params=_SPLIT,
    )(*bufs, *sems, after)
    return list(res)


def _gather_forward(bufs):
    n = len(bufs)

    def body(*refs):
        outs = refs[n:2 * n]
        send_sems, recv_sems = refs[2 * n:]
        x, y, c, chips = _place()

        def cp(w, j, shard, half):
            blk = _rows(outs[w], shard, half)
            return pltpu.make_async_remote_copy(src_ref=blk, dst_ref=blk, send_sem=send_sems.at[w, j],
                                                recv_sem=recv_sems.at[w, j], device_id=(x, y, 1 - c),
                                                device_id_type=MESH)

        sends = [cp(w, j, 2 * px + py, c) for w in range(n) for j, (px, py) in enumerate(chips)]
        for s in sends:
            s.start()
        for w in range(n):
            for j, (px, py) in enumerate(chips):
                cp(w, j, 2 * px + py, 1 - c).wait_recv()
        for s in sends:
            s.wait_send()

    return pl.pallas_call(
        body, name="ag_forward", in_specs=[ANY] * n, out_specs=[ANY] * n,
        out_shape=[jax.ShapeDtypeStruct(b.shape, b.dtype) for b in bufs],
        input_output_aliases={w: w for w in range(n)},
        scratch_shapes=[pltpu.SemaphoreType.DMA((n, 3)), pltpu.SemaphoreType.DMA((n, 3))],
    )(*bufs)


def _scatter_start(parts, after):
    n = len(parts)
    k = 3 * n
    lands = [lax.empty((3,) + g.shape[1:], g.dtype) for g in parts]

    def body(*refs):
        srcs, dsts, sems, token = refs[:n], refs[n:2 * n], refs[2 * n + 1:2 * n + 1 + 2 * k], refs[-1]
        x, y, c, chips = _place()
        for w in range(n):
            for j, (px, py) in enumerate(chips):
                pltpu.make_async_remote_copy(src_ref=srcs[w].at[2 * px + py], dst_ref=dsts[w].at[j],
                                             send_sem=sems[3 * w + j], recv_sem=sems[k + 3 * w + j],
                                             device_id=(px, py, c), device_id_type=MESH).start()
        token[...] = jnp.zeros_like(token)

    res = pl.pallas_call(
        body, name="rs_scatter_start",
        out_shape=(*[pltpu.SemaphoreType.DMA(())] * (2 * k), *[pltpu.HBM(a.shape, a.dtype) for a in parts + lands],
                   jax.ShapeDtypeStruct((8, LANES), F32)),
        in_specs=[HBM] * (2 * n) + [ANY],
        out_specs=(*[SEM] * (2 * k), *[HBM] * (2 * n), pl.BlockSpec(memory_space=pltpu.VMEM)),
        input_output_aliases={w: 2 * k + w for w in range(2 * n)}, compiler_params=_SPLIT,
    )(*_in_hbm(parts + lands), after)
    return list(res[:2 * k]), list(res[2 * k:2 * k + n]), list(res[2 * k + n:2 * k + 2 * n]), res[2 * k + 2 * n]


def _scatter_wait(sems, parts, lands, after):
    n = len(parts)
    k = 3 * n

    def body(*refs):
        srcs, dsts, sems = refs[:n], refs[n:2 * n], refs[2 * n:2 * n + 2 * k]
        x, y, c, chips = _place()
        for w in range(n):
            for j, (px, py) in enumerate(chips):
                cp = pltpu.make_async_remote_copy(src_ref=srcs[w].at[2 * px + py], dst_ref=dsts[w].at[j],
                                                  send_sem=sems[3 * w + j], recv_sem=sems[k + 3 * w + j],
                                                  device_id=(px, py, c), device_id_type=MESH)
                cp.wait_send()
                cp.wait_recv()

    res = pl.pallas_call(
        body, name="rs_scatter_wait", out_shape=tuple(pltpu.HBM(a.shape, a.dtype) for a in parts + lands),
        in_specs=[HBM] * (2 * n) + [SEM] * (2 * k) + [ANY], out_specs=(HBM,) * (2 * n),
        input_output_aliases={w: w for w in range(2 * n)}, compiler_params=_SPLIT,
    )(*parts, *lands, *sems, after)
    return list(res[:n]), list(res[n:])


def _swap_halves(grads):
    n = len(grads)

    def body(*refs):
        srcs, outs = refs[:n], refs[n:2 * n]
        send_sems, recv_sems = refs[2 * n:]
        x, y, c, _ = _place()
        cps = []
        for w in range(n):
            rh = srcs[w].shape[1] // 2
            theirs = srcs[w].at[:, pl.ds(pl.multiple_of((1 - c) * rh, rh), rh), :]
            cps.append(pltpu.make_async_remote_copy(src_ref=theirs, dst_ref=outs[w], send_sem=send_sems.at[w],
                                                    recv_sem=recv_sems.at[w], device_id=(x, y, 1 - c),
                                                    device_id_type=MESH))
        for cpy in cps:
            cpy.start()
        for cpy in cps:
            cpy.wait()

    return pl.pallas_call(
        body, name="rs_swap_halves", in_specs=[ANY] * n, out_specs=[ANY] * n,
        out_shape=[jax.ShapeDtypeStruct((g.shape[0], g.shape[1] // 2, g.shape[2]), g.dtype) for g in grads],
        scratch_shapes=[pltpu.SemaphoreType.DMA((n,)), pltpu.SemaphoreType.DMA((n,))],
    )(*grads)


def _scatter_shards(parts):
    n = len(parts)

    def body(*refs):
        srcs, outs = refs[:n], refs[n:2 * n]
        send_sems, recv_sems = refs[2 * n:]
        x, y, c, chips = _place()

        def cp(w, j, src_shard, to):
            return pltpu.make_async_remote_copy(
                src_ref=srcs[w].at[src_shard], dst_ref=outs[w].at[j], send_sem=send_sems.at[w, j],
                recv_sem=recv_sems.at[w, j], device_id=to, device_id_type=MESH)

        sends = [cp(w, j, 2 * px + py, (px, py, c)) for w in range(n) for j, (px, py) in enumerate(chips)]
        for s in sends:
            s.start()
        for s in sends:
            s.wait()

    return pl.pallas_call(
        body, name="rs_scatter", in_specs=[ANY] * n, out_specs=[ANY] * n,
        out_shape=[jax.ShapeDtypeStruct((3,) + g.shape[1:], g.dtype) for g in parts],
        scratch_shapes=[pltpu.SemaphoreType.DMA((n, 3)), pltpu.SemaphoreType.DMA((n, 3))],
    )(*parts)


def _sum_reduced(part, slots, shard, core, layer, prev):
    _, rh, cc = part.shape
    tr = min(TR, rh)
    nbh = rh // tr

    def body(shard_ref, core_ref, p_ref, s0_ref, s1_ref, s2_ref, *rest):
        acc = p_ref[...].astype(F32)
        for s_ref in (s0_ref, s1_ref, s2_ref):
            acc = acc + s_ref[...].astype(F32)
        rest[-1][...] = acc

    slot = lambda j: pl.BlockSpec((None, tr, cc), lambda i, sh, co: (j, i, 0))
    ins = [shard, core, part, slots, slots, slots] + ([] if prev is None else [prev])
    return pl.pallas_call(
        body, name="rs_sum_reduced",
        grid_spec=pltpu.PrefetchScalarGridSpec(
            num_scalar_prefetch=2, grid=(nbh,),
            in_specs=[pl.BlockSpec((None, tr, cc), lambda i, sh, co: (sh[0], i, 0)), slot(0), slot(1), slot(2)]
            + ([] if prev is None else [ANY]),
            out_specs=pl.BlockSpec((None, tr, cc), lambda i, sh, co: (layer, co[0] * nbh + i, 0))),
        out_shape=jax.ShapeDtypeStruct((DEPTH, 2 * rh, cc), F32),
        input_output_aliases={} if prev is None else {6: 0},
        compiler_params=_cparams(("arbitrary",)),
    )(*ins)


def _join_halves(bufs, layer):
    n = len(bufs)

    def body(*refs):
        outs = refs[n:2 * n]
        send_sems, recv_sems = refs[2 * n:]
        x, y, c, _ = _place()

        def cp(w, half):
            blk = _rows(outs[w], layer, half)
            return pltpu.make_async_remote_copy(src_ref=blk, dst_ref=blk, send_sem=send_sems.at[w],
                                                recv_sem=recv_sems.at[w], device_id=(x, y, 1 - c), device_id_type=MESH)

        sends = [cp(w, c) for w in range(n)]
        for s in sends:
            s.start()
        for w in range(n):
            cp(w, 1 - c).wait_recv()
        for s in sends:
            s.wait_send()

    return pl.pallas_call(
        body, name="rs_join_halves", in_specs=[ANY] * n, out_specs=[ANY] * n,
        out_shape=[jax.ShapeDtypeStruct(b.shape, b.dtype) for b in bufs],
        input_output_aliases={w: w for w in range(n)},
        scratch_shapes=[pltpu.SemaphoreType.DMA((n,)), pltpu.SemaphoreType.DMA((n,))],
    )(*bufs)


def _exchange_small(vec):
    def body(src, out, send_sems, recv_sems, local_sem):
        x, y, c, _ = _place()
        me = 4 * x + 2 * y + c
        lc = pltpu.make_async_copy(src, out.at[me], local_sem)
        lc.start()
        sends = []
        for k in range(1, N_DEV):
            fx, fy, fc = (k >> 2) & 1, (k >> 1) & 1, k & 1
            to = (x ^ fx, y ^ fy, c ^ fc)
            s = pltpu.make_async_remote_copy(src_ref=src, dst_ref=out.at[me], send_sem=send_sems.at[k - 1],
                                             recv_sem=recv_sems.at[k - 1], device_id=to, device_id_type=MESH)
            s.start()
            sends.append(s)
        for k in range(1, N_DEV):
            fx, fy, fc = (k >> 2) & 1, (k >> 1) & 1, k & 1
            frm = 4 * (x ^ fx) + 2 * (y ^ fy) + (c ^ fc)
            pltpu.make_async_remote_copy(src_ref=src, dst_ref=out.at[frm], send_sem=send_sems.at[k - 1],
                                         recv_sem=recv_sems.at[k - 1], device_id=(x ^ fx, y ^ fy, c ^ fc),
                                         device_id_type=MESH).wait_recv()
        for s in sends:
            s.wait_send()
        lc.wait()

    return pl.pallas_call(
        body, name="small_exchange", in_specs=[ANY], out_specs=ANY,
        out_shape=jax.ShapeDtypeStruct((N_DEV,) + vec.shape, vec.dtype),
        scratch_shapes=[pltpu.SemaphoreType.DMA((N_DEV - 1,)), pltpu.SemaphoreType.DMA((N_DEV - 1,)),
                        pltpu.SemaphoreType.DMA],
    )(vec)


def _add_own_half(g, other, c_arr):
    ns, rh, cc = other.shape
    tr = min(TR, rh)
    nb = rh // tr

    def body(c_ref, g_ref, o_ref, out_ref):
        out_ref[...] = (g_ref[...].astype(F32) + o_ref[...].astype(F32)).astype(out_ref.dtype)

    slab = pl.BlockSpec((None, tr, cc), lambda i, cr: (i // nb, i % nb, 0))
    return pl.pallas_call(
        body, name="rs_add_own_half",
        grid_spec=pltpu.PrefetchScalarGridSpec(
            num_scalar_prefetch=1, grid=(ns * nb,),
            in_specs=[pl.BlockSpec((None, tr, cc), lambda i, cr: (i // nb, cr[0] * nb + i % nb, 0)), slab],
            out_specs=slab),
        out_shape=jax.ShapeDtypeStruct(other.shape, other.dtype),
        compiler_params=_cparams(("arbitrary",)),
    )(c_arr, g, other)


def _sum_slots(name, a):
    ns, r, cc = a.shape
    tr = min(TR, r)
    ins = [_In(a, (None, tr, cc), functools.partial(lambda s, i: (s, i, 0), s)) for s in range(ns)]

    def fn(*blocks):
        acc = blocks[0].astype(F32)
        for b in blocks[1:]:
            acc = acc + b.astype(F32)
        return (acc,)

    return _fused(name, fn, r // tr, ins, [((r, cc), F32, (tr, cc), lambda i: (i, 0))])[0]


def _adamw_fn(w, g, m, v):
    m = ADAM_B1 * m + (1.0 - ADAM_B1) * g
    v = ADAM_B2 * v + (1.0 - ADAM_B2) * jnp.square(g)
    m_hat = m / (1.0 - ADAM_B1 ** ADAM_STEP)
    v_hat = v / (1.0 - ADAM_B2 ** ADAM_STEP)
    delta = -ADAM_LR * (m_hat / (jnp.sqrt(v_hat) + ADAM_EPS) + ADAM_WD * w)
    return delta, m, v


def _adamw(w, g, m, v):
    r, cc = w.shape
    tr = min(TR, r)
    ins = [_In(a, (tr, cc), lambda i: (i, 0)) for a in (w, g, m, v)]
    return _fused("adamw", _adamw_fn, r // tr, ins, [((r, cc), F32, (tr, cc), lambda i: (i, 0))] * 3)


def _pack(arrs):
    flat = jnp.concatenate([a.reshape(-1) for a in arrs])
    tile = TR * LANES
    n = -(-flat.shape[0] // tile) * tile
    return jnp.pad(flat, (0, n - flat.shape[0])).reshape(-1, LANES)


def _unpack(vec, shapes):
    flat = vec.reshape(-1)
    out, o = [], 0
    for s in shapes:
        n = math.prod(s)
        out.append(flat[o:o + n].reshape(s))
        o += n
    return out


def kernel(x, p, norm_mix, w_in, sg_ln_g, sg_ln_b, sg_w, sg_b, sc_conv, gdn_conv, gdn_a_log, gdn_dt_bias, gdn_norm, out_norm_a, out_norm_b, w_o, norm_ffn, w_ff1, w_ff2, norm_ple, w_ple_gate, w_ple_proj, norm_final, loss_target, m_norm_mix, m_w_in, m_sg_ln_g, m_sg_ln_b, m_sg_w, m_sg_b, m_sc_conv, m_gdn_conv, m_gdn_a_log, m_gdn_dt_bias, m_gdn_norm, m_out_norm_a, m_out_norm_b, m_w_o, m_norm_ffn, m_w_ff1, m_w_ff2, m_norm_ple, m_w_ple_gate, m_w_ple_proj, m_norm_final, v_norm_mix, v_w_in, v_sg_ln_g, v_sg_ln_b, v_sg_w, v_sg_b, v_sc_conv, v_gdn_conv, v_gdn_a_log, v_gdn_dt_bias, v_gdn_norm, v_out_norm_a, v_out_norm_b, v_w_o, v_norm_ffn, v_w_ff1, v_w_ff2, v_norm_ple, v_w_ple_gate, v_w_ple_proj, v_norm_final):
    given = dict(locals())
    w = {n: given[n] for n in WEIGHTS}
    m = {n: given['m_' + n] for n in WEIGHTS}
    v = {n: given['v_' + n] for n in WEIGHTS}
    shard = 2 * lax.axis_index("x") + lax.axis_index("y")
    core = lax.axis_index("c")

    shard_arr = shard.reshape(1).astype(jnp.int32)
    c_arr = core.reshape(1).astype(jnp.int32)
    convs = dict(zip(CONVS, _gather_direct([_place_shard(w[n], shard_arr, F32) for n in CONVS])))
    small = {n: w[n] for n in SMALL if n not in CONVS}
    full = [None] * DEPTH
    full[0] = _gather_halves([_place_shard(w[n], shard_arr, BF16, 0) for n in BIG])
    flight = _gather_start([_place_shard(w[n], shard_arr, BF16, DEPTH - 1) for n in BIG], full[0][0])

    h = x[0]
    saved, lws = [], []
    for l in range(DEPTH):
        if l == DEPTH - 1:
            full[l] = _gather_forward(_gather_wait(flight[0], flight[1], h))
        lw = _layer_weights(l, dict(zip(BIG, full[l])), convs, small)
        if l == 0:
            lw['norm_mix'] = lw['norm_mix'] + flight[2][0, 0]
        h, sv = _layer_fwd(h, p[l, 0], lw)
        saved.append(sv)
        lws.append(lw)
    loss, dh, dnf = _loss_grad(h, small['norm_final'].reshape(1, -1), loss_target[0])

    per_layer = [None] * DEPTH
    parts = [None] * DEPTH
    for l in reversed(range(DEPTH)):
        dh, g = _layer_bwd(dh, saved[l], lws[l])
        per_layer[l] = _weight_grads(g)
        big = [per_layer[l][n] for n in BIG]
        parts[l] = [_add_own_half(g_, o, c_arr) for g_, o in zip(big, _swap_halves(big))]
        if l == DEPTH - 1:
            flight = _scatter_start(parts[l], dh)
            lws[0]['norm_ple'] = lws[0]['norm_ple'] + flight[3][0, 0]
    reduced = [None] * len(BIG)
    for l in range(DEPTH):
        if l == DEPTH - 1:
            parts[l], slots = _scatter_wait(flight[0], flight[1], flight[2], dh)
        else:
            slots = _scatter_shards(parts[l])
        reduced = _join_halves([_sum_reduced(pt, sl, shard_arr, c_arr, l, prev)
                                for pt, sl, prev in zip(parts[l], slots, reduced)], l)
    g_big = dict(zip(BIG, reduced))
    grad_x = dh
    grads = {n: jnp.stack([per_layer[l][n] for l in range(DEPTH)], axis=0) for n in SMALL if n != 'norm_final'}
    grads['norm_final'] = dnf.reshape(-1)

    rep = [n for n in SMALL if n not in CONVS]
    names = rep + CONVS
    vec = _pack([grads[n] for n in names] + [loss[0, :1]])
    total = _sum_slots("small_sum", _exchange_small(vec))
    parts_small = _unpack(total, [grads[n].shape for n in names] + [(1,)])
    g_small = dict(zip(names, parts_small[:-1]))
    loss_out = parts_small[-1].reshape(())
    for n in CONVS:
        width = w[n].shape[-1]
        g_small[n] = lax.dynamic_slice_in_dim(g_small[n], shard * width, width, axis=2)

    delta, new_m, new_v, grad_w = {}, {}, {}, {}
    for n in BIG:
        shp = w[n].shape
        two_d = lambda a: a.reshape(-1, shp[-1])
        d_, m_, v_ = _adamw(two_d(w[n]), two_d(g_big[n]), two_d(m[n]), two_d(v[n]))
        delta[n], new_m[n], new_v[n], grad_w[n] = d_.reshape(shp), m_.reshape(shp), v_.reshape(shp), g_big[n]
    shapes = [w[n].shape for n in SMALL]
    d_, m_, v_ = _adamw(_pack([w[n] for n in SMALL]), _pack([g_small[n] for n in SMALL]),
                        _pack([m[n] for n in SMALL]), _pack([v[n] for n in SMALL]))
    for n, dd, mm, vv in zip(SMALL, _unpack(d_, shapes), _unpack(m_, shapes), _unpack(v_, shapes)):
        delta[n], new_m[n], new_v[n], grad_w[n] = dd, mm, vv, g_small[n]

    return (loss_out, grad_x[None], *[grad_w[n] for n in WEIGHTS], *[delta[n] for n in WEIGHTS],
            *[new_m[n] for n in WEIGHTS], *[new_v[n] for n in WEIGHTS])
```

```python
import functools
import math

import jax
import jax.numpy as jnp
from jax import lax
from jax.experimental import pallas as pl
from jax.experimental.pallas import tpu as pltpu

F32 = jnp.float32
BF16 = jnp.bfloat16
HI = lax.Precision.HIGH
MESH = pl.DeviceIdType.MESH

LANES = 128
EPS = 1e-6
SG_HEADS, SG_CHUNK = 4, 128
SC_GROUPS, SC_KERNEL = 4, 3
GDN_HEADS, GDN_CONV = 8, 4
GDN_L = 128
GDN_CPS = 8
SG_W = SG_HEADS * LANES
SC_W = SC_GROUPS * LANES
GDN_W = GDN_HEADS * LANES
IN_COLS = 2 * SG_W + 3 * SC_W + 4 * GDN_W + 2 * GDN_HEADS
PROJ_W = 7168
CB_U, CB_V = 0, 4
CB_GB, CB_GC, CB_X = 8, 12, 16
CB_Q, CB_K, CB_VV, CB_Z, CB_AB = 20, 28, 36, 44, 52
N_SHARD = 4
N_DEV = 8
DEPTH = 2

ADAM_LR, ADAM_B1, ADAM_B2, ADAM_EPS, ADAM_WD, ADAM_STEP = 0.001, 0.9, 0.999, 1e-08, 0.01, 10

VMEM_LIMIT = 56 << 20

WEIGHTS = ['norm_mix', 'w_in', 'sg_ln_g', 'sg_ln_b', 'sg_w', 'sg_b', 'sc_conv', 'gdn_conv', 'gdn_a_log',
           'gdn_dt_bias', 'gdn_norm', 'out_norm_a', 'out_norm_b', 'w_o', 'norm_ffn', 'w_ff1', 'w_ff2', 'norm_ple',
           'w_ple_gate', 'w_ple_proj', 'norm_final']
BIG = ['w_in', 'w_o', 'w_ff1', 'w_ff2', 'w_ple_gate', 'w_ple_proj']
CONVS = ['sc_conv', 'gdn_conv']
SMALL = [n for n in WEIGHTS if n not in BIG]


def _cparams(sem=None):
    return pltpu.CompilerParams(dimension_semantics=sem, vmem_limit_bytes=VMEM_LIMIT)


def _dot(a, b, dims, prec=None):
    return lax.dot_general(a, b, (dims, ((), ())), precision=prec, preferred_element_type=F32)


def _mm(a, b):
    return _dot(a, b, ((1,), (0,)), HI)


def _mm_nt(a, b):
    return _dot(a, b, ((1,), (1,)), HI)


def _mm_tn(a, b):
    return _dot(a, b, ((0,), (0,)), HI)


@jax.custom_vjp
def _bmm(a, b):
    return _dot(a.astype(BF16), b.astype(BF16), ((1,), (0,)))


def _bmm_fwd(a, b):
    return _bmm(a, b), (a, b)


def _bmm_bwd(res, g):
    a, b = res
    gb = g.astype(BF16)
    return _dot(gb, b.astype(BF16), ((1,), (1,))), _dot(a.astype(BF16), gb, ((0,), (0,)))


_bmm.defvjp(_bmm_fwd, _bmm_bwd)


@jax.custom_vjp
def _bmm_tn(a, b):
    return _dot(a.astype(BF16), b.astype(BF16), ((0,), (0,)))


def _bmm_tn_fwd(a, b):
    return _bmm_tn(a, b), (a, b)


def _bmm_tn_bwd(res, g):
    a, b = res
    gb = g.astype(BF16)
    return _dot(b.astype(BF16), gb, ((1,), (1,))), _dot(a.astype(BF16), gb, ((1,), (0,)))


_bmm_tn.defvjp(_bmm_tn_fwd, _bmm_tn_bwd)


def _sigmoid(x):
    return 1.0 / (1.0 + jnp.exp(-x))


def _silu(x):
    return x * _sigmoid(x)


def _gelu(x):
    c = math.sqrt(2.0 / math.pi)
    return 0.5 * x * (1.0 + jnp.tanh(c * (x + 0.044715 * (x * x * x))))


def _softplus(x):
    return jnp.maximum(x, 0.0) + jnp.log(1.0 + jnp.exp(-jnp.abs(x)))


def _rms(x, g):
    return x * lax.rsqrt(jnp.mean(x * x, axis=-1, keepdims=True) + EPS) * g


def _roll_rows(x, shift):
    return pltpu.roll(x, shift % x.shape[0], 0)


@functools.partial(jax.custom_vjp, nondiff_argnums=(1,))
def _shift_down(x, j):
    row = lax.broadcasted_iota(jnp.int32, x.shape, 0)
    return jnp.where(row >= j, _roll_rows(x, j), 0.0)


def _shift_down_fwd(x, j):
    return _shift_down(x, j), None


def _shift_down_bwd(j, _, dy):
    row = lax.broadcasted_iota(jnp.int32, dy.shape, 0)
    return (jnp.where(row < dy.shape[0] - j, _roll_rows(dy, -j), 0.0),)


_shift_down.defvjp(_shift_down_fwd, _shift_down_bwd)


def _causal_conv(x, taps):
    k = len(taps)
    y = taps[k - 1] * x
    for j in range(k - 1):
        y = y + taps[j] * _shift_down(x, k - 1 - j)
    return y


class _In:
    def __init__(self, arr, block, imap, shared=False, gshape=None, gmap=None):
        self.arr, self.block, self.imap, self.shared = arr, block, imap, shared
        self.gshape = arr.shape if gshape is None else gshape
        self.gmap = imap if gmap is None else gmap


def _fused(name, fn, n, ins, outs, douts=None, need=None, gdt=None, add=None):
    n_in, n_out = len(ins), len(outs)
    in_specs = [pl.BlockSpec(s.block, s.imap) for s in ins]
    out_specs = [pl.BlockSpec(bs, im) for _, _, bs, im in outs]
    if douts is None:
        def body(*refs):
            res = fn(*[r[...] for r in refs[:n_in]])
            for r, v in zip(refs[n_in:], res):
                r[...] = v.astype(r.dtype)

        return pl.pallas_call(
            body, name=name, grid=(n,), in_specs=in_specs, out_specs=out_specs,
            out_shape=[jax.ShapeDtypeStruct(s, d) for s, d, _, _ in outs],
            compiler_params=_cparams(("arbitrary",)),
        )(*[s.arr for s in ins])

    gdt = list(gdt) if gdt is not None else [F32] * n_in
    add = dict(add or {})
    gidx = [i for i in range(n_in) if need[i]]
    aidx = [i for i in gidx if i in add]

    def body(*refs):
        in_refs, d_refs = refs[:n_in], refs[n_in:n_in + n_out]
        a_refs = dict(zip(aidx, refs[n_in + n_out:n_in + n_out + len(aidx)]))
        g_refs = refs[n_in + n_out + len(aidx):]
        vals = [r[...] for r in in_refs]

        def f(*dv):
            full = list(vals)
            for i, v in zip(gidx, dv):
                full[i] = v
            return tuple(o.astype(F32) for o in fn(*full))

        _, vjp = jax.vjp(f, *[vals[i].astype(F32) for i in gidx])
        grads = vjp(tuple(r[...].astype(F32) for r in d_refs))
        for i, g_ref, g in zip(gidx, g_refs, grads):
            if i in a_refs:
                g = g + a_refs[i][...].astype(F32)
            if ins[i].shared:
                period = n if ins[i].shared is True else ins[i].shared

                @pl.when(pl.program_id(0) % period == 0)
                def _():
                    g_ref[...] = jnp.zeros_like(g_ref)
                g_ref[...] += g.astype(g_ref.dtype)
            else:
                g_ref[...] = g.astype(g_ref.dtype)

    g_specs = [pl.BlockSpec(ins[i].block, ins[i].gmap) for i in gidx]
    g_shape = [jax.ShapeDtypeStruct(ins[i].gshape, gdt[i]) for i in gidx]
    res = pl.pallas_call(
        body, name=name, grid=(n,), in_specs=in_specs + out_specs + [g_specs[gidx.index(i)] for i in aidx],
        out_specs=g_specs, out_shape=g_shape,
        compiler_params=_cparams(("arbitrary",)),
    )(*[s.arr for s in ins], *douts, *[add[i] for i in aidx])
    full = [None] * n_in
    for i, g in zip(gidx, res):
        full[i] = g
    return full


def _row_in(a, tm, cb=None):
    if cb is None:
        return _In(a, (tm, a.shape[1]), lambda i: (i, 0))
    return _In(a, (tm, LANES), lambda i: (i, cb), gshape=(a.shape[0], LANES), gmap=lambda i: (i, 0))


def _row_shared(a):
    return _In(a, a.shape, lambda i: (0, 0), shared=True)


def _row_out(t, w, dt, tm):
    return ((t, w), dt, (tm, w), lambda i: (i, 0))


def _col_in(a, base, nblk):
    t = a.shape[0]
    return _In(a, (t, LANES), lambda i: (0, base + i), gshape=(t, nblk * LANES), gmap=lambda i: (0, i))


def _col_par(a):
    return _In(a, (None,) + a.shape[1:], lambda i: (i, 0, 0))


def _col_out(t, w, dt, base=0):
    return ((t, w), dt, (t, LANES), lambda i: (0, base + i))


def _matmul(name, a, b, mode, m, n, k, tm, tn, tk, b_spec=None, epilogue=None, extras=(), outs=None, out_dtype=F32,
            n_outer=False):
    tm, tn, tk = min(tm, m), min(tn, n), min(tk, k)
    assert m % tm == 0 and n % tn == 0 and k % tk == 0, (name, m, n, k, tm, tn, tk)
    nk = k // tk
    a_spec = (pl.BlockSpec((tk, tm), lambda i, j, kk: (kk, i)) if mode == "tn"
              else pl.BlockSpec((tm, tk), lambda i, j, kk: (i, kk)))
    if b_spec is None:
        b_spec = (pl.BlockSpec((tn, tk), lambda i, j, kk: (j, kk)) if mode == "nt"
                  else pl.BlockSpec((tk, tn), lambda i, j, kk: (kk, j)))
    dims = {"nn": ((1,), (0,)), "nt": ((1,), (1,)), "tn": ((0,), (0,))}[mode]
    if outs is None:
        outs = [((m, n), out_dtype, (tm, tn), lambda i, j, kk: (i, j))]
    if epilogue is None:
        epilogue = lambda acc: (acc,)
    n_ex = len(extras)

    def body(*refs):
        a_ref, b_ref = refs[0], refs[1]
        ex_refs = refs[2:2 + n_ex]
        o_refs = refs[2 + n_ex:2 + n_ex + len(outs)]
        part = _dot(a_ref[...].astype(BF16), b_ref[...].astype(BF16), dims)

        def finish(acc):
            for r, v in zip(o_refs, epilogue(acc, *[e[...] for e in ex_refs])):
                r[...] = v.astype(r.dtype)

        if nk == 1:
            finish(part)
        else:
            acc_ref = refs[-1]
            kk = pl.program_id(2)

            @pl.when(kk == 0)
            def _():
                acc_ref[...] = part

            @pl.when(kk > 0)
            def _():
                acc_ref[...] += part

            @pl.when(kk == nk - 1)
            def _():
                finish(acc_ref[...])

    ex_specs = [pl.BlockSpec((tm, tn), lambda i, j, kk: (i, j)) for _ in extras]
    out_specs = [pl.BlockSpec(bs, im) for _, _, bs, im in outs]
    in_specs = [a_spec, b_spec] + ex_specs
    grid = (m // tm, n // tn, nk)
    if n_outer:
        swap = lambda sp: pl.BlockSpec(sp.block_shape, functools.partial(lambda f, j, i, kk: f(i, j, kk), sp.index_map))
        in_specs, out_specs, grid = [swap(sp) for sp in in_specs], [swap(sp) for sp in out_specs], (n // tn, m // tm, nk)
    res = pl.pallas_call(
        body, name=name, grid=grid,
        in_specs=in_specs,
        out_specs=out_specs,
        out_shape=[jax.ShapeDtypeStruct(s, d) for s, d, _, _ in outs],
        scratch_shapes=[pltpu.VMEM((tm, tn), F32)] if nk > 1 else [],
        compiler_params=_cparams(("parallel", "parallel", "arbitrary")),
    )(a, b, *extras)
    return res if len(res) > 1 else res[0]


def _fn_sgu(u_pre, v_pre, ln_g, ln_b, w, bb, na):
    t = u_pre.shape[0]
    u = _gelu(u_pre)
    v = _gelu(v_pre)
    mu = jnp.mean(v, axis=-1, keepdims=True)
    vc = v - mu
    vh = vc * lax.rsqrt(jnp.mean(vc * vc, axis=-1, keepdims=True) + EPS) * ln_g + ln_b
    ri = lax.broadcasted_iota(jnp.int32, w.shape, 0)
    ci = lax.broadcasted_iota(jnp.int32, w.shape, 1)
    wc = jnp.where(ri >= ci, w, 0.0)
    f = jnp.concatenate([_bmm(wc, vh[c * SG_CHUNK:(c + 1) * SG_CHUNK]) + bb for c in range(t // SG_CHUNK)], axis=0)
    return (_rms(u * f, na),)


def _fn_sconv(gb, gc, xin, w0, w1, w2, nb):
    return (_rms(gb * _causal_conv(gc * xin, (w0, w1, w2)), nb),)


def _fn_gdn_qk(pre, w0, w1, w2, w3):
    a = _silu(_causal_conv(pre, (w0, w1, w2, w3)))
    return (a * lax.rsqrt(jnp.sum(a * a, axis=-1, keepdims=True) + EPS),)


def _fn_gdn_v(pre, w0, w1, w2, w3):
    return (_silu(_causal_conv(pre, (w0, w1, w2, w3))),)


def _fn_gdn_gates(ab, a_log, dt_bias):
    lane = lax.broadcasted_iota(jnp.int32, ab.shape, 1)
    g = -jnp.exp(a_log) * _softplus(ab + dt_bias)
    return (jnp.where(lane < GDN_HEADS, g, jnp.where(lane < 2 * GDN_HEADS, _sigmoid(ab), 0.0)),)


def _solve_unit_lower_impl(mats):
    n = mats[0].shape[0]
    ri = lax.broadcasted_iota(jnp.int32, (n, n), 0)
    ci = lax.broadcasted_iota(jnp.int32, (n, n), 1)
    ts = [(ri == ci).astype(F32)] * len(mats)
    sh = 0
    while (1 << sh) < n:
        rb = jnp.right_shift(ri, sh)
        cb = jnp.right_shift(ci, sh)
        off = ((rb & 1) == 1) & (cb == rb - 1)
        us = [_mm(t, jnp.where(off, a, 0.0)) for t, a in zip(ts, mats)]
        ts = [t - _mm(u, t) for t, u in zip(ts, us)]
        sh += 1
    return tuple(ts)


def _solve_cotangents(ts, dts):
    us = [_mm_nt(dt, t) for t, dt in zip(ts, dts)]
    return tuple(-_mm_tn(t, u) for t, u in zip(ts, us))


@jax.custom_vjp
def _solve_unit_lower(mats):
    return _solve_unit_lower_impl(mats)


def _solve_unit_lower_fwd(mats):
    ts = _solve_unit_lower_impl(mats)
    return ts, ts


def _solve_unit_lower_bwd(ts, dts):
    return (_solve_cotangents(ts, dts),)


_solve_unit_lower.defvjp(_solve_unit_lower_fwd, _solve_unit_lower_bwd)


@jax.custom_vjp
def _solved_unit_lower(mats, ts):
    return ts


def _solved_unit_lower_fwd(mats, ts):
    return ts, ts


def _solved_unit_lower_bwd(ts, dts):
    return _solve_cotangents(ts, dts), tuple(jnp.zeros_like(t) for t in ts)


_solved_unit_lower.defvjp(_solved_unit_lower_fwd, _solved_unit_lower_bwd)


def _fn_gdn_wy(q, k, v, gates, pick_g, pick_b, t_saved=None):
    n, dk = GDN_L, q.shape[1]
    rows = [slice(c * n, (c + 1) * n) for c in range(q.shape[0] // n)]
    ri = lax.broadcasted_iota(jnp.int32, (n, n), 0)
    ci = lax.broadcasted_iota(jnp.int32, (n, n), 1)
    incl = ri >= ci
    eye = (ri == ci).astype(F32)
    last = lax.broadcasted_iota(jnp.int32, (n, 1), 0) == (n - 1)
    qs = [q[r] * (dk ** -0.5) for r in rows]
    ks = [k[r] for r in rows]
    gcs, betas, d_incl = [], [], []
    for r in rows:
        g = jnp.sum(gates[r] * pick_g, axis=1, keepdims=True)
        betas.append(jnp.sum(gates[r] * pick_b, axis=1, keepdims=True))
        g_row = jnp.sum(eye * g, axis=0, keepdims=True)
        gc = jnp.sum(jnp.where(incl, g_row, 0.0), axis=1, keepdims=True)
        gc_row = jnp.sum(eye * gc, axis=0, keepdims=True)
        gcs.append(gc)
        d_incl.append(jnp.where(incl, jnp.exp(jnp.where(incl, gc - gc_row, 0.0)), 0.0))
    kbs = [kk * b for kk, b in zip(ks, betas)]
    mats = tuple(_mm_nt(kb, kk) * jnp.where(ri > ci, d, 0.0) for kb, kk, d in zip(kbs, ks, d_incl))
    ts = _solve_unit_lower(mats) if t_saved is None else _solved_unit_lower(mats, tuple(t_saved[r] for r in rows))
    egs = [jnp.exp(gc) for gc in gcs]
    values = [_mm(t, v[r] * b) for t, r, b in zip(ts, rows, betas)]
    kcds = [_mm(t, kb * eg) for t, kb, eg in zip(ts, kbs, egs)]
    intras = [_mm_nt(qq, kk) * d for qq, kk, d in zip(qs, ks, d_incl)]
    g_lasts = [jnp.sum(jnp.where(last, gc, 0.0), axis=0, keepdims=True) for gc in gcs]
    qes = [qq * eg for qq, eg in zip(qs, egs)]
    kts = [kk * jnp.exp(gl - gc) for kk, gl, gc in zip(ks, g_lasts, gcs)]
    carries = [jnp.broadcast_to(jnp.exp(gl), (8, LANES)) for gl in g_lasts]
    cat = lambda parts: jnp.concatenate(parts, axis=0)
    res = (cat(values), cat(kcds), cat(qes), cat(kts), cat(intras), cat(carries))
    return res + (cat(ts),) if t_saved is None else res


def _gdn_step(s, value, kcd, qe, kt, intra, carry):
    v_new = value - _bmm(kcd, s)
    o = _bmm(qe, s) + _bmm(intra, v_new)
    return s * carry + _bmm_tn(kt, v_new), o


def _gdn_post(o, z, nrm):
    return _rms(o, nrm) * _silu(z)


def _gdn_wy_ins(q, k, v, gates, t_saved=None):
    t = q.shape[0]
    rb = min(GDN_CPS * GDN_L, t)
    hd = GDN_HEADS
    lane = jnp.arange(LANES)[None, None, :]
    pick_g = (lane == jnp.arange(hd)[:, None, None]).astype(F32)
    pick_b = (lane == jnp.arange(hd)[:, None, None] + hd).astype(F32)
    blk = lambda a: _In(a, (rb, LANES), lambda i: (i // hd, i % hd))
    par = lambda a: _In(a, (None, 1, LANES), lambda i: (i % hd, 0, 0))
    ins = [blk(q), blk(k), blk(v), _In(gates, (rb, LANES), lambda i: (i // hd, 0), shared=hd), par(pick_g),
           par(pick_b)]
    wide = lambda dt: ((t, GDN_W), dt, (rb, LANES), lambda i: (i // hd, i % hd))
    carry = ((hd, 8 * (t // GDN_L), LANES), F32, (None, 8 * (rb // GDN_L), LANES), lambda i: (i % hd, i // hd, 0))
    outs = [wide(F32), wide(BF16), wide(BF16), wide(BF16), wide(BF16), carry]
    if t_saved is None:
        outs.append(wide(F32))
    else:
        ins.append(blk(t_saved))
    return (t // rb) * hd, ins, outs


def _gdn_scan_specs(t):
    col = lambda base: pl.BlockSpec((t, LANES), lambda h: (0, base + h))
    carry = pl.BlockSpec((None, 8 * (t // GDN_L), LANES), lambda h: (h, 0, 0))
    par = pl.BlockSpec((None, 1, LANES), lambda h: (h, 0, 0))
    return col, carry, par


def _gdn_scan_fwd(wy, proj, nrm):
    t = wy[0].shape[0]
    nc = t // GDN_L

    def body(val_ref, kcd_ref, qe_ref, kt_ref, in_ref, cy_ref, z_ref, n_ref, y_ref, o_scr):
        def step(c, s):
            rows = pl.ds(pl.multiple_of(c * GDN_L, GDN_L), GDN_L)
            s, o = _gdn_step(s, val_ref[rows, :], kcd_ref[rows, :], qe_ref[rows, :], kt_ref[rows, :],
                             in_ref[rows, :], cy_ref[pl.ds(pl.multiple_of(c * 8, 8), 1), :])
            o_scr[rows, :] = o
            return s

        lax.fori_loop(0, nc, step, jnp.zeros((LANES, LANES), F32))
        y_ref[...] = _gdn_post(o_scr[...], z_ref[...], n_ref[...]).astype(y_ref.dtype)

    col, carry, par = _gdn_scan_specs(t)
    return pl.pallas_call(
        body, name="gdn_scan_fwd", grid=(GDN_HEADS,),
        in_specs=[col(0)] * 5 + [carry, col(CB_Z), par],
        out_specs=col(0), out_shape=jax.ShapeDtypeStruct((t, GDN_W), BF16),
        scratch_shapes=[pltpu.VMEM((t, LANES), F32)],
        compiler_params=_cparams(("arbitrary",)),
    )(*wy, proj, nrm)


def _gdn_scan_bwd(wy, proj, nrm, dy, dy_base):
    t = wy[0].shape[0]
    nc = t // GDN_L

    def body(val_ref, kcd_ref, qe_ref, kt_ref, in_ref, cy_ref, z_ref, n_ref, dy_ref,
             dval_ref, dkcd_ref, dqe_ref, dkt_ref, din_ref, dcy_ref, dz_ref, dn_ref, o_scr, s_scr):
        def operands(c):
            rows = pl.ds(pl.multiple_of(c * GDN_L, GDN_L), GDN_L)
            return rows, (val_ref[rows, :], kcd_ref[rows, :], qe_ref[rows, :], kt_ref[rows, :], in_ref[rows, :],
                          cy_ref[pl.ds(pl.multiple_of(c * 8, 8), 1), :])

        def step(c, s):
            rows, ops = operands(c)
            s_scr[c] = s
            s, o = _gdn_step(s, *ops)
            o_scr[rows, :] = o
            return s

        lax.fori_loop(0, nc, step, jnp.zeros((LANES, LANES), F32))
        _, vjp_post = jax.vjp(_gdn_post, o_scr[...], z_ref[...], n_ref[...])
        do, dz, dn = vjp_post(dy_ref[...].astype(F32))
        dz_ref[...] = dz.astype(dz_ref.dtype)
        dn_ref[...] = dn
        o_scr[...] = do
        dcy_ref[...] = jnp.zeros_like(dcy_ref)

        def rstep(i, ds):
            c = nc - 1 - i
            rows, ops = operands(c)
            _, vjp_c = jax.vjp(_gdn_step, s_scr[c], *[o.astype(F32) for o in ops])
            ds, dval, dkcd, dqe, dkt, din, dcy = vjp_c((ds, o_scr[rows, :]))
            for r, g in ((dval_ref, dval), (dkcd_ref, dkcd), (dqe_ref, dqe), (dkt_ref, dkt), (din_ref, din)):
                r[rows, :] = g.astype(r.dtype)
            dcy_ref[pl.ds(pl.multiple_of(c * 8, 8), 1), :] = dcy
            return ds

        lax.fori_loop(0, nc, rstep, jnp.zeros((LANES, LANES), F32))

    col, carry, par = _gdn_scan_specs(t)
    wide = jax.ShapeDtypeStruct((t, GDN_W), BF16)
    return pl.pallas_call(
        body, name="gdn_scan_bwd", grid=(GDN_HEADS,),
        in_specs=[col(0)] * 5 + [carry, col(CB_Z), par, col(dy_base)],
        out_specs=[col(0)] * 5 + [carry, col(0), par],
        out_shape=[wide] * 5 + [jax.ShapeDtypeStruct(wy[5].shape, F32), wide,
                                jax.ShapeDtypeStruct((GDN_HEADS, 1, LANES), F32)],
        scratch_shapes=[pltpu.VMEM((t, LANES), F32), pltpu.VMEM((nc, LANES, LANES), F32)],
        compiler_params=_cparams(("arbitrary",)),
    )(*wy, proj, nrm, dy)


TM = 512
TR = 256


def _rms_fwd(name, h, g):
    t, d = h.shape
    tm = min(TR, t)
    return _fused(name, lambda hb, gb: (_rms(hb, gb),), t // tm, [_row_in(h, tm), _row_shared(g)],
                  [_row_out(t, d, BF16, tm)])[0]


def _rms_bwd(name, h, g, dxn, dh_next):
    t, d = h.shape
    tm = min(TR, t)
    dh, dg = _fused(name, lambda hb, gb: (_rms(hb, gb),), t // tm, [_row_in(h, tm), _row_shared(g)],
                    [_row_out(t, d, F32, tm)], douts=[dxn], need=[True, True], add={0: dh_next})
    return dh, dg


def _mixer_ins(proj, lw):
    sgu = [_col_in(proj, CB_U, SG_HEADS), _col_in(proj, CB_V, SG_HEADS), _col_par(lw['sg_ln_g']),
           _col_par(lw['sg_ln_b']), _col_par(lw['sg_w']), _col_par(lw['sg_bb']), _col_par(lw['out_norm_a'])]
    sconv = [_col_in(proj, CB_GB, SC_GROUPS), _col_in(proj, CB_GC, SC_GROUPS), _col_in(proj, CB_X, SC_GROUPS)] + \
            [_col_par(w) for w in lw['sc_taps']] + [_col_par(lw['out_norm_b'])]
    gq = [_col_in(proj, CB_Q, GDN_HEADS)] + [_col_par(w) for w in lw['q_taps']]
    gk = [_col_in(proj, CB_K, GDN_HEADS)] + [_col_par(w) for w in lw['k_taps']]
    gv = [_col_in(proj, CB_VV, GDN_HEADS)] + [_col_par(w) for w in lw['v_taps']]
    return sgu, sconv, gq, gk, gv


def _gates_ins(proj, lw, tm):
    return [_row_in(proj, tm, CB_AB), _row_shared(lw['a_log_row']), _row_shared(lw['dt_bias_row'])]


def _layer_fwd(h, p_l, lw, late=None):
    t, d = h.shape
    xn = _rms_fwd("rms_fwd", h, lw['norm_mix'])
    proj = _matmul("proj_fwd", xn, lw['w_in'], "nn", t, PROJ_W, d, TM, 1024, d)
    sgu, sconv, gq, gk, gv = _mixer_ins(proj, lw)
    ya = _fused("sgu_fwd", _fn_sgu, SG_HEADS, sgu, [_col_out(t, SG_W, BF16)])[0]
    yb = _fused("sconv_fwd", _fn_sconv, SC_GROUPS, sconv, [_col_out(t, SC_W, BF16)])[0]
    q = _fused("gdn_q_fwd", _fn_gdn_qk, GDN_HEADS, gq, [_col_out(t, GDN_W, F32)])[0]
    k = _fused("gdn_k_fwd", _fn_gdn_qk, GDN_HEADS, gk, [_col_out(t, GDN_W, F32)])[0]
    v = _fused("gdn_v_fwd", _fn_gdn_v, GDN_HEADS, gv, [_col_out(t, GDN_W, F32)])[0]
    tm = min(TR, t)
    gates = _fused("gdn_gates_fwd", _fn_gdn_gates, t // tm, _gates_ins(proj, lw, tm),
                   [_row_out(t, LANES, F32, tm)])[0]
    n_wy, wy_ins, wy_outs = _gdn_wy_ins(q, k, v, gates)
    wy = _fused("gdn_wy_fwd", _fn_gdn_wy, n_wy, wy_ins, wy_outs)
    wy, wy_t = wy[:6], wy[6]
    yc = _gdn_scan_fwd(wy, proj, lw['gdn_norm'])
    ycat = jnp.concatenate([ya, yb, yc], axis=1)
    if late is not None:
        late(ycat)
    dff = lw['w_ff2'].shape[0]
    h2 = _matmul("wo_fwd", ycat, lw['w_o'], "nn", t, d, d, TM, 1024, d,
                 epilogue=lambda acc, hb: (hb + acc,), extras=(h,))
    hn = _rms_fwd("rms_fwd", h2, lw['norm_ffn'])
    per = d // 1024
    s, r = _matmul("ff1_fwd", hn, lw['w_ff1'], "nn", t, dff, d, TM, 1024, d,
                   b_spec=pl.BlockSpec((None, d, 1024), lambda i, j, kk: (j // per, 0, j % per)),
                   epilogue=lambda acc: (jnp.maximum(acc, 0.0), jnp.square(jnp.maximum(acc, 0.0))),
                   outs=[((t, dff), BF16, (min(TM, t), 1024), lambda i, j, kk: (i, j))] * 2)
    h3 = _matmul("ff2_fwd", r, lw['w_ff2'], "nn", t, d, dff, TM, 1024, 4096,
                 epilogue=lambda acc, hb: (hb + acc,), extras=(h2,))
    hn2 = _rms_fwd("rms_fwd", h3, lw['norm_ple'])
    pp = _matmul("ple_proj_fwd", p_l, lw['w_ple_proj'], "nn", t, d, p_l.shape[1], TM, 1024, p_l.shape[1])

    def gate_epilogue(acc, hb, ppb):
        sg = _sigmoid(acc)
        return hb + ppb * sg, sg

    h4, gate = _matmul("ple_gate_fwd", hn2, lw['w_ple_gate'], "nn", t, d, d, TM, 1024, d, epilogue=gate_epilogue,
                       extras=(h3, pp), outs=[((t, d), F32, (min(TM, t), 1024), lambda i, j, kk: (i, j))] * 2)
    saved = dict(h=h, xn=xn, proj=proj, q=q, k=k, v=v, gates=gates, ycat=ycat, h2=h2, hn=hn, s=s, r=r, h3=h3,
                 hn2=hn2, pp=pp, gate=gate, p=p_l, wy=wy, wy_t=wy_t)
    return h4, saved


def _layer_bwd(dh4, sv, lw, early=None):
    t, d = dh4.shape
    dff = lw['w_ff2'].shape[0]
    tm = min(TR, t)
    g = {}
    dacc, dpp = _fused("ple_bwd_gate", lambda dh, pp, gt: (dh * pp * gt * (1.0 - gt), dh * gt), t // tm,
                       [_row_in(dh4, tm), _row_in(sv['pp'], tm), _row_in(sv['gate'], tm)],
                       [_row_out(t, d, BF16, tm)] * 2)
    g['w_ple_gate'] = _matmul("dw_ple_gate", sv['hn2'], dacc, "tn", d, d, t, TM, 1024, t, out_dtype=BF16,
                              n_outer=True)
    g['w_ple_proj'] = _matmul("dw_ple_proj", sv['p'], dpp, "tn", sv['p'].shape[1], d, t, TM, 1024, t,
                              out_dtype=BF16, n_outer=True)
    dhn2 = _matmul("dx_ple_gate", dacc, lw['w_ple_gate'], "nt", t, d, d, TM, 1024, d)
    dh3, g['norm_ple'] = _rms_bwd("rms_bwd", sv['h3'], lw['norm_ple'], dhn2, dh4)
    da = _matmul("dx_ff2", dh3, lw['w_ff2'], "nt", t, dff, d, TM, 1024, d,
                 epilogue=lambda acc, sb: (acc * (2.0 * sb.astype(F32)),), extras=(sv['s'],),
                 outs=[((t, dff), BF16, (min(TM, t), 1024), lambda i, j, kk: (i, j))])
    g['w_ff2'] = _matmul("dw_ff2", sv['r'], dh3, "tn", dff, d, t, TM, 512, t, out_dtype=BF16, n_outer=True)
    per = d // 1024
    dhn = _matmul("dx_ff1", da, lw['w_ff1'], "nt", t, d, dff, TM, 1024, d,
                  b_spec=pl.BlockSpec((None, 1024, d), lambda i, j, kk: (kk, j, 0)))
    g['w_ff1'] = _matmul("dw_ff1", sv['hn'], da, "tn", d, dff, t, TM, 1024, t, n_outer=True,
                         outs=[((N_SHARD, d, d), BF16, (None, TM, 1024), lambda i, j, kk: (j // per, i, j % per))])
    norm_ffn = lw['norm_ffn'] if early is None else lw['norm_ffn'] + early(g)[0, 0]
    dh2, g['norm_ffn'] = _rms_bwd("rms_bwd", sv['h2'], norm_ffn, dhn, dh3)
    dycat = _matmul("dx_o", dh2, lw['w_o'], "nt", t, d, d, TM, 1024, d)
    g['w_o'] = _matmul("dw_o", sv['ycat'], dh2, "tn", d, d, t, TM, 512, t, out_dtype=BF16, n_outer=True)
    proj = sv['proj']
    sgu, sconv, gq, gk, gv = _mixer_ins(proj, lw)
    bf2 = [BF16, BF16]
    r_ = _fused("sgu_bwd", _fn_sgu, SG_HEADS, sgu, [_col_out(t, d, F32, 0)], douts=[dycat], need=[True] * 7,
                gdt=bf2 + [F32] * 5)
    du, dv_, g['sg_ln_g'], g['sg_ln_b'], g['sg_w'], g['sg_bb'], g['out_norm_a'] = r_
    r_ = _fused("sconv_bwd", _fn_sconv, SC_GROUPS, sconv, [_col_out(t, d, F32, SG_HEADS)], douts=[dycat],
                need=[True] * 7, gdt=[BF16] * 3 + [F32] * 4)
    dgb, dgc, dxin = r_[:3]
    g['sc_taps'], g['out_norm_b'] = r_[3:6], r_[6]
    r_ = _gdn_scan_bwd(sv['wy'], proj, lw['gdn_norm'], dycat, SG_HEADS + SC_GROUPS)
    dwy, dz, g['gdn_norm'] = r_[:6], r_[6], r_[7]
    n_wy, wy_ins, wy_outs = _gdn_wy_ins(sv['q'], sv['k'], sv['v'], sv['gates'], sv['wy_t'])
    dq, dk, dvv, dgates = _fused("gdn_wy_bwd", _fn_gdn_wy, n_wy, wy_ins, wy_outs, douts=dwy,
                                 need=[True] * 4 + [False] * 3)[:4]
    one = [_col_out(t, GDN_W, F32)]
    r_ = _fused("gdn_q_bwd", _fn_gdn_qk, GDN_HEADS, gq, one, douts=[dq], need=[True] * 5, gdt=[BF16] + [F32] * 4)
    dpq, g['q_taps'] = r_[0], r_[1:]
    r_ = _fused("gdn_k_bwd", _fn_gdn_qk, GDN_HEADS, gk, one, douts=[dk], need=[True] * 5, gdt=[BF16] + [F32] * 4)
    dpk, g['k_taps'] = r_[0], r_[1:]
    r_ = _fused("gdn_v_bwd", _fn_gdn_v, GDN_HEADS, gv, one, douts=[dvv], need=[True] * 5, gdt=[BF16] + [F32] * 4)
    dpv, g['v_taps'] = r_[0], r_[1:]
    dab, g['a_log_row'], g['dt_bias_row'] = _fused(
        "gdn_gates_bwd", _fn_gdn_gates, t // tm, _gates_ins(proj, lw, tm), [_row_out(t, LANES, F32, tm)],
        douts=[dgates], need=[True] * 3, gdt=[BF16, F32, F32])
    pad = jnp.zeros((t, PROJ_W - (CB_AB + 1) * LANES), BF16)
    dproj = jnp.concatenate([du, dv_, dgb, dgc, dxin, dpq, dpk, dpv, dz, dab, pad], axis=1)
    dxn = _matmul("dx_in", dproj, lw['w_in'], "nt", t, d, PROJ_W, TM, 1024, PROJ_W // 2)
    g['w_in'] = _matmul("dw_in", sv['xn'], dproj, "tn", d, PROJ_W, t, TM, 1024, t, out_dtype=BF16, n_outer=True)
    dh, g['norm_mix'] = _rms_bwd("rms_bwd", sv['h'], lw['norm_mix'], dxn, dh2)
    return dh, g


def _loss_grad(h, g, tgt):
    t, d = h.shape
    tm = min(TR, t)

    def body(h_ref, g_ref, t_ref, loss_ref, dh_ref, dg_ref):
        y, vjp = jax.vjp(_rms, h_ref[...], g_ref[...])
        e = y - t_ref[...]
        dh, dg = vjp(e * (1.0 / d))

        @pl.when(pl.program_id(0) == 0)
        def _():
            loss_ref[...] = jnp.zeros_like(loss_ref)
            dg_ref[...] = jnp.zeros_like(dg_ref)

        loss_ref[...] += jnp.sum(jnp.sum(e * e, axis=1, keepdims=True), axis=0, keepdims=True) * (0.5 / d)
        dh_ref[...] = dh
        dg_ref[...] += dg

    row = pl.BlockSpec((tm, d), lambda i: (i, 0))
    return pl.pallas_call(
        body, name="loss_grad", grid=(t // tm,),
        in_specs=[row, pl.BlockSpec((1, d), lambda i: (0, 0)), row],
        out_specs=[pl.BlockSpec((1, LANES), lambda i: (0, 0)), row, pl.BlockSpec((1, d), lambda i: (0, 0))],
        out_shape=[jax.ShapeDtypeStruct((1, LANES), F32), jax.ShapeDtypeStruct((t, d), F32),
                   jax.ShapeDtypeStruct((1, d), F32)],
        compiler_params=_cparams(("arbitrary",)),
    )(h, g, tgt)


def _big_weights(full):
    lw = {}
    if 'w_in' in full:
        d = full['w_in'].shape[1]
        w_in = jnp.transpose(full['w_in'], (1, 0, 2)).reshape(d, IN_COLS)
        lw['w_in'] = jnp.pad(w_in, ((0, 0), (0, PROJ_W - IN_COLS)))
    if 'w_o' in full:
        lw['w_o'] = full['w_o'].reshape(-1, full['w_o'].shape[-1])
    if 'w_ff1' in full:
        lw['w_ff1'] = full['w_ff1']
    if 'w_ff2' in full:
        lw['w_ff2'] = full['w_ff2'].reshape(-1, full['w_ff2'].shape[-1])
    if 'w_ple_gate' in full:
        lw['w_ple_gate'] = full['w_ple_gate'].reshape(-1, full['w_ple_gate'].shape[-1])
    if 'w_ple_proj' in full:
        wpp = full['w_ple_proj']
        lw['w_ple_proj'] = jnp.transpose(wpp, (1, 0, 2)).reshape(wpp.shape[1], -1)
    return lw


def _layer_weights(l, full, convs, small):
    d = small['norm_mix'].shape[-1]
    lw = _big_weights(full)
    for n in ('norm_mix', 'norm_ffn', 'norm_ple'):
        lw[n] = small[n][l].reshape(1, d)
    lw['sg_ln_g'] = small['sg_ln_g'][l].reshape(SG_HEADS, 1, LANES)
    lw['sg_ln_b'] = small['sg_ln_b'][l].reshape(SG_HEADS, 1, LANES)
    lw['sg_w'] = small['sg_w'][l]
    lw['sg_bb'] = jnp.broadcast_to(small['sg_b'][l][:, :, None], (SG_HEADS, SG_CHUNK, LANES))
    lw['out_norm_a'] = small['out_norm_a'][l].reshape(SG_HEADS, 1, LANES)
    lw['out_norm_b'] = small['out_norm_b'][l].reshape(SC_GROUPS, 1, LANES)
    lw['gdn_norm'] = jnp.broadcast_to(small['gdn_norm'][l].reshape(1, 1, LANES), (GDN_HEADS, 1, LANES))
    lw['a_log_row'] = jnp.pad(small['gdn_a_log'][l].reshape(1, GDN_HEADS), ((0, 0), (0, LANES - GDN_HEADS)))
    lw['dt_bias_row'] = jnp.pad(small['gdn_dt_bias'][l].reshape(1, GDN_HEADS), ((0, 0), (0, LANES - GDN_HEADS)))
    sc = convs['sc_conv'][l]
    lw['sc_taps'] = [sc[:, j:j + 1, :] for j in range(SC_KERNEL)]
    gc = jnp.transpose(convs['gdn_conv'][l], (1, 0, 2)).reshape(GDN_CONV, 3 * GDN_W)
    for i, nm in enumerate(('q_taps', 'k_taps', 'v_taps')):
        part = gc[:, i * GDN_W:(i + 1) * GDN_W].reshape(GDN_CONV, GDN_HEADS, 1, LANES)
        lw[nm] = [part[j] for j in range(GDN_CONV)]
    return lw


def _shard_major(name, g):
    if name == 'w_in':
        d = g.shape[0]
        return jnp.transpose(g[:, :IN_COLS].reshape(d, N_SHARD, IN_COLS // N_SHARD), (1, 0, 2))
    if name == 'w_ple_proj':
        return jnp.transpose(g.reshape(g.shape[0], N_SHARD, g.shape[1] // N_SHARD), (1, 0, 2))
    if name == 'w_ff1':
        return g
    return g.reshape(N_SHARD, -1, g.shape[-1])


def _weight_grads(g):
    d = g['norm_mix'].shape[-1]
    out = {}
    for n in ('norm_mix', 'norm_ffn', 'norm_ple'):
        out[n] = g[n].reshape(d)
    out['sg_ln_g'] = g['sg_ln_g'].reshape(SG_W)
    out['sg_ln_b'] = g['sg_ln_b'].reshape(SG_W)
    out['sg_w'] = g['sg_w']
    out['sg_b'] = jnp.sum(g['sg_bb'], axis=2)
    out['out_norm_a'] = g['out_norm_a'].reshape(SG_W)
    out['out_norm_b'] = g['out_norm_b'].reshape(SC_W)
    out['gdn_norm'] = jnp.sum(g['gdn_norm'], axis=(0, 1))
    out['gdn_a_log'] = g['a_log_row'][0, :GDN_HEADS]
    out['gdn_dt_bias'] = g['dt_bias_row'][0, :GDN_HEADS]
    out['sc_conv'] = jnp.concatenate([w.reshape(1, SC_W) for w in g['sc_taps']], axis=0)
    taps = [jnp.concatenate([g[nm][j].reshape(1, GDN_W) for nm in ('q_taps', 'k_taps', 'v_taps')], axis=1)
            for j in range(GDN_CONV)]
    out['gdn_conv'] = jnp.concatenate(taps, axis=0)
    return out


ANY = pl.BlockSpec(memory_space=pl.ANY)


def _place():
    x, y, c = lax.axis_index("x"), lax.axis_index("y"), lax.axis_index("c")
    chips = [(1 - x, y), (x, 1 - y), (1 - x, 1 - y)]
    return x, y, c, chips


def _place_shard(wsh, shard, dtype, layer=None):
    dp, r, cc = wsh.shape
    tr = min(TR, r)
    nb = r // tr

    def body(idx_ref, w_ref, o_ref):
        o_ref[...] = w_ref[...].astype(o_ref.dtype)

    if layer is None:
        grid, shape = (dp * nb,), (dp, N_SHARD, r, cc)
        in_spec = pl.BlockSpec((None, tr, cc), lambda i, ix: (i // nb, i % nb, 0))
        out_spec = pl.BlockSpec((None, None, tr, cc), lambda i, ix: (i // nb, ix[0], i % nb, 0))
    else:
        grid, shape = (nb,), (N_SHARD, r, cc)
        in_spec = pl.BlockSpec((None, tr, cc), lambda i, ix: (layer, i, 0))
        out_spec = pl.BlockSpec((None, tr, cc), lambda i, ix: (ix[0], i, 0))
    return pl.pallas_call(
        body, name="ag_place_shard",
        grid_spec=pltpu.PrefetchScalarGridSpec(num_scalar_prefetch=1, grid=grid, in_specs=[in_spec],
                                               out_specs=out_spec),
        out_shape=jax.ShapeDtypeStruct(shape, dtype),
        compiler_params=_cparams(("arbitrary",)),
    )(shard, wsh)


def _rows(ref, slab, half):
    rh = ref.shape[1] // 2
    return ref.at[slab, pl.ds(pl.multiple_of(half * rh, rh), rh), :]


def _gather_direct(bufs):
    n = len(bufs)

    def body(*refs):
        outs = refs[n:2 * n]
        send_sems, recv_sems = refs[2 * n:]
        x, y, c, chips = _place()
        me = 2 * x + y

        def cp(w, l, j, shard, to):
            blk = outs[w].at[l, shard]
            return pltpu.make_async_remote_copy(src_ref=blk, dst_ref=blk, send_sem=send_sems.at[w, l, j],
                                                recv_sem=recv_sems.at[w, l, j], device_id=to, device_id_type=MESH)

        sends = [cp(w, l, j, me, (*chip, c)) for w in range(n) for l in range(DEPTH) for j, chip in enumerate(chips)]
        for s in sends:
            s.start()
        for w in range(n):
            for l in range(DEPTH):
                for j, (px, py) in enumerate(chips):
                    cp(w, l, j, 2 * px + py, (px, py, c)).wait_recv()
        for s in sends:
            s.wait_send()

    return pl.pallas_call(
        body, name="ag_small", in_specs=[ANY] * n, out_specs=[ANY] * n,
        out_shape=[jax.ShapeDtypeStruct(b.shape, b.dtype) for b in bufs],
        input_output_aliases={w: w for w in range(n)},
        scratch_shapes=[pltpu.SemaphoreType.DMA((n, DEPTH, 3)), pltpu.SemaphoreType.DMA((n, DEPTH, 3))],
    )(*bufs)


def _gather_halves(bufs):
    n = len(bufs)

    def body(*refs):
        outs = refs[n:2 * n]
        send_sems, recv_sems = refs[2 * n:]
        x, y, c, chips = _place()
        me = 2 * x + y
        sibling = (x, y, 1 - c)

        def cp(w, k, shard, half, to):
            blk = _rows(outs[w], shard, half)
            return pltpu.make_async_remote_copy(src_ref=blk, dst_ref=blk, send_sem=send_sems.at[w, k],
                                                recv_sem=recv_sems.at[w, k], device_id=to, device_id_type=MESH)

        sends = []
        for w in range(n):
            for j, chip in enumerate(chips):
                s = cp(w, j, me, c, (*chip, c))
                s.start()
                sends.append(s)
        for w in range(n):
            for j, (px, py) in enumerate(chips):
                cp(w, j, 2 * px + py, c, sibling).wait_recv()
                s = cp(w, 3 + j, 2 * px + py, c, sibling)
                s.start()
                sends.append(s)
        for w in range(n):
            for j, (px, py) in enumerate(chips):
                cp(w, 3 + j, 2 * px + py, 1 - c, sibling).wait_recv()
        for s in sends:
            s.wait_send()

    return pl.pallas_call(
        body, name="ag_gather_halves",
        in_specs=[ANY] * n, out_specs=[ANY] * n,
        out_shape=[jax.ShapeDtypeStruct(b.shape, b.dtype) for b in bufs],
        input_output_aliases={w: w for w in range(n)},
        scratch_shapes=[pltpu.SemaphoreType.DMA((n, 6)), pltpu.SemaphoreType.DMA((n, 6))],
    )(*bufs)


HBM = pl.BlockSpec(memory_space=pltpu.HBM)
SEM = pl.BlockSpec(memory_space=pltpu.SEMAPHORE)
_SPLIT = pltpu.CompilerParams(has_side_effects=pltpu.SideEffectType.DATAFLOW_SIDE_EFFECTING)


def _in_hbm(arrs):
    return [pltpu.with_memory_space_constraint(a, pltpu.HBM) for a in arrs]


def _gather_start(tag, bufs, after):
    n, na = len(bufs), len(after)
    k = 3 * n

    def body(*refs):
        b_refs, sems, token = refs[:n], refs[n + na:n + na + 2 * k], refs[-1]
        x, y, c, chips = _place()
        me = 2 * x + y
        for w in range(n):
            for j, chip in enumerate(chips):
                blk = _rows(b_refs[w], me, c)
                pltpu.make_async_remote_copy(src_ref=blk, dst_ref=blk, send_sem=sems[3 * w + j],
                                             recv_sem=sems[k + 3 * w + j], device_id=(*chip, c),
                                             device_id_type=MESH).start()
        token[...] = jnp.zeros_like(token)

    res = pl.pallas_call(
        body, name="ag_start_" + tag,
        out_shape=(*[pltpu.SemaphoreType.DMA(())] * (2 * k), *[pltpu.HBM(b.shape, b.dtype) for b in bufs],
                   jax.ShapeDtypeStruct((8, LANES), F32)),
        in_specs=[HBM] * n + [ANY] * na,
        out_specs=(*[SEM] * (2 * k), *[HBM] * n, pl.BlockSpec(memory_space=pltpu.VMEM)),
        input_output_aliases={w: 2 * k + w for w in range(n)}, compiler_params=_SPLIT,
    )(*_in_hbm(bufs), *after)
    return list(res[:2 * k]), list(res[2 * k:2 * k + n]), res[2 * k + n]


def _gather_wait(tag, sems, bufs, after):
    n = len(bufs)
    k = 3 * n

    def body(*refs):
        b_refs, sems = refs[:n], refs[n:n + 2 * k]
        x, y, c, chips = _place()
        me = 2 * x + y
        for w in range(n):
            for j, (px, py) in enumerate(chips):
                mine, theirs = _rows(b_refs[w], me, c), _rows(b_refs[w], 2 * px + py, c)
                cp = pltpu.make_async_remote_copy(src_ref=mine, dst_ref=theirs, send_sem=sems[3 * w + j],
                                                  recv_sem=sems[k + 3 * w + j], device_id=(px, py, c),
                                                  device_id_type=MESH)
                cp.wait_send()
                cp.wait_recv()

    res = pl.pallas_call(
        body, name="ag_wait_" + tag, out_shape=tuple(pltpu.HBM(b.shape, b.dtype) for b in bufs),
        in_specs=[HBM] * n + [SEM] * (2 * k) + [ANY], out_specs=(HBM,) * n,
        input_output_aliases={w: w for w in range(n)}, compiler_params=_SPLIT,
    )(*bufs, *sems, after)
    return list(res)


def _gather_forward(bufs):
    n = len(bufs)

    def body(*refs):
        outs = refs[n:2 * n]
        send_sems, recv_sems = refs[2 * n:]
        x, y, c, chips = _place()

        def cp(w, j, shard, half):
            blk = _rows(outs[w], shard, half)
            return pltpu.make_async_remote_copy(src_ref=blk, dst_ref=blk, send_sem=send_sems.at[w, j],
                                                recv_sem=recv_sems.at[w, j], device_id=(x, y, 1 - c),
                                                device_id_type=MESH)

        sends = [cp(w, j, 2 * px + py, c) for w in range(n) for j, (px, py) in enumerate(chips)]
        for s in sends:
            s.start()
        for w in range(n):
            for j, (px, py) in enumerate(chips):
                cp(w, j, 2 * px + py, 1 - c).wait_recv()
        for s in sends:
            s.wait_send()

    return pl.pallas_call(
        body, name="ag_forward", in_specs=[ANY] * n, out_specs=[ANY] * n,
        out_shape=[jax.ShapeDtypeStruct(b.shape, b.dtype) for b in bufs],
        input_output_aliases={w: w for w in range(n)},
        scratch_shapes=[pltpu.SemaphoreType.DMA((n, 3)), pltpu.SemaphoreType.DMA((n, 3))],
    )(*bufs)


def _scatter_start(tag, parts, after):
    n, na = len(parts), len(after)
    k = 3 * n
    lands = [lax.empty((3,) + g.shape[1:], g.dtype) for g in parts]

    def body(*refs):
        srcs, dsts, sems, token = refs[:n], refs[n:2 * n], refs[2 * n + na:2 * n + na + 2 * k], refs[-1]
        x, y, c, chips = _place()
        for w in range(n):
            for j, (px, py) in enumerate(chips):
                pltpu.make_async_remote_copy(src_ref=srcs[w].at[2 * px + py], dst_ref=dsts[w].at[j],
                                             send_sem=sems[3 * w + j], recv_sem=sems[k + 3 * w + j],
                                             device_id=(px, py, c), device_id_type=MESH).start()
        token[...] = jnp.zeros_like(token)

    res = pl.pallas_call(
        body, name="rs_scatter_start_" + tag,
        out_shape=(*[pltpu.SemaphoreType.DMA(())] * (2 * k), *[pltpu.HBM(a.shape, a.dtype) for a in parts + lands],
                   jax.ShapeDtypeStruct((8, LANES), F32)),
        in_specs=[HBM] * (2 * n) + [ANY] * na,
        out_specs=(*[SEM] * (2 * k), *[HBM] * (2 * n), pl.BlockSpec(memory_space=pltpu.VMEM)),
        input_output_aliases={w: 2 * k + w for w in range(2 * n)}, compiler_params=_SPLIT,
    )(*_in_hbm(parts + lands), *after)
    return list(res[:2 * k]), list(res[2 * k:2 * k + n]), list(res[2 * k + n:2 * k + 2 * n]), res[2 * k + 2 * n]


def _scatter_wait(tag, sems, parts, lands, after):
    n = len(parts)
    k = 3 * n

    def body(*refs):
        srcs, dsts, sems = refs[:n], refs[n:2 * n], refs[2 * n:2 * n + 2 * k]
        x, y, c, chips = _place()
        for w in range(n):
            for j, (px, py) in enumerate(chips):
                cp = pltpu.make_async_remote_copy(src_ref=srcs[w].at[2 * px + py], dst_ref=dsts[w].at[j],
                                                  send_sem=sems[3 * w + j], recv_sem=sems[k + 3 * w + j],
                                                  device_id=(px, py, c), device_id_type=MESH)
                cp.wait_send()
                cp.wait_recv()

    res = pl.pallas_call(
        body, name="rs_scatter_wait_" + tag, out_shape=tuple(pltpu.HBM(a.shape, a.dtype) for a in parts + lands),
        in_specs=[HBM] * (2 * n) + [SEM] * (2 * k) + [ANY], out_specs=(HBM,) * (2 * n),
        input_output_aliases={w: w for w in range(2 * n)}, compiler_params=_SPLIT,
    )(*parts, *lands, *sems, after)
    return list(res[:n]), list(res[n:])


def _swap_halves(grads):
    n = len(grads)

    def body(*refs):
        srcs, outs = refs[:n], refs[n:2 * n]
        send_sems, recv_sems = refs[2 * n:]
        x, y, c, _ = _place()
        cps = []
        for w in range(n):
            rh = srcs[w].shape[1] // 2
            theirs = srcs[w].at[:, pl.ds(pl.multiple_of((1 - c) * rh, rh), rh), :]
            cps.append(pltpu.make_async_remote_copy(src_ref=theirs, dst_ref=outs[w], send_sem=send_sems.at[w],
                                                    recv_sem=recv_sems.at[w], device_id=(x, y, 1 - c),
                                                    device_id_type=MESH))
        for cpy in cps:
            cpy.start()
        for cpy in cps:
            cpy.wait()

    return pl.pallas_call(
        body, name="rs_swap_halves", in_specs=[ANY] * n, out_specs=[ANY] * n,
        out_shape=[jax.ShapeDtypeStruct((g.shape[0], g.shape[1] // 2, g.shape[2]), g.dtype) for g in grads],
        scratch_shapes=[pltpu.SemaphoreType.DMA((n,)), pltpu.SemaphoreType.DMA((n,))],
    )(*grads)


def _scatter_shards(parts):
    n = len(parts)

    def body(*refs):
        srcs, outs = refs[:n], refs[n:2 * n]
        send_sems, recv_sems = refs[2 * n:]
        x, y, c, chips = _place()

        def cp(w, j, src_shard, to):
            return pltpu.make_async_remote_copy(
                src_ref=srcs[w].at[src_shard], dst_ref=outs[w].at[j], send_sem=send_sems.at[w, j],
                recv_sem=recv_sems.at[w, j], device_id=to, device_id_type=MESH)

        sends = [cp(w, j, 2 * px + py, (px, py, c)) for w in range(n) for j, (px, py) in enumerate(chips)]
        for s in sends:
            s.start()
        for s in sends:
            s.wait()

    return pl.pallas_call(
        body, name="rs_scatter", in_specs=[ANY] * n, out_specs=[ANY] * n,
        out_shape=[jax.ShapeDtypeStruct((3,) + g.shape[1:], g.dtype) for g in parts],
        scratch_shapes=[pltpu.SemaphoreType.DMA((n, 3)), pltpu.SemaphoreType.DMA((n, 3))],
    )(*parts)


def _sum_reduced(part, slots, shard, core, layer, prev):
    _, rh, cc = part.shape
    tr = min(TR, rh)
    nbh = rh // tr

    def body(shard_ref, core_ref, p_ref, s0_ref, s1_ref, s2_ref, *rest):
        acc = p_ref[...].astype(F32)
        for s_ref in (s0_ref, s1_ref, s2_ref):
            acc = acc + s_ref[...].astype(F32)
        rest[-1][...] = acc

    slot = lambda j: pl.BlockSpec((None, tr, cc), lambda i, sh, co: (j, i, 0))
    ins = [shard, core, part, slots, slots, slots] + ([] if prev is None else [prev])
    return pl.pallas_call(
        body, name="rs_sum_reduced",
        grid_spec=pltpu.PrefetchScalarGridSpec(
            num_scalar_prefetch=2, grid=(nbh,),
            in_specs=[pl.BlockSpec((None, tr, cc), lambda i, sh, co: (sh[0], i, 0)), slot(0), slot(1), slot(2)]
            + ([] if prev is None else [ANY]),
            out_specs=pl.BlockSpec((None, tr, cc), lambda i, sh, co: (layer, co[0] * nbh + i, 0))),
        out_shape=jax.ShapeDtypeStruct((DEPTH, 2 * rh, cc), F32),
        input_output_aliases={} if prev is None else {6: 0},
        compiler_params=_cparams(("arbitrary",)),
    )(*ins)


def _join_halves(bufs, layer):
    n = len(bufs)

    def body(*refs):
        outs = refs[n:2 * n]
        send_sems, recv_sems = refs[2 * n:]
        x, y, c, _ = _place()

        def cp(w, half):
            blk = _rows(outs[w], layer, half)
            return pltpu.make_async_remote_copy(src_ref=blk, dst_ref=blk, send_sem=send_sems.at[w],
                                                recv_sem=recv_sems.at[w], device_id=(x, y, 1 - c), device_id_type=MESH)

        sends = [cp(w, c) for w in range(n)]
        for s in sends:
            s.start()
        for w in range(n):
            cp(w, 1 - c).wait_recv()
        for s in sends:
            s.wait_send()

    return pl.pallas_call(
        body, name="rs_join_halves", in_specs=[ANY] * n, out_specs=[ANY] * n,
        out_shape=[jax.ShapeDtypeStruct(b.shape, b.dtype) for b in bufs],
        input_output_aliases={w: w for w in range(n)},
        scratch_shapes=[pltpu.SemaphoreType.DMA((n,)), pltpu.SemaphoreType.DMA((n,))],
    )(*bufs)


def _exchange_small(vec):
    def body(src, out, send_sems, recv_sems, local_sem):
        x, y, c, _ = _place()
        me = 4 * x + 2 * y + c
        lc = pltpu.make_async_copy(src, out.at[me], local_sem)
        lc.start()
        sends = []
        for k in range(1, N_DEV):
            fx, fy, fc = (k >> 2) & 1, (k >> 1) & 1, k & 1
            to = (x ^ fx, y ^ fy, c ^ fc)
            s = pltpu.make_async_remote_copy(src_ref=src, dst_ref=out.at[me], send_sem=send_sems.at[k - 1],
                                             recv_sem=recv_sems.at[k - 1], device_id=to, device_id_type=MESH)
            s.start()
            sends.append(s)
        for k in range(1, N_DEV):
            fx, fy, fc = (k >> 2) & 1, (k >> 1) & 1, k & 1
            frm = 4 * (x ^ fx) + 2 * (y ^ fy) + (c ^ fc)
            pltpu.make_async_remote_copy(src_ref=src, dst_ref=out.at[frm], send_sem=send_sems.at[k - 1],
                                         recv_sem=recv_sems.at[k - 1], device_id=(x ^ fx, y ^ fy, c ^ fc),
                                         device_id_type=MESH).wait_recv()
        for s in sends:
            s.wait_send()
        lc.wait()

    return pl.pallas_call(
        body, name="small_exchange", in_specs=[ANY], out_specs=ANY,
        out_shape=jax.ShapeDtypeStruct((N_DEV,) + vec.shape, vec.dtype),
        scratch_shapes=[pltpu.SemaphoreType.DMA((N_DEV - 1,)), pltpu.SemaphoreType.DMA((N_DEV - 1,)),
                        pltpu.SemaphoreType.DMA],
    )(vec)


def _add_own_half(g, other, c_arr):
    ns, rh, cc = other.shape
    tr = min(TR, rh)
    nb = rh // tr

    def body(c_ref, g_ref, o_ref, out_ref):
        out_ref[...] = (g_ref[...].astype(F32) + o_ref[...].astype(F32)).astype(out_ref.dtype)

    slab = pl.BlockSpec((None, tr, cc), lambda i, cr: (i // nb, i % nb, 0))
    return pl.pallas_call(
        body, name="rs_add_own_half",
        grid_spec=pltpu.PrefetchScalarGridSpec(
            num_scalar_prefetch=1, grid=(ns * nb,),
            in_specs=[pl.BlockSpec((None, tr, cc), lambda i, cr: (i // nb, cr[0] * nb + i % nb, 0)), slab],
            out_specs=slab),
        out_shape=jax.ShapeDtypeStruct(other.shape, other.dtype),
        compiler_params=_cparams(("arbitrary",)),
    )(c_arr, g, other)


def _sum_slots(name, a):
    ns, r, cc = a.shape
    tr = min(TR, r)
    ins = [_In(a, (None, tr, cc), functools.partial(lambda s, i: (s, i, 0), s)) for s in range(ns)]

    def fn(*blocks):
        acc = blocks[0].astype(F32)
        for b in blocks[1:]:
            acc = acc + b.astype(F32)
        return (acc,)

    return _fused(name, fn, r // tr, ins, [((r, cc), F32, (tr, cc), lambda i: (i, 0))])[0]


def _adamw_fn(w, g, m, v):
    m = ADAM_B1 * m + (1.0 - ADAM_B1) * g
    v = ADAM_B2 * v + (1.0 - ADAM_B2) * jnp.square(g)
    m_hat = m / (1.0 - ADAM_B1 ** ADAM_STEP)
    v_hat = v / (1.0 - ADAM_B2 ** ADAM_STEP)
    delta = -ADAM_LR * (m_hat / (jnp.sqrt(v_hat) + ADAM_EPS) + ADAM_WD * w)
    return delta, m, v


def _adamw(w, g, m, v):
    r, cc = w.shape
    tr = min(TR, r)
    ins = [_In(a, (tr, cc), lambda i: (i, 0)) for a in (w, g, m, v)]
    return _fused("adamw", _adamw_fn, r // tr, ins, [((r, cc), F32, (tr, cc), lambda i: (i, 0))] * 3)


def _pack(arrs):
    flat = jnp.concatenate([a.reshape(-1) for a in arrs])
    tile = TR * LANES
    n = -(-flat.shape[0] // tile) * tile
    return jnp.pad(flat, (0, n - flat.shape[0])).reshape(-1, LANES)


def _unpack(vec, shapes):
    flat = vec.reshape(-1)
    out, o = [], 0
    for s in shapes:
        n = math.prod(s)
        out.append(flat[o:o + n].reshape(s))
        o += n
    return out


def kernel(x, p, norm_mix, w_in, sg_ln_g, sg_ln_b, sg_w, sg_b, sc_conv, gdn_conv, gdn_a_log, gdn_dt_bias, gdn_norm, out_norm_a, out_norm_b, w_o, norm_ffn, w_ff1, w_ff2, norm_ple, w_ple_gate, w_ple_proj, norm_final, loss_target, m_norm_mix, m_w_in, m_sg_ln_g, m_sg_ln_b, m_sg_w, m_sg_b, m_sc_conv, m_gdn_conv, m_gdn_a_log, m_gdn_dt_bias, m_gdn_norm, m_out_norm_a, m_out_norm_b, m_w_o, m_norm_ffn, m_w_ff1, m_w_ff2, m_norm_ple, m_w_ple_gate, m_w_ple_proj, m_norm_final, v_norm_mix, v_w_in, v_sg_ln_g, v_sg_ln_b, v_sg_w, v_sg_b, v_sc_conv, v_gdn_conv, v_gdn_a_log, v_gdn_dt_bias, v_gdn_norm, v_out_norm_a, v_out_norm_b, v_w_o, v_norm_ffn, v_w_ff1, v_w_ff2, v_norm_ple, v_w_ple_gate, v_w_ple_proj, v_norm_final):
    given = dict(locals())
    w = {n: given[n] for n in WEIGHTS}
    m = {n: given['m_' + n] for n in WEIGHTS}
    v = {n: given['v_' + n] for n in WEIGHTS}
    shard = 2 * lax.axis_index("x") + lax.axis_index("y")
    core = lax.axis_index("c")

    shard_arr = shard.reshape(1).astype(jnp.int32)
    c_arr = core.reshape(1).astype(jnp.int32)
    convs = dict(zip(CONVS, _gather_direct([_place_shard(w[n], shard_arr, F32) for n in CONVS])))
    small = {n: w[n] for n in SMALL if n not in CONVS}
    late_w = [n for n in BIG if n != 'w_in']
    first = _gather_halves([_place_shard(w['w_in'], shard_arr, BF16, 0)])
    fly0 = _gather_start("l0", [_place_shard(w[n], shard_arr, BF16, 0) for n in late_w],
                         [first[0], *convs.values()])
    fly1 = _gather_start("l1", [_place_shard(w[n], shard_arr, BF16, 1) for n in BIG], [fly0[2]])

    lws = [_layer_weights(0, {'w_in': first[0]}, convs, small), None]
    lws[0]['norm_mix'] = lws[0]['norm_mix'] + fly1[2][0, 0]

    def rest_of_layer0(behind):
        got = _gather_forward(_gather_wait("l0", fly0[0], fly0[1], behind))
        lws[0].update(_big_weights(dict(zip(late_w, got))))

    h, sv0 = _layer_fwd(x[0], p[0, 0], lws[0], rest_of_layer0)
    got = _gather_forward(_gather_wait("l1", fly1[0], fly1[1], h))
    lws[1] = _layer_weights(1, dict(zip(BIG, got)), convs, small)
    h, sv1 = _layer_fwd(h, p[1, 0], lws[1])
    saved = [sv0, sv1]
    loss, dh, dnf = _loss_grad(h, small['norm_final'].reshape(1, -1), loss_target[0])

    def swap_add(names, g):
        big = [_shard_major(n, g[n]) for n in names]
        return [_add_own_half(g_, o, c_arr) for g_, o in zip(big, _swap_halves(big))]

    early_g = ['w_ff1', 'w_ff2', 'w_ple_gate', 'w_ple_proj']
    late_g = [n for n in BIG if n not in early_g]
    per_layer = [None] * DEPTH
    dh, g = _layer_bwd(dh, saved[1], lws[1])
    per_layer[1] = _weight_grads(g)
    fly1 = _scatter_start("l1", swap_add(BIG, g), [dh])
    lws[0]['norm_ple'] = lws[0]['norm_ple'] + fly1[3][0, 0]
    fly0 = []

    def early_grads(g0):
        fly0.extend(_scatter_start("l0", swap_add(early_g, g0), []))
        return fly0[3]

    dh, g = _layer_bwd(dh, saved[0], lws[0], early_grads)
    per_layer[0] = _weight_grads(g)
    last_parts = swap_add(late_g, g)
    last_slots = _scatter_shards(last_parts)
    groups = [(1, BIG, *_scatter_wait("l1", fly1[0], fly1[1], fly1[2], dh)),
              (0, early_g, *_scatter_wait("l0", fly0[0], fly0[1], fly0[2], dh)), (0, late_g, last_parts, last_slots)]
    g_big = {}
    for l, names, parts, slots in groups:
        sums = [_sum_reduced(pt, sl, shard_arr, c_arr, l, g_big.get(n)) for n, pt, sl in zip(names, parts, slots)]
        g_big.update(zip(names, _join_halves(sums, l)))
    grad_x = dh
    grads = {n: jnp.stack([per_layer[l][n] for l in range(DEPTH)], axis=0) for n in SMALL if n != 'norm_final'}
    grads['norm_final'] = dnf.reshape(-1)

    rep = [n for n in SMALL if n not in CONVS]
    names = rep + CONVS
    vec = _pack([grads[n] for n in names] + [loss[0, :1]])
    total = _sum_slots("small_sum", _exchange_small(vec))
    parts_small = _unpack(total, [grads[n].shape for n in names] + [(1,)])
    g_small = dict(zip(names, parts_small[:-1]))
    loss_out = parts_small[-1].reshape(())
    for n in CONVS:
        width = w[n].shape[-1]
        g_small[n] = lax.dynamic_slice_in_dim(g_small[n], shard * width, width, axis=2)

    delta, new_m, new_v, grad_w = {}, {}, {}, {}
    for n in BIG:
        shp = w[n].shape
        two_d = lambda a: a.reshape(-1, shp[-1])
        d_, m_, v_ = _adamw(two_d(w[n]), two_d(g_big[n]), two_d(m[n]), two_d(v[n]))
        delta[n], new_m[n], new_v[n], grad_w[n] = d_.reshape(shp), m_.reshape(shp), v_.reshape(shp), g_big[n]
    shapes = [w[n].shape for n in SMALL]
    d_, m_, v_ = _adamw(_pack([w[n] for n in SMALL]), _pack([g_small[n] for n in SMALL]),
                        _pack([m[n] for n in SMALL]), _pack([v[n] for n in SMALL]))
    for n, dd, mm, vv in zip(SMALL, _unpack(d_, shapes), _unpack(m_, shapes), _unpack(v_, shapes)):
        delta[n], new_m[n], new_v[n], grad_w[n] = dd, mm, vv, g_small[n]

    return (loss_out, grad_x[None], *[grad_w[n] for n in WEIGHTS], *[delta[n] for n in WEIGHTS],
            *[new_m[n] for n in WEIGHTS], *[new_v[n] for n in WEIGHTS])
```

```python
import functools
import math

import jax
import jax.numpy as jnp
from jax import lax
from jax.experimental import pallas as pl
from jax.experimental.pallas import tpu as pltpu

F32 = jnp.float32
BF16 = jnp.bfloat16
HI = lax.Precision.HIGH
MESH = pl.DeviceIdType.MESH

LANES = 128
EPS = 1e-6
SG_HEADS, SG_CHUNK = 4, 128
SC_GROUPS, SC_KERNEL = 4, 3
GDN_HEADS, GDN_CONV = 8, 4
GDN_L = 128
GDN_CPS = 8
GDN_HPS = 2
SG_W = SG_HEADS * LANES
SC_W = SC_GROUPS * LANES
GDN_W = GDN_HEADS * LANES
IN_COLS = 2 * SG_W + 3 * SC_W + 4 * GDN_W + 2 * GDN_HEADS
PROJ_W = 7168
CB_U, CB_V = 0, 4
CB_GB, CB_GC, CB_X = 8, 12, 16
CB_Q, CB_K, CB_VV, CB_Z, CB_AB = 20, 28, 36, 44, 52
N_SHARD = 4
N_DEV = 8
DEPTH = 2

ADAM_LR, ADAM_B1, ADAM_B2, ADAM_EPS, ADAM_WD, ADAM_STEP = 0.001, 0.9, 0.999, 1e-08, 0.01, 10

VMEM_LIMIT = 56 << 20

WEIGHTS = ['norm_mix', 'w_in', 'sg_ln_g', 'sg_ln_b', 'sg_w', 'sg_b', 'sc_conv', 'gdn_conv', 'gdn_a_log',
           'gdn_dt_bias', 'gdn_norm', 'out_norm_a', 'out_norm_b', 'w_o', 'norm_ffn', 'w_ff1', 'w_ff2', 'norm_ple',
           'w_ple_gate', 'w_ple_proj', 'norm_final']
BIG = ['w_in', 'w_o', 'w_ff1', 'w_ff2', 'w_ple_gate', 'w_ple_proj']
CONVS = ['sc_conv', 'gdn_conv']
SMALL = [n for n in WEIGHTS if n not in BIG]


def _cparams(sem=None):
    return pltpu.CompilerParams(dimension_semantics=sem, vmem_limit_bytes=VMEM_LIMIT)


def _dot(a, b, dims, prec=None):
    return lax.dot_general(a, b, (dims, ((), ())), precision=prec, preferred_element_type=F32)


def _mm(a, b):
    return _dot(a, b, ((1,), (0,)), HI)


def _mm_nt(a, b):
    return _dot(a, b, ((1,), (1,)), HI)


def _mm_tn(a, b):
    return _dot(a, b, ((0,), (0,)), HI)


@jax.custom_vjp
def _bmm(a, b):
    return _dot(a.astype(BF16), b.astype(BF16), ((1,), (0,)))


def _bmm_fwd(a, b):
    return _bmm(a, b), (a, b)


def _bmm_bwd(res, g):
    a, b = res
    gb = g.astype(BF16)
    return _dot(gb, b.astype(BF16), ((1,), (1,))), _dot(a.astype(BF16), gb, ((0,), (0,)))


_bmm.defvjp(_bmm_fwd, _bmm_bwd)


@jax.custom_vjp
def _bmm_tn(a, b):
    return _dot(a.astype(BF16), b.astype(BF16), ((0,), (0,)))


def _bmm_tn_fwd(a, b):
    return _bmm_tn(a, b), (a, b)


def _bmm_tn_bwd(res, g):
    a, b = res
    gb = g.astype(BF16)
    return _dot(b.astype(BF16), gb, ((1,), (1,))), _dot(a.astype(BF16), gb, ((1,), (0,)))


_bmm_tn.defvjp(_bmm_tn_fwd, _bmm_tn_bwd)


def _sigmoid(x):
    return 1.0 / (1.0 + jnp.exp(-x))


def _silu(x):
    return x * _sigmoid(x)


def _gelu(x):
    c = math.sqrt(2.0 / math.pi)
    return 0.5 * x * (1.0 + jnp.tanh(c * (x + 0.044715 * (x * x * x))))


def _softplus(x):
    return jnp.maximum(x, 0.0) + jnp.log(1.0 + jnp.exp(-jnp.abs(x)))


def _rms(x, g):
    return x * lax.rsqrt(jnp.mean(x * x, axis=-1, keepdims=True) + EPS) * g


def _roll_rows(x, shift):
    return pltpu.roll(x, shift % x.shape[0], 0)


@functools.partial(jax.custom_vjp, nondiff_argnums=(1,))
def _shift_down(x, j):
    row = lax.broadcasted_iota(jnp.int32, x.shape, 0)
    return jnp.where(row >= j, _roll_rows(x, j), 0.0)


def _shift_down_fwd(x, j):
    return _shift_down(x, j), None


def _shift_down_bwd(j, _, dy):
    row = lax.broadcasted_iota(jnp.int32, dy.shape, 0)
    return (jnp.where(row < dy.shape[0] - j, _roll_rows(dy, -j), 0.0),)


_shift_down.defvjp(_shift_down_fwd, _shift_down_bwd)


def _causal_conv(x, taps):
    k = len(taps)
    y = taps[k - 1] * x
    for j in range(k - 1):
        y = y + taps[j] * _shift_down(x, k - 1 - j)
    return y


class _In:
    def __init__(self, arr, block, imap, shared=False, gshape=None, gmap=None):
        self.arr, self.block, self.imap, self.shared = arr, block, imap, shared
        self.gshape = arr.shape if gshape is None else gshape
        self.gmap = imap if gmap is None else gmap


def _fused(name, fn, n, ins, outs, douts=None, need=None, gdt=None, add=None):
    n_in, n_out = len(ins), len(outs)
    in_specs = [pl.BlockSpec(s.block, s.imap) for s in ins]
    out_specs = [pl.BlockSpec(bs, im) for _, _, bs, im in outs]
    if douts is None:
        def body(*refs):
            res = fn(*[r[...] for r in refs[:n_in]])
            for r, v in zip(refs[n_in:], res):
                r[...] = v.astype(r.dtype)

        return pl.pallas_call(
            body, name=name, grid=(n,), in_specs=in_specs, out_specs=out_specs,
            out_shape=[jax.ShapeDtypeStruct(s, d) for s, d, _, _ in outs],
            compiler_params=_cparams(("arbitrary",)),
        )(*[s.arr for s in ins])

    gdt = list(gdt) if gdt is not None else [F32] * n_in
    add = dict(add or {})
    gidx = [i for i in range(n_in) if need[i]]
    aidx = [i for i in gidx if i in add]

    def body(*refs):
        in_refs, d_refs = refs[:n_in], refs[n_in:n_in + n_out]
        a_refs = dict(zip(aidx, refs[n_in + n_out:n_in + n_out + len(aidx)]))
        g_refs = refs[n_in + n_out + len(aidx):]
        vals = [r[...] for r in in_refs]

        def f(*dv):
            full = list(vals)
            for i, v in zip(gidx, dv):
                full[i] = v
            return tuple(o.astype(F32) for o in fn(*full))

        _, vjp = jax.vjp(f, *[vals[i].astype(F32) for i in gidx])
        grads = vjp(tuple(r[...].astype(F32) for r in d_refs))
        for i, g_ref, g in zip(gidx, g_refs, grads):
            if i in a_refs:
                g = g + a_refs[i][...].astype(F32)
            if ins[i].shared:
                period = n if ins[i].shared is True else ins[i].shared

                @pl.when(pl.program_id(0) % period == 0)
                def _():
                    g_ref[...] = jnp.zeros_like(g_ref)
                g_ref[...] += g.astype(g_ref.dtype)
            else:
                g_ref[...] = g.astype(g_ref.dtype)

    g_specs = [pl.BlockSpec(ins[i].block, ins[i].gmap) for i in gidx]
    g_shape = [jax.ShapeDtypeStruct(ins[i].gshape, gdt[i]) for i in gidx]
    res = pl.pallas_call(
        body, name=name, grid=(n,), in_specs=in_specs + out_specs + [g_specs[gidx.index(i)] for i in aidx],
        out_specs=g_specs, out_shape=g_shape,
        compiler_params=_cparams(("arbitrary",)),
    )(*[s.arr for s in ins], *douts, *[add[i] for i in aidx])
    full = [None] * n_in
    for i, g in zip(gidx, res):
        full[i] = g
    return full


def _row_in(a, tm, cb=None):
    if cb is None:
        return _In(a, (tm, a.shape[1]), lambda i: (i, 0))
    return _In(a, (tm, LANES), lambda i: (i, cb), gshape=(a.shape[0], LANES), gmap=lambda i: (i, 0))


def _row_shared(a):
    return _In(a, a.shape, lambda i: (0, 0), shared=True)


def _row_out(t, w, dt, tm):
    return ((t, w), dt, (tm, w), lambda i: (i, 0))


def _col_in(a, base, nblk):
    t = a.shape[0]
    return _In(a, (t, LANES), lambda i: (0, base + i), gshape=(t, nblk * LANES), gmap=lambda i: (0, i))


def _col_par(a):
    return _In(a, (None,) + a.shape[1:], lambda i: (i, 0, 0))


def _col_out(t, w, dt, base=0):
    return ((t, w), dt, (t, LANES), lambda i: (0, base + i))


def _matmul(name, a, b, mode, m, n, k, tm, tn, tk, b_spec=None, epilogue=None, extras=(), outs=None, out_dtype=F32,
            n_outer=False):
    tm, tn, tk = min(tm, m), min(tn, n), min(tk, k)
    assert m % tm == 0 and n % tn == 0 and k % tk == 0, (name, m, n, k, tm, tn, tk)
    nk = k // tk
    a_spec = (pl.BlockSpec((tk, tm), lambda i, j, kk: (kk, i)) if mode == "tn"
              else pl.BlockSpec((tm, tk), lambda i, j, kk: (i, kk)))
    if b_spec is None:
        b_spec = (pl.BlockSpec((tn, tk), lambda i, j, kk: (j, kk)) if mode == "nt"
                  else pl.BlockSpec((tk, tn), lambda i, j, kk: (kk, j)))
    dims = {"nn": ((1,), (0,)), "nt": ((1,), (1,)), "tn": ((0,), (0,))}[mode]
    if outs is None:
        outs = [((m, n), out_dtype, (tm, tn), lambda i, j, kk: (i, j))]
    if epilogue is None:
        epilogue = lambda acc: (acc,)
    n_ex = len(extras)

    def body(*refs):
        a_ref, b_ref = refs[0], refs[1]
        ex_refs = refs[2:2 + n_ex]
        o_refs = refs[2 + n_ex:2 + n_ex + len(outs)]
        part = _dot(a_ref[...].astype(BF16), b_ref[...].astype(BF16), dims)

        def finish(acc):
            for r, v in zip(o_refs, epilogue(acc, *[e[...] for e in ex_refs])):
                r[...] = v.astype(r.dtype)

        if nk == 1:
            finish(part)
        else:
            acc_ref = refs[-1]
            kk = pl.program_id(2)

            @pl.when(kk == 0)
            def _():
                acc_ref[...] = part

            @pl.when(kk > 0)
            def _():
                acc_ref[...] += part

            @pl.when(kk == nk - 1)
            def _():
                finish(acc_ref[...])

    ex_specs = [pl.BlockSpec((tm, tn), lambda i, j, kk: (i, j)) for _ in extras]
    out_specs = [pl.BlockSpec(bs, im) for _, _, bs, im in outs]
    in_specs = [a_spec, b_spec] + ex_specs
    grid = (m // tm, n // tn, nk)
    if n_outer:
        swap = lambda sp: pl.BlockSpec(sp.block_shape, functools.partial(lambda f, j, i, kk: f(i, j, kk), sp.index_map))
        in_specs, out_specs, grid = [swap(sp) for sp in in_specs], [swap(sp) for sp in out_specs], (n // tn, m // tm, nk)
    res = pl.pallas_call(
        body, name=name, grid=grid,
        in_specs=in_specs,
        out_specs=out_specs,
        out_shape=[jax.ShapeDtypeStruct(s, d) for s, d, _, _ in outs],
        scratch_shapes=[pltpu.VMEM((tm, tn), F32)] if nk > 1 else [],
        compiler_params=_cparams(("parallel", "parallel", "arbitrary")),
    )(a, b, *extras)
    return res if len(res) > 1 else res[0]


def _fn_sgu(u_pre, v_pre, ln_g, ln_b, w, bb, na):
    t = u_pre.shape[0]
    u = _gelu(u_pre)
    v = _gelu(v_pre)
    mu = jnp.mean(v, axis=-1, keepdims=True)
    vc = v - mu
    vh = vc * lax.rsqrt(jnp.mean(vc * vc, axis=-1, keepdims=True) + EPS) * ln_g + ln_b
    ri = lax.broadcasted_iota(jnp.int32, w.shape, 0)
    ci = lax.broadcasted_iota(jnp.int32, w.shape, 1)
    wc = jnp.where(ri >= ci, w, 0.0)
    f = jnp.concatenate([_bmm(wc, vh[c * SG_CHUNK:(c + 1) * SG_CHUNK]) + bb for c in range(t // SG_CHUNK)], axis=0)
    return (_rms(u * f, na),)


def _fn_sconv(gb, gc, xin, w0, w1, w2, nb):
    return (_rms(gb * _causal_conv(gc * xin, (w0, w1, w2)), nb),)


def _fn_gdn_qk(pre, w0, w1, w2, w3):
    a = _silu(_causal_conv(pre, (w0, w1, w2, w3)))
    return (a * lax.rsqrt(jnp.sum(a * a, axis=-1, keepdims=True) + EPS),)


def _fn_gdn_v(pre, w0, w1, w2, w3):
    return (_silu(_causal_conv(pre, (w0, w1, w2, w3))),)


def _fn_gdn_gates(ab, a_log, dt_bias):
    lane = lax.broadcasted_iota(jnp.int32, ab.shape, 1)
    g = -jnp.exp(a_log) * _softplus(ab + dt_bias)
    return (jnp.where(lane < GDN_HEADS, g, jnp.where(lane < 2 * GDN_HEADS, _sigmoid(ab), 0.0)),)


def _solve_unit_lower_impl(mats):
    n = mats[0].shape[0]
    ri = lax.broadcasted_iota(jnp.int32, (n, n), 0)
    ci = lax.broadcasted_iota(jnp.int32, (n, n), 1)
    ts = [(ri == ci).astype(F32)] * len(mats)
    sh = 0
    while (1 << sh) < n:
        rb = jnp.right_shift(ri, sh)
        cb = jnp.right_shift(ci, sh)
        off = ((rb & 1) == 1) & (cb == rb - 1)
        us = [_mm(t, jnp.where(off, a, 0.0)) for t, a in zip(ts, mats)]
        ts = [t - _mm(u, t) for t, u in zip(ts, us)]
        sh += 1
    return tuple(ts)


def _solve_cotangents(ts, dts):
    us = [_mm_nt(dt, t) for t, dt in zip(ts, dts)]
    return tuple(-_mm_tn(t, u) for t, u in zip(ts, us))


@jax.custom_vjp
def _solve_unit_lower(mats):
    return _solve_unit_lower_impl(mats)


def _solve_unit_lower_fwd(mats):
    ts = _solve_unit_lower_impl(mats)
    return ts, ts


def _solve_unit_lower_bwd(ts, dts):
    return (_solve_cotangents(ts, dts),)


_solve_unit_lower.defvjp(_solve_unit_lower_fwd, _solve_unit_lower_bwd)


@jax.custom_vjp
def _solved_unit_lower(mats, ts):
    return ts


def _solved_unit_lower_fwd(mats, ts):
    return ts, ts


def _solved_unit_lower_bwd(ts, dts):
    return _solve_cotangents(ts, dts), tuple(jnp.zeros_like(t) for t in ts)


_solved_unit_lower.defvjp(_solved_unit_lower_fwd, _solved_unit_lower_bwd)


def _fn_gdn_wy(q, k, v, gates, pick_g, pick_b, t_saved=None):
    n, dk = GDN_L, q.shape[1]
    rows = [slice(c * n, (c + 1) * n) for c in range(q.shape[0] // n)]
    ri = lax.broadcasted_iota(jnp.int32, (n, n), 0)
    ci = lax.broadcasted_iota(jnp.int32, (n, n), 1)
    incl = ri >= ci
    eye = (ri == ci).astype(F32)
    last = lax.broadcasted_iota(jnp.int32, (n, 1), 0) == (n - 1)
    qs = [q[r] * (dk ** -0.5) for r in rows]
    ks = [k[r] for r in rows]
    gcs, betas, d_incl = [], [], []
    for r in rows:
        g = jnp.sum(gates[r] * pick_g, axis=1, keepdims=True)
        betas.append(jnp.sum(gates[r] * pick_b, axis=1, keepdims=True))
        g_row = jnp.sum(eye * g, axis=0, keepdims=True)
        gc = jnp.sum(jnp.where(incl, g_row, 0.0), axis=1, keepdims=True)
        gc_row = jnp.sum(eye * gc, axis=0, keepdims=True)
        gcs.append(gc)
        d_incl.append(jnp.where(incl, jnp.exp(jnp.where(incl, gc - gc_row, 0.0)), 0.0))
    kbs = [kk * b for kk, b in zip(ks, betas)]
    mats = tuple(_mm_nt(kb, kk) * jnp.where(ri > ci, d, 0.0) for kb, kk, d in zip(kbs, ks, d_incl))
    ts = _solve_unit_lower(mats) if t_saved is None else _solved_unit_lower(mats, tuple(t_saved[r] for r in rows))
    egs = [jnp.exp(gc) for gc in gcs]
    values = [_mm(t, v[r] * b) for t, r, b in zip(ts, rows, betas)]
    kcds = [_mm(t, kb * eg) for t, kb, eg in zip(ts, kbs, egs)]
    intras = [_mm_nt(qq, kk) * d for qq, kk, d in zip(qs, ks, d_incl)]
    g_lasts = [jnp.sum(jnp.where(last, gc, 0.0), axis=0, keepdims=True) for gc in gcs]
    qes = [qq * eg for qq, eg in zip(qs, egs)]
    kts = [kk * jnp.exp(gl - gc) for kk, gl, gc in zip(ks, g_lasts, gcs)]
    carries = [jnp.broadcast_to(jnp.exp(gl), (8, LANES)) for gl in g_lasts]
    cat = lambda parts: jnp.concatenate(parts, axis=0)
    res = (cat(values), cat(kcds), cat(qes), cat(kts), cat(intras), cat(carries))
    return res + (cat(ts),) if t_saved is None else res


def _gdn_steps(states, operands):
    v_new = [value - _bmm(kcd, s) for s, (value, kcd, _, _, _, _) in zip(states, operands)]
    outs = [_bmm(qe, s) + _bmm(intra, vn) for s, vn, (_, _, qe, _, intra, _) in zip(states, v_new, operands)]
    new = [s * carry + _bmm_tn(kt, vn) for s, vn, (_, _, _, kt, _, carry) in zip(states, v_new, operands)]
    return tuple(new), tuple(outs)


def _gdn_post(o, z, nrm):
    return _rms(o, nrm) * _silu(z)


def _gdn_wy_ins(q, k, v, gates, t_saved=None):
    t = q.shape[0]
    rb = min(GDN_CPS * GDN_L, t)
    hd = GDN_HEADS
    lane = jnp.arange(LANES)[None, None, :]
    pick_g = (lane == jnp.arange(hd)[:, None, None]).astype(F32)
    pick_b = (lane == jnp.arange(hd)[:, None, None] + hd).astype(F32)
    blk = lambda a: _In(a, (rb, LANES), lambda i: (i // hd, i % hd))
    par = lambda a: _In(a, (None, 1, LANES), lambda i: (i % hd, 0, 0))
    ins = [blk(q), blk(k), blk(v), _In(gates, (rb, LANES), lambda i: (i // hd, 0), shared=hd), par(pick_g),
           par(pick_b)]
    wide = lambda dt: ((t, GDN_W), dt, (rb, LANES), lambda i: (i // hd, i % hd))
    carry = ((hd, 8 * (t // GDN_L), LANES), F32, (None, 8 * (rb // GDN_L), LANES), lambda i: (i % hd, i // hd, 0))
    outs = [wide(F32), wide(BF16), wide(BF16), wide(BF16), wide(BF16), carry]
    if t_saved is None:
        outs.append(wide(F32))
    else:
        ins.append(blk(t_saved))
    return (t // rb) * hd, ins, outs


def _gdn_scan_specs(t):
    wide = GDN_HPS * LANES
    once = pl.Buffered(1)
    col = lambda base: pl.BlockSpec((t, wide), lambda h: (0, base // GDN_HPS + h), pipeline_mode=once)
    carry = pl.BlockSpec((GDN_HPS, 8 * (t // GDN_L), LANES), lambda h: (h, 0, 0))
    par = pl.BlockSpec((GDN_HPS, 1, LANES), lambda h: (h, 0, 0))
    return col, carry, par


def _head_operands(refs, cy_ref, c, i):
    rows = pl.ds(pl.multiple_of(c * GDN_L, GDN_L), GDN_L)
    lanes = pl.ds(i * LANES, LANES)
    return tuple(r[rows, lanes] for r in refs) + (cy_ref[i, pl.ds(pl.multiple_of(c * 8, 8), 1), :],)


def _gdn_scan_fwd(wy, proj, nrm):
    t = wy[0].shape[0]
    nc = t // GDN_L
    heads = range(GDN_HPS)

    def body(val_ref, kcd_ref, qe_ref, kt_ref, in_ref, cy_ref, z_ref, n_ref, y_ref, o_scr):
        big = (val_ref, kcd_ref, qe_ref, kt_ref, in_ref)

        def step(c, states):
            rows = pl.ds(pl.multiple_of(c * GDN_L, GDN_L), GDN_L)
            states, outs = _gdn_steps(states, [_head_operands(big, cy_ref, c, i) for i in heads])
            for i in heads:
                o_scr[rows, pl.ds(i * LANES, LANES)] = outs[i]
            return states

        lax.fori_loop(0, nc, step, tuple(jnp.zeros((LANES, LANES), F32) for _ in heads))
        for i in heads:
            lanes = pl.ds(i * LANES, LANES)
            y_ref[:, lanes] = _gdn_post(o_scr[:, lanes], z_ref[:, lanes], n_ref[i]).astype(y_ref.dtype)

    col, carry, par = _gdn_scan_specs(t)
    return pl.pallas_call(
        body, name="gdn_scan_fwd", grid=(GDN_HEADS // GDN_HPS,),
        in_specs=[col(0)] * 5 + [carry, col(CB_Z), par],
        out_specs=col(0), out_shape=jax.ShapeDtypeStruct((t, GDN_W), BF16),
        scratch_shapes=[pltpu.VMEM((t, GDN_HPS * LANES), F32)],
        compiler_params=_cparams(("arbitrary",)),
    )(*wy, proj, nrm)


def _gdn_scan_bwd(wy, proj, nrm, dy, dy_base):
    t = wy[0].shape[0]
    nc = t // GDN_L
    heads = range(GDN_HPS)

    def body(val_ref, kcd_ref, qe_ref, kt_ref, in_ref, cy_ref, z_ref, n_ref, dy_ref,
             dval_ref, dkcd_ref, dqe_ref, dkt_ref, din_ref, dcy_ref, dz_ref, dn_ref, o_scr, s_scr):
        big = (val_ref, kcd_ref, qe_ref, kt_ref, in_ref)
        dbig = (dval_ref, dkcd_ref, dqe_ref, dkt_ref, din_ref)

        def step(c, states):
            rows = pl.ds(pl.multiple_of(c * GDN_L, GDN_L), GDN_L)
            for i in heads:
                s_scr[i, c] = states[i]
            states, outs = _gdn_steps(states, [_head_operands(big, cy_ref, c, i) for i in heads])
            for i in heads:
                o_scr[rows, pl.ds(i * LANES, LANES)] = outs[i]
            return states

        zeros = tuple(jnp.zeros((LANES, LANES), F32) for _ in heads)
        lax.fori_loop(0, nc, step, zeros)
        for i in heads:
            lanes = pl.ds(i * LANES, LANES)
            _, vjp_post = jax.vjp(_gdn_post, o_scr[:, lanes], z_ref[:, lanes], n_ref[i])
            do, dz, dn = vjp_post(dy_ref[:, lanes].astype(F32))
            dz_ref[:, lanes] = dz.astype(dz_ref.dtype)
            dn_ref[i] = dn
            o_scr[:, lanes] = do
        dcy_ref[...] = jnp.zeros_like(dcy_ref)

        def rstep(k, dstates):
            c = nc - 1 - k
            rows = pl.ds(pl.multiple_of(c * GDN_L, GDN_L), GDN_L)
            ops = [tuple(o.astype(F32) for o in _head_operands(big, cy_ref, c, i)) for i in heads]
            _, vjp_c = jax.vjp(_gdn_steps, tuple(s_scr[i, c] for i in heads), ops)
            dstates, dops = vjp_c((dstates, tuple(o_scr[rows, pl.ds(i * LANES, LANES)] for i in heads)))
            for i in heads:
                for r, g in zip(dbig, dops[i][:5]):
                    r[rows, pl.ds(i * LANES, LANES)] = g.astype(r.dtype)
                dcy_ref[i, pl.ds(pl.multiple_of(c * 8, 8), 1), :] = dops[i][5]
            return dstates

        lax.fori_loop(0, nc, rstep, zeros)

    col, carry, par = _gdn_scan_specs(t)
    wide = jax.ShapeDtypeStruct((t, GDN_W), BF16)
    return pl.pallas_call(
        body, name="gdn_scan_bwd", grid=(GDN_HEADS // GDN_HPS,),
        in_specs=[col(0)] * 5 + [carry, col(CB_Z), par, col(dy_base)],
        out_specs=[col(0)] * 5 + [carry, col(0), par],
        out_shape=[wide] * 5 + [jax.ShapeDtypeStruct(wy[5].shape, F32), wide,
                                jax.ShapeDtypeStruct((GDN_HEADS, 1, LANES), F32)],
        scratch_shapes=[pltpu.VMEM((t, GDN_HPS * LANES), F32), pltpu.VMEM((GDN_HPS, nc, LANES, LANES), F32)],
        compiler_params=_cparams(("arbitrary",)),
    )(*wy, proj, nrm, dy)


TM = 512
TR = 256


def _rms_fwd(name, h, g):
    t, d = h.shape
    tm = min(TR, t)
    return _fused(name, lambda hb, gb: (_rms(hb, gb),), t // tm, [_row_in(h, tm), _row_shared(g)],
                  [_row_out(t, d, BF16, tm)])[0]


def _rms_bwd(name, h, g, dxn, dh_next):
    t, d = h.shape
    tm = min(TR, t)
    dh, dg = _fused(name, lambda hb, gb: (_rms(hb, gb),), t // tm, [_row_in(h, tm), _row_shared(g)],
                    [_row_out(t, d, F32, tm)], douts=[dxn], need=[True, True], add={0: dh_next})
    return dh, dg


def _mixer_ins(proj, lw):
    sgu = [_col_in(proj, CB_U, SG_HEADS), _col_in(proj, CB_V, SG_HEADS), _col_par(lw['sg_ln_g']),
           _col_par(lw['sg_ln_b']), _col_par(lw['sg_w']), _col_par(lw['sg_bb']), _col_par(lw['out_norm_a'])]
    sconv = [_col_in(proj, CB_GB, SC_GROUPS), _col_in(proj, CB_GC, SC_GROUPS), _col_in(proj, CB_X, SC_GROUPS)] + \
            [_col_par(w) for w in lw['sc_taps']] + [_col_par(lw['out_norm_b'])]
    gq = [_col_in(proj, CB_Q, GDN_HEADS)] + [_col_par(w) for w in lw['q_taps']]
    gk = [_col_in(proj, CB_K, GDN_HEADS)] + [_col_par(w) for w in lw['k_taps']]
    gv = [_col_in(proj, CB_VV, GDN_HEADS)] + [_col_par(w) for w in lw['v_taps']]
    return sgu, sconv, gq, gk, gv


def _gates_ins(proj, lw, tm):
    return [_row_in(proj, tm, CB_AB), _row_shared(lw['a_log_row']), _row_shared(lw['dt_bias_row'])]


def _layer_fwd(h, p_l, lw, late=None):
    t, d = h.shape
    xn = _rms_fwd("rms_fwd", h, lw['norm_mix'])
    proj = _matmul("proj_fwd", xn, lw['w_in'], "nn", t, PROJ_W, d, TM, 1024, d)
    sgu, sconv, gq, gk, gv = _mixer_ins(proj, lw)
    ya = _fused("sgu_fwd", _fn_sgu, SG_HEADS, sgu, [_col_out(t, SG_W, BF16)])[0]
    yb = _fused("sconv_fwd", _fn_sconv, SC_GROUPS, sconv, [_col_out(t, SC_W, BF16)])[0]
    q = _fused("gdn_q_fwd", _fn_gdn_qk, GDN_HEADS, gq, [_col_out(t, GDN_W, F32)])[0]
    k = _fused("gdn_k_fwd", _fn_gdn_qk, GDN_HEADS, gk, [_col_out(t, GDN_W, F32)])[0]
    v = _fused("gdn_v_fwd", _fn_gdn_v, GDN_HEADS, gv, [_col_out(t, GDN_W, F32)])[0]
    tm = min(TR, t)
    gates = _fused("gdn_gates_fwd", _fn_gdn_gates, t // tm, _gates_ins(proj, lw, tm),
                   [_row_out(t, LANES, F32, tm)])[0]
    n_wy, wy_ins, wy_outs = _gdn_wy_ins(q, k, v, gates)
    wy = _fused("gdn_wy_fwd", _fn_gdn_wy, n_wy, wy_ins, wy_outs)
    wy, wy_t = wy[:6], wy[6]
    yc = _gdn_scan_fwd(wy, proj, lw['gdn_norm'])
    ycat = jnp.concatenate([ya, yb, yc], axis=1)
    if late is not None:
        late(ycat)
    dff = lw['w_ff2'].shape[0]
    h2 = _matmul("wo_fwd", ycat, lw['w_o'], "nn", t, d, d, TM, 1024, d,
                 epilogue=lambda acc, hb: (hb + acc,), extras=(h,))
    hn = _rms_fwd("rms_fwd", h2, lw['norm_ffn'])
    per = d // 1024
    s, r = _matmul("ff1_fwd", hn, lw['w_ff1'], "nn", t, dff, d, TM, 1024, d,
                   b_spec=pl.BlockSpec((None, d, 1024), lambda i, j, kk: (j // per, 0, j % per)),
                   epilogue=lambda acc: (jnp.maximum(acc, 0.0), jnp.square(jnp.maximum(acc, 0.0))),
                   outs=[((t, dff), BF16, (min(TM, t), 1024), lambda i, j, kk: (i, j))] * 2)
    h3 = _matmul("ff2_fwd", r, lw['w_ff2'], "nn", t, d, dff, TM, 1024, 4096,
                 epilogue=lambda acc, hb: (hb + acc,), extras=(h2,))
    hn2 = _rms_fwd("rms_fwd", h3, lw['norm_ple'])
    pp = _matmul("ple_proj_fwd", p_l, lw['w_ple_proj'], "nn", t, d, p_l.shape[1], TM, 1024, p_l.shape[1])

    def gate_epilogue(acc, hb, ppb):
        sg = _sigmoid(acc)
        return hb + ppb * sg, sg

    h4, gate = _matmul("ple_gate_fwd", hn2, lw['w_ple_gate'], "nn", t, d, d, TM, 1024, d, epilogue=gate_epilogue,
                       extras=(h3, pp), outs=[((t, d), F32, (min(TM, t), 1024), lambda i, j, kk: (i, j))] * 2)
    saved = dict(h=h, xn=xn, proj=proj, q=q, k=k, v=v, gates=gates, ycat=ycat, h2=h2, hn=hn, s=s, r=r, h3=h3,
                 hn2=hn2, pp=pp, gate=gate, p=p_l, wy=wy, wy_t=wy_t)
    return h4, saved


def _layer_bwd(dh4, sv, lw, early=None):
    t, d = dh4.shape
    dff = lw['w_ff2'].shape[0]
    tm = min(TR, t)
    g = {}
    dacc, dpp = _fused("ple_bwd_gate", lambda dh, pp, gt: (dh * pp * gt * (1.0 - gt), dh * gt), t // tm,
                       [_row_in(dh4, tm), _row_in(sv['pp'], tm), _row_in(sv['gate'], tm)],
                       [_row_out(t, d, BF16, tm)] * 2)
    g['w_ple_gate'] = _matmul("dw_ple_gate", sv['hn2'], dacc, "tn", d, d, t, TM, 1024, t, out_dtype=BF16,
                              n_outer=True)
    g['w_ple_proj'] = _matmul("dw_ple_proj", sv['p'], dpp, "tn", sv['p'].shape[1], d, t, TM, 1024, t,
                              out_dtype=BF16, n_outer=True)
    dhn2 = _matmul("dx_ple_gate", dacc, lw['w_ple_gate'], "nt", t, d, d, TM, 1024, d)
    dh3, g['norm_ple'] = _rms_bwd("rms_bwd", sv['h3'], lw['norm_ple'], dhn2, dh4)
    da = _matmul("dx_ff2", dh3, lw['w_ff2'], "nt", t, dff, d, TM, 1024, d,
                 epilogue=lambda acc, sb: (acc * (2.0 * sb.astype(F32)),), extras=(sv['s'],),
                 outs=[((t, dff), BF16, (min(TM, t), 1024), lambda i, j, kk: (i, j))])
    g['w_ff2'] = _matmul("dw_ff2", sv['r'], dh3, "tn", dff, d, t, TM, 512, t, out_dtype=BF16, n_outer=True)
    per = d // 1024
    dhn = _matmul("dx_ff1", da, lw['w_ff1'], "nt", t, d, dff, TM, 1024, d,
                  b_spec=pl.BlockSpec((None, 1024, d), lambda i, j, kk: (kk, j, 0)))
    g['w_ff1'] = _matmul("dw_ff1", sv['hn'], da, "tn", d, dff, t, TM, 1024, t, n_outer=True,
                         outs=[((N_SHARD, d, d), BF16, (None, TM, 1024), lambda i, j, kk: (j // per, i, j % per))])
    norm_ffn = lw['norm_ffn'] if early is None else lw['norm_ffn'] + early(g)[0, 0]
    dh2, g['norm_ffn'] = _rms_bwd("rms_bwd", sv['h2'], norm_ffn, dhn, dh3)
    dycat = _matmul("dx_o", dh2, lw['w_o'], "nt", t, d, d, TM, 1024, d)
    g['w_o'] = _matmul("dw_o", sv['ycat'], dh2, "tn", d, d, t, TM, 512, t, out_dtype=BF16, n_outer=True)
    proj = sv['proj']
    sgu, sconv, gq, gk, gv = _mixer_ins(proj, lw)
    bf2 = [BF16, BF16]
    r_ = _fused("sgu_bwd", _fn_sgu, SG_HEADS, sgu, [_col_out(t, d, F32, 0)], douts=[dycat], need=[True] * 7,
                gdt=bf2 + [F32] * 5)
    du, dv_, g['sg_ln_g'], g['sg_ln_b'], g['sg_w'], g['sg_bb'], g['out_norm_a'] = r_
    r_ = _fused("sconv_bwd", _fn_sconv, SC_GROUPS, sconv, [_col_out(t, d, F32, SG_HEADS)], douts=[dycat],
                need=[True] * 7, gdt=[BF16] * 3 + [F32] * 4)
    dgb, dgc, dxin = r_[:3]
    g['sc_taps'], g['out_norm_b'] = r_[3:6], r_[6]
    r_ = _gdn_scan_bwd(sv['wy'], proj, lw['gdn_norm'], dycat, SG_HEADS + SC_GROUPS)
    dwy, dz, g['gdn_norm'] = r_[:6], r_[6], r_[7]
    n_wy, wy_ins, wy_outs = _gdn_wy_ins(sv['q'], sv['k'], sv['v'], sv['gates'], sv['wy_t'])
    dq, dk, dvv, dgates = _fused("gdn_wy_bwd", _fn_gdn_wy, n_wy, wy_ins, wy_outs, douts=dwy,
                                 need=[True] * 4 + [False] * 3)[:4]
    one = [_col_out(t, GDN_W, F32)]
    r_ = _fused("gdn_q_bwd", _fn_gdn_qk, GDN_HEADS, gq, one, douts=[dq], need=[True] * 5, gdt=[BF16] + [F32] * 4)
    dpq, g['q_taps'] = r_[0], r_[1:]
    r_ = _fused("gdn_k_bwd", _fn_gdn_qk, GDN_HEADS, gk, one, douts=[dk], need=[True] * 5, gdt=[BF16] + [F32] * 4)
    dpk, g['k_taps'] = r_[0], r_[1:]
    r_ = _fused("gdn_v_bwd", _fn_gdn_v, GDN_HEADS, gv, one, douts=[dvv], need=[True] * 5, gdt=[BF16] + [F32] * 4)
    dpv, g['v_taps'] = r_[0], r_[1:]
    dab, g['a_log_row'], g['dt_bias_row'] = _fused(
        "gdn_gates_bwd", _fn_gdn_gates, t // tm, _gates_ins(proj, lw, tm), [_row_out(t, LANES, F32, tm)],
        douts=[dgates], need=[True] * 3, gdt=[BF16, F32, F32])
    pad = jnp.zeros((t, PROJ_W - (CB_AB + 1) * LANES), BF16)
    dproj = jnp.concatenate([du, dv_, dgb, dgc, dxin, dpq, dpk, dpv, dz, dab, pad], axis=1)
    dxn = _matmul("dx_in", dproj, lw['w_in'], "nt", t, d, PROJ_W, TM, 1024, PROJ_W // 2)
    g['w_in'] = _matmul("dw_in", sv['xn'], dproj, "tn", d, PROJ_W, t, TM, 1024, t, out_dtype=BF16, n_outer=True)
    dh, g['norm_mix'] = _rms_bwd("rms_bwd", sv['h'], lw['norm_mix'], dxn, dh2)
    return dh, g


def _loss_grad(h, g, tgt):
    t, d = h.shape
    tm = min(TR, t)

    def body(h_ref, g_ref, t_ref, loss_ref, dh_ref, dg_ref):
        y, vjp = jax.vjp(_rms, h_ref[...], g_ref[...])
        e = y - t_ref[...]
        dh, dg = vjp(e * (1.0 / d))

        @pl.when(pl.program_id(0) == 0)
        def _():
            loss_ref[...] = jnp.zeros_like(loss_ref)
            dg_ref[...] = jnp.zeros_like(dg_ref)

        loss_ref[...] += jnp.sum(jnp.sum(e * e, axis=1, keepdims=True), axis=0, keepdims=True) * (0.5 / d)
        dh_ref[...] = dh
        dg_ref[...] += dg

    row = pl.BlockSpec((tm, d), lambda i: (i, 0))
    return pl.pallas_call(
        body, name="loss_grad", grid=(t // tm,),
        in_specs=[row, pl.BlockSpec((1, d), lambda i: (0, 0)), row],
        out_specs=[pl.BlockSpec((1, LANES), lambda i: (0, 0)), row, pl.BlockSpec((1, d), lambda i: (0, 0))],
        out_shape=[jax.ShapeDtypeStruct((1, LANES), F32), jax.ShapeDtypeStruct((t, d), F32),
                   jax.ShapeDtypeStruct((1, d), F32)],
        compiler_params=_cparams(("arbitrary",)),
    )(h, g, tgt)


def _big_weights(full):
    lw = {}
    if 'w_in' in full:
        d = full['w_in'].shape[1]
        w_in = jnp.transpose(full['w_in'], (1, 0, 2)).reshape(d, IN_COLS)
        lw['w_in'] = jnp.pad(w_in, ((0, 0), (0, PROJ_W - IN_COLS)))
    if 'w_o' in full:
        lw['w_o'] = full['w_o'].reshape(-1, full['w_o'].shape[-1])
    if 'w_ff1' in full:
        lw['w_ff1'] = full['w_ff1']
    if 'w_ff2' in full:
        lw['w_ff2'] = full['w_ff2'].reshape(-1, full['w_ff2'].shape[-1])
    if 'w_ple_gate' in full:
        lw['w_ple_gate'] = full['w_ple_gate'].reshape(-1, full['w_ple_gate'].shape[-1])
    if 'w_ple_proj' in full:
        wpp = full['w_ple_proj']
        lw['w_ple_proj'] = jnp.transpose(wpp, (1, 0, 2)).reshape(wpp.shape[1], -1)
    return lw


def _layer_weights(l, full, convs, small):
    d = small['norm_mix'].shape[-1]
    lw = _big_weights(full)
    for n in ('norm_mix', 'norm_ffn', 'norm_ple'):
        lw[n] = small[n][l].reshape(1, d)
    lw['sg_ln_g'] = small['sg_ln_g'][l].reshape(SG_HEADS, 1, LANES)
    lw['sg_ln_b'] = small['sg_ln_b'][l].reshape(SG_HEADS, 1, LANES)
    lw['sg_w'] = small['sg_w'][l]
    lw['sg_bb'] = jnp.broadcast_to(small['sg_b'][l][:, :, None], (SG_HEADS, SG_CHUNK, LANES))
    lw['out_norm_a'] = small['out_norm_a'][l].reshape(SG_HEADS, 1, LANES)
    lw['out_norm_b'] = small['out_norm_b'][l].reshape(SC_GROUPS, 1, LANES)
    lw['gdn_norm'] = jnp.broadcast_to(small['gdn_norm'][l].reshape(1, 1, LANES), (GDN_HEADS, 1, LANES))
    lw['a_log_row'] = jnp.pad(small['gdn_a_log'][l].reshape(1, GDN_HEADS), ((0, 0), (0, LANES - GDN_HEADS)))
    lw['dt_bias_row'] = jnp.pad(small['gdn_dt_bias'][l].reshape(1, GDN_HEADS), ((0, 0), (0, LANES - GDN_HEADS)))
    sc = convs['sc_conv'][l]
    lw['sc_taps'] = [sc[:, j:j + 1, :] for j in range(SC_KERNEL)]
    gc = jnp.transpose(convs['gdn_conv'][l], (1, 0, 2)).reshape(GDN_CONV, 3 * GDN_W)
    for i, nm in enumerate(('q_taps', 'k_taps', 'v_taps')):
        part = gc[:, i * GDN_W:(i + 1) * GDN_W].reshape(GDN_CONV, GDN_HEADS, 1, LANES)
        lw[nm] = [part[j] for j in range(GDN_CONV)]
    return lw


def _shard_major(name, g):
    if name == 'w_in':
        d = g.shape[0]
        return jnp.transpose(g[:, :IN_COLS].reshape(d, N_SHARD, IN_COLS // N_SHARD), (1, 0, 2))
    if name == 'w_ple_proj':
        return jnp.transpose(g.reshape(g.shape[0], N_SHARD, g.shape[1] // N_SHARD), (1, 0, 2))
    if name == 'w_ff1':
        return g
    return g.reshape(N_SHARD, -1, g.shape[-1])


def _weight_grads(g):
    d = g['norm_mix'].shape[-1]
    out = {}
    for n in ('norm_mix', 'norm_ffn', 'norm_ple'):
        out[n] = g[n].reshape(d)
    out['sg_ln_g'] = g['sg_ln_g'].reshape(SG_W)
    out['sg_ln_b'] = g['sg_ln_b'].reshape(SG_W)
    out['sg_w'] = g['sg_w']
    out['sg_b'] = jnp.sum(g['sg_bb'], axis=2)
    out['out_norm_a'] = g['out_norm_a'].reshape(SG_W)
    out['out_norm_b'] = g['out_norm_b'].reshape(SC_W)
    out['gdn_norm'] = jnp.sum(g['gdn_norm'], axis=(0, 1))
    out['gdn_a_log'] = g['a_log_row'][0, :GDN_HEADS]
    out['gdn_dt_bias'] = g['dt_bias_row'][0, :GDN_HEADS]
    out['sc_conv'] = jnp.concatenate([w.reshape(1, SC_W) for w in g['sc_taps']], axis=0)
    taps = [jnp.concatenate([g[nm][j].reshape(1, GDN_W) for nm in ('q_taps', 'k_taps', 'v_taps')], axis=1)
            for j in range(GDN_CONV)]
    out['gdn_conv'] = jnp.concatenate(taps, axis=0)
    return out


ANY = pl.BlockSpec(memory_space=pl.ANY)


def _place():
    x, y, c = lax.axis_index("x"), lax.axis_index("y"), lax.axis_index("c")
    chips = [(1 - x, y), (x, 1 - y), (1 - x, 1 - y)]
    return x, y, c, chips


def _place_shard(wsh, shard, dtype, layer=None):
    dp, r, cc = wsh.shape
    tr = min(TR, r)
    nb = r // tr

    def body(idx_ref, w_ref, o_ref):
        o_ref[...] = w_ref[...].astype(o_ref.dtype)

    if layer is None:
        grid, shape = (dp * nb,), (dp, N_SHARD, r, cc)
        in_spec = pl.BlockSpec((None, tr, cc), lambda i, ix: (i // nb, i % nb, 0))
        out_spec = pl.BlockSpec((None, None, tr, cc), lambda i, ix: (i // nb, ix[0], i % nb, 0))
    else:
        grid, shape = (nb,), (N_SHARD, r, cc)
        in_spec = pl.BlockSpec((None, tr, cc), lambda i, ix: (layer, i, 0))
        out_spec = pl.BlockSpec((None, tr, cc), lambda i, ix: (ix[0], i, 0))
    return pl.pallas_call(
        body, name="ag_place_shard",
        grid_spec=pltpu.PrefetchScalarGridSpec(num_scalar_prefetch=1, grid=grid, in_specs=[in_spec],
                                               out_specs=out_spec),
        out_shape=jax.ShapeDtypeStruct(shape, dtype),
        compiler_params=_cparams(("arbitrary",)),
    )(shard, wsh)


def _rows(ref, slab, half):
    rh = ref.shape[1] // 2
    return ref.at[slab, pl.ds(pl.multiple_of(half * rh, rh), rh), :]


def _gather_direct(bufs):
    n = len(bufs)

    def body(*refs):
        outs = refs[n:2 * n]
        send_sems, recv_sems = refs[2 * n:]
        x, y, c, chips = _place()
        me = 2 * x + y

        def cp(w, l, j, shard, to):
            blk = outs[w].at[l, shard]
            return pltpu.make_async_remote_copy(src_ref=blk, dst_ref=blk, send_sem=send_sems.at[w, l, j],
                                                recv_sem=recv_sems.at[w, l, j], device_id=to, device_id_type=MESH)

        sends = [cp(w, l, j, me, (*chip, c)) for w in range(n) for l in range(DEPTH) for j, chip in enumerate(chips)]
        for s in sends:
            s.start()
        for w in range(n):
            for l in range(DEPTH):
                for j, (px, py) in enumerate(chips):
                    cp(w, l, j, 2 * px + py, (px, py, c)).wait_recv()
        for s in sends:
            s.wait_send()

    return pl.pallas_call(
        body, name="ag_small", in_specs=[ANY] * n, out_specs=[ANY] * n,
        out_shape=[jax.ShapeDtypeStruct(b.shape, b.dtype) for b in bufs],
        input_output_aliases={w: w for w in range(n)},
        scratch_shapes=[pltpu.SemaphoreType.DMA((n, DEPTH, 3)), pltpu.SemaphoreType.DMA((n, DEPTH, 3))],
    )(*bufs)


def _gather_halves(bufs):
    n = len(bufs)

    def body(*refs):
        outs = refs[n:2 * n]
        send_sems, recv_sems = refs[2 * n:]
        x, y, c, chips = _place()
        me = 2 * x + y
        sibling = (x, y, 1 - c)

        def cp(w, k, shard, half, to):
            blk = _rows(outs[w], shard, half)
            return pltpu.make_async_remote_copy(src_ref=blk, dst_ref=blk, send_sem=send_sems.at[w, k],
                                                recv_sem=recv_sems.at[w, k], device_id=to, device_id_type=MESH)

        sends = []
        for w in range(n):
            for j, chip in enumerate(chips):
                s = cp(w, j, me, c, (*chip, c))
                s.start()
                sends.append(s)
        for w in range(n):
            for j, (px, py) in enumerate(chips):
                cp(w, j, 2 * px + py, c, sibling).wait_recv()
                s = cp(w, 3 + j, 2 * px + py, c, sibling)
                s.start()
                sends.append(s)
        for w in range(n):
            for j, (px, py) in enumerate(chips):
                cp(w, 3 + j, 2 * px + py, 1 - c, sibling).wait_recv()
        for s in sends:
            s.wait_send()

    return pl.pallas_call(
        body, name="ag_gather_halves",
        in_specs=[ANY] * n, out_specs=[ANY] * n,
        out_shape=[jax.ShapeDtypeStruct(b.shape, b.dtype) for b in bufs],
        input_output_aliases={w: w for w in range(n)},
        scratch_shapes=[pltpu.SemaphoreType.DMA((n, 6)), pltpu.SemaphoreType.DMA((n, 6))],
    )(*bufs)


HBM = pl.BlockSpec(memory_space=pltpu.HBM)
SEM = pl.BlockSpec(memory_space=pltpu.SEMAPHORE)
_SPLIT = pltpu.CompilerParams(has_side_effects=pltpu.SideEffectType.DATAFLOW_SIDE_EFFECTING)


def _in_hbm(arrs):
    return [pltpu.with_memory_space_constraint(a, pltpu.HBM) for a in arrs]


def _gather_start(tag, bufs, after):
    n, na = len(bufs), len(after)
    k = 3 * n

    def body(*refs):
        b_refs, sems, token = refs[:n], refs[n + na:n + na + 2 * k], refs[-1]
        x, y, c, chips = _place()
        me = 2 * x + y
        for w in range(n):
            for j, chip in enumerate(chips):
                blk = _rows(b_refs[w], me, c)
                pltpu.make_async_remote_copy(src_ref=blk, dst_ref=blk, send_sem=sems[3 * w + j],
                                             recv_sem=sems[k + 3 * w + j], device_id=(*chip, c),
                                             device_id_type=MESH).start()
        token[...] = jnp.zeros_like(token)

    res = pl.pallas_call(
        body, name="ag_start_" + tag,
        out_shape=(*[pltpu.SemaphoreType.DMA(())] * (2 * k), *[pltpu.HBM(b.shape, b.dtype) for b in bufs],
                   jax.ShapeDtypeStruct((8, LANES), F32)),
        in_specs=[HBM] * n + [ANY] * na,
        out_specs=(*[SEM] * (2 * k), *[HBM] * n, pl.BlockSpec(memory_space=pltpu.VMEM)),
        input_output_aliases={w: 2 * k + w for w in range(n)}, compiler_params=_SPLIT,
    )(*_in_hbm(bufs), *after)
    return list(res[:2 * k]), list(res[2 * k:2 * k + n]), res[2 * k + n]


def _gather_wait(tag, sems, bufs, after):
    n = len(bufs)
    k = 3 * n

    def body(*refs):
        b_refs, sems = refs[:n], refs[n:n + 2 * k]
        x, y, c, chips = _place()
        me = 2 * x + y
        for w in range(n):
            for j, (px, py) in enumerate(chips):
                mine, theirs = _rows(b_refs[w], me, c), _rows(b_refs[w], 2 * px + py, c)
                cp = pltpu.make_async_remote_copy(src_ref=mine, dst_ref=theirs, send_sem=sems[3 * w + j],
                                                  recv_sem=sems[k + 3 * w + j], device_id=(px, py, c),
                                                  device_id_type=MESH)
                cp.wait_send()
                cp.wait_recv()

    res = pl.pallas_call(
        body, name="ag_wait_" + tag, out_shape=tuple(pltpu.HBM(b.shape, b.dtype) for b in bufs),
        in_specs=[HBM] * n + [SEM] * (2 * k) + [ANY], out_specs=(HBM,) * n,
        input_output_aliases={w: w for w in range(n)}, compiler_params=_SPLIT,
    )(*bufs, *sems, after)
    return list(res)


def _gather_forward(bufs):
    n = len(bufs)

    def body(*refs):
        outs = refs[n:2 * n]
        send_sems, recv_sems = refs[2 * n:]
        x, y, c, chips = _place()

        def cp(w, j, shard, half):
            blk = _rows(outs[w], shard, half)
            return pltpu.make_async_remote_copy(src_ref=blk, dst_ref=blk, send_sem=send_sems.at[w, j],
                                                recv_sem=recv_sems.at[w, j], device_id=(x, y, 1 - c),
                                                device_id_type=MESH)

        sends = [cp(w, j, 2 * px + py, c) for w in range(n) for j, (px, py) in enumerate(chips)]
        for s in sends:
            s.start()
        for w in range(n):
            for j, (px, py) in enumerate(chips):
                cp(w, j, 2 * px + py, 1 - c).wait_recv()
        for s in sends:
            s.wait_send()

    return pl.pallas_call(
        body, name="ag_forward", in_specs=[ANY] * n, out_specs=[ANY] * n,
        out_shape=[jax.ShapeDtypeStruct(b.shape, b.dtype) for b in bufs],
        input_output_aliases={w: w for w in range(n)},
        scratch_shapes=[pltpu.SemaphoreType.DMA((n, 3)), pltpu.SemaphoreType.DMA((n, 3))],
    )(*bufs)


def _scatter_start(tag, parts, after):
    n, na = len(parts), len(after)
    k = 3 * n
    lands = [lax.empty((3,) + g.shape[1:], g.dtype) for g in parts]

    def body(*refs):
        srcs, dsts, sems, token = refs[:n], refs[n:2 * n], refs[2 * n + na:2 * n + na + 2 * k], refs[-1]
        x, y, c, chips = _place()
        for w in range(n):
            for j, (px, py) in enumerate(chips):
                pltpu.make_async_remote_copy(src_ref=srcs[w].at[2 * px + py], dst_ref=dsts[w].at[j],
                                             send_sem=sems[3 * w + j], recv_sem=sems[k + 3 * w + j],
                                             device_id=(px, py, c), device_id_type=MESH).start()
        token[...] = jnp.zeros_like(token)

    res = pl.pallas_call(
        body, name="rs_scatter_start_" + tag,
        out_shape=(*[pltpu.SemaphoreType.DMA(())] * (2 * k), *[pltpu.HBM(a.shape, a.dtype) for a in parts + lands],
                   jax.ShapeDtypeStruct((8, LANES), F32)),
        in_specs=[HBM] * (2 * n) + [ANY] * na,
        out_specs=(*[SEM] * (2 * k), *[HBM] * (2 * n), pl.BlockSpec(memory_space=pltpu.VMEM)),
        input_output_aliases={w: 2 * k + w for w in range(2 * n)}, compiler_params=_SPLIT,
    )(*_in_hbm(parts + lands), *after)
    return list(res[:2 * k]), list(res[2 * k:2 * k + n]), list(res[2 * k + n:2 * k + 2 * n]), res[2 * k + 2 * n]


def _scatter_wait(tag, sems, parts, lands, after):
    n = len(parts)
    k = 3 * n

    def body(*refs):
        srcs, dsts, sems = refs[:n], refs[n:2 * n], refs[2 * n:2 * n + 2 * k]
        x, y, c, chips = _place()
        for w in range(n):
            for j, (px, py) in enumerate(chips):
                cp = pltpu.make_async_remote_copy(src_ref=srcs[w].at[2 * px + py], dst_ref=dsts[w].at[j],
                                                  send_sem=sems[3 * w + j], recv_sem=sems[k + 3 * w + j],
                                                  device_id=(px, py, c), device_id_type=MESH)
                cp.wait_send()
                cp.wait_recv()

    res = pl.pallas_call(
        body, name="rs_scatter_wait_" + tag, out_shape=tuple(pltpu.HBM(a.shape, a.dtype) for a in parts + lands),
        in_specs=[HBM] * (2 * n) + [SEM] * (2 * k) + [ANY], out_specs=(HBM,) * (2 * n),
        input_output_aliases={w: w for w in range(2 * n)}, compiler_params=_SPLIT,
    )(*parts, *lands, *sems, after)
    return list(res[:n]), list(res[n:])


def _swap_halves(grads):
    n = len(grads)

    def body(*refs):
        srcs, outs = refs[:n], refs[n:2 * n]
        send_sems, recv_sems = refs[2 * n:]
        x, y, c, _ = _place()
        cps = []
        for w in range(n):
            rh = srcs[w].shape[1] // 2
            theirs = srcs[w].at[:, pl.ds(pl.multiple_of((1 - c) * rh, rh), rh), :]
            cps.append(pltpu.make_async_remote_copy(src_ref=theirs, dst_ref=outs[w], send_sem=send_sems.at[w],
                                                    recv_sem=recv_sems.at[w], device_id=(x, y, 1 - c),
                                                    device_id_type=MESH))
        for cpy in cps:
            cpy.start()
        for cpy in cps:
            cpy.wait()

    return pl.pallas_call(
        body, name="rs_swap_halves", in_specs=[ANY] * n, out_specs=[ANY] * n,
        out_shape=[jax.ShapeDtypeStruct((g.shape[0], g.shape[1] // 2, g.shape[2]), g.dtype) for g in grads],
        scratch_shapes=[pltpu.SemaphoreType.DMA((n,)), pltpu.SemaphoreType.DMA((n,))],
    )(*grads)


def _scatter_shards(parts):
    n = len(parts)

    def body(*refs):
        srcs, outs = refs[:n], refs[n:2 * n]
        send_sems, recv_sems = refs[2 * n:]
        x, y, c, chips = _place()

        def cp(w, j, src_shard, to):
            return pltpu.make_async_remote_copy(
                src_ref=srcs[w].at[src_shard], dst_ref=outs[w].at[j], send_sem=send_sems.at[w, j],
                recv_sem=recv_sems.at[w, j], device_id=to, device_id_type=MESH)

        sends = [cp(w, j, 2 * px + py, (px, py, c)) for w in range(n) for j, (px, py) in enumerate(chips)]
        for s in sends:
            s.start()
        for s in sends:
            s.wait()

    return pl.pallas_call(
        body, name="rs_scatter", in_specs=[ANY] * n, out_specs=[ANY] * n,
        out_shape=[jax.ShapeDtypeStruct((3,) + g.shape[1:], g.dtype) for g in parts],
        scratch_shapes=[pltpu.SemaphoreType.DMA((n, 3)), pltpu.SemaphoreType.DMA((n, 3))],
    )(*parts)


def _sum_reduced(part, slots, shard, core, layer, prev):
    _, rh, cc = part.shape
    tr = min(TR, rh)
    nbh = rh // tr

    def body(shard_ref, core_ref, p_ref, s0_ref, s1_ref, s2_ref, *rest):
        acc = p_ref[...].astype(F32)
        for s_ref in (s0_ref, s1_ref, s2_ref):
            acc = acc + s_ref[...].astype(F32)
        rest[-1][...] = acc

    slot = lambda j: pl.BlockSpec((None, tr, cc), lambda i, sh, co: (j, i, 0))
    ins = [shard, core, part, slots, slots, slots] + ([] if prev is None else [prev])
    return pl.pallas_call(
        body, name="rs_sum_reduced",
        grid_spec=pltpu.PrefetchScalarGridSpec(
            num_scalar_prefetch=2, grid=(nbh,),
            in_specs=[pl.BlockSpec((None, tr, cc), lambda i, sh, co: (sh[0], i, 0)), slot(0), slot(1), slot(2)]
            + ([] if prev is None else [ANY]),
            out_specs=pl.BlockSpec((None, tr, cc), lambda i, sh, co: (layer, co[0] * nbh + i, 0))),
        out_shape=jax.ShapeDtypeStruct((DEPTH, 2 * rh, cc), F32),
        input_output_aliases={} if prev is None else {6: 0},
        compiler_params=_cparams(("arbitrary",)),
    )(*ins)


def _join_halves(bufs, layer):
    n = len(bufs)

    def body(*refs):
        outs = refs[n:2 * n]
        send_sems, recv_sems = refs[2 * n:]
        x, y, c, _ = _place()

        def cp(w, half):
            blk = _rows(outs[w], layer, half)
            return pltpu.make_async_remote_copy(src_ref=blk, dst_ref=blk, send_sem=send_sems.at[w],
                                                recv_sem=recv_sems.at[w], device_id=(x, y, 1 - c), device_id_type=MESH)

        sends = [cp(w, c) for w in range(n)]
        for s in sends:
            s.start()
        for w in range(n):
            cp(w, 1 - c).wait_recv()
        for s in sends:
            s.wait_send()

    return pl.pallas_call(
        body, name="rs_join_halves", in_specs=[ANY] * n, out_specs=[ANY] * n,
        out_shape=[jax.ShapeDtypeStruct(b.shape, b.dtype) for b in bufs],
        input_output_aliases={w: w for w in range(n)},
        scratch_shapes=[pltpu.SemaphoreType.DMA((n,)), pltpu.SemaphoreType.DMA((n,))],
    )(*bufs)


def _exchange_small(vec):
    def body(src, out, send_sems, recv_sems, local_sem):
        x, y, c, _ = _place()
        me = 4 * x + 2 * y + c
        lc = pltpu.make_async_copy(src, out.at[me], local_sem)
        lc.start()
        sends = []
        for k in range(1, N_DEV):
            fx, fy, fc = (k >> 2) & 1, (k >> 1) & 1, k & 1
            to = (x ^ fx, y ^ fy, c ^ fc)
            s = pltpu.make_async_remote_copy(src_ref=src, dst_ref=out.at[me], send_sem=send_sems.at[k - 1],
                                             recv_sem=recv_sems.at[k - 1], device_id=to, device_id_type=MESH)
            s.start()
            sends.append(s)
        for k in range(1, N_DEV):
            fx, fy, fc = (k >> 2) & 1, (k >> 1) & 1, k & 1
            frm = 4 * (x ^ fx) + 2 * (y ^ fy) + (c ^ fc)
            pltpu.make_async_remote_copy(src_ref=src, dst_ref=out.at[frm], send_sem=send_sems.at[k - 1],
                                         recv_sem=recv_sems.at[k - 1], device_id=(x ^ fx, y ^ fy, c ^ fc),
                                         device_id_type=MESH).wait_recv()
        for s in sends:
            s.wait_send()
        lc.wait()

    return pl.pallas_call(
        body, name="small_exchange", in_specs=[ANY], out_specs=ANY,
        out_shape=jax.ShapeDtypeStruct((N_DEV,) + vec.shape, vec.dtype),
        scratch_shapes=[pltpu.SemaphoreType.DMA((N_DEV - 1,)), pltpu.SemaphoreType.DMA((N_DEV - 1,)),
                        pltpu.SemaphoreType.DMA],
    )(vec)


def _add_own_half(g, other, c_arr):
    ns, rh, cc = other.shape
    tr = min(TR, rh)
    nb = rh // tr

    def body(c_ref, g_ref, o_ref, out_ref):
        out_ref[...] = (g_ref[...].astype(F32) + o_ref[...].astype(F32)).astype(out_ref.dtype)

    slab = pl.BlockSpec((None, tr, cc), lambda i, cr: (i // nb, i % nb, 0))
    return pl.pallas_call(
        body, name="rs_add_own_half",
        grid_spec=pltpu.PrefetchScalarGridSpec(
            num_scalar_prefetch=1, grid=(ns * nb,),
            in_specs=[pl.BlockSpec((None, tr, cc), lambda i, cr: (i // nb, cr[0] * nb + i % nb, 0)), slab],
            out_specs=slab),
        out_shape=jax.ShapeDtypeStruct(other.shape, other.dtype),
        compiler_params=_cparams(("arbitrary",)),
    )(c_arr, g, other)


def _sum_slots(name, a):
    ns, r, cc = a.shape
    tr = min(TR, r)
    ins = [_In(a, (None, tr, cc), functools.partial(lambda s, i: (s, i, 0), s)) for s in range(ns)]

    def fn(*blocks):
        acc = blocks[0].astype(F32)
        for b in blocks[1:]:
            acc = acc + b.astype(F32)
        return (acc,)

    return _fused(name, fn, r // tr, ins, [((r, cc), F32, (tr, cc), lambda i: (i, 0))])[0]


def _adamw_fn(w, g, m, v):
    m = ADAM_B1 * m + (1.0 - ADAM_B1) * g
    v = ADAM_B2 * v + (1.0 - ADAM_B2) * jnp.square(g)
    m_hat = m / (1.0 - ADAM_B1 ** ADAM_STEP)
    v_hat = v / (1.0 - ADAM_B2 ** ADAM_STEP)
    delta = -ADAM_LR * (m_hat / (jnp.sqrt(v_hat) + ADAM_EPS) + ADAM_WD * w)
    return delta, m, v


def _adamw(w, g, m, v):
    if w.ndim == 2:
        r, cc = w.shape
        tr = min(TR, r)
        block, imap, n = (tr, cc), (lambda i: (i, 0)), r // tr
    else:
        dp, r, cc = w.shape
        tr = min(TR, r)
        nb = r // tr
        block, imap, n = (None, tr, cc), (lambda i: (i // nb, i % nb, 0)), dp * nb
    ins = [_In(a, block, imap) for a in (w, g, m, v)]
    return _fused("adamw", _adamw_fn, n, ins, [(w.shape, F32, block, imap)] * 3)


def _pack(arrs):
    flat = jnp.concatenate([a.reshape(-1) for a in arrs])
    tile = TR * LANES
    n = -(-flat.shape[0] // tile) * tile
    return jnp.pad(flat, (0, n - flat.shape[0])).reshape(-1, LANES)


def _unpack(vec, shapes):
    flat = vec.reshape(-1)
    out, o = [], 0
    for s in shapes:
        n = math.prod(s)
        out.append(flat[o:o + n].reshape(s))
        o += n
    return out


def kernel(x, p, norm_mix, w_in, sg_ln_g, sg_ln_b, sg_w, sg_b, sc_conv, gdn_conv, gdn_a_log, gdn_dt_bias, gdn_norm, out_norm_a, out_norm_b, w_o, norm_ffn, w_ff1, w_ff2, norm_ple, w_ple_gate, w_ple_proj, norm_final, loss_target, m_norm_mix, m_w_in, m_sg_ln_g, m_sg_ln_b, m_sg_w, m_sg_b, m_sc_conv, m_gdn_conv, m_gdn_a_log, m_gdn_dt_bias, m_gdn_norm, m_out_norm_a, m_out_norm_b, m_w_o, m_norm_ffn, m_w_ff1, m_w_ff2, m_norm_ple, m_w_ple_gate, m_w_ple_proj, m_norm_final, v_norm_mix, v_w_in, v_sg_ln_g, v_sg_ln_b, v_sg_w, v_sg_b, v_sc_conv, v_gdn_conv, v_gdn_a_log, v_gdn_dt_bias, v_gdn_norm, v_out_norm_a, v_out_norm_b, v_w_o, v_norm_ffn, v_w_ff1, v_w_ff2, v_norm_ple, v_w_ple_gate, v_w_ple_proj, v_norm_final):
    given = dict(locals())
    w = {n: given[n] for n in WEIGHTS}
    m = {n: given['m_' + n] for n in WEIGHTS}
    v = {n: given['v_' + n] for n in WEIGHTS}
    shard = 2 * lax.axis_index("x") + lax.axis_index("y")
    core = lax.axis_index("c")

    shard_arr = shard.reshape(1).astype(jnp.int32)
    c_arr = core.reshape(1).astype(jnp.int32)
    convs = dict(zip(CONVS, _gather_direct([_place_shard(w[n], shard_arr, F32) for n in CONVS])))
    small = {n: w[n] for n in SMALL if n not in CONVS}
    late_w = [n for n in BIG if n != 'w_in']
    first = _gather_halves([_place_shard(w['w_in'], shard_arr, BF16, 0)])
    fly0 = _gather_start("l0", [_place_shard(w[n], shard_arr, BF16, 0) for n in late_w],
                         [first[0], *convs.values()])
    fly1 = _gather_start("l1", [_place_shard(w[n], shard_arr, BF16, 1) for n in BIG], [fly0[2]])

    lws = [_layer_weights(0, {'w_in': first[0]}, convs, small), None]
    lws[0]['norm_mix'] = lws[0]['norm_mix'] + fly1[2][0, 0]

    def rest_of_layer0(behind):
        got = _gather_forward(_gather_wait("l0", fly0[0], fly0[1], behind))
        lws[0].update(_big_weights(dict(zip(late_w, got))))

    h, sv0 = _layer_fwd(x[0], p[0, 0], lws[0], rest_of_layer0)
    got = _gather_forward(_gather_wait("l1", fly1[0], fly1[1], h))
    lws[1] = _layer_weights(1, dict(zip(BIG, got)), convs, small)
    h, sv1 = _layer_fwd(h, p[1, 0], lws[1])
    saved = [sv0, sv1]
    loss, dh, dnf = _loss_grad(h, small['norm_final'].reshape(1, -1), loss_target[0])

    def swap_add(names, g):
        big = [_shard_major(n, g[n]) for n in names]
        return [_add_own_half(g_, o, c_arr) for g_, o in zip(big, _swap_halves(big))]

    early_g = ['w_ff1', 'w_ff2', 'w_ple_gate', 'w_ple_proj']
    late_g = [n for n in BIG if n not in early_g]
    per_layer = [None] * DEPTH
    dh, g = _layer_bwd(dh, saved[1], lws[1])
    per_layer[1] = _weight_grads(g)
    fly1 = _scatter_start("l1", swap_add(BIG, g), [dh])
    lws[0]['norm_ple'] = lws[0]['norm_ple'] + fly1[3][0, 0]
    fly0 = []

    def early_grads(g0):
        fly0.extend(_scatter_start("l0", swap_add(early_g, g0), []))
        return fly0[3]

    dh, g = _layer_bwd(dh, saved[0], lws[0], early_grads)
    per_layer[0] = _weight_grads(g)
    last_parts = swap_add(late_g, g)
    last_slots = _scatter_shards(last_parts)
    groups = [(1, BIG, *_scatter_wait("l1", fly1[0], fly1[1], fly1[2], dh)),
              (0, early_g, *_scatter_wait("l0", fly0[0], fly0[1], fly0[2], dh)), (0, late_g, last_parts, last_slots)]
    g_big = {}
    for l, names, parts, slots in groups:
        sums = [_sum_reduced(pt, sl, shard_arr, c_arr, l, g_big.get(n)) for n, pt, sl in zip(names, parts, slots)]
        g_big.update(zip(names, _join_halves(sums, l)))
    grad_x = dh
    grads = {n: jnp.stack([per_layer[l][n] for l in range(DEPTH)], axis=0) for n in SMALL if n != 'norm_final'}
    grads['norm_final'] = dnf.reshape(-1)

    rep = [n for n in SMALL if n not in CONVS]
    names = rep + CONVS
    vec = _pack([grads[n] for n in names] + [loss[0, :1]])
    total = _sum_slots("small_sum", _exchange_small(vec))
    parts_small = _unpack(total, [grads[n].shape for n in names] + [(1,)])
    g_small = dict(zip(names, parts_small[:-1]))
    loss_out = parts_small[-1].reshape(())
    for n in CONVS:
        width = w[n].shape[-1]
        g_small[n] = lax.dynamic_slice_in_dim(g_small[n], shard * width, width, axis=2)

    delta, new_m, new_v, grad_w = {}, {}, {}, {}
    for n in BIG:
        delta[n], new_m[n], new_v[n] = _adamw(w[n], g_big[n], m[n], v[n])
        grad_w[n] = g_big[n]
    shapes = [w[n].shape for n in SMALL]
    d_, m_, v_ = _adamw(_pack([w[n] for n in SMALL]), _pack([g_small[n] for n in SMALL]),
                        _pack([m[n] for n in SMALL]), _pack([v[n] for n in SMALL]))
    for n, dd, mm, vv in zip(SMALL, _unpack(d_, shapes), _unpack(m_, shapes), _unpack(v_, shapes)):
        delta[n], new_m[n], new_v[n], grad_w[n] = dd, mm, vv, g_small[n]

    return (loss_out, grad_x[None], *[grad_w[n] for n in WEIGHTS], *[delta[n] for n in WEIGHTS],
            *[new_m[n] for n in WEIGHTS], *[new_v[n] for n in WEIGHTS])
```

```python
import functools
import math

import jax
import jax.numpy as jnp
from jax import lax
from jax.experimental import pallas as pl
from jax.experimental.pallas import tpu as pltpu

F32 = jnp.float32
BF16 = jnp.bfloat16
HI = lax.Precision.HIGH
MESH = pl.DeviceIdType.MESH

LANES = 128
EPS = 1e-6
SG_HEADS, SG_CHUNK = 4, 128
SC_GROUPS, SC_KERNEL = 4, 3
GDN_HEADS, GDN_CONV = 8, 4
GDN_L = 128
GDN_CPS = 8
GDN_HPS = 2
SG_W = SG_HEADS * LANES
SC_W = SC_GROUPS * LANES
GDN_W = GDN_HEADS * LANES
IN_COLS = 2 * SG_W + 3 * SC_W + 4 * GDN_W + 2 * GDN_HEADS
PROJ_W = 7168
CB_U, CB_V = 0, 4
CB_GB, CB_GC, CB_X = 8, 12, 16
CB_Q, CB_K, CB_VV, CB_Z, CB_AB = 20, 28, 36, 44, 52
N_SHARD = 4
N_DEV = 8
DEPTH = 2

ADAM_LR, ADAM_B1, ADAM_B2, ADAM_EPS, ADAM_WD, ADAM_STEP = 0.001, 0.9, 0.999, 1e-08, 0.01, 10

VMEM_LIMIT = 56 << 20

WEIGHTS = ['norm_mix', 'w_in', 'sg_ln_g', 'sg_ln_b', 'sg_w', 'sg_b', 'sc_conv', 'gdn_conv', 'gdn_a_log',
           'gdn_dt_bias', 'gdn_norm', 'out_norm_a', 'out_norm_b', 'w_o', 'norm_ffn', 'w_ff1', 'w_ff2', 'norm_ple',
           'w_ple_gate', 'w_ple_proj', 'norm_final']
BIG = ['w_in', 'w_o', 'w_ff1', 'w_ff2', 'w_ple_gate', 'w_ple_proj']
CONVS = ['sc_conv', 'gdn_conv']
SMALL = [n for n in WEIGHTS if n not in BIG]


def _cparams(sem=None):
    return pltpu.CompilerParams(dimension_semantics=sem, vmem_limit_bytes=VMEM_LIMIT)


def _dot(a, b, dims, prec=None):
    return lax.dot_general(a, b, (dims, ((), ())), precision=prec, preferred_element_type=F32)


def _mm(a, b):
    return _dot(a, b, ((1,), (0,)), HI)


def _mm_nt(a, b):
    return _dot(a, b, ((1,), (1,)), HI)


def _mm_tn(a, b):
    return _dot(a, b, ((0,), (0,)), HI)


@jax.custom_vjp
def _bmm(a, b):
    return _dot(a.astype(BF16), b.astype(BF16), ((1,), (0,)))


def _bmm_fwd(a, b):
    return _bmm(a, b), (a, b)


def _bmm_bwd(res, g):
    a, b = res
    gb = g.astype(BF16)
    return _dot(gb, b.astype(BF16), ((1,), (1,))), _dot(a.astype(BF16), gb, ((0,), (0,)))


_bmm.defvjp(_bmm_fwd, _bmm_bwd)


@jax.custom_vjp
def _bmm_tn(a, b):
    return _dot(a.astype(BF16), b.astype(BF16), ((0,), (0,)))


def _bmm_tn_fwd(a, b):
    return _bmm_tn(a, b), (a, b)


def _bmm_tn_bwd(res, g):
    a, b = res
    gb = g.astype(BF16)
    return _dot(b.astype(BF16), gb, ((1,), (1,))), _dot(a.astype(BF16), gb, ((1,), (0,)))


_bmm_tn.defvjp(_bmm_tn_fwd, _bmm_tn_bwd)


def _sigmoid(x):
    return 1.0 / (1.0 + jnp.exp(-x))


def _silu(x):
    return x * _sigmoid(x)


def _gelu(x):
    c = math.sqrt(2.0 / math.pi)
    return 0.5 * x * (1.0 + jnp.tanh(c * (x + 0.044715 * (x * x * x))))


def _softplus(x):
    return jnp.maximum(x, 0.0) + jnp.log(1.0 + jnp.exp(-jnp.abs(x)))


def _rms(x, g):
    return x * lax.rsqrt(jnp.mean(x * x, axis=-1, keepdims=True) + EPS) * g


def _roll_rows(x, shift):
    return pltpu.roll(x, shift % x.shape[0], 0)


@functools.partial(jax.custom_vjp, nondiff_argnums=(1,))
def _shift_down(x, j):
    row = lax.broadcasted_iota(jnp.int32, x.shape, 0)
    return jnp.where(row >= j, _roll_rows(x, j), 0.0)


def _shift_down_fwd(x, j):
    return _shift_down(x, j), None


def _shift_down_bwd(j, _, dy):
    row = lax.broadcasted_iota(jnp.int32, dy.shape, 0)
    return (jnp.where(row < dy.shape[0] - j, _roll_rows(dy, -j), 0.0),)


_shift_down.defvjp(_shift_down_fwd, _shift_down_bwd)


def _causal_conv(x, taps):
    k = len(taps)
    y = taps[k - 1] * x
    for j in range(k - 1):
        y = y + taps[j] * _shift_down(x, k - 1 - j)
    return y


class _In:
    def __init__(self, arr, block, imap, shared=False, gshape=None, gmap=None):
        self.arr, self.block, self.imap, self.shared = arr, block, imap, shared
        self.gshape = arr.shape if gshape is None else gshape
        self.gmap = imap if gmap is None else gmap


def _fused(name, fn, n, ins, outs, douts=None, need=None, gdt=None, add=None):
    n_in, n_out = len(ins), len(outs)
    in_specs = [pl.BlockSpec(s.block, s.imap) for s in ins]
    out_specs = [pl.BlockSpec(bs, im) for _, _, bs, im in outs]
    if douts is None:
        def body(*refs):
            res = fn(*[r[...] for r in refs[:n_in]])
            for r, v in zip(refs[n_in:], res):
                r[...] = v.astype(r.dtype)

        return pl.pallas_call(
            body, name=name, grid=(n,), in_specs=in_specs, out_specs=out_specs,
            out_shape=[jax.ShapeDtypeStruct(s, d) for s, d, _, _ in outs],
            compiler_params=_cparams(("arbitrary",)),
        )(*[s.arr for s in ins])

    gdt = list(gdt) if gdt is not None else [F32] * n_in
    add = dict(add or {})
    gidx = [i for i in range(n_in) if need[i]]
    aidx = [i for i in gidx if i in add]

    def body(*refs):
        in_refs, d_refs = refs[:n_in], refs[n_in:n_in + n_out]
        a_refs = dict(zip(aidx, refs[n_in + n_out:n_in + n_out + len(aidx)]))
        g_refs = refs[n_in + n_out + len(aidx):]
        vals = [r[...] for r in in_refs]

        def f(*dv):
            full = list(vals)
            for i, v in zip(gidx, dv):
                full[i] = v
            return tuple(o.astype(F32) for o in fn(*full))

        _, vjp = jax.vjp(f, *[vals[i].astype(F32) for i in gidx])
        grads = vjp(tuple(r[...].astype(F32) for r in d_refs))
        for i, g_ref, g in zip(gidx, g_refs, grads):
            if i in a_refs:
                g = g + a_refs[i][...].astype(F32)
            if ins[i].shared:
                period = n if ins[i].shared is True else ins[i].shared

                @pl.when(pl.program_id(0) % period == 0)
                def _():
                    g_ref[...] = jnp.zeros_like(g_ref)
                g_ref[...] += g.astype(g_ref.dtype)
            else:
                g_ref[...] = g.astype(g_ref.dtype)

    g_specs = [pl.BlockSpec(ins[i].block, ins[i].gmap) for i in gidx]
    g_shape = [jax.ShapeDtypeStruct(ins[i].gshape, gdt[i]) for i in gidx]
    res = pl.pallas_call(
        body, name=name, grid=(n,), in_specs=in_specs + out_specs + [g_specs[gidx.index(i)] for i in aidx],
        out_specs=g_specs, out_shape=g_shape,
        compiler_params=_cparams(("arbitrary",)),
    )(*[s.arr for s in ins], *douts, *[add[i] for i in aidx])
    full = [None] * n_in
    for i, g in zip(gidx, res):
        full[i] = g
    return full


def _row_in(a, tm, cb=None):
    if cb is None:
        return _In(a, (tm, a.shape[1]), lambda i: (i, 0))
    return _In(a, (tm, LANES), lambda i: (i, cb), gshape=(a.shape[0], LANES), gmap=lambda i: (i, 0))


def _row_shared(a):
    return _In(a, a.shape, lambda i: (0, 0), shared=True)


def _row_out(t, w, dt, tm):
    return ((t, w), dt, (tm, w), lambda i: (i, 0))


def _col_in(a, base, nblk):
    t = a.shape[0]
    return _In(a, (t, LANES), lambda i: (0, base + i), gshape=(t, nblk * LANES), gmap=lambda i: (0, i))


def _col_par(a):
    return _In(a, (None,) + a.shape[1:], lambda i: (i, 0, 0))


def _col_out(t, w, dt, base=0):
    return ((t, w), dt, (t, LANES), lambda i: (0, base + i))


def _matmul(name, a, b, mode, m, n, k, tm, tn, tk, b_spec=None, epilogue=None, extras=(), outs=None, out_dtype=F32,
            n_outer=False):
    tm, tn, tk = min(tm, m), min(tn, n), min(tk, k)
    assert m % tm == 0 and n % tn == 0 and k % tk == 0, (name, m, n, k, tm, tn, tk)
    nk = k // tk
    a_spec = (pl.BlockSpec((tk, tm), lambda i, j, kk: (kk, i)) if mode == "tn"
              else pl.BlockSpec((tm, tk), lambda i, j, kk: (i, kk)))
    if b_spec is None:
        b_spec = (pl.BlockSpec((tn, tk), lambda i, j, kk: (j, kk)) if mode == "nt"
                  else pl.BlockSpec((tk, tn), lambda i, j, kk: (kk, j)))
    dims = {"nn": ((1,), (0,)), "nt": ((1,), (1,)), "tn": ((0,), (0,))}[mode]
    if outs is None:
        outs = [((m, n), out_dtype, (tm, tn), lambda i, j, kk: (i, j))]
    if epilogue is None:
        epilogue = lambda acc: (acc,)
    n_ex = len(extras)

    def body(*refs):
        a_ref, b_ref = refs[0], refs[1]
        ex_refs = refs[2:2 + n_ex]
        o_refs = refs[2 + n_ex:2 + n_ex + len(outs)]
        part = _dot(a_ref[...].astype(BF16), b_ref[...].astype(BF16), dims)

        def finish(acc):
            for r, v in zip(o_refs, epilogue(acc, *[e[...] for e in ex_refs])):
                r[...] = v.astype(r.dtype)

        if nk == 1:
            finish(part)
        else:
            acc_ref = refs[-1]
            kk = pl.program_id(2)

            @pl.when(kk == 0)
            def _():
                acc_ref[...] = part

            @pl.when(kk > 0)
            def _():
                acc_ref[...] += part

            @pl.when(kk == nk - 1)
            def _():
                finish(acc_ref[...])

    ex_specs = [pl.BlockSpec((tm, tn), lambda i, j, kk: (i, j)) for _ in extras]
    out_specs = [pl.BlockSpec(bs, im) for _, _, bs, im in outs]
    in_specs = [a_spec, b_spec] + ex_specs
    grid = (m // tm, n // tn, nk)
    if n_outer:
        swap = lambda sp: pl.BlockSpec(sp.block_shape, functools.partial(lambda f, j, i, kk: f(i, j, kk), sp.index_map))
        in_specs, out_specs, grid = [swap(sp) for sp in in_specs], [swap(sp) for sp in out_specs], (n // tn, m // tm, nk)
    res = pl.pallas_call(
        body, name=name, grid=grid,
        in_specs=in_specs,
        out_specs=out_specs,
        out_shape=[jax.ShapeDtypeStruct(s, d) for s, d, _, _ in outs],
        scratch_shapes=[pltpu.VMEM((tm, tn), F32)] if nk > 1 else [],
        compiler_params=_cparams(("parallel", "parallel", "arbitrary")),
    )(a, b, *extras)
    return res if len(res) > 1 else res[0]


def _fn_sgu(u_pre, v_pre, ln_g, ln_b, w, bb, na):
    t = u_pre.shape[0]
    u = _gelu(u_pre)
    v = _gelu(v_pre)
    mu = jnp.mean(v, axis=-1, keepdims=True)
    vc = v - mu
    vh = vc * lax.rsqrt(jnp.mean(vc * vc, axis=-1, keepdims=True) + EPS) * ln_g + ln_b
    ri = lax.broadcasted_iota(jnp.int32, w.shape, 0)
    ci = lax.broadcasted_iota(jnp.int32, w.shape, 1)
    wc = jnp.where(ri >= ci, w, 0.0)
    f = jnp.concatenate([_bmm(wc, vh[c * SG_CHUNK:(c + 1) * SG_CHUNK]) + bb for c in range(t // SG_CHUNK)], axis=0)
    return (_rms(u * f, na),)


def _fn_sconv(gb, gc, xin, w0, w1, w2, nb):
    return (_rms(gb * _causal_conv(gc * xin, (w0, w1, w2)), nb),)


def _fn_gdn_qk(pre, w0, w1, w2, w3):
    a = _silu(_causal_conv(pre, (w0, w1, w2, w3)))
    return (a * lax.rsqrt(jnp.sum(a * a, axis=-1, keepdims=True) + EPS),)


def _fn_gdn_v(pre, w0, w1, w2, w3):
    return (_silu(_causal_conv(pre, (w0, w1, w2, w3))),)


def _fn_gdn_gates(ab, a_log, dt_bias):
    lane = lax.broadcasted_iota(jnp.int32, ab.shape, 1)
    g = -jnp.exp(a_log) * _softplus(ab + dt_bias)
    return (jnp.where(lane < GDN_HEADS, g, jnp.where(lane < 2 * GDN_HEADS, _sigmoid(ab), 0.0)),)


def _solve_unit_lower_impl(mats):
    n = mats[0].shape[0]
    ri = lax.broadcasted_iota(jnp.int32, (n, n), 0)
    ci = lax.broadcasted_iota(jnp.int32, (n, n), 1)
    ts = [(ri == ci).astype(F32)] * len(mats)
    sh = 0
    while (1 << sh) < n:
        rb = jnp.right_shift(ri, sh)
        cb = jnp.right_shift(ci, sh)
        off = ((rb & 1) == 1) & (cb == rb - 1)
        us = [_mm(t, jnp.where(off, a, 0.0)) for t, a in zip(ts, mats)]
        ts = [t - _mm(u, t) for t, u in zip(ts, us)]
        sh += 1
    return tuple(ts)


def _solve_cotangents(ts, dts):
    us = [_mm_nt(dt, t) for t, dt in zip(ts, dts)]
    return tuple(-_mm_tn(t, u) for t, u in zip(ts, us))


@jax.custom_vjp
def _solve_unit_lower(mats):
    return _solve_unit_lower_impl(mats)


def _solve_unit_lower_fwd(mats):
    ts = _solve_unit_lower_impl(mats)
    return ts, ts


def _solve_unit_lower_bwd(ts, dts):
    return (_solve_cotangents(ts, dts),)


_solve_unit_lower.defvjp(_solve_unit_lower_fwd, _solve_unit_lower_bwd)


@jax.custom_vjp
def _solved_unit_lower(mats, ts):
    return ts


def _solved_unit_lower_fwd(mats, ts):
    return ts, ts


def _solved_unit_lower_bwd(ts, dts):
    return _solve_cotangents(ts, dts), tuple(jnp.zeros_like(t) for t in ts)


_solved_unit_lower.defvjp(_solved_unit_lower_fwd, _solved_unit_lower_bwd)


def _fn_gdn_wy(q, k, v, gates, pick_g, pick_b, t_saved=None):
    n, dk = GDN_L, q.shape[1]
    rows = [slice(c * n, (c + 1) * n) for c in range(q.shape[0] // n)]
    ri = lax.broadcasted_iota(jnp.int32, (n, n), 0)
    ci = lax.broadcasted_iota(jnp.int32, (n, n), 1)
    incl = ri >= ci
    eye = (ri == ci).astype(F32)
    last = lax.broadcasted_iota(jnp.int32, (n, 1), 0) == (n - 1)
    qs = [q[r] * (dk ** -0.5) for r in rows]
    ks = [k[r] for r in rows]
    gcs, betas, d_incl = [], [], []
    for r in rows:
        g = jnp.sum(gates[r] * pick_g, axis=1, keepdims=True)
        betas.append(jnp.sum(gates[r] * pick_b, axis=1, keepdims=True))
        g_row = jnp.sum(eye * g, axis=0, keepdims=True)
        gc = jnp.sum(jnp.where(incl, g_row, 0.0), axis=1, keepdims=True)
        gc_row = jnp.sum(eye * gc, axis=0, keepdims=True)
        gcs.append(gc)
        d_incl.append(jnp.where(incl, jnp.exp(jnp.where(incl, gc - gc_row, 0.0)), 0.0))
    kbs = [kk * b for kk, b in zip(ks, betas)]
    mats = tuple(_mm_nt(kb, kk) * jnp.where(ri > ci, d, 0.0) for kb, kk, d in zip(kbs, ks, d_incl))
    ts = _solve_unit_lower(mats) if t_saved is None else _solved_unit_lower(mats, tuple(t_saved[r] for r in rows))
    egs = [jnp.exp(gc) for gc in gcs]
    values = [_mm(t, v[r] * b) for t, r, b in zip(ts, rows, betas)]
    kcds = [_mm(t, kb * eg) for t, kb, eg in zip(ts, kbs, egs)]
    intras = [_mm_nt(qq, kk) * d for qq, kk, d in zip(qs, ks, d_incl)]
    g_lasts = [jnp.sum(jnp.where(last, gc, 0.0), axis=0, keepdims=True) for gc in gcs]
    qes = [qq * eg for qq, eg in zip(qs, egs)]
    kts = [kk * jnp.exp(gl - gc) for kk, gl, gc in zip(ks, g_lasts, gcs)]
    carries = [jnp.broadcast_to(jnp.exp(gl), (8, LANES)) for gl in g_lasts]
    cat = lambda parts: jnp.concatenate(parts, axis=0)
    res = (cat(values), cat(kcds), cat(qes), cat(kts), cat(intras), cat(carries))
    return res + (cat(ts),) if t_saved is None else res


def _gdn_steps(states, operands):
    v_new = [value - _bmm(kcd, s) for s, (value, kcd, _, _, _, _) in zip(states, operands)]
    outs = [_bmm(qe, s) + _bmm(intra, vn) for s, vn, (_, _, qe, _, intra, _) in zip(states, v_new, operands)]
    new = [s * carry + _bmm_tn(kt, vn) for s, vn, (_, _, _, kt, _, carry) in zip(states, v_new, operands)]
    return tuple(new), tuple(outs)


def _gdn_post(o, z, nrm):
    return _rms(o, nrm) * _silu(z)


def _gdn_wy_ins(q, k, v, gates, t_saved=None):
    t = q.shape[0]
    rb = min(GDN_CPS * GDN_L, t)
    hd = GDN_HEADS
    lane = jnp.arange(LANES)[None, None, :]
    pick_g = (lane == jnp.arange(hd)[:, None, None]).astype(F32)
    pick_b = (lane == jnp.arange(hd)[:, None, None] + hd).astype(F32)
    blk = lambda a: _In(a, (rb, LANES), lambda i: (i // hd, i % hd))
    par = lambda a: _In(a, (None, 1, LANES), lambda i: (i % hd, 0, 0))
    ins = [blk(q), blk(k), blk(v), _In(gates, (rb, LANES), lambda i: (i // hd, 0), shared=hd), par(pick_g),
           par(pick_b)]
    wide = lambda dt: ((t, GDN_W), dt, (rb, LANES), lambda i: (i // hd, i % hd))
    carry = ((hd, 8 * (t // GDN_L), LANES), F32, (None, 8 * (rb // GDN_L), LANES), lambda i: (i % hd, i // hd, 0))
    outs = [wide(F32), wide(BF16), wide(BF16), wide(BF16), wide(BF16), carry]
    if t_saved is None:
        outs.append(wide(F32))
    else:
        ins.append(blk(t_saved))
    return (t // rb) * hd, ins, outs


def _gdn_scan_specs(t):
    wide = GDN_HPS * LANES
    once = pl.Buffered(1)
    col = lambda base: pl.BlockSpec((t, wide), lambda h: (0, base // GDN_HPS + h), pipeline_mode=once)
    carry = pl.BlockSpec((GDN_HPS, 8 * (t // GDN_L), LANES), lambda h: (h, 0, 0))
    par = pl.BlockSpec((GDN_HPS, 1, LANES), lambda h: (h, 0, 0))
    return col, carry, par


def _head_operands(refs, cy_ref, c, i):
    rows = pl.ds(pl.multiple_of(c * GDN_L, GDN_L), GDN_L)
    lanes = pl.ds(i * LANES, LANES)
    return tuple(r[rows, lanes] for r in refs) + (cy_ref[i, pl.ds(pl.multiple_of(c * 8, 8), 1), :],)


def _gdn_scan_fwd(wy, proj, nrm):
    t = wy[0].shape[0]
    nc = t // GDN_L
    heads = range(GDN_HPS)

    def body(val_ref, kcd_ref, qe_ref, kt_ref, in_ref, cy_ref, z_ref, n_ref, y_ref, o_scr):
        big = (val_ref, kcd_ref, qe_ref, kt_ref, in_ref)

        def step(c, states):
            rows = pl.ds(pl.multiple_of(c * GDN_L, GDN_L), GDN_L)
            states, outs = _gdn_steps(states, [_head_operands(big, cy_ref, c, i) for i in heads])
            for i in heads:
                o_scr[rows, pl.ds(i * LANES, LANES)] = outs[i]
            return states

        lax.fori_loop(0, nc, step, tuple(jnp.zeros((LANES, LANES), F32) for _ in heads))
        for i in heads:
            lanes = pl.ds(i * LANES, LANES)
            y_ref[:, lanes] = _gdn_post(o_scr[:, lanes], z_ref[:, lanes], n_ref[i]).astype(y_ref.dtype)

    col, carry, par = _gdn_scan_specs(t)
    return pl.pallas_call(
        body, name="gdn_scan_fwd", grid=(GDN_HEADS // GDN_HPS,),
        in_specs=[col(0)] * 5 + [carry, col(CB_Z), par],
        out_specs=col(0), out_shape=jax.ShapeDtypeStruct((t, GDN_W), BF16),
        scratch_shapes=[pltpu.VMEM((t, GDN_HPS * LANES), F32)],
        compiler_params=_cparams(("arbitrary",)),
    )(*wy, proj, nrm)


def _gdn_scan_bwd(wy, proj, nrm, dy, dy_base):
    t = wy[0].shape[0]
    nc = t // GDN_L
    heads = range(GDN_HPS)

    def body(val_ref, kcd_ref, qe_ref, kt_ref, in_ref, cy_ref, z_ref, n_ref, dy_ref,
             dval_ref, dkcd_ref, dqe_ref, dkt_ref, din_ref, dcy_ref, dz_ref, dn_ref, o_scr, s_scr):
        big = (val_ref, kcd_ref, qe_ref, kt_ref, in_ref)
        dbig = (dval_ref, dkcd_ref, dqe_ref, dkt_ref, din_ref)

        def step(c, states):
            rows = pl.ds(pl.multiple_of(c * GDN_L, GDN_L), GDN_L)
            for i in heads:
                s_scr[i, c] = states[i]
            states, outs = _gdn_steps(states, [_head_operands(big, cy_ref, c, i) for i in heads])
            for i in heads:
                o_scr[rows, pl.ds(i * LANES, LANES)] = outs[i]
            return states

        zeros = tuple(jnp.zeros((LANES, LANES), F32) for _ in heads)
        lax.fori_loop(0, nc, step, zeros)
        for i in heads:
            lanes = pl.ds(i * LANES, LANES)
            _, vjp_post = jax.vjp(_gdn_post, o_scr[:, lanes], z_ref[:, lanes], n_ref[i])
            do, dz, dn = vjp_post(dy_ref[:, lanes].astype(F32))
            dz_ref[:, lanes] = dz.astype(dz_ref.dtype)
            dn_ref[i] = dn
            o_scr[:, lanes] = do
        dcy_ref[...] = jnp.zeros_like(dcy_ref)

        def rstep(k, dstates):
            c = nc - 1 - k
            rows = pl.ds(pl.multiple_of(c * GDN_L, GDN_L), GDN_L)
            ops = [tuple(o.astype(F32) for o in _head_operands(big, cy_ref, c, i)) for i in heads]
            _, vjp_c = jax.vjp(_gdn_steps, tuple(s_scr[i, c] for i in heads), ops)
            dstates, dops = vjp_c((dstates, tuple(o_scr[rows, pl.ds(i * LANES, LANES)] for i in heads)))
            for i in heads:
                for r, g in zip(dbig, dops[i][:5]):
                    r[rows, pl.ds(i * LANES, LANES)] = g.astype(r.dtype)
                dcy_ref[i, pl.ds(pl.multiple_of(c * 8, 8), 1), :] = dops[i][5]
            return dstates

        lax.fori_loop(0, nc, rstep, zeros)

    col, carry, par = _gdn_scan_specs(t)
    wide = jax.ShapeDtypeStruct((t, GDN_W), BF16)
    return pl.pallas_call(
        body, name="gdn_scan_bwd", grid=(GDN_HEADS // GDN_HPS,),
        in_specs=[col(0)] * 5 + [carry, col(CB_Z), par, col(dy_base)],
        out_specs=[col(0)] * 5 + [carry, col(0), par],
        out_shape=[wide] * 5 + [jax.ShapeDtypeStruct(wy[5].shape, F32), wide,
                                jax.ShapeDtypeStruct((GDN_HEADS, 1, LANES), F32)],
        scratch_shapes=[pltpu.VMEM((t, GDN_HPS * LANES), F32), pltpu.VMEM((GDN_HPS, nc, LANES, LANES), F32)],
        compiler_params=_cparams(("arbitrary",)),
    )(*wy, proj, nrm, dy)


TM = 512
TR = 256


def _rms_fwd(name, h, g):
    t, d = h.shape
    tm = min(TR, t)
    return _fused(name, lambda hb, gb: (_rms(hb, gb),), t // tm, [_row_in(h, tm), _row_shared(g)],
                  [_row_out(t, d, BF16, tm)])[0]


def _rms_bwd(name, h, g, dxn, dh_next):
    t, d = h.shape
    tm = min(TR, t)
    dh, dg = _fused(name, lambda hb, gb: (_rms(hb, gb),), t // tm, [_row_in(h, tm), _row_shared(g)],
                    [_row_out(t, d, F32, tm)], douts=[dxn], need=[True, True], add={0: dh_next})
    return dh, dg


def _mixer_ins(proj, lw):
    sgu = [_col_in(proj, CB_U, SG_HEADS), _col_in(proj, CB_V, SG_HEADS), _col_par(lw['sg_ln_g']),
           _col_par(lw['sg_ln_b']), _col_par(lw['sg_w']), _col_par(lw['sg_bb']), _col_par(lw['out_norm_a'])]
    sconv = [_col_in(proj, CB_GB, SC_GROUPS), _col_in(proj, CB_GC, SC_GROUPS), _col_in(proj, CB_X, SC_GROUPS)] + \
            [_col_par(w) for w in lw['sc_taps']] + [_col_par(lw['out_norm_b'])]
    gq = [_col_in(proj, CB_Q, GDN_HEADS)] + [_col_par(w) for w in lw['q_taps']]
    gk = [_col_in(proj, CB_K, GDN_HEADS)] + [_col_par(w) for w in lw['k_taps']]
    gv = [_col_in(proj, CB_VV, GDN_HEADS)] + [_col_par(w) for w in lw['v_taps']]
    return sgu, sconv, gq, gk, gv


def _gates_ins(proj, lw, tm):
    return [_row_in(proj, tm, CB_AB), _row_shared(lw['a_log_row']), _row_shared(lw['dt_bias_row'])]


def _layer_fwd(h, p_l, lw, late=None):
    t, d = h.shape
    xn = _rms_fwd("rms_fwd", h, lw['norm_mix'])
    proj = _matmul("proj_fwd", xn, lw['w_in'], "nn", t, PROJ_W, d, TM, 1024, d)
    sgu, sconv, gq, gk, gv = _mixer_ins(proj, lw)
    ya = _fused("sgu_fwd", _fn_sgu, SG_HEADS, sgu, [_col_out(t, SG_W, BF16)])[0]
    yb = _fused("sconv_fwd", _fn_sconv, SC_GROUPS, sconv, [_col_out(t, SC_W, BF16)])[0]
    q = _fused("gdn_q_fwd", _fn_gdn_qk, GDN_HEADS, gq, [_col_out(t, GDN_W, F32)])[0]
    k = _fused("gdn_k_fwd", _fn_gdn_qk, GDN_HEADS, gk, [_col_out(t, GDN_W, F32)])[0]
    v = _fused("gdn_v_fwd", _fn_gdn_v, GDN_HEADS, gv, [_col_out(t, GDN_W, F32)])[0]
    tm = min(TR, t)
    gates = _fused("gdn_gates_fwd", _fn_gdn_gates, t // tm, _gates_ins(proj, lw, tm),
                   [_row_out(t, LANES, F32, tm)])[0]
    n_wy, wy_ins, wy_outs = _gdn_wy_ins(q, k, v, gates)
    wy = _fused("gdn_wy_fwd", _fn_gdn_wy, n_wy, wy_ins, wy_outs)
    wy, wy_t = wy[:6], wy[6]
    yc = _gdn_scan_fwd(wy, proj, lw['gdn_norm'])
    ycat = jnp.concatenate([ya, yb, yc], axis=1)
    if late is not None:
        late(ycat)
    dff = lw['w_ff2'].shape[0]
    h2 = _matmul("wo_fwd", ycat, lw['w_o'], "nn", t, d, d, TM, 1024, d,
                 epilogue=lambda acc, hb: (hb + acc,), extras=(h,))
    hn = _rms_fwd("rms_fwd", h2, lw['norm_ffn'])
    per = d // 1024
    s, r = _matmul("ff1_fwd", hn, lw['w_ff1'], "nn", t, dff, d, TM, 1024, d,
                   b_spec=pl.BlockSpec((None, d, 1024), lambda i, j, kk: (j // per, 0, j % per)),
                   epilogue=lambda acc: (jnp.maximum(acc, 0.0), jnp.square(jnp.maximum(acc, 0.0))),
                   outs=[((t, dff), BF16, (min(TM, t), 1024), lambda i, j, kk: (i, j))] * 2)
    h3 = _matmul("ff2_fwd", r, lw['w_ff2'], "nn", t, d, dff, TM, 1024, 4096,
                 epilogue=lambda acc, hb: (hb + acc,), extras=(h2,))
    hn2 = _rms_fwd("rms_fwd", h3, lw['norm_ple'])
    pp = _matmul("ple_proj_fwd", p_l, lw['w_ple_proj'], "nn", t, d, p_l.shape[1], TM, 1024, p_l.shape[1])

    def gate_epilogue(acc, hb, ppb):
        sg = _sigmoid(acc)
        return hb + ppb * sg, sg

    h4, gate = _matmul("ple_gate_fwd", hn2, lw['w_ple_gate'], "nn", t, d, d, TM, 1024, d, epilogue=gate_epilogue,
                       extras=(h3, pp), outs=[((t, d), F32, (min(TM, t), 1024), lambda i, j, kk: (i, j))] * 2)
    saved = dict(h=h, xn=xn, proj=proj, q=q, k=k, v=v, gates=gates, ycat=ycat, h2=h2, hn=hn, s=s, r=r, h3=h3,
                 hn2=hn2, pp=pp, gate=gate, p=p_l, wy=wy, wy_t=wy_t)
    return h4, saved


def _layer_bwd(dh4, sv, lw, early=None):
    t, d = dh4.shape
    dff = lw['w_ff2'].shape[0]
    tm = min(TR, t)
    g = {}
    dacc, dpp = _fused("ple_bwd_gate", lambda dh, pp, gt: (dh * pp * gt * (1.0 - gt), dh * gt), t // tm,
                       [_row_in(dh4, tm), _row_in(sv['pp'], tm), _row_in(sv['gate'], tm)],
                       [_row_out(t, d, BF16, tm)] * 2)
    g['w_ple_gate'] = _matmul("dw_ple_gate", sv['hn2'], dacc, "tn", d, d, t, TM, 1024, t, out_dtype=BF16,
                              n_outer=True)
    g['w_ple_proj'] = _matmul("dw_ple_proj", sv['p'], dpp, "tn", sv['p'].shape[1], d, t, TM, 1024, t,
                              out_dtype=BF16, n_outer=True)
    dhn2 = _matmul("dx_ple_gate", dacc, lw['w_ple_gate'], "nt", t, d, d, TM, 1024, d)
    dh3, g['norm_ple'] = _rms_bwd("rms_bwd", sv['h3'], lw['norm_ple'], dhn2, dh4)
    da = _matmul("dx_ff2", dh3, lw['w_ff2'], "nt", t, dff, d, TM, 1024, d,
                 epilogue=lambda acc, sb: (acc * (2.0 * sb.astype(F32)),), extras=(sv['s'],),
                 outs=[((t, dff), BF16, (min(TM, t), 1024), lambda i, j, kk: (i, j))])
    g['w_ff2'] = _matmul("dw_ff2", sv['r'], dh3, "tn", dff, d, t, TM, 512, t, out_dtype=BF16, n_outer=True)
    per = d // 1024
    dhn = _matmul("dx_ff1", da, lw['w_ff1'], "nt", t, d, dff, TM, 1024, d,
                  b_spec=pl.BlockSpec((None, 1024, d), lambda i, j, kk: (kk, j, 0)))
    g['w_ff1'] = _matmul("dw_ff1", sv['hn'], da, "tn", d, dff, t, TM, 1024, t, n_outer=True,
                         outs=[((N_SHARD, d, d), BF16, (None, TM, 1024), lambda i, j, kk: (j // per, i, j % per))])
    norm_ffn = lw['norm_ffn'] if early is None else lw['norm_ffn'] + early(g)[0, 0]
    dh2, g['norm_ffn'] = _rms_bwd("rms_bwd", sv['h2'], norm_ffn, dhn, dh3)
    dycat = _matmul("dx_o", dh2, lw['w_o'], "nt", t, d, d, TM, 1024, d)
    g['w_o'] = _matmul("dw_o", sv['ycat'], dh2, "tn", d, d, t, TM, 512, t, out_dtype=BF16, n_outer=True)
    proj = sv['proj']
    sgu, sconv, gq, gk, gv = _mixer_ins(proj, lw)
    bf2 = [BF16, BF16]
    r_ = _fused("sgu_bwd", _fn_sgu, SG_HEADS, sgu, [_col_out(t, d, F32, 0)], douts=[dycat], need=[True] * 7,
                gdt=bf2 + [F32] * 5)
    du, dv_, g['sg_ln_g'], g['sg_ln_b'], g['sg_w'], g['sg_bb'], g['out_norm_a'] = r_
    r_ = _fused("sconv_bwd", _fn_sconv, SC_GROUPS, sconv, [_col_out(t, d, F32, SG_HEADS)], douts=[dycat],
                need=[True] * 7, gdt=[BF16] * 3 + [F32] * 4)
    dgb, dgc, dxin = r_[:3]
    g['sc_taps'], g['out_norm_b'] = r_[3:6], r_[6]
    r_ = _gdn_scan_bwd(sv['wy'], proj, lw['gdn_norm'], dycat, SG_HEADS + SC_GROUPS)
    dwy, dz, g['gdn_norm'] = r_[:6], r_[6], r_[7]
    n_wy, wy_ins, wy_outs = _gdn_wy_ins(sv['q'], sv['k'], sv['v'], sv['gates'], sv['wy_t'])
    dq, dk, dvv, dgates = _fused("gdn_wy_bwd", _fn_gdn_wy, n_wy, wy_ins, wy_outs, douts=dwy,
                                 need=[True] * 4 + [False] * 3)[:4]
    one = [_col_out(t, GDN_W, F32)]
    r_ = _fused("gdn_q_bwd", _fn_gdn_qk, GDN_HEADS, gq, one, douts=[dq], need=[True] * 5, gdt=[BF16] + [F32] * 4)
    dpq, g['q_taps'] = r_[0], r_[1:]
    r_ = _fused("gdn_k_bwd", _fn_gdn_qk, GDN_HEADS, gk, one, douts=[dk], need=[True] * 5, gdt=[BF16] + [F32] * 4)
    dpk, g['k_taps'] = r_[0], r_[1:]
    r_ = _fused("gdn_v_bwd", _fn_gdn_v, GDN_HEADS, gv, one, douts=[dvv], need=[True] * 5, gdt=[BF16] + [F32] * 4)
    dpv, g['v_taps'] = r_[0], r_[1:]
    dab, g['a_log_row'], g['dt_bias_row'] = _fused(
        "gdn_gates_bwd", _fn_gdn_gates, t // tm, _gates_ins(proj, lw, tm), [_row_out(t, LANES, F32, tm)],
        douts=[dgates], need=[True] * 3, gdt=[BF16, F32, F32])
    pad = jnp.zeros((t, PROJ_W - (CB_AB + 1) * LANES), BF16)
    dproj = jnp.concatenate([du, dv_, dgb, dgc, dxin, dpq, dpk, dpv, dz, dab, pad], axis=1)
    dxn = _matmul("dx_in", dproj, lw['w_in'], "nt", t, d, PROJ_W, TM, 1024, PROJ_W // 2)
    g['w_in'] = _matmul("dw_in", sv['xn'], dproj, "tn", d, PROJ_W, t, TM, 1024, t, out_dtype=BF16, n_outer=True)
    dh, g['norm_mix'] = _rms_bwd("rms_bwd", sv['h'], lw['norm_mix'], dxn, dh2)
    return dh, g


def _loss_grad(h, g, tgt):
    t, d = h.shape
    tm = min(TR, t)

    def body(h_ref, g_ref, t_ref, loss_ref, dh_ref, dg_ref):
        y, vjp = jax.vjp(_rms, h_ref[...], g_ref[...])
        e = y - t_ref[...]
        dh, dg = vjp(e * (1.0 / d))

        @pl.when(pl.program_id(0) == 0)
        def _():
            loss_ref[...] = jnp.zeros_like(loss_ref)
            dg_ref[...] = jnp.zeros_like(dg_ref)

        loss_ref[...] += jnp.sum(jnp.sum(e * e, axis=1, keepdims=True), axis=0, keepdims=True) * (0.5 / d)
        dh_ref[...] = dh
        dg_ref[...] += dg

    row = pl.BlockSpec((tm, d), lambda i: (i, 0))
    return pl.pallas_call(
        body, name="loss_grad", grid=(t // tm,),
        in_specs=[row, pl.BlockSpec((1, d), lambda i: (0, 0)), row],
        out_specs=[pl.BlockSpec((1, LANES), lambda i: (0, 0)), row, pl.BlockSpec((1, d), lambda i: (0, 0))],
        out_shape=[jax.ShapeDtypeStruct((1, LANES), F32), jax.ShapeDtypeStruct((t, d), F32),
                   jax.ShapeDtypeStruct((1, d), F32)],
        compiler_params=_cparams(("arbitrary",)),
    )(h, g, tgt)


def _big_weights(full):
    lw = {}
    if 'w_in' in full:
        d = full['w_in'].shape[1]
        w_in = jnp.transpose(full['w_in'], (1, 0, 2)).reshape(d, IN_COLS)
        lw['w_in'] = jnp.pad(w_in, ((0, 0), (0, PROJ_W - IN_COLS)))
    if 'w_o' in full:
        lw['w_o'] = full['w_o'].reshape(-1, full['w_o'].shape[-1])
    if 'w_ff1' in full:
        lw['w_ff1'] = full['w_ff1']
    if 'w_ff2' in full:
        lw['w_ff2'] = full['w_ff2'].reshape(-1, full['w_ff2'].shape[-1])
    if 'w_ple_gate' in full:
        lw['w_ple_gate'] = full['w_ple_gate'].reshape(-1, full['w_ple_gate'].shape[-1])
    if 'w_ple_proj' in full:
        wpp = full['w_ple_proj']
        lw['w_ple_proj'] = jnp.transpose(wpp, (1, 0, 2)).reshape(wpp.shape[1], -1)
    return lw


def _layer_weights(l, full, convs, small):
    d = small['norm_mix'].shape[-1]
    lw = _big_weights(full)
    for n in ('norm_mix', 'norm_ffn', 'norm_ple'):
        lw[n] = small[n][l].reshape(1, d)
    lw['sg_ln_g'] = small['sg_ln_g'][l].reshape(SG_HEADS, 1, LANES)
    lw['sg_ln_b'] = small['sg_ln_b'][l].reshape(SG_HEADS, 1, LANES)
    lw['sg_w'] = small['sg_w'][l]
    lw['sg_bb'] = jnp.broadcast_to(small['sg_b'][l][:, :, None], (SG_HEADS, SG_CHUNK, LANES))
    lw['out_norm_a'] = small['out_norm_a'][l].reshape(SG_HEADS, 1, LANES)
    lw['out_norm_b'] = small['out_norm_b'][l].reshape(SC_GROUPS, 1, LANES)
    lw['gdn_norm'] = jnp.broadcast_to(small['gdn_norm'][l].reshape(1, 1, LANES), (GDN_HEADS, 1, LANES))
    lw['a_log_row'] = jnp.pad(small['gdn_a_log'][l].reshape(1, GDN_HEADS), ((0, 0), (0, LANES - GDN_HEADS)))
    lw['dt_bias_row'] = jnp.pad(small['gdn_dt_bias'][l].reshape(1, GDN_HEADS), ((0, 0), (0, LANES - GDN_HEADS)))
    sc = convs['sc_conv'][l]
    lw['sc_taps'] = [sc[:, j:j + 1, :] for j in range(SC_KERNEL)]
    gc = jnp.transpose(convs['gdn_conv'][l], (1, 0, 2)).reshape(GDN_CONV, 3 * GDN_W)
    for i, nm in enumerate(('q_taps', 'k_taps', 'v_taps')):
        part = gc[:, i * GDN_W:(i + 1) * GDN_W].reshape(GDN_CONV, GDN_HEADS, 1, LANES)
        lw[nm] = [part[j] for j in range(GDN_CONV)]
    return lw


def _shard_major(name, g):
    if name == 'w_in':
        d = g.shape[0]
        return jnp.transpose(g[:, :IN_COLS].reshape(d, N_SHARD, IN_COLS // N_SHARD), (1, 0, 2))
    if name == 'w_ple_proj':
        return jnp.transpose(g.reshape(g.shape[0], N_SHARD, g.shape[1] // N_SHARD), (1, 0, 2))
    if name == 'w_ff1':
        return g
    return g.reshape(N_SHARD, -1, g.shape[-1])


def _weight_grads(g):
    d = g['norm_mix'].shape[-1]
    out = {}
    for n in ('norm_mix', 'norm_ffn', 'norm_ple'):
        out[n] = g[n].reshape(d)
    out['sg_ln_g'] = g['sg_ln_g'].reshape(SG_W)
    out['sg_ln_b'] = g['sg_ln_b'].reshape(SG_W)
    out['sg_w'] = g['sg_w']
    out['sg_b'] = jnp.sum(g['sg_bb'], axis=2)
    out['out_norm_a'] = g['out_norm_a'].reshape(SG_W)
    out['out_norm_b'] = g['out_norm_b'].reshape(SC_W)
    out['gdn_norm'] = jnp.sum(g['gdn_norm'], axis=(0, 1))
    out['gdn_a_log'] = g['a_log_row'][0, :GDN_HEADS]
    out['gdn_dt_bias'] = g['dt_bias_row'][0, :GDN_HEADS]
    out['sc_conv'] = jnp.concatenate([w.reshape(1, SC_W) for w in g['sc_taps']], axis=0)
    taps = [jnp.concatenate([g[nm][j].reshape(1, GDN_W) for nm in ('q_taps', 'k_taps', 'v_taps')], axis=1)
            for j in range(GDN_CONV)]
    out['gdn_conv'] = jnp.concatenate(taps, axis=0)
    return out


ANY = pl.BlockSpec(memory_space=pl.ANY)


def _place():
    x, y, c = lax.axis_index("x"), lax.axis_index("y"), lax.axis_index("c")
    chips = [(1 - x, y), (x, 1 - y), (1 - x, 1 - y)]
    return x, y, c, chips


def _place_shard(wsh, shard, dtype, layer=None):
    dp, r, cc = wsh.shape
    tr = min(TR, r)
    nb = r // tr

    def body(idx_ref, w_ref, o_ref):
        o_ref[...] = w_ref[...].astype(o_ref.dtype)

    if layer is None:
        grid, shape = (dp * nb,), (dp, N_SHARD, r, cc)
        in_spec = pl.BlockSpec((None, tr, cc), lambda i, ix: (i // nb, i % nb, 0))
        out_spec = pl.BlockSpec((None, None, tr, cc), lambda i, ix: (i // nb, ix[0], i % nb, 0))
    else:
        grid, shape = (nb,), (N_SHARD, r, cc)
        in_spec = pl.BlockSpec((None, tr, cc), lambda i, ix: (layer, i, 0))
        out_spec = pl.BlockSpec((None, tr, cc), lambda i, ix: (ix[0], i, 0))
    return pl.pallas_call(
        body, name="ag_place_shard",
        grid_spec=pltpu.PrefetchScalarGridSpec(num_scalar_prefetch=1, grid=grid, in_specs=[in_spec],
                                               out_specs=out_spec),
        out_shape=jax.ShapeDtypeStruct(shape, dtype),
        compiler_params=_cparams(("arbitrary",)),
    )(shard, wsh)


def _rows(ref, slab, half):
    rh = ref.shape[1] // 2
    return ref.at[slab, pl.ds(pl.multiple_of(half * rh, rh), rh), :]


def _gather_direct(bufs):
    n = len(bufs)

    def body(*refs):
        outs = refs[n:2 * n]
        send_sems, recv_sems = refs[2 * n:]
        x, y, c, chips = _place()
        me = 2 * x + y

        def cp(w, l, j, shard, to):
            blk = outs[w].at[l, shard]
            return pltpu.make_async_remote_copy(src_ref=blk, dst_ref=blk, send_sem=send_sems.at[w, l, j],
                                                recv_sem=recv_sems.at[w, l, j], device_id=to, device_id_type=MESH)

        sends = [cp(w, l, j, me, (*chip, c)) for w in range(n) for l in range(DEPTH) for j, chip in enumerate(chips)]
        for s in sends:
            s.start()
        for w in range(n):
            for l in range(DEPTH):
                for j, (px, py) in enumerate(chips):
                    cp(w, l, j, 2 * px + py, (px, py, c)).wait_recv()
        for s in sends:
            s.wait_send()

    return pl.pallas_call(
        body, name="ag_small", in_specs=[ANY] * n, out_specs=[ANY] * n,
        out_shape=[jax.ShapeDtypeStruct(b.shape, b.dtype) for b in bufs],
        input_output_aliases={w: w for w in range(n)},
        scratch_shapes=[pltpu.SemaphoreType.DMA((n, DEPTH, 3)), pltpu.SemaphoreType.DMA((n, DEPTH, 3))],
    )(*bufs)


def _gather_halves(bufs):
    n = len(bufs)

    def body(*refs):
        outs = refs[n:2 * n]
        send_sems, recv_sems = refs[2 * n:]
        x, y, c, chips = _place()
        me = 2 * x + y
        sibling = (x, y, 1 - c)

        def cp(w, k, shard, half, to):
            blk = _rows(outs[w], shard, half)
            return pltpu.make_async_remote_copy(src_ref=blk, dst_ref=blk, send_sem=send_sems.at[w, k],
                                                recv_sem=recv_sems.at[w, k], device_id=to, device_id_type=MESH)

        sends = []
        for w in range(n):
            for j, chip in enumerate(chips):
                s = cp(w, j, me, c, (*chip, c))
                s.start()
                sends.append(s)
        for w in range(n):
            for j, (px, py) in enumerate(chips):
                cp(w, j, 2 * px + py, c, sibling).wait_recv()
                s = cp(w, 3 + j, 2 * px + py, c, sibling)
                s.start()
                sends.append(s)
        for w in range(n):
            for j, (px, py) in enumerate(chips):
                cp(w, 3 + j, 2 * px + py, 1 - c, sibling).wait_recv()
        for s in sends:
            s.wait_send()

    return pl.pallas_call(
        body, name="ag_gather_halves",
        in_specs=[ANY] * n, out_specs=[ANY] * n,
        out_shape=[jax.ShapeDtypeStruct(b.shape, b.dtype) for b in bufs],
        input_output_aliases={w: w for w in range(n)},
        scratch_shapes=[pltpu.SemaphoreType.DMA((n, 6)), pltpu.SemaphoreType.DMA((n, 6))],
    )(*bufs)


HBM = pl.BlockSpec(memory_space=pltpu.HBM)
SEM = pl.BlockSpec(memory_space=pltpu.SEMAPHORE)
_SPLIT = pltpu.CompilerParams(has_side_effects=pltpu.SideEffectType.DATAFLOW_SIDE_EFFECTING)


def _in_hbm(arrs):
    return [pltpu.with_memory_space_constraint(a, pltpu.HBM) for a in arrs]


def _gather_start(tag, bufs, after):
    n, na = len(bufs), len(after)
    k = 3 * n

    def body(*refs):
        b_refs, sems, token = refs[:n], refs[n + na:n + na + 2 * k], refs[-1]
        x, y, c, chips = _place()
        me = 2 * x + y
        for w in range(n):
            for j, chip in enumerate(chips):
                blk = _rows(b_refs[w], me, c)
                pltpu.make_async_remote_copy(src_ref=blk, dst_ref=blk, send_sem=sems[3 * w + j],
                                             recv_sem=sems[k + 3 * w + j], device_id=(*chip, c),
                                             device_id_type=MESH).start()
        token[...] = jnp.zeros_like(token)

    res = pl.pallas_call(
        body, name="ag_start_" + tag,
        out_shape=(*[pltpu.SemaphoreType.DMA(())] * (2 * k), *[pltpu.HBM(b.shape, b.dtype) for b in bufs],
                   jax.ShapeDtypeStruct((8, LANES), F32)),
        in_specs=[HBM] * n + [ANY] * na,
        out_specs=(*[SEM] * (2 * k), *[HBM] * n, pl.BlockSpec(memory_space=pltpu.VMEM)),
        input_output_aliases={w: 2 * k + w for w in range(n)}, compiler_params=_SPLIT,
    )(*_in_hbm(bufs), *after)
    return list(res[:2 * k]), list(res[2 * k:2 * k + n]), res[2 * k + n]


def _gather_wait(tag, sems, bufs, after):
    n = len(bufs)
    k = 3 * n

    def body(*refs):
        b_refs, sems = refs[:n], refs[n:n + 2 * k]
        x, y, c, chips = _place()
        me = 2 * x + y
        for w in range(n):
            for j, (px, py) in enumerate(chips):
                mine, theirs = _rows(b_refs[w], me, c), _rows(b_refs[w], 2 * px + py, c)
                cp = pltpu.make_async_remote_copy(src_ref=mine, dst_ref=theirs, send_sem=sems[3 * w + j],
                                                  recv_sem=sems[k + 3 * w + j], device_id=(px, py, c),
                                                  device_id_type=MESH)
                cp.wait_send()
                cp.wait_recv()

    res = pl.pallas_call(
        body, name="ag_wait_" + tag, out_shape=tuple(pltpu.HBM(b.shape, b.dtype) for b in bufs),
        in_specs=[HBM] * n + [SEM] * (2 * k) + [ANY], out_specs=(HBM,) * n,
        input_output_aliases={w: w for w in range(n)}, compiler_params=_SPLIT,
    )(*bufs, *sems, after)
    return list(res)


def _gather_forward(bufs):
    n = len(bufs)

    def body(*refs):
        outs = refs[n:2 * n]
        send_sems, recv_sems = refs[2 * n:]
        x, y, c, chips = _place()

        def cp(w, j, shard, half):
            blk = _rows(outs[w], shard, half)
            return pltpu.make_async_remote_copy(src_ref=blk, dst_ref=blk, send_sem=send_sems.at[w, j],
                                                recv_sem=recv_sems.at[w, j], device_id=(x, y, 1 - c),
                                                device_id_type=MESH)

        sends = [cp(w, j, 2 * px + py, c) for w in range(n) for j, (px, py) in enumerate(chips)]
        for s in sends:
            s.start()
        for w in range(n):
            for j, (px, py) in enumerate(chips):
                cp(w, j, 2 * px + py, 1 - c).wait_recv()
        for s in sends:
            s.wait_send()

    return pl.pallas_call(
        body, name="ag_forward", in_specs=[ANY] * n, out_specs=[ANY] * n,
        out_shape=[jax.ShapeDtypeStruct(b.shape, b.dtype) for b in bufs],
        input_output_aliases={w: w for w in range(n)},
        scratch_shapes=[pltpu.SemaphoreType.DMA((n, 3)), pltpu.SemaphoreType.DMA((n, 3))],
    )(*bufs)


def _scatter_start(tag, parts, after):
    n, na = len(parts), len(after)
    k = 3 * n
    lands = [lax.empty((3,) + g.shape[1:], g.dtype) for g in parts]

    def body(*refs):
        srcs, dsts, sems, token = refs[:n], refs[n:2 * n], refs[2 * n + na:2 * n + na + 2 * k], refs[-1]
        x, y, c, chips = _place()
        for w in range(n):
            for j, (px, py) in enumerate(chips):
                pltpu.make_async_remote_copy(src_ref=srcs[w].at[2 * px + py], dst_ref=dsts[w].at[j],
                                             send_sem=sems[3 * w + j], recv_sem=sems[k + 3 * w + j],
                                             device_id=(px, py, c), device_id_type=MESH).start()
        token[...] = jnp.zeros_like(token)

    res = pl.pallas_call(
        body, name="rs_scatter_start_" + tag,
        out_shape=(*[pltpu.SemaphoreType.DMA(())] * (2 * k), *[pltpu.HBM(a.shape, a.dtype) for a in parts + lands],
                   jax.ShapeDtypeStruct((8, LANES), F32)),
        in_specs=[HBM] * (2 * n) + [ANY] * na,
        out_specs=(*[SEM] * (2 * k), *[HBM] * (2 * n), pl.BlockSpec(memory_space=pltpu.VMEM)),
        input_output_aliases={w: 2 * k + w for w in range(2 * n)}, compiler_params=_SPLIT,
    )(*_in_hbm(parts + lands), *after)
    return list(res[:2 * k]), list(res[2 * k:2 * k + n]), list(res[2 * k + n:2 * k + 2 * n]), res[2 * k + 2 * n]


def _scatter_wait(tag, sems, parts, lands, after):
    n = len(parts)
    k = 3 * n

    def body(*refs):
        srcs, dsts, sems = refs[:n], refs[n:2 * n], refs[2 * n:2 * n + 2 * k]
        x, y, c, chips = _place()
        for w in range(n):
            for j, (px, py) in enumerate(chips):
                cp = pltpu.make_async_remote_copy(src_ref=srcs[w].at[2 * px + py], dst_ref=dsts[w].at[j],
                                                  send_sem=sems[3 * w + j], recv_sem=sems[k + 3 * w + j],
                                                  device_id=(px, py, c), device_id_type=MESH)
                cp.wait_send()
                cp.wait_recv()

    res = pl.pallas_call(
        body, name="rs_scatter_wait_" + tag, out_shape=tuple(pltpu.HBM(a.shape, a.dtype) for a in parts + lands),
        in_specs=[HBM] * (2 * n) + [SEM] * (2 * k) + [ANY], out_specs=(HBM,) * (2 * n),
        input_output_aliases={w: w for w in range(2 * n)}, compiler_params=_SPLIT,
    )(*parts, *lands, *sems, after)
    return list(res[:n]), list(res[n:])


def _swap_halves(grads):
    n = len(grads)

    def body(*refs):
        srcs, outs = refs[:n], refs[n:2 * n]
        send_sems, recv_sems = refs[2 * n:]
        x, y, c, _ = _place()
        cps = []
        for w in range(n):
            rh = srcs[w].shape[1] // 2
            theirs = srcs[w].at[:, pl.ds(pl.multiple_of((1 - c) * rh, rh), rh), :]
            cps.append(pltpu.make_async_remote_copy(src_ref=theirs, dst_ref=outs[w], send_sem=send_sems.at[w],
                                                    recv_sem=recv_sems.at[w], device_id=(x, y, 1 - c),
                                                    device_id_type=MESH))
        for cpy in cps:
            cpy.start()
        for cpy in cps:
            cpy.wait()

    return pl.pallas_call(
        body, name="rs_swap_halves", in_specs=[ANY] * n, out_specs=[ANY] * n,
        out_shape=[jax.ShapeDtypeStruct((g.shape[0], g.shape[1] // 2, g.shape[2]), g.dtype) for g in grads],
        scratch_shapes=[pltpu.SemaphoreType.DMA((n,)), pltpu.SemaphoreType.DMA((n,))],
    )(*grads)


def _scatter_shards(parts):
    n = len(parts)

    def body(*refs):
        srcs, outs = refs[:n], refs[n:2 * n]
        send_sems, recv_sems = refs[2 * n:]
        x, y, c, chips = _place()

        def cp(w, j, src_shard, to):
            return pltpu.make_async_remote_copy(
                src_ref=srcs[w].at[src_shard], dst_ref=outs[w].at[j], send_sem=send_sems.at[w, j],
                recv_sem=recv_sems.at[w, j], device_id=to, device_id_type=MESH)

        sends = [cp(w, j, 2 * px + py, (px, py, c)) for w in range(n) for j, (px, py) in enumerate(chips)]
        for s in sends:
            s.start()
        for s in sends:
            s.wait()

    return pl.pallas_call(
        body, name="rs_scatter", in_specs=[ANY] * n, out_specs=[ANY] * n,
        out_shape=[jax.ShapeDtypeStruct((3,) + g.shape[1:], g.dtype) for g in parts],
        scratch_shapes=[pltpu.SemaphoreType.DMA((n, 3)), pltpu.SemaphoreType.DMA((n, 3))],
    )(*parts)


def _sum_reduced(part, slots, shard, core, layer, prev):
    _, rh, cc = part.shape
    tr = min(TR, rh)
    nbh = rh // tr

    def body(shard_ref, core_ref, p_ref, s0_ref, s1_ref, s2_ref, *rest):
        acc = p_ref[...].astype(F32)
        for s_ref in (s0_ref, s1_ref, s2_ref):
            acc = acc + s_ref[...].astype(F32)
        rest[-1][...] = acc

    slot = lambda j: pl.BlockSpec((None, tr, cc), lambda i, sh, co: (j, i, 0))
    ins = [shard, core, part, slots, slots, slots] + ([] if prev is None else [prev])
    return pl.pallas_call(
        body, name="rs_sum_reduced",
        grid_spec=pltpu.PrefetchScalarGridSpec(
            num_scalar_prefetch=2, grid=(nbh,),
            in_specs=[pl.BlockSpec((None, tr, cc), lambda i, sh, co: (sh[0], i, 0)), slot(0), slot(1), slot(2)]
            + ([] if prev is None else [ANY]),
            out_specs=pl.BlockSpec((None, tr, cc), lambda i, sh, co: (layer, co[0] * nbh + i, 0))),
        out_shape=jax.ShapeDtypeStruct((DEPTH, 2 * rh, cc), F32),
        input_output_aliases={} if prev is None else {6: 0},
        compiler_params=_cparams(("arbitrary",)),
    )(*ins)


def _join_halves(bufs, layer):
    n = len(bufs)

    def body(*refs):
        outs = refs[n:2 * n]
        send_sems, recv_sems = refs[2 * n:]
        x, y, c, _ = _place()

        def cp(w, half):
            blk = _rows(outs[w], layer, half)
            return pltpu.make_async_remote_copy(src_ref=blk, dst_ref=blk, send_sem=send_sems.at[w],
                                                recv_sem=recv_sems.at[w], device_id=(x, y, 1 - c), device_id_type=MESH)

        sends = [cp(w, c) for w in range(n)]
        for s in sends:
            s.start()
        for w in range(n):
            cp(w, 1 - c).wait_recv()
        for s in sends:
            s.wait_send()

    return pl.pallas_call(
        body, name="rs_join_halves", in_specs=[ANY] * n, out_specs=[ANY] * n,
        out_shape=[jax.ShapeDtypeStruct(b.shape, b.dtype) for b in bufs],
        input_output_aliases={w: w for w in range(n)},
        scratch_shapes=[pltpu.SemaphoreType.DMA((n,)), pltpu.SemaphoreType.DMA((n,))],
    )(*bufs)


def _exchange_small(vec):
    def body(src, out, send_sems, recv_sems, local_sem):
        x, y, c, _ = _place()
        me = 4 * x + 2 * y + c
        lc = pltpu.make_async_copy(src, out.at[me], local_sem)
        lc.start()
        sends = []
        for k in range(1, N_DEV):
            fx, fy, fc = (k >> 2) & 1, (k >> 1) & 1, k & 1
            to = (x ^ fx, y ^ fy, c ^ fc)
            s = pltpu.make_async_remote_copy(src_ref=src, dst_ref=out.at[me], send_sem=send_sems.at[k - 1],
                                             recv_sem=recv_sems.at[k - 1], device_id=to, device_id_type=MESH)
            s.start()
            sends.append(s)
        for k in range(1, N_DEV):
            fx, fy, fc = (k >> 2) & 1, (k >> 1) & 1, k & 1
            frm = 4 * (x ^ fx) + 2 * (y ^ fy) + (c ^ fc)
            pltpu.make_async_remote_copy(src_ref=src, dst_ref=out.at[frm], send_sem=send_sems.at[k - 1],
                                         recv_sem=recv_sems.at[k - 1], device_id=(x ^ fx, y ^ fy, c ^ fc),
                                         device_id_type=MESH).wait_recv()
        for s in sends:
            s.wait_send()
        lc.wait()

    return pl.pallas_call(
        body, name="small_exchange", in_specs=[ANY], out_specs=ANY,
        out_shape=jax.ShapeDtypeStruct((N_DEV,) + vec.shape, vec.dtype),
        scratch_shapes=[pltpu.SemaphoreType.DMA((N_DEV - 1,)), pltpu.SemaphoreType.DMA((N_DEV - 1,)),
                        pltpu.SemaphoreType.DMA],
    )(vec)


def _add_own_half(g, other, c_arr):
    ns, rh, cc = other.shape
    tr = min(TR, rh)
    nb = rh // tr

    def body(c_ref, g_ref, o_ref, out_ref):
        out_ref[...] = (g_ref[...].astype(F32) + o_ref[...].astype(F32)).astype(out_ref.dtype)

    slab = pl.BlockSpec((None, tr, cc), lambda i, cr: (i // nb, i % nb, 0))
    return pl.pallas_call(
        body, name="rs_add_own_half",
        grid_spec=pltpu.PrefetchScalarGridSpec(
            num_scalar_prefetch=1, grid=(ns * nb,),
            in_specs=[pl.BlockSpec((None, tr, cc), lambda i, cr: (i // nb, cr[0] * nb + i % nb, 0)), slab],
            out_specs=slab),
        out_shape=jax.ShapeDtypeStruct(other.shape, other.dtype),
        compiler_params=_cparams(("arbitrary",)),
    )(c_arr, g, other)


def _sum_slots(name, a):
    ns, r, cc = a.shape
    tr = min(TR, r)
    ins = [_In(a, (None, tr, cc), functools.partial(lambda s, i: (s, i, 0), s)) for s in range(ns)]

    def fn(*blocks):
        acc = blocks[0].astype(F32)
        for b in blocks[1:]:
            acc = acc + b.astype(F32)
        return (acc,)

    return _fused(name, fn, r // tr, ins, [((r, cc), F32, (tr, cc), lambda i: (i, 0))])[0]


def _adamw_fn(w, g, m, v):
    m = ADAM_B1 * m + (1.0 - ADAM_B1) * g
    v = ADAM_B2 * v + (1.0 - ADAM_B2) * jnp.square(g)
    m_hat = m / (1.0 - ADAM_B1 ** ADAM_STEP)
    v_hat = v / (1.0 - ADAM_B2 ** ADAM_STEP)
    delta = -ADAM_LR * (m_hat / (jnp.sqrt(v_hat) + ADAM_EPS) + ADAM_WD * w)
    return delta, m, v


def _adamw(w, g, m, v):
    r, cc = w.shape
    tr = min(TR, r)
    ins = [_In(a, (tr, cc), lambda i: (i, 0)) for a in (w, g, m, v)]
    return _fused("adamw", _adamw_fn, r // tr, ins, [((r, cc), F32, (tr, cc), lambda i: (i, 0))] * 3)


def _pack(arrs):
    flat = jnp.concatenate([a.reshape(-1) for a in arrs])
    tile = TR * LANES
    n = -(-flat.shape[0] // tile) * tile
    return jnp.pad(flat, (0, n - flat.shape[0])).reshape(-1, LANES)


def _unpack(vec, shapes):
    flat = vec.reshape(-1)
    out, o = [], 0
    for s in shapes:
        n = math.prod(s)
        out.append(flat[o:o + n].reshape(s))
        o += n
    return out


def kernel(x, p, norm_mix, w_in, sg_ln_g, sg_ln_b, sg_w, sg_b, sc_conv, gdn_conv, gdn_a_log, gdn_dt_bias, gdn_norm, out_norm_a, out_norm_b, w_o, norm_ffn, w_ff1, w_ff2, norm_ple, w_ple_gate, w_ple_proj, norm_final, loss_target, m_norm_mix, m_w_in, m_sg_ln_g, m_sg_ln_b, m_sg_w, m_sg_b, m_sc_conv, m_gdn_conv, m_gdn_a_log, m_gdn_dt_bias, m_gdn_norm, m_out_norm_a, m_out_norm_b, m_w_o, m_norm_ffn, m_w_ff1, m_w_ff2, m_norm_ple, m_w_ple_gate, m_w_ple_proj, m_norm_final, v_norm_mix, v_w_in, v_sg_ln_g, v_sg_ln_b, v_sg_w, v_sg_b, v_sc_conv, v_gdn_conv, v_gdn_a_log, v_gdn_dt_bias, v_gdn_norm, v_out_norm_a, v_out_norm_b, v_w_o, v_norm_ffn, v_w_ff1, v_w_ff2, v_norm_ple, v_w_ple_gate, v_w_ple_proj, v_norm_final):
    given = dict(locals())
    w = {n: given[n] for n in WEIGHTS}
    m = {n: given['m_' + n] for n in WEIGHTS}
    v = {n: given['v_' + n] for n in WEIGHTS}
    shard = 2 * lax.axis_index("x") + lax.axis_index("y")
    core = lax.axis_index("c")

    shard_arr = shard.reshape(1).astype(jnp.int32)
    c_arr = core.reshape(1).astype(jnp.int32)
    convs = dict(zip(CONVS, _gather_direct([_place_shard(w[n], shard_arr, F32) for n in CONVS])))
    small = {n: w[n] for n in SMALL if n not in CONVS}
    late_w = [n for n in BIG if n != 'w_in']
    first = _gather_halves([_place_shard(w['w_in'], shard_arr, BF16, 0)])
    fly0 = _gather_start("l0", [_place_shard(w[n], shard_arr, BF16, 0) for n in late_w],
                         [first[0], *convs.values()])
    fly1 = _gather_start("l1", [_place_shard(w[n], shard_arr, BF16, 1) for n in BIG], [fly0[2]])

    lws = [_layer_weights(0, {'w_in': first[0]}, convs, small), None]
    lws[0]['norm_mix'] = lws[0]['norm_mix'] + fly1[2][0, 0]

    def rest_of_layer0(behind):
        got = _gather_forward(_gather_wait("l0", fly0[0], fly0[1], behind))
        lws[0].update(_big_weights(dict(zip(late_w, got))))

    h, sv0 = _layer_fwd(x[0], p[0, 0], lws[0], rest_of_layer0)
    got = _gather_forward(_gather_wait("l1", fly1[0], fly1[1], h))
    lws[1] = _layer_weights(1, dict(zip(BIG, got)), convs, small)
    h, sv1 = _layer_fwd(h, p[1, 0], lws[1])
    saved = [sv0, sv1]
    loss, dh, dnf = _loss_grad(h, small['norm_final'].reshape(1, -1), loss_target[0])

    def swap_add(names, g):
        big = [_shard_major(n, g[n]) for n in names]
        return [_add_own_half(g_, o, c_arr) for g_, o in zip(big, _swap_halves(big))]

    early_g = ['w_ff1', 'w_ff2', 'w_ple_gate', 'w_ple_proj']
    late_g = [n for n in BIG if n not in early_g]
    per_layer = [None] * DEPTH
    dh, g = _layer_bwd(dh, saved[1], lws[1])
    per_layer[1] = _weight_grads(g)
    fly1 = _scatter_start("l1", swap_add(BIG, g), [dh])
    lws[0]['norm_ple'] = lws[0]['norm_ple'] + fly1[3][0, 0]
    fly0 = []

    def early_grads(g0):
        fly0.extend(_scatter_start("l0", swap_add(early_g, g0), []))
        return fly0[3]

    dh, g = _layer_bwd(dh, saved[0], lws[0], early_grads)
    per_layer[0] = _weight_grads(g)
    last_parts = swap_add(late_g, g)
    last_slots = _scatter_shards(last_parts)
    groups = [(1, BIG, *_scatter_wait("l1", fly1[0], fly1[1], fly1[2], dh)),
              (0, early_g, *_scatter_wait("l0", fly0[0], fly0[1], fly0[2], dh)), (0, late_g, last_parts, last_slots)]
    g_big = {}
    for l, names, parts, slots in groups:
        sums = [_sum_reduced(pt, sl, shard_arr, c_arr, l, g_big.get(n)) for n, pt, sl in zip(names, parts, slots)]
        g_big.update(zip(names, _join_halves(sums, l)))
    grad_x = dh
    grads = {n: jnp.stack([per_layer[l][n] for l in range(DEPTH)], axis=0) for n in SMALL if n != 'norm_final'}
    grads['norm_final'] = dnf.reshape(-1)

    rep = [n for n in SMALL if n not in CONVS]
    names = rep + CONVS
    vec = _pack([grads[n] for n in names] + [loss[0, :1]])
    total = _sum_slots("small_sum", _exchange_small(vec))
    parts_small = _unpack(total, [grads[n].shape for n in names] + [(1,)])
    g_small = dict(zip(names, parts_small[:-1]))
    loss_out = parts_small[-1].reshape(())
    for n in CONVS:
        width = w[n].shape[-1]
        g_small[n] = lax.dynamic_slice_in_dim(g_small[n], shard * width, width, axis=2)

    delta, new_m, new_v, grad_w = {}, {}, {}, {}
    for n in BIG:
        shp = w[n].shape
        two_d = lambda a: a.reshape(-1, shp[-1])
        d_, m_, v_ = _adamw(two_d(w[n]), two_d(g_big[n]), two_d(m[n]), two_d(v[n]))
        delta[n], new_m[n], new_v[n], grad_w[n] = d_.reshape(shp), m_.reshape(shp), v_.reshape(shp), g_big[n]
    shapes = [w[n].shape for n in SMALL]
    d_, m_, v_ = _adamw(_pack([w[n] for n in SMALL]), _pack([g_small[n] for n in SMALL]),
                        _pack([m[n] for n in SMALL]), _pack([v[n] for n in SMALL]))
    for n, dd, mm, vv in zip(SMALL, _unpack(d_, shapes), _unpack(m_, shapes), _unpack(v_, shapes)):
        delta[n], new_m[n], new_v[n], grad_w[n] = dd, mm, vv, g_small[n]

    return (loss_out, grad_x[None], *[grad_w[n] for n in WEIGHTS], *[delta[n] for n in WEIGHTS],
            *[new_m[n] for n in WEIGHTS], *[new_v[n] for n in WEIGHTS])
```

```python
import functools
import math

import jax
import jax.numpy as jnp
from jax import lax
from jax.experimental import pallas as pl
from jax.experimental.pallas import tpu as pltpu

F32 = jnp.float32
BF16 = jnp.bfloat16
HI = lax.Precision.HIGH
MESH = pl.DeviceIdType.MESH

LANES = 128
EPS = 1e-6
SG_HEADS, SG_CHUNK = 4, 128
SC_GROUPS, SC_KERNEL = 4, 3
GDN_HEADS, GDN_CONV = 8, 4
GDN_L = 128
GDN_CPS = 8
GDN_HPS = 2
SG_W = SG_HEADS * LANES
SC_W = SC_GROUPS * LANES
GDN_W = GDN_HEADS * LANES
IN_COLS = 2 * SG_W + 3 * SC_W + 4 * GDN_W + 2 * GDN_HEADS
PROJ_W = 7168
CB_U, CB_V = 0, 4
CB_GB, CB_GC, CB_X = 8, 12, 16
CB_Q, CB_K, CB_VV, CB_Z, CB_AB = 20, 28, 36, 44, 52
N_SHARD = 4
N_DEV = 8
DEPTH = 2

ADAM_LR, ADAM_B1, ADAM_B2, ADAM_EPS, ADAM_WD, ADAM_STEP = 0.001, 0.9, 0.999, 1e-08, 0.01, 10

VMEM_LIMIT = 56 << 20

WEIGHTS = ['norm_mix', 'w_in', 'sg_ln_g', 'sg_ln_b', 'sg_w', 'sg_b', 'sc_conv', 'gdn_conv', 'gdn_a_log',
           'gdn_dt_bias', 'gdn_norm', 'out_norm_a', 'out_norm_b', 'w_o', 'norm_ffn', 'w_ff1', 'w_ff2', 'norm_ple',
           'w_ple_gate', 'w_ple_proj', 'norm_final']
BIG = ['w_in', 'w_o', 'w_ff1', 'w_ff2', 'w_ple_gate', 'w_ple_proj']
CONVS = ['sc_conv', 'gdn_conv']
SMALL = [n for n in WEIGHTS if n not in BIG]


def _cparams(sem=None):
    return pltpu.CompilerParams(dimension_semantics=sem, vmem_limit_bytes=VMEM_LIMIT)


def _dot(a, b, dims, prec=None):
    return lax.dot_general(a, b, (dims, ((), ())), precision=prec, preferred_element_type=F32)


def _mm(a, b):
    return _dot(a, b, ((1,), (0,)), HI)


def _mm_nt(a, b):
    return _dot(a, b, ((1,), (1,)), HI)


def _mm_tn(a, b):
    return _dot(a, b, ((0,), (0,)), HI)


@jax.custom_vjp
def _bmm(a, b):
    return _dot(a.astype(BF16), b.astype(BF16), ((1,), (0,)))


def _bmm_fwd(a, b):
    return _bmm(a, b), (a, b)


def _bmm_bwd(res, g):
    a, b = res
    gb = g.astype(BF16)
    return _dot(gb, b.astype(BF16), ((1,), (1,))), _dot(a.astype(BF16), gb, ((0,), (0,)))


_bmm.defvjp(_bmm_fwd, _bmm_bwd)


@jax.custom_vjp
def _bmm_tn(a, b):
    return _dot(a.astype(BF16), b.astype(BF16), ((0,), (0,)))


def _bmm_tn_fwd(a, b):
    return _bmm_tn(a, b), (a, b)


def _bmm_tn_bwd(res, g):
    a, b = res
    gb = g.astype(BF16)
    return _dot(b.astype(BF16), gb, ((1,), (1,))), _dot(a.astype(BF16), gb, ((1,), (0,)))


_bmm_tn.defvjp(_bmm_tn_fwd, _bmm_tn_bwd)


def _sigmoid(x):
    return 1.0 / (1.0 + jnp.exp(-x))


def _silu(x):
    return x * _sigmoid(x)


def _gelu(x):
    c = math.sqrt(2.0 / math.pi)
    return 0.5 * x * (1.0 + jnp.tanh(c * (x + 0.044715 * (x * x * x))))


def _softplus(x):
    return jnp.maximum(x, 0.0) + jnp.log(1.0 + jnp.exp(-jnp.abs(x)))


def _rms(x, g):
    return x * lax.rsqrt(jnp.mean(x * x, axis=-1, keepdims=True) + EPS) * g


def _roll_rows(x, shift):
    return pltpu.roll(x, shift % x.shape[0], 0)


@functools.partial(jax.custom_vjp, nondiff_argnums=(1,))
def _shift_down(x, j):
    row = lax.broadcasted_iota(jnp.int32, x.shape, 0)
    return jnp.where(row >= j, _roll_rows(x, j), 0.0)


def _shift_down_fwd(x, j):
    return _shift_down(x, j), None


def _shift_down_bwd(j, _, dy):
    row = lax.broadcasted_iota(jnp.int32, dy.shape, 0)
    return (jnp.where(row < dy.shape[0] - j, _roll_rows(dy, -j), 0.0),)


_shift_down.defvjp(_shift_down_fwd, _shift_down_bwd)


def _causal_conv(x, taps):
    k = len(taps)
    y = taps[k - 1] * x
    for j in range(k - 1):
        y = y + taps[j] * _shift_down(x, k - 1 - j)
    return y


class _In:
    def __init__(self, arr, block, imap, shared=False, gshape=None, gmap=None):
        self.arr, self.block, self.imap, self.shared = arr, block, imap, shared
        self.gshape = arr.shape if gshape is None else gshape
        self.gmap = imap if gmap is None else gmap


def _fused(name, fn, n, ins, outs, douts=None, need=None, gdt=None, add=None):
    n_in, n_out = len(ins), len(outs)
    in_specs = [pl.BlockSpec(s.block, s.imap) for s in ins]
    out_specs = [pl.BlockSpec(bs, im) for _, _, bs, im in outs]
    if douts is None:
        def body(*refs):
            res = fn(*[r[...] for r in refs[:n_in]])
            for r, v in zip(refs[n_in:], res):
                r[...] = v.astype(r.dtype)

        return pl.pallas_call(
            body, name=name, grid=(n,), in_specs=in_specs, out_specs=out_specs,
            out_shape=[jax.ShapeDtypeStruct(s, d) for s, d, _, _ in outs],
            compiler_params=_cparams(("arbitrary",)),
        )(*[s.arr for s in ins])

    gdt = list(gdt) if gdt is not None else [F32] * n_in
    add = dict(add or {})
    gidx = [i for i in range(n_in) if need[i]]
    aidx = [i for i in gidx if i in add]

    def body(*refs):
        in_refs, d_refs = refs[:n_in], refs[n_in:n_in + n_out]
        a_refs = dict(zip(aidx, refs[n_in + n_out:n_in + n_out + len(aidx)]))
        g_refs = refs[n_in + n_out + len(aidx):]
        vals = [r[...] for r in in_refs]

        def f(*dv):
            full = list(vals)
            for i, v in zip(gidx, dv):
                full[i] = v
            return tuple(o.astype(F32) for o in fn(*full))

        _, vjp = jax.vjp(f, *[vals[i].astype(F32) for i in gidx])
        grads = vjp(tuple(r[...].astype(F32) for r in d_refs))
        for i, g_ref, g in zip(gidx, g_refs, grads):
            if i in a_refs:
                g = g + a_refs[i][...].astype(F32)
            if ins[i].shared:
                period = n if ins[i].shared is True else ins[i].shared

                @pl.when(pl.program_id(0) % period == 0)
                def _():
                    g_ref[...] = jnp.zeros_like(g_ref)
                g_ref[...] += g.astype(g_ref.dtype)
            else:
                g_ref[...] = g.astype(g_ref.dtype)

    g_specs = [pl.BlockSpec(ins[i].block, ins[i].gmap) for i in gidx]
    g_shape = [jax.ShapeDtypeStruct(ins[i].gshape, gdt[i]) for i in gidx]
    res = pl.pallas_call(
        body, name=name, grid=(n,), in_specs=in_specs + out_specs + [g_specs[gidx.index(i)] for i in aidx],
        out_specs=g_specs, out_shape=g_shape,
        compiler_params=_cparams(("arbitrary",)),
    )(*[s.arr for s in ins], *douts, *[add[i] for i in aidx])
    full = [None] * n_in
    for i, g in zip(gidx, res):
        full[i] = g
    return full


def _row_in(a, tm, cb=None):
    if cb is None:
        return _In(a, (tm, a.shape[1]), lambda i: (i, 0))
    return _In(a, (tm, LANES), lambda i: (i, cb), gshape=(a.shape[0], LANES), gmap=lambda i: (i, 0))


def _row_shared(a):
    return _In(a, a.shape, lambda i: (0, 0), shared=True)


def _row_out(t, w, dt, tm):
    return ((t, w), dt, (tm, w), lambda i: (i, 0))


def _col_in(a, base, nblk):
    t = a.shape[0]
    return _In(a, (t, LANES), lambda i: (0, base + i), gshape=(t, nblk * LANES), gmap=lambda i: (0, i))


def _col_par(a):
    return _In(a, (None,) + a.shape[1:], lambda i: (i, 0, 0))


def _col_out(t, w, dt, base=0):
    return ((t, w), dt, (t, LANES), lambda i: (0, base + i))


def _matmul(name, a, b, mode, m, n, k, tm, tn, tk, b_spec=None, epilogue=None, extras=(), outs=None, out_dtype=F32,
            n_outer=False):
    tm, tn, tk = min(tm, m), min(tn, n), min(tk, k)
    assert m % tm == 0 and n % tn == 0 and k % tk == 0, (name, m, n, k, tm, tn, tk)
    nk = k // tk
    a_spec = (pl.BlockSpec((tk, tm), lambda i, j, kk: (kk, i)) if mode == "tn"
              else pl.BlockSpec((tm, tk), lambda i, j, kk: (i, kk)))
    if b_spec is None:
        b_spec = (pl.BlockSpec((tn, tk), lambda i, j, kk: (j, kk)) if mode == "nt"
                  else pl.BlockSpec((tk, tn), lambda i, j, kk: (kk, j)))
    dims = {"nn": ((1,), (0,)), "nt": ((1,), (1,)), "tn": ((0,), (0,))}[mode]
    if outs is None:
        outs = [((m, n), out_dtype, (tm, tn), lambda i, j, kk: (i, j))]
    if epilogue is None:
        epilogue = lambda acc: (acc,)
    n_ex = len(extras)

    def body(*refs):
        a_ref, b_ref = refs[0], refs[1]
        ex_refs = refs[2:2 + n_ex]
        o_refs = refs[2 + n_ex:2 + n_ex + len(outs)]
        part = _dot(a_ref[...].astype(BF16), b_ref[...].astype(BF16), dims)

        def finish(acc):
            for r, v in zip(o_refs, epilogue(acc, *[e[...] for e in ex_refs])):
                r[...] = v.astype(r.dtype)

        if nk == 1:
            finish(part)
        else:
            acc_ref = refs[-1]
            kk = pl.program_id(2)

            @pl.when(kk == 0)
            def _():
                acc_ref[...] = part

            @pl.when(kk > 0)
            def _():
                acc_ref[...] += part

            @pl.when(kk == nk - 1)
            def _():
                finish(acc_ref[...])

    ex_specs = [pl.BlockSpec((tm, tn), lambda i, j, kk: (i, j)) for _ in extras]
    out_specs = [pl.BlockSpec(bs, im) for _, _, bs, im in outs]
    in_specs = [a_spec, b_spec] + ex_specs
    grid = (m // tm, n // tn, nk)
    if n_outer:
        swap = lambda sp: pl.BlockSpec(sp.block_shape, functools.partial(lambda f, j, i, kk: f(i, j, kk), sp.index_map))
        in_specs, out_specs, grid = [swap(sp) for sp in in_specs], [swap(sp) for sp in out_specs], (n // tn, m // tm, nk)
    res = pl.pallas_call(
        body, name=name, grid=grid,
        in_specs=in_specs,
        out_specs=out_specs,
        out_shape=[jax.ShapeDtypeStruct(s, d) for s, d, _, _ in outs],
        scratch_shapes=[pltpu.VMEM((tm, tn), F32)] if nk > 1 else [],
        compiler_params=_cparams(("parallel", "parallel", "arbitrary")),
    )(a, b, *extras)
    return res if len(res) > 1 else res[0]


def _fn_sgu(u_pre, v_pre, ln_g, ln_b, w, bb, na):
    t = u_pre.shape[0]
    u = _gelu(u_pre)
    v = _gelu(v_pre)
    mu = jnp.mean(v, axis=-1, keepdims=True)
    vc = v - mu
    vh = vc * lax.rsqrt(jnp.mean(vc * vc, axis=-1, keepdims=True) + EPS) * ln_g + ln_b
    ri = lax.broadcasted_iota(jnp.int32, w.shape, 0)
    ci = lax.broadcasted_iota(jnp.int32, w.shape, 1)
    wc = jnp.where(ri >= ci, w, 0.0)
    f = jnp.concatenate([_bmm(wc, vh[c * SG_CHUNK:(c + 1) * SG_CHUNK]) + bb for c in range(t // SG_CHUNK)], axis=0)
    return (_rms(u * f, na),)


def _fn_sconv(gb, gc, xin, w0, w1, w2, nb):
    return (_rms(gb * _causal_conv(gc * xin, (w0, w1, w2)), nb),)


def _fn_gdn_qk(pre, w0, w1, w2, w3):
    a = _silu(_causal_conv(pre, (w0, w1, w2, w3)))
    return (a * lax.rsqrt(jnp.sum(a * a, axis=-1, keepdims=True) + EPS),)


def _fn_gdn_v(pre, w0, w1, w2, w3):
    return (_silu(_causal_conv(pre, (w0, w1, w2, w3))),)


def _fn_gdn_gates(ab, a_log, dt_bias):
    lane = lax.broadcasted_iota(jnp.int32, ab.shape, 1)
    g = -jnp.exp(a_log) * _softplus(ab + dt_bias)
    return (jnp.where(lane < GDN_HEADS, g, jnp.where(lane < 2 * GDN_HEADS, _sigmoid(ab), 0.0)),)


def _solve_unit_lower_impl(mats):
    n = mats[0].shape[0]
    ri = lax.broadcasted_iota(jnp.int32, (n, n), 0)
    ci = lax.broadcasted_iota(jnp.int32, (n, n), 1)
    ts = [(ri == ci).astype(F32)] * len(mats)
    sh = 0
    while (1 << sh) < n:
        rb = jnp.right_shift(ri, sh)
        cb = jnp.right_shift(ci, sh)
        off = ((rb & 1) == 1) & (cb == rb - 1)
        us = [_mm(t, jnp.where(off, a, 0.0)) for t, a in zip(ts, mats)]
        ts = [t - _mm(u, t) for t, u in zip(ts, us)]
        sh += 1
    return tuple(ts)


def _solve_cotangents(ts, dts):
    us = [_mm_nt(dt, t) for t, dt in zip(ts, dts)]
    return tuple(-_mm_tn(t, u) for t, u in zip(ts, us))


@jax.custom_vjp
def _solve_unit_lower(mats):
    return _solve_unit_lower_impl(mats)


def _solve_unit_lower_fwd(mats):
    ts = _solve_unit_lower_impl(mats)
    return ts, ts


def _solve_unit_lower_bwd(ts, dts):
    return (_solve_cotangents(ts, dts),)


_solve_unit_lower.defvjp(_solve_unit_lower_fwd, _solve_unit_lower_bwd)


@jax.custom_vjp
def _solved_unit_lower(mats, ts):
    return ts


def _solved_unit_lower_fwd(mats, ts):
    return ts, ts


def _solved_unit_lower_bwd(ts, dts):
    return _solve_cotangents(ts, dts), tuple(jnp.zeros_like(t) for t in ts)


_solved_unit_lower.defvjp(_solved_unit_lower_fwd, _solved_unit_lower_bwd)


def _fn_gdn_wy(q, k, v, gates, pick_g, pick_b, t_saved=None):
    n, dk = GDN_L, q.shape[1]
    rows = [slice(c * n, (c + 1) * n) for c in range(q.shape[0] // n)]
    ri = lax.broadcasted_iota(jnp.int32, (n, n), 0)
    ci = lax.broadcasted_iota(jnp.int32, (n, n), 1)
    incl = ri >= ci
    eye = (ri == ci).astype(F32)
    last = lax.broadcasted_iota(jnp.int32, (n, 1), 0) == (n - 1)
    qs = [q[r] * (dk ** -0.5) for r in rows]
    ks = [k[r] for r in rows]
    gcs, betas, d_incl = [], [], []
    for r in rows:
        g = jnp.sum(gates[r] * pick_g, axis=1, keepdims=True)
        betas.append(jnp.sum(gates[r] * pick_b, axis=1, keepdims=True))
        g_row = jnp.sum(eye * g, axis=0, keepdims=True)
        gc = jnp.sum(jnp.where(incl, g_row, 0.0), axis=1, keepdims=True)
        gc_row = jnp.sum(eye * gc, axis=0, keepdims=True)
        gcs.append(gc)
        d_incl.append(jnp.where(incl, jnp.exp(jnp.where(incl, gc - gc_row, 0.0)), 0.0))
    kbs = [kk * b for kk, b in zip(ks, betas)]
    mats = tuple(_mm_nt(kb, kk) * jnp.where(ri > ci, d, 0.0) for kb, kk, d in zip(kbs, ks, d_incl))
    ts = _solve_unit_lower(mats) if t_saved is None else _solved_unit_lower(mats, tuple(t_saved[r] for r in rows))
    egs = [jnp.exp(gc) for gc in gcs]
    values = [_mm(t, v[r] * b) for t, r, b in zip(ts, rows, betas)]
    kcds = [_mm(t, kb * eg) for t, kb, eg in zip(ts, kbs, egs)]
    intras = [_mm_nt(qq, kk) * d for qq, kk, d in zip(qs, ks, d_incl)]
    g_lasts = [jnp.sum(jnp.where(last, gc, 0.0), axis=0, keepdims=True) for gc in gcs]
    qes = [qq * eg for qq, eg in zip(qs, egs)]
    kts = [kk * jnp.exp(gl - gc) for kk, gl, gc in zip(ks, g_lasts, gcs)]
    carries = [jnp.broadcast_to(jnp.exp(gl), (8, LANES)) for gl in g_lasts]
    cat = lambda parts: jnp.concatenate(parts, axis=0)
    res = (cat(values), cat(kcds), cat(qes), cat(kts), cat(intras), cat(carries))
    return res + (cat(ts),) if t_saved is None else res


def _gdn_steps(states, operands):
    v_new = [value - _bmm(kcd, s) for s, (value, kcd, _, _, _, _) in zip(states, operands)]
    outs = [_bmm(qe, s) + _bmm(intra, vn) for s, vn, (_, _, qe, _, intra, _) in zip(states, v_new, operands)]
    new = [s * carry + _bmm_tn(kt, vn) for s, vn, (_, _, _, kt, _, carry) in zip(states, v_new, operands)]
    return tuple(new), tuple(outs)


def _gdn_post(o, z, nrm):
    return _rms(o, nrm) * _silu(z)


def _gdn_wy_ins(q, k, v, gates, t_saved=None):
    t = q.shape[0]
    rb = min(GDN_CPS * GDN_L, t)
    hd = GDN_HEADS
    lane = jnp.arange(LANES)[None, None, :]
    pick_g = (lane == jnp.arange(hd)[:, None, None]).astype(F32)
    pick_b = (lane == jnp.arange(hd)[:, None, None] + hd).astype(F32)
    blk = lambda a: _In(a, (rb, LANES), lambda i: (i // hd, i % hd))
    par = lambda a: _In(a, (None, 1, LANES), lambda i: (i % hd, 0, 0))
    ins = [blk(q), blk(k), blk(v), _In(gates, (rb, LANES), lambda i: (i // hd, 0), shared=hd), par(pick_g),
           par(pick_b)]
    wide = lambda dt: ((t, GDN_W), dt, (rb, LANES), lambda i: (i // hd, i % hd))
    carry = ((hd, 8 * (t // GDN_L), LANES), F32, (None, 8 * (rb // GDN_L), LANES), lambda i: (i % hd, i // hd, 0))
    outs = [wide(F32), wide(BF16), wide(BF16), wide(BF16), wide(BF16), carry]
    if t_saved is None:
        outs.append(wide(F32))
    else:
        ins.append(blk(t_saved))
    return (t // rb) * hd, ins, outs


def _gdn_scan_specs(t):
    wide = GDN_HPS * LANES
    once = pl.Buffered(1)
    col = lambda base: pl.BlockSpec((t, wide), lambda h: (0, base // GDN_HPS + h), pipeline_mode=once)
    carry = pl.BlockSpec((GDN_HPS, 8 * (t // GDN_L), LANES), lambda h: (h, 0, 0))
    par = pl.BlockSpec((GDN_HPS, 1, LANES), lambda h: (h, 0, 0))
    return col, carry, par


def _head_operands(refs, cy_ref, c, i):
    rows = pl.ds(pl.multiple_of(c * GDN_L, GDN_L), GDN_L)
    lanes = pl.ds(i * LANES, LANES)
    return tuple(r[rows, lanes] for r in refs) + (cy_ref[i, pl.ds(pl.multiple_of(c * 8, 8), 1), :],)


def _gdn_scan_fwd(wy, proj, nrm):
    t = wy[0].shape[0]
    nc = t // GDN_L
    heads = range(GDN_HPS)

    def body(val_ref, kcd_ref, qe_ref, kt_ref, in_ref, cy_ref, z_ref, n_ref, y_ref, o_scr):
        big = (val_ref, kcd_ref, qe_ref, kt_ref, in_ref)

        def step(c, states):
            rows = pl.ds(pl.multiple_of(c * GDN_L, GDN_L), GDN_L)
            states, outs = _gdn_steps(states, [_head_operands(big, cy_ref, c, i) for i in heads])
            for i in heads:
                o_scr[rows, pl.ds(i * LANES, LANES)] = outs[i]
            return states

        lax.fori_loop(0, nc, step, tuple(jnp.zeros((LANES, LANES), F32) for _ in heads))
        for i in heads:
            lanes = pl.ds(i * LANES, LANES)
            y_ref[:, lanes] = _gdn_post(o_scr[:, lanes], z_ref[:, lanes], n_ref[i]).astype(y_ref.dtype)

    col, carry, par = _gdn_scan_specs(t)
    return pl.pallas_call(
        body, name="gdn_scan_fwd", grid=(GDN_HEADS // GDN_HPS,),
        in_specs=[col(0)] * 5 + [carry, col(CB_Z), par],
        out_specs=col(0), out_shape=jax.ShapeDtypeStruct((t, GDN_W), BF16),
        scratch_shapes=[pltpu.VMEM((t, GDN_HPS * LANES), F32)],
        compiler_params=_cparams(("arbitrary",)),
    )(*wy, proj, nrm)


def _gdn_scan_bwd(wy, proj, nrm, dy, dy_base):
    t = wy[0].shape[0]
    nc = t // GDN_L
    heads = range(GDN_HPS)

    def body(val_ref, kcd_ref, qe_ref, kt_ref, in_ref, cy_ref, z_ref, n_ref, dy_ref,
             dval_ref, dkcd_ref, dqe_ref, dkt_ref, din_ref, dcy_ref, dz_ref, dn_ref, o_scr, s_scr):
        big = (val_ref, kcd_ref, qe_ref, kt_ref, in_ref)
        dbig = (dval_ref, dkcd_ref, dqe_ref, dkt_ref, din_ref)

        def step(c, states):
            rows = pl.ds(pl.multiple_of(c * GDN_L, GDN_L), GDN_L)
            for i in heads:
                s_scr[i, c] = states[i]
            states, outs = _gdn_steps(states, [_head_operands(big, cy_ref, c, i) for i in heads])
            for i in heads:
                o_scr[rows, pl.ds(i * LANES, LANES)] = outs[i]
            return states

        zeros = tuple(jnp.zeros((LANES, LANES), F32) for _ in heads)
        lax.fori_loop(0, nc, step, zeros)
        for i in heads:
            lanes = pl.ds(i * LANES, LANES)
            _, vjp_post = jax.vjp(_gdn_post, o_scr[:, lanes], z_ref[:, lanes], n_ref[i])
            do, dz, dn = vjp_post(dy_ref[:, lanes].astype(F32))
            dz_ref[:, lanes] = dz.astype(dz_ref.dtype)
            dn_ref[i] = dn
            o_scr[:, lanes] = do
        dcy_ref[...] = jnp.zeros_like(dcy_ref)

        def rstep(k, dstates):
            c = nc - 1 - k
            rows = pl.ds(pl.multiple_of(c * GDN_L, GDN_L), GDN_L)
            ops = [tuple(o.astype(F32) for o in _head_operands(big, cy_ref, c, i)) for i in heads]
            _, vjp_c = jax.vjp(_gdn_steps, tuple(s_scr[i, c] for i in heads), ops)
            dstates, dops = vjp_c((dstates, tuple(o_scr[rows, pl.ds(i * LANES, LANES)] for i in heads)))
            for i in heads:
                for r, g in zip(dbig, dops[i][:5]):
                    r[rows, pl.ds(i * LANES, LANES)] = g.astype(r.dtype)
                dcy_ref[i, pl.ds(pl.multiple_of(c * 8, 8), 1), :] = dops[i][5]
            return dstates

        lax.fori_loop(0, nc, rstep, zeros)

    col, carry, par = _gdn_scan_specs(t)
    wide = jax.ShapeDtypeStruct((t, GDN_W), BF16)
    return pl.pallas_call(
        body, name="gdn_scan_bwd", grid=(GDN_HEADS // GDN_HPS,),
        in_specs=[col(0)] * 5 + [carry, col(CB_Z), par, col(dy_base)],
        out_specs=[col(0)] * 5 + [carry, col(0), par],
        out_shape=[wide] * 5 + [jax.ShapeDtypeStruct(wy[5].shape, F32), wide,
                                jax.ShapeDtypeStruct((GDN_HEADS, 1, LANES), F32)],
        scratch_shapes=[pltpu.VMEM((t, GDN_HPS * LANES), F32), pltpu.VMEM((GDN_HPS, nc, LANES, LANES), F32)],
        compiler_params=_cparams(("arbitrary",)),
    )(*wy, proj, nrm, dy)


TM = 512
TR = 256


def _rms_fwd(name, h, g):
    t, d = h.shape
    tm = min(TR, t)
    return _fused(name, lambda hb, gb: (_rms(hb, gb),), t // tm, [_row_in(h, tm), _row_shared(g)],
                  [_row_out(t, d, BF16, tm)])[0]


def _rms_bwd(name, h, g, dxn, dh_next):
    t, d = h.shape
    tm = min(TR, t)
    dh, dg = _fused(name, lambda hb, gb: (_rms(hb, gb),), t // tm, [_row_in(h, tm), _row_shared(g)],
                    [_row_out(t, d, F32, tm)], douts=[dxn], need=[True, True], add={0: dh_next})
    return dh, dg


def _mixer_ins(proj, lw):
    sgu = [_col_in(proj, CB_U, SG_HEADS), _col_in(proj, CB_V, SG_HEADS), _col_par(lw['sg_ln_g']),
           _col_par(lw['sg_ln_b']), _col_par(lw['sg_w']), _col_par(lw['sg_bb']), _col_par(lw['out_norm_a'])]
    sconv = [_col_in(proj, CB_GB, SC_GROUPS), _col_in(proj, CB_GC, SC_GROUPS), _col_in(proj, CB_X, SC_GROUPS)] + \
            [_col_par(w) for w in lw['sc_taps']] + [_col_par(lw['out_norm_b'])]
    gq = [_col_in(proj, CB_Q, GDN_HEADS)] + [_col_par(w) for w in lw['q_taps']]
    gk = [_col_in(proj, CB_K, GDN_HEADS)] + [_col_par(w) for w in lw['k_taps']]
    gv = [_col_in(proj, CB_VV, GDN_HEADS)] + [_col_par(w) for w in lw['v_taps']]
    return sgu, sconv, gq, gk, gv


def _gates_ins(proj, lw, tm):
    return [_row_in(proj, tm, CB_AB), _row_shared(lw['a_log_row']), _row_shared(lw['dt_bias_row'])]


def _layer_fwd(h, p_l, lw, late=None):
    t, d = h.shape
    xn = _rms_fwd("rms_fwd", h, lw['norm_mix'])
    proj = _matmul("proj_fwd", xn, lw['w_in'], "nn", t, PROJ_W, d, TM, 1024, d)
    sgu, sconv, gq, gk, gv = _mixer_ins(proj, lw)
    ya = _fused("sgu_fwd", _fn_sgu, SG_HEADS, sgu, [_col_out(t, SG_W, BF16)])[0]
    yb = _fused("sconv_fwd", _fn_sconv, SC_GROUPS, sconv, [_col_out(t, SC_W, BF16)])[0]
    q = _fused("gdn_q_fwd", _fn_gdn_qk, GDN_HEADS, gq, [_col_out(t, GDN_W, F32)])[0]
    k = _fused("gdn_k_fwd", _fn_gdn_qk, GDN_HEADS, gk, [_col_out(t, GDN_W, F32)])[0]
    v = _fused("gdn_v_fwd", _fn_gdn_v, GDN_HEADS, gv, [_col_out(t, GDN_W, F32)])[0]
    tm = min(TR, t)
    gates = _fused("gdn_gates_fwd", _fn_gdn_gates, t // tm, _gates_ins(proj, lw, tm),
                   [_row_out(t, LANES, F32, tm)])[0]
    n_wy, wy_ins, wy_outs = _gdn_wy_ins(q, k, v, gates)
    wy = _fused("gdn_wy_fwd", _fn_gdn_wy, n_wy, wy_ins, wy_outs)
    wy, wy_t = wy[:6], wy[6]
    yc = _gdn_scan_fwd(wy, proj, lw['gdn_norm'])
    ycat = jnp.concatenate([ya, yb, yc], axis=1)
    if late is not None:
        late(ycat)
    dff = lw['w_ff2'].shape[0]
    h2 = _matmul("wo_fwd", ycat, lw['w_o'], "nn", t, d, d, TM, 1024, d,
                 epilogue=lambda acc, hb: (hb + acc,), extras=(h,))
    hn = _rms_fwd("rms_fwd", h2, lw['norm_ffn'])
    per = d // 1024
    s, r = _matmul("ff1_fwd", hn, lw['w_ff1'], "nn", t, dff, d, TM, 1024, d,
                   b_spec=pl.BlockSpec((None, d, 1024), lambda i, j, kk: (j // per, 0, j % per)),
                   epilogue=lambda acc: (jnp.maximum(acc, 0.0), jnp.square(jnp.maximum(acc, 0.0))),
                   outs=[((t, dff), BF16, (min(TM, t), 1024), lambda i, j, kk: (i, j))] * 2)
    h3 = _matmul("ff2_fwd", r, lw['w_ff2'], "nn", t, d, dff, TM, 1024, 4096,
                 epilogue=lambda acc, hb: (hb + acc,), extras=(h2,))
    hn2 = _rms_fwd("rms_fwd", h3, lw['norm_ple'])
    pp = _matmul("ple_proj_fwd", p_l, lw['w_ple_proj'], "nn", t, d, p_l.shape[1], TM, 1024, p_l.shape[1])

    def gate_epilogue(acc, hb, ppb):
        sg = _sigmoid(acc)
        return hb + ppb * sg, sg

    h4, gate = _matmul("ple_gate_fwd", hn2, lw['w_ple_gate'], "nn", t, d, d, TM, 1024, d, epilogue=gate_epilogue,
                       extras=(h3, pp), outs=[((t, d), F32, (min(TM, t), 1024), lambda i, j, kk: (i, j))] * 2)
    saved = dict(h=h, xn=xn, proj=proj, q=q, k=k, v=v, gates=gates, ycat=ycat, h2=h2, hn=hn, s=s, r=r, h3=h3,
                 hn2=hn2, pp=pp, gate=gate, p=p_l, wy=wy, wy_t=wy_t)
    return h4, saved


def _layer_bwd(dh4, sv, lw, early=None):
    t, d = dh4.shape
    dff = lw['w_ff2'].shape[0]
    tm = min(TR, t)
    g = {}
    dacc, dpp = _fused("ple_bwd_gate", lambda dh, pp, gt: (dh * pp * gt * (1.0 - gt), dh * gt), t // tm,
                       [_row_in(dh4, tm), _row_in(sv['pp'], tm), _row_in(sv['gate'], tm)],
                       [_row_out(t, d, BF16, tm)] * 2)
    g['w_ple_gate'] = _matmul("dw_ple_gate", sv['hn2'], dacc, "tn", d, d, t, TM, 1024, t, out_dtype=BF16,
                              n_outer=True)
    g['w_ple_proj'] = _matmul("dw_ple_proj", sv['p'], dpp, "tn", sv['p'].shape[1], d, t, TM, 1024, t,
                              out_dtype=BF16, n_outer=True)
    dhn2 = _matmul("dx_ple_gate", dacc, lw['w_ple_gate'], "nt", t, d, d, TM, 1024, d)
    dh3, g['norm_ple'] = _rms_bwd("rms_bwd", sv['h3'], lw['norm_ple'], dhn2, dh4)
    da = _matmul("dx_ff2", dh3, lw['w_ff2'], "nt", t, dff, d, TM, 1024, d,
                 epilogue=lambda acc, sb: (acc * (2.0 * sb.astype(F32)),), extras=(sv['s'],),
                 outs=[((t, dff), BF16, (min(TM, t), 1024), lambda i, j, kk: (i, j))])
    g['w_ff2'] = _matmul("dw_ff2", sv['r'], dh3, "tn", dff, d, t, TM, 512, t, out_dtype=BF16, n_outer=True)
    per = d // 1024
    dhn = _matmul("dx_ff1", da, lw['w_ff1'], "nt", t, d, dff, TM, 1024, d,
                  b_spec=pl.BlockSpec((None, 1024, d), lambda i, j, kk: (kk, j, 0)))
    g['w_ff1'] = _matmul("dw_ff1", sv['hn'], da, "tn", d, dff, t, TM, 1024, t, n_outer=True,
                         outs=[((N_SHARD, d, d), BF16, (None, TM, 1024), lambda i, j, kk: (j // per, i, j % per))])
    norm_ffn = lw['norm_ffn'] if early is None else lw['norm_ffn'] + early(g)[0, 0]
    dh2, g['norm_ffn'] = _rms_bwd("rms_bwd", sv['h2'], norm_ffn, dhn, dh3)
    dycat = _matmul("dx_o", dh2, lw['w_o'], "nt", t, d, d, TM, 1024, d)
    g['w_o'] = _matmul("dw_o", sv['ycat'], dh2, "tn", d, d, t, TM, 512, t, out_dtype=BF16, n_outer=True)
    proj = sv['proj']
    sgu, sconv, gq, gk, gv = _mixer_ins(proj, lw)
    bf2 = [BF16, BF16]
    r_ = _fused("sgu_bwd", _fn_sgu, SG_HEADS, sgu, [_col_out(t, d, F32, 0)], douts=[dycat], need=[True] * 7,
                gdt=bf2 + [F32] * 5)
    du, dv_, g['sg_ln_g'], g['sg_ln_b'], g['sg_w'], g['sg_bb'], g['out_norm_a'] = r_
    r_ = _fused("sconv_bwd", _fn_sconv, SC_GROUPS, sconv, [_col_out(t, d, F32, SG_HEADS)], douts=[dycat],
                need=[True] * 7, gdt=[BF16] * 3 + [F32] * 4)
    dgb, dgc, dxin = r_[:3]
    g['sc_taps'], g['out_norm_b'] = r_[3:6], r_[6]
    r_ = _gdn_scan_bwd(sv['wy'], proj, lw['gdn_norm'], dycat, SG_HEADS + SC_GROUPS)
    dwy, dz, g['gdn_norm'] = r_[:6], r_[6], r_[7]
    n_wy, wy_ins, wy_outs = _gdn_wy_ins(sv['q'], sv['k'], sv['v'], sv['gates'], sv['wy_t'])
    dq, dk, dvv, dgates = _fused("gdn_wy_bwd", _fn_gdn_wy, n_wy, wy_ins, wy_outs, douts=dwy,
                                 need=[True] * 4 + [False] * 3)[:4]
    one = [_col_out(t, GDN_W, F32)]
    r_ = _fused("gdn_q_bwd", _fn_gdn_qk, GDN_HEADS, gq, one, douts=[dq], need=[True] * 5, gdt=[BF16] + [F32] * 4)
    dpq, g['q_taps'] = r_[0], r_[1:]
    r_ = _fused("gdn_k_bwd", _fn_gdn_qk, GDN_HEADS, gk, one, douts=[dk], need=[True] * 5, gdt=[BF16] + [F32] * 4)
    dpk, g['k_taps'] = r_[0], r_[1:]
    r_ = _fused("gdn_v_bwd", _fn_gdn_v, GDN_HEADS, gv, one, douts=[dvv], need=[True] * 5, gdt=[BF16] + [F32] * 4)
    dpv, g['v_taps'] = r_[0], r_[1:]
    dab, g['a_log_row'], g['dt_bias_row'] = _fused(
        "gdn_gates_bwd", _fn_gdn_gates, t // tm, _gates_ins(proj, lw, tm), [_row_out(t, LANES, F32, tm)],
        douts=[dgates], need=[True] * 3, gdt=[BF16, F32, F32])
    pad = jnp.zeros((t, PROJ_W - (CB_AB + 1) * LANES), BF16)
    dproj = jnp.concatenate([du, dv_, dgb, dgc, dxin, dpq, dpk, dpv, dz, dab, pad], axis=1)
    dxn = _matmul("dx_in", dproj, lw['w_in'], "nt", t, d, PROJ_W, TM, 1024, PROJ_W // 2)
    g['w_in'] = _matmul("dw_in", sv['xn'], dproj, "tn", d, PROJ_W, t, TM, 1024, t, out_dtype=BF16, n_outer=True)
    dh, g['norm_mix'] = _rms_bwd("rms_bwd", sv['h'], lw['norm_mix'], dxn, dh2)
    return dh, g


def _loss_grad(h, g, tgt):
    t, d = h.shape
    tm = min(TR, t)

    def body(h_ref, g_ref, t_ref, loss_ref, dh_ref, dg_ref):
        y, vjp = jax.vjp(_rms, h_ref[...], g_ref[...])
        e = y - t_ref[...]
        dh, dg = vjp(e * (1.0 / d))

        @pl.when(pl.program_id(0) == 0)
        def _():
            loss_ref[...] = jnp.zeros_like(loss_ref)
            dg_ref[...] = jnp.zeros_like(dg_ref)

        loss_ref[...] += jnp.sum(jnp.sum(e * e, axis=1, keepdims=True), axis=0, keepdims=True) * (0.5 / d)
        dh_ref[...] = dh
        dg_ref[...] += dg

    row = pl.BlockSpec((tm, d), lambda i: (i, 0))
    return pl.pallas_call(
        body, name="loss_grad", grid=(t // tm,),
        in_specs=[row, pl.BlockSpec((1, d), lambda i: (0, 0)), row],
        out_specs=[pl.BlockSpec((1, LANES), lambda i: (0, 0)), row, pl.BlockSpec((1, d), lambda i: (0, 0))],
        out_shape=[jax.ShapeDtypeStruct((1, LANES), F32), jax.ShapeDtypeStruct((t, d), F32),
                   jax.ShapeDtypeStruct((1, d), F32)],
        compiler_params=_cparams(("arbitrary",)),
    )(h, g, tgt)


def _big_weights(full):
    lw = {}
    if 'w_in' in full:
        d = full['w_in'].shape[1]
        w_in = jnp.transpose(full['w_in'], (1, 0, 2)).reshape(d, IN_COLS)
        lw['w_in'] = jnp.pad(w_in, ((0, 0), (0, PROJ_W - IN_COLS)))
    if 'w_o' in full:
        lw['w_o'] = full['w_o'].reshape(-1, full['w_o'].shape[-1])
    if 'w_ff1' in full:
        lw['w_ff1'] = full['w_ff1']
    if 'w_ff2' in full:
        lw['w_ff2'] = full['w_ff2'].reshape(-1, full['w_ff2'].shape[-1])
    if 'w_ple_gate' in full:
        lw['w_ple_gate'] = full['w_ple_gate'].reshape(-1, full['w_ple_gate'].shape[-1])
    if 'w_ple_proj' in full:
        wpp = full['w_ple_proj']
        lw['w_ple_proj'] = jnp.transpose(wpp, (1, 0, 2)).reshape(wpp.shape[1], -1)
    return lw


def _layer_weights(l, full, convs, small):
    d = small['norm_mix'].shape[-1]
    lw = _big_weights(full)
    for n in ('norm_mix', 'norm_ffn', 'norm_ple'):
        lw[n] = small[n][l].reshape(1, d)
    lw['sg_ln_g'] = small['sg_ln_g'][l].reshape(SG_HEADS, 1, LANES)
    lw['sg_ln_b'] = small['sg_ln_b'][l].reshape(SG_HEADS, 1, LANES)
    lw['sg_w'] = small['sg_w'][l]
    lw['sg_bb'] = jnp.broadcast_to(small['sg_b'][l][:, :, None], (SG_HEADS, SG_CHUNK, LANES))
    lw['out_norm_a'] = small['out_norm_a'][l].reshape(SG_HEADS, 1, LANES)
    lw['out_norm_b'] = small['out_norm_b'][l].reshape(SC_GROUPS, 1, LANES)
    lw['gdn_norm'] = jnp.broadcast_to(small['gdn_norm'][l].reshape(1, 1, LANES), (GDN_HEADS, 1, LANES))
    lw['a_log_row'] = jnp.pad(small['gdn_a_log'][l].reshape(1, GDN_HEADS), ((0, 0), (0, LANES - GDN_HEADS)))
    lw['dt_bias_row'] = jnp.pad(small['gdn_dt_bias'][l].reshape(1, GDN_HEADS), ((0, 0), (0, LANES - GDN_HEADS)))
    sc = convs['sc_conv'][l]
    lw['sc_taps'] = [sc[:, j:j + 1, :] for j in range(SC_KERNEL)]
    gc = jnp.transpose(convs['gdn_conv'][l], (1, 0, 2)).reshape(GDN_CONV, 3 * GDN_W)
    for i, nm in enumerate(('q_taps', 'k_taps', 'v_taps')):
        part = gc[:, i * GDN_W:(i + 1) * GDN_W].reshape(GDN_CONV, GDN_HEADS, 1, LANES)
        lw[nm] = [part[j] for j in range(GDN_CONV)]
    return lw


def _shard_major(name, g):
    if name == 'w_in':
        d = g.shape[0]
        return jnp.transpose(g[:, :IN_COLS].reshape(d, N_SHARD, IN_COLS // N_SHARD), (1, 0, 2))
    if name == 'w_ple_proj':
        return jnp.transpose(g.reshape(g.shape[0], N_SHARD, g.shape[1] // N_SHARD), (1, 0, 2))
    if name == 'w_ff1':
        return g
    return g.reshape(N_SHARD, -1, g.shape[-1])


def _weight_grads(g):
    d = g['norm_mix'].shape[-1]
    out = {}
    for n in ('norm_mix', 'norm_ffn', 'norm_ple'):
        out[n] = g[n].reshape(d)
    out['sg_ln_g'] = g['sg_ln_g'].reshape(SG_W)
    out['sg_ln_b'] = g['sg_ln_b'].reshape(SG_W)
    out['sg_w'] = g['sg_w']
    out['sg_b'] = jnp.sum(g['sg_bb'], axis=2)
    out['out_norm_a'] = g['out_norm_a'].reshape(SG_W)
    out['out_norm_b'] = g['out_norm_b'].reshape(SC_W)
    out['gdn_norm'] = jnp.sum(g['gdn_norm'], axis=(0, 1))
    out['gdn_a_log'] = g['a_log_row'][0, :GDN_HEADS]
    out['gdn_dt_bias'] = g['dt_bias_row'][0, :GDN_HEADS]
    out['sc_conv'] = jnp.concatenate([w.reshape(1, SC_W) for w in g['sc_taps']], axis=0)
    taps = [jnp.concatenate([g[nm][j].reshape(1, GDN_W) for nm in ('q_taps', 'k_taps', 'v_taps')], axis=1)
            for j in range(GDN_CONV)]
    out['gdn_conv'] = jnp.concatenate(taps, axis=0)
    return out


ANY = pl.BlockSpec(memory_space=pl.ANY)


def _place():
    x, y, c = lax.axis_index("x"), lax.axis_index("y"), lax.axis_index("c")
    chips = [(1 - x, y), (x, 1 - y), (1 - x, 1 - y)]
    return x, y, c, chips


def _place_shard(wsh, shard, dtype, layer=None, after=None):
    dp, r, cc = wsh.shape
    tr = min(TR, r)
    nb = r // tr

    def body(idx_ref, w_ref, *rest):
        rest[-1][...] = w_ref[...].astype(rest[-1].dtype)

    if layer is None:
        grid, shape = (dp * nb,), (dp, N_SHARD, r, cc)
        in_spec = pl.BlockSpec((None, tr, cc), lambda i, ix: (i // nb, i % nb, 0))
        out_spec = pl.BlockSpec((None, None, tr, cc), lambda i, ix: (i // nb, ix[0], i % nb, 0))
    else:
        grid, shape = (nb,), (N_SHARD, r, cc)
        in_spec = pl.BlockSpec((None, tr, cc), lambda i, ix: (layer, i, 0))
        out_spec = pl.BlockSpec((None, tr, cc), lambda i, ix: (ix[0], i, 0))
    return pl.pallas_call(
        body, name="ag_place_shard",
        grid_spec=pltpu.PrefetchScalarGridSpec(num_scalar_prefetch=1, grid=grid,
                                               in_specs=[in_spec] + ([] if after is None else [ANY]),
                                               out_specs=out_spec),
        out_shape=jax.ShapeDtypeStruct(shape, dtype),
        compiler_params=_cparams(("arbitrary",)),
    )(shard, wsh, *([] if after is None else [after]))


def _rows(ref, slab, half):
    rh = ref.shape[1] // 2
    return ref.at[slab, pl.ds(pl.multiple_of(half * rh, rh), rh), :]


def _gather_direct(bufs):
    n = len(bufs)

    def body(*refs):
        outs = refs[n:2 * n]
        send_sems, recv_sems = refs[2 * n:]
        x, y, c, chips = _place()
        me = 2 * x + y

        def cp(w, l, j, shard, to):
            blk = outs[w].at[l, shard]
            return pltpu.make_async_remote_copy(src_ref=blk, dst_ref=blk, send_sem=send_sems.at[w, l, j],
                                                recv_sem=recv_sems.at[w, l, j], device_id=to, device_id_type=MESH)

        sends = [cp(w, l, j, me, (*chip, c)) for w in range(n) for l in range(DEPTH) for j, chip in enumerate(chips)]
        for s in sends:
            s.start()
        for w in range(n):
            for l in range(DEPTH):
                for j, (px, py) in enumerate(chips):
                    cp(w, l, j, 2 * px + py, (px, py, c)).wait_recv()
        for s in sends:
            s.wait_send()

    return pl.pallas_call(
        body, name="ag_small", in_specs=[ANY] * n, out_specs=[ANY] * n,
        out_shape=[jax.ShapeDtypeStruct(b.shape, b.dtype) for b in bufs],
        input_output_aliases={w: w for w in range(n)},
        scratch_shapes=[pltpu.SemaphoreType.DMA((n, DEPTH, 3)), pltpu.SemaphoreType.DMA((n, DEPTH, 3))],
    )(*bufs)


def _gather_halves(bufs):
    n = len(bufs)

    def body(*refs):
        outs = refs[n:2 * n]
        send_sems, recv_sems = refs[2 * n:]
        x, y, c, chips = _place()
        me = 2 * x + y
        sibling = (x, y, 1 - c)

        def cp(w, k, shard, half, to):
            blk = _rows(outs[w], shard, half)
            return pltpu.make_async_remote_copy(src_ref=blk, dst_ref=blk, send_sem=send_sems.at[w, k],
                                                recv_sem=recv_sems.at[w, k], device_id=to, device_id_type=MESH)

        sends = []
        for w in range(n):
            for j, chip in enumerate(chips):
                s = cp(w, j, me, c, (*chip, c))
                s.start()
                sends.append(s)
        for w in range(n):
            for j, (px, py) in enumerate(chips):
                cp(w, j, 2 * px + py, c, sibling).wait_recv()
                s = cp(w, 3 + j, 2 * px + py, c, sibling)
                s.start()
                sends.append(s)
        for w in range(n):
            for j, (px, py) in enumerate(chips):
                cp(w, 3 + j, 2 * px + py, 1 - c, sibling).wait_recv()
        for s in sends:
            s.wait_send()

    return pl.pallas_call(
        body, name="ag_gather_halves",
        in_specs=[ANY] * n, out_specs=[ANY] * n,
        out_shape=[jax.ShapeDtypeStruct(b.shape, b.dtype) for b in bufs],
        input_output_aliases={w: w for w in range(n)},
        scratch_shapes=[pltpu.SemaphoreType.DMA((n, 6)), pltpu.SemaphoreType.DMA((n, 6))],
    )(*bufs)


HBM = pl.BlockSpec(memory_space=pltpu.HBM)
SEM = pl.BlockSpec(memory_space=pltpu.SEMAPHORE)
_SPLIT = pltpu.CompilerParams(has_side_effects=pltpu.SideEffectType.DATAFLOW_SIDE_EFFECTING)


def _in_hbm(arrs):
    return [pltpu.with_memory_space_constraint(a, pltpu.HBM) for a in arrs]


def _gather_start(tag, bufs, after):
    n, na = len(bufs), len(after)
    k = 3 * n

    def body(*refs):
        b_refs, sems, token = refs[:n], refs[n + na:n + na + 2 * k], refs[-1]
        x, y, c, chips = _place()
        me = 2 * x + y
        for w in range(n):
            for j, chip in enumerate(chips):
                blk = _rows(b_refs[w], me, c)
                pltpu.make_async_remote_copy(src_ref=blk, dst_ref=blk, send_sem=sems[3 * w + j],
                                             recv_sem=sems[k + 3 * w + j], device_id=(*chip, c),
                                             device_id_type=MESH).start()
        token[...] = jnp.zeros_like(token)

    res = pl.pallas_call(
        body, name="ag_start_" + tag,
        out_shape=(*[pltpu.SemaphoreType.DMA(())] * (2 * k), *[pltpu.HBM(b.shape, b.dtype) for b in bufs],
                   jax.ShapeDtypeStruct((8, LANES), F32)),
        in_specs=[HBM] * n + [ANY] * na,
        out_specs=(*[SEM] * (2 * k), *[HBM] * n, pl.BlockSpec(memory_space=pltpu.VMEM)),
        input_output_aliases={w: 2 * k + w for w in range(n)}, compiler_params=_SPLIT,
    )(*_in_hbm(bufs), *after)
    return list(res[:2 * k]), list(res[2 * k:2 * k + n]), res[2 * k + n]


def _gather_wait(tag, sems, bufs, after):
    n = len(bufs)
    k = 3 * n

    def body(*refs):
        b_refs, sems = refs[:n], refs[n:n + 2 * k]
        x, y, c, chips = _place()
        me = 2 * x + y
        for w in range(n):
            for j, (px, py) in enumerate(chips):
                mine, theirs = _rows(b_refs[w], me, c), _rows(b_refs[w], 2 * px + py, c)
                cp = pltpu.make_async_remote_copy(src_ref=mine, dst_ref=theirs, send_sem=sems[3 * w + j],
                                                  recv_sem=sems[k + 3 * w + j], device_id=(px, py, c),
                                                  device_id_type=MESH)
                cp.wait_send()
                cp.wait_recv()

    res = pl.pallas_call(
        body, name="ag_wait_" + tag, out_shape=tuple(pltpu.HBM(b.shape, b.dtype) for b in bufs),
        in_specs=[HBM] * n + [SEM] * (2 * k) + [ANY], out_specs=(HBM,) * n,
        input_output_aliases={w: w for w in range(n)}, compiler_params=_SPLIT,
    )(*bufs, *sems, after)
    return list(res)


def _gather_forward(bufs):
    n = len(bufs)

    def body(*refs):
        outs = refs[n:2 * n]
        send_sems, recv_sems = refs[2 * n:]
        x, y, c, chips = _place()

        def cp(w, j, shard, half):
            blk = _rows(outs[w], shard, half)
            return pltpu.make_async_remote_copy(src_ref=blk, dst_ref=blk, send_sem=send_sems.at[w, j],
                                                recv_sem=recv_sems.at[w, j], device_id=(x, y, 1 - c),
                                                device_id_type=MESH)

        sends = [cp(w, j, 2 * px + py, c) for w in range(n) for j, (px, py) in enumerate(chips)]
        for s in sends:
            s.start()
        for w in range(n):
            for j, (px, py) in enumerate(chips):
                cp(w, j, 2 * px + py, 1 - c).wait_recv()
        for s in sends:
            s.wait_send()

    return pl.pallas_call(
        body, name="ag_forward", in_specs=[ANY] * n, out_specs=[ANY] * n,
        out_shape=[jax.ShapeDtypeStruct(b.shape, b.dtype) for b in bufs],
        input_output_aliases={w: w for w in range(n)},
        scratch_shapes=[pltpu.SemaphoreType.DMA((n, 3)), pltpu.SemaphoreType.DMA((n, 3))],
    )(*bufs)


def _scatter_start(tag, parts, after):
    n, na = len(parts), len(after)
    k = 3 * n
    lands = [lax.empty((3,) + g.shape[1:], g.dtype) for g in parts]

    def body(*refs):
        srcs, dsts, sems, token = refs[:n], refs[n:2 * n], refs[2 * n + na:2 * n + na + 2 * k], refs[-1]
        x, y, c, chips = _place()
        for w in range(n):
            for j, (px, py) in enumerate(chips):
                pltpu.make_async_remote_copy(src_ref=srcs[w].at[2 * px + py], dst_ref=dsts[w].at[j],
                                             send_sem=sems[3 * w + j], recv_sem=sems[k + 3 * w + j],
                                             device_id=(px, py, c), device_id_type=MESH).start()
        token[...] = jnp.zeros_like(token)

    res = pl.pallas_call(
        body, name="rs_scatter_start_" + tag,
        out_shape=(*[pltpu.SemaphoreType.DMA(())] * (2 * k), *[pltpu.HBM(a.shape, a.dtype) for a in parts + lands],
                   jax.ShapeDtypeStruct((8, LANES), F32)),
        in_specs=[HBM] * (2 * n) + [ANY] * na,
        out_specs=(*[SEM] * (2 * k), *[HBM] * (2 * n), pl.BlockSpec(memory_space=pltpu.VMEM)),
        input_output_aliases={w: 2 * k + w for w in range(2 * n)}, compiler_params=_SPLIT,
    )(*_in_hbm(parts + lands), *after)
    return list(res[:2 * k]), list(res[2 * k:2 * k + n]), list(res[2 * k + n:2 * k + 2 * n]), res[2 * k + 2 * n]


def _scatter_wait(tag, sems, parts, lands, after):
    n = len(parts)
    k = 3 * n

    def body(*refs):
        srcs, dsts, sems = refs[:n], refs[n:2 * n], refs[2 * n:2 * n + 2 * k]
        x, y, c, chips = _place()
        for w in range(n):
            for j, (px, py) in enumerate(chips):
                cp = pltpu.make_async_remote_copy(src_ref=srcs[w].at[2 * px + py], dst_ref=dsts[w].at[j],
                                                  send_sem=sems[3 * w + j], recv_sem=sems[k + 3 * w + j],
                                                  device_id=(px, py, c), device_id_type=MESH)
                cp.wait_send()
                cp.wait_recv()

    res = pl.pallas_call(
        body, name="rs_scatter_wait_" + tag, out_shape=tuple(pltpu.HBM(a.shape, a.dtype) for a in parts + lands),
        in_specs=[HBM] * (2 * n) + [SEM] * (2 * k) + [ANY], out_specs=(HBM,) * (2 * n),
        input_output_aliases={w: w for w in range(2 * n)}, compiler_params=_SPLIT,
    )(*parts, *lands, *sems, after)
    return list(res[:n]), list(res[n:])


def _swap_halves(grads):
    n = len(grads)

    def body(*refs):
        srcs, outs = refs[:n], refs[n:2 * n]
        send_sems, recv_sems = refs[2 * n:]
        x, y, c, _ = _place()
        cps = []
        for w in range(n):
            rh = srcs[w].shape[1] // 2
            theirs = srcs[w].at[:, pl.ds(pl.multiple_of((1 - c) * rh, rh), rh), :]
            cps.append(pltpu.make_async_remote_copy(src_ref=theirs, dst_ref=outs[w], send_sem=send_sems.at[w],
                                                    recv_sem=recv_sems.at[w], device_id=(x, y, 1 - c),
                                                    device_id_type=MESH))
        for cpy in cps:
            cpy.start()
        for cpy in cps:
            cpy.wait()

    return pl.pallas_call(
        body, name="rs_swap_halves", in_specs=[ANY] * n, out_specs=[ANY] * n,
        out_shape=[jax.ShapeDtypeStruct((g.shape[0], g.shape[1] // 2, g.shape[2]), g.dtype) for g in grads],
        scratch_shapes=[pltpu.SemaphoreType.DMA((n,)), pltpu.SemaphoreType.DMA((n,))],
    )(*grads)


def _scatter_shards(parts):
    n = len(parts)

    def body(*refs):
        srcs, outs = refs[:n], refs[n:2 * n]
        send_sems, recv_sems = refs[2 * n:]
        x, y, c, chips = _place()

        def cp(w, j, src_shard, to):
            return pltpu.make_async_remote_copy(
                src_ref=srcs[w].at[src_shard], dst_ref=outs[w].at[j], send_sem=send_sems.at[w, j],
                recv_sem=recv_sems.at[w, j], device_id=to, device_id_type=MESH)

        sends = [cp(w, j, 2 * px + py, (px, py, c)) for w in range(n) for j, (px, py) in enumerate(chips)]
        for s in sends:
            s.start()
        for s in sends:
            s.wait()

    return pl.pallas_call(
        body, name="rs_scatter", in_specs=[ANY] * n, out_specs=[ANY] * n,
        out_shape=[jax.ShapeDtypeStruct((3,) + g.shape[1:], g.dtype) for g in parts],
        scratch_shapes=[pltpu.SemaphoreType.DMA((n, 3)), pltpu.SemaphoreType.DMA((n, 3))],
    )(*parts)


def _sum_reduced(part, slots, shard, core, layer, prev):
    _, rh, cc = part.shape
    tr = min(TR, rh)
    nbh = rh // tr

    def body(shard_ref, core_ref, p_ref, s0_ref, s1_ref, s2_ref, *rest):
        acc = p_ref[...].astype(F32)
        for s_ref in (s0_ref, s1_ref, s2_ref):
            acc = acc + s_ref[...].astype(F32)
        rest[-1][...] = acc

    slot = lambda j: pl.BlockSpec((None, tr, cc), lambda i, sh, co: (j, i, 0))
    ins = [shard, core, part, slots, slots, slots] + ([] if prev is None else [prev])
    return pl.pallas_call(
        body, name="rs_sum_reduced",
        grid_spec=pltpu.PrefetchScalarGridSpec(
            num_scalar_prefetch=2, grid=(nbh,),
            in_specs=[pl.BlockSpec((None, tr, cc), lambda i, sh, co: (sh[0], i, 0)), slot(0), slot(1), slot(2)]
            + ([] if prev is None else [ANY]),
            out_specs=pl.BlockSpec((None, tr, cc), lambda i, sh, co: (layer, co[0] * nbh + i, 0))),
        out_shape=jax.ShapeDtypeStruct((DEPTH, 2 * rh, cc), F32),
        input_output_aliases={} if prev is None else {6: 0},
        compiler_params=_cparams(("arbitrary",)),
    )(*ins)


def _join_halves(bufs, layer):
    n = len(bufs)

    def body(*refs):
        outs = refs[n:2 * n]
        send_sems, recv_sems = refs[2 * n:]
        x, y, c, _ = _place()

        def cp(w, half):
            blk = _rows(outs[w], layer, half)
            return pltpu.make_async_remote_copy(src_ref=blk, dst_ref=blk, send_sem=send_sems.at[w],
                                                recv_sem=recv_sems.at[w], device_id=(x, y, 1 - c), device_id_type=MESH)

        sends = [cp(w, c) for w in range(n)]
        for s in sends:
            s.start()
        for w in range(n):
            cp(w, 1 - c).wait_recv()
        for s in sends:
            s.wait_send()

    return pl.pallas_call(
        body, name="rs_join_halves", in_specs=[ANY] * n, out_specs=[ANY] * n,
        out_shape=[jax.ShapeDtypeStruct(b.shape, b.dtype) for b in bufs],
        input_output_aliases={w: w for w in range(n)},
        scratch_shapes=[pltpu.SemaphoreType.DMA((n,)), pltpu.SemaphoreType.DMA((n,))],
    )(*bufs)


def _exchange_small(vec):
    def body(src, out, send_sems, recv_sems, local_sem):
        x, y, c, _ = _place()
        me = 4 * x + 2 * y + c
        lc = pltpu.make_async_copy(src, out.at[me], local_sem)
        lc.start()
        sends = []
        for k in range(1, N_DEV):
            fx, fy, fc = (k >> 2) & 1, (k >> 1) & 1, k & 1
            to = (x ^ fx, y ^ fy, c ^ fc)
            s = pltpu.make_async_remote_copy(src_ref=src, dst_ref=out.at[me], send_sem=send_sems.at[k - 1],
                                             recv_sem=recv_sems.at[k - 1], device_id=to, device_id_type=MESH)
            s.start()
            sends.append(s)
        for k in range(1, N_DEV):
            fx, fy, fc = (k >> 2) & 1, (k >> 1) & 1, k & 1
            frm = 4 * (x ^ fx) + 2 * (y ^ fy) + (c ^ fc)
            pltpu.make_async_remote_copy(src_ref=src, dst_ref=out.at[frm], send_sem=send_sems.at[k - 1],
                                         recv_sem=recv_sems.at[k - 1], device_id=(x ^ fx, y ^ fy, c ^ fc),
                                         device_id_type=MESH).wait_recv()
        for s in sends:
            s.wait_send()
        lc.wait()

    return pl.pallas_call(
        body, name="small_exchange", in_specs=[ANY], out_specs=ANY,
        out_shape=jax.ShapeDtypeStruct((N_DEV,) + vec.shape, vec.dtype),
        scratch_shapes=[pltpu.SemaphoreType.DMA((N_DEV - 1,)), pltpu.SemaphoreType.DMA((N_DEV - 1,)),
                        pltpu.SemaphoreType.DMA],
    )(vec)


def _add_own_half(g, other, c_arr):
    ns, rh, cc = other.shape
    tr = min(TR, rh)
    nb = rh // tr

    def body(c_ref, g_ref, o_ref, out_ref):
        out_ref[...] = (g_ref[...].astype(F32) + o_ref[...].astype(F32)).astype(out_ref.dtype)

    slab = pl.BlockSpec((None, tr, cc), lambda i, cr: (i // nb, i % nb, 0))
    return pl.pallas_call(
        body, name="rs_add_own_half",
        grid_spec=pltpu.PrefetchScalarGridSpec(
            num_scalar_prefetch=1, grid=(ns * nb,),
            in_specs=[pl.BlockSpec((None, tr, cc), lambda i, cr: (i // nb, cr[0] * nb + i % nb, 0)), slab],
            out_specs=slab),
        out_shape=jax.ShapeDtypeStruct(other.shape, other.dtype),
        compiler_params=_cparams(("arbitrary",)),
    )(c_arr, g, other)


def _sum_slots(name, a):
    ns, r, cc = a.shape
    tr = min(TR, r)
    ins = [_In(a, (None, tr, cc), functools.partial(lambda s, i: (s, i, 0), s)) for s in range(ns)]

    def fn(*blocks):
        acc = blocks[0].astype(F32)
        for b in blocks[1:]:
            acc = acc + b.astype(F32)
        return (acc,)

    return _fused(name, fn, r // tr, ins, [((r, cc), F32, (tr, cc), lambda i: (i, 0))])[0]


def _adamw_fn(w, g, m, v):
    m = ADAM_B1 * m + (1.0 - ADAM_B1) * g
    v = ADAM_B2 * v + (1.0 - ADAM_B2) * jnp.square(g)
    m_hat = m / (1.0 - ADAM_B1 ** ADAM_STEP)
    v_hat = v / (1.0 - ADAM_B2 ** ADAM_STEP)
    delta = -ADAM_LR * (m_hat / (jnp.sqrt(v_hat) + ADAM_EPS) + ADAM_WD * w)
    return delta, m, v


def _adamw(w, g, m, v):
    r, cc = w.shape
    tr = min(TR, r)
    ins = [_In(a, (tr, cc), lambda i: (i, 0)) for a in (w, g, m, v)]
    return _fused("adamw", _adamw_fn, r // tr, ins, [((r, cc), F32, (tr, cc), lambda i: (i, 0))] * 3)


def _pack(arrs):
    flat = jnp.concatenate([a.reshape(-1) for a in arrs])
    tile = TR * LANES
    n = -(-flat.shape[0] // tile) * tile
    return jnp.pad(flat, (0, n - flat.shape[0])).reshape(-1, LANES)


def _unpack(vec, shapes):
    flat = vec.reshape(-1)
    out, o = [], 0
    for s in shapes:
        n = math.prod(s)
        out.append(flat[o:o + n].reshape(s))
        o += n
    return out


def kernel(x, p, norm_mix, w_in, sg_ln_g, sg_ln_b, sg_w, sg_b, sc_conv, gdn_conv, gdn_a_log, gdn_dt_bias, gdn_norm, out_norm_a, out_norm_b, w_o, norm_ffn, w_ff1, w_ff2, norm_ple, w_ple_gate, w_ple_proj, norm_final, loss_target, m_norm_mix, m_w_in, m_sg_ln_g, m_sg_ln_b, m_sg_w, m_sg_b, m_sc_conv, m_gdn_conv, m_gdn_a_log, m_gdn_dt_bias, m_gdn_norm, m_out_norm_a, m_out_norm_b, m_w_o, m_norm_ffn, m_w_ff1, m_w_ff2, m_norm_ple, m_w_ple_gate, m_w_ple_proj, m_norm_final, v_norm_mix, v_w_in, v_sg_ln_g, v_sg_ln_b, v_sg_w, v_sg_b, v_sc_conv, v_gdn_conv, v_gdn_a_log, v_gdn_dt_bias, v_gdn_norm, v_out_norm_a, v_out_norm_b, v_w_o, v_norm_ffn, v_w_ff1, v_w_ff2, v_norm_ple, v_w_ple_gate, v_w_ple_proj, v_norm_final):
    given = dict(locals())
    w = {n: given[n] for n in WEIGHTS}
    m = {n: given['m_' + n] for n in WEIGHTS}
    v = {n: given['v_' + n] for n in WEIGHTS}
    shard = 2 * lax.axis_index("x") + lax.axis_index("y")
    core = lax.axis_index("c")

    shard_arr = shard.reshape(1).astype(jnp.int32)
    c_arr = core.reshape(1).astype(jnp.int32)
    convs = dict(zip(CONVS, _gather_direct([_place_shard(w[n], shard_arr, F32) for n in CONVS])))
    small = {n: w[n] for n in SMALL if n not in CONVS}
    late_w = [n for n in BIG if n != 'w_in']
    flyw = _gather_start("w", [_place_shard(w['w_in'], shard_arr, BF16, 0)], list(convs.values()))
    placed0 = [_place_shard(w[n], shard_arr, BF16, 0, flyw[2]) for n in late_w]
    placed1 = [_place_shard(w[n], shard_arr, BF16, 1, flyw[2]) for n in BIG]
    fly0 = _gather_start("l0", placed0, [flyw[2]])
    fly1 = _gather_start("l1", placed1, [fly0[2]])
    first = _gather_forward(_gather_wait("w", flyw[0], flyw[1], placed1[0]))

    lws = [_layer_weights(0, {'w_in': first[0]}, convs, small), None]
    lws[0]['norm_mix'] = lws[0]['norm_mix'] + fly1[2][0, 0]

    def rest_of_layer0(behind):
        got = _gather_forward(_gather_wait("l0", fly0[0], fly0[1], behind))
        lws[0].update(_big_weights(dict(zip(late_w, got))))

    h, sv0 = _layer_fwd(x[0], p[0, 0], lws[0], rest_of_layer0)
    got = _gather_forward(_gather_wait("l1", fly1[0], fly1[1], h))
    lws[1] = _layer_weights(1, dict(zip(BIG, got)), convs, small)
    h, sv1 = _layer_fwd(h, p[1, 0], lws[1])
    saved = [sv0, sv1]
    loss, dh, dnf = _loss_grad(h, small['norm_final'].reshape(1, -1), loss_target[0])

    def swap_add(names, g):
        big = [_shard_major(n, g[n]) for n in names]
        return [_add_own_half(g_, o, c_arr) for g_, o in zip(big, _swap_halves(big))]

    early_g = ['w_ff1', 'w_ff2', 'w_ple_gate', 'w_ple_proj']
    late_g = [n for n in BIG if n not in early_g]
    per_layer = [None] * DEPTH
    dh, g = _layer_bwd(dh, saved[1], lws[1])
    per_layer[1] = _weight_grads(g)
    fly1 = _scatter_start("l1", swap_add(BIG, g), [dh])
    lws[0]['norm_ple'] = lws[0]['norm_ple'] + fly1[3][0, 0]
    fly0 = []

    def early_grads(g0):
        fly0.extend(_scatter_start("l0", swap_add(early_g, g0), []))
        return fly0[3]

    dh, g = _layer_bwd(dh, saved[0], lws[0], early_grads)
    per_layer[0] = _weight_grads(g)
    flyl = _scatter_start("l0b", swap_add(late_g, g), [])
    groups = [(1, BIG, *_scatter_wait("l1", fly1[0], fly1[1], fly1[2], flyl[3])),
              (0, early_g, *_scatter_wait("l0", fly0[0], fly0[1], fly0[2], flyl[3]))]
    g_big = {}

    def finish(groups):
        for l, names, parts, slots in groups:
            sums = [_sum_reduced(pt, sl, shard_arr, c_arr, l, g_big.get(n)) for n, pt, sl in zip(names, parts, slots)]
            g_big.update(zip(names, _join_halves(sums, l)))

    delta, new_m, new_v, grad_w = {}, {}, {}, {}

    def update(names):
        for n in names:
            shp = w[n].shape
            two_d = lambda a: a.reshape(-1, shp[-1])
            d_, m_, v_ = _adamw(two_d(w[n]), two_d(g_big[n]), two_d(m[n]), two_d(v[n]))
            delta[n], new_m[n], new_v[n], grad_w[n] = d_.reshape(shp), m_.reshape(shp), v_.reshape(shp), g_big[n]

    finish(groups)
    update(early_g)
    finish([(0, late_g, *_scatter_wait("l0b", flyl[0], flyl[1], flyl[2], delta[early_g[0]]))])
    update(late_g)
    grad_x = dh
    grads = {n: jnp.stack([per_layer[l][n] for l in range(DEPTH)], axis=0) for n in SMALL if n != 'norm_final'}
    grads['norm_final'] = dnf.reshape(-1)

    rep = [n for n in SMALL if n not in CONVS]
    names = rep + CONVS
    vec = _pack([grads[n] for n in names] + [loss[0, :1]])
    total = _sum_slots("small_sum", _exchange_small(vec))
    parts_small = _unpack(total, [grads[n].shape for n in names] + [(1,)])
    g_small = dict(zip(names, parts_small[:-1]))
    loss_out = parts_small[-1].reshape(())
    for n in CONVS:
        width = w[n].shape[-1]
        g_small[n] = lax.dynamic_slice_in_dim(g_small[n], shard * width, width, axis=2)

    shapes = [w[n].shape for n in SMALL]
    d_, m_, v_ = _adamw(_pack([w[n] for n in SMALL]), _pack([g_small[n] for n in SMALL]),
                        _pack([m[n] for n in SMALL]), _pack([v[n] for n in SMALL]))
    for n, dd, mm, vv in zip(SMALL, _unpack(d_, shapes), _unpack(m_, shapes), _unpack(v_, shapes)):
        delta[n], new_m[n], new_v[n], grad_w[n] = dd, mm, vv, g_small[n]

    return (loss_out, grad_x[None], *[grad_w[n] for n in WEIGHTS], *[delta[n] for n in WEIGHTS],
            *[new_m[n] for n in WEIGHTS], *[new_v[n] for n in WEIGHTS])
```

```python
import functools
import math

import jax
import jax.numpy as jnp
from jax import lax
from jax.experimental import pallas as pl
from jax.experimental.pallas import tpu as pltpu

F32 = jnp.float32
BF16 = jnp.bfloat16
HI = lax.Precision.HIGH
MESH = pl.DeviceIdType.MESH

LANES = 128
EPS = 1e-6
SG_HEADS, SG_CHUNK = 4, 128
SC_GROUPS, SC_KERNEL = 4, 3
GDN_HEADS, GDN_CONV = 8, 4
GDN_L = 128
GDN_CPS = 8
GDN_HPS = 2
SG_W = SG_HEADS * LANES
SC_W = SC_GROUPS * LANES
GDN_W = GDN_HEADS * LANES
IN_COLS = 2 * SG_W + 3 * SC_W + 4 * GDN_W + 2 * GDN_HEADS
PROJ_W = 7168
CB_U, CB_V = 0, 4
CB_GB, CB_GC, CB_X = 8, 12, 16
CB_Q, CB_K, CB_VV, CB_Z, CB_AB = 20, 28, 36, 44, 52
N_SHARD = 4
N_DEV = 8
DEPTH = 2

ADAM_LR, ADAM_B1, ADAM_B2, ADAM_EPS, ADAM_WD, ADAM_STEP = 0.001, 0.9, 0.999, 1e-08, 0.01, 10

VMEM_LIMIT = 56 << 20

WEIGHTS = ['norm_mix', 'w_in', 'sg_ln_g', 'sg_ln_b', 'sg_w', 'sg_b', 'sc_conv', 'gdn_conv', 'gdn_a_log',
           'gdn_dt_bias', 'gdn_norm', 'out_norm_a', 'out_norm_b', 'w_o', 'norm_ffn', 'w_ff1', 'w_ff2', 'norm_ple',
           'w_ple_gate', 'w_ple_proj', 'norm_final']
BIG = ['w_in', 'w_o', 'w_ff1', 'w_ff2', 'w_ple_gate', 'w_ple_proj']
CONVS = ['sc_conv', 'gdn_conv']
SMALL = [n for n in WEIGHTS if n not in BIG]


def _cparams(sem=None):
    return pltpu.CompilerParams(dimension_semantics=sem, vmem_limit_bytes=VMEM_LIMIT)


def _dot(a, b, dims, prec=None):
    return lax.dot_general(a, b, (dims, ((), ())), precision=prec, preferred_element_type=F32)


def _mm(a, b):
    return _dot(a, b, ((1,), (0,)), HI)


def _mm_nt(a, b):
    return _dot(a, b, ((1,), (1,)), HI)


def _mm_tn(a, b):
    return _dot(a, b, ((0,), (0,)), HI)


@jax.custom_vjp
def _bmm(a, b):
    return _dot(a.astype(BF16), b.astype(BF16), ((1,), (0,)))


def _bmm_fwd(a, b):
    return _bmm(a, b), (a, b)


def _bmm_bwd(res, g):
    a, b = res
    gb = g.astype(BF16)
    return _dot(gb, b.astype(BF16), ((1,), (1,))), _dot(a.astype(BF16), gb, ((0,), (0,)))


_bmm.defvjp(_bmm_fwd, _bmm_bwd)


@jax.custom_vjp
def _bmm_tn(a, b):
    return _dot(a.astype(BF16), b.astype(BF16), ((0,), (0,)))


def _bmm_tn_fwd(a, b):
    return _bmm_tn(a, b), (a, b)


def _bmm_tn_bwd(res, g):
    a, b = res
    gb = g.astype(BF16)
    return _dot(b.astype(BF16), gb, ((1,), (1,))), _dot(a.astype(BF16), gb, ((1,), (0,)))


_bmm_tn.defvjp(_bmm_tn_fwd, _bmm_tn_bwd)


def _sigmoid(x):
    return 1.0 / (1.0 + jnp.exp(-x))


def _silu(x):
    return x * _sigmoid(x)


def _gelu(x):
    c = math.sqrt(2.0 / math.pi)
    return 0.5 * x * (1.0 + jnp.tanh(c * (x + 0.044715 * (x * x * x))))


def _softplus(x):
    return jnp.maximum(x, 0.0) + jnp.log(1.0 + jnp.exp(-jnp.abs(x)))


def _rms(x, g):
    return x * lax.rsqrt(jnp.mean(x * x, axis=-1, keepdims=True) + EPS) * g


def _roll_rows(x, shift):
    return pltpu.roll(x, shift % x.shape[0], 0)


@functools.partial(jax.custom_vjp, nondiff_argnums=(1,))
def _shift_down(x, j):
    row = lax.broadcasted_iota(jnp.int32, x.shape, 0)
    return jnp.where(row >= j, _roll_rows(x, j), 0.0)


def _shift_down_fwd(x, j):
    return _shift_down(x, j), None


def _shift_down_bwd(j, _, dy):
    row = lax.broadcasted_iota(jnp.int32, dy.shape, 0)
    return (jnp.where(row < dy.shape[0] - j, _roll_rows(dy, -j), 0.0),)


_shift_down.defvjp(_shift_down_fwd, _shift_down_bwd)


def _causal_conv(x, taps):
    k = len(taps)
    y = taps[k - 1] * x
    for j in range(k - 1):
        y = y + taps[j] * _shift_down(x, k - 1 - j)
    return y


class _In:
    def __init__(self, arr, block, imap, shared=False, gshape=None, gmap=None):
        self.arr, self.block, self.imap, self.shared = arr, block, imap, shared
        self.gshape = arr.shape if gshape is None else gshape
        self.gmap = imap if gmap is None else gmap


def _fused(name, fn, n, ins, outs, douts=None, need=None, gdt=None, add=None):
    n_in, n_out = len(ins), len(outs)
    in_specs = [pl.BlockSpec(s.block, s.imap) for s in ins]
    out_specs = [pl.BlockSpec(bs, im) for _, _, bs, im in outs]
    if douts is None:
        def body(*refs):
            res = fn(*[r[...] for r in refs[:n_in]])
            for r, v in zip(refs[n_in:], res):
                r[...] = v.astype(r.dtype)

        return pl.pallas_call(
            body, name=name, grid=(n,), in_specs=in_specs, out_specs=out_specs,
            out_shape=[jax.ShapeDtypeStruct(s, d) for s, d, _, _ in outs],
            compiler_params=_cparams(("arbitrary",)),
        )(*[s.arr for s in ins])

    gdt = list(gdt) if gdt is not None else [F32] * n_in
    add = dict(add or {})
    gidx = [i for i in range(n_in) if need[i]]
    aidx = [i for i in gidx if i in add]

    def body(*refs):
        in_refs, d_refs = refs[:n_in], refs[n_in:n_in + n_out]
        a_refs = dict(zip(aidx, refs[n_in + n_out:n_in + n_out + len(aidx)]))
        g_refs = refs[n_in + n_out + len(aidx):]
        vals = [r[...] for r in in_refs]

        def f(*dv):
            full = list(vals)
            for i, v in zip(gidx, dv):
                full[i] = v
            return tuple(o.astype(F32) for o in fn(*full))

        _, vjp = jax.vjp(f, *[vals[i].astype(F32) for i in gidx])
        grads = vjp(tuple(r[...].astype(F32) for r in d_refs))
        for i, g_ref, g in zip(gidx, g_refs, grads):
            if i in a_refs:
                g = g + a_refs[i][...].astype(F32)
            if ins[i].shared:
                period = n if ins[i].shared is True else ins[i].shared

                @pl.when(pl.program_id(0) % period == 0)
                def _():
                    g_ref[...] = jnp.zeros_like(g_ref)
                g_ref[...] += g.astype(g_ref.dtype)
            else:
                g_ref[...] = g.astype(g_ref.dtype)

    g_specs = [pl.BlockSpec(ins[i].block, ins[i].gmap) for i in gidx]
    g_shape = [jax.ShapeDtypeStruct(ins[i].gshape, gdt[i]) for i in gidx]
    res = pl.pallas_call(
        body, name=name, grid=(n,), in_specs=in_specs + out_specs + [g_specs[gidx.index(i)] for i in aidx],
        out_specs=g_specs, out_shape=g_shape,
        compiler_params=_cparams(("arbitrary",)),
    )(*[s.arr for s in ins], *douts, *[add[i] for i in aidx])
    full = [None] * n_in
    for i, g in zip(gidx, res):
        full[i] = g
    return full


def _row_in(a, tm, cb=None):
    if cb is None:
        return _In(a, (tm, a.shape[1]), lambda i: (i, 0))
    return _In(a, (tm, LANES), lambda i: (i, cb), gshape=(a.shape[0], LANES), gmap=lambda i: (i, 0))


def _row_shared(a):
    return _In(a, a.shape, lambda i: (0, 0), shared=True)


def _row_out(t, w, dt, tm):
    return ((t, w), dt, (tm, w), lambda i: (i, 0))


def _col_in(a, base, nblk):
    t = a.shape[0]
    return _In(a, (t, LANES), lambda i: (0, base + i), gshape=(t, nblk * LANES), gmap=lambda i: (0, i))


def _col_par(a):
    return _In(a, (None,) + a.shape[1:], lambda i: (i, 0, 0))


def _col_out(t, w, dt, base=0):
    return ((t, w), dt, (t, LANES), lambda i: (0, base + i))


def _matmul(name, a, b, mode, m, n, k, tm, tn, tk, b_spec=None, epilogue=None, extras=(), outs=None, out_dtype=F32,
            n_outer=False):
    tm, tn, tk = min(tm, m), min(tn, n), min(tk, k)
    assert m % tm == 0 and n % tn == 0 and k % tk == 0, (name, m, n, k, tm, tn, tk)
    nk = k // tk
    a_spec = (pl.BlockSpec((tk, tm), lambda i, j, kk: (kk, i)) if mode == "tn"
              else pl.BlockSpec((tm, tk), lambda i, j, kk: (i, kk)))
    if b_spec is None:
        b_spec = (pl.BlockSpec((tn, tk), lambda i, j, kk: (j, kk)) if mode == "nt"
                  else pl.BlockSpec((tk, tn), lambda i, j, kk: (kk, j)))
    dims = {"nn": ((1,), (0,)), "nt": ((1,), (1,)), "tn": ((0,), (0,))}[mode]
    if outs is None:
        outs = [((m, n), out_dtype, (tm, tn), lambda i, j, kk: (i, j))]
    if epilogue is None:
        epilogue = lambda acc: (acc,)
    n_ex = len(extras)

    def body(*refs):
        a_ref, b_ref = refs[0], refs[1]
        ex_refs = refs[2:2 + n_ex]
        o_refs = refs[2 + n_ex:2 + n_ex + len(outs)]
        part = _dot(a_ref[...].astype(BF16), b_ref[...].astype(BF16), dims)

        def finish(acc):
            for r, v in zip(o_refs, epilogue(acc, *[e[...] for e in ex_refs])):
                r[...] = v.astype(r.dtype)

        if nk == 1:
            finish(part)
        else:
            acc_ref = refs[-1]
            kk = pl.program_id(2)

            @pl.when(kk == 0)
            def _():
                acc_ref[...] = part

            @pl.when(kk > 0)
            def _():
                acc_ref[...] += part

            @pl.when(kk == nk - 1)
            def _():
                finish(acc_ref[...])

    ex_specs = [pl.BlockSpec((tm, tn), lambda i, j, kk: (i, j)) for _ in extras]
    out_specs = [pl.BlockSpec(bs, im) for _, _, bs, im in outs]
    in_specs = [a_spec, b_spec] + ex_specs
    grid = (m // tm, n // tn, nk)
    if n_outer:
        swap = lambda sp: pl.BlockSpec(sp.block_shape, functools.partial(lambda f, j, i, kk: f(i, j, kk), sp.index_map))
        in_specs, out_specs, grid = [swap(sp) for sp in in_specs], [swap(sp) for sp in out_specs], (n // tn, m // tm, nk)
    res = pl.pallas_call(
        body, name=name, grid=grid,
        in_specs=in_specs,
        out_specs=out_specs,
        out_shape=[jax.ShapeDtypeStruct(s, d) for s, d, _, _ in outs],
        scratch_shapes=[pltpu.VMEM((tm, tn), F32)] if nk > 1 else [],
        compiler_params=_cparams(("parallel", "parallel", "arbitrary")),
    )(a, b, *extras)
    return res if len(res) > 1 else res[0]


def _fn_sgu(u_pre, v_pre, ln_g, ln_b, w, bb, na):
    t = u_pre.shape[0]
    u = _gelu(u_pre)
    v = _gelu(v_pre)
    mu = jnp.mean(v, axis=-1, keepdims=True)
    vc = v - mu
    vh = vc * lax.rsqrt(jnp.mean(vc * vc, axis=-1, keepdims=True) + EPS) * ln_g + ln_b
    ri = lax.broadcasted_iota(jnp.int32, w.shape, 0)
    ci = lax.broadcasted_iota(jnp.int32, w.shape, 1)
    wc = jnp.where(ri >= ci, w, 0.0)
    f = jnp.concatenate([_bmm(wc, vh[c * SG_CHUNK:(c + 1) * SG_CHUNK]) + bb for c in range(t // SG_CHUNK)], axis=0)
    return (_rms(u * f, na),)


def _fn_sconv(gb, gc, xin, w0, w1, w2, nb):
    return (_rms(gb * _causal_conv(gc * xin, (w0, w1, w2)), nb),)


def _fn_gdn_qk(pre, w0, w1, w2, w3):
    a = _silu(_causal_conv(pre, (w0, w1, w2, w3)))
    return (a * lax.rsqrt(jnp.sum(a * a, axis=-1, keepdims=True) + EPS),)


def _fn_gdn_v(pre, w0, w1, w2, w3):
    return (_silu(_causal_conv(pre, (w0, w1, w2, w3))),)


def _fn_gdn_gates(ab, a_log, dt_bias):
    lane = lax.broadcasted_iota(jnp.int32, ab.shape, 1)
    g = -jnp.exp(a_log) * _softplus(ab + dt_bias)
    return (jnp.where(lane < GDN_HEADS, g, jnp.where(lane < 2 * GDN_HEADS, _sigmoid(ab), 0.0)),)


def _solve_unit_lower_impl(mats):
    n = mats[0].shape[0]
    ri = lax.broadcasted_iota(jnp.int32, (n, n), 0)
    ci = lax.broadcasted_iota(jnp.int32, (n, n), 1)
    ts = [(ri == ci).astype(F32)] * len(mats)
    sh = 0
    while (1 << sh) < n:
        rb = jnp.right_shift(ri, sh)
        cb = jnp.right_shift(ci, sh)
        off = ((rb & 1) == 1) & (cb == rb - 1)
        us = [_mm(t, jnp.where(off, a, 0.0)) for t, a in zip(ts, mats)]
        ts = [t - _mm(u, t) for t, u in zip(ts, us)]
        sh += 1
    return tuple(ts)


def _solve_cotangents(ts, dts):
    us = [_mm_nt(dt, t) for t, dt in zip(ts, dts)]
    return tuple(-_mm_tn(t, u) for t, u in zip(ts, us))


@jax.custom_vjp
def _solve_unit_lower(mats):
    return _solve_unit_lower_impl(mats)


def _solve_unit_lower_fwd(mats):
    ts = _solve_unit_lower_impl(mats)
    return ts, ts


def _solve_unit_lower_bwd(ts, dts):
    return (_solve_cotangents(ts, dts),)


_solve_unit_lower.defvjp(_solve_unit_lower_fwd, _solve_unit_lower_bwd)


@jax.custom_vjp
def _solved_unit_lower(mats, ts):
    return ts


def _solved_unit_lower_fwd(mats, ts):
    return ts, ts


def _solved_unit_lower_bwd(ts, dts):
    return _solve_cotangents(ts, dts), tuple(jnp.zeros_like(t) for t in ts)


_solved_unit_lower.defvjp(_solved_unit_lower_fwd, _solved_unit_lower_bwd)


def _fn_gdn_wy(q, k, v, gates, pick_g, pick_b, t_saved=None):
    n, dk = GDN_L, q.shape[1]
    rows = [slice(c * n, (c + 1) * n) for c in range(q.shape[0] // n)]
    ri = lax.broadcasted_iota(jnp.int32, (n, n), 0)
    ci = lax.broadcasted_iota(jnp.int32, (n, n), 1)
    incl = ri >= ci
    eye = (ri == ci).astype(F32)
    last = lax.broadcasted_iota(jnp.int32, (n, 1), 0) == (n - 1)
    qs = [q[r] * (dk ** -0.5) for r in rows]
    ks = [k[r] for r in rows]
    gcs, betas, d_incl = [], [], []
    for r in rows:
        g = jnp.sum(gates[r] * pick_g, axis=1, keepdims=True)
        betas.append(jnp.sum(gates[r] * pick_b, axis=1, keepdims=True))
        g_row = jnp.sum(eye * g, axis=0, keepdims=True)
        gc = jnp.sum(jnp.where(incl, g_row, 0.0), axis=1, keepdims=True)
        gc_row = jnp.sum(eye * gc, axis=0, keepdims=True)
        gcs.append(gc)
        d_incl.append(jnp.where(incl, jnp.exp(jnp.where(incl, gc - gc_row, 0.0)), 0.0))
    kbs = [kk * b for kk, b in zip(ks, betas)]
    mats = tuple(_mm_nt(kb, kk) * jnp.where(ri > ci, d, 0.0) for kb, kk, d in zip(kbs, ks, d_incl))
    ts = _solve_unit_lower(mats) if t_saved is None else _solved_unit_lower(mats, tuple(t_saved[r] for r in rows))
    egs = [jnp.exp(gc) for gc in gcs]
    values = [_mm(t, v[r] * b) for t, r, b in zip(ts, rows, betas)]
    kcds = [_mm(t, kb * eg) for t, kb, eg in zip(ts, kbs, egs)]
    intras = [_mm_nt(qq, kk) * d for qq, kk, d in zip(qs, ks, d_incl)]
    g_lasts = [jnp.sum(jnp.where(last, gc, 0.0), axis=0, keepdims=True) for gc in gcs]
    qes = [qq * eg for qq, eg in zip(qs, egs)]
    kts = [kk * jnp.exp(gl - gc) for kk, gl, gc in zip(ks, g_lasts, gcs)]
    carries = [jnp.broadcast_to(jnp.exp(gl), (8, LANES)) for gl in g_lasts]
    cat = lambda parts: jnp.concatenate(parts, axis=0)
    res = (cat(values), cat(kcds), cat(qes), cat(kts), cat(intras), cat(carries))
    return res + (cat(ts),) if t_saved is None else res


def _gdn_steps(states, operands):
    v_new = [value - _bmm(kcd, s) for s, (value, kcd, _, _, _, _) in zip(states, operands)]
    outs = [_bmm(qe, s) + _bmm(intra, vn) for s, vn, (_, _, qe, _, intra, _) in zip(states, v_new, operands)]
    new = [s * carry + _bmm_tn(kt, vn) for s, vn, (_, _, _, kt, _, carry) in zip(states, v_new, operands)]
    return tuple(new), tuple(outs)


def _gdn_post(o, z, nrm):
    return _rms(o, nrm) * _silu(z)


def _gdn_wy_ins(q, k, v, gates, t_saved=None):
    t = q.shape[0]
    rb = min(GDN_CPS * GDN_L, t)
    hd = GDN_HEADS
    lane = jnp.arange(LANES)[None, None, :]
    pick_g = (lane == jnp.arange(hd)[:, None, None]).astype(F32)
    pick_b = (lane == jnp.arange(hd)[:, None, None] + hd).astype(F32)
    blk = lambda a: _In(a, (rb, LANES), lambda i: (i // hd, i % hd))
    par = lambda a: _In(a, (None, 1, LANES), lambda i: (i % hd, 0, 0))
    ins = [blk(q), blk(k), blk(v), _In(gates, (rb, LANES), lambda i: (i // hd, 0), shared=hd), par(pick_g),
           par(pick_b)]
    wide = lambda dt: ((t, GDN_W), dt, (rb, LANES), lambda i: (i // hd, i % hd))
    carry = ((hd, 8 * (t // GDN_L), LANES), F32, (None, 8 * (rb // GDN_L), LANES), lambda i: (i % hd, i // hd, 0))
    outs = [wide(F32), wide(BF16), wide(BF16), wide(BF16), wide(BF16), carry]
    if t_saved is None:
        outs.append(wide(F32))
    else:
        ins.append(blk(t_saved))
    return (t // rb) * hd, ins, outs


def _gdn_scan_specs(t):
    wide = GDN_HPS * LANES
    once = pl.Buffered(1)
    col = lambda base: pl.BlockSpec((t, wide), lambda h: (0, base // GDN_HPS + h), pipeline_mode=once)
    carry = pl.BlockSpec((GDN_HPS, 8 * (t // GDN_L), LANES), lambda h: (h, 0, 0))
    par = pl.BlockSpec((GDN_HPS, 1, LANES), lambda h: (h, 0, 0))
    return col, carry, par


def _head_operands(refs, cy_ref, c, i):
    rows = pl.ds(pl.multiple_of(c * GDN_L, GDN_L), GDN_L)
    lanes = pl.ds(i * LANES, LANES)
    return tuple(r[rows, lanes] for r in refs) + (cy_ref[i, pl.ds(pl.multiple_of(c * 8, 8), 1), :],)


def _gdn_scan_fwd(wy, proj, nrm):
    t = wy[0].shape[0]
    nc = t // GDN_L
    heads = range(GDN_HPS)

    def body(val_ref, kcd_ref, qe_ref, kt_ref, in_ref, cy_ref, z_ref, n_ref, y_ref, o_scr):
        big = (val_ref, kcd_ref, qe_ref, kt_ref, in_ref)

        def step(c, states):
            rows = pl.ds(pl.multiple_of(c * GDN_L, GDN_L), GDN_L)
            states, outs = _gdn_steps(states, [_head_operands(big, cy_ref, c, i) for i in heads])
            for i in heads:
                o_scr[rows, pl.ds(i * LANES, LANES)] = outs[i]
            return states

        lax.fori_loop(0, nc, step, tuple(jnp.zeros((LANES, LANES), F32) for _ in heads))
        for i in heads:
            lanes = pl.ds(i * LANES, LANES)
            y_ref[:, lanes] = _gdn_post(o_scr[:, lanes], z_ref[:, lanes], n_ref[i]).astype(y_ref.dtype)

    col, carry, par = _gdn_scan_specs(t)
    return pl.pallas_call(
        body, name="gdn_scan_fwd", grid=(GDN_HEADS // GDN_HPS,),
        in_specs=[col(0)] * 5 + [carry, col(CB_Z), par],
        out_specs=col(0), out_shape=jax.ShapeDtypeStruct((t, GDN_W), BF16),
        scratch_shapes=[pltpu.VMEM((t, GDN_HPS * LANES), F32)],
        compiler_params=_cparams(("arbitrary",)),
    )(*wy, proj, nrm)


def _gdn_scan_bwd(wy, proj, nrm, dy, dy_base):
    t = wy[0].shape[0]
    nc = t // GDN_L
    heads = range(GDN_HPS)

    def body(val_ref, kcd_ref, qe_ref, kt_ref, in_ref, cy_ref, z_ref, n_ref, dy_ref,
             dval_ref, dkcd_ref, dqe_ref, dkt_ref, din_ref, dcy_ref, dz_ref, dn_ref, o_scr, s_scr):
        big = (val_ref, kcd_ref, qe_ref, kt_ref, in_ref)
        dbig = (dval_ref, dkcd_ref, dqe_ref, dkt_ref, din_ref)

        def step(c, states):
            rows = pl.ds(pl.multiple_of(c * GDN_L, GDN_L), GDN_L)
            for i in heads:
                s_scr[i, c] = states[i]
            states, outs = _gdn_steps(states, [_head_operands(big, cy_ref, c, i) for i in heads])
            for i in heads:
                o_scr[rows, pl.ds(i * LANES, LANES)] = outs[i]
            return states

        zeros = tuple(jnp.zeros((LANES, LANES), F32) for _ in heads)
        lax.fori_loop(0, nc, step, zeros)
        for i in heads:
            lanes = pl.ds(i * LANES, LANES)
            _, vjp_post = jax.vjp(_gdn_post, o_scr[:, lanes], z_ref[:, lanes], n_ref[i])
            do, dz, dn = vjp_post(dy_ref[:, lanes].astype(F32))
            dz_ref[:, lanes] = dz.astype(dz_ref.dtype)
            dn_ref[i] = dn
            o_scr[:, lanes] = do
        dcy_ref[...] = jnp.zeros_like(dcy_ref)

        def rstep(k, dstates):
            c = nc - 1 - k
            rows = pl.ds(pl.multiple_of(c * GDN_L, GDN_L), GDN_L)
            ops = [tuple(o.astype(F32) for o in _head_operands(big, cy_ref, c, i)) for i in heads]
            _, vjp_c = jax.vjp(_gdn_steps, tuple(s_scr[i, c] for i in heads), ops)
            dstates, dops = vjp_c((dstates, tuple(o_scr[rows, pl.ds(i * LANES, LANES)] for i in heads)))
            for i in heads:
                for r, g in zip(dbig, dops[i][:5]):
                    r[rows, pl.ds(i * LANES, LANES)] = g.astype(r.dtype)
                dcy_ref[i, pl.ds(pl.multiple_of(c * 8, 8), 1), :] = dops[i][5]
            return dstates

        lax.fori_loop(0, nc, rstep, zeros)

    col, carry, par = _gdn_scan_specs(t)
    wide = jax.ShapeDtypeStruct((t, GDN_W), BF16)
    return pl.pallas_call(
        body, name="gdn_scan_bwd", grid=(GDN_HEADS // GDN_HPS,),
        in_specs=[col(0)] * 5 + [carry, col(CB_Z), par, col(dy_base)],
        out_specs=[col(0)] * 5 + [carry, col(0), par],
        out_shape=[wide] * 5 + [jax.ShapeDtypeStruct(wy[5].shape, F32), wide,
                                jax.ShapeDtypeStruct((GDN_HEADS, 1, LANES), F32)],
        scratch_shapes=[pltpu.VMEM((t, GDN_HPS * LANES), F32), pltpu.VMEM((GDN_HPS, nc, LANES, LANES), F32)],
        compiler_params=_cparams(("arbitrary",)),
    )(*wy, proj, nrm, dy)


TM = 512
TR = 256


def _rms_fwd(name, h, g):
    t, d = h.shape
    tm = min(TR, t)
    return _fused(name, lambda hb, gb: (_rms(hb, gb),), t // tm, [_row_in(h, tm), _row_shared(g)],
                  [_row_out(t, d, BF16, tm)])[0]


def _rms_bwd(name, h, g, dxn, dh_next):
    t, d = h.shape
    tm = min(TR, t)
    dh, dg = _fused(name, lambda hb, gb: (_rms(hb, gb),), t // tm, [_row_in(h, tm), _row_shared(g)],
                    [_row_out(t, d, F32, tm)], douts=[dxn], need=[True, True], add={0: dh_next})
    return dh, dg


def _mixer_ins(proj, lw):
    sgu = [_col_in(proj, CB_U, SG_HEADS), _col_in(proj, CB_V, SG_HEADS), _col_par(lw['sg_ln_g']),
           _col_par(lw['sg_ln_b']), _col_par(lw['sg_w']), _col_par(lw['sg_bb']), _col_par(lw['out_norm_a'])]
    sconv = [_col_in(proj, CB_GB, SC_GROUPS), _col_in(proj, CB_GC, SC_GROUPS), _col_in(proj, CB_X, SC_GROUPS)] + \
            [_col_par(w) for w in lw['sc_taps']] + [_col_par(lw['out_norm_b'])]
    gq = [_col_in(proj, CB_Q, GDN_HEADS)] + [_col_par(w) for w in lw['q_taps']]
    gk = [_col_in(proj, CB_K, GDN_HEADS)] + [_col_par(w) for w in lw['k_taps']]
    gv = [_col_in(proj, CB_VV, GDN_HEADS)] + [_col_par(w) for w in lw['v_taps']]
    return sgu, sconv, gq, gk, gv


def _gates_ins(proj, lw, tm):
    return [_row_in(proj, tm, CB_AB), _row_shared(lw['a_log_row']), _row_shared(lw['dt_bias_row'])]


def _layer_fwd(h, p_l, lw, late=None):
    t, d = h.shape
    xn = _rms_fwd("rms_fwd", h, lw['norm_mix'])
    proj = _matmul("proj_fwd", xn, lw['w_in'], "nn", t, PROJ_W, d, TM, 1024, d)
    sgu, sconv, gq, gk, gv = _mixer_ins(proj, lw)
    ya = _fused("sgu_fwd", _fn_sgu, SG_HEADS, sgu, [_col_out(t, SG_W, BF16)])[0]
    yb = _fused("sconv_fwd", _fn_sconv, SC_GROUPS, sconv, [_col_out(t, SC_W, BF16)])[0]
    q = _fused("gdn_q_fwd", _fn_gdn_qk, GDN_HEADS, gq, [_col_out(t, GDN_W, F32)])[0]
    k = _fused("gdn_k_fwd", _fn_gdn_qk, GDN_HEADS, gk, [_col_out(t, GDN_W, F32)])[0]
    v = _fused("gdn_v_fwd", _fn_gdn_v, GDN_HEADS, gv, [_col_out(t, GDN_W, F32)])[0]
    tm = min(TR, t)
    gates = _fused("gdn_gates_fwd", _fn_gdn_gates, t // tm, _gates_ins(proj, lw, tm),
                   [_row_out(t, LANES, F32, tm)])[0]
    n_wy, wy_ins, wy_outs = _gdn_wy_ins(q, k, v, gates)
    wy = _fused("gdn_wy_fwd", _fn_gdn_wy, n_wy, wy_ins, wy_outs)
    wy, wy_t = wy[:6], wy[6]
    yc = _gdn_scan_fwd(wy, proj, lw['gdn_norm'])
    ycat = jnp.concatenate([ya, yb, yc], axis=1)
    if late is not None:
        late(ycat)
    dff = lw['w_ff2'].shape[0]
    h2 = _matmul("wo_fwd", ycat, lw['w_o'], "nn", t, d, d, TM, 1024, d,
                 epilogue=lambda acc, hb: (hb + acc,), extras=(h,))
    hn = _rms_fwd("rms_fwd", h2, lw['norm_ffn'])
    per = d // 1024
    s, r = _matmul("ff1_fwd", hn, lw['w_ff1'], "nn", t, dff, d, TM, 1024, d,
                   b_spec=pl.BlockSpec((None, d, 1024), lambda i, j, kk: (j // per, 0, j % per)),
                   epilogue=lambda acc: (jnp.maximum(acc, 0.0), jnp.square(jnp.maximum(acc, 0.0))),
                   outs=[((t, dff), BF16, (min(TM, t), 1024), lambda i, j, kk: (i, j))] * 2)
    h3 = _matmul("ff2_fwd", r, lw['w_ff2'], "nn", t, d, dff, TM, 1024, 4096,
                 epilogue=lambda acc, hb: (hb + acc,), extras=(h2,))
    hn2 = _rms_fwd("rms_fwd", h3, lw['norm_ple'])
    pp = _matmul("ple_proj_fwd", p_l, lw['w_ple_proj'], "nn", t, d, p_l.shape[1], TM, 1024, p_l.shape[1])

    def gate_epilogue(acc, hb, ppb):
        sg = _sigmoid(acc)
        return hb + ppb * sg, sg

    h4, gate = _matmul("ple_gate_fwd", hn2, lw['w_ple_gate'], "nn", t, d, d, TM, 1024, d, epilogue=gate_epilogue,
                       extras=(h3, pp), outs=[((t, d), F32, (min(TM, t), 1024), lambda i, j, kk: (i, j))] * 2)
    saved = dict(h=h, xn=xn, proj=proj, q=q, k=k, v=v, gates=gates, ycat=ycat, h2=h2, hn=hn, s=s, r=r, h3=h3,
                 hn2=hn2, pp=pp, gate=gate, p=p_l, wy=wy, wy_t=wy_t)
    return h4, saved


def _layer_bwd(dh4, sv, lw, early=None):
    t, d = dh4.shape
    dff = lw['w_ff2'].shape[0]
    tm = min(TR, t)
    g = {}
    dacc, dpp = _fused("ple_bwd_gate", lambda dh, pp, gt: (dh * pp * gt * (1.0 - gt), dh * gt), t // tm,
                       [_row_in(dh4, tm), _row_in(sv['pp'], tm), _row_in(sv['gate'], tm)],
                       [_row_out(t, d, BF16, tm)] * 2)
    g['w_ple_gate'] = _matmul("dw_ple_gate", sv['hn2'], dacc, "tn", d, d, t, TM, 1024, t, out_dtype=BF16,
                              n_outer=True)
    g['w_ple_proj'] = _matmul("dw_ple_proj", sv['p'], dpp, "tn", sv['p'].shape[1], d, t, TM, 1024, t,
                              out_dtype=BF16, n_outer=True)
    dhn2 = _matmul("dx_ple_gate", dacc, lw['w_ple_gate'], "nt", t, d, d, TM, 1024, d)
    dh3, g['norm_ple'] = _rms_bwd("rms_bwd", sv['h3'], lw['norm_ple'], dhn2, dh4)
    da = _matmul("dx_ff2", dh3, lw['w_ff2'], "nt", t, dff, d, TM, 1024, d,
                 epilogue=lambda acc, sb: (acc * (2.0 * sb.astype(F32)),), extras=(sv['s'],),
                 outs=[((t, dff), BF16, (min(TM, t), 1024), lambda i, j, kk: (i, j))])
    g['w_ff2'] = _matmul("dw_ff2", sv['r'], dh3, "tn", dff, d, t, TM, 512, t, out_dtype=BF16, n_outer=True)
    per = d // 1024
    dhn = _matmul("dx_ff1", da, lw['w_ff1'], "nt", t, d, dff, TM, 1024, d,
                  b_spec=pl.BlockSpec((None, 1024, d), lambda i, j, kk: (kk, j, 0)))
    g['w_ff1'] = _matmul("dw_ff1", sv['hn'], da, "tn", d, dff, t, TM, 1024, t, n_outer=True,
                         outs=[((N_SHARD, d, d), BF16, (None, TM, 1024), lambda i, j, kk: (j // per, i, j % per))])
    norm_ffn = lw['norm_ffn'] if early is None else lw['norm_ffn'] + early(g)[0, 0]
    dh2, g['norm_ffn'] = _rms_bwd("rms_bwd", sv['h2'], norm_ffn, dhn, dh3)
    dycat = _matmul("dx_o", dh2, lw['w_o'], "nt", t, d, d, TM, 1024, d)
    g['w_o'] = _matmul("dw_o", sv['ycat'], dh2, "tn", d, d, t, TM, 512, t, out_dtype=BF16, n_outer=True)
    proj = sv['proj']
    sgu, sconv, gq, gk, gv = _mixer_ins(proj, lw)
    bf2 = [BF16, BF16]
    r_ = _fused("sgu_bwd", _fn_sgu, SG_HEADS, sgu, [_col_out(t, d, F32, 0)], douts=[dycat], need=[True] * 7,
                gdt=bf2 + [F32] * 5)
    du, dv_, g['sg_ln_g'], g['sg_ln_b'], g['sg_w'], g['sg_bb'], g['out_norm_a'] = r_
    r_ = _fused("sconv_bwd", _fn_sconv, SC_GROUPS, sconv, [_col_out(t, d, F32, SG_HEADS)], douts=[dycat],
                need=[True] * 7, gdt=[BF16] * 3 + [F32] * 4)
    dgb, dgc, dxin = r_[:3]
    g['sc_taps'], g['out_norm_b'] = r_[3:6], r_[6]
    r_ = _gdn_scan_bwd(sv['wy'], proj, lw['gdn_norm'], dycat, SG_HEADS + SC_GROUPS)
    dwy, dz, g['gdn_norm'] = r_[:6], r_[6], r_[7]
    n_wy, wy_ins, wy_outs = _gdn_wy_ins(sv['q'], sv['k'], sv['v'], sv['gates'], sv['wy_t'])
    dq, dk, dvv, dgates = _fused("gdn_wy_bwd", _fn_gdn_wy, n_wy, wy_ins, wy_outs, douts=dwy,
                                 need=[True] * 4 + [False] * 3)[:4]
    one = [_col_out(t, GDN_W, F32)]
    r_ = _fused("gdn_q_bwd", _fn_gdn_qk, GDN_HEADS, gq, one, douts=[dq], need=[True] * 5, gdt=[BF16] + [F32] * 4)
    dpq, g['q_taps'] = r_[0], r_[1:]
    r_ = _fused("gdn_k_bwd", _fn_gdn_qk, GDN_HEADS, gk, one, douts=[dk], need=[True] * 5, gdt=[BF16] + [F32] * 4)
    dpk, g['k_taps'] = r_[0], r_[1:]
    r_ = _fused("gdn_v_bwd", _fn_gdn_v, GDN_HEADS, gv, one, douts=[dvv], need=[True] * 5, gdt=[BF16] + [F32] * 4)
    dpv, g['v_taps'] = r_[0], r_[1:]
    dab, g['a_log_row'], g['dt_bias_row'] = _fused(
        "gdn_gates_bwd", _fn_gdn_gates, t // tm, _gates_ins(proj, lw, tm), [_row_out(t, LANES, F32, tm)],
        douts=[dgates], need=[True] * 3, gdt=[BF16, F32, F32])
    pad = jnp.zeros((t, PROJ_W - (CB_AB + 1) * LANES), BF16)
    dproj = jnp.concatenate([du, dv_, dgb, dgc, dxin, dpq, dpk, dpv, dz, dab, pad], axis=1)
    dxn = _matmul("dx_in", dproj, lw['w_in'], "nt", t, d, PROJ_W, TM, 1024, PROJ_W // 2)
    g['w_in'] = _matmul("dw_in", sv['xn'], dproj, "tn", d, PROJ_W, t, TM, 1024, t, out_dtype=BF16, n_outer=True)
    dh, g['norm_mix'] = _rms_bwd("rms_bwd", sv['h'], lw['norm_mix'], dxn, dh2)
    return dh, g


def _loss_grad(h, g, tgt):
    t, d = h.shape
    tm = min(TR, t)

    def body(h_ref, g_ref, t_ref, loss_ref, dh_ref, dg_ref):
        y, vjp = jax.vjp(_rms, h_ref[...], g_ref[...])
        e = y - t_ref[...]
        dh, dg = vjp(e * (1.0 / d))

        @pl.when(pl.program_id(0) == 0)
        def _():
            loss_ref[...] = jnp.zeros_like(loss_ref)
            dg_ref[...] = jnp.zeros_like(dg_ref)

        loss_ref[...] += jnp.sum(jnp.sum(e * e, axis=1, keepdims=True), axis=0, keepdims=True) * (0.5 / d)
        dh_ref[...] = dh
        dg_ref[...] += dg

    row = pl.BlockSpec((tm, d), lambda i: (i, 0))
    return pl.pallas_call(
        body, name="loss_grad", grid=(t // tm,),
        in_specs=[row, pl.BlockSpec((1, d), lambda i: (0, 0)), row],
        out_specs=[pl.BlockSpec((1, LANES), lambda i: (0, 0)), row, pl.BlockSpec((1, d), lambda i: (0, 0))],
        out_shape=[jax.ShapeDtypeStruct((1, LANES), F32), jax.ShapeDtypeStruct((t, d), F32),
                   jax.ShapeDtypeStruct((1, d), F32)],
        compiler_params=_cparams(("arbitrary",)),
    )(h, g, tgt)


def _big_weights(full):
    lw = {}
    if 'w_in' in full:
        d = full['w_in'].shape[1]
        w_in = jnp.transpose(full['w_in'], (1, 0, 2)).reshape(d, IN_COLS)
        lw['w_in'] = jnp.pad(w_in, ((0, 0), (0, PROJ_W - IN_COLS)))
    if 'w_o' in full:
        lw['w_o'] = full['w_o'].reshape(-1, full['w_o'].shape[-1])
    if 'w_ff1' in full:
        lw['w_ff1'] = full['w_ff1']
    if 'w_ff2' in full:
        lw['w_ff2'] = full['w_ff2'].reshape(-1, full['w_ff2'].shape[-1])
    if 'w_ple_gate' in full:
        lw['w_ple_gate'] = full['w_ple_gate'].reshape(-1, full['w_ple_gate'].shape[-1])
    if 'w_ple_proj' in full:
        wpp = full['w_ple_proj']
        lw['w_ple_proj'] = jnp.transpose(wpp, (1, 0, 2)).reshape(wpp.shape[1], -1)
    return lw


def _layer_weights(l, full, convs, small):
    d = small['norm_mix'].shape[-1]
    lw = _big_weights(full)
    for n in ('norm_mix', 'norm_ffn', 'norm_ple'):
        lw[n] = small[n][l].reshape(1, d)
    lw['sg_ln_g'] = small['sg_ln_g'][l].reshape(SG_HEADS, 1, LANES)
    lw['sg_ln_b'] = small['sg_ln_b'][l].reshape(SG_HEADS, 1, LANES)
    lw['sg_w'] = small['sg_w'][l]
    lw['sg_bb'] = jnp.broadcast_to(small['sg_b'][l][:, :, None], (SG_HEADS, SG_CHUNK, LANES))
    lw['out_norm_a'] = small['out_norm_a'][l].reshape(SG_HEADS, 1, LANES)
    lw['out_norm_b'] = small['out_norm_b'][l].reshape(SC_GROUPS, 1, LANES)
    lw['gdn_norm'] = jnp.broadcast_to(small['gdn_norm'][l].reshape(1, 1, LANES), (GDN_HEADS, 1, LANES))
    lw['a_log_row'] = jnp.pad(small['gdn_a_log'][l].reshape(1, GDN_HEADS), ((0, 0), (0, LANES - GDN_HEADS)))
    lw['dt_bias_row'] = jnp.pad(small['gdn_dt_bias'][l].reshape(1, GDN_HEADS), ((0, 0), (0, LANES - GDN_HEADS)))
    sc = convs['sc_conv'][l]
    lw['sc_taps'] = [sc[:, j:j + 1, :] for j in range(SC_KERNEL)]
    gc = jnp.transpose(convs['gdn_conv'][l], (1, 0, 2)).reshape(GDN_CONV, 3 * GDN_W)
    for i, nm in enumerate(('q_taps', 'k_taps', 'v_taps')):
        part = gc[:, i * GDN_W:(i + 1) * GDN_W].reshape(GDN_CONV, GDN_HEADS, 1, LANES)
        lw[nm] = [part[j] for j in range(GDN_CONV)]
    return lw


def _shard_major(name, g):
    if name == 'w_in':
        d = g.shape[0]
        return jnp.transpose(g[:, :IN_COLS].reshape(d, N_SHARD, IN_COLS // N_SHARD), (1, 0, 2))
    if name == 'w_ple_proj':
        return jnp.transpose(g.reshape(g.shape[0], N_SHARD, g.shape[1] // N_SHARD), (1, 0, 2))
    if name == 'w_ff1':
        return g
    return g.reshape(N_SHARD, -1, g.shape[-1])


def _weight_grads(g):
    d = g['norm_mix'].shape[-1]
    out = {}
    for n in ('norm_mix', 'norm_ffn', 'norm_ple'):
        out[n] = g[n].reshape(d)
    out['sg_ln_g'] = g['sg_ln_g'].reshape(SG_W)
    out['sg_ln_b'] = g['sg_ln_b'].reshape(SG_W)
    out['sg_w'] = g['sg_w']
    out['sg_b'] = jnp.sum(g['sg_bb'], axis=2)
    out['out_norm_a'] = g['out_norm_a'].reshape(SG_W)
    out['out_norm_b'] = g['out_norm_b'].reshape(SC_W)
    out['gdn_norm'] = jnp.sum(g['gdn_norm'], axis=(0, 1))
    out['gdn_a_log'] = g['a_log_row'][0, :GDN_HEADS]
    out['gdn_dt_bias'] = g['dt_bias_row'][0, :GDN_HEADS]
    out['sc_conv'] = jnp.concatenate([w.reshape(1, SC_W) for w in g['sc_taps']], axis=0)
    taps = [jnp.concatenate([g[nm][j].reshape(1, GDN_W) for nm in ('q_taps', 'k_taps', 'v_taps')], axis=1)
            for j in range(GDN_CONV)]
    out['gdn_conv'] = jnp.concatenate(taps, axis=0)
    return out


ANY = pl.BlockSpec(memory_space=pl.ANY)


def _place():
    x, y, c = lax.axis_index("x"), lax.axis_index("y"), lax.axis_index("c")
    chips = [(1 - x, y), (x, 1 - y), (1 - x, 1 - y)]
    return x, y, c, chips


def _place_shard(wsh, shard, dtype, layer=None, after=None):
    dp, r, cc = wsh.shape
    tr = min(TR, r)
    nb = r // tr

    def body(idx_ref, w_ref, *rest):
        rest[-1][...] = w_ref[...].astype(rest[-1].dtype)

    if layer is None:
        grid, shape = (dp * nb,), (dp, N_SHARD, r, cc)
        in_spec = pl.BlockSpec((None, tr, cc), lambda i, ix: (i // nb, i % nb, 0))
        out_spec = pl.BlockSpec((None, None, tr, cc), lambda i, ix: (i // nb, ix[0], i % nb, 0))
    else:
        grid, shape = (nb,), (N_SHARD, r, cc)
        in_spec = pl.BlockSpec((None, tr, cc), lambda i, ix: (layer, i, 0))
        out_spec = pl.BlockSpec((None, tr, cc), lambda i, ix: (ix[0], i, 0))
    return pl.pallas_call(
        body, name="ag_place_shard",
        grid_spec=pltpu.PrefetchScalarGridSpec(num_scalar_prefetch=1, grid=grid,
                                               in_specs=[in_spec] + ([] if after is None else [ANY]),
                                               out_specs=out_spec),
        out_shape=jax.ShapeDtypeStruct(shape, dtype),
        compiler_params=_cparams(("arbitrary",)),
    )(shard, wsh, *([] if after is None else [after]))


def _rows(ref, slab, half):
    rh = ref.shape[1] // 2
    return ref.at[slab, pl.ds(pl.multiple_of(half * rh, rh), rh), :]


def _gather_direct(bufs):
    n = len(bufs)

    def body(*refs):
        outs = refs[n:2 * n]
        send_sems, recv_sems = refs[2 * n:]
        x, y, c, chips = _place()
        me = 2 * x + y

        def cp(w, l, j, shard, to):
            blk = outs[w].at[l, shard]
            return pltpu.make_async_remote_copy(src_ref=blk, dst_ref=blk, send_sem=send_sems.at[w, l, j],
                                                recv_sem=recv_sems.at[w, l, j], device_id=to, device_id_type=MESH)

        sends = [cp(w, l, j, me, (*chip, c)) for w in range(n) for l in range(DEPTH) for j, chip in enumerate(chips)]
        for s in sends:
            s.start()
        for w in range(n):
            for l in range(DEPTH):
                for j, (px, py) in enumerate(chips):
                    cp(w, l, j, 2 * px + py, (px, py, c)).wait_recv()
        for s in sends:
            s.wait_send()

    return pl.pallas_call(
        body, name="ag_small", in_specs=[ANY] * n, out_specs=[ANY] * n,
        out_shape=[jax.ShapeDtypeStruct(b.shape, b.dtype) for b in bufs],
        input_output_aliases={w: w for w in range(n)},
        scratch_shapes=[pltpu.SemaphoreType.DMA((n, DEPTH, 3)), pltpu.SemaphoreType.DMA((n, DEPTH, 3))],
    )(*bufs)


def _gather_halves(bufs):
    n = len(bufs)

    def body(*refs):
        outs = refs[n:2 * n]
        send_sems, recv_sems = refs[2 * n:]
        x, y, c, chips = _place()
        me = 2 * x + y
        sibling = (x, y, 1 - c)

        def cp(w, k, shard, half, to):
            blk = _rows(outs[w], shard, half)
            return pltpu.make_async_remote_copy(src_ref=blk, dst_ref=blk, send_sem=send_sems.at[w, k],
                                                recv_sem=recv_sems.at[w, k], device_id=to, device_id_type=MESH)

        sends = []
        for w in range(n):
            for j, chip in enumerate(chips):
                s = cp(w, j, me, c, (*chip, c))
                s.start()
                sends.append(s)
        for w in range(n):
            for j, (px, py) in enumerate(chips):
                cp(w, j, 2 * px + py, c, sibling).wait_recv()
                s = cp(w, 3 + j, 2 * px + py, c, sibling)
                s.start()
                sends.append(s)
        for w in range(n):
            for j, (px, py) in enumerate(chips):
                cp(w, 3 + j, 2 * px + py, 1 - c, sibling).wait_recv()
        for s in sends:
            s.wait_send()

    return pl.pallas_call(
        body, name="ag_gather_halves",
        in_specs=[ANY] * n, out_specs=[ANY] * n,
        out_shape=[jax.ShapeDtypeStruct(b.shape, b.dtype) for b in bufs],
        input_output_aliases={w: w for w in range(n)},
        scratch_shapes=[pltpu.SemaphoreType.DMA((n, 6)), pltpu.SemaphoreType.DMA((n, 6))],
    )(*bufs)


HBM = pl.BlockSpec(memory_space=pltpu.HBM)
SEM = pl.BlockSpec(memory_space=pltpu.SEMAPHORE)
_SPLIT = pltpu.CompilerParams(has_side_effects=pltpu.SideEffectType.DATAFLOW_SIDE_EFFECTING)


def _in_hbm(arrs):
    return [pltpu.with_memory_space_constraint(a, pltpu.HBM) for a in arrs]


def _gather_start(tag, bufs, after):
    n, na = len(bufs), len(after)
    k = 3 * n

    def body(*refs):
        b_refs, sems, token = refs[:n], refs[n + na:n + na + 2 * k], refs[-1]
        x, y, c, chips = _place()
        me = 2 * x + y
        for w in range(n):
            for j, chip in enumerate(chips):
                blk = _rows(b_refs[w], me, c)
                pltpu.make_async_remote_copy(src_ref=blk, dst_ref=blk, send_sem=sems[3 * w + j],
                                             recv_sem=sems[k + 3 * w + j], device_id=(*chip, c),
                                             device_id_type=MESH).start()
        token[...] = jnp.zeros_like(token)

    res = pl.pallas_call(
        body, name="ag_start_" + tag,
        out_shape=(*[pltpu.SemaphoreType.DMA(())] * (2 * k), *[pltpu.HBM(b.shape, b.dtype) for b in bufs],
                   jax.ShapeDtypeStruct((8, LANES), F32)),
        in_specs=[HBM] * n + [ANY] * na,
        out_specs=(*[SEM] * (2 * k), *[HBM] * n, pl.BlockSpec(memory_space=pltpu.VMEM)),
        input_output_aliases={w: 2 * k + w for w in range(n)}, compiler_params=_SPLIT,
    )(*_in_hbm(bufs), *after)
    return list(res[:2 * k]), list(res[2 * k:2 * k + n]), res[2 * k + n]


def _gather_wait(tag, sems, bufs, after):
    n = len(bufs)
    k = 3 * n

    def body(*refs):
        b_refs, sems = refs[:n], refs[n:n + 2 * k]
        x, y, c, chips = _place()
        me = 2 * x + y
        for w in range(n):
            for j, (px, py) in enumerate(chips):
                mine, theirs = _rows(b_refs[w], me, c), _rows(b_refs[w], 2 * px + py, c)
                cp = pltpu.make_async_remote_copy(src_ref=mine, dst_ref=theirs, send_sem=sems[3 * w + j],
                                                  recv_sem=sems[k + 3 * w + j], device_id=(px, py, c),
                                                  device_id_type=MESH)
                cp.wait_send()
                cp.wait_recv()

    res = pl.pallas_call(
        body, name="ag_wait_" + tag, out_shape=tuple(pltpu.HBM(b.shape, b.dtype) for b in bufs),
        in_specs=[HBM] * n + [SEM] * (2 * k) + [ANY], out_specs=(HBM,) * n,
        input_output_aliases={w: w for w in range(n)}, compiler_params=_SPLIT,
    )(*bufs, *sems, after)
    return list(res)


def _gather_forward(bufs):
    n = len(bufs)

    def body(*refs):
        outs = refs[n:2 * n]
        send_sems, recv_sems = refs[2 * n:]
        x, y, c, chips = _place()

        def cp(w, j, shard, half):
            blk = _rows(outs[w], shard, half)
            return pltpu.make_async_remote_copy(src_ref=blk, dst_ref=blk, send_sem=send_sems.at[w, j],
                                                recv_sem=recv_sems.at[w, j], device_id=(x, y, 1 - c),
                                                device_id_type=MESH)

        sends = [cp(w, j, 2 * px + py, c) for w in range(n) for j, (px, py) in enumerate(chips)]
        for s in sends:
            s.start()
        for w in range(n):
            for j, (px, py) in enumerate(chips):
                cp(w, j, 2 * px + py, 1 - c).wait_recv()
        for s in sends:
            s.wait_send()

    return pl.pallas_call(
        body, name="ag_forward", in_specs=[ANY] * n, out_specs=[ANY] * n,
        out_shape=[jax.ShapeDtypeStruct(b.shape, b.dtype) for b in bufs],
        input_output_aliases={w: w for w in range(n)},
        scratch_shapes=[pltpu.SemaphoreType.DMA((n, 3)), pltpu.SemaphoreType.DMA((n, 3))],
    )(*bufs)


def _scatter_start(tag, parts, after):
    n, na = len(parts), len(after)
    k = 3 * n
    lands = [lax.empty((3,) + g.shape[1:], g.dtype) for g in parts]

    def body(*refs):
        srcs, dsts, sems, token = refs[:n], refs[n:2 * n], refs[2 * n + na:2 * n + na + 2 * k], refs[-1]
        x, y, c, chips = _place()
        for w in range(n):
            for j, (px, py) in enumerate(chips):
                pltpu.make_async_remote_copy(src_ref=srcs[w].at[2 * px + py], dst_ref=dsts[w].at[j],
                                             send_sem=sems[3 * w + j], recv_sem=sems[k + 3 * w + j],
                                             device_id=(px, py, c), device_id_type=MESH).start()
        token[...] = jnp.zeros_like(token)

    res = pl.pallas_call(
        body, name="rs_scatter_start_" + tag,
        out_shape=(*[pltpu.SemaphoreType.DMA(())] * (2 * k), *[pltpu.HBM(a.shape, a.dtype) for a in parts + lands],
                   jax.ShapeDtypeStruct((8, LANES), F32)),
        in_specs=[HBM] * (2 * n) + [ANY] * na,
        out_specs=(*[SEM] * (2 * k), *[HBM] * (2 * n), pl.BlockSpec(memory_space=pltpu.VMEM)),
        input_output_aliases={w: 2 * k + w for w in range(2 * n)}, compiler_params=_SPLIT,
    )(*_in_hbm(parts + lands), *after)
    return list(res[:2 * k]), list(res[2 * k:2 * k + n]), list(res[2 * k + n:2 * k + 2 * n]), res[2 * k + 2 * n]


def _scatter_wait(tag, sems, parts, lands, after):
    n = len(parts)
    k = 3 * n

    def body(*refs):
        srcs, dsts, sems = refs[:n], refs[n:2 * n], refs[2 * n:2 * n + 2 * k]
        x, y, c, chips = _place()
        for w in range(n):
            for j, (px, py) in enumerate(chips):
                cp = pltpu.make_async_remote_copy(src_ref=srcs[w].at[2 * px + py], dst_ref=dsts[w].at[j],
                                                  send_sem=sems[3 * w + j], recv_sem=sems[k + 3 * w + j],
                                                  device_id=(px, py, c), device_id_type=MESH)
                cp.wait_send()
                cp.wait_recv()

    res = pl.pallas_call(
        body, name="rs_scatter_wait_" + tag, out_shape=tuple(pltpu.HBM(a.shape, a.dtype) for a in parts + lands),
        in_specs=[HBM] * (2 * n) + [SEM] * (2 * k) + [ANY], out_specs=(HBM,) * (2 * n),
        input_output_aliases={w: w for w in range(2 * n)}, compiler_params=_SPLIT,
    )(*parts, *lands, *sems, after)
    return list(res[:n]), list(res[n:])


def _swap_halves(grads):
    n = len(grads)

    def body(*refs):
        srcs, outs = refs[:n], refs[n:2 * n]
        send_sems, recv_sems = refs[2 * n:]
        x, y, c, _ = _place()
        cps = []
        for w in range(n):
            rh = srcs[w].shape[1] // 2
            theirs = srcs[w].at[:, pl.ds(pl.multiple_of((1 - c) * rh, rh), rh), :]
            cps.append(pltpu.make_async_remote_copy(src_ref=theirs, dst_ref=outs[w], send_sem=send_sems.at[w],
                                                    recv_sem=recv_sems.at[w], device_id=(x, y, 1 - c),
                                                    device_id_type=MESH))
        for cpy in cps:
            cpy.start()
        for cpy in cps:
            cpy.wait()

    return pl.pallas_call(
        body, name="rs_swap_halves", in_specs=[ANY] * n, out_specs=[ANY] * n,
        out_shape=[jax.ShapeDtypeStruct((g.shape[0], g.shape[1] // 2, g.shape[2]), g.dtype) for g in grads],
        scratch_shapes=[pltpu.SemaphoreType.DMA((n,)), pltpu.SemaphoreType.DMA((n,))],
    )(*grads)


def _scatter_shards(parts):
    n = len(parts)

    def body(*refs):
        srcs, outs = refs[:n], refs[n:2 * n]
        send_sems, recv_sems = refs[2 * n:]
        x, y, c, chips = _place()

        def cp(w, j, src_shard, to):
            return pltpu.make_async_remote_copy(
                src_ref=srcs[w].at[src_shard], dst_ref=outs[w].at[j], send_sem=send_sems.at[w, j],
                recv_sem=recv_sems.at[w, j], device_id=to, device_id_type=MESH)

        sends = [cp(w, j, 2 * px + py, (px, py, c)) for w in range(n) for j, (px, py) in enumerate(chips)]
        for s in sends:
            s.start()
        for s in sends:
            s.wait()

    return pl.pallas_call(
        body, name="rs_scatter", in_specs=[ANY] * n, out_specs=[ANY] * n,
        out_shape=[jax.ShapeDtypeStruct((3,) + g.shape[1:], g.dtype) for g in parts],
        scratch_shapes=[pltpu.SemaphoreType.DMA((n, 3)), pltpu.SemaphoreType.DMA((n, 3))],
    )(*parts)


def _sum_reduced(part, slots, shard, core, layer, prev):
    _, rh, cc = part.shape
    tr = min(TR, rh)
    nbh = rh // tr

    def body(shard_ref, core_ref, p_ref, s0_ref, s1_ref, s2_ref, *rest):
        acc = p_ref[...].astype(F32)
        for s_ref in (s0_ref, s1_ref, s2_ref):
            acc = acc + s_ref[...].astype(F32)
        rest[-1][...] = acc

    slot = lambda j: pl.BlockSpec((None, tr, cc), lambda i, sh, co: (j, i, 0))
    ins = [shard, core, part, slots, slots, slots] + ([] if prev is None else [prev])
    return pl.pallas_call(
        body, name="rs_sum_reduced",
        grid_spec=pltpu.PrefetchScalarGridSpec(
            num_scalar_prefetch=2, grid=(nbh,),
            in_specs=[pl.BlockSpec((None, tr, cc), lambda i, sh, co: (sh[0], i, 0)), slot(0), slot(1), slot(2)]
            + ([] if prev is None else [ANY]),
            out_specs=pl.BlockSpec((None, tr, cc), lambda i, sh, co: (layer, co[0] * nbh + i, 0))),
        out_shape=jax.ShapeDtypeStruct((DEPTH, 2 * rh, cc), F32),
        input_output_aliases={} if prev is None else {6: 0},
        compiler_params=_cparams(("arbitrary",)),
    )(*ins)


def _join_halves(bufs, layer):
    n = len(bufs)

    def body(*refs):
        outs = refs[n:2 * n]
        send_sems, recv_sems = refs[2 * n:]
        x, y, c, _ = _place()

        def cp(w, half):
            blk = _rows(outs[w], layer, half)
            return pltpu.make_async_remote_copy(src_ref=blk, dst_ref=blk, send_sem=send_sems.at[w],
                                                recv_sem=recv_sems.at[w], device_id=(x, y, 1 - c), device_id_type=MESH)

        sends = [cp(w, c) for w in range(n)]
        for s in sends:
            s.start()
        for w in range(n):
            cp(w, 1 - c).wait_recv()
        for s in sends:
            s.wait_send()

    return pl.pallas_call(
        body, name="rs_join_halves", in_specs=[ANY] * n, out_specs=[ANY] * n,
        out_shape=[jax.ShapeDtypeStruct(b.shape, b.dtype) for b in bufs],
        input_output_aliases={w: w for w in range(n)},
        scratch_shapes=[pltpu.SemaphoreType.DMA((n,)), pltpu.SemaphoreType.DMA((n,))],
    )(*bufs)


def _exchange_small(vec):
    def body(src, out, send_sems, recv_sems, local_sem):
        x, y, c, _ = _place()
        me = 4 * x + 2 * y + c
        lc = pltpu.make_async_copy(src, out.at[me], local_sem)
        lc.start()
        sends = []
        for k in range(1, N_DEV):
            fx, fy, fc = (k >> 2) & 1, (k >> 1) & 1, k & 1
            to = (x ^ fx, y ^ fy, c ^ fc)
            s = pltpu.make_async_remote_copy(src_ref=src, dst_ref=out.at[me], send_sem=send_sems.at[k - 1],
                                             recv_sem=recv_sems.at[k - 1], device_id=to, device_id_type=MESH)
            s.start()
            sends.append(s)
        for k in range(1, N_DEV):
            fx, fy, fc = (k >> 2) & 1, (k >> 1) & 1, k & 1
            frm = 4 * (x ^ fx) + 2 * (y ^ fy) + (c ^ fc)
            pltpu.make_async_remote_copy(src_ref=src, dst_ref=out.at[frm], send_sem=send_sems.at[k - 1],
                                         recv_sem=recv_sems.at[k - 1], device_id=(x ^ fx, y ^ fy, c ^ fc),
                                         device_id_type=MESH).wait_recv()
        for s in sends:
            s.wait_send()
        lc.wait()

    return pl.pallas_call(
        body, name="small_exchange", in_specs=[ANY], out_specs=ANY,
        out_shape=jax.ShapeDtypeStruct((N_DEV,) + vec.shape, vec.dtype),
        scratch_shapes=[pltpu.SemaphoreType.DMA((N_DEV - 1,)), pltpu.SemaphoreType.DMA((N_DEV - 1,)),
                        pltpu.SemaphoreType.DMA],
    )(vec)


def _add_own_half(g, other, c_arr):
    ns, rh, cc = other.shape
    tr = min(TR, rh)
    nb = rh // tr

    def body(c_ref, g_ref, o_ref, out_ref):
        out_ref[...] = (g_ref[...].astype(F32) + o_ref[...].astype(F32)).astype(out_ref.dtype)

    slab = pl.BlockSpec((None, tr, cc), lambda i, cr: (i // nb, i % nb, 0))
    return pl.pallas_call(
        body, name="rs_add_own_half",
        grid_spec=pltpu.PrefetchScalarGridSpec(
            num_scalar_prefetch=1, grid=(ns * nb,),
            in_specs=[pl.BlockSpec((None, tr, cc), lambda i, cr: (i // nb, cr[0] * nb + i % nb, 0)), slab],
            out_specs=slab),
        out_shape=jax.ShapeDtypeStruct(other.shape, other.dtype),
        compiler_params=_cparams(("arbitrary",)),
    )(c_arr, g, other)


def _sum_slots(name, a):
    ns, r, cc = a.shape
    tr = min(TR, r)
    ins = [_In(a, (None, tr, cc), functools.partial(lambda s, i: (s, i, 0), s)) for s in range(ns)]

    def fn(*blocks):
        acc = blocks[0].astype(F32)
        for b in blocks[1:]:
            acc = acc + b.astype(F32)
        return (acc,)

    return _fused(name, fn, r // tr, ins, [((r, cc), F32, (tr, cc), lambda i: (i, 0))])[0]


def _adamw_fn(w, g, m, v):
    m = ADAM_B1 * m + (1.0 - ADAM_B1) * g
    v = ADAM_B2 * v + (1.0 - ADAM_B2) * jnp.square(g)
    m_hat = m / (1.0 - ADAM_B1 ** ADAM_STEP)
    v_hat = v / (1.0 - ADAM_B2 ** ADAM_STEP)
    delta = -ADAM_LR * (m_hat / (jnp.sqrt(v_hat) + ADAM_EPS) + ADAM_WD * w)
    return delta, m, v


def _adamw(w, g, m, v):
    r, cc = w.shape
    tr = min(TR, r)
    ins = [_In(a, (tr, cc), lambda i: (i, 0)) for a in (w, g, m, v)]
    return _fused("adamw", _adamw_fn, r // tr, ins, [((r, cc), F32, (tr, cc), lambda i: (i, 0))] * 3)


def _pack(arrs):
    flat = jnp.concatenate([a.reshape(-1) for a in arrs])
    tile = TR * LANES
    n = -(-flat.shape[0] // tile) * tile
    return jnp.pad(flat, (0, n - flat.shape[0])).reshape(-1, LANES)


def _unpack(vec, shapes):
    flat = vec.reshape(-1)
    out, o = [], 0
    for s in shapes:
        n = math.prod(s)
        out.append(flat[o:o + n].reshape(s))
        o += n
    return out


def kernel(x, p, norm_mix, w_in, sg_ln_g, sg_ln_b, sg_w, sg_b, sc_conv, gdn_conv, gdn_a_log, gdn_dt_bias, gdn_norm, out_norm_a, out_norm_b, w_o, norm_ffn, w_ff1, w_ff2, norm_ple, w_ple_gate, w_ple_proj, norm_final, loss_target, m_norm_mix, m_w_in, m_sg_ln_g, m_sg_ln_b, m_sg_w, m_sg_b, m_sc_conv, m_gdn_conv, m_gdn_a_log, m_gdn_dt_bias, m_gdn_norm, m_out_norm_a, m_out_norm_b, m_w_o, m_norm_ffn, m_w_ff1, m_w_ff2, m_norm_ple, m_w_ple_gate, m_w_ple_proj, m_norm_final, v_norm_mix, v_w_in, v_sg_ln_g, v_sg_ln_b, v_sg_w, v_sg_b, v_sc_conv, v_gdn_conv, v_gdn_a_log, v_gdn_dt_bias, v_gdn_norm, v_out_norm_a, v_out_norm_b, v_w_o, v_norm_ffn, v_w_ff1, v_w_ff2, v_norm_ple, v_w_ple_gate, v_w_ple_proj, v_norm_final):
    given = dict(locals())
    w = {n: given[n] for n in WEIGHTS}
    m = {n: given['m_' + n] for n in WEIGHTS}
    v = {n: given['v_' + n] for n in WEIGHTS}
    shard = 2 * lax.axis_index("x") + lax.axis_index("y")
    core = lax.axis_index("c")

    shard_arr = shard.reshape(1).astype(jnp.int32)
    c_arr = core.reshape(1).astype(jnp.int32)
    convs = dict(zip(CONVS, _gather_direct([_place_shard(w[n], shard_arr, F32) for n in CONVS])))
    small = {n: w[n] for n in SMALL if n not in CONVS}
    late_w = [n for n in BIG if n != 'w_in']
    flyw = _gather_start("w", [_place_shard(w['w_in'], shard_arr, BF16, 0)], list(convs.values()))
    placed0 = [_place_shard(w[n], shard_arr, BF16, 0, flyw[2]) for n in late_w]
    placed1 = [_place_shard(w[n], shard_arr, BF16, 1, flyw[2]) for n in BIG]
    fly0 = _gather_start("l0", placed0, [flyw[2]])
    fly1 = _gather_start("l1", placed1, [fly0[2]])
    first = _gather_forward(_gather_wait("w", flyw[0], flyw[1], fly1[2]))

    lws = [_layer_weights(0, {'w_in': first[0]}, convs, small), None]
    lws[0]['norm_mix'] = lws[0]['norm_mix'] + fly1[2][0, 0]

    def rest_of_layer0(behind):
        got = _gather_forward(_gather_wait("l0", fly0[0], fly0[1], behind))
        lws[0].update(_big_weights(dict(zip(late_w, got))))

    h, sv0 = _layer_fwd(x[0], p[0, 0], lws[0], rest_of_layer0)
    got = _gather_forward(_gather_wait("l1", fly1[0], fly1[1], h))
    lws[1] = _layer_weights(1, dict(zip(BIG, got)), convs, small)
    h, sv1 = _layer_fwd(h, p[1, 0], lws[1])
    saved = [sv0, sv1]
    loss, dh, dnf = _loss_grad(h, small['norm_final'].reshape(1, -1), loss_target[0])

    def swap_add(names, g):
        big = [_shard_major(n, g[n]) for n in names]
        return [_add_own_half(g_, o, c_arr) for g_, o in zip(big, _swap_halves(big))]

    early_g = ['w_ff1', 'w_ff2', 'w_ple_gate', 'w_ple_proj']
    late_g = [n for n in BIG if n not in early_g]
    per_layer = [None] * DEPTH
    dh, g = _layer_bwd(dh, saved[1], lws[1])
    per_layer[1] = _weight_grads(g)
    fly1 = _scatter_start("l1", swap_add(BIG, g), [dh])
    lws[0]['norm_ple'] = lws[0]['norm_ple'] + fly1[3][0, 0]
    fly0 = []

    def early_grads(g0):
        fly0.extend(_scatter_start("l0", swap_add(early_g, g0), []))
        return fly0[3]

    dh, g = _layer_bwd(dh, saved[0], lws[0], early_grads)
    per_layer[0] = _weight_grads(g)
    flyl = _scatter_start("l0b", swap_add(late_g, g), [])
    groups = [(1, BIG, *_scatter_wait("l1", fly1[0], fly1[1], fly1[2], flyl[3])),
              (0, early_g, *_scatter_wait("l0", fly0[0], fly0[1], fly0[2], flyl[3]))]
    g_big = {}

    def finish(groups):
        for l, names, parts, slots in groups:
            sums = [_sum_reduced(pt, sl, shard_arr, c_arr, l, g_big.get(n)) for n, pt, sl in zip(names, parts, slots)]
            g_big.update(zip(names, _join_halves(sums, l)))

    delta, new_m, new_v, grad_w = {}, {}, {}, {}

    def update(names):
        for n in names:
            shp = w[n].shape
            two_d = lambda a: a.reshape(-1, shp[-1])
            d_, m_, v_ = _adamw(two_d(w[n]), two_d(g_big[n]), two_d(m[n]), two_d(v[n]))
            delta[n], new_m[n], new_v[n], grad_w[n] = d_.reshape(shp), m_.reshape(shp), v_.reshape(shp), g_big[n]

    finish(groups)
    update(early_g)
    finish([(0, late_g, *_scatter_wait("l0b", flyl[0], flyl[1], flyl[2], delta[early_g[0]]))])
    update(late_g)
    grad_x = dh
    grads = {n: jnp.stack([per_layer[l][n] for l in range(DEPTH)], axis=0) for n in SMALL if n != 'norm_final'}
    grads['norm_final'] = dnf.reshape(-1)

    rep = [n for n in SMALL if n not in CONVS]
    names = rep + CONVS
    vec = _pack([grads[n] for n in names] + [loss[0, :1]])
    total = _sum_slots("small_sum", _exchange_small(vec))
    parts_small = _unpack(total, [grads[n].shape for n in names] + [(1,)])
    g_small = dict(zip(names, parts_small[:-1]))
    loss_out = parts_small[-1].reshape(())
    for n in CONVS:
        width = w[n].shape[-1]
        g_small[n] = lax.dynamic_slice_in_dim(g_small[n], shard * width, width, axis=2)

    shapes = [w[n].shape for n in SMALL]
    d_, m_, v_ = _adamw(_pack([w[n] for n in SMALL]), _pack([g_small[n] for n in SMALL]),
                        _pack([m[n] for n in SMALL]), _pack([v[n] for n in SMALL]))
    for n, dd, mm, vv in zip(SMALL, _unpack(d_, shapes), _unpack(m_, shapes), _unpack(v_, shapes)):
        delta[n], new_m[n], new_v[n], grad_w[n] = dd, mm, vv, g_small[n]

    return (loss_out, grad_x[None], *[grad_w[n] for n in WEIGHTS], *[delta[n] for n in WEIGHTS],
            *[new_m[n] for n in WEIGHTS], *[new_v[n] for n in WEIGHTS])
```

```python
import functools
import math

import jax
import jax.numpy as jnp
from jax import lax
from jax.experimental import pallas as pl
from jax.experimental.pallas import tpu as pltpu

F32 = jnp.float32
BF16 = jnp.bfloat16
HI = lax.Precision.HIGH
MESH = pl.DeviceIdType.MESH

LANES = 128
EPS = 1e-6
SG_HEADS, SG_CHUNK = 4, 128
SC_GROUPS, SC_KERNEL = 4, 3
GDN_HEADS, GDN_CONV = 8, 4
GDN_L = 128
GDN_CPS = 8
GDN_HPS = 2
SG_W = SG_HEADS * LANES
SC_W = SC_GROUPS * LANES
GDN_W = GDN_HEADS * LANES
IN_COLS = 2 * SG_W + 3 * SC_W + 4 * GDN_W + 2 * GDN_HEADS
PROJ_W = 7168
CB_U, CB_V = 0, 4
CB_GB, CB_GC, CB_X = 8, 12, 16
CB_Q, CB_K, CB_VV, CB_Z, CB_AB = 20, 28, 36, 44, 52
N_SHARD = 4
N_DEV = 8
DEPTH = 2

ADAM_LR, ADAM_B1, ADAM_B2, ADAM_EPS, ADAM_WD, ADAM_STEP = 0.001, 0.9, 0.999, 1e-08, 0.01, 10

VMEM_LIMIT = 56 << 20

WEIGHTS = ['norm_mix', 'w_in', 'sg_ln_g', 'sg_ln_b', 'sg_w', 'sg_b', 'sc_conv', 'gdn_conv', 'gdn_a_log',
           'gdn_dt_bias', 'gdn_norm', 'out_norm_a', 'out_norm_b', 'w_o', 'norm_ffn', 'w_ff1', 'w_ff2', 'norm_ple',
           'w_ple_gate', 'w_ple_proj', 'norm_final']
BIG = ['w_in', 'w_o', 'w_ff1', 'w_ff2', 'w_ple_gate', 'w_ple_proj']
CONVS = ['sc_conv', 'gdn_conv']
SMALL = [n for n in WEIGHTS if n not in BIG]


def _cparams(sem=None):
    return pltpu.CompilerParams(dimension_semantics=sem, vmem_limit_bytes=VMEM_LIMIT)


def _dot(a, b, dims, prec=None):
    return lax.dot_general(a, b, (dims, ((), ())), precision=prec, preferred_element_type=F32)


def _mm(a, b):
    return _dot(a, b, ((1,), (0,)), HI)


def _mm_nt(a, b):
    return _dot(a, b, ((1,), (1,)), HI)


def _mm_tn(a, b):
    return _dot(a, b, ((0,), (0,)), HI)


@jax.custom_vjp
def _bmm(a, b):
    return _dot(a.astype(BF16), b.astype(BF16), ((1,), (0,)))


def _bmm_fwd(a, b):
    return _bmm(a, b), (a, b)


def _bmm_bwd(res, g):
    a, b = res
    gb = g.astype(BF16)
    return _dot(gb, b.astype(BF16), ((1,), (1,))), _dot(a.astype(BF16), gb, ((0,), (0,)))


_bmm.defvjp(_bmm_fwd, _bmm_bwd)


@jax.custom_vjp
def _bmm_tn(a, b):
    return _dot(a.astype(BF16), b.astype(BF16), ((0,), (0,)))


def _bmm_tn_fwd(a, b):
    return _bmm_tn(a, b), (a, b)


def _bmm_tn_bwd(res, g):
    a, b = res
    gb = g.astype(BF16)
    return _dot(b.astype(BF16), gb, ((1,), (1,))), _dot(a.astype(BF16), gb, ((1,), (0,)))


_bmm_tn.defvjp(_bmm_tn_fwd, _bmm_tn_bwd)


def _sigmoid(x):
    return 1.0 / (1.0 + jnp.exp(-x))


def _silu(x):
    return x * _sigmoid(x)


def _gelu(x):
    c = math.sqrt(2.0 / math.pi)
    return 0.5 * x * (1.0 + jnp.tanh(c * (x + 0.044715 * (x * x * x))))


def _softplus(x):
    return jnp.maximum(x, 0.0) + jnp.log(1.0 + jnp.exp(-jnp.abs(x)))


def _rms(x, g):
    return x * lax.rsqrt(jnp.mean(x * x, axis=-1, keepdims=True) + EPS) * g


def _roll_rows(x, shift):
    return pltpu.roll(x, shift % x.shape[0], 0)


@functools.partial(jax.custom_vjp, nondiff_argnums=(1,))
def _shift_down(x, j):
    row = lax.broadcasted_iota(jnp.int32, x.shape, 0)
    return jnp.where(row >= j, _roll_rows(x, j), 0.0)


def _shift_down_fwd(x, j):
    return _shift_down(x, j), None


def _shift_down_bwd(j, _, dy):
    row = lax.broadcasted_iota(jnp.int32, dy.shape, 0)
    return (jnp.where(row < dy.shape[0] - j, _roll_rows(dy, -j), 0.0),)


_shift_down.defvjp(_shift_down_fwd, _shift_down_bwd)


def _causal_conv(x, taps):
    k = len(taps)
    y = taps[k - 1] * x
    for j in range(k - 1):
        y = y + taps[j] * _shift_down(x, k - 1 - j)
    return y


class _In:
    def __init__(self, arr, block, imap, shared=False, gshape=None, gmap=None):
        self.arr, self.block, self.imap, self.shared = arr, block, imap, shared
        self.gshape = arr.shape if gshape is None else gshape
        self.gmap = imap if gmap is None else gmap


def _fused(name, fn, n, ins, outs, douts=None, need=None, gdt=None, add=None):
    n_in, n_out = len(ins), len(outs)
    in_specs = [pl.BlockSpec(s.block, s.imap) for s in ins]
    out_specs = [pl.BlockSpec(bs, im) for _, _, bs, im in outs]
    if douts is None:
        def body(*refs):
            res = fn(*[r[...] for r in refs[:n_in]])
            for r, v in zip(refs[n_in:], res):
                r[...] = v.astype(r.dtype)

        return pl.pallas_call(
            body, name=name, grid=(n,), in_specs=in_specs, out_specs=out_specs,
            out_shape=[jax.ShapeDtypeStruct(s, d) for s, d, _, _ in outs],
            compiler_params=_cparams(("arbitrary",)),
        )(*[s.arr for s in ins])

    gdt = list(gdt) if gdt is not None else [F32] * n_in
    add = dict(add or {})
    gidx = [i for i in range(n_in) if need[i]]
    aidx = [i for i in gidx if i in add]

    def body(*refs):
        in_refs, d_refs = refs[:n_in], refs[n_in:n_in + n_out]
        a_refs = dict(zip(aidx, refs[n_in + n_out:n_in + n_out + len(aidx)]))
        g_refs = refs[n_in + n_out + len(aidx):]
        vals = [r[...] for r in in_refs]

        def f(*dv):
            full = list(vals)
            for i, v in zip(gidx, dv):
                full[i] = v
            return tuple(o.astype(F32) for o in fn(*full))

        _, vjp = jax.vjp(f, *[vals[i].astype(F32) for i in gidx])
        grads = vjp(tuple(r[...].astype(F32) for r in d_refs))
        for i, g_ref, g in zip(gidx, g_refs, grads):
            if i in a_refs:
                g = g + a_refs[i][...].astype(F32)
            if ins[i].shared:
                period = n if ins[i].shared is True else ins[i].shared

                @pl.when(pl.program_id(0) % period == 0)
                def _():
                    g_ref[...] = jnp.zeros_like(g_ref)
                g_ref[...] += g.astype(g_ref.dtype)
            else:
                g_ref[...] = g.astype(g_ref.dtype)

    g_specs = [pl.BlockSpec(ins[i].block, ins[i].gmap) for i in gidx]
    g_shape = [jax.ShapeDtypeStruct(ins[i].gshape, gdt[i]) for i in gidx]
    res = pl.pallas_call(
        body, name=name, grid=(n,), in_specs=in_specs + out_specs + [g_specs[gidx.index(i)] for i in aidx],
        out_specs=g_specs, out_shape=g_shape,
        compiler_params=_cparams(("arbitrary",)),
    )(*[s.arr for s in ins], *douts, *[add[i] for i in aidx])
    full = [None] * n_in
    for i, g in zip(gidx, res):
        full[i] = g
    return full


def _row_in(a, tm, cb=None):
    if cb is None:
        return _In(a, (tm, a.shape[1]), lambda i: (i, 0))
    return _In(a, (tm, LANES), lambda i: (i, cb), gshape=(a.shape[0], LANES), gmap=lambda i: (i, 0))


def _row_shared(a):
    return _In(a, a.shape, lambda i: (0, 0), shared=True)


def _row_out(t, w, dt, tm):
    return ((t, w), dt, (tm, w), lambda i: (i, 0))


def _col_in(a, base, nblk):
    t = a.shape[0]
    return _In(a, (t, LANES), lambda i: (0, base + i), gshape=(t, nblk * LANES), gmap=lambda i: (0, i))


def _col_par(a):
    return _In(a, (None,) + a.shape[1:], lambda i: (i, 0, 0))


def _col_out(t, w, dt, base=0):
    return ((t, w), dt, (t, LANES), lambda i: (0, base + i))


def _matmul(name, a, b, mode, m, n, k, tm, tn, tk, b_spec=None, epilogue=None, extras=(), outs=None, out_dtype=F32,
            n_outer=False):
    tm, tn, tk = min(tm, m), min(tn, n), min(tk, k)
    assert m % tm == 0 and n % tn == 0 and k % tk == 0, (name, m, n, k, tm, tn, tk)
    nk = k // tk
    a_spec = (pl.BlockSpec((tk, tm), lambda i, j, kk: (kk, i)) if mode == "tn"
              else pl.BlockSpec((tm, tk), lambda i, j, kk: (i, kk)))
    if b_spec is None:
        b_spec = (pl.BlockSpec((tn, tk), lambda i, j, kk: (j, kk)) if mode == "nt"
                  else pl.BlockSpec((tk, tn), lambda i, j, kk: (kk, j)))
    dims = {"nn": ((1,), (0,)), "nt": ((1,), (1,)), "tn": ((0,), (0,))}[mode]
    if outs is None:
        outs = [((m, n), out_dtype, (tm, tn), lambda i, j, kk: (i, j))]
    if epilogue is None:
        epilogue = lambda acc: (acc,)
    n_ex = len(extras)

    def body(*refs):
        a_ref, b_ref = refs[0], refs[1]
        ex_refs = refs[2:2 + n_ex]
        o_refs = refs[2 + n_ex:2 + n_ex + len(outs)]
        part = _dot(a_ref[...].astype(BF16), b_ref[...].astype(BF16), dims)

        def finish(acc):
            for r, v in zip(o_refs, epilogue(acc, *[e[...] for e in ex_refs])):
                r[...] = v.astype(r.dtype)

        if nk == 1:
            finish(part)
        else:
            acc_ref = refs[-1]
            kk = pl.program_id(2)

            @pl.when(kk == 0)
            def _():
                acc_ref[...] = part

            @pl.when(kk > 0)
            def _():
                acc_ref[...] += part

            @pl.when(kk == nk - 1)
            def _():
                finish(acc_ref[...])

    ex_specs = [pl.BlockSpec((tm, tn), lambda i, j, kk: (i, j)) for _ in extras]
    out_specs = [pl.BlockSpec(bs, im) for _, _, bs, im in outs]
    in_specs = [a_spec, b_spec] + ex_specs
    grid = (m // tm, n // tn, nk)
    if n_outer:
        swap = lambda sp: pl.BlockSpec(sp.block_shape, functools.partial(lambda f, j, i, kk: f(i, j, kk), sp.index_map))
        in_specs, out_specs, grid = [swap(sp) for sp in in_specs], [swap(sp) for sp in out_specs], (n // tn, m // tm, nk)
    res = pl.pallas_call(
        body, name=name, grid=grid,
        in_specs=in_specs,
        out_specs=out_specs,
        out_shape=[jax.ShapeDtypeStruct(s, d) for s, d, _, _ in outs],
        scratch_shapes=[pltpu.VMEM((tm, tn), F32)] if nk > 1 else [],
        compiler_params=_cparams(("parallel", "parallel", "arbitrary")),
    )(a, b, *extras)
    return res if len(res) > 1 else res[0]


def _fn_sgu(u_pre, v_pre, ln_g, ln_b, w, bb, na):
    t = u_pre.shape[0]
    u = _gelu(u_pre)
    v = _gelu(v_pre)
    mu = jnp.mean(v, axis=-1, keepdims=True)
    vc = v - mu
    vh = vc * lax.rsqrt(jnp.mean(vc * vc, axis=-1, keepdims=True) + EPS) * ln_g + ln_b
    ri = lax.broadcasted_iota(jnp.int32, w.shape, 0)
    ci = lax.broadcasted_iota(jnp.int32, w.shape, 1)
    wc = jnp.where(ri >= ci, w, 0.0)
    f = jnp.concatenate([_bmm(wc, vh[c * SG_CHUNK:(c + 1) * SG_CHUNK]) + bb for c in range(t // SG_CHUNK)], axis=0)
    return (_rms(u * f, na),)


def _fn_sconv(gb, gc, xin, w0, w1, w2, nb):
    return (_rms(gb * _causal_conv(gc * xin, (w0, w1, w2)), nb),)


def _fn_gdn_qk(pre, w0, w1, w2, w3):
    a = _silu(_causal_conv(pre, (w0, w1, w2, w3)))
    return (a * lax.rsqrt(jnp.sum(a * a, axis=-1, keepdims=True) + EPS),)


def _fn_gdn_v(pre, w0, w1, w2, w3):
    return (_silu(_causal_conv(pre, (w0, w1, w2, w3))),)


def _fn_gdn_gates(ab, a_log, dt_bias):
    lane = lax.broadcasted_iota(jnp.int32, ab.shape, 1)
    g = -jnp.exp(a_log) * _softplus(ab + dt_bias)
    return (jnp.where(lane < GDN_HEADS, g, jnp.where(lane < 2 * GDN_HEADS, _sigmoid(ab), 0.0)),)


def _solve_unit_lower_impl(mats):
    n = mats[0].shape[0]
    ri = lax.broadcasted_iota(jnp.int32, (n, n), 0)
    ci = lax.broadcasted_iota(jnp.int32, (n, n), 1)
    ts = [(ri == ci).astype(F32)] * len(mats)
    sh = 0
    while (1 << sh) < n:
        rb = jnp.right_shift(ri, sh)
        cb = jnp.right_shift(ci, sh)
        off = ((rb & 1) == 1) & (cb == rb - 1)
        us = [_mm(t, jnp.where(off, a, 0.0)) for t, a in zip(ts, mats)]
        ts = [t - _mm(u, t) for t, u in zip(ts, us)]
        sh += 1
    return tuple(ts)


def _solve_cotangents(ts, dts):
    us = [_mm_nt(dt, t) for t, dt in zip(ts, dts)]
    return tuple(-_mm_tn(t, u) for t, u in zip(ts, us))


@jax.custom_vjp
def _solve_unit_lower(mats):
    return _solve_unit_lower_impl(mats)


def _solve_unit_lower_fwd(mats):
    ts = _solve_unit_lower_impl(mats)
    return ts, ts


def _solve_unit_lower_bwd(ts, dts):
    return (_solve_cotangents(ts, dts),)


_solve_unit_lower.defvjp(_solve_unit_lower_fwd, _solve_unit_lower_bwd)


@jax.custom_vjp
def _solved_unit_lower(mats, ts):
    return ts


def _solved_unit_lower_fwd(mats, ts):
    return ts, ts


def _solved_unit_lower_bwd(ts, dts):
    return _solve_cotangents(ts, dts), tuple(jnp.zeros_like(t) for t in ts)


_solved_unit_lower.defvjp(_solved_unit_lower_fwd, _solved_unit_lower_bwd)


def _fn_gdn_wy(q, k, v, gates, pick_g, pick_b, t_saved=None):
    n, dk = GDN_L, q.shape[1]
    rows = [slice(c * n, (c + 1) * n) for c in range(q.shape[0] // n)]
    ri = lax.broadcasted_iota(jnp.int32, (n, n), 0)
    ci = lax.broadcasted_iota(jnp.int32, (n, n), 1)
    incl = ri >= ci
    eye = (ri == ci).astype(F32)
    last = lax.broadcasted_iota(jnp.int32, (n, 1), 0) == (n - 1)
    qs = [q[r] * (dk ** -0.5) for r in rows]
    ks = [k[r] for r in rows]
    gcs, betas, d_incl = [], [], []
    for r in rows:
        g = jnp.sum(gates[r] * pick_g, axis=1, keepdims=True)
        betas.append(jnp.sum(gates[r] * pick_b, axis=1, keepdims=True))
        g_row = jnp.sum(eye * g, axis=0, keepdims=True)
        gc = jnp.sum(jnp.where(incl, g_row, 0.0), axis=1, keepdims=True)
        gc_row = jnp.sum(eye * gc, axis=0, keepdims=True)
        gcs.append(gc)
        d_incl.append(jnp.where(incl, jnp.exp(jnp.where(incl, gc - gc_row, 0.0)), 0.0))
    kbs = [kk * b for kk, b in zip(ks, betas)]
    mats = tuple(_mm_nt(kb, kk) * jnp.where(ri > ci, d, 0.0) for kb, kk, d in zip(kbs, ks, d_incl))
    ts = _solve_unit_lower(mats) if t_saved is None else _solved_unit_lower(mats, tuple(t_saved[r] for r in rows))
    egs = [jnp.exp(gc) for gc in gcs]
    values = [_mm(t, v[r] * b) for t, r, b in zip(ts, rows, betas)]
    kcds = [_mm(t, kb * eg) for t, kb, eg in zip(ts, kbs, egs)]
    intras = [_mm_nt(qq, kk) * d for qq, kk, d in zip(qs, ks, d_incl)]
    g_lasts = [jnp.sum(jnp.where(last, gc, 0.0), axis=0, keepdims=True) for gc in gcs]
    qes = [qq * eg for qq, eg in zip(qs, egs)]
    kts = [kk * jnp.exp(gl - gc) for kk, gl, gc in zip(ks, g_lasts, gcs)]
    carries = [jnp.broadcast_to(jnp.exp(gl), (8, LANES)) for gl in g_lasts]
    cat = lambda parts: jnp.concatenate(parts, axis=0)
    res = (cat(values), cat(kcds), cat(qes), cat(kts), cat(intras), cat(carries))
    return res + (cat(ts),) if t_saved is None else res


def _gdn_steps(states, operands):
    v_new = [value - _bmm(kcd, s) for s, (value, kcd, _, _, _, _) in zip(states, operands)]
    outs = [_bmm(qe, s) + _bmm(intra, vn) for s, vn, (_, _, qe, _, intra, _) in zip(states, v_new, operands)]
    new = [s * carry + _bmm_tn(kt, vn) for s, vn, (_, _, _, kt, _, carry) in zip(states, v_new, operands)]
    return tuple(new), tuple(outs)


def _gdn_post(o, z, nrm):
    return _rms(o, nrm) * _silu(z)


def _gdn_wy_ins(q, k, v, gates, t_saved=None):
    t = q.shape[0]
    rb = min(GDN_CPS * GDN_L, t)
    hd = GDN_HEADS
    lane = jnp.arange(LANES)[None, None, :]
    pick_g = (lane == jnp.arange(hd)[:, None, None]).astype(F32)
    pick_b = (lane == jnp.arange(hd)[:, None, None] + hd).astype(F32)
    blk = lambda a: _In(a, (rb, LANES), lambda i: (i // hd, i % hd))
    par = lambda a: _In(a, (None, 1, LANES), lambda i: (i % hd, 0, 0))
    ins = [blk(q), blk(k), blk(v), _In(gates, (rb, LANES), lambda i: (i // hd, 0), shared=hd), par(pick_g),
           par(pick_b)]
    wide = lambda dt: ((t, GDN_W), dt, (rb, LANES), lambda i: (i // hd, i % hd))
    carry = ((hd, 8 * (t // GDN_L), LANES), F32, (None, 8 * (rb // GDN_L), LANES), lambda i: (i % hd, i // hd, 0))
    outs = [wide(F32), wide(BF16), wide(BF16), wide(BF16), wide(BF16), carry]
    if t_saved is None:
        outs.append(wide(F32))
    else:
        ins.append(blk(t_saved))
    return (t // rb) * hd, ins, outs


def _gdn_scan_specs(t):
    wide = GDN_HPS * LANES
    once = pl.Buffered(1)
    col = lambda base: pl.BlockSpec((t, wide), lambda h: (0, base // GDN_HPS + h), pipeline_mode=once)
    carry = pl.BlockSpec((GDN_HPS, 8 * (t // GDN_L), LANES), lambda h: (h, 0, 0))
    par = pl.BlockSpec((GDN_HPS, 1, LANES), lambda h: (h, 0, 0))
    return col, carry, par


def _head_operands(refs, cy_ref, c, i):
    rows = pl.ds(pl.multiple_of(c * GDN_L, GDN_L), GDN_L)
    lanes = pl.ds(i * LANES, LANES)
    return tuple(r[rows, lanes] for r in refs) + (cy_ref[i, pl.ds(pl.multiple_of(c * 8, 8), 1), :],)


def _gdn_scan_fwd(wy, proj, nrm):
    t = wy[0].shape[0]
    nc = t // GDN_L
    heads = range(GDN_HPS)

    def body(val_ref, kcd_ref, qe_ref, kt_ref, in_ref, cy_ref, z_ref, n_ref, y_ref, o_scr):
        big = (val_ref, kcd_ref, qe_ref, kt_ref, in_ref)

        def step(c, states):
            rows = pl.ds(pl.multiple_of(c * GDN_L, GDN_L), GDN_L)
            states, outs = _gdn_steps(states, [_head_operands(big, cy_ref, c, i) for i in heads])
            for i in heads:
                o_scr[rows, pl.ds(i * LANES, LANES)] = outs[i]
            return states

        lax.fori_loop(0, nc, step, tuple(jnp.zeros((LANES, LANES), F32) for _ in heads))
        for i in heads:
            lanes = pl.ds(i * LANES, LANES)
            y_ref[:, lanes] = _gdn_post(o_scr[:, lanes], z_ref[:, lanes], n_ref[i]).astype(y_ref.dtype)

    col, carry, par = _gdn_scan_specs(t)
    return pl.pallas_call(
        body, name="gdn_scan_fwd", grid=(GDN_HEADS // GDN_HPS,),
        in_specs=[col(0)] * 5 + [carry, col(CB_Z), par],
        out_specs=col(0), out_shape=jax.ShapeDtypeStruct((t, GDN_W), BF16),
        scratch_shapes=[pltpu.VMEM((t, GDN_HPS * LANES), F32)],
        compiler_params=_cparams(("arbitrary",)),
    )(*wy, proj, nrm)


def _gdn_scan_bwd(wy, proj, nrm, dy, dy_base):
    t = wy[0].shape[0]
    nc = t // GDN_L
    heads = range(GDN_HPS)

    def body(val_ref, kcd_ref, qe_ref, kt_ref, in_ref, cy_ref, z_ref, n_ref, dy_ref,
             dval_ref, dkcd_ref, dqe_ref, dkt_ref, din_ref, dcy_ref, dz_ref, dn_ref, o_scr, s_scr):
        big = (val_ref, kcd_ref, qe_ref, kt_ref, in_ref)
        dbig = (dval_ref, dkcd_ref, dqe_ref, dkt_ref, din_ref)

        def step(c, states):
            rows = pl.ds(pl.multiple_of(c * GDN_L, GDN_L), GDN_L)
            for i in heads:
                s_scr[i, c] = states[i]
            states, outs = _gdn_steps(states, [_head_operands(big, cy_ref, c, i) for i in heads])
            for i in heads:
                o_scr[rows, pl.ds(i * LANES, LANES)] = outs[i]
            return states

        zeros = tuple(jnp.zeros((LANES, LANES), F32) for _ in heads)
        lax.fori_loop(0, nc, step, zeros)
        for i in heads:
            lanes = pl.ds(i * LANES, LANES)
            _, vjp_post = jax.vjp(_gdn_post, o_scr[:, lanes], z_ref[:, lanes], n_ref[i])
            do, dz, dn = vjp_post(dy_ref[:, lanes].astype(F32))
            dz_ref[:, lanes] = dz.astype(dz_ref.dtype)
            dn_ref[i] = dn
            o_scr[:, lanes] = do
        dcy_ref[...] = jnp.zeros_like(dcy_ref)

        def rstep(k, dstates):
            c = nc - 1 - k
            rows = pl.ds(pl.multiple_of(c * GDN_L, GDN_L), GDN_L)
            ops = [tuple(o.astype(F32) for o in _head_operands(big, cy_ref, c, i)) for i in heads]
            _, vjp_c = jax.vjp(_gdn_steps, tuple(s_scr[i, c] for i in heads), ops)
            dstates, dops = vjp_c((dstates, tuple(o_scr[rows, pl.ds(i * LANES, LANES)] for i in heads)))
            for i in heads:
                for r, g in zip(dbig, dops[i][:5]):
                    r[rows, pl.ds(i * LANES, LANES)] = g.astype(r.dtype)
                dcy_ref[i, pl.ds(pl.multiple_of(c * 8, 8), 1), :] = dops[i][5]
            return dstates

        lax.fori_loop(0, nc, rstep, zeros)

    col, carry, par = _gdn_scan_specs(t)
    wide = jax.ShapeDtypeStruct((t, GDN_W), BF16)
    return pl.pallas_call(
        body, name="gdn_scan_bwd", grid=(GDN_HEADS // GDN_HPS,),
        in_specs=[col(0)] * 5 + [carry, col(CB_Z), par, col(dy_base)],
        out_specs=[col(0)] * 5 + [carry, col(0), par],
        out_shape=[wide] * 5 + [jax.ShapeDtypeStruct(wy[5].shape, F32), wide,
                                jax.ShapeDtypeStruct((GDN_HEADS, 1, LANES), F32)],
        scratch_shapes=[pltpu.VMEM((t, GDN_HPS * LANES), F32), pltpu.VMEM((GDN_HPS, nc, LANES, LANES), F32)],
        compiler_params=_cparams(("arbitrary",)),
    )(*wy, proj, nrm, dy)


TM = 512
TMW = 1024
TR = 256


def _rms_fwd(name, h, g):
    t, d = h.shape
    tm = min(TR, t)
    return _fused(name, lambda hb, gb: (_rms(hb, gb),), t // tm, [_row_in(h, tm), _row_shared(g)],
                  [_row_out(t, d, BF16, tm)])[0]


def _rms_bwd(name, h, g, dxn, dh_next):
    t, d = h.shape
    tm = min(TR, t)
    dh, dg = _fused(name, lambda hb, gb: (_rms(hb, gb),), t // tm, [_row_in(h, tm), _row_shared(g)],
                    [_row_out(t, d, F32, tm)], douts=[dxn], need=[True, True], add={0: dh_next})
    return dh, dg


def _mixer_ins(proj, lw):
    sgu = [_col_in(proj, CB_U, SG_HEADS), _col_in(proj, CB_V, SG_HEADS), _col_par(lw['sg_ln_g']),
           _col_par(lw['sg_ln_b']), _col_par(lw['sg_w']), _col_par(lw['sg_bb']), _col_par(lw['out_norm_a'])]
    sconv = [_col_in(proj, CB_GB, SC_GROUPS), _col_in(proj, CB_GC, SC_GROUPS), _col_in(proj, CB_X, SC_GROUPS)] + \
            [_col_par(w) for w in lw['sc_taps']] + [_col_par(lw['out_norm_b'])]
    gq = [_col_in(proj, CB_Q, GDN_HEADS)] + [_col_par(w) for w in lw['q_taps']]
    gk = [_col_in(proj, CB_K, GDN_HEADS)] + [_col_par(w) for w in lw['k_taps']]
    gv = [_col_in(proj, CB_VV, GDN_HEADS)] + [_col_par(w) for w in lw['v_taps']]
    return sgu, sconv, gq, gk, gv


def _gates_ins(proj, lw, tm):
    return [_row_in(proj, tm, CB_AB), _row_shared(lw['a_log_row']), _row_shared(lw['dt_bias_row'])]


def _layer_fwd(h, p_l, lw, late=None):
    t, d = h.shape
    xn = _rms_fwd("rms_fwd", h, lw['norm_mix'])
    proj = _matmul("proj_fwd", xn, lw['w_in'], "nn", t, PROJ_W, d, TMW, 1024, d)
    sgu, sconv, gq, gk, gv = _mixer_ins(proj, lw)
    ya = _fused("sgu_fwd", _fn_sgu, SG_HEADS, sgu, [_col_out(t, SG_W, BF16)])[0]
    yb = _fused("sconv_fwd", _fn_sconv, SC_GROUPS, sconv, [_col_out(t, SC_W, BF16)])[0]
    q = _fused("gdn_q_fwd", _fn_gdn_qk, GDN_HEADS, gq, [_col_out(t, GDN_W, F32)])[0]
    k = _fused("gdn_k_fwd", _fn_gdn_qk, GDN_HEADS, gk, [_col_out(t, GDN_W, F32)])[0]
    v = _fused("gdn_v_fwd", _fn_gdn_v, GDN_HEADS, gv, [_col_out(t, GDN_W, F32)])[0]
    tm = min(TR, t)
    gates = _fused("gdn_gates_fwd", _fn_gdn_gates, t // tm, _gates_ins(proj, lw, tm),
                   [_row_out(t, LANES, F32, tm)])[0]
    n_wy, wy_ins, wy_outs = _gdn_wy_ins(q, k, v, gates)
    wy = _fused("gdn_wy_fwd", _fn_gdn_wy, n_wy, wy_ins, wy_outs)
    wy, wy_t = wy[:6], wy[6]
    yc = _gdn_scan_fwd(wy, proj, lw['gdn_norm'])
    ycat = jnp.concatenate([ya, yb, yc], axis=1)
    if late is not None:
        late(ycat)
    dff = lw['w_ff2'].shape[0]
    h2 = _matmul("wo_fwd", ycat, lw['w_o'], "nn", t, d, d, TMW, 1024, d,
                 epilogue=lambda acc, hb: (hb + acc,), extras=(h,))
    hn = _rms_fwd("rms_fwd", h2, lw['norm_ffn'])
    per = d // 1024
    s, r = _matmul("ff1_fwd", hn, lw['w_ff1'], "nn", t, dff, d, TMW, 1024, d,
                   b_spec=pl.BlockSpec((None, d, 1024), lambda i, j, kk: (j // per, 0, j % per)),
                   epilogue=lambda acc: (jnp.maximum(acc, 0.0), jnp.square(jnp.maximum(acc, 0.0))),
                   outs=[((t, dff), BF16, (min(TMW, t), 1024), lambda i, j, kk: (i, j))] * 2)
    h3 = _matmul("ff2_fwd", r, lw['w_ff2'], "nn", t, d, dff, TM, 1024, 4096,
                 epilogue=lambda acc, hb: (hb + acc,), extras=(h2,))
    hn2 = _rms_fwd("rms_fwd", h3, lw['norm_ple'])
    pp = _matmul("ple_proj_fwd", p_l, lw['w_ple_proj'], "nn", t, d, p_l.shape[1], TM, 1024, p_l.shape[1])

    def gate_epilogue(acc, hb, ppb):
        sg = _sigmoid(acc)
        return hb + ppb * sg, sg

    h4, gate = _matmul("ple_gate_fwd", hn2, lw['w_ple_gate'], "nn", t, d, d, TM, 1024, d, epilogue=gate_epilogue,
                       extras=(h3, pp), outs=[((t, d), F32, (min(TM, t), 1024), lambda i, j, kk: (i, j))] * 2)
    saved = dict(h=h, xn=xn, proj=proj, q=q, k=k, v=v, gates=gates, ycat=ycat, h2=h2, hn=hn, s=s, r=r, h3=h3,
                 hn2=hn2, pp=pp, gate=gate, p=p_l, wy=wy, wy_t=wy_t)
    return h4, saved


def _layer_bwd(dh4, sv, lw, early=None):
    t, d = dh4.shape
    dff = lw['w_ff2'].shape[0]
    tm = min(TR, t)
    g = {}
    dacc, dpp = _fused("ple_bwd_gate", lambda dh, pp, gt: (dh * pp * gt * (1.0 - gt), dh * gt), t // tm,
                       [_row_in(dh4, tm), _row_in(sv['pp'], tm), _row_in(sv['gate'], tm)],
                       [_row_out(t, d, BF16, tm)] * 2)
    g['w_ple_gate'] = _matmul("dw_ple_gate", sv['hn2'], dacc, "tn", d, d, t, TM, 1024, t, out_dtype=BF16,
                              n_outer=True)
    g['w_ple_proj'] = _matmul("dw_ple_proj", sv['p'], dpp, "tn", sv['p'].shape[1], d, t, TM, 1024, t,
                              out_dtype=BF16, n_outer=True)
    dhn2 = _matmul("dx_ple_gate", dacc, lw['w_ple_gate'], "nt", t, d, d, TMW, 1024, d)
    dh3, g['norm_ple'] = _rms_bwd("rms_bwd", sv['h3'], lw['norm_ple'], dhn2, dh4)
    da = _matmul("dx_ff2", dh3, lw['w_ff2'], "nt", t, dff, d, TMW, 1024, d,
                 epilogue=lambda acc, sb: (acc * (2.0 * sb.astype(F32)),), extras=(sv['s'],),
                 outs=[((t, dff), BF16, (min(TMW, t), 1024), lambda i, j, kk: (i, j))])
    g['w_ff2'] = _matmul("dw_ff2", sv['r'], dh3, "tn", dff, d, t, TM, 512, t, out_dtype=BF16, n_outer=True)
    per = d // 1024
    dhn = _matmul("dx_ff1", da, lw['w_ff1'], "nt", t, d, dff, TM, 1024, d,
                  b_spec=pl.BlockSpec((None, 1024, d), lambda i, j, kk: (kk, j, 0)))
    g['w_ff1'] = _matmul("dw_ff1", sv['hn'], da, "tn", d, dff, t, TM, 1024, t, n_outer=True,
                         outs=[((N_SHARD, d, d), BF16, (None, TM, 1024), lambda i, j, kk: (j // per, i, j % per))])
    norm_ffn = lw['norm_ffn'] if early is None else lw['norm_ffn'] + early(g)[0, 0]
    dh2, g['norm_ffn'] = _rms_bwd("rms_bwd", sv['h2'], norm_ffn, dhn, dh3)
    dycat = _matmul("dx_o", dh2, lw['w_o'], "nt", t, d, d, TMW, 1024, d)
    g['w_o'] = _matmul("dw_o", sv['ycat'], dh2, "tn", d, d, t, TM, 512, t, out_dtype=BF16, n_outer=True)
    proj = sv['proj']
    sgu, sconv, gq, gk, gv = _mixer_ins(proj, lw)
    bf2 = [BF16, BF16]
    r_ = _fused("sgu_bwd", _fn_sgu, SG_HEADS, sgu, [_col_out(t, d, F32, 0)], douts=[dycat], need=[True] * 7,
                gdt=bf2 + [F32] * 5)
    du, dv_, g['sg_ln_g'], g['sg_ln_b'], g['sg_w'], g['sg_bb'], g['out_norm_a'] = r_
    r_ = _fused("sconv_bwd", _fn_sconv, SC_GROUPS, sconv, [_col_out(t, d, F32, SG_HEADS)], douts=[dycat],
                need=[True] * 7, gdt=[BF16] * 3 + [F32] * 4)
    dgb, dgc, dxin = r_[:3]
    g['sc_taps'], g['out_norm_b'] = r_[3:6], r_[6]
    r_ = _gdn_scan_bwd(sv['wy'], proj, lw['gdn_norm'], dycat, SG_HEADS + SC_GROUPS)
    dwy, dz, g['gdn_norm'] = r_[:6], r_[6], r_[7]
    n_wy, wy_ins, wy_outs = _gdn_wy_ins(sv['q'], sv['k'], sv['v'], sv['gates'], sv['wy_t'])
    dq, dk, dvv, dgates = _fused("gdn_wy_bwd", _fn_gdn_wy, n_wy, wy_ins, wy_outs, douts=dwy,
                                 need=[True] * 4 + [False] * 3)[:4]
    one = [_col_out(t, GDN_W, F32)]
    r_ = _fused("gdn_q_bwd", _fn_gdn_qk, GDN_HEADS, gq, one, douts=[dq], need=[True] * 5, gdt=[BF16] + [F32] * 4)
    dpq, g['q_taps'] = r_[0], r_[1:]
    r_ = _fused("gdn_k_bwd", _fn_gdn_qk, GDN_HEADS, gk, one, douts=[dk], need=[True] * 5, gdt=[BF16] + [F32] * 4)
    dpk, g['k_taps'] = r_[0], r_[1:]
    r_ = _fused("gdn_v_bwd", _fn_gdn_v, GDN_HEADS, gv, one, douts=[dvv], need=[True] * 5, gdt=[BF16] + [F32] * 4)
    dpv, g['v_taps'] = r_[0], r_[1:]
    dab, g['a_log_row'], g['dt_bias_row'] = _fused(
        "gdn_gates_bwd", _fn_gdn_gates, t // tm, _gates_ins(proj, lw, tm), [_row_out(t, LANES, F32, tm)],
        douts=[dgates], need=[True] * 3, gdt=[BF16, F32, F32])
    pad = jnp.zeros((t, PROJ_W - (CB_AB + 1) * LANES), BF16)
    dproj = jnp.concatenate([du, dv_, dgb, dgc, dxin, dpq, dpk, dpv, dz, dab, pad], axis=1)
    dxn = _matmul("dx_in", dproj, lw['w_in'], "nt", t, d, PROJ_W, TM, 1024, PROJ_W // 2)
    g['w_in'] = _matmul("dw_in", sv['xn'], dproj, "tn", d, PROJ_W, t, TM, 1024, t, out_dtype=BF16, n_outer=True)
    dh, g['norm_mix'] = _rms_bwd("rms_bwd", sv['h'], lw['norm_mix'], dxn, dh2)
    return dh, g


def _loss_grad(h, g, tgt):
    t, d = h.shape
    tm = min(TR, t)

    def body(h_ref, g_ref, t_ref, loss_ref, dh_ref, dg_ref):
        y, vjp = jax.vjp(_rms, h_ref[...], g_ref[...])
        e = y - t_ref[...]
        dh, dg = vjp(e * (1.0 / d))

        @pl.when(pl.program_id(0) == 0)
        def _():
            loss_ref[...] = jnp.zeros_like(loss_ref)
            dg_ref[...] = jnp.zeros_like(dg_ref)

        loss_ref[...] += jnp.sum(jnp.sum(e * e, axis=1, keepdims=True), axis=0, keepdims=True) * (0.5 / d)
        dh_ref[...] = dh
        dg_ref[...] += dg

    row = pl.BlockSpec((tm, d), lambda i: (i, 0))
    return pl.pallas_call(
        body, name="loss_grad", grid=(t // tm,),
        in_specs=[row, pl.BlockSpec((1, d), lambda i: (0, 0)), row],
        out_specs=[pl.BlockSpec((1, LANES), lambda i: (0, 0)), row, pl.BlockSpec((1, d), lambda i: (0, 0))],
        out_shape=[jax.ShapeDtypeStruct((1, LANES), F32), jax.ShapeDtypeStruct((t, d), F32),
                   jax.ShapeDtypeStruct((1, d), F32)],
        compiler_params=_cparams(("arbitrary",)),
    )(h, g, tgt)


def _big_weights(full):
    lw = {}
    if 'w_in' in full:
        d = full['w_in'].shape[1]
        w_in = jnp.transpose(full['w_in'], (1, 0, 2)).reshape(d, IN_COLS)
        lw['w_in'] = jnp.pad(w_in, ((0, 0), (0, PROJ_W - IN_COLS)))
    if 'w_o' in full:
        lw['w_o'] = full['w_o'].reshape(-1, full['w_o'].shape[-1])
    if 'w_ff1' in full:
        lw['w_ff1'] = full['w_ff1']
    if 'w_ff2' in full:
        lw['w_ff2'] = full['w_ff2'].reshape(-1, full['w_ff2'].shape[-1])
    if 'w_ple_gate' in full:
        lw['w_ple_gate'] = full['w_ple_gate'].reshape(-1, full['w_ple_gate'].shape[-1])
    if 'w_ple_proj' in full:
        wpp = full['w_ple_proj']
        lw['w_ple_proj'] = jnp.transpose(wpp, (1, 0, 2)).reshape(wpp.shape[1], -1)
    return lw


def _layer_weights(l, full, convs, small):
    d = small['norm_mix'].shape[-1]
    lw = _big_weights(full)
    for n in ('norm_mix', 'norm_ffn', 'norm_ple'):
        lw[n] = small[n][l].reshape(1, d)
    lw['sg_ln_g'] = small['sg_ln_g'][l].reshape(SG_HEADS, 1, LANES)
    lw['sg_ln_b'] = small['sg_ln_b'][l].reshape(SG_HEADS, 1, LANES)
    lw['sg_w'] = small['sg_w'][l]
    lw['sg_bb'] = jnp.broadcast_to(small['sg_b'][l][:, :, None], (SG_HEADS, SG_CHUNK, LANES))
    lw['out_norm_a'] = small['out_norm_a'][l].reshape(SG_HEADS, 1, LANES)
    lw['out_norm_b'] = small['out_norm_b'][l].reshape(SC_GROUPS, 1, LANES)
    lw['gdn_norm'] = jnp.broadcast_to(small['gdn_norm'][l].reshape(1, 1, LANES), (GDN_HEADS, 1, LANES))
    lw['a_log_row'] = jnp.pad(small['gdn_a_log'][l].reshape(1, GDN_HEADS), ((0, 0), (0, LANES - GDN_HEADS)))
    lw['dt_bias_row'] = jnp.pad(small['gdn_dt_bias'][l].reshape(1, GDN_HEADS), ((0, 0), (0, LANES - GDN_HEADS)))
    sc = convs['sc_conv'][l]
    lw['sc_taps'] = [sc[:, j:j + 1, :] for j in range(SC_KERNEL)]
    gc = jnp.transpose(convs['gdn_conv'][l], (1, 0, 2)).reshape(GDN_CONV, 3 * GDN_W)
    for i, nm in enumerate(('q_taps', 'k_taps', 'v_taps')):
        part = gc[:, i * GDN_W:(i + 1) * GDN_W].reshape(GDN_CONV, GDN_HEADS, 1, LANES)
        lw[nm] = [part[j] for j in range(GDN_CONV)]
    return lw


def _shard_major(name, g):
    if name == 'w_in':
        d = g.shape[0]
        return jnp.transpose(g[:, :IN_COLS].reshape(d, N_SHARD, IN_COLS // N_SHARD), (1, 0, 2))
    if name == 'w_ple_proj':
        return jnp.transpose(g.reshape(g.shape[0], N_SHARD, g.shape[1] // N_SHARD), (1, 0, 2))
    if name == 'w_ff1':
        return g
    return g.reshape(N_SHARD, -1, g.shape[-1])


def _weight_grads(g):
    d = g['norm_mix'].shape[-1]
    out = {}
    for n in ('norm_mix', 'norm_ffn', 'norm_ple'):
        out[n] = g[n].reshape(d)
    out['sg_ln_g'] = g['sg_ln_g'].reshape(SG_W)
    out['sg_ln_b'] = g['sg_ln_b'].reshape(SG_W)
    out['sg_w'] = g['sg_w']
    out['sg_b'] = jnp.sum(g['sg_bb'], axis=2)
    out['out_norm_a'] = g['out_norm_a'].reshape(SG_W)
    out['out_norm_b'] = g['out_norm_b'].reshape(SC_W)
    out['gdn_norm'] = jnp.sum(g['gdn_norm'], axis=(0, 1))
    out['gdn_a_log'] = g['a_log_row'][0, :GDN_HEADS]
    out['gdn_dt_bias'] = g['dt_bias_row'][0, :GDN_HEADS]
    out['sc_conv'] = jnp.concatenate([w.reshape(1, SC_W) for w in g['sc_taps']], axis=0)
    taps = [jnp.concatenate([g[nm][j].reshape(1, GDN_W) for nm in ('q_taps', 'k_taps', 'v_taps')], axis=1)
            for j in range(GDN_CONV)]
    out['gdn_conv'] = jnp.concatenate(taps, axis=0)
    return out


ANY = pl.BlockSpec(memory_space=pl.ANY)


def _place():
    x, y, c = lax.axis_index("x"), lax.axis_index("y"), lax.axis_index("c")
    chips = [(1 - x, y), (x, 1 - y), (1 - x, 1 - y)]
    return x, y, c, chips


def _place_shard(wsh, shard, dtype, layer=None, after=None):
    dp, r, cc = wsh.shape
    tr = min(TR, r)
    nb = r // tr

    def body(idx_ref, w_ref, *rest):
        rest[-1][...] = w_ref[...].astype(rest[-1].dtype)

    if layer is None:
        grid, shape = (dp * nb,), (dp, N_SHARD, r, cc)
        in_spec = pl.BlockSpec((None, tr, cc), lambda i, ix: (i // nb, i % nb, 0))
        out_spec = pl.BlockSpec((None, None, tr, cc), lambda i, ix: (i // nb, ix[0], i % nb, 0))
    else:
        grid, shape = (nb,), (N_SHARD, r, cc)
        in_spec = pl.BlockSpec((None, tr, cc), lambda i, ix: (layer, i, 0))
        out_spec = pl.BlockSpec((None, tr, cc), lambda i, ix: (ix[0], i, 0))
    return pl.pallas_call(
        body, name="ag_place_shard",
        grid_spec=pltpu.PrefetchScalarGridSpec(num_scalar_prefetch=1, grid=grid,
                                               in_specs=[in_spec] + ([] if after is None else [ANY]),
                                               out_specs=out_spec),
        out_shape=jax.ShapeDtypeStruct(shape, dtype),
        compiler_params=_cparams(("arbitrary",)),
    )(shard, wsh, *([] if after is None else [after]))


def _rows(ref, slab, half):
    rh = ref.shape[1] // 2
    return ref.at[slab, pl.ds(pl.multiple_of(half * rh, rh), rh), :]


def _gather_direct(bufs):
    n = len(bufs)

    def body(*refs):
        outs = refs[n:2 * n]
        send_sems, recv_sems = refs[2 * n:]
        x, y, c, chips = _place()
        me = 2 * x + y

        def cp(w, l, j, shard, to):
            blk = outs[w].at[l, shard]
            return pltpu.make_async_remote_copy(src_ref=blk, dst_ref=blk, send_sem=send_sems.at[w, l, j],
                                                recv_sem=recv_sems.at[w, l, j], device_id=to, device_id_type=MESH)

        sends = [cp(w, l, j, me, (*chip, c)) for w in range(n) for l in range(DEPTH) for j, chip in enumerate(chips)]
        for s in sends:
            s.start()
        for w in range(n):
            for l in range(DEPTH):
                for j, (px, py) in enumerate(chips):
                    cp(w, l, j, 2 * px + py, (px, py, c)).wait_recv()
        for s in sends:
            s.wait_send()

    return pl.pallas_call(
        body, name="ag_small", in_specs=[ANY] * n, out_specs=[ANY] * n,
        out_shape=[jax.ShapeDtypeStruct(b.shape, b.dtype) for b in bufs],
        input_output_aliases={w: w for w in range(n)},
        scratch_shapes=[pltpu.SemaphoreType.DMA((n, DEPTH, 3)), pltpu.SemaphoreType.DMA((n, DEPTH, 3))],
    )(*bufs)


def _gather_halves(bufs):
    n = len(bufs)

    def body(*refs):
        outs = refs[n:2 * n]
        send_sems, recv_sems = refs[2 * n:]
        x, y, c, chips = _place()
        me = 2 * x + y
        sibling = (x, y, 1 - c)

        def cp(w, k, shard, half, to):
            blk = _rows(outs[w], shard, half)
            return pltpu.make_async_remote_copy(src_ref=blk, dst_ref=blk, send_sem=send_sems.at[w, k],
                                                recv_sem=recv_sems.at[w, k], device_id=to, device_id_type=MESH)

        sends = []
        for w in range(n):
            for j, chip in enumerate(chips):
                s = cp(w, j, me, c, (*chip, c))
                s.start()
                sends.append(s)
        for w in range(n):
            for j, (px, py) in enumerate(chips):
                cp(w, j, 2 * px + py, c, sibling).wait_recv()
                s = cp(w, 3 + j, 2 * px + py, c, sibling)
                s.start()
                sends.append(s)
        for w in range(n):
            for j, (px, py) in enumerate(chips):
                cp(w, 3 + j, 2 * px + py, 1 - c, sibling).wait_recv()
        for s in sends:
            s.wait_send()

    return pl.pallas_call(
        body, name="ag_gather_halves",
        in_specs=[ANY] * n, out_specs=[ANY] * n,
        out_shape=[jax.ShapeDtypeStruct(b.shape, b.dtype) for b in bufs],
        input_output_aliases={w: w for w in range(n)},
        scratch_shapes=[pltpu.SemaphoreType.DMA((n, 6)), pltpu.SemaphoreType.DMA((n, 6))],
    )(*bufs)


HBM = pl.BlockSpec(memory_space=pltpu.HBM)
SEM = pl.BlockSpec(memory_space=pltpu.SEMAPHORE)
_SPLIT = pltpu.CompilerParams(has_side_effects=pltpu.SideEffectType.DATAFLOW_SIDE_EFFECTING)


def _in_hbm(arrs):
    return [pltpu.with_memory_space_constraint(a, pltpu.HBM) for a in arrs]


def _gather_start(tag, bufs, after):
    n, na = len(bufs), len(after)
    k = 3 * n

    def body(*refs):
        b_refs, sems, token = refs[:n], refs[n + na:n + na + 2 * k], refs[-1]
        x, y, c, chips = _place()
        me = 2 * x + y
        for w in range(n):
            for j, chip in enumerate(chips):
                blk = _rows(b_refs[w], me, c)
                pltpu.make_async_remote_copy(src_ref=blk, dst_ref=blk, send_sem=sems[3 * w + j],
                                             recv_sem=sems[k + 3 * w + j], device_id=(*chip, c),
                                             device_id_type=MESH).start()
        token[...] = jnp.zeros_like(token)

    res = pl.pallas_call(
        body, name="ag_start_" + tag,
        out_shape=(*[pltpu.SemaphoreType.DMA(())] * (2 * k), *[pltpu.HBM(b.shape, b.dtype) for b in bufs],
                   jax.ShapeDtypeStruct((8, LANES), F32)),
        in_specs=[HBM] * n + [ANY] * na,
        out_specs=(*[SEM] * (2 * k), *[HBM] * n, pl.BlockSpec(memory_space=pltpu.VMEM)),
        input_output_aliases={w: 2 * k + w for w in range(n)}, compiler_params=_SPLIT,
    )(*_in_hbm(bufs), *after)
    return list(res[:2 * k]), list(res[2 * k:2 * k + n]), res[2 * k + n]


def _gather_wait(tag, sems, bufs, after):
    n = len(bufs)
    k = 3 * n

    def body(*refs):
        b_refs, sems = refs[:n], refs[n:n + 2 * k]
        x, y, c, chips = _place()
        me = 2 * x + y
        for w in range(n):
            for j, (px, py) in enumerate(chips):
                mine, theirs = _rows(b_refs[w], me, c), _rows(b_refs[w], 2 * px + py, c)
                cp = pltpu.make_async_remote_copy(src_ref=mine, dst_ref=theirs, send_sem=sems[3 * w + j],
                                                  recv_sem=sems[k + 3 * w + j], device_id=(px, py, c),
                                                  device_id_type=MESH)
                cp.wait_send()
                cp.wait_recv()

    res = pl.pallas_call(
        body, name="ag_wait_" + tag, out_shape=tuple(pltpu.HBM(b.shape, b.dtype) for b in bufs),
        in_specs=[HBM] * n + [SEM] * (2 * k) + [ANY], out_specs=(HBM,) * n,
        input_output_aliases={w: w for w in range(n)}, compiler_params=_SPLIT,
    )(*bufs, *sems, after)
    return list(res)


def _gather_forward(bufs):
    n = len(bufs)

    def body(*refs):
        outs = refs[n:2 * n]
        send_sems, recv_sems = refs[2 * n:]
        x, y, c, chips = _place()

        def cp(w, j, shard, half):
            blk = _rows(outs[w], shard, half)
            return pltpu.make_async_remote_copy(src_ref=blk, dst_ref=blk, send_sem=send_sems.at[w, j],
                                                recv_sem=recv_sems.at[w, j], device_id=(x, y, 1 - c),
                                                device_id_type=MESH)

        sends = [cp(w, j, 2 * px + py, c) for w in range(n) for j, (px, py) in enumerate(chips)]
        for s in sends:
            s.start()
        for w in range(n):
            for j, (px, py) in enumerate(chips):
                cp(w, j, 2 * px + py, 1 - c).wait_recv()
        for s in sends:
            s.wait_send()

    return pl.pallas_call(
        body, name="ag_forward", in_specs=[ANY] * n, out_specs=[ANY] * n,
        out_shape=[jax.ShapeDtypeStruct(b.shape, b.dtype) for b in bufs],
        input_output_aliases={w: w for w in range(n)},
        scratch_shapes=[pltpu.SemaphoreType.DMA((n, 3)), pltpu.SemaphoreType.DMA((n, 3))],
    )(*bufs)


def _scatter_start(tag, parts, after):
    n, na = len(parts), len(after)
    k = 3 * n
    lands = [lax.empty((3,) + g.shape[1:], g.dtype) for g in parts]

    def body(*refs):
        srcs, dsts, sems, token = refs[:n], refs[n:2 * n], refs[2 * n + na:2 * n + na + 2 * k], refs[-1]
        x, y, c, chips = _place()
        for w in range(n):
            for j, (px, py) in enumerate(chips):
                pltpu.make_async_remote_copy(src_ref=srcs[w].at[2 * px + py], dst_ref=dsts[w].at[j],
                                             send_sem=sems[3 * w + j], recv_sem=sems[k + 3 * w + j],
                                             device_id=(px, py, c), device_id_type=MESH).start()
        token[...] = jnp.zeros_like(token)

    res = pl.pallas_call(
        body, name="rs_scatter_start_" + tag,
        out_shape=(*[pltpu.SemaphoreType.DMA(())] * (2 * k), *[pltpu.HBM(a.shape, a.dtype) for a in parts + lands],
                   jax.ShapeDtypeStruct((8, LANES), F32)),
        in_specs=[HBM] * (2 * n) + [ANY] * na,
        out_specs=(*[SEM] * (2 * k), *[HBM] * (2 * n), pl.BlockSpec(memory_space=pltpu.VMEM)),
        input_output_aliases={w: 2 * k + w for w in range(2 * n)}, compiler_params=_SPLIT,
    )(*_in_hbm(parts + lands), *after)
    return list(res[:2 * k]), list(res[2 * k:2 * k + n]), list(res[2 * k + n:2 * k + 2 * n]), res[2 * k + 2 * n]


def _scatter_wait(tag, sems, parts, lands, after):
    n = len(parts)
    k = 3 * n

    def body(*refs):
        srcs, dsts, sems = refs[:n], refs[n:2 * n], refs[2 * n:2 * n + 2 * k]
        x, y, c, chips = _place()
        for w in range(n):
            for j, (px, py) in enumerate(chips):
                cp = pltpu.make_async_remote_copy(src_ref=srcs[w].at[2 * px + py], dst_ref=dsts[w].at[j],
                                                  send_sem=sems[3 * w + j], recv_sem=sems[k + 3 * w + j],
                                                  device_id=(px, py, c), device_id_type=MESH)
                cp.wait_send()
                cp.wait_recv()

    res = pl.pallas_call(
        body, name="rs_scatter_wait_" + tag, out_shape=tuple(pltpu.HBM(a.shape, a.dtype) for a in parts + lands),
        in_specs=[HBM] * (2 * n) + [SEM] * (2 * k) + [ANY], out_specs=(HBM,) * (2 * n),
        input_output_aliases={w: w for w in range(2 * n)}, compiler_params=_SPLIT,
    )(*parts, *lands, *sems, after)
    return list(res[:n]), list(res[n:])


def _swap_halves(grads):
    n = len(grads)

    def body(*refs):
        srcs, outs = refs[:n], refs[n:2 * n]
        send_sems, recv_sems = refs[2 * n:]
        x, y, c, _ = _place()
        cps = []
        for w in range(n):
            rh = srcs[w].shape[1] // 2
            theirs = srcs[w].at[:, pl.ds(pl.multiple_of((1 - c) * rh, rh), rh), :]
            cps.append(pltpu.make_async_remote_copy(src_ref=theirs, dst_ref=outs[w], send_sem=send_sems.at[w],
                                                    recv_sem=recv_sems.at[w], device_id=(x, y, 1 - c),
                                                    device_id_type=MESH))
        for cpy in cps:
            cpy.start()
        for cpy in cps:
            cpy.wait()

    return pl.pallas_call(
        body, name="rs_swap_halves", in_specs=[ANY] * n, out_specs=[ANY] * n,
        out_shape=[jax.ShapeDtypeStruct((g.shape[0], g.shape[1] // 2, g.shape[2]), g.dtype) for g in grads],
        scratch_shapes=[pltpu.SemaphoreType.DMA((n,)), pltpu.SemaphoreType.DMA((n,))],
    )(*grads)


def _scatter_shards(parts):
    n = len(parts)

    def body(*refs):
        srcs, outs = refs[:n], refs[n:2 * n]
        send_sems, recv_sems = refs[2 * n:]
        x, y, c, chips = _place()

        def cp(w, j, src_shard, to):
            return pltpu.make_async_remote_copy(
                src_ref=srcs[w].at[src_shard], dst_ref=outs[w].at[j], send_sem=send_sems.at[w, j],
                recv_sem=recv_sems.at[w, j], device_id=to, device_id_type=MESH)

        sends = [cp(w, j, 2 * px + py, (px, py, c)) for w in range(n) for j, (px, py) in enumerate(chips)]
        for s in sends:
            s.start()
        for s in sends:
            s.wait()

    return pl.pallas_call(
        body, name="rs_scatter", in_specs=[ANY] * n, out_specs=[ANY] * n,
        out_shape=[jax.ShapeDtypeStruct((3,) + g.shape[1:], g.dtype) for g in parts],
        scratch_shapes=[pltpu.SemaphoreType.DMA((n, 3)), pltpu.SemaphoreType.DMA((n, 3))],
    )(*parts)


def _sum_reduced(part, slots, shard, core, layer, prev):
    _, rh, cc = part.shape
    tr = min(TR, rh)
    nbh = rh // tr

    def body(shard_ref, core_ref, p_ref, s0_ref, s1_ref, s2_ref, *rest):
        acc = p_ref[...].astype(F32)
        for s_ref in (s0_ref, s1_ref, s2_ref):
            acc = acc + s_ref[...].astype(F32)
        rest[-1][...] = acc

    slot = lambda j: pl.BlockSpec((None, tr, cc), lambda i, sh, co: (j, i, 0))
    ins = [shard, core, part, slots, slots, slots] + ([] if prev is None else [prev])
    return pl.pallas_call(
        body, name="rs_sum_reduced",
        grid_spec=pltpu.PrefetchScalarGridSpec(
            num_scalar_prefetch=2, grid=(nbh,),
            in_specs=[pl.BlockSpec((None, tr, cc), lambda i, sh, co: (sh[0], i, 0)), slot(0), slot(1), slot(2)]
            + ([] if prev is None else [ANY]),
            out_specs=pl.BlockSpec((None, tr, cc), lambda i, sh, co: (layer, co[0] * nbh + i, 0))),
        out_shape=jax.ShapeDtypeStruct((DEPTH, 2 * rh, cc), F32),
        input_output_aliases={} if prev is None else {6: 0},
        compiler_params=_cparams(("arbitrary",)),
    )(*ins)


def _join_halves(bufs, layer):
    n = len(bufs)

    def body(*refs):
        outs = refs[n:2 * n]
        send_sems, recv_sems = refs[2 * n:]
        x, y, c, _ = _place()

        def cp(w, half):
            blk = _rows(outs[w], layer, half)
            return pltpu.make_async_remote_copy(src_ref=blk, dst_ref=blk, send_sem=send_sems.at[w],
                                                recv_sem=recv_sems.at[w], device_id=(x, y, 1 - c), device_id_type=MESH)

        sends = [cp(w, c) for w in range(n)]
        for s in sends:
            s.start()
        for w in range(n):
            cp(w, 1 - c).wait_recv()
        for s in sends:
            s.wait_send()

    return pl.pallas_call(
        body, name="rs_join_halves", in_specs=[ANY] * n, out_specs=[ANY] * n,
        out_shape=[jax.ShapeDtypeStruct(b.shape, b.dtype) for b in bufs],
        input_output_aliases={w: w for w in range(n)},
        scratch_shapes=[pltpu.SemaphoreType.DMA((n,)), pltpu.SemaphoreType.DMA((n,))],
    )(*bufs)


def _exchange_small(vec):
    def body(src, out, send_sems, recv_sems, local_sem):
        x, y, c, _ = _place()
        me = 4 * x + 2 * y + c
        lc = pltpu.make_async_copy(src, out.at[me], local_sem)
        lc.start()
        sends = []
        for k in range(1, N_DEV):
            fx, fy, fc = (k >> 2) & 1, (k >> 1) & 1, k & 1
            to = (x ^ fx, y ^ fy, c ^ fc)
            s = pltpu.make_async_remote_copy(src_ref=src, dst_ref=out.at[me], send_sem=send_sems.at[k - 1],
                                             recv_sem=recv_sems.at[k - 1], device_id=to, device_id_type=MESH)
            s.start()
            sends.append(s)
        for k in range(1, N_DEV):
            fx, fy, fc = (k >> 2) & 1, (k >> 1) & 1, k & 1
            frm = 4 * (x ^ fx) + 2 * (y ^ fy) + (c ^ fc)
            pltpu.make_async_remote_copy(src_ref=src, dst_ref=out.at[frm], send_sem=send_sems.at[k - 1],
                                         recv_sem=recv_sems.at[k - 1], device_id=(x ^ fx, y ^ fy, c ^ fc),
                                         device_id_type=MESH).wait_recv()
        for s in sends:
            s.wait_send()
        lc.wait()

    return pl.pallas_call(
        body, name="small_exchange", in_specs=[ANY], out_specs=ANY,
        out_shape=jax.ShapeDtypeStruct((N_DEV,) + vec.shape, vec.dtype),
        scratch_shapes=[pltpu.SemaphoreType.DMA((N_DEV - 1,)), pltpu.SemaphoreType.DMA((N_DEV - 1,)),
                        pltpu.SemaphoreType.DMA],
    )(vec)


def _add_own_half(g, other, c_arr):
    ns, rh, cc = other.shape
    tr = min(TR, rh)
    nb = rh // tr

    def body(c_ref, g_ref, o_ref, out_ref):
        out_ref[...] = (g_ref[...].astype(F32) + o_ref[...].astype(F32)).astype(out_ref.dtype)

    slab = pl.BlockSpec((None, tr, cc), lambda i, cr: (i // nb, i % nb, 0))
    return pl.pallas_call(
        body, name="rs_add_own_half",
        grid_spec=pltpu.PrefetchScalarGridSpec(
            num_scalar_prefetch=1, grid=(ns * nb,),
            in_specs=[pl.BlockSpec((None, tr, cc), lambda i, cr: (i // nb, cr[0] * nb + i % nb, 0)), slab],
            out_specs=slab),
        out_shape=jax.ShapeDtypeStruct(other.shape, other.dtype),
        compiler_params=_cparams(("arbitrary",)),
    )(c_arr, g, other)


def _sum_slots(name, a):
    ns, r, cc = a.shape
    tr = min(TR, r)
    ins = [_In(a, (None, tr, cc), functools.partial(lambda s, i: (s, i, 0), s)) for s in range(ns)]

    def fn(*blocks):
        acc = blocks[0].astype(F32)
        for b in blocks[1:]:
            acc = acc + b.astype(F32)
        return (acc,)

    return _fused(name, fn, r // tr, ins, [((r, cc), F32, (tr, cc), lambda i: (i, 0))])[0]


def _adamw_fn(w, g, m, v):
    m = ADAM_B1 * m + (1.0 - ADAM_B1) * g
    v = ADAM_B2 * v + (1.0 - ADAM_B2) * jnp.square(g)
    m_hat = m / (1.0 - ADAM_B1 ** ADAM_STEP)
    v_hat = v / (1.0 - ADAM_B2 ** ADAM_STEP)
    delta = -ADAM_LR * (m_hat / (jnp.sqrt(v_hat) + ADAM_EPS) + ADAM_WD * w)
    return delta, m, v


def _adamw(w, g, m, v):
    r, cc = w.shape
    tr = min(TR, r)
    ins = [_In(a, (tr, cc), lambda i: (i, 0)) for a in (w, g, m, v)]
    return _fused("adamw", _adamw_fn, r // tr, ins, [((r, cc), F32, (tr, cc), lambda i: (i, 0))] * 3)


def _pack(arrs):
    flat = jnp.concatenate([a.reshape(-1) for a in arrs])
    tile = TR * LANES
    n = -(-flat.shape[0] // tile) * tile
    return jnp.pad(flat, (0, n - flat.shape[0])).reshape(-1, LANES)


def _unpack(vec, shapes):
    flat = vec.reshape(-1)
    out, o = [], 0
    for s in shapes:
        n = math.prod(s)
        out.append(flat[o:o + n].reshape(s))
        o += n
    return out


def kernel(x, p, norm_mix, w_in, sg_ln_g, sg_ln_b, sg_w, sg_b, sc_conv, gdn_conv, gdn_a_log, gdn_dt_bias, gdn_norm, out_norm_a, out_norm_b, w_o, norm_ffn, w_ff1, w_ff2, norm_ple, w_ple_gate, w_ple_proj, norm_final, loss_target, m_norm_mix, m_w_in, m_sg_ln_g, m_sg_ln_b, m_sg_w, m_sg_b, m_sc_conv, m_gdn_conv, m_gdn_a_log, m_gdn_dt_bias, m_gdn_norm, m_out_norm_a, m_out_norm_b, m_w_o, m_norm_ffn, m_w_ff1, m_w_ff2, m_norm_ple, m_w_ple_gate, m_w_ple_proj, m_norm_final, v_norm_mix, v_w_in, v_sg_ln_g, v_sg_ln_b, v_sg_w, v_sg_b, v_sc_conv, v_gdn_conv, v_gdn_a_log, v_gdn_dt_bias, v_gdn_norm, v_out_norm_a, v_out_norm_b, v_w_o, v_norm_ffn, v_w_ff1, v_w_ff2, v_norm_ple, v_w_ple_gate, v_w_ple_proj, v_norm_final):
    given = dict(locals())
    w = {n: given[n] for n in WEIGHTS}
    m = {n: given['m_' + n] for n in WEIGHTS}
    v = {n: given['v_' + n] for n in WEIGHTS}
    shard = 2 * lax.axis_index("x") + lax.axis_index("y")
    core = lax.axis_index("c")

    shard_arr = shard.reshape(1).astype(jnp.int32)
    c_arr = core.reshape(1).astype(jnp.int32)
    convs = dict(zip(CONVS, _gather_direct([_place_shard(w[n], shard_arr, F32) for n in CONVS])))
    small = {n: w[n] for n in SMALL if n not in CONVS}
    late_w = [n for n in BIG if n != 'w_in']
    flyw = _gather_start("w", [_place_shard(w['w_in'], shard_arr, BF16, 0)], list(convs.values()))
    placed0 = [_place_shard(w[n], shard_arr, BF16, 0, flyw[2]) for n in late_w]
    placed1 = [_place_shard(w[n], shard_arr, BF16, 1, flyw[2]) for n in BIG]
    fly0 = _gather_start("l0", placed0, [flyw[2]])
    fly1 = _gather_start("l1", placed1, [fly0[2]])
    first = _gather_forward(_gather_wait("w", flyw[0], flyw[1], fly1[2]))

    lws = [_layer_weights(0, {'w_in': first[0]}, convs, small), None]
    lws[0]['norm_mix'] = lws[0]['norm_mix'] + fly1[2][0, 0]

    def rest_of_layer0(behind):
        got = _gather_forward(_gather_wait("l0", fly0[0], fly0[1], behind))
        lws[0].update(_big_weights(dict(zip(late_w, got))))

    h, sv0 = _layer_fwd(x[0], p[0, 0], lws[0], rest_of_layer0)
    got = _gather_forward(_gather_wait("l1", fly1[0], fly1[1], h))
    lws[1] = _layer_weights(1, dict(zip(BIG, got)), convs, small)
    h, sv1 = _layer_fwd(h, p[1, 0], lws[1])
    saved = [sv0, sv1]
    loss, dh, dnf = _loss_grad(h, small['norm_final'].reshape(1, -1), loss_target[0])

    def swap_add(names, g):
        big = [_shard_major(n, g[n]) for n in names]
        return [_add_own_half(g_, o, c_arr) for g_, o in zip(big, _swap_halves(big))]

    early_g = ['w_ff1', 'w_ff2', 'w_ple_gate', 'w_ple_proj']
    late_g = [n for n in BIG if n not in early_g]
    per_layer = [None] * DEPTH
    dh, g = _layer_bwd(dh, saved[1], lws[1])
    per_layer[1] = _weight_grads(g)
    fly1 = _scatter_start("l1", swap_add(BIG, g), [dh])
    lws[0]['norm_ple'] = lws[0]['norm_ple'] + fly1[3][0, 0]
    fly0 = []

    def early_grads(g0):
        fly0.extend(_scatter_start("l0", swap_add(early_g, g0), []))
        return fly0[3]

    dh, g = _layer_bwd(dh, saved[0], lws[0], early_grads)
    per_layer[0] = _weight_grads(g)
    flyl = _scatter_start("l0b", swap_add(late_g, g), [])
    groups = [(1, BIG, *_scatter_wait("l1", fly1[0], fly1[1], fly1[2], flyl[3])),
              (0, early_g, *_scatter_wait("l0", fly0[0], fly0[1], fly0[2], flyl[3]))]
    g_big = {}

    def finish(groups):
        for l, names, parts, slots in groups:
            sums = [_sum_reduced(pt, sl, shard_arr, c_arr, l, g_big.get(n)) for n, pt, sl in zip(names, parts, slots)]
            g_big.update(zip(names, _join_halves(sums, l)))

    delta, new_m, new_v, grad_w = {}, {}, {}, {}

    def update(names):
        for n in names:
            shp = w[n].shape
            two_d = lambda a: a.reshape(-1, shp[-1])
            d_, m_, v_ = _adamw(two_d(w[n]), two_d(g_big[n]), two_d(m[n]), two_d(v[n]))
            delta[n], new_m[n], new_v[n], grad_w[n] = d_.reshape(shp), m_.reshape(shp), v_.reshape(shp), g_big[n]

    finish(groups)
    update(early_g)
    finish([(0, late_g, *_scatter_wait("l0b", flyl[0], flyl[1], flyl[2], delta[early_g[0]]))])
    update(late_g)
    grad_x = dh
    grads = {n: jnp.stack([per_layer[l][n] for l in range(DEPTH)], axis=0) for n in SMALL if n != 'norm_final'}
    grads['norm_final'] = dnf.reshape(-1)

    rep = [n for n in SMALL if n not in CONVS]
    names = rep + CONVS
    vec = _pack([grads[n] for n in names] + [loss[0, :1]])
    total = _sum_slots("small_sum", _exchange_small(vec))
    parts_small = _unpack(total, [grads[n].shape for n in names] + [(1,)])
    g_small = dict(zip(names, parts_small[:-1]))
    loss_out = parts_small[-1].reshape(())
    for n in CONVS:
        width = w[n].shape[-1]
        g_small[n] = lax.dynamic_slice_in_dim(g_small[n], shard * width, width, axis=2)

    shapes = [w[n].shape for n in SMALL]
    d_, m_, v_ = _adamw(_pack([w[n] for n in SMALL]), _pack([g_small[n] for n in SMALL]),
                        _pack([m[n] for n in SMALL]), _pack([v[n] for n in SMALL]))
    for n, dd, mm, vv in zip(SMALL, _unpack(d_, shapes), _unpack(m_, shapes), _unpack(v_, shapes)):
        delta[n], new_m[n], new_v[n], grad_w[n] = dd, mm, vv, g_small[n]

    return (loss_out, grad_x[None], *[grad_w[n] for n in WEIGHTS], *[delta[n] for n in WEIGHTS],
            *[new_m[n] for n in WEIGHTS], *[new_v[n] for n in WEIGHTS])
```

```python
import functools
import math

import jax
import jax.numpy as jnp
from jax import lax
from jax.experimental import pallas as pl
from jax.experimental.pallas import tpu as pltpu

F32 = jnp.float32
BF16 = jnp.bfloat16
HI = lax.Precision.HIGH
MESH = pl.DeviceIdType.MESH

LANES = 128
EPS = 1e-6
SG_HEADS, SG_CHUNK = 4, 128
SC_GROUPS, SC_KERNEL = 4, 3
GDN_HEADS, GDN_CONV = 8, 4
GDN_L = 128
GDN_CPS = 8
GDN_HPS = 2
SG_W = SG_HEADS * LANES
SC_W = SC_GROUPS * LANES
GDN_W = GDN_HEADS * LANES
IN_COLS = 2 * SG_W + 3 * SC_W + 4 * GDN_W + 2 * GDN_HEADS
PROJ_W = 7168
CB_U, CB_V = 0, 4
CB_GB, CB_GC, CB_X = 8, 12, 16
CB_Q, CB_K, CB_VV, CB_Z, CB_AB = 20, 28, 36, 44, 52
N_SHARD = 4
N_DEV = 8
DEPTH = 2

ADAM_LR, ADAM_B1, ADAM_B2, ADAM_EPS, ADAM_WD, ADAM_STEP = 0.001, 0.9, 0.999, 1e-08, 0.01, 10

VMEM_LIMIT = 56 << 20

WEIGHTS = ['norm_mix', 'w_in', 'sg_ln_g', 'sg_ln_b', 'sg_w', 'sg_b', 'sc_conv', 'gdn_conv', 'gdn_a_log',
           'gdn_dt_bias', 'gdn_norm', 'out_norm_a', 'out_norm_b', 'w_o', 'norm_ffn', 'w_ff1', 'w_ff2', 'norm_ple',
           'w_ple_gate', 'w_ple_proj', 'norm_final']
BIG = ['w_in', 'w_o', 'w_ff1', 'w_ff2', 'w_ple_gate', 'w_ple_proj']
CONVS = ['sc_conv', 'gdn_conv']
SMALL = [n for n in WEIGHTS if n not in BIG]


def _cparams(sem=None):
    return pltpu.CompilerParams(dimension_semantics=sem, vmem_limit_bytes=VMEM_LIMIT)


def _dot(a, b, dims, prec=None):
    return lax.dot_general(a, b, (dims, ((), ())), precision=prec, preferred_element_type=F32)


def _mm(a, b):
    return _dot(a, b, ((1,), (0,)), HI)


def _mm_nt(a, b):
    return _dot(a, b, ((1,), (1,)), HI)


def _mm_tn(a, b):
    return _dot(a, b, ((0,), (0,)), HI)


@jax.custom_vjp
def _bmm(a, b):
    return _dot(a.astype(BF16), b.astype(BF16), ((1,), (0,)))


def _bmm_fwd(a, b):
    return _bmm(a, b), (a, b)


def _bmm_bwd(res, g):
    a, b = res
    gb = g.astype(BF16)
    return _dot(gb, b.astype(BF16), ((1,), (1,))), _dot(a.astype(BF16), gb, ((0,), (0,)))


_bmm.defvjp(_bmm_fwd, _bmm_bwd)


@jax.custom_vjp
def _bmm_tn(a, b):
    return _dot(a.astype(BF16), b.astype(BF16), ((0,), (0,)))


def _bmm_tn_fwd(a, b):
    return _bmm_tn(a, b), (a, b)


def _bmm_tn_bwd(res, g):
    a, b = res
    gb = g.astype(BF16)
    return _dot(b.astype(BF16), gb, ((1,), (1,))), _dot(a.astype(BF16), gb, ((1,), (0,)))


_bmm_tn.defvjp(_bmm_tn_fwd, _bmm_tn_bwd)


def _sigmoid(x):
    return 1.0 / (1.0 + jnp.exp(-x))


def _silu(x):
    return x * _sigmoid(x)


def _gelu(x):
    c = math.sqrt(2.0 / math.pi)
    return 0.5 * x * (1.0 + jnp.tanh(c * (x + 0.044715 * (x * x * x))))


def _softplus(x):
    return jnp.maximum(x, 0.0) + jnp.log(1.0 + jnp.exp(-jnp.abs(x)))


def _rms(x, g):
    return x * lax.rsqrt(jnp.mean(x * x, axis=-1, keepdims=True) + EPS) * g


def _roll_rows(x, shift):
    return pltpu.roll(x, shift % x.shape[0], 0)


@functools.partial(jax.custom_vjp, nondiff_argnums=(1,))
def _shift_down(x, j):
    row = lax.broadcasted_iota(jnp.int32, x.shape, 0)
    return jnp.where(row >= j, _roll_rows(x, j), 0.0)


def _shift_down_fwd(x, j):
    return _shift_down(x, j), None


def _shift_down_bwd(j, _, dy):
    row = lax.broadcasted_iota(jnp.int32, dy.shape, 0)
    return (jnp.where(row < dy.shape[0] - j, _roll_rows(dy, -j), 0.0),)


_shift_down.defvjp(_shift_down_fwd, _shift_down_bwd)


def _causal_conv(x, taps):
    k = len(taps)
    y = taps[k - 1] * x
    for j in range(k - 1):
        y = y + taps[j] * _shift_down(x, k - 1 - j)
    return y


class _In:
    def __init__(self, arr, block, imap, shared=False, gshape=None, gmap=None):
        self.arr, self.block, self.imap, self.shared = arr, block, imap, shared
        self.gshape = arr.shape if gshape is None else gshape
        self.gmap = imap if gmap is None else gmap


def _fused(name, fn, n, ins, outs, douts=None, need=None, gdt=None, add=None):
    n_in, n_out = len(ins), len(outs)
    in_specs = [pl.BlockSpec(s.block, s.imap) for s in ins]
    out_specs = [pl.BlockSpec(bs, im) for _, _, bs, im in outs]
    if douts is None:
        def body(*refs):
            res = fn(*[r[...] for r in refs[:n_in]])
            for r, v in zip(refs[n_in:], res):
                r[...] = v.astype(r.dtype)

        return pl.pallas_call(
            body, name=name, grid=(n,), in_specs=in_specs, out_specs=out_specs,
            out_shape=[jax.ShapeDtypeStruct(s, d) for s, d, _, _ in outs],
            compiler_params=_cparams(("arbitrary",)),
        )(*[s.arr for s in ins])

    gdt = list(gdt) if gdt is not None else [F32] * n_in
    add = dict(add or {})
    gidx = [i for i in range(n_in) if need[i]]
    aidx = [i for i in gidx if i in add]

    def body(*refs):
        in_refs, d_refs = refs[:n_in], refs[n_in:n_in + n_out]
        a_refs = dict(zip(aidx, refs[n_in + n_out:n_in + n_out + len(aidx)]))
        g_refs = refs[n_in + n_out + len(aidx):]
        vals = [r[...] for r in in_refs]

        def f(*dv):
            full = list(vals)
            for i, v in zip(gidx, dv):
                full[i] = v
            return tuple(o.astype(F32) for o in fn(*full))

        _, vjp = jax.vjp(f, *[vals[i].astype(F32) for i in gidx])
        grads = vjp(tuple(r[...].astype(F32) for r in d_refs))
        for i, g_ref, g in zip(gidx, g_refs, grads):
            if i in a_refs:
                g = g + a_refs[i][...].astype(F32)
            if ins[i].shared:
                period = n if ins[i].shared is True else ins[i].shared

                @pl.when(pl.program_id(0) % period == 0)
                def _():
                    g_ref[...] = jnp.zeros_like(g_ref)
                g_ref[...] += g.astype(g_ref.dtype)
            else:
                g_ref[...] = g.astype(g_ref.dtype)

    g_specs = [pl.BlockSpec(ins[i].block, ins[i].gmap) for i in gidx]
    g_shape = [jax.ShapeDtypeStruct(ins[i].gshape, gdt[i]) for i in gidx]
    res = pl.pallas_call(
        body, name=name, grid=(n,), in_specs=in_specs + out_specs + [g_specs[gidx.index(i)] for i in aidx],
        out_specs=g_specs, out_shape=g_shape,
        compiler_params=_cparams(("arbitrary",)),
    )(*[s.arr for s in ins], *douts, *[add[i] for i in aidx])
    full = [None] * n_in
    for i, g in zip(gidx, res):
        full[i] = g
    return full


def _row_in(a, tm, cb=None):
    if cb is None:
        return _In(a, (tm, a.shape[1]), lambda i: (i, 0))
    return _In(a, (tm, LANES), lambda i: (i, cb), gshape=(a.shape[0], LANES), gmap=lambda i: (i, 0))


def _row_shared(a):
    return _In(a, a.shape, lambda i: (0, 0), shared=True)


def _row_out(t, w, dt, tm):
    return ((t, w), dt, (tm, w), lambda i: (i, 0))


def _col_in(a, base, nblk):
    t = a.shape[0]
    return _In(a, (t, LANES), lambda i: (0, base + i), gshape=(t, nblk * LANES), gmap=lambda i: (0, i))


def _col_par(a):
    return _In(a, (None,) + a.shape[1:], lambda i: (i, 0, 0))


def _col_out(t, w, dt, base=0):
    return ((t, w), dt, (t, LANES), lambda i: (0, base + i))


def _matmul(name, a, b, mode, m, n, k, tm, tn, tk, b_spec=None, epilogue=None, extras=(), outs=None, out_dtype=F32,
            n_outer=False):
    tm, tn, tk = min(tm, m), min(tn, n), min(tk, k)
    assert m % tm == 0 and n % tn == 0 and k % tk == 0, (name, m, n, k, tm, tn, tk)
    nk = k // tk
    a_spec = (pl.BlockSpec((tk, tm), lambda i, j, kk: (kk, i)) if mode == "tn"
              else pl.BlockSpec((tm, tk), lambda i, j, kk: (i, kk)))
    if b_spec is None:
        b_spec = (pl.BlockSpec((tn, tk), lambda i, j, kk: (j, kk)) if mode == "nt"
                  else pl.BlockSpec((tk, tn), lambda i, j, kk: (kk, j)))
    dims = {"nn": ((1,), (0,)), "nt": ((1,), (1,)), "tn": ((0,), (0,))}[mode]
    if outs is None:
        outs = [((m, n), out_dtype, (tm, tn), lambda i, j, kk: (i, j))]
    if epilogue is None:
        epilogue = lambda acc: (acc,)
    n_ex = len(extras)

    def body(*refs):
        a_ref, b_ref = refs[0], refs[1]
        ex_refs = refs[2:2 + n_ex]
        o_refs = refs[2 + n_ex:2 + n_ex + len(outs)]
        part = _dot(a_ref[...].astype(BF16), b_ref[...].astype(BF16), dims)

        def finish(acc):
            for r, v in zip(o_refs, epilogue(acc, *[e[...] for e in ex_refs])):
                r[...] = v.astype(r.dtype)

        if nk == 1:
            finish(part)
        else:
            acc_ref = refs[-1]
            kk = pl.program_id(2)

            @pl.when(kk == 0)
            def _():
                acc_ref[...] = part

            @pl.when(kk > 0)
            def _():
                acc_ref[...] += part

            @pl.when(kk == nk - 1)
            def _():
                finish(acc_ref[...])

    ex_specs = [pl.BlockSpec((tm, tn), lambda i, j, kk: (i, j)) for _ in extras]
    out_specs = [pl.BlockSpec(bs, im) for _, _, bs, im in outs]
    in_specs = [a_spec, b_spec] + ex_specs
    grid = (m // tm, n // tn, nk)
    if n_outer:
        swap = lambda sp: pl.BlockSpec(sp.block_shape, functools.partial(lambda f, j, i, kk: f(i, j, kk), sp.index_map))
        in_specs, out_specs, grid = [swap(sp) for sp in in_specs], [swap(sp) for sp in out_specs], (n // tn, m // tm, nk)
    res = pl.pallas_call(
        body, name=name, grid=grid,
        in_specs=in_specs,
        out_specs=out_specs,
        out_shape=[jax.ShapeDtypeStruct(s, d) for s, d, _, _ in outs],
        scratch_shapes=[pltpu.VMEM((tm, tn), F32)] if nk > 1 else [],
        compiler_params=_cparams(("parallel", "parallel", "arbitrary")),
    )(a, b, *extras)
    return res if len(res) > 1 else res[0]


def _fn_sgu(u_pre, v_pre, ln_g, ln_b, w, bb, na):
    t = u_pre.shape[0]
    u = _gelu(u_pre)
    v = _gelu(v_pre)
    mu = jnp.mean(v, axis=-1, keepdims=True)
    vc = v - mu
    vh = vc * lax.rsqrt(jnp.mean(vc * vc, axis=-1, keepdims=True) + EPS) * ln_g + ln_b
    ri = lax.broadcasted_iota(jnp.int32, w.shape, 0)
    ci = lax.broadcasted_iota(jnp.int32, w.shape, 1)
    wc = jnp.where(ri >= ci, w, 0.0)
    f = jnp.concatenate([_bmm(wc, vh[c * SG_CHUNK:(c + 1) * SG_CHUNK]) + bb for c in range(t // SG_CHUNK)], axis=0)
    return (_rms(u * f, na),)


def _fn_sconv(gb, gc, xin, w0, w1, w2, nb):
    return (_rms(gb * _causal_conv(gc * xin, (w0, w1, w2)), nb),)


def _fn_gdn_qk(pre, w0, w1, w2, w3):
    a = _silu(_causal_conv(pre, (w0, w1, w2, w3)))
    return (a * lax.rsqrt(jnp.sum(a * a, axis=-1, keepdims=True) + EPS),)


def _fn_gdn_v(pre, w0, w1, w2, w3):
    return (_silu(_causal_conv(pre, (w0, w1, w2, w3))),)


def _fn_gdn_gates(ab, a_log, dt_bias):
    lane = lax.broadcasted_iota(jnp.int32, ab.shape, 1)
    g = -jnp.exp(a_log) * _softplus(ab + dt_bias)
    return (jnp.where(lane < GDN_HEADS, g, jnp.where(lane < 2 * GDN_HEADS, _sigmoid(ab), 0.0)),)


def _solve_unit_lower_impl(mats):
    n = mats[0].shape[0]
    ri = lax.broadcasted_iota(jnp.int32, (n, n), 0)
    ci = lax.broadcasted_iota(jnp.int32, (n, n), 1)
    ts = [(ri == ci).astype(F32)] * len(mats)
    sh = 0
    while (1 << sh) < n:
        rb = jnp.right_shift(ri, sh)
        cb = jnp.right_shift(ci, sh)
        off = ((rb & 1) == 1) & (cb == rb - 1)
        us = [_mm(t, jnp.where(off, a, 0.0)) for t, a in zip(ts, mats)]
        ts = [t - _mm(u, t) for t, u in zip(ts, us)]
        sh += 1
    return tuple(ts)


def _solve_cotangents(ts, dts):
    us = [_mm_nt(dt, t) for t, dt in zip(ts, dts)]
    return tuple(-_mm_tn(t, u) for t, u in zip(ts, us))


@jax.custom_vjp
def _solve_unit_lower(mats):
    return _solve_unit_lower_impl(mats)


def _solve_unit_lower_fwd(mats):
    ts = _solve_unit_lower_impl(mats)
    return ts, ts


def _solve_unit_lower_bwd(ts, dts):
    return (_solve_cotangents(ts, dts),)


_solve_unit_lower.defvjp(_solve_unit_lower_fwd, _solve_unit_lower_bwd)


@jax.custom_vjp
def _solved_unit_lower(mats, ts):
    return ts


def _solved_unit_lower_fwd(mats, ts):
    return ts, ts


def _solved_unit_lower_bwd(ts, dts):
    return _solve_cotangents(ts, dts), tuple(jnp.zeros_like(t) for t in ts)


_solved_unit_lower.defvjp(_solved_unit_lower_fwd, _solved_unit_lower_bwd)


def _fn_gdn_wy(q, k, v, gates, pick_g, pick_b, t_saved=None):
    n, dk = GDN_L, q.shape[1]
    rows = [slice(c * n, (c + 1) * n) for c in range(q.shape[0] // n)]
    ri = lax.broadcasted_iota(jnp.int32, (n, n), 0)
    ci = lax.broadcasted_iota(jnp.int32, (n, n), 1)
    incl = ri >= ci
    eye = (ri == ci).astype(F32)
    last = lax.broadcasted_iota(jnp.int32, (n, 1), 0) == (n - 1)
    qs = [q[r] * (dk ** -0.5) for r in rows]
    ks = [k[r] for r in rows]
    gcs, betas, d_incl = [], [], []
    for r in rows:
        g = jnp.sum(gates[r] * pick_g, axis=1, keepdims=True)
        betas.append(jnp.sum(gates[r] * pick_b, axis=1, keepdims=True))
        g_row = jnp.sum(eye * g, axis=0, keepdims=True)
        gc = jnp.sum(jnp.where(incl, g_row, 0.0), axis=1, keepdims=True)
        gc_row = jnp.sum(eye * gc, axis=0, keepdims=True)
        gcs.append(gc)
        d_incl.append(jnp.where(incl, jnp.exp(jnp.where(incl, gc - gc_row, 0.0)), 0.0))
    kbs = [kk * b for kk, b in zip(ks, betas)]
    mats = tuple(_mm_nt(kb, kk) * jnp.where(ri > ci, d, 0.0) for kb, kk, d in zip(kbs, ks, d_incl))
    ts = _solve_unit_lower(mats) if t_saved is None else _solved_unit_lower(mats, tuple(t_saved[r] for r in rows))
    egs = [jnp.exp(gc) for gc in gcs]
    values = [_mm(t, v[r] * b) for t, r, b in zip(ts, rows, betas)]
    kcds = [_mm(t, kb * eg) for t, kb, eg in zip(ts, kbs, egs)]
    intras = [_mm_nt(qq, kk) * d for qq, kk, d in zip(qs, ks, d_incl)]
    g_lasts = [jnp.sum(jnp.where(last, gc, 0.0), axis=0, keepdims=True) for gc in gcs]
    qes = [qq * eg for qq, eg in zip(qs, egs)]
    kts = [kk * jnp.exp(gl - gc) for kk, gl, gc in zip(ks, g_lasts, gcs)]
    carries = [jnp.broadcast_to(jnp.exp(gl), (8, LANES)) for gl in g_lasts]
    cat = lambda parts: jnp.concatenate(parts, axis=0)
    res = (cat(values), cat(kcds), cat(qes), cat(kts), cat(intras), cat(carries))
    return res + (cat(ts),) if t_saved is None else res


def _gdn_steps(states, operands):
    v_new = [value - _bmm(kcd, s) for s, (value, kcd, _, _, _, _) in zip(states, operands)]
    outs = [_bmm(qe, s) + _bmm(intra, vn) for s, vn, (_, _, qe, _, intra, _) in zip(states, v_new, operands)]
    new = [s * carry + _bmm_tn(kt, vn) for s, vn, (_, _, _, kt, _, carry) in zip(states, v_new, operands)]
    return tuple(new), tuple(outs)


def _gdn_post(o, z, nrm):
    return _rms(o, nrm) * _silu(z)


def _gdn_wy_ins(q, k, v, gates, t_saved=None):
    t = q.shape[0]
    rb = min(GDN_CPS * GDN_L, t)
    hd = GDN_HEADS
    lane = jnp.arange(LANES)[None, None, :]
    pick_g = (lane == jnp.arange(hd)[:, None, None]).astype(F32)
    pick_b = (lane == jnp.arange(hd)[:, None, None] + hd).astype(F32)
    blk = lambda a: _In(a, (rb, LANES), lambda i: (i // hd, i % hd))
    par = lambda a: _In(a, (None, 1, LANES), lambda i: (i % hd, 0, 0))
    ins = [blk(q), blk(k), blk(v), _In(gates, (rb, LANES), lambda i: (i // hd, 0), shared=hd), par(pick_g),
           par(pick_b)]
    wide = lambda dt: ((t, GDN_W), dt, (rb, LANES), lambda i: (i // hd, i % hd))
    carry = ((hd, 8 * (t // GDN_L), LANES), F32, (None, 8 * (rb // GDN_L), LANES), lambda i: (i % hd, i // hd, 0))
    outs = [wide(F32), wide(BF16), wide(BF16), wide(BF16), wide(BF16), carry]
    if t_saved is None:
        outs.append(wide(F32))
    else:
        ins.append(blk(t_saved))
    return (t // rb) * hd, ins, outs


def _gdn_scan_specs(t):
    wide = GDN_HPS * LANES
    once = pl.Buffered(1)
    col = lambda base: pl.BlockSpec((t, wide), lambda h: (0, base // GDN_HPS + h), pipeline_mode=once)
    carry = pl.BlockSpec((GDN_HPS, 8 * (t // GDN_L), LANES), lambda h: (h, 0, 0))
    par = pl.BlockSpec((GDN_HPS, 1, LANES), lambda h: (h, 0, 0))
    return col, carry, par


def _head_operands(refs, cy_ref, c, i):
    rows = pl.ds(pl.multiple_of(c * GDN_L, GDN_L), GDN_L)
    lanes = pl.ds(i * LANES, LANES)
    return tuple(r[rows, lanes] for r in refs) + (cy_ref[i, pl.ds(pl.multiple_of(c * 8, 8), 1), :],)


def _gdn_scan_fwd(wy, proj, nrm):
    t = wy[0].shape[0]
    nc = t // GDN_L
    heads = range(GDN_HPS)

    def body(val_ref, kcd_ref, qe_ref, kt_ref, in_ref, cy_ref, z_ref, n_ref, y_ref, o_scr):
        big = (val_ref, kcd_ref, qe_ref, kt_ref, in_ref)

        def step(c, states):
            rows = pl.ds(pl.multiple_of(c * GDN_L, GDN_L), GDN_L)
            states, outs = _gdn_steps(states, [_head_operands(big, cy_ref, c, i) for i in heads])
            for i in heads:
                o_scr[rows, pl.ds(i * LANES, LANES)] = outs[i]
            return states

        lax.fori_loop(0, nc, step, tuple(jnp.zeros((LANES, LANES), F32) for _ in heads))
        for i in heads:
            lanes = pl.ds(i * LANES, LANES)
            y_ref[:, lanes] = _gdn_post(o_scr[:, lanes], z_ref[:, lanes], n_ref[i]).astype(y_ref.dtype)

    col, carry, par = _gdn_scan_specs(t)
    return pl.pallas_call(
        body, name="gdn_scan_fwd", grid=(GDN_HEADS // GDN_HPS,),
        in_specs=[col(0)] * 5 + [carry, col(CB_Z), par],
        out_specs=col(0), out_shape=jax.ShapeDtypeStruct((t, GDN_W), BF16),
        scratch_shapes=[pltpu.VMEM((t, GDN_HPS * LANES), F32)],
        compiler_params=_cparams(("arbitrary",)),
    )(*wy, proj, nrm)


def _gdn_scan_bwd(wy, proj, nrm, dy, dy_base):
    t = wy[0].shape[0]
    nc = t // GDN_L
    heads = range(GDN_HPS)

    def body(val_ref, kcd_ref, qe_ref, kt_ref, in_ref, cy_ref, z_ref, n_ref, dy_ref,
             dval_ref, dkcd_ref, dqe_ref, dkt_ref, din_ref, dcy_ref, dz_ref, dn_ref, o_scr, s_scr):
        big = (val_ref, kcd_ref, qe_ref, kt_ref, in_ref)
        dbig = (dval_ref, dkcd_ref, dqe_ref, dkt_ref, din_ref)

        def step(c, states):
            rows = pl.ds(pl.multiple_of(c * GDN_L, GDN_L), GDN_L)
            for i in heads:
                s_scr[i, c] = states[i]
            states, outs = _gdn_steps(states, [_head_operands(big, cy_ref, c, i) for i in heads])
            for i in heads:
                o_scr[rows, pl.ds(i * LANES, LANES)] = outs[i]
            return states

        zeros = tuple(jnp.zeros((LANES, LANES), F32) for _ in heads)
        lax.fori_loop(0, nc, step, zeros)
        for i in heads:
            lanes = pl.ds(i * LANES, LANES)
            _, vjp_post = jax.vjp(_gdn_post, o_scr[:, lanes], z_ref[:, lanes], n_ref[i])
            do, dz, dn = vjp_post(dy_ref[:, lanes].astype(F32))
            dz_ref[:, lanes] = dz.astype(dz_ref.dtype)
            dn_ref[i] = dn
            o_scr[:, lanes] = do
        dcy_ref[...] = jnp.zeros_like(dcy_ref)

        def rstep(k, dstates):
            c = nc - 1 - k
            rows = pl.ds(pl.multiple_of(c * GDN_L, GDN_L), GDN_L)
            ops = [tuple(o.astype(F32) for o in _head_operands(big, cy_ref, c, i)) for i in heads]
            _, vjp_c = jax.vjp(_gdn_steps, tuple(s_scr[i, c] for i in heads), ops)
            dstates, dops = vjp_c((dstates, tuple(o_scr[rows, pl.ds(i * LANES, LANES)] for i in heads)))
            for i in heads:
                for r, g in zip(dbig, dops[i][:5]):
                    r[rows, pl.ds(i * LANES, LANES)] = g.astype(r.dtype)
                dcy_ref[i, pl.ds(pl.multiple_of(c * 8, 8), 1), :] = dops[i][5]
            return dstates

        lax.fori_loop(0, nc, rstep, zeros)

    col, carry, par = _gdn_scan_specs(t)
    wide = jax.ShapeDtypeStruct((t, GDN_W), BF16)
    return pl.pallas_call(
        body, name="gdn_scan_bwd", grid=(GDN_HEADS // GDN_HPS,),
        in_specs=[col(0)] * 5 + [carry, col(CB_Z), par, col(dy_base)],
        out_specs=[col(0)] * 5 + [carry, col(0), par],
        out_shape=[wide] * 5 + [jax.ShapeDtypeStruct(wy[5].shape, F32), wide,
                                jax.ShapeDtypeStruct((GDN_HEADS, 1, LANES), F32)],
        scratch_shapes=[pltpu.VMEM((t, GDN_HPS * LANES), F32), pltpu.VMEM((GDN_HPS, nc, LANES, LANES), F32)],
        compiler_params=_cparams(("arbitrary",)),
    )(*wy, proj, nrm, dy)


TM = 512
TMW = 1024
TR = 256


def _rms_fwd(name, h, g):
    t, d = h.shape
    tm = min(TR, t)
    return _fused(name, lambda hb, gb: (_rms(hb, gb),), t // tm, [_row_in(h, tm), _row_shared(g)],
                  [_row_out(t, d, BF16, tm)])[0]


def _rms_bwd(name, h, g, dxn, dh_next):
    t, d = h.shape
    tm = min(TR, t)
    dh, dg = _fused(name, lambda hb, gb: (_rms(hb, gb),), t // tm, [_row_in(h, tm), _row_shared(g)],
                    [_row_out(t, d, F32, tm)], douts=[dxn], need=[True, True], add={0: dh_next})
    return dh, dg


def _mixer_ins(proj, lw):
    sgu = [_col_in(proj, CB_U, SG_HEADS), _col_in(proj, CB_V, SG_HEADS), _col_par(lw['sg_ln_g']),
           _col_par(lw['sg_ln_b']), _col_par(lw['sg_w']), _col_par(lw['sg_bb']), _col_par(lw['out_norm_a'])]
    sconv = [_col_in(proj, CB_GB, SC_GROUPS), _col_in(proj, CB_GC, SC_GROUPS), _col_in(proj, CB_X, SC_GROUPS)] + \
            [_col_par(w) for w in lw['sc_taps']] + [_col_par(lw['out_norm_b'])]
    gq = [_col_in(proj, CB_Q, GDN_HEADS)] + [_col_par(w) for w in lw['q_taps']]
    gk = [_col_in(proj, CB_K, GDN_HEADS)] + [_col_par(w) for w in lw['k_taps']]
    gv = [_col_in(proj, CB_VV, GDN_HEADS)] + [_col_par(w) for w in lw['v_taps']]
    return sgu, sconv, gq, gk, gv


def _gates_ins(proj, lw, tm):
    return [_row_in(proj, tm, CB_AB), _row_shared(lw['a_log_row']), _row_shared(lw['dt_bias_row'])]


def _layer_fwd(h, p_l, lw, late=None):
    t, d = h.shape
    xn = _rms_fwd("rms_fwd", h, lw['norm_mix'])
    proj = _matmul("proj_fwd", xn, lw['w_in'], "nn", t, PROJ_W, d, TMW, 1024, d)
    sgu, sconv, gq, gk, gv = _mixer_ins(proj, lw)
    ya = _fused("sgu_fwd", _fn_sgu, SG_HEADS, sgu, [_col_out(t, SG_W, BF16)])[0]
    yb = _fused("sconv_fwd", _fn_sconv, SC_GROUPS, sconv, [_col_out(t, SC_W, BF16)])[0]
    q = _fused("gdn_q_fwd", _fn_gdn_qk, GDN_HEADS, gq, [_col_out(t, GDN_W, F32)])[0]
    k = _fused("gdn_k_fwd", _fn_gdn_qk, GDN_HEADS, gk, [_col_out(t, GDN_W, F32)])[0]
    v = _fused("gdn_v_fwd", _fn_gdn_v, GDN_HEADS, gv, [_col_out(t, GDN_W, F32)])[0]
    tm = min(TR, t)
    gates = _fused("gdn_gates_fwd", _fn_gdn_gates, t // tm, _gates_ins(proj, lw, tm),
                   [_row_out(t, LANES, F32, tm)])[0]
    n_wy, wy_ins, wy_outs = _gdn_wy_ins(q, k, v, gates)
    wy = _fused("gdn_wy_fwd", _fn_gdn_wy, n_wy, wy_ins, wy_outs)
    wy, wy_t = wy[:6], wy[6]
    yc = _gdn_scan_fwd(wy, proj, lw['gdn_norm'])
    ycat = jnp.concatenate([ya, yb, yc], axis=1)
    if late is not None:
        late(ycat)
    dff = lw['w_ff2'].shape[0]
    h2 = _matmul("wo_fwd", ycat, lw['w_o'], "nn", t, d, d, TMW, 1024, d,
                 epilogue=lambda acc, hb: (hb + acc,), extras=(h,))
    hn = _rms_fwd("rms_fwd", h2, lw['norm_ffn'])
    per = d // 1024
    s, r = _matmul("ff1_fwd", hn, lw['w_ff1'], "nn", t, dff, d, TMW, 1024, d,
                   b_spec=pl.BlockSpec((None, d, 1024), lambda i, j, kk: (j // per, 0, j % per)),
                   epilogue=lambda acc: (jnp.maximum(acc, 0.0), jnp.square(jnp.maximum(acc, 0.0))),
                   outs=[((t, dff), BF16, (min(TMW, t), 1024), lambda i, j, kk: (i, j))] * 2)
    h3 = _matmul("ff2_fwd", r, lw['w_ff2'], "nn", t, d, dff, TM, 1024, 4096,
                 epilogue=lambda acc, hb: (hb + acc,), extras=(h2,))
    hn2 = _rms_fwd("rms_fwd", h3, lw['norm_ple'])
    pp = _matmul("ple_proj_fwd", p_l, lw['w_ple_proj'], "nn", t, d, p_l.shape[1], TM, 1024, p_l.shape[1])

    def gate_epilogue(acc, hb, ppb):
        sg = _sigmoid(acc)
        return hb + ppb * sg, sg

    h4, gate = _matmul("ple_gate_fwd", hn2, lw['w_ple_gate'], "nn", t, d, d, TM, 1024, d, epilogue=gate_epilogue,
                       extras=(h3, pp), outs=[((t, d), F32, (min(TM, t), 1024), lambda i, j, kk: (i, j))] * 2)
    saved = dict(h=h, xn=xn, proj=proj, q=q, k=k, v=v, gates=gates, ycat=ycat, h2=h2, hn=hn, s=s, r=r, h3=h3,
                 hn2=hn2, pp=pp, gate=gate, p=p_l, wy=wy, wy_t=wy_t)
    return h4, saved


def _layer_bwd(dh4, sv, lw, early=None):
    t, d = dh4.shape
    dff = lw['w_ff2'].shape[0]
    tm = min(TR, t)
    g = {}
    dacc, dpp = _fused("ple_bwd_gate", lambda dh, pp, gt: (dh * pp * gt * (1.0 - gt), dh * gt), t // tm,
                       [_row_in(dh4, tm), _row_in(sv['pp'], tm), _row_in(sv['gate'], tm)],
                       [_row_out(t, d, BF16, tm)] * 2)
    g['w_ple_gate'] = _matmul("dw_ple_gate", sv['hn2'], dacc, "tn", d, d, t, TMW, 1024, t, out_dtype=BF16,
                              n_outer=True)
    g['w_ple_proj'] = _matmul("dw_ple_proj", sv['p'], dpp, "tn", sv['p'].shape[1], d, t, TM, 1024, t,
                              out_dtype=BF16, n_outer=True)
    dhn2 = _matmul("dx_ple_gate", dacc, lw['w_ple_gate'], "nt", t, d, d, TMW, 1024, d)
    dh3, g['norm_ple'] = _rms_bwd("rms_bwd", sv['h3'], lw['norm_ple'], dhn2, dh4)
    da = _matmul("dx_ff2", dh3, lw['w_ff2'], "nt", t, dff, d, TMW, 1024, d,
                 epilogue=lambda acc, sb: (acc * (2.0 * sb.astype(F32)),), extras=(sv['s'],),
                 outs=[((t, dff), BF16, (min(TMW, t), 1024), lambda i, j, kk: (i, j))])
    g['w_ff2'] = _matmul("dw_ff2", sv['r'], dh3, "tn", dff, d, t, TM, 512, t, out_dtype=BF16, n_outer=True)
    per = d // 1024
    dhn = _matmul("dx_ff1", da, lw['w_ff1'], "nt", t, d, dff, TMW, 1024, d,
                  b_spec=pl.BlockSpec((None, 1024, d), lambda i, j, kk: (kk, j, 0)))
    g['w_ff1'] = _matmul("dw_ff1", sv['hn'], da, "tn", d, dff, t, TMW, 1024, t, n_outer=True,
                         outs=[((N_SHARD, d, d), BF16, (None, TMW, 1024), lambda i, j, kk: (j // per, i, j % per))])
    norm_ffn = lw['norm_ffn'] if early is None else lw['norm_ffn'] + early(g)[0, 0]
    dh2, g['norm_ffn'] = _rms_bwd("rms_bwd", sv['h2'], norm_ffn, dhn, dh3)
    dycat = _matmul("dx_o", dh2, lw['w_o'], "nt", t, d, d, TMW, 1024, d)
    g['w_o'] = _matmul("dw_o", sv['ycat'], dh2, "tn", d, d, t, TM, 512, t, out_dtype=BF16, n_outer=True)
    proj = sv['proj']
    sgu, sconv, gq, gk, gv = _mixer_ins(proj, lw)
    bf2 = [BF16, BF16]
    r_ = _fused("sgu_bwd", _fn_sgu, SG_HEADS, sgu, [_col_out(t, d, F32, 0)], douts=[dycat], need=[True] * 7,
                gdt=bf2 + [F32] * 5)
    du, dv_, g['sg_ln_g'], g['sg_ln_b'], g['sg_w'], g['sg_bb'], g['out_norm_a'] = r_
    r_ = _fused("sconv_bwd", _fn_sconv, SC_GROUPS, sconv, [_col_out(t, d, F32, SG_HEADS)], douts=[dycat],
                need=[True] * 7, gdt=[BF16] * 3 + [F32] * 4)
    dgb, dgc, dxin = r_[:3]
    g['sc_taps'], g['out_norm_b'] = r_[3:6], r_[6]
    r_ = _gdn_scan_bwd(sv['wy'], proj, lw['gdn_norm'], dycat, SG_HEADS + SC_GROUPS)
    dwy, dz, g['gdn_norm'] = r_[:6], r_[6], r_[7]
    n_wy, wy_ins, wy_outs = _gdn_wy_ins(sv['q'], sv['k'], sv['v'], sv['gates'], sv['wy_t'])
    dq, dk, dvv, dgates = _fused("gdn_wy_bwd", _fn_gdn_wy, n_wy, wy_ins, wy_outs, douts=dwy,
                                 need=[True] * 4 + [False] * 3)[:4]
    one = [_col_out(t, GDN_W, F32)]
    r_ = _fused("gdn_q_bwd", _fn_gdn_qk, GDN_HEADS, gq, one, douts=[dq], need=[True] * 5, gdt=[BF16] + [F32] * 4)
    dpq, g['q_taps'] = r_[0], r_[1:]
    r_ = _fused("gdn_k_bwd", _fn_gdn_qk, GDN_HEADS, gk, one, douts=[dk], need=[True] * 5, gdt=[BF16] + [F32] * 4)
    dpk, g['k_taps'] = r_[0], r_[1:]
    r_ = _fused("gdn_v_bwd", _fn_gdn_v, GDN_HEADS, gv, one, douts=[dvv], need=[True] * 5, gdt=[BF16] + [F32] * 4)
    dpv, g['v_taps'] = r_[0], r_[1:]
    dab, g['a_log_row'], g['dt_bias_row'] = _fused(
        "gdn_gates_bwd", _fn_gdn_gates, t // tm, _gates_ins(proj, lw, tm), [_row_out(t, LANES, F32, tm)],
        douts=[dgates], need=[True] * 3, gdt=[BF16, F32, F32])
    pad = jnp.zeros((t, PROJ_W - (CB_AB + 1) * LANES), BF16)
    dproj = jnp.concatenate([du, dv_, dgb, dgc, dxin, dpq, dpk, dpv, dz, dab, pad], axis=1)
    dxn = _matmul("dx_in", dproj, lw['w_in'], "nt", t, d, PROJ_W, TMW, 1024, PROJ_W // 2)
    g['w_in'] = _matmul("dw_in", sv['xn'], dproj, "tn", d, PROJ_W, t, TMW, 1024, t, out_dtype=BF16, n_outer=True)
    dh, g['norm_mix'] = _rms_bwd("rms_bwd", sv['h'], lw['norm_mix'], dxn, dh2)
    return dh, g


def _loss_grad(h, g, tgt):
    t, d = h.shape
    tm = min(TR, t)

    def body(h_ref, g_ref, t_ref, loss_ref, dh_ref, dg_ref):
        y, vjp = jax.vjp(_rms, h_ref[...], g_ref[...])
        e = y - t_ref[...]
        dh, dg = vjp(e * (1.0 / d))

        @pl.when(pl.program_id(0) == 0)
        def _():
            loss_ref[...] = jnp.zeros_like(loss_ref)
            dg_ref[...] = jnp.zeros_like(dg_ref)

        loss_ref[...] += jnp.sum(jnp.sum(e * e, axis=1, keepdims=True), axis=0, keepdims=True) * (0.5 / d)
        dh_ref[...] = dh
        dg_ref[...] += dg

    row = pl.BlockSpec((tm, d), lambda i: (i, 0))
    return pl.pallas_call(
        body, name="loss_grad", grid=(t // tm,),
        in_specs=[row, pl.BlockSpec((1, d), lambda i: (0, 0)), row],
        out_specs=[pl.BlockSpec((1, LANES), lambda i: (0, 0)), row, pl.BlockSpec((1, d), lambda i: (0, 0))],
        out_shape=[jax.ShapeDtypeStruct((1, LANES), F32), jax.ShapeDtypeStruct((t, d), F32),
                   jax.ShapeDtypeStruct((1, d), F32)],
        compiler_params=_cparams(("arbitrary",)),
    )(h, g, tgt)


def _big_weights(full):
    lw = {}
    if 'w_in' in full:
        d = full['w_in'].shape[1]
        w_in = jnp.transpose(full['w_in'], (1, 0, 2)).reshape(d, IN_COLS)
        lw['w_in'] = jnp.pad(w_in, ((0, 0), (0, PROJ_W - IN_COLS)))
    if 'w_o' in full:
        lw['w_o'] = full['w_o'].reshape(-1, full['w_o'].shape[-1])
    if 'w_ff1' in full:
        lw['w_ff1'] = full['w_ff1']
    if 'w_ff2' in full:
        lw['w_ff2'] = full['w_ff2'].reshape(-1, full['w_ff2'].shape[-1])
    if 'w_ple_gate' in full:
        lw['w_ple_gate'] = full['w_ple_gate'].reshape(-1, full['w_ple_gate'].shape[-1])
    if 'w_ple_proj' in full:
        wpp = full['w_ple_proj']
        lw['w_ple_proj'] = jnp.transpose(wpp, (1, 0, 2)).reshape(wpp.shape[1], -1)
    return lw


def _layer_weights(l, full, convs, small):
    d = small['norm_mix'].shape[-1]
    lw = _big_weights(full)
    for n in ('norm_mix', 'norm_ffn', 'norm_ple'):
        lw[n] = small[n][l].reshape(1, d)
    lw['sg_ln_g'] = small['sg_ln_g'][l].reshape(SG_HEADS, 1, LANES)
    lw['sg_ln_b'] = small['sg_ln_b'][l].reshape(SG_HEADS, 1, LANES)
    lw['sg_w'] = small['sg_w'][l]
    lw['sg_bb'] = jnp.broadcast_to(small['sg_b'][l][:, :, None], (SG_HEADS, SG_CHUNK, LANES))
    lw['out_norm_a'] = small['out_norm_a'][l].reshape(SG_HEADS, 1, LANES)
    lw['out_norm_b'] = small['out_norm_b'][l].reshape(SC_GROUPS, 1, LANES)
    lw['gdn_norm'] = jnp.broadcast_to(small['gdn_norm'][l].reshape(1, 1, LANES), (GDN_HEADS, 1, LANES))
    lw['a_log_row'] = jnp.pad(small['gdn_a_log'][l].reshape(1, GDN_HEADS), ((0, 0), (0, LANES - GDN_HEADS)))
    lw['dt_bias_row'] = jnp.pad(small['gdn_dt_bias'][l].reshape(1, GDN_HEADS), ((0, 0), (0, LANES - GDN_HEADS)))
    sc = convs['sc_conv'][l]
    lw['sc_taps'] = [sc[:, j:j + 1, :] for j in range(SC_KERNEL)]
    gc = jnp.transpose(convs['gdn_conv'][l], (1, 0, 2)).reshape(GDN_CONV, 3 * GDN_W)
    for i, nm in enumerate(('q_taps', 'k_taps', 'v_taps')):
        part = gc[:, i * GDN_W:(i + 1) * GDN_W].reshape(GDN_CONV, GDN_HEADS, 1, LANES)
        lw[nm] = [part[j] for j in range(GDN_CONV)]
    return lw


def _shard_major(name, g):
    if name == 'w_in':
        d = g.shape[0]
        return jnp.transpose(g[:, :IN_COLS].reshape(d, N_SHARD, IN_COLS // N_SHARD), (1, 0, 2))
    if name == 'w_ple_proj':
        return jnp.transpose(g.reshape(g.shape[0], N_SHARD, g.shape[1] // N_SHARD), (1, 0, 2))
    if name == 'w_ff1':
        return g
    return g.reshape(N_SHARD, -1, g.shape[-1])


def _weight_grads(g):
    d = g['norm_mix'].shape[-1]
    out = {}
    for n in ('norm_mix', 'norm_ffn', 'norm_ple'):
        out[n] = g[n].reshape(d)
    out['sg_ln_g'] = g['sg_ln_g'].reshape(SG_W)
    out['sg_ln_b'] = g['sg_ln_b'].reshape(SG_W)
    out['sg_w'] = g['sg_w']
    out['sg_b'] = jnp.sum(g['sg_bb'], axis=2)
    out['out_norm_a'] = g['out_norm_a'].reshape(SG_W)
    out['out_norm_b'] = g['out_norm_b'].reshape(SC_W)
    out['gdn_norm'] = jnp.sum(g['gdn_norm'], axis=(0, 1))
    out['gdn_a_log'] = g['a_log_row'][0, :GDN_HEADS]
    out['gdn_dt_bias'] = g['dt_bias_row'][0, :GDN_HEADS]
    out['sc_conv'] = jnp.concatenate([w.reshape(1, SC_W) for w in g['sc_taps']], axis=0)
    taps = [jnp.concatenate([g[nm][j].reshape(1, GDN_W) for nm in ('q_taps', 'k_taps', 'v_taps')], axis=1)
            for j in range(GDN_CONV)]
    out['gdn_conv'] = jnp.concatenate(taps, axis=0)
    return out


ANY = pl.BlockSpec(memory_space=pl.ANY)


def _place():
    x, y, c = lax.axis_index("x"), lax.axis_index("y"), lax.axis_index("c")
    chips = [(1 - x, y), (x, 1 - y), (1 - x, 1 - y)]
    return x, y, c, chips


def _place_shard(wsh, shard, dtype, layer=None, after=None):
    dp, r, cc = wsh.shape
    tr = min(TR, r)
    nb = r // tr

    def body(idx_ref, w_ref, *rest):
        rest[-1][...] = w_ref[...].astype(rest[-1].dtype)

    if layer is None:
        grid, shape = (dp * nb,), (dp, N_SHARD, r, cc)
        in_spec = pl.BlockSpec((None, tr, cc), lambda i, ix: (i // nb, i % nb, 0))
        out_spec = pl.BlockSpec((None, None, tr, cc), lambda i, ix: (i // nb, ix[0], i % nb, 0))
    else:
        grid, shape = (nb,), (N_SHARD, r, cc)
        in_spec = pl.BlockSpec((None, tr, cc), lambda i, ix: (layer, i, 0))
        out_spec = pl.BlockSpec((None, tr, cc), lambda i, ix: (ix[0], i, 0))
    return pl.pallas_call(
        body, name="ag_place_shard",
        grid_spec=pltpu.PrefetchScalarGridSpec(num_scalar_prefetch=1, grid=grid,
                                               in_specs=[in_spec] + ([] if after is None else [ANY]),
                                               out_specs=out_spec),
        out_shape=jax.ShapeDtypeStruct(shape, dtype),
        compiler_params=_cparams(("arbitrary",)),
    )(shard, wsh, *([] if after is None else [after]))


def _rows(ref, slab, half):
    rh = ref.shape[1] // 2
    return ref.at[slab, pl.ds(pl.multiple_of(half * rh, rh), rh), :]


def _gather_direct(bufs):
    n = len(bufs)

    def body(*refs):
        outs = refs[n:2 * n]
        send_sems, recv_sems = refs[2 * n:]
        x, y, c, chips = _place()
        me = 2 * x + y

        def cp(w, l, j, shard, to):
            blk = outs[w].at[l, shard]
            return pltpu.make_async_remote_copy(src_ref=blk, dst_ref=blk, send_sem=send_sems.at[w, l, j],
                                                recv_sem=recv_sems.at[w, l, j], device_id=to, device_id_type=MESH)

        sends = [cp(w, l, j, me, (*chip, c)) for w in range(n) for l in range(DEPTH) for j, chip in enumerate(chips)]
        for s in sends:
            s.start()
        for w in range(n):
            for l in range(DEPTH):
                for j, (px, py) in enumerate(chips):
                    cp(w, l, j, 2 * px + py, (px, py, c)).wait_recv()
        for s in sends:
            s.wait_send()

    return pl.pallas_call(
        body, name="ag_small", in_specs=[ANY] * n, out_specs=[ANY] * n,
        out_shape=[jax.ShapeDtypeStruct(b.shape, b.dtype) for b in bufs],
        input_output_aliases={w: w for w in range(n)},
        scratch_shapes=[pltpu.SemaphoreType.DMA((n, DEPTH, 3)), pltpu.SemaphoreType.DMA((n, DEPTH, 3))],
    )(*bufs)


def _gather_halves(bufs):
    n = len(bufs)

    def body(*refs):
        outs = refs[n:2 * n]
        send_sems, recv_sems = refs[2 * n:]
        x, y, c, chips = _place()
        me = 2 * x + y
        sibling = (x, y, 1 - c)

        def cp(w, k, shard, half, to):
            blk = _rows(outs[w], shard, half)
            return pltpu.make_async_remote_copy(src_ref=blk, dst_ref=blk, send_sem=send_sems.at[w, k],
                                                recv_sem=recv_sems.at[w, k], device_id=to, device_id_type=MESH)

        sends = []
        for w in range(n):
            for j, chip in enumerate(chips):
                s = cp(w, j, me, c, (*chip, c))
                s.start()
                sends.append(s)
        for w in range(n):
            for j, (px, py) in enumerate(chips):
                cp(w, j, 2 * px + py, c, sibling).wait_recv()
                s = cp(w, 3 + j, 2 * px + py, c, sibling)
                s.start()
                sends.append(s)
        for w in range(n):
            for j, (px, py) in enumerate(chips):
                cp(w, 3 + j, 2 * px + py, 1 - c, sibling).wait_recv()
        for s in sends:
            s.wait_send()

    return pl.pallas_call(
        body, name="ag_gather_halves",
        in_specs=[ANY] * n, out_specs=[ANY] * n,
        out_shape=[jax.ShapeDtypeStruct(b.shape, b.dtype) for b in bufs],
        input_output_aliases={w: w for w in range(n)},
        scratch_shapes=[pltpu.SemaphoreType.DMA((n, 6)), pltpu.SemaphoreType.DMA((n, 6))],
    )(*bufs)


HBM = pl.BlockSpec(memory_space=pltpu.HBM)
SEM = pl.BlockSpec(memory_space=pltpu.SEMAPHORE)
_SPLIT = pltpu.CompilerParams(has_side_effects=pltpu.SideEffectType.DATAFLOW_SIDE_EFFECTING)


def _in_hbm(arrs):
    return [pltpu.with_memory_space_constraint(a, pltpu.HBM) for a in arrs]


def _gather_start(tag, bufs, after):
    n, na = len(bufs), len(after)
    k = 3 * n

    def body(*refs):
        b_refs, sems, token = refs[:n], refs[n + na:n + na + 2 * k], refs[-1]
        x, y, c, chips = _place()
        me = 2 * x + y
        for w in range(n):
            for j, chip in enumerate(chips):
                blk = _rows(b_refs[w], me, c)
                pltpu.make_async_remote_copy(src_ref=blk, dst_ref=blk, send_sem=sems[3 * w + j],
                                             recv_sem=sems[k + 3 * w + j], device_id=(*chip, c),
                                             device_id_type=MESH).start()
        token[...] = jnp.zeros_like(token)

    res = pl.pallas_call(
        body, name="ag_start_" + tag,
        out_shape=(*[pltpu.SemaphoreType.DMA(())] * (2 * k), *[pltpu.HBM(b.shape, b.dtype) for b in bufs],
                   jax.ShapeDtypeStruct((8, LANES), F32)),
        in_specs=[HBM] * n + [ANY] * na,
        out_specs=(*[SEM] * (2 * k), *[HBM] * n, pl.BlockSpec(memory_space=pltpu.VMEM)),
        input_output_aliases={w: 2 * k + w for w in range(n)}, compiler_params=_SPLIT,
    )(*_in_hbm(bufs), *after)
    return list(res[:2 * k]), list(res[2 * k:2 * k + n]), res[2 * k + n]


def _gather_wait(tag, sems, bufs, after):
    n = len(bufs)
    k = 3 * n

    def body(*refs):
        b_refs, sems = refs[:n], refs[n:n + 2 * k]
        x, y, c, chips = _place()
        me = 2 * x + y
        for w in range(n):
            for j, (px, py) in enumerate(chips):
                mine, theirs = _rows(b_refs[w], me, c), _rows(b_refs[w], 2 * px + py, c)
                cp = pltpu.make_async_remote_copy(src_ref=mine, dst_ref=theirs, send_sem=sems[3 * w + j],
                                                  recv_sem=sems[k + 3 * w + j], device_id=(px, py, c),
                                                  device_id_type=MESH)
                cp.wait_send()
                cp.wait_recv()

    res = pl.pallas_call(
        body, name="ag_wait_" + tag, out_shape=tuple(pltpu.HBM(b.shape, b.dtype) for b in bufs),
        in_specs=[HBM] * n + [SEM] * (2 * k) + [ANY], out_specs=(HBM,) * n,
        input_output_aliases={w: w for w in range(n)}, compiler_params=_SPLIT,
    )(*bufs, *sems, after)
    return list(res)


def _gather_forward(bufs):
    n = len(bufs)

    def body(*refs):
        outs = refs[n:2 * n]
        send_sems, recv_sems = refs[2 * n:]
        x, y, c, chips = _place()

        def cp(w, j, shard, half):
            blk = _rows(outs[w], shard, half)
            return pltpu.make_async_remote_copy(src_ref=blk, dst_ref=blk, send_sem=send_sems.at[w, j],
                                                recv_sem=recv_sems.at[w, j], device_id=(x, y, 1 - c),
                                                device_id_type=MESH)

        sends = [cp(w, j, 2 * px + py, c) for w in range(n) for j, (px, py) in enumerate(chips)]
        for s in sends:
            s.start()
        for w in range(n):
            for j, (px, py) in enumerate(chips):
                cp(w, j, 2 * px + py, 1 - c).wait_recv()
        for s in sends:
            s.wait_send()

    return pl.pallas_call(
        body, name="ag_forward", in_specs=[ANY] * n, out_specs=[ANY] * n,
        out_shape=[jax.ShapeDtypeStruct(b.shape, b.dtype) for b in bufs],
        input_output_aliases={w: w for w in range(n)},
        scratch_shapes=[pltpu.SemaphoreType.DMA((n, 3)), pltpu.SemaphoreType.DMA((n, 3))],
    )(*bufs)


def _scatter_start(tag, parts, after):
    n, na = len(parts), len(after)
    k = 3 * n
    lands = [lax.empty((3,) + g.shape[1:], g.dtype) for g in parts]

    def body(*refs):
        srcs, dsts, sems, token = refs[:n], refs[n:2 * n], refs[2 * n + na:2 * n + na + 2 * k], refs[-1]
        x, y, c, chips = _place()
        for w in range(n):
            for j, (px, py) in enumerate(chips):
                pltpu.make_async_remote_copy(src_ref=srcs[w].at[2 * px + py], dst_ref=dsts[w].at[j],
                                             send_sem=sems[3 * w + j], recv_sem=sems[k + 3 * w + j],
                                             device_id=(px, py, c), device_id_type=MESH).start()
        token[...] = jnp.zeros_like(token)

    res = pl.pallas_call(
        body, name="rs_scatter_start_" + tag,
        out_shape=(*[pltpu.SemaphoreType.DMA(())] * (2 * k), *[pltpu.HBM(a.shape, a.dtype) for a in parts + lands],
                   jax.ShapeDtypeStruct((8, LANES), F32)),
        in_specs=[HBM] * (2 * n) + [ANY] * na,
        out_specs=(*[SEM] * (2 * k), *[HBM] * (2 * n), pl.BlockSpec(memory_space=pltpu.VMEM)),
        input_output_aliases={w: 2 * k + w for w in range(2 * n)}, compiler_params=_SPLIT,
    )(*_in_hbm(parts + lands), *after)
    return list(res[:2 * k]), list(res[2 * k:2 * k + n]), list(res[2 * k + n:2 * k + 2 * n]), res[2 * k + 2 * n]


def _scatter_wait(tag, sems, parts, lands, after):
    n = len(parts)
    k = 3 * n

    def body(*refs):
        srcs, dsts, sems = refs[:n], refs[n:2 * n], refs[2 * n:2 * n + 2 * k]
        x, y, c, chips = _place()
        for w in range(n):
            for j, (px, py) in enumerate(chips):
                cp = pltpu.make_async_remote_copy(src_ref=srcs[w].at[2 * px + py], dst_ref=dsts[w].at[j],
                                                  send_sem=sems[3 * w + j], recv_sem=sems[k + 3 * w + j],
                                                  device_id=(px, py, c), device_id_type=MESH)
                cp.wait_send()
                cp.wait_recv()

    res = pl.pallas_call(
        body, name="rs_scatter_wait_" + tag, out_shape=tuple(pltpu.HBM(a.shape, a.dtype) for a in parts + lands),
        in_specs=[HBM] * (2 * n) + [SEM] * (2 * k) + [ANY], out_specs=(HBM,) * (2 * n),
        input_output_aliases={w: w for w in range(2 * n)}, compiler_params=_SPLIT,
    )(*parts, *lands, *sems, after)
    return list(res[:n]), list(res[n:])


def _swap_halves(grads):
    n = len(grads)

    def body(*refs):
        srcs, outs = refs[:n], refs[n:2 * n]
        send_sems, recv_sems = refs[2 * n:]
        x, y, c, _ = _place()
        cps = []
        for w in range(n):
            rh = srcs[w].shape[1] // 2
            theirs = srcs[w].at[:, pl.ds(pl.multiple_of((1 - c) * rh, rh), rh), :]
            cps.append(pltpu.make_async_remote_copy(src_ref=theirs, dst_ref=outs[w], send_sem=send_sems.at[w],
                                                    recv_sem=recv_sems.at[w], device_id=(x, y, 1 - c),
                                                    device_id_type=MESH))
        for cpy in cps:
            cpy.start()
        for cpy in cps:
            cpy.wait()

    return pl.pallas_call(
        body, name="rs_swap_halves", in_specs=[ANY] * n, out_specs=[ANY] * n,
        out_shape=[jax.ShapeDtypeStruct((g.shape[0], g.shape[1] // 2, g.shape[2]), g.dtype) for g in grads],
        scratch_shapes=[pltpu.SemaphoreType.DMA((n,)), pltpu.SemaphoreType.DMA((n,))],
    )(*grads)


def _scatter_shards(parts):
    n = len(parts)

    def body(*refs):
        srcs, outs = refs[:n], refs[n:2 * n]
        send_sems, recv_sems = refs[2 * n:]
        x, y, c, chips = _place()

        def cp(w, j, src_shard, to):
            return pltpu.make_async_remote_copy(
                src_ref=srcs[w].at[src_shard], dst_ref=outs[w].at[j], send_sem=send_sems.at[w, j],
                recv_sem=recv_sems.at[w, j], device_id=to, device_id_type=MESH)

        sends = [cp(w, j, 2 * px + py, (px, py, c)) for w in range(n) for j, (px, py) in enumerate(chips)]
        for s in sends:
            s.start()
        for s in sends:
            s.wait()

    return pl.pallas_call(
        body, name="rs_scatter", in_specs=[ANY] * n, out_specs=[ANY] * n,
        out_shape=[jax.ShapeDtypeStruct((3,) + g.shape[1:], g.dtype) for g in parts],
        scratch_shapes=[pltpu.SemaphoreType.DMA((n, 3)), pltpu.SemaphoreType.DMA((n, 3))],
    )(*parts)


def _sum_reduced(part, slots, shard, core, layer, prev):
    _, rh, cc = part.shape
    tr = min(TR, rh)
    nbh = rh // tr

    def body(shard_ref, core_ref, p_ref, s0_ref, s1_ref, s2_ref, *rest):
        acc = p_ref[...].astype(F32)
        for s_ref in (s0_ref, s1_ref, s2_ref):
            acc = acc + s_ref[...].astype(F32)
        rest[-1][...] = acc

    slot = lambda j: pl.BlockSpec((None, tr, cc), lambda i, sh, co: (j, i, 0))
    ins = [shard, core, part, slots, slots, slots] + ([] if prev is None else [prev])
    return pl.pallas_call(
        body, name="rs_sum_reduced",
        grid_spec=pltpu.PrefetchScalarGridSpec(
            num_scalar_prefetch=2, grid=(nbh,),
            in_specs=[pl.BlockSpec((None, tr, cc), lambda i, sh, co: (sh[0], i, 0)), slot(0), slot(1), slot(2)]
            + ([] if prev is None else [ANY]),
            out_specs=pl.BlockSpec((None, tr, cc), lambda i, sh, co: (layer, co[0] * nbh + i, 0))),
        out_shape=jax.ShapeDtypeStruct((DEPTH, 2 * rh, cc), F32),
        input_output_aliases={} if prev is None else {6: 0},
        compiler_params=_cparams(("arbitrary",)),
    )(*ins)


def _join_halves(bufs, layer):
    n = len(bufs)

    def body(*refs):
        outs = refs[n:2 * n]
        send_sems, recv_sems = refs[2 * n:]
        x, y, c, _ = _place()

        def cp(w, half):
            blk = _rows(outs[w], layer, half)
            return pltpu.make_async_remote_copy(src_ref=blk, dst_ref=blk, send_sem=send_sems.at[w],
                                                recv_sem=recv_sems.at[w], device_id=(x, y, 1 - c), device_id_type=MESH)

        sends = [cp(w, c) for w in range(n)]
        for s in sends:
            s.start()
        for w in range(n):
            cp(w, 1 - c).wait_recv()
        for s in sends:
            s.wait_send()

    return pl.pallas_call(
        body, name="rs_join_halves", in_specs=[ANY] * n, out_specs=[ANY] * n,
        out_shape=[jax.ShapeDtypeStruct(b.shape, b.dtype) for b in bufs],
        input_output_aliases={w: w for w in range(n)},
        scratch_shapes=[pltpu.SemaphoreType.DMA((n,)), pltpu.SemaphoreType.DMA((n,))],
    )(*bufs)


def _exchange_small(vec):
    def body(src, out, send_sems, recv_sems, local_sem):
        x, y, c, _ = _place()
        me = 4 * x + 2 * y + c
        lc = pltpu.make_async_copy(src, out.at[me], local_sem)
        lc.start()
        sends = []
        for k in range(1, N_DEV):
            fx, fy, fc = (k >> 2) & 1, (k >> 1) & 1, k & 1
            to = (x ^ fx, y ^ fy, c ^ fc)
            s = pltpu.make_async_remote_copy(src_ref=src, dst_ref=out.at[me], send_sem=send_sems.at[k - 1],
                                             recv_sem=recv_sems.at[k - 1], device_id=to, device_id_type=MESH)
            s.start()
            sends.append(s)
        for k in range(1, N_DEV):
            fx, fy, fc = (k >> 2) & 1, (k >> 1) & 1, k & 1
            frm = 4 * (x ^ fx) + 2 * (y ^ fy) + (c ^ fc)
            pltpu.make_async_remote_copy(src_ref=src, dst_ref=out.at[frm], send_sem=send_sems.at[k - 1],
                                         recv_sem=recv_sems.at[k - 1], device_id=(x ^ fx, y ^ fy, c ^ fc),
                                         device_id_type=MESH).wait_recv()
        for s in sends:
            s.wait_send()
        lc.wait()

    return pl.pallas_call(
        body, name="small_exchange", in_specs=[ANY], out_specs=ANY,
        out_shape=jax.ShapeDtypeStruct((N_DEV,) + vec.shape, vec.dtype),
        scratch_shapes=[pltpu.SemaphoreType.DMA((N_DEV - 1,)), pltpu.SemaphoreType.DMA((N_DEV - 1,)),
                        pltpu.SemaphoreType.DMA],
    )(vec)


def _add_own_half(g, other, c_arr):
    ns, rh, cc = other.shape
    tr = min(TR, rh)
    nb = rh // tr

    def body(c_ref, g_ref, o_ref, out_ref):
        out_ref[...] = (g_ref[...].astype(F32) + o_ref[...].astype(F32)).astype(out_ref.dtype)

    slab = pl.BlockSpec((None, tr, cc), lambda i, cr: (i // nb, i % nb, 0))
    return pl.pallas_call(
        body, name="rs_add_own_half",
        grid_spec=pltpu.PrefetchScalarGridSpec(
            num_scalar_prefetch=1, grid=(ns * nb,),
            in_specs=[pl.BlockSpec((None, tr, cc), lambda i, cr: (i // nb, cr[0] * nb + i % nb, 0)), slab],
            out_specs=slab),
        out_shape=jax.ShapeDtypeStruct(other.shape, other.dtype),
        compiler_params=_cparams(("arbitrary",)),
    )(c_arr, g, other)


def _sum_slots(name, a):
    ns, r, cc = a.shape
    tr = min(TR, r)
    ins = [_In(a, (None, tr, cc), functools.partial(lambda s, i: (s, i, 0), s)) for s in range(ns)]

    def fn(*blocks):
        acc = blocks[0].astype(F32)
        for b in blocks[1:]:
            acc = acc + b.astype(F32)
        return (acc,)

    return _fused(name, fn, r // tr, ins, [((r, cc), F32, (tr, cc), lambda i: (i, 0))])[0]


def _adamw_fn(w, g, m, v):
    m = ADAM_B1 * m + (1.0 - ADAM_B1) * g
    v = ADAM_B2 * v + (1.0 - ADAM_B2) * jnp.square(g)
    m_hat = m / (1.0 - ADAM_B1 ** ADAM_STEP)
    v_hat = v / (1.0 - ADAM_B2 ** ADAM_STEP)
    delta = -ADAM_LR * (m_hat / (jnp.sqrt(v_hat) + ADAM_EPS) + ADAM_WD * w)
    return delta, m, v


def _adamw(w, g, m, v):
    r, cc = w.shape
    tr = min(TR, r)
    ins = [_In(a, (tr, cc), lambda i: (i, 0)) for a in (w, g, m, v)]
    return _fused("adamw", _adamw_fn, r // tr, ins, [((r, cc), F32, (tr, cc), lambda i: (i, 0))] * 3)


def _pack(arrs):
    flat = jnp.concatenate([a.reshape(-1) for a in arrs])
    tile = TR * LANES
    n = -(-flat.shape[0] // tile) * tile
    return jnp.pad(flat, (0, n - flat.shape[0])).reshape(-1, LANES)


def _unpack(vec, shapes):
    flat = vec.reshape(-1)
    out, o = [], 0
    for s in shapes:
        n = math.prod(s)
        out.append(flat[o:o + n].reshape(s))
        o += n
    return out


def kernel(x, p, norm_mix, w_in, sg_ln_g, sg_ln_b, sg_w, sg_b, sc_conv, gdn_conv, gdn_a_log, gdn_dt_bias, gdn_norm, out_norm_a, out_norm_b, w_o, norm_ffn, w_ff1, w_ff2, norm_ple, w_ple_gate, w_ple_proj, norm_final, loss_target, m_norm_mix, m_w_in, m_sg_ln_g, m_sg_ln_b, m_sg_w, m_sg_b, m_sc_conv, m_gdn_conv, m_gdn_a_log, m_gdn_dt_bias, m_gdn_norm, m_out_norm_a, m_out_norm_b, m_w_o, m_norm_ffn, m_w_ff1, m_w_ff2, m_norm_ple, m_w_ple_gate, m_w_ple_proj, m_norm_final, v_norm_mix, v_w_in, v_sg_ln_g, v_sg_ln_b, v_sg_w, v_sg_b, v_sc_conv, v_gdn_conv, v_gdn_a_log, v_gdn_dt_bias, v_gdn_norm, v_out_norm_a, v_out_norm_b, v_w_o, v_norm_ffn, v_w_ff1, v_w_ff2, v_norm_ple, v_w_ple_gate, v_w_ple_proj, v_norm_final):
    given = dict(locals())
    w = {n: given[n] for n in WEIGHTS}
    m = {n: given['m_' + n] for n in WEIGHTS}
    v = {n: given['v_' + n] for n in WEIGHTS}
    shard = 2 * lax.axis_index("x") + lax.axis_index("y")
    core = lax.axis_index("c")

    shard_arr = shard.reshape(1).astype(jnp.int32)
    c_arr = core.reshape(1).astype(jnp.int32)
    convs = dict(zip(CONVS, _gather_direct([_place_shard(w[n], shard_arr, F32) for n in CONVS])))
    small = {n: w[n] for n in SMALL if n not in CONVS}
    late_w = [n for n in BIG if n != 'w_in']
    flyw = _gather_start("w", [_place_shard(w['w_in'], shard_arr, BF16, 0)], list(convs.values()))
    placed0 = [_place_shard(w[n], shard_arr, BF16, 0, flyw[2]) for n in late_w]
    placed1 = [_place_shard(w[n], shard_arr, BF16, 1, flyw[2]) for n in BIG]
    fly0 = _gather_start("l0", placed0, [flyw[2]])
    fly1 = _gather_start("l1", placed1, [fly0[2]])
    first = _gather_forward(_gather_wait("w", flyw[0], flyw[1], fly1[2]))

    lws = [_layer_weights(0, {'w_in': first[0]}, convs, small), None]
    lws[0]['norm_mix'] = lws[0]['norm_mix'] + fly1[2][0, 0]

    def rest_of_layer0(behind):
        got = _gather_forward(_gather_wait("l0", fly0[0], fly0[1], behind))
        lws[0].update(_big_weights(dict(zip(late_w, got))))

    h, sv0 = _layer_fwd(x[0], p[0, 0], lws[0], rest_of_layer0)
    got = _gather_forward(_gather_wait("l1", fly1[0], fly1[1], h))
    lws[1] = _layer_weights(1, dict(zip(BIG, got)), convs, small)
    h, sv1 = _layer_fwd(h, p[1, 0], lws[1])
    saved = [sv0, sv1]
    loss, dh, dnf = _loss_grad(h, small['norm_final'].reshape(1, -1), loss_target[0])

    def swap_add(names, g):
        big = [_shard_major(n, g[n]) for n in names]
        return [_add_own_half(g_, o, c_arr) for g_, o in zip(big, _swap_halves(big))]

    early_g = ['w_ff1', 'w_ff2', 'w_ple_gate', 'w_ple_proj']
    late_g = [n for n in BIG if n not in early_g]
    per_layer = [None] * DEPTH
    dh, g = _layer_bwd(dh, saved[1], lws[1])
    per_layer[1] = _weight_grads(g)
    fly1 = _scatter_start("l1", swap_add(BIG, g), [dh])
    lws[0]['norm_ple'] = lws[0]['norm_ple'] + fly1[3][0, 0]
    fly0 = []

    def early_grads(g0):
        fly0.extend(_scatter_start("l0", swap_add(early_g, g0), []))
        return fly0[3]

    dh, g = _layer_bwd(dh, saved[0], lws[0], early_grads)
    per_layer[0] = _weight_grads(g)
    flyl = _scatter_start("l0b", swap_add(late_g, g), [])
    groups = [(1, BIG, *_scatter_wait("l1", fly1[0], fly1[1], fly1[2], flyl[3])),
              (0, early_g, *_scatter_wait("l0", fly0[0], fly0[1], fly0[2], flyl[3]))]
    g_big = {}

    def finish(groups):
        for l, names, parts, slots in groups:
            sums = [_sum_reduced(pt, sl, shard_arr, c_arr, l, g_big.get(n)) for n, pt, sl in zip(names, parts, slots)]
            g_big.update(zip(names, _join_halves(sums, l)))

    delta, new_m, new_v, grad_w = {}, {}, {}, {}

    def update(names):
        for n in names:
            shp = w[n].shape
            two_d = lambda a: a.reshape(-1, shp[-1])
            d_, m_, v_ = _adamw(two_d(w[n]), two_d(g_big[n]), two_d(m[n]), two_d(v[n]))
            delta[n], new_m[n], new_v[n], grad_w[n] = d_.reshape(shp), m_.reshape(shp), v_.reshape(shp), g_big[n]

    finish(groups)
    update(early_g)
    finish([(0, late_g, *_scatter_wait("l0b", flyl[0], flyl[1], flyl[2], delta[early_g[0]]))])
    update(late_g)
    grad_x = dh
    grads = {n: jnp.stack([per_layer[l][n] for l in range(DEPTH)], axis=0) for n in SMALL if n != 'norm_final'}
    grads['norm_final'] = dnf.reshape(-1)

    rep = [n for n in SMALL if n not in CONVS]
    names = rep + CONVS
    vec = _pack([grads[n] for n in names] + [loss[0, :1]])
    total = _sum_slots("small_sum", _exchange_small(vec))
    parts_small = _unpack(total, [grads[n].shape for n in names] + [(1,)])
    g_small = dict(zip(names, parts_small[:-1]))
    loss_out = parts_small[-1].reshape(())
    for n in CONVS:
        width = w[n].shape[-1]
        g_small[n] = lax.dynamic_slice_in_dim(g_small[n], shard * width, width, axis=2)

    shapes = [w[n].shape for n in SMALL]
    d_, m_, v_ = _adamw(_pack([w[n] for n in SMALL]), _pack([g_small[n] for n in SMALL]),
                        _pack([m[n] for n in SMALL]), _pack([v[n] for n in SMALL]))
    for n, dd, mm, vv in zip(SMALL, _unpack(d_, shapes), _unpack(m_, shapes), _unpack(v_, shapes)):
        delta[n], new_m[n], new_v[n], grad_w[n] = dd, mm, vv, g_small[n]

    return (loss_out, grad_x[None], *[grad_w[n] for n in WEIGHTS], *[delta[n] for n in WEIGHTS],
            *[new_m[n] for n in WEIGHTS], *[new_v[n] for n in WEIGHTS])
```

```python
import functools
import math

import jax
import jax.numpy as jnp
from jax import lax
from jax.experimental import pallas as pl
from jax.experimental.pallas import tpu as pltpu

F32 = jnp.float32
BF16 = jnp.bfloat16
HI = lax.Precision.HIGH
MESH = pl.DeviceIdType.MESH

LANES = 128
EPS = 1e-6
SG_HEADS, SG_CHUNK = 4, 128
SC_GROUPS, SC_KERNEL = 4, 3
GDN_HEADS, GDN_CONV = 8, 4
GDN_L = 128
GDN_CPS = 8
GDN_HPS = 2
SG_W = SG_HEADS * LANES
SC_W = SC_GROUPS * LANES
GDN_W = GDN_HEADS * LANES
IN_COLS = 2 * SG_W + 3 * SC_W + 4 * GDN_W + 2 * GDN_HEADS
PROJ_W = 7168
CB_U, CB_V = 0, 4
CB_GB, CB_GC, CB_X = 8, 12, 16
CB_Q, CB_K, CB_VV, CB_Z, CB_AB = 20, 28, 36, 44, 52
N_SHARD = 4
N_DEV = 8
DEPTH = 2

ADAM_LR, ADAM_B1, ADAM_B2, ADAM_EPS, ADAM_WD, ADAM_STEP = 0.001, 0.9, 0.999, 1e-08, 0.01, 10

VMEM_LIMIT = 56 << 20

WEIGHTS = ['norm_mix', 'w_in', 'sg_ln_g', 'sg_ln_b', 'sg_w', 'sg_b', 'sc_conv', 'gdn_conv', 'gdn_a_log',
           'gdn_dt_bias', 'gdn_norm', 'out_norm_a', 'out_norm_b', 'w_o', 'norm_ffn', 'w_ff1', 'w_ff2', 'norm_ple',
           'w_ple_gate', 'w_ple_proj', 'norm_final']
BIG = ['w_in', 'w_o', 'w_ff1', 'w_ff2', 'w_ple_gate', 'w_ple_proj']
CONVS = ['sc_conv', 'gdn_conv']
SMALL = [n for n in WEIGHTS if n not in BIG]


def _cparams(sem=None):
    return pltpu.CompilerParams(dimension_semantics=sem, vmem_limit_bytes=VMEM_LIMIT)


def _dot(a, b, dims, prec=None):
    return lax.dot_general(a, b, (dims, ((), ())), precision=prec, preferred_element_type=F32)


def _mm(a, b):
    return _dot(a, b, ((1,), (0,)), HI)


def _mm_nt(a, b):
    return _dot(a, b, ((1,), (1,)), HI)


def _mm_tn(a, b):
    return _dot(a, b, ((0,), (0,)), HI)


@jax.custom_vjp
def _bmm(a, b):
    return _dot(a.astype(BF16), b.astype(BF16), ((1,), (0,)))


def _bmm_fwd(a, b):
    return _bmm(a, b), (a, b)


def _bmm_bwd(res, g):
    a, b = res
    gb = g.astype(BF16)
    return _dot(gb, b.astype(BF16), ((1,), (1,))), _dot(a.astype(BF16), gb, ((0,), (0,)))


_bmm.defvjp(_bmm_fwd, _bmm_bwd)


@jax.custom_vjp
def _bmm_tn(a, b):
    return _dot(a.astype(BF16), b.astype(BF16), ((0,), (0,)))


def _bmm_tn_fwd(a, b):
    return _bmm_tn(a, b), (a, b)


def _bmm_tn_bwd(res, g):
    a, b = res
    gb = g.astype(BF16)
    return _dot(b.astype(BF16), gb, ((1,), (1,))), _dot(a.astype(BF16), gb, ((1,), (0,)))


_bmm_tn.defvjp(_bmm_tn_fwd, _bmm_tn_bwd)


def _sigmoid(x):
    return 1.0 / (1.0 + jnp.exp(-x))


def _silu(x):
    return x * _sigmoid(x)


def _gelu(x):
    c = math.sqrt(2.0 / math.pi)
    return 0.5 * x * (1.0 + jnp.tanh(c * (x + 0.044715 * (x * x * x))))


def _softplus(x):
    return jnp.maximum(x, 0.0) + jnp.log(1.0 + jnp.exp(-jnp.abs(x)))


def _rms(x, g):
    return x * lax.rsqrt(jnp.mean(x * x, axis=-1, keepdims=True) + EPS) * g


def _roll_rows(x, shift):
    return pltpu.roll(x, shift % x.shape[0], 0)


@functools.partial(jax.custom_vjp, nondiff_argnums=(1,))
def _shift_down(x, j):
    row = lax.broadcasted_iota(jnp.int32, x.shape, 0)
    return jnp.where(row >= j, _roll_rows(x, j), 0.0)


def _shift_down_fwd(x, j):
    return _shift_down(x, j), None


def _shift_down_bwd(j, _, dy):
    row = lax.broadcasted_iota(jnp.int32, dy.shape, 0)
    return (jnp.where(row < dy.shape[0] - j, _roll_rows(dy, -j), 0.0),)


_shift_down.defvjp(_shift_down_fwd, _shift_down_bwd)


def _causal_conv(x, taps):
    k = len(taps)
    y = taps[k - 1] * x
    for j in range(k - 1):
        y = y + taps[j] * _shift_down(x, k - 1 - j)
    return y


class _In:
    def __init__(self, arr, block, imap, shared=False, gshape=None, gmap=None):
        self.arr, self.block, self.imap, self.shared = arr, block, imap, shared
        self.gshape = arr.shape if gshape is None else gshape
        self.gmap = imap if gmap is None else gmap


def _fused(name, fn, n, ins, outs, douts=None, need=None, gdt=None, add=None):
    n_in, n_out = len(ins), len(outs)
    in_specs = [pl.BlockSpec(s.block, s.imap) for s in ins]
    out_specs = [pl.BlockSpec(bs, im) for _, _, bs, im in outs]
    if douts is None:
        def body(*refs):
            res = fn(*[r[...] for r in refs[:n_in]])
            for r, v in zip(refs[n_in:], res):
                r[...] = v.astype(r.dtype)

        return pl.pallas_call(
            body, name=name, grid=(n,), in_specs=in_specs, out_specs=out_specs,
            out_shape=[jax.ShapeDtypeStruct(s, d) for s, d, _, _ in outs],
            compiler_params=_cparams(("arbitrary",)),
        )(*[s.arr for s in ins])

    gdt = list(gdt) if gdt is not None else [F32] * n_in
    add = dict(add or {})
    gidx = [i for i in range(n_in) if need[i]]
    aidx = [i for i in gidx if i in add]

    def body(*refs):
        in_refs, d_refs = refs[:n_in], refs[n_in:n_in + n_out]
        a_refs = dict(zip(aidx, refs[n_in + n_out:n_in + n_out + len(aidx)]))
        g_refs = refs[n_in + n_out + len(aidx):]
        vals = [r[...] for r in in_refs]

        def f(*dv):
            full = list(vals)
            for i, v in zip(gidx, dv):
                full[i] = v
            return tuple(o.astype(F32) for o in fn(*full))

        _, vjp = jax.vjp(f, *[vals[i].astype(F32) for i in gidx])
        grads = vjp(tuple(r[...].astype(F32) for r in d_refs))
        for i, g_ref, g in zip(gidx, g_refs, grads):
            if i in a_refs:
                g = g + a_refs[i][...].astype(F32)
            if ins[i].shared:
                period = n if ins[i].shared is True else ins[i].shared

                @pl.when(pl.program_id(0) % period == 0)
                def _():
                    g_ref[...] = jnp.zeros_like(g_ref)
                g_ref[...] += g.astype(g_ref.dtype)
            else:
                g_ref[...] = g.astype(g_ref.dtype)

    g_specs = [pl.BlockSpec(ins[i].block, ins[i].gmap) for i in gidx]
    g_shape = [jax.ShapeDtypeStruct(ins[i].gshape, gdt[i]) for i in gidx]
    res = pl.pallas_call(
        body, name=name, grid=(n,), in_specs=in_specs + out_specs + [g_specs[gidx.index(i)] for i in aidx],
        out_specs=g_specs, out_shape=g_shape,
        compiler_params=_cparams(("arbitrary",)),
    )(*[s.arr for s in ins], *douts, *[add[i] for i in aidx])
    full = [None] * n_in
    for i, g in zip(gidx, res):
        full[i] = g
    return full


def _row_in(a, tm, cb=None):
    if cb is None:
        return _In(a, (tm, a.shape[1]), lambda i: (i, 0))
    return _In(a, (tm, LANES), lambda i: (i, cb), gshape=(a.shape[0], LANES), gmap=lambda i: (i, 0))


def _row_shared(a):
    return _In(a, a.shape, lambda i: (0, 0), shared=True)


def _row_out(t, w, dt, tm):
    return ((t, w), dt, (tm, w), lambda i: (i, 0))


def _col_in(a, base, nblk):
    t = a.shape[0]
    return _In(a, (t, LANES), lambda i: (0, base + i), gshape=(t, nblk * LANES), gmap=lambda i: (0, i))


def _col_par(a):
    return _In(a, (None,) + a.shape[1:], lambda i: (i, 0, 0))


def _col_out(t, w, dt, base=0):
    return ((t, w), dt, (t, LANES), lambda i: (0, base + i))


def _matmul(name, a, b, mode, m, n, k, tm, tn, tk, b_spec=None, epilogue=None, extras=(), outs=None, out_dtype=F32,
            n_outer=False):
    tm, tn, tk = min(tm, m), min(tn, n), min(tk, k)
    assert m % tm == 0 and n % tn == 0 and k % tk == 0, (name, m, n, k, tm, tn, tk)
    nk = k // tk
    a_spec = (pl.BlockSpec((tk, tm), lambda i, j, kk: (kk, i)) if mode == "tn"
              else pl.BlockSpec((tm, tk), lambda i, j, kk: (i, kk)))
    if b_spec is None:
        b_spec = (pl.BlockSpec((tn, tk), lambda i, j, kk: (j, kk)) if mode == "nt"
                  else pl.BlockSpec((tk, tn), lambda i, j, kk: (kk, j)))
    dims = {"nn": ((1,), (0,)), "nt": ((1,), (1,)), "tn": ((0,), (0,))}[mode]
    if outs is None:
        outs = [((m, n), out_dtype, (tm, tn), lambda i, j, kk: (i, j))]
    if epilogue is None:
        epilogue = lambda acc: (acc,)
    n_ex = len(extras)

    def body(*refs):
        a_ref, b_ref = refs[0], refs[1]
        ex_refs = refs[2:2 + n_ex]
        o_refs = refs[2 + n_ex:2 + n_ex + len(outs)]
        part = _dot(a_ref[...].astype(BF16), b_ref[...].astype(BF16), dims)

        def finish(acc):
            for r, v in zip(o_refs, epilogue(acc, *[e[...] for e in ex_refs])):
                r[...] = v.astype(r.dtype)

        if nk == 1:
            finish(part)
        else:
            acc_ref = refs[-1]
            kk = pl.program_id(2)

            @pl.when(kk == 0)
            def _():
                acc_ref[...] = part

            @pl.when(kk > 0)
            def _():
                acc_ref[...] += part

            @pl.when(kk == nk - 1)
            def _():
                finish(acc_ref[...])

    ex_specs = [pl.BlockSpec((tm, tn), lambda i, j, kk: (i, j)) for _ in extras]
    out_specs = [pl.BlockSpec(bs, im) for _, _, bs, im in outs]
    in_specs = [a_spec, b_spec] + ex_specs
    grid = (m // tm, n // tn, nk)
    if n_outer:
        swap = lambda sp: pl.BlockSpec(sp.block_shape, functools.partial(lambda f, j, i, kk: f(i, j, kk), sp.index_map))
        in_specs, out_specs, grid = [swap(sp) for sp in in_specs], [swap(sp) for sp in out_specs], (n // tn, m // tm, nk)
    res = pl.pallas_call(
        body, name=name, grid=grid,
        in_specs=in_specs,
        out_specs=out_specs,
        out_shape=[jax.ShapeDtypeStruct(s, d) for s, d, _, _ in outs],
        scratch_shapes=[pltpu.VMEM((tm, tn), F32)] if nk > 1 else [],
        compiler_params=_cparams(("parallel", "parallel", "arbitrary")),
    )(a, b, *extras)
    return res if len(res) > 1 else res[0]


def _fn_sgu(u_pre, v_pre, ln_g, ln_b, w, bb, na):
    t = u_pre.shape[0]
    u = _gelu(u_pre)
    v = _gelu(v_pre)
    mu = jnp.mean(v, axis=-1, keepdims=True)
    vc = v - mu
    vh = vc * lax.rsqrt(jnp.mean(vc * vc, axis=-1, keepdims=True) + EPS) * ln_g + ln_b
    ri = lax.broadcasted_iota(jnp.int32, w.shape, 0)
    ci = lax.broadcasted_iota(jnp.int32, w.shape, 1)
    wc = jnp.where(ri >= ci, w, 0.0)
    f = jnp.concatenate([_bmm(wc, vh[c * SG_CHUNK:(c + 1) * SG_CHUNK]) + bb for c in range(t // SG_CHUNK)], axis=0)
    return (_rms(u * f, na),)


def _fn_sconv(gb, gc, xin, w0, w1, w2, nb):
    return (_rms(gb * _causal_conv(gc * xin, (w0, w1, w2)), nb),)


def _fn_gdn_qk(pre, w0, w1, w2, w3):
    a = _silu(_causal_conv(pre, (w0, w1, w2, w3)))
    return (a * lax.rsqrt(jnp.sum(a * a, axis=-1, keepdims=True) + EPS),)


def _fn_gdn_v(pre, w0, w1, w2, w3):
    return (_silu(_causal_conv(pre, (w0, w1, w2, w3))),)


def _fn_gdn_gates(ab, a_log, dt_bias):
    lane = lax.broadcasted_iota(jnp.int32, ab.shape, 1)
    g = -jnp.exp(a_log) * _softplus(ab + dt_bias)
    return (jnp.where(lane < GDN_HEADS, g, jnp.where(lane < 2 * GDN_HEADS, _sigmoid(ab), 0.0)),)


def _solve_unit_lower_impl(mats):
    n = mats[0].shape[0]
    ri = lax.broadcasted_iota(jnp.int32, (n, n), 0)
    ci = lax.broadcasted_iota(jnp.int32, (n, n), 1)
    ts = [(ri == ci).astype(F32)] * len(mats)
    sh = 0
    while (1 << sh) < n:
        rb = jnp.right_shift(ri, sh)
        cb = jnp.right_shift(ci, sh)
        off = ((rb & 1) == 1) & (cb == rb - 1)
        us = [_mm(t, jnp.where(off, a, 0.0)) for t, a in zip(ts, mats)]
        ts = [t - _mm(u, t) for t, u in zip(ts, us)]
        sh += 1
    return tuple(ts)


def _solve_cotangents(ts, dts):
    us = [_mm_nt(dt, t) for t, dt in zip(ts, dts)]
    return tuple(-_mm_tn(t, u) for t, u in zip(ts, us))


@jax.custom_vjp
def _solve_unit_lower(mats):
    return _solve_unit_lower_impl(mats)


def _solve_unit_lower_fwd(mats):
    ts = _solve_unit_lower_impl(mats)
    return ts, ts


def _solve_unit_lower_bwd(ts, dts):
    return (_solve_cotangents(ts, dts),)


_solve_unit_lower.defvjp(_solve_unit_lower_fwd, _solve_unit_lower_bwd)


@jax.custom_vjp
def _solved_unit_lower(mats, ts):
    return ts


def _solved_unit_lower_fwd(mats, ts):
    return ts, ts


def _solved_unit_lower_bwd(ts, dts):
    return _solve_cotangents(ts, dts), tuple(jnp.zeros_like(t) for t in ts)


_solved_unit_lower.defvjp(_solved_unit_lower_fwd, _solved_unit_lower_bwd)


def _fn_gdn_wy(q, k, v, gates, pick_g, pick_b, t_saved=None):
    n, dk = GDN_L, q.shape[1]
    rows = [slice(c * n, (c + 1) * n) for c in range(q.shape[0] // n)]
    ri = lax.broadcasted_iota(jnp.int32, (n, n), 0)
    ci = lax.broadcasted_iota(jnp.int32, (n, n), 1)
    incl = ri >= ci
    eye = (ri == ci).astype(F32)
    last = lax.broadcasted_iota(jnp.int32, (n, 1), 0) == (n - 1)
    qs = [q[r] * (dk ** -0.5) for r in rows]
    ks = [k[r] for r in rows]
    gcs, betas, d_incl = [], [], []
    for r in rows:
        g = jnp.sum(gates[r] * pick_g, axis=1, keepdims=True)
        betas.append(jnp.sum(gates[r] * pick_b, axis=1, keepdims=True))
        g_row = jnp.sum(eye * g, axis=0, keepdims=True)
        gc = jnp.sum(jnp.where(incl, g_row, 0.0), axis=1, keepdims=True)
        gc_row = jnp.sum(eye * gc, axis=0, keepdims=True)
        gcs.append(gc)
        d_incl.append(jnp.where(incl, jnp.exp(jnp.where(incl, gc - gc_row, 0.0)), 0.0))
    kbs = [kk * b for kk, b in zip(ks, betas)]
    mats = tuple(_mm_nt(kb, kk) * jnp.where(ri > ci, d, 0.0) for kb, kk, d in zip(kbs, ks, d_incl))
    ts = _solve_unit_lower(mats) if t_saved is None else _solved_unit_lower(mats, tuple(t_saved[r] for r in rows))
    egs = [jnp.exp(gc) for gc in gcs]
    values = [_mm(t, v[r] * b) for t, r, b in zip(ts, rows, betas)]
    kcds = [_mm(t, kb * eg) for t, kb, eg in zip(ts, kbs, egs)]
    intras = [_mm_nt(qq, kk) * d for qq, kk, d in zip(qs, ks, d_incl)]
    g_lasts = [jnp.sum(jnp.where(last, gc, 0.0), axis=0, keepdims=True) for gc in gcs]
    qes = [qq * eg for qq, eg in zip(qs, egs)]
    kts = [kk * jnp.exp(gl - gc) for kk, gl, gc in zip(ks, g_lasts, gcs)]
    carries = [jnp.broadcast_to(jnp.exp(gl), (8, LANES)) for gl in g_lasts]
    cat = lambda parts: jnp.concatenate(parts, axis=0)
    res = (cat(values), cat(kcds), cat(qes), cat(kts), cat(intras), cat(carries))
    return res + (cat(ts),) if t_saved is None else res


def _gdn_steps(states, operands):
    v_new = [value - _bmm(kcd, s) for s, (value, kcd, _, _, _, _) in zip(states, operands)]
    outs = [_bmm(qe, s) + _bmm(intra, vn) for s, vn, (_, _, qe, _, intra, _) in zip(states, v_new, operands)]
    new = [s * carry + _bmm_tn(kt, vn) for s, vn, (_, _, _, kt, _, carry) in zip(states, v_new, operands)]
    return tuple(new), tuple(outs)


def _gdn_post(o, z, nrm):
    return _rms(o, nrm) * _silu(z)


def _gdn_wy_ins(q, k, v, gates, t_saved=None):
    t = q.shape[0]
    rb = min(GDN_CPS * GDN_L, t)
    hd = GDN_HEADS
    lane = jnp.arange(LANES)[None, None, :]
    pick_g = (lane == jnp.arange(hd)[:, None, None]).astype(F32)
    pick_b = (lane == jnp.arange(hd)[:, None, None] + hd).astype(F32)
    blk = lambda a: _In(a, (rb, LANES), lambda i: (i // hd, i % hd))
    par = lambda a: _In(a, (None, 1, LANES), lambda i: (i % hd, 0, 0))
    ins = [blk(q), blk(k), blk(v), _In(gates, (rb, LANES), lambda i: (i // hd, 0), shared=hd), par(pick_g),
           par(pick_b)]
    wide = lambda dt: ((t, GDN_W), dt, (rb, LANES), lambda i: (i // hd, i % hd))
    carry = ((hd, 8 * (t // GDN_L), LANES), F32, (None, 8 * (rb // GDN_L), LANES), lambda i: (i % hd, i // hd, 0))
    outs = [wide(F32), wide(BF16), wide(BF16), wide(BF16), wide(BF16), carry]
    if t_saved is None:
        outs.append(wide(F32))
    else:
        ins.append(blk(t_saved))
    return (t // rb) * hd, ins, outs


def _gdn_scan_specs(t):
    wide = GDN_HPS * LANES
    once = pl.Buffered(1)
    col = lambda base: pl.BlockSpec((t, wide), lambda h: (0, base // GDN_HPS + h), pipeline_mode=once)
    carry = pl.BlockSpec((GDN_HPS, 8 * (t // GDN_L), LANES), lambda h: (h, 0, 0))
    par = pl.BlockSpec((GDN_HPS, 1, LANES), lambda h: (h, 0, 0))
    return col, carry, par


def _head_operands(refs, cy_ref, c, i):
    rows = pl.ds(pl.multiple_of(c * GDN_L, GDN_L), GDN_L)
    lanes = pl.ds(i * LANES, LANES)
    return tuple(r[rows, lanes] for r in refs) + (cy_ref[i, pl.ds(pl.multiple_of(c * 8, 8), 1), :],)


def _gdn_scan_fwd(wy, proj, nrm):
    t = wy[0].shape[0]
    nc = t // GDN_L
    heads = range(GDN_HPS)

    def body(val_ref, kcd_ref, qe_ref, kt_ref, in_ref, cy_ref, z_ref, n_ref, y_ref, o_scr):
        big = (val_ref, kcd_ref, qe_ref, kt_ref, in_ref)

        def step(c, states):
            rows = pl.ds(pl.multiple_of(c * GDN_L, GDN_L), GDN_L)
            states, outs = _gdn_steps(states, [_head_operands(big, cy_ref, c, i) for i in heads])
            for i in heads:
                o_scr[rows, pl.ds(i * LANES, LANES)] = outs[i]
            return states

        lax.fori_loop(0, nc, step, tuple(jnp.zeros((LANES, LANES), F32) for _ in heads))
        for i in heads:
            lanes = pl.ds(i * LANES, LANES)
            y_ref[:, lanes] = _gdn_post(o_scr[:, lanes], z_ref[:, lanes], n_ref[i]).astype(y_ref.dtype)

    col, carry, par = _gdn_scan_specs(t)
    return pl.pallas_call(
        body, name="gdn_scan_fwd", grid=(GDN_HEADS // GDN_HPS,),
        in_specs=[col(0)] * 5 + [carry, col(CB_Z), par],
        out_specs=col(0), out_shape=jax.ShapeDtypeStruct((t, GDN_W), BF16),
        scratch_shapes=[pltpu.VMEM((t, GDN_HPS * LANES), F32)],
        compiler_params=_cparams(("arbitrary",)),
    )(*wy, proj, nrm)


def _gdn_scan_bwd(wy, proj, nrm, dy, dy_base):
    t = wy[0].shape[0]
    nc = t // GDN_L
    heads = range(GDN_HPS)

    def body(val_ref, kcd_ref, qe_ref, kt_ref, in_ref, cy_ref, z_ref, n_ref, dy_ref,
             dval_ref, dkcd_ref, dqe_ref, dkt_ref, din_ref, dcy_ref, dz_ref, dn_ref, o_scr, s_scr):
        big = (val_ref, kcd_ref, qe_ref, kt_ref, in_ref)
        dbig = (dval_ref, dkcd_ref, dqe_ref, dkt_ref, din_ref)

        def step(c, states):
            rows = pl.ds(pl.multiple_of(c * GDN_L, GDN_L), GDN_L)
            for i in heads:
                s_scr[i, c] = states[i]
            states, outs = _gdn_steps(states, [_head_operands(big, cy_ref, c, i) for i in heads])
            for i in heads:
                o_scr[rows, pl.ds(i * LANES, LANES)] = outs[i]
            return states

        zeros = tuple(jnp.zeros((LANES, LANES), F32) for _ in heads)
        lax.fori_loop(0, nc, step, zeros)
        for i in heads:
            lanes = pl.ds(i * LANES, LANES)
            _, vjp_post = jax.vjp(_gdn_post, o_scr[:, lanes], z_ref[:, lanes], n_ref[i])
            do, dz, dn = vjp_post(dy_ref[:, lanes].astype(F32))
            dz_ref[:, lanes] = dz.astype(dz_ref.dtype)
            dn_ref[i] = dn
            o_scr[:, lanes] = do
        dcy_ref[...] = jnp.zeros_like(dcy_ref)

        def rstep(k, dstates):
            c = nc - 1 - k
            rows = pl.ds(pl.multiple_of(c * GDN_L, GDN_L), GDN_L)
            ops = [tuple(o.astype(F32) for o in _head_operands(big, cy_ref, c, i)) for i in heads]
            _, vjp_c = jax.vjp(_gdn_steps, tuple(s_scr[i, c] for i in heads), ops)
            dstates, dops = vjp_c((dstates, tuple(o_scr[rows, pl.ds(i * LANES, LANES)] for i in heads)))
            for i in heads:
                for r, g in zip(dbig, dops[i][:5]):
                    r[rows, pl.ds(i * LANES, LANES)] = g.astype(r.dtype)
                dcy_ref[i, pl.ds(pl.multiple_of(c * 8, 8), 1), :] = dops[i][5]
            return dstates

        lax.fori_loop(0, nc, rstep, zeros)

    col, carry, par = _gdn_scan_specs(t)
    wide = jax.ShapeDtypeStruct((t, GDN_W), BF16)
    return pl.pallas_call(
        body, name="gdn_scan_bwd", grid=(GDN_HEADS // GDN_HPS,),
        in_specs=[col(0)] * 5 + [carry, col(CB_Z), par, col(dy_base)],
        out_specs=[col(0)] * 5 + [carry, col(0), par],
        out_shape=[wide] * 5 + [jax.ShapeDtypeStruct(wy[5].shape, F32), wide,
                                jax.ShapeDtypeStruct((GDN_HEADS, 1, LANES), F32)],
        scratch_shapes=[pltpu.VMEM((t, GDN_HPS * LANES), F32), pltpu.VMEM((GDN_HPS, nc, LANES, LANES), F32)],
        compiler_params=_cparams(("arbitrary",)),
    )(*wy, proj, nrm, dy)


TM = 512
TMW = 1024
TR = 256


def _rms_fwd(name, h, g):
    t, d = h.shape
    tm = min(TR, t)
    return _fused(name, lambda hb, gb: (_rms(hb, gb),), t // tm, [_row_in(h, tm), _row_shared(g)],
                  [_row_out(t, d, BF16, tm)])[0]


def _rms_bwd(name, h, g, dxn, dh_next):
    t, d = h.shape
    tm = min(TR, t)
    dh, dg = _fused(name, lambda hb, gb: (_rms(hb, gb),), t // tm, [_row_in(h, tm), _row_shared(g)],
                    [_row_out(t, d, F32, tm)], douts=[dxn], need=[True, True], add={0: dh_next})
    return dh, dg


def _mixer_ins(proj, lw):
    sgu = [_col_in(proj, CB_U, SG_HEADS), _col_in(proj, CB_V, SG_HEADS), _col_par(lw['sg_ln_g']),
           _col_par(lw['sg_ln_b']), _col_par(lw['sg_w']), _col_par(lw['sg_bb']), _col_par(lw['out_norm_a'])]
    sconv = [_col_in(proj, CB_GB, SC_GROUPS), _col_in(proj, CB_GC, SC_GROUPS), _col_in(proj, CB_X, SC_GROUPS)] + \
            [_col_par(w) for w in lw['sc_taps']] + [_col_par(lw['out_norm_b'])]
    gq = [_col_in(proj, CB_Q, GDN_HEADS)] + [_col_par(w) for w in lw['q_taps']]
    gk = [_col_in(proj, CB_K, GDN_HEADS)] + [_col_par(w) for w in lw['k_taps']]
    gv = [_col_in(proj, CB_VV, GDN_HEADS)] + [_col_par(w) for w in lw['v_taps']]
    return sgu, sconv, gq, gk, gv


def _gates_ins(proj, lw, tm):
    return [_row_in(proj, tm, CB_AB), _row_shared(lw['a_log_row']), _row_shared(lw['dt_bias_row'])]


def _layer_fwd(h, p_l, lw, late=None):
    t, d = h.shape
    xn = _rms_fwd("rms_fwd", h, lw['norm_mix'])
    proj = _matmul("proj_fwd", xn, lw['w_in'], "nn", t, PROJ_W, d, TMW, 1024, d)
    sgu, sconv, gq, gk, gv = _mixer_ins(proj, lw)
    ya = _fused("sgu_fwd", _fn_sgu, SG_HEADS, sgu, [_col_out(t, SG_W, BF16)])[0]
    yb = _fused("sconv_fwd", _fn_sconv, SC_GROUPS, sconv, [_col_out(t, SC_W, BF16)])[0]
    q = _fused("gdn_q_fwd", _fn_gdn_qk, GDN_HEADS, gq, [_col_out(t, GDN_W, F32)])[0]
    k = _fused("gdn_k_fwd", _fn_gdn_qk, GDN_HEADS, gk, [_col_out(t, GDN_W, F32)])[0]
    v = _fused("gdn_v_fwd", _fn_gdn_v, GDN_HEADS, gv, [_col_out(t, GDN_W, F32)])[0]
    tm = min(TR, t)
    gates = _fused("gdn_gates_fwd", _fn_gdn_gates, t // tm, _gates_ins(proj, lw, tm),
                   [_row_out(t, LANES, F32, tm)])[0]
    n_wy, wy_ins, wy_outs = _gdn_wy_ins(q, k, v, gates)
    wy = _fused("gdn_wy_fwd", _fn_gdn_wy, n_wy, wy_ins, wy_outs)
    wy, wy_t = wy[:6], wy[6]
    yc = _gdn_scan_fwd(wy, proj, lw['gdn_norm'])
    ycat = jnp.concatenate([ya, yb, yc], axis=1)
    if late is not None:
        late(ycat)
    dff = lw['w_ff2'].shape[0]
    h2 = _matmul("wo_fwd", ycat, lw['w_o'], "nn", t, d, d, TMW, 1024, d,
                 epilogue=lambda acc, hb: (hb + acc,), extras=(h,))
    hn = _rms_fwd("rms_fwd", h2, lw['norm_ffn'])
    per = d // 1024
    s, r = _matmul("ff1_fwd", hn, lw['w_ff1'], "nn", t, dff, d, TMW, 1024, d,
                   b_spec=pl.BlockSpec((None, d, 1024), lambda i, j, kk: (j // per, 0, j % per)),
                   epilogue=lambda acc: (jnp.maximum(acc, 0.0), jnp.square(jnp.maximum(acc, 0.0))),
                   outs=[((t, dff), BF16, (min(TMW, t), 1024), lambda i, j, kk: (i, j))] * 2)
    h3 = _matmul("ff2_fwd", r, lw['w_ff2'], "nn", t, d, dff, TM, 1024, 4096,
                 epilogue=lambda acc, hb: (hb + acc,), extras=(h2,))
    hn2 = _rms_fwd("rms_fwd", h3, lw['norm_ple'])
    pp = _matmul("ple_proj_fwd", p_l, lw['w_ple_proj'], "nn", t, d, p_l.shape[1], TM, 1024, p_l.shape[1])

    def gate_epilogue(acc, hb, ppb):
        sg = _sigmoid(acc)
        return hb + ppb * sg, sg

    h4, gate = _matmul("ple_gate_fwd", hn2, lw['w_ple_gate'], "nn", t, d, d, TM, 1024, d, epilogue=gate_epilogue,
                       extras=(h3, pp), outs=[((t, d), F32, (min(TM, t), 1024), lambda i, j, kk: (i, j))] * 2)
    saved = dict(h=h, xn=xn, proj=proj, q=q, k=k, v=v, gates=gates, ycat=ycat, h2=h2, hn=hn, s=s, r=r, h3=h3,
                 hn2=hn2, pp=pp, gate=gate, p=p_l, wy=wy, wy_t=wy_t)
    return h4, saved


def _layer_bwd(dh4, sv, lw, early=None):
    t, d = dh4.shape
    dff = lw['w_ff2'].shape[0]
    tm = min(TR, t)
    g = {}
    dacc, dpp = _fused("ple_bwd_gate", lambda dh, pp, gt: (dh * pp * gt * (1.0 - gt), dh * gt), t // tm,
                       [_row_in(dh4, tm), _row_in(sv['pp'], tm), _row_in(sv['gate'], tm)],
                       [_row_out(t, d, BF16, tm)] * 2)
    g['w_ple_gate'] = _matmul("dw_ple_gate", sv['hn2'], dacc, "tn", d, d, t, TMW, 1024, t, out_dtype=BF16,
                              n_outer=True)
    g['w_ple_proj'] = _matmul("dw_ple_proj", sv['p'], dpp, "tn", sv['p'].shape[1], d, t, TM, 1024, t,
                              out_dtype=BF16, n_outer=True)
    dhn2 = _matmul("dx_ple_gate", dacc, lw['w_ple_gate'], "nt", t, d, d, TMW, 1024, d)
    dh3, g['norm_ple'] = _rms_bwd("rms_bwd", sv['h3'], lw['norm_ple'], dhn2, dh4)
    da = _matmul("dx_ff2", dh3, lw['w_ff2'], "nt", t, dff, d, TMW, 1024, d,
                 epilogue=lambda acc, sb: (acc * (2.0 * sb.astype(F32)),), extras=(sv['s'],),
                 outs=[((t, dff), BF16, (min(TMW, t), 1024), lambda i, j, kk: (i, j))])
    g['w_ff2'] = _matmul("dw_ff2", sv['r'], dh3, "tn", dff, d, t, TMW, 512, t, out_dtype=BF16, n_outer=True)
    per = d // 1024
    dhn = _matmul("dx_ff1", da, lw['w_ff1'], "nt", t, d, dff, TMW, 1024, d,
                  b_spec=pl.BlockSpec((None, 1024, d), lambda i, j, kk: (kk, j, 0)))
    g['w_ff1'] = _matmul("dw_ff1", sv['hn'], da, "tn", d, dff, t, TMW, 1024, t, n_outer=True,
                         outs=[((N_SHARD, d, d), BF16, (None, TMW, 1024), lambda i, j, kk: (j // per, i, j % per))])
    norm_ffn = lw['norm_ffn'] if early is None else lw['norm_ffn'] + early(g)[0, 0]
    dh2, g['norm_ffn'] = _rms_bwd("rms_bwd", sv['h2'], norm_ffn, dhn, dh3)
    dycat = _matmul("dx_o", dh2, lw['w_o'], "nt", t, d, d, TMW, 1024, d)
    g['w_o'] = _matmul("dw_o", sv['ycat'], dh2, "tn", d, d, t, TMW, 512, t, out_dtype=BF16, n_outer=True)
    proj = sv['proj']
    sgu, sconv, gq, gk, gv = _mixer_ins(proj, lw)
    bf2 = [BF16, BF16]
    r_ = _fused("sgu_bwd", _fn_sgu, SG_HEADS, sgu, [_col_out(t, d, F32, 0)], douts=[dycat], need=[True] * 7,
                gdt=bf2 + [F32] * 5)
    du, dv_, g['sg_ln_g'], g['sg_ln_b'], g['sg_w'], g['sg_bb'], g['out_norm_a'] = r_
    r_ = _fused("sconv_bwd", _fn_sconv, SC_GROUPS, sconv, [_col_out(t, d, F32, SG_HEADS)], douts=[dycat],
                need=[True] * 7, gdt=[BF16] * 3 + [F32] * 4)
    dgb, dgc, dxin = r_[:3]
    g['sc_taps'], g['out_norm_b'] = r_[3:6], r_[6]
    r_ = _gdn_scan_bwd(sv['wy'], proj, lw['gdn_norm'], dycat, SG_HEADS + SC_GROUPS)
    dwy, dz, g['gdn_norm'] = r_[:6], r_[6], r_[7]
    n_wy, wy_ins, wy_outs = _gdn_wy_ins(sv['q'], sv['k'], sv['v'], sv['gates'], sv['wy_t'])
    dq, dk, dvv, dgates = _fused("gdn_wy_bwd", _fn_gdn_wy, n_wy, wy_ins, wy_outs, douts=dwy,
                                 need=[True] * 4 + [False] * 3)[:4]
    one = [_col_out(t, GDN_W, F32)]
    r_ = _fused("gdn_q_bwd", _fn_gdn_qk, GDN_HEADS, gq, one, douts=[dq], need=[True] * 5, gdt=[BF16] + [F32] * 4)
    dpq, g['q_taps'] = r_[0], r_[1:]
    r_ = _fused("gdn_k_bwd", _fn_gdn_qk, GDN_HEADS, gk, one, douts=[dk], need=[True] * 5, gdt=[BF16] + [F32] * 4)
    dpk, g['k_taps'] = r_[0], r_[1:]
    r_ = _fused("gdn_v_bwd", _fn_gdn_v, GDN_HEADS, gv, one, douts=[dvv], need=[True] * 5, gdt=[BF16] + [F32] * 4)
    dpv, g['v_taps'] = r_[0], r_[1:]
    dab, g['a_log_row'], g['dt_bias_row'] = _fused(
        "gdn_gates_bwd", _fn_gdn_gates, t // tm, _gates_ins(proj, lw, tm), [_row_out(t, LANES, F32, tm)],
        douts=[dgates], need=[True] * 3, gdt=[BF16, F32, F32])
    pad = jnp.zeros((t, PROJ_W - (CB_AB + 1) * LANES), BF16)
    dproj = jnp.concatenate([du, dv_, dgb, dgc, dxin, dpq, dpk, dpv, dz, dab, pad], axis=1)
    dxn = _matmul("dx_in", dproj, lw['w_in'], "nt", t, d, PROJ_W, TMW, 1024, PROJ_W // 2)
    g['w_in'] = _matmul("dw_in", sv['xn'], dproj, "tn", d, PROJ_W, t, TMW, 1024, t, out_dtype=BF16, n_outer=True)
    dh, g['norm_mix'] = _rms_bwd("rms_bwd", sv['h'], lw['norm_mix'], dxn, dh2)
    return dh, g


def _loss_grad(h, g, tgt):
    t, d = h.shape
    tm = min(TR, t)

    def body(h_ref, g_ref, t_ref, loss_ref, dh_ref, dg_ref):
        y, vjp = jax.vjp(_rms, h_ref[...], g_ref[...])
        e = y - t_ref[...]
        dh, dg = vjp(e * (1.0 / d))

        @pl.when(pl.program_id(0) == 0)
        def _():
            loss_ref[...] = jnp.zeros_like(loss_ref)
            dg_ref[...] = jnp.zeros_like(dg_ref)

        loss_ref[...] += jnp.sum(jnp.sum(e * e, axis=1, keepdims=True), axis=0, keepdims=True) * (0.5 / d)
        dh_ref[...] = dh
        dg_ref[...] += dg

    row = pl.BlockSpec((tm, d), lambda i: (i, 0))
    return pl.pallas_call(
        body, name="loss_grad", grid=(t // tm,),
        in_specs=[row, pl.BlockSpec((1, d), lambda i: (0, 0)), row],
        out_specs=[pl.BlockSpec((1, LANES), lambda i: (0, 0)), row, pl.BlockSpec((1, d), lambda i: (0, 0))],
        out_shape=[jax.ShapeDtypeStruct((1, LANES), F32), jax.ShapeDtypeStruct((t, d), F32),
                   jax.ShapeDtypeStruct((1, d), F32)],
        compiler_params=_cparams(("arbitrary",)),
    )(h, g, tgt)


def _big_weights(full):
    lw = {}
    if 'w_in' in full:
        d = full['w_in'].shape[1]
        w_in = jnp.transpose(full['w_in'], (1, 0, 2)).reshape(d, IN_COLS)
        lw['w_in'] = jnp.pad(w_in, ((0, 0), (0, PROJ_W - IN_COLS)))
    if 'w_o' in full:
        lw['w_o'] = full['w_o'].reshape(-1, full['w_o'].shape[-1])
    if 'w_ff1' in full:
        lw['w_ff1'] = full['w_ff1']
    if 'w_ff2' in full:
        lw['w_ff2'] = full['w_ff2'].reshape(-1, full['w_ff2'].shape[-1])
    if 'w_ple_gate' in full:
        lw['w_ple_gate'] = full['w_ple_gate'].reshape(-1, full['w_ple_gate'].shape[-1])
    if 'w_ple_proj' in full:
        wpp = full['w_ple_proj']
        lw['w_ple_proj'] = jnp.transpose(wpp, (1, 0, 2)).reshape(wpp.shape[1], -1)
    return lw


def _layer_weights(l, full, convs, small):
    d = small['norm_mix'].shape[-1]
    lw = _big_weights(full)
    for n in ('norm_mix', 'norm_ffn', 'norm_ple'):
        lw[n] = small[n][l].reshape(1, d)
    lw['sg_ln_g'] = small['sg_ln_g'][l].reshape(SG_HEADS, 1, LANES)
    lw['sg_ln_b'] = small['sg_ln_b'][l].reshape(SG_HEADS, 1, LANES)
    lw['sg_w'] = small['sg_w'][l]
    lw['sg_bb'] = jnp.broadcast_to(small['sg_b'][l][:, :, None], (SG_HEADS, SG_CHUNK, LANES))
    lw['out_norm_a'] = small['out_norm_a'][l].reshape(SG_HEADS, 1, LANES)
    lw['out_norm_b'] = small['out_norm_b'][l].reshape(SC_GROUPS, 1, LANES)
    lw['gdn_norm'] = jnp.broadcast_to(small['gdn_norm'][l].reshape(1, 1, LANES), (GDN_HEADS, 1, LANES))
    lw['a_log_row'] = jnp.pad(small['gdn_a_log'][l].reshape(1, GDN_HEADS), ((0, 0), (0, LANES - GDN_HEADS)))
    lw['dt_bias_row'] = jnp.pad(small['gdn_dt_bias'][l].reshape(1, GDN_HEADS), ((0, 0), (0, LANES - GDN_HEADS)))
    sc = convs['sc_conv'][l]
    lw['sc_taps'] = [sc[:, j:j + 1, :] for j in range(SC_KERNEL)]
    gc = jnp.transpose(convs['gdn_conv'][l], (1, 0, 2)).reshape(GDN_CONV, 3 * GDN_W)
    for i, nm in enumerate(('q_taps', 'k_taps', 'v_taps')):
        part = gc[:, i * GDN_W:(i + 1) * GDN_W].reshape(GDN_CONV, GDN_HEADS, 1, LANES)
        lw[nm] = [part[j] for j in range(GDN_CONV)]
    return lw


def _shard_major(name, g):
    if name == 'w_in':
        d = g.shape[0]
        return jnp.transpose(g[:, :IN_COLS].reshape(d, N_SHARD, IN_COLS // N_SHARD), (1, 0, 2))
    if name == 'w_ple_proj':
        return jnp.transpose(g.reshape(g.shape[0], N_SHARD, g.shape[1] // N_SHARD), (1, 0, 2))
    if name == 'w_ff1':
        return g
    return g.reshape(N_SHARD, -1, g.shape[-1])


def _weight_grads(g):
    d = g['norm_mix'].shape[-1]
    out = {}
    for n in ('norm_mix', 'norm_ffn', 'norm_ple'):
        out[n] = g[n].reshape(d)
    out['sg_ln_g'] = g['sg_ln_g'].reshape(SG_W)
    out['sg_ln_b'] = g['sg_ln_b'].reshape(SG_W)
    out['sg_w'] = g['sg_w']
    out['sg_b'] = jnp.sum(g['sg_bb'], axis=2)
    out['out_norm_a'] = g['out_norm_a'].reshape(SG_W)
    out['out_norm_b'] = g['out_norm_b'].reshape(SC_W)
    out['gdn_norm'] = jnp.sum(g['gdn_norm'], axis=(0, 1))
    out['gdn_a_log'] = g['a_log_row'][0, :GDN_HEADS]
    out['gdn_dt_bias'] = g['dt_bias_row'][0, :GDN_HEADS]
    out['sc_conv'] = jnp.concatenate([w.reshape(1, SC_W) for w in g['sc_taps']], axis=0)
    taps = [jnp.concatenate([g[nm][j].reshape(1, GDN_W) for nm in ('q_taps', 'k_taps', 'v_taps')], axis=1)
            for j in range(GDN_CONV)]
    out['gdn_conv'] = jnp.concatenate(taps, axis=0)
    return out


ANY = pl.BlockSpec(memory_space=pl.ANY)


def _place():
    x, y, c = lax.axis_index("x"), lax.axis_index("y"), lax.axis_index("c")
    chips = [(1 - x, y), (x, 1 - y), (1 - x, 1 - y)]
    return x, y, c, chips


def _place_shard(wsh, shard, dtype, layer=None, after=None):
    dp, r, cc = wsh.shape
    tr = min(TR, r)
    nb = r // tr

    def body(idx_ref, w_ref, *rest):
        rest[-1][...] = w_ref[...].astype(rest[-1].dtype)

    if layer is None:
        grid, shape = (dp * nb,), (dp, N_SHARD, r, cc)
        in_spec = pl.BlockSpec((None, tr, cc), lambda i, ix: (i // nb, i % nb, 0))
        out_spec = pl.BlockSpec((None, None, tr, cc), lambda i, ix: (i // nb, ix[0], i % nb, 0))
    else:
        grid, shape = (nb,), (N_SHARD, r, cc)
        in_spec = pl.BlockSpec((None, tr, cc), lambda i, ix: (layer, i, 0))
        out_spec = pl.BlockSpec((None, tr, cc), lambda i, ix: (ix[0], i, 0))
    return pl.pallas_call(
        body, name="ag_place_shard",
        grid_spec=pltpu.PrefetchScalarGridSpec(num_scalar_prefetch=1, grid=grid,
                                               in_specs=[in_spec] + ([] if after is None else [ANY]),
                                               out_specs=out_spec),
        out_shape=jax.ShapeDtypeStruct(shape, dtype),
        compiler_params=_cparams(("arbitrary",)),
    )(shard, wsh, *([] if after is None else [after]))


def _rows(ref, slab, half):
    rh = ref.shape[1] // 2
    return ref.at[slab, pl.ds(pl.multiple_of(half * rh, rh), rh), :]


def _gather_direct(bufs):
    n = len(bufs)

    def body(*refs):
        outs = refs[n:2 * n]
        send_sems, recv_sems = refs[2 * n:]
        x, y, c, chips = _place()
        me = 2 * x + y

        def cp(w, l, j, shard, to):
            blk = outs[w].at[l, shard]
            return pltpu.make_async_remote_copy(src_ref=blk, dst_ref=blk, send_sem=send_sems.at[w, l, j],
                                                recv_sem=recv_sems.at[w, l, j], device_id=to, device_id_type=MESH)

        sends = [cp(w, l, j, me, (*chip, c)) for w in range(n) for l in range(DEPTH) for j, chip in enumerate(chips)]
        for s in sends:
            s.start()
        for w in range(n):
            for l in range(DEPTH):
                for j, (px, py) in enumerate(chips):
                    cp(w, l, j, 2 * px + py, (px, py, c)).wait_recv()
        for s in sends:
            s.wait_send()

    return pl.pallas_call(
        body, name="ag_small", in_specs=[ANY] * n, out_specs=[ANY] * n,
        out_shape=[jax.ShapeDtypeStruct(b.shape, b.dtype) for b in bufs],
        input_output_aliases={w: w for w in range(n)},
        scratch_shapes=[pltpu.SemaphoreType.DMA((n, DEPTH, 3)), pltpu.SemaphoreType.DMA((n, DEPTH, 3))],
    )(*bufs)


def _gather_halves(bufs):
    n = len(bufs)

    def body(*refs):
        outs = refs[n:2 * n]
        send_sems, recv_sems = refs[2 * n:]
        x, y, c, chips = _place()
        me = 2 * x + y
        sibling = (x, y, 1 - c)

        def cp(w, k, shard, half, to):
            blk = _rows(outs[w], shard, half)
            return pltpu.make_async_remote_copy(src_ref=blk, dst_ref=blk, send_sem=send_sems.at[w, k],
                                                recv_sem=recv_sems.at[w, k], device_id=to, device_id_type=MESH)

        sends = []
        for w in range(n):
            for j, chip in enumerate(chips):
                s = cp(w, j, me, c, (*chip, c))
                s.start()
                sends.append(s)
        for w in range(n):
            for j, (px, py) in enumerate(chips):
                cp(w, j, 2 * px + py, c, sibling).wait_recv()
                s = cp(w, 3 + j, 2 * px + py, c, sibling)
                s.start()
                sends.append(s)
        for w in range(n):
            for j, (px, py) in enumerate(chips):
                cp(w, 3 + j, 2 * px + py, 1 - c, sibling).wait_recv()
        for s in sends:
            s.wait_send()

    return pl.pallas_call(
        body, name="ag_gather_halves",
        in_specs=[ANY] * n, out_specs=[ANY] * n,
        out_shape=[jax.ShapeDtypeStruct(b.shape, b.dtype) for b in bufs],
        input_output_aliases={w: w for w in range(n)},
        scratch_shapes=[pltpu.SemaphoreType.DMA((n, 6)), pltpu.SemaphoreType.DMA((n, 6))],
    )(*bufs)


HBM = pl.BlockSpec(memory_space=pltpu.HBM)
SEM = pl.BlockSpec(memory_space=pltpu.SEMAPHORE)
_SPLIT = pltpu.CompilerParams(has_side_effects=pltpu.SideEffectType.DATAFLOW_SIDE_EFFECTING)


def _in_hbm(arrs):
    return [pltpu.with_memory_space_constraint(a, pltpu.HBM) for a in arrs]


def _gather_start(tag, bufs, after):
    n, na = len(bufs), len(after)
    k = 3 * n

    def body(*refs):
        b_refs, sems, token = refs[:n], refs[n + na:n + na + 2 * k], refs[-1]
        x, y, c, chips = _place()
        me = 2 * x + y
        for w in range(n):
            for j, chip in enumerate(chips):
                blk = _rows(b_refs[w], me, c)
                pltpu.make_async_remote_copy(src_ref=blk, dst_ref=blk, send_sem=sems[3 * w + j],
                                             recv_sem=sems[k + 3 * w + j], device_id=(*chip, c),
                                             device_id_type=MESH).start()
        token[...] = jnp.zeros_like(token)

    res = pl.pallas_call(
        body, name="ag_start_" + tag,
        out_shape=(*[pltpu.SemaphoreType.DMA(())] * (2 * k), *[pltpu.HBM(b.shape, b.dtype) for b in bufs],
                   jax.ShapeDtypeStruct((8, LANES), F32)),
        in_specs=[HBM] * n + [ANY] * na,
        out_specs=(*[SEM] * (2 * k), *[HBM] * n, pl.BlockSpec(memory_space=pltpu.VMEM)),
        input_output_aliases={w: 2 * k + w for w in range(n)}, compiler_params=_SPLIT,
    )(*_in_hbm(bufs), *after)
    return list(res[:2 * k]), list(res[2 * k:2 * k + n]), res[2 * k + n]


def _gather_wait(tag, sems, bufs, after):
    n = len(bufs)
    k = 3 * n

    def body(*refs):
        b_refs, sems = refs[:n], refs[n:n + 2 * k]
        x, y, c, chips = _place()
        me = 2 * x + y
        for w in range(n):
            for j, (px, py) in enumerate(chips):
                mine, theirs = _rows(b_refs[w], me, c), _rows(b_refs[w], 2 * px + py, c)
                cp = pltpu.make_async_remote_copy(src_ref=mine, dst_ref=theirs, send_sem=sems[3 * w + j],
                                                  recv_sem=sems[k + 3 * w + j], device_id=(px, py, c),
                                                  device_id_type=MESH)
                cp.wait_send()
                cp.wait_recv()

    res = pl.pallas_call(
        body, name="ag_wait_" + tag, out_shape=tuple(pltpu.HBM(b.shape, b.dtype) for b in bufs),
        in_specs=[HBM] * n + [SEM] * (2 * k) + [ANY], out_specs=(HBM,) * n,
        input_output_aliases={w: w for w in range(n)}, compiler_params=_SPLIT,
    )(*bufs, *sems, after)
    return list(res)


def _gather_forward(bufs):
    n = len(bufs)

    def body(*refs):
        outs = refs[n:2 * n]
        send_sems, recv_sems = refs[2 * n:]
        x, y, c, chips = _place()

        def cp(w, j, shard, half):
            blk = _rows(outs[w], shard, half)
            return pltpu.make_async_remote_copy(src_ref=blk, dst_ref=blk, send_sem=send_sems.at[w, j],
                                                recv_sem=recv_sems.at[w, j], device_id=(x, y, 1 - c),
                                                device_id_type=MESH)

        sends = [cp(w, j, 2 * px + py, c) for w in range(n) for j, (px, py) in enumerate(chips)]
        for s in sends:
            s.start()
        for w in range(n):
            for j, (px, py) in enumerate(chips):
                cp(w, j, 2 * px + py, 1 - c).wait_recv()
        for s in sends:
            s.wait_send()

    return pl.pallas_call(
        body, name="ag_forward", in_specs=[ANY] * n, out_specs=[ANY] * n,
        out_shape=[jax.ShapeDtypeStruct(b.shape, b.dtype) for b in bufs],
        input_output_aliases={w: w for w in range(n)},
        scratch_shapes=[pltpu.SemaphoreType.DMA((n, 3)), pltpu.SemaphoreType.DMA((n, 3))],
    )(*bufs)


def _scatter_start(tag, parts, after):
    n, na = len(parts), len(after)
    k = 3 * n
    lands = [lax.empty((3,) + g.shape[1:], g.dtype) for g in parts]

    def body(*refs):
        srcs, dsts, sems, token = refs[:n], refs[n:2 * n], refs[2 * n + na:2 * n + na + 2 * k], refs[-1]
        x, y, c, chips = _place()
        for w in range(n):
            for j, (px, py) in enumerate(chips):
                pltpu.make_async_remote_copy(src_ref=srcs[w].at[2 * px + py], dst_ref=dsts[w].at[j],
                                             send_sem=sems[3 * w + j], recv_sem=sems[k + 3 * w + j],
                                             device_id=(px, py, c), device_id_type=MESH).start()
        token[...] = jnp.zeros_like(token)

    res = pl.pallas_call(
        body, name="rs_scatter_start_" + tag,
        out_shape=(*[pltpu.SemaphoreType.DMA(())] * (2 * k), *[pltpu.HBM(a.shape, a.dtype) for a in parts + lands],
                   jax.ShapeDtypeStruct((8, LANES), F32)),
        in_specs=[HBM] * (2 * n) + [ANY] * na,
        out_specs=(*[SEM] * (2 * k), *[HBM] * (2 * n), pl.BlockSpec(memory_space=pltpu.VMEM)),
        input_output_aliases={w: 2 * k + w for w in range(2 * n)}, compiler_params=_SPLIT,
    )(*_in_hbm(parts + lands), *after)
    return list(res[:2 * k]), list(res[2 * k:2 * k + n]), list(res[2 * k + n:2 * k + 2 * n]), res[2 * k + 2 * n]


def _scatter_wait(tag, sems, parts, lands, after):
    n = len(parts)
    k = 3 * n

    def body(*refs):
        srcs, dsts, sems = refs[:n], refs[n:2 * n], refs[2 * n:2 * n + 2 * k]
        x, y, c, chips = _place()
        for w in range(n):
            for j, (px, py) in enumerate(chips):
                cp = pltpu.make_async_remote_copy(src_ref=srcs[w].at[2 * px + py], dst_ref=dsts[w].at[j],
                                                  send_sem=sems[3 * w + j], recv_sem=sems[k + 3 * w + j],
                                                  device_id=(px, py, c), device_id_type=MESH)
                cp.wait_send()
                cp.wait_recv()

    res = pl.pallas_call(
        body, name="rs_scatter_wait_" + tag, out_shape=tuple(pltpu.HBM(a.shape, a.dtype) for a in parts + lands),
        in_specs=[HBM] * (2 * n) + [SEM] * (2 * k) + [ANY], out_specs=(HBM,) * (2 * n),
        input_output_aliases={w: w for w in range(2 * n)}, compiler_params=_SPLIT,
    )(*parts, *lands, *sems, after)
    return list(res[:n]), list(res[n:])


def _swap_halves(grads):
    n = len(grads)

    def body(*refs):
        srcs, outs = refs[:n], refs[n:2 * n]
        send_sems, recv_sems = refs[2 * n:]
        x, y, c, _ = _place()
        cps = []
        for w in range(n):
            rh = srcs[w].shape[1] // 2
            theirs = srcs[w].at[:, pl.ds(pl.multiple_of((1 - c) * rh, rh), rh), :]
            cps.append(pltpu.make_async_remote_copy(src_ref=theirs, dst_ref=outs[w], send_sem=send_sems.at[w],
                                                    recv_sem=recv_sems.at[w], device_id=(x, y, 1 - c),
                                                    device_id_type=MESH))
        for cpy in cps:
            cpy.start()
        for cpy in cps:
            cpy.wait()

    return pl.pallas_call(
        body, name="rs_swap_halves", in_specs=[ANY] * n, out_specs=[ANY] * n,
        out_shape=[jax.ShapeDtypeStruct((g.shape[0], g.shape[1] // 2, g.shape[2]), g.dtype) for g in grads],
        scratch_shapes=[pltpu.SemaphoreType.DMA((n,)), pltpu.SemaphoreType.DMA((n,))],
    )(*grads)


def _scatter_shards(parts):
    n = len(parts)

    def body(*refs):
        srcs, outs = refs[:n], refs[n:2 * n]
        send_sems, recv_sems = refs[2 * n:]
        x, y, c, chips = _place()

        def cp(w, j, src_shard, to):
            return pltpu.make_async_remote_copy(
                src_ref=srcs[w].at[src_shard], dst_ref=outs[w].at[j], send_sem=send_sems.at[w, j],
                recv_sem=recv_sems.at[w, j], device_id=to, device_id_type=MESH)

        sends = [cp(w, j, 2 * px + py, (px, py, c)) for w in range(n) for j, (px, py) in enumerate(chips)]
        for s in sends:
            s.start()
        for s in sends:
            s.wait()

    return pl.pallas_call(
        body, name="rs_scatter", in_specs=[ANY] * n, out_specs=[ANY] * n,
        out_shape=[jax.ShapeDtypeStruct((3,) + g.shape[1:], g.dtype) for g in parts],
        scratch_shapes=[pltpu.SemaphoreType.DMA((n, 3)), pltpu.SemaphoreType.DMA((n, 3))],
    )(*parts)


def _sum_reduced(part, slots, shard, core, layer, prev):
    _, rh, cc = part.shape
    tr = min(TR, rh)
    nbh = rh // tr

    def body(shard_ref, core_ref, p_ref, s0_ref, s1_ref, s2_ref, *rest):
        acc = p_ref[...].astype(F32)
        for s_ref in (s0_ref, s1_ref, s2_ref):
            acc = acc + s_ref[...].astype(F32)
        rest[-1][...] = acc

    slot = lambda j: pl.BlockSpec((None, tr, cc), lambda i, sh, co: (j, i, 0))
    ins = [shard, core, part, slots, slots, slots] + ([] if prev is None else [prev])
    return pl.pallas_call(
        body, name="rs_sum_reduced",
        grid_spec=pltpu.PrefetchScalarGridSpec(
            num_scalar_prefetch=2, grid=(nbh,),
            in_specs=[pl.BlockSpec((None, tr, cc), lambda i, sh, co: (sh[0], i, 0)), slot(0), slot(1), slot(2)]
            + ([] if prev is None else [ANY]),
            out_specs=pl.BlockSpec((None, tr, cc), lambda i, sh, co: (layer, co[0] * nbh + i, 0))),
        out_shape=jax.ShapeDtypeStruct((DEPTH, 2 * rh, cc), F32),
        input_output_aliases={} if prev is None else {6: 0},
        compiler_params=_cparams(("arbitrary",)),
    )(*ins)


def _join_halves(bufs, layer):
    n = len(bufs)

    def body(*refs):
        outs = refs[n:2 * n]
        send_sems, recv_sems = refs[2 * n:]
        x, y, c, _ = _place()

        def cp(w, half):
            blk = _rows(outs[w], layer, half)
            return pltpu.make_async_remote_copy(src_ref=blk, dst_ref=blk, send_sem=send_sems.at[w],
                                                recv_sem=recv_sems.at[w], device_id=(x, y, 1 - c), device_id_type=MESH)

        sends = [cp(w, c) for w in range(n)]
        for s in sends:
            s.start()
        for w in range(n):
            cp(w, 1 - c).wait_recv()
        for s in sends:
            s.wait_send()

    return pl.pallas_call(
        body, name="rs_join_halves", in_specs=[ANY] * n, out_specs=[ANY] * n,
        out_shape=[jax.ShapeDtypeStruct(b.shape, b.dtype) for b in bufs],
        input_output_aliases={w: w for w in range(n)},
        scratch_shapes=[pltpu.SemaphoreType.DMA((n,)), pltpu.SemaphoreType.DMA((n,))],
    )(*bufs)


def _exchange_small(vec):
    def body(src, out, send_sems, recv_sems, local_sem):
        x, y, c, _ = _place()
        me = 4 * x + 2 * y + c
        lc = pltpu.make_async_copy(src, out.at[me], local_sem)
        lc.start()
        sends = []
        for k in range(1, N_DEV):
            fx, fy, fc = (k >> 2) & 1, (k >> 1) & 1, k & 1
            to = (x ^ fx, y ^ fy, c ^ fc)
            s = pltpu.make_async_remote_copy(src_ref=src, dst_ref=out.at[me], send_sem=send_sems.at[k - 1],
                                             recv_sem=recv_sems.at[k - 1], device_id=to, device_id_type=MESH)
            s.start()
            sends.append(s)
        for k in range(1, N_DEV):
            fx, fy, fc = (k >> 2) & 1, (k >> 1) & 1, k & 1
            frm = 4 * (x ^ fx) + 2 * (y ^ fy) + (c ^ fc)
            pltpu.make_async_remote_copy(src_ref=src, dst_ref=out.at[frm], send_sem=send_sems.at[k - 1],
                                         recv_sem=recv_sems.at[k - 1], device_id=(x ^ fx, y ^ fy, c ^ fc),
                                         device_id_type=MESH).wait_recv()
        for s in sends:
            s.wait_send()
        lc.wait()

    return pl.pallas_call(
        body, name="small_exchange", in_specs=[ANY], out_specs=ANY,
        out_shape=jax.ShapeDtypeStruct((N_DEV,) + vec.shape, vec.dtype),
        scratch_shapes=[pltpu.SemaphoreType.DMA((N_DEV - 1,)), pltpu.SemaphoreType.DMA((N_DEV - 1,)),
                        pltpu.SemaphoreType.DMA],
    )(vec)


def _add_own_half(g, other, c_arr):
    ns, rh, cc = other.shape
    tr = min(TR, rh)
    nb = rh // tr

    def body(c_ref, g_ref, o_ref, out_ref):
        out_ref[...] = (g_ref[...].astype(F32) + o_ref[...].astype(F32)).astype(out_ref.dtype)

    slab = pl.BlockSpec((None, tr, cc), lambda i, cr: (i // nb, i % nb, 0))
    return pl.pallas_call(
        body, name="rs_add_own_half",
        grid_spec=pltpu.PrefetchScalarGridSpec(
            num_scalar_prefetch=1, grid=(ns * nb,),
            in_specs=[pl.BlockSpec((None, tr, cc), lambda i, cr: (i // nb, cr[0] * nb + i % nb, 0)), slab],
            out_specs=slab),
        out_shape=jax.ShapeDtypeStruct(other.shape, other.dtype),
        compiler_params=_cparams(("arbitrary",)),
    )(c_arr, g, other)


def _sum_slots(name, a):
    ns, r, cc = a.shape
    tr = min(TR, r)
    ins = [_In(a, (None, tr, cc), functools.partial(lambda s, i: (s, i, 0), s)) for s in range(ns)]

    def fn(*blocks):
        acc = blocks[0].astype(F32)
        for b in blocks[1:]:
            acc = acc + b.astype(F32)
        return (acc,)

    return _fused(name, fn, r // tr, ins, [((r, cc), F32, (tr, cc), lambda i: (i, 0))])[0]


def _adamw_fn(w, g, m, v):
    m = ADAM_B1 * m + (1.0 - ADAM_B1) * g
    v = ADAM_B2 * v + (1.0 - ADAM_B2) * jnp.square(g)
    m_hat = m / (1.0 - ADAM_B1 ** ADAM_STEP)
    v_hat = v / (1.0 - ADAM_B2 ** ADAM_STEP)
    delta = -ADAM_LR * (m_hat / (jnp.sqrt(v_hat) + ADAM_EPS) + ADAM_WD * w)
    return delta, m, v


def _adamw(w, g, m, v):
    r, cc = w.shape
    tr = min(TR, r)
    ins = [_In(a, (tr, cc), lambda i: (i, 0)) for a in (w, g, m, v)]
    return _fused("adamw", _adamw_fn, r // tr, ins, [((r, cc), F32, (tr, cc), lambda i: (i, 0))] * 3)


def _pack(arrs):
    flat = jnp.concatenate([a.reshape(-1) for a in arrs])
    tile = TR * LANES
    n = -(-flat.shape[0] // tile) * tile
    return jnp.pad(flat, (0, n - flat.shape[0])).reshape(-1, LANES)


def _unpack(vec, shapes):
    flat = vec.reshape(-1)
    out, o = [], 0
    for s in shapes:
        n = math.prod(s)
        out.append(flat[o:o + n].reshape(s))
        o += n
    return out


def kernel(x, p, norm_mix, w_in, sg_ln_g, sg_ln_b, sg_w, sg_b, sc_conv, gdn_conv, gdn_a_log, gdn_dt_bias, gdn_norm, out_norm_a, out_norm_b, w_o, norm_ffn, w_ff1, w_ff2, norm_ple, w_ple_gate, w_ple_proj, norm_final, loss_target, m_norm_mix, m_w_in, m_sg_ln_g, m_sg_ln_b, m_sg_w, m_sg_b, m_sc_conv, m_gdn_conv, m_gdn_a_log, m_gdn_dt_bias, m_gdn_norm, m_out_norm_a, m_out_norm_b, m_w_o, m_norm_ffn, m_w_ff1, m_w_ff2, m_norm_ple, m_w_ple_gate, m_w_ple_proj, m_norm_final, v_norm_mix, v_w_in, v_sg_ln_g, v_sg_ln_b, v_sg_w, v_sg_b, v_sc_conv, v_gdn_conv, v_gdn_a_log, v_gdn_dt_bias, v_gdn_norm, v_out_norm_a, v_out_norm_b, v_w_o, v_norm_ffn, v_w_ff1, v_w_ff2, v_norm_ple, v_w_ple_gate, v_w_ple_proj, v_norm_final):
    given = dict(locals())
    w = {n: given[n] for n in WEIGHTS}
    m = {n: given['m_' + n] for n in WEIGHTS}
    v = {n: given['v_' + n] for n in WEIGHTS}
    shard = 2 * lax.axis_index("x") + lax.axis_index("y")
    core = lax.axis_index("c")

    shard_arr = shard.reshape(1).astype(jnp.int32)
    c_arr = core.reshape(1).astype(jnp.int32)
    convs = dict(zip(CONVS, _gather_direct([_place_shard(w[n], shard_arr, F32) for n in CONVS])))
    small = {n: w[n] for n in SMALL if n not in CONVS}
    late_w = [n for n in BIG if n != 'w_in']
    flyw = _gather_start("w", [_place_shard(w['w_in'], shard_arr, BF16, 0)], list(convs.values()))
    placed0 = [_place_shard(w[n], shard_arr, BF16, 0, flyw[2]) for n in late_w]
    placed1 = [_place_shard(w[n], shard_arr, BF16, 1, flyw[2]) for n in BIG]
    fly0 = _gather_start("l0", placed0, [flyw[2]])
    fly1 = _gather_start("l1", placed1, [fly0[2]])
    first = _gather_forward(_gather_wait("w", flyw[0], flyw[1], fly1[2]))

    lws = [_layer_weights(0, {'w_in': first[0]}, convs, small), None]
    lws[0]['norm_mix'] = lws[0]['norm_mix'] + fly1[2][0, 0]

    def rest_of_layer0(behind):
        got = _gather_forward(_gather_wait("l0", fly0[0], fly0[1], behind))
        lws[0].update(_big_weights(dict(zip(late_w, got))))

    h, sv0 = _layer_fwd(x[0], p[0, 0], lws[0], rest_of_layer0)
    got = _gather_forward(_gather_wait("l1", fly1[0], fly1[1], h))
    lws[1] = _layer_weights(1, dict(zip(BIG, got)), convs, small)
    h, sv1 = _layer_fwd(h, p[1, 0], lws[1])
    saved = [sv0, sv1]
    loss, dh, dnf = _loss_grad(h, small['norm_final'].reshape(1, -1), loss_target[0])

    def swap_add(names, g):
        big = [_shard_major(n, g[n]) for n in names]
        return [_add_own_half(g_, o, c_arr) for g_, o in zip(big, _swap_halves(big))]

    early_g = ['w_ff1', 'w_ff2', 'w_ple_gate', 'w_ple_proj']
    late_g = [n for n in BIG if n not in early_g]
    per_layer = [None] * DEPTH
    dh, g = _layer_bwd(dh, saved[1], lws[1])
    per_layer[1] = _weight_grads(g)
    fly1 = _scatter_start("l1", swap_add(BIG, g), [dh])
    lws[0]['norm_ple'] = lws[0]['norm_ple'] + fly1[3][0, 0]
    fly0 = []

    def early_grads(g0):
        fly0.extend(_scatter_start("l0", swap_add(early_g, g0), []))
        return fly0[3]

    dh, g = _layer_bwd(dh, saved[0], lws[0], early_grads)
    per_layer[0] = _weight_grads(g)
    flyl = _scatter_start("l0b", swap_add(late_g, g), [])
    groups = [(1, BIG, *_scatter_wait("l1", fly1[0], fly1[1], fly1[2], flyl[3])),
              (0, early_g, *_scatter_wait("l0", fly0[0], fly0[1], fly0[2], flyl[3]))]
    g_big = {}

    def finish(groups):
        for l, names, parts, slots in groups:
            sums = [_sum_reduced(pt, sl, shard_arr, c_arr, l, g_big.get(n)) for n, pt, sl in zip(names, parts, slots)]
            g_big.update(zip(names, _join_halves(sums, l)))

    delta, new_m, new_v, grad_w = {}, {}, {}, {}

    def update(names):
        for n in names:
            shp = w[n].shape
            two_d = lambda a: a.reshape(-1, shp[-1])
            d_, m_, v_ = _adamw(two_d(w[n]), two_d(g_big[n]), two_d(m[n]), two_d(v[n]))
            delta[n], new_m[n], new_v[n], grad_w[n] = d_.reshape(shp), m_.reshape(shp), v_.reshape(shp), g_big[n]

    finish(groups)
    update(early_g)
    finish([(0, late_g, *_scatter_wait("l0b", flyl[0], flyl[1], flyl[2], delta[early_g[0]]))])
    update(late_g)
    grad_x = dh
    grads = {n: jnp.stack([per_layer[l][n] for l in range(DEPTH)], axis=0) for n in SMALL if n != 'norm_final'}
    grads['norm_final'] = dnf.reshape(-1)

    rep = [n for n in SMALL if n not in CONVS]
    names = rep + CONVS
    vec = _pack([grads[n] for n in names] + [loss[0, :1]])
    total = _sum_slots("small_sum", _exchange_small(vec))
    parts_small = _unpack(total, [grads[n].shape for n in names] + [(1,)])
    g_small = dict(zip(names, parts_small[:-1]))
    loss_out = parts_small[-1].reshape(())
    for n in CONVS:
        width = w[n].shape[-1]
        g_small[n] = lax.dynamic_slice_in_dim(g_small[n], shard * width, width, axis=2)

    shapes = [w[n].shape for n in SMALL]
    d_, m_, v_ = _adamw(_pack([w[n] for n in SMALL]), _pack([g_small[n] for n in SMALL]),
                        _pack([m[n] for n in SMALL]), _pack([v[n] for n in SMALL]))
    for n, dd, mm, vv in zip(SMALL, _unpack(d_, shapes), _unpack(m_, shapes), _unpack(v_, shapes)):
        delta[n], new_m[n], new_v[n], grad_w[n] = dd, mm, vv, g_small[n]

    return (loss_out, grad_x[None], *[grad_w[n] for n in WEIGHTS], *[delta[n] for n in WEIGHTS],
            *[new_m[n] for n in WEIGHTS], *[new_v[n] for n in WEIGHTS])
```
